```python
import jax, jax.numpy as jnp
from jax import lax
import numpy as np

D_MODEL = 1024
BATCH = 4
SEQ = 8192
DEPTH = 4

GRID_W = 64
CTX_LEN = 256
HEAD_DIM = 64
D_MIX = D_MODEL
H_MLSTM = D_MIX // (4 * HEAD_DIM)
H_NA = D_MIX // (2 * HEAD_DIM)
H_RET = D_MIX // (4 * HEAD_DIM)
D_M = H_MLSTM * HEAD_DIM
D_NA = H_NA * HEAD_DIM
D_R = H_RET * HEAD_DIM
MLSTM_CONV = 3
MLSTM_CHUNK = 64
RET_CHUNK = 64
NA_ROWS = 8
NA_COLS = 16
ROPE_BASE = 10000.0
N_EXPERTS = 32
TOP_K = 4
D_EXPERT = D_MODEL
SWIGLU_ALPHA = 1.702
SWIGLU_LIMIT = 7.0
MOE_BLOCK = 128
NORM_EPS = 1e-6
NEG_INF = -1e30
SPLIT_WIDTHS = (2 * D_M, D_M, D_M, 4 * H_MLSTM, D_NA, D_NA, D_NA, D_R, D_R, D_R, D_R)
SPLIT_POINTS = tuple(int(s) for s in np.cumsum(SPLIT_WIDTHS)[:-1])
D_IN = int(sum(SPLIT_WIDTHS))

kernel_name = "hybrid_mlstm_natten_retention_moe_dit"


def rms_norm(x, g):
    xf = x.astype(jnp.float32)
    y = xf * lax.rsqrt(jnp.mean(xf * xf, axis=-1, keepdims=True) + NORM_EPS)
    return (y * g.astype(jnp.float32)).astype(x.dtype)


def split_heads(t, n_heads):
    return t.reshape(t.shape[0], t.shape[1], n_heads, HEAD_DIM)


def to_bhtd(t):
    return jnp.transpose(t, (0, 2, 1, 3))


def merge_heads(t):
    b, h, t_len, d = t.shape
    return jnp.transpose(t, (0, 2, 1, 3)).reshape(b, t_len, h * d)


def flip_t(t):
    return jnp.flip(t, axis=2)


def dwconv_centred(x, w, b):
    k_size = w.shape[0]
    pad = k_size // 2
    t_len = x.shape[1]
    xp = jnp.pad(x, ((0, 0), (pad, pad), (0, 0)))
    y = b
    for i in range(k_size):
        y = y + xp[:, i:i + t_len] * w[i]
    return y


def axial_rope_tables(t_len):
    pos = jnp.arange(t_len)
    row = (pos // GRID_W).astype(jnp.float32)
    col = (pos % GRID_W).astype(jnp.float32)
    n = HEAD_DIM // 4
    inv = ROPE_BASE ** (-jnp.arange(n, dtype=jnp.float32) / n)
    ar = row[:, None] * inv
    ac = col[:, None] * inv
    return (jnp.cos(ar), jnp.sin(ar), jnp.cos(ac), jnp.sin(ac))


def rotate_half(x, cos, sin):
    x1, x2 = jnp.split(x, 2, axis=-1)
    cos = cos.astype(x.dtype)
    sin = sin.astype(x.dtype)
    return jnp.concatenate([x1 * cos - x2 * sin, x2 * cos + x1 * sin], axis=-1)


def axial_rope(x, cos_r, sin_r, cos_c, sin_c):
    xr, xc = jnp.split(x, 2, axis=-1)
    return jnp.concatenate([rotate_half(xr, cos_r, sin_r), rotate_half(xc, cos_c, sin_c)], axis=-1)


def mlstm_chunkwise(q, k, v, log_i, log_f, state0):
    b_sz, h_sz, t_len, d = q.shape
    L = MLSTM_CHUNK
    nc = t_len // L
    qc = q.reshape(b_sz, h_sz, nc, L, d)
    kc = k.reshape(b_sz, h_sz, nc, L, d)
    vc = v.reshape(b_sz, h_sz, nc, L, d)
    li = log_i.reshape(b_sz, h_sz, nc, L)
    bcum = jnp.cumsum(log_f.reshape(b_sz, h_sz, nc, L), axis=-1)
    b_tot = bcum[..., -1]
    tri = jnp.tril(jnp.ones((L, L), dtype=bool))
    d_log = jnp.where(tri, bcum[..., :, None] - bcum[..., None, :] + li[..., None, :], NEG_INF)
    m_intra = jnp.max(d_log, axis=-1)
    s = jnp.einsum('bhcjd,bhcld->bhcjl', qc, kc).astype(jnp.float32) * jnp.exp(d_log - m_intra[..., None])
    num_intra = jnp.einsum('bhcjl,bhcld->bhcjd', s, vc)
    den_intra = jnp.sum(s, axis=-1)
    w_log = b_tot[..., None] - bcum + li
    m_loc = jnp.max(w_log, axis=-1)
    e = jnp.exp(w_log - m_loc[..., None])
    c_loc = jnp.einsum('bhcl,bhcld,bhcle->bhcde', e, kc, vc)
    n_loc = jnp.einsum('bhcl,bhcld->bhcd', e, kc)

    def step(carry, inp):
        c_st, n_st, m_st = carry
        cl, nl, ml, bt = inp
        m_new = jnp.maximum(bt + m_st, ml)
        a = jnp.exp(bt + m_st - m_new)
        g = jnp.exp(ml - m_new)
        new = (a[..., None, None] * c_st + g[..., None, None] * cl, a[..., None] * n_st + g[..., None] * nl, m_new)
        return new, carry

    xs = (jnp.moveaxis(c_loc, 2, 0), jnp.moveaxis(n_loc, 2, 0), jnp.moveaxis(m_loc, 2, 0), jnp.moveaxis(b_tot, 2, 0))
    final, prev = lax.scan(step, state0, xs)
    c_prev = jnp.moveaxis(prev[0], 0, 2)
    n_prev = jnp.moveaxis(prev[1], 0, 2)
    m_prev = jnp.moveaxis(prev[2], 0, 2)
    inter_log = bcum + m_prev[..., None]
    m_q = jnp.maximum(m_intra, inter_log)
    a = jnp.exp(inter_log - m_q)
    g = jnp.exp(m_intra - m_q)
    num = a[..., None] * jnp.einsum('bhcjd,bhcde->bhcje', qc, c_prev) + g[..., None] * num_intra
    den = a * jnp.einsum('bhcjd,bhcd->bhcj', qc, n_prev) + g * den_intra
    h = num / jnp.maximum(jnp.abs(den), jnp.exp(-m_q))[..., None]
    return h.reshape(b_sz, h_sz, t_len, d), final


def retention_log_decay(first_exp):
    e = first_exp + 2.0 * jnp.arange(H_RET, dtype=jnp.float32)
    return jnp.log1p(-jnp.exp2(-e))


def retention_chunkwise(q, k, v, log_gamma, state0):
    b_sz, h_sz, t_len, d = q.shape
    L = RET_CHUNK
    nc = t_len // L
    qc = q.reshape(b_sz, h_sz, nc, L, d)
    kc = k.reshape(b_sz, h_sz, nc, L, d)
    vc = v.reshape(b_sz, h_sz, nc, L, d)
    pos = jnp.arange(L, dtype=jnp.float32)
    diff = pos[:, None] - pos[None, :]
    decay = jnp.where(diff >= 0, jnp.exp(log_gamma[:, None, None] * jnp.maximum(diff, 0.0)), 0.0)
    s = jnp.einsum('bhcjd,bhcld->bhcjl', qc, kc).astype(jnp.float32) * decay[:, None]
    intra = jnp.einsum('bhcjl,bhcld->bhcjd', s, vc)
    zeta = jnp.exp(log_gamma[:, None] * (L - 1 - pos))
    s_loc = jnp.einsum('hl,bhcld,bhcle->bhcde', zeta, kc, vc)
    g_chunk = jnp.exp(log_gamma * L)[:, None, None]

    def step(r, sl):
        return g_chunk * r + sl, r

    final, r_prev = lax.scan(step, state0, jnp.moveaxis(s_loc, 2, 0))
    r_prev = jnp.moveaxis(r_prev, 0, 2)
    xi = jnp.exp(log_gamma[:, None] * (pos + 1.0))
    inter = jnp.einsum('bhcjd,bhcde->bhcje', qc, r_prev) * xi[:, None, :, None]
    return (intra + inter).reshape(b_sz, h_sz, t_len, d), final


def na_window(rows):
    wr = min(NA_ROWS, rows)
    r = jnp.arange(rows)
    col = jnp.arange(GRID_W)
    row_idx = jnp.clip(r - wr // 2, 0, rows - wr)[:, None] + jnp.arange(wr)[None, :]
    col_start = jnp.clip(col - NA_COLS // 2, 0, GRID_W - NA_COLS)
    col_in = (col[None, :] >= col_start[:, None]) & (col[None, :] < col_start[:, None] + NA_COLS)
    dr = row_idx - r[:, None] + NA_ROWS - 1
    dc = jnp.clip(col[None, :] - col[:, None] + NA_COLS - 1, 0, 2 * NA_COLS - 2)
    return (row_idx, col_in, dr[:, None, :, None], dc[None, :, None, :])


def na_latent(q, k, v, k_ctx, v_ctx, row_idx, col_in, bias):
    t_len, h_sz, d = q.shape
    rows, wr = row_idx.shape
    scale = d ** -0.5
    qg = q.reshape(rows, GRID_W, h_sz, d)
    kb = k.reshape(rows, GRID_W, h_sz, d)[row_idx]
    vb = v.reshape(rows, GRID_W, h_sz, d)[row_idx]
    s_loc = jnp.einsum('rqhd,rjwhd->hrqjw', qg, kb).astype(jnp.float32) * scale + bias
    s_loc = jnp.where(col_in[:, None, :], s_loc, NEG_INF)
    s_ctx = jnp.einsum('rqhd,chd->hrqc', qg, k_ctx).astype(jnp.float32) * scale
    n_loc = wr * GRID_W
    s = jnp.concatenate([s_loc.reshape(h_sz, rows, GRID_W, n_loc), s_ctx], axis=-1)
    p = jax.nn.softmax(s, axis=-1).astype(v.dtype)
    p_loc = p[..., :n_loc].reshape(h_sz, rows, GRID_W, wr, GRID_W)
    o = jnp.einsum('hrqjw,rjwhd->rqhd', p_loc, vb) + jnp.einsum('hrqc,chd->rqhd', p[..., n_loc:], v_ctx)
    return o.reshape(t_len, h_sz * d)


def ctx_attention(q, k, v):
    s = jnp.einsum('bqhd,bkhd->bhqk', q, k).astype(jnp.float32) * HEAD_DIM ** -0.5
    p = jax.nn.softmax(s, axis=-1).astype(v.dtype)
    o = jnp.einsum('bhqk,bkhd->bqhd', p, v)
    return o.reshape(o.shape[0], o.shape[1], -1)


def hybrid_mixer(px, pc, rope, na_win, conv_w, conv_b, gate_b, m_norm, q_norm, k_norm, rpb, r_norm, ctx_out):
    b_sz = px.shape[0]
    xp = jnp.split(px, SPLIT_POINTS, axis=-1)
    cp = jnp.split(pc, SPLIT_POINTS, axis=-1)

    def mlstm_prep(p):
        qk = jax.nn.silu(dwconv_centred(p[0], conv_w, conv_b))
        q, k = jnp.split(qk, 2, axis=-1)
        g = (p[3] + gate_b).astype(jnp.float32)
        g = jnp.transpose(g.reshape(b_sz, -1, 4, H_MLSTM), (2, 0, 3, 1))
        return (to_bhtd(split_heads(q, H_MLSTM)), to_bhtd(split_heads(k, H_MLSTM)) * HEAD_DIM ** -0.5,
                to_bhtd(split_heads(p[1], H_MLSTM)), g[0], jax.nn.log_sigmoid(g[1]), g[2], jax.nn.log_sigmoid(g[3]))

    qx, kx, vx, ix_f, lfx_f, ix_b, lfx_b = mlstm_prep(xp)
    qc, kc, vc, ic_f, lfc_f, ic_b, lfc_b = mlstm_prep(cp)
    zero_m = (jnp.zeros((b_sz, H_MLSTM, HEAD_DIM, HEAD_DIM), jnp.float32),
              jnp.zeros((b_sz, H_MLSTM, HEAD_DIM), jnp.float32), jnp.zeros((b_sz, H_MLSTM), jnp.float32))
    hc_f, st_f = mlstm_chunkwise(qc, kc, vc, ic_f, lfc_f, zero_m)
    hc_b, st_b = mlstm_chunkwise(flip_t(qc), flip_t(kc), flip_t(vc), flip_t(ic_b), flip_t(lfc_b), zero_m)
    hx_f, _ = mlstm_chunkwise(qx, kx, vx, ix_f, lfx_f, st_f)
    hx_b, _ = mlstm_chunkwise(flip_t(qx), flip_t(kx), flip_t(vx), flip_t(ix_b), flip_t(lfx_b), st_b)

    def mlstm_out(h, o):
        return merge_heads(rms_norm(h, m_norm)).astype(o.dtype) * jax.nn.sigmoid(o)

    a_x = mlstm_out(hx_f + flip_t(hx_b), xp[2])

    def na_prep(p):
        return (rms_norm(split_heads(p[4], H_NA), q_norm), rms_norm(split_heads(p[5], H_NA), k_norm),
                split_heads(p[6], H_NA))

    nqx, nkx, nvx = na_prep(xp)
    nqc, nkc, nvc = na_prep(cp)
    row_idx, col_in, dr, dc = na_win
    bias = rpb[:, dr, dc].astype(jnp.float32)
    b_x = lax.map(lambda a: na_latent(a[0], a[1], a[2], a[3], a[4], row_idx, col_in, bias),
                  (nqx, nkx, nvx, nkc, nvc))

    lg_f = retention_log_decay(5.0)
    lg_b = retention_log_decay(6.0)
    rqx = axial_rope(to_bhtd(split_heads(xp[7], H_RET)), *rope)
    rkx = axial_rope(to_bhtd(split_heads(xp[8], H_RET)), *rope) * HEAD_DIM ** -0.5
    rvx = to_bhtd(split_heads(xp[9], H_RET))
    rqc = to_bhtd(split_heads(cp[7], H_RET))
    rkc = to_bhtd(split_heads(cp[8], H_RET)) * HEAD_DIM ** -0.5
    rvc = to_bhtd(split_heads(cp[9], H_RET))
    zero_r = jnp.zeros((b_sz, H_RET, HEAD_DIM, HEAD_DIM), jnp.float32)
    rc_f, rs_f = retention_chunkwise(rqc, rkc, rvc, lg_f, zero_r)
    rc_b, rs_b = retention_chunkwise(flip_t(rqc), flip_t(rkc), flip_t(rvc), lg_b, zero_r)
    rx_f, _ = retention_chunkwise(rqx, rkx, rvx, lg_f, rs_f)
    rx_b, _ = retention_chunkwise(flip_t(rqx), flip_t(rkx), flip_t(rvx), lg_b, rs_b)

    def ret_out(h, g):
        return merge_heads(rms_norm(h, r_norm)).astype(g.dtype) * jax.nn.silu(g)

    c_x = ret_out(rx_f + flip_t(rx_b), xp[10])
    mix_x = jnp.concatenate([a_x, b_x, c_x], axis=-1)
    if not ctx_out:
        return mix_x, None
    a_c = mlstm_out(hc_f + flip_t(hc_b), cp[2])
    b_c = ctx_attention(nqc, nkc, nvc)
    c_c = ret_out(rc_f + flip_t(rc_b), cp[10])
    return mix_x, jnp.concatenate([a_c, b_c, c_c], axis=-1)


def clamped_swiglu(up):
    glu = jnp.minimum(up[..., ::2], SWIGLU_LIMIT)
    lin = jnp.clip(up[..., 1::2], -SWIGLU_LIMIT, SWIGLU_LIMIT)
    return glu * jax.nn.sigmoid(SWIGLU_ALPHA * glu) * (lin + 1.0)


def moe_ffn(h, w_r, b_r, w_up, b_up, w_down, b_down):
    n_tok, d = h.shape
    logits = (h @ w_r + b_r).astype(jnp.float32)
    top_v, top_e = lax.top_k(logits, TOP_K)
    gates = jax.nn.softmax(top_v, axis=-1)
    n_as = n_tok * TOP_K
    flat_e = top_e.reshape(n_as)
    order = jnp.argsort(flat_e)
    e_sorted = flat_e[order]
    tok_sorted = order // TOP_K
    counts = jnp.bincount(flat_e, length=N_EXPERTS)
    padded = (counts + MOE_BLOCK - 1) // MOE_BLOCK * MOE_BLOCK
    end_pad = jnp.cumsum(padded)
    start_pad = end_pad - padded
    start = jnp.cumsum(counts) - counts
    dest = start_pad[e_sorted] + jnp.arange(n_as) - start[e_sorted]
    n_blocks = -(-n_as // MOE_BLOCK) + N_EXPERTS
    n_rows = n_blocks * MOE_BLOCK
    row_tok = jnp.full((n_rows,), n_tok, dtype=jnp.int32).at[dest].set(tok_sorted.astype(jnp.int32))
    h_pad = jnp.concatenate([h, jnp.zeros((1, d), h.dtype)], axis=0)
    x_blocks = h_pad[row_tok].reshape(n_blocks, MOE_BLOCK, d)
    blk_e = jnp.minimum(jnp.searchsorted(end_pad, jnp.arange(n_blocks) * MOE_BLOCK, side='right'), N_EXPERTS - 1)

    def expert_block(args):
        xb, e = args
        return clamped_swiglu(xb @ w_up[e] + b_up[e]) @ w_down[e] + b_down[e]

    y = lax.map(expert_block, (x_blocks, blk_e)).reshape(n_rows, d)
    y_as = y[dest] * gates.reshape(n_as)[order][:, None].astype(y.dtype)
    return jax.ops.segment_sum(y_as, tok_sorted, num_segments=n_tok)


def setup_inputs(seed: int = 0) -> dict:
    key = jax.random.key(seed)
    ks = jax.random.split(key, 24)
    f32 = jnp.float32

    def normal(k, shape, scale):
        return jax.random.normal(k, shape, f32) * scale

    D, L, E, F = D_MODEL, DEPTH, N_EXPERTS, D_EXPERT
    fb = jnp.linspace(3.0, 6.0, H_MLSTM)
    gate_pattern = jnp.stack([jnp.zeros_like(fb), fb, jnp.zeros_like(fb), fb])
    gate_b = (normal(ks[11], (L, 4, H_MLSTM), 0.1) + gate_pattern).reshape(L, 4 * H_MLSTM)
    return {
        "x": normal(ks[0], (BATCH, SEQ, D), 1.0),
        "c": normal(ks[1], (BATCH, D), 1.0),
        "ctx": normal(ks[2], (BATCH, CTX_LEN, D), 1.0),
        "c_ctx": normal(ks[3], (D,), 1.0),
        "w_mod": normal(ks[4], (L, D, 6 * D), 0.5 * D ** -0.5),
        "b_mod": normal(ks[5], (L, 6 * D), 0.02),
        "norm_mix": 1.0 + normal(ks[6], (L, D), 0.02),
        "norm_ffn": 1.0 + normal(ks[7], (L, D), 0.02),
        "w_in": normal(ks[8], (L, D, D_IN), D ** -0.5),
        "w_out": normal(ks[9], (L, D_MIX, D), D_MIX ** -0.5),
        "mlstm_conv_w": normal(ks[10], (L, MLSTM_CONV, 2 * D_M), MLSTM_CONV ** -0.5),
        "mlstm_conv_b": normal(ks[12], (L, 2 * D_M), 0.02),
        "mlstm_gate_b": gate_b,
        "mlstm_norm": 1.0 + normal(ks[13], (L, HEAD_DIM), 0.02),
        "na_q_norm": 1.0 + normal(ks[14], (L, HEAD_DIM), 0.02),
        "na_k_norm": 1.0 + normal(ks[15], (L, HEAD_DIM), 0.02),
        "na_rpb": normal(ks[16], (L, H_NA, 2 * NA_ROWS - 1, 2 * NA_COLS - 1), 0.1),
        "ret_norm": 1.0 + normal(ks[17], (L, HEAD_DIM), 0.02),
        "router_w": normal(ks[18], (L, D, E), D ** -0.5),
        "router_b": normal(ks[19], (L, E), 0.01),
        "expert_w_up": normal(ks[20], (L, E, D, 2 * F), D ** -0.5),
        "expert_b_up": normal(ks[21], (L, E, 2 * F), 0.02),
        "expert_w_down": normal(ks[22], (L, E, F, D), F ** -0.5),
        "expert_b_down": normal(ks[23], (L, E, D), 0.02),
    }


def reference(x, c, ctx, c_ctx, w_mod, b_mod, norm_mix, norm_ffn, w_in, w_out,
              mlstm_conv_w, mlstm_conv_b, mlstm_gate_b, mlstm_norm,
              na_q_norm, na_k_norm, na_rpb, ret_norm,
              router_w, router_b, expert_w_up, expert_b_up, expert_w_down, expert_b_down):
    b_sz, t_len, d = x.shape
    t_ctx = ctx.shape[1]
    rows = t_len // GRID_W
    rope = axial_rope_tables(t_len)
    na_win = na_window(rows)
    for l in range(DEPTH):
        ctx_out = l < DEPTH - 1
        mx = jnp.split(jax.nn.silu(c) @ w_mod[l] + b_mod[l], 6, axis=-1)
        sh1, sc1, g1, sh2, sc2, g2 = [m[:, None, :] for m in mx]
        sh1c, sc1c, g1c, sh2c, sc2c, g2c = jnp.split(jax.nn.silu(c_ctx) @ w_mod[l] + b_mod[l], 6, axis=-1)
        hx = rms_norm(x, norm_mix[l]) * (1.0 + sc1) + sh1
        hc = rms_norm(ctx, norm_mix[l]) * (1.0 + sc1c) + sh1c
        mix_x, mix_c = hybrid_mixer(hx @ w_in[l], hc @ w_in[l], rope, na_win,
                                    mlstm_conv_w[l], mlstm_conv_b[l], mlstm_gate_b[l], mlstm_norm[l],
                                    na_q_norm[l], na_k_norm[l], na_rpb[l], ret_norm[l], ctx_out)
        x = x + g1 * (mix_x @ w_out[l])
        if ctx_out:
            ctx = ctx + g1c * (mix_c @ w_out[l])
        tok = (rms_norm(x, norm_ffn[l]) * (1.0 + sc2) + sh2).reshape(b_sz * t_len, d)
        if ctx_out:
            tok_c = (rms_norm(ctx, norm_ffn[l]) * (1.0 + sc2c) + sh2c).reshape(b_sz * t_ctx, d)
            tok = jnp.concatenate([tok, tok_c], axis=0)
        y = moe_ffn(tok, router_w[l], router_b[l], expert_w_up[l], expert_b_up[l], expert_w_down[l], expert_b_down[l])
        x = x + g2 * y[:b_sz * t_len].reshape(b_sz, t_len, d)
        if ctx_out:
            ctx = ctx + g2c * y[b_sz * t_len:].reshape(b_sz, t_ctx, d)
    return x
```

```python
import functools

import jax
import jax.numpy as jnp
import numpy as np
from jax import lax
from jax.experimental import pallas as pl
from jax.experimental.pallas import tpu as pltpu

F32 = jnp.float32
BF16 = jnp.bfloat16

GRID_W = 64
HEAD_DIM = 64
MLSTM_CHUNK = 64
RET_CHUNK = 64
NA_ROWS = 8
NA_COLS = 16
ROPE_BASE = 10000.0
TOP_K = 4
SWIGLU_ALPHA = 1.702
SWIGLU_LIMIT = 7.0
NORM_EPS = 1e-6
NEG_INF = -1e30

LANES = 128
VMEM_LIMIT = 48 * 1024 * 1024
MOD_ROWS = 8
ROW_TILE = 768
MOE_TILE = 512


def _cparams(sem):
    return pltpu.CompilerParams(dimension_semantics=sem, vmem_limit_bytes=VMEM_LIMIT)


def _mod_kernel(cc_ref, w_ref, b_ref, o_ref):
    cc = cc_ref[...]
    a = cc * jax.nn.sigmoid(cc)
    o_ref[0] = jnp.dot(a, w_ref[0], precision=lax.Precision.HIGHEST,
                       preferred_element_type=F32) + b_ref[0]


def _modulation(cc, w_mod, b_mod):
    n_l, d, d6 = w_mod.shape
    tn = d6 // 4
    return pl.pallas_call(
        _mod_kernel,
        grid=(n_l, d6 // tn),
        in_specs=[pl.BlockSpec((MOD_ROWS, d), lambda l, j: (0, 0)),
                  pl.BlockSpec((1, d, tn), lambda l, j: (l, 0, j)),
                  pl.BlockSpec((1, 1, tn), lambda l, j: (l, 0, j))],
        out_specs=pl.BlockSpec((1, MOD_ROWS, tn), lambda l, j: (l, 0, j)),
        out_shape=jax.ShapeDtypeStruct((n_l, MOD_ROWS, d6), F32),
        compiler_params=_cparams(("arbitrary", "arbitrary")),
        name="adaln_modulation",
    )(cc, w_mod, b_mod.reshape(n_l, 1, d6))


def _pick_mod(mb_ref, mc_ref, k, d, is_ctx):
    vb = mb_ref[0, :, k * d:(k + 1) * d]
    vc = mc_ref[0, :, k * d:(k + 1) * d]
    return jnp.where(is_ctx, vc, vb)


def _rms(x, g):
    return x * lax.rsqrt(jnp.mean(x * x, axis=-1, keepdims=True) + NORM_EPS) * g


def _proj_in_kernel(s_ref, mb_ref, mc_ref, g_ref, w_ref, wg_ref, pm_ref, pg_ref, xn_ref, *, t_ctx):
    i = pl.program_id(1)
    j = pl.program_id(2)
    tm, d = xn_ref.shape

    @pl.when(j == 0)
    def _():
        row = lax.broadcasted_iota(jnp.int32, (tm, 1), 0) + i * tm
        is_ctx = row < t_ctx
        sh = _pick_mod(mb_ref, mc_ref, 0, d, is_ctx)
        sc = _pick_mod(mb_ref, mc_ref, 1, d, is_ctx)
        h = _rms(s_ref[0], g_ref[...]) * (1.0 + sc) + sh
        xn_ref[...] = h.astype(BF16)
        pg_ref[0] = jnp.dot(xn_ref[...], wg_ref[...], preferred_element_type=F32)

    pm_ref[0] = jnp.dot(xn_ref[...], w_ref[...], preferred_element_type=F32)


def _proj_in(s, mod_l, g, w_main, w_gate, t_ctx):
    b_sz, s_len, d = s.shape
    n_main = w_main.shape[1]
    tn = n_main // 4
    tm = ROW_TILE
    nb = mod_l.shape[0] - 0
    return pl.pallas_call(
        functools.partial(_proj_in_kernel, t_ctx=t_ctx),
        grid=(b_sz, s_len // tm, n_main // tn),
        in_specs=[pl.BlockSpec((1, tm, d), lambda b, i, j: (b, i, 0)),
                  pl.BlockSpec((1, 1, 6 * d), lambda b, i, j: (b, 0, 0)),
                  pl.BlockSpec((1, 1, 6 * d), lambda b, i, j: (b_sz, 0, 0)),
                  pl.BlockSpec((1, d), lambda b, i, j: (0, 0)),
                  pl.BlockSpec((d, tn), lambda b, i, j: (0, j)),
                  pl.BlockSpec((d, LANES), lambda b, i, j: (0, 0))],
        out_specs=[pl.BlockSpec((1, tm, tn), lambda b, i, j: (b, i, j)),
                   pl.BlockSpec((1, tm, LANES), lambda b, i, j: (b, i, 0))],
        out_shape=[jax.ShapeDtypeStruct((b_sz, s_len, n_main), F32),
                   jax.ShapeDtypeStruct((b_sz, s_len, LANES), F32)],
        scratch_shapes=[pltpu.VMEM((tm, d), BF16)],
        compiler_params=_cparams(("arbitrary", "arbitrary", "arbitrary")),
        name="proj_in",
    )(s, mod_l, mod_l, g.reshape(1, d), w_main, w_gate)


def _proj_out_kernel(mix_ref, s_ref, mb_ref, mc_ref, g_ref, w_ref, rw_ref, rb_ref,
                     so_ref, tok_ref, lg_ref, *, t_ctx):
    i = pl.program_id(1)
    tm, d = mix_ref.shape[1], mix_ref.shape[2]
    row = lax.broadcasted_iota(jnp.int32, (tm, 1), 0) + i * tm
    is_ctx = row < t_ctx
    g1 = _pick_mod(mb_ref, mc_ref, 2, d, is_ctx)
    y = jnp.dot(mix_ref[0], w_ref[...], preferred_element_type=F32)
    s_new = s_ref[0] + g1 * y
    so_ref[0] = s_new
    sh = _pick_mod(mb_ref, mc_ref, 3, d, is_ctx)
    sc = _pick_mod(mb_ref, mc_ref, 4, d, is_ctx)
    t = _rms(s_new, g_ref[...]) * (1.0 + sc) + sh
    tok_ref[0] = t.astype(BF16)
    lg_ref[0] = jnp.dot(t, rw_ref[...], precision=lax.Precision.HIGHEST,
                        preferred_element_type=F32) + rb_ref[...]


def _proj_out(mix, s, mod_l, g, w_out, rw, rb, t_ctx):
    b_sz, s_len, d = s.shape
    tm = ROW_TILE
    row_spec = pl.BlockSpec((1, tm, d), lambda b, i: (b, i, 0))
    return pl.pallas_call(
        functools.partial(_proj_out_kernel, t_ctx=t_ctx),
        grid=(b_sz, s_len // tm),
        in_specs=[row_spec, row_spec,
                  pl.BlockSpec((1, 1, 6 * d), lambda b, i: (b, 0, 0)),
                  pl.BlockSpec((1, 1, 6 * d), lambda b, i: (b_sz, 0, 0)),
                  pl.BlockSpec((1, d), lambda b, i: (0, 0)),
                  pl.BlockSpec((d, d), lambda b, i: (0, 0)),
                  pl.BlockSpec((d, LANES), lambda b, i: (0, 0)),
                  pl.BlockSpec((1, LANES), lambda b, i: (0, 0))],
        out_specs=[row_spec, row_spec, pl.BlockSpec((1, tm, LANES), lambda b, i: (b, i, 0))],
        out_shape=[jax.ShapeDtypeStruct((b_sz, s_len, d), F32),
                   jax.ShapeDtypeStruct((b_sz, s_len, d), BF16),
                   jax.ShapeDtypeStruct((b_sz, s_len, LANES), F32)],
        compiler_params=_cparams(("arbitrary", "arbitrary")),
        name="proj_out_router",
    )(mix, s, mod_l, mod_l, g.reshape(1, d), w_out, rw, rb)


def _moe_kernel(be_ref, nu_ref, x_ref, wg_ref, wl_ref, wd_ref, bg_ref, bl_ref, bd_ref, y_ref):
    i = pl.program_id(0)

    @pl.when(i < nu_ref[0])
    def _():
        x = x_ref[...]
        glu = jnp.dot(x, wg_ref[0], preferred_element_type=F32) + bg_ref[0]
        lin = jnp.dot(x, wl_ref[0], preferred_element_type=F32) + bl_ref[0]
        glu = jnp.minimum(glu, SWIGLU_LIMIT)
        lin = jnp.clip(lin, -SWIGLU_LIMIT, SWIGLU_LIMIT)
        act = glu * jax.nn.sigmoid(SWIGLU_ALPHA * glu) * (lin + 1.0)
        y_ref[...] = jnp.dot(act.astype(BF16), wd_ref[0], preferred_element_type=F32) + bd_ref[0]

    @pl.when(i >= nu_ref[0])
    def _():
        y_ref[...] = jnp.zeros_like(y_ref)


def _moe_blocks(blk_e, n_used, x_sorted, w_glu, w_lin, w_down, b_glu, b_lin, b_down):
    n_rows, d = x_sorted.shape
    n_e, _, f = w_glu.shape
    tm = MOE_TILE
    wmap = lambda i, be, nu: (be[i], 0, 0)
    return pl.pallas_call(
        _moe_kernel,
        grid_spec=pltpu.PrefetchScalarGridSpec(
            num_scalar_prefetch=2,
            grid=(n_rows // tm,),
            in_specs=[pl.BlockSpec((tm, d), lambda i, be, nu: (i, 0)),
                      pl.BlockSpec((1, d, f), wmap),
                      pl.BlockSpec((1, d, f), wmap),
                      pl.BlockSpec((1, f, d), wmap),
                      pl.BlockSpec((1, 1, f), wmap),
                      pl.BlockSpec((1, 1, f), wmap),
                      pl.BlockSpec((1, 1, d), wmap)],
            out_specs=pl.BlockSpec((tm, d), lambda i, be, nu: (i, 0))),
        out_shape=jax.ShapeDtypeStruct((n_rows, d), F32),
        compiler_params=_cparams(("arbitrary",)),
        name="moe_expert_blocks",
    )(blk_e, n_used, x_sorted, w_glu, w_lin, w_down, b_glu, b_lin, b_down)


def _moe(tok, logits, n_e, w_glu, w_lin, w_down, b_glu, b_lin, b_down):
    n_tok, d = tok.shape
    tm = MOE_TILE
    top_v, top_e = lax.top_k(logits, TOP_K)
    gates = jax.nn.softmax(top_v, axis=-1)
    n_as = n_tok * TOP_K
    flat_e = top_e.reshape(n_as)
    order = jnp.argsort(flat_e)
    e_sorted = flat_e[order]
    tok_sorted = (order // TOP_K).astype(jnp.int32)
    counts = jnp.bincount(flat_e, length=n_e)
    padded = (counts + tm - 1) // tm * tm
    end_pad = jnp.cumsum(padded)
    start_pad = end_pad - padded
    start = jnp.cumsum(counts) - counts
    dest = (start_pad[e_sorted] + jnp.arange(n_as) - start[e_sorted]).astype(jnp.int32)
    n_blocks = -(-n_as // tm) + n_e
    n_rows = n_blocks * tm
    row_tok = jnp.zeros((n_rows,), jnp.int32).at[dest].set(tok_sorted)
    dest_as = jnp.zeros((n_as,), jnp.int32).at[order].set(dest)
    blk_e = jnp.minimum(jnp.searchsorted(end_pad, jnp.arange(n_blocks) * tm, side='right'),
                        n_e - 1).astype(jnp.int32)
    n_used = (end_pad[-1:] // tm).astype(jnp.int32)
    x_sorted = tok[row_tok]
    y = _moe_blocks(blk_e, n_used, x_sorted, w_glu, w_lin, w_down, b_glu, b_lin, b_down)
    y_as = y[dest_as].reshape(n_tok, TOP_K, d) * gates[..., None]
    return jnp.sum(y_as, axis=1)


def _rms_norm(x, g):
    xf = x.astype(F32)
    y = xf * lax.rsqrt(jnp.mean(xf * xf, axis=-1, keepdims=True) + NORM_EPS)
    return (y * g.astype(F32)).astype(x.dtype)


def _split_heads(t, n_heads):
    return t.reshape(t.shape[0], t.shape[1], n_heads, HEAD_DIM)


def _to_bhtd(t):
    return jnp.transpose(t, (0, 2, 1, 3))


def _merge_heads(t):
    b, h, t_len, d = t.shape
    return jnp.transpose(t, (0, 2, 1, 3)).reshape(b, t_len, h * d)


def _flip_t(t):
    return jnp.flip(t, axis=2)


def _dwconv_centred(x, w, b):
    k_size = w.shape[0]
    pad = k_size // 2
    t_len = x.shape[1]
    xp = jnp.pad(x, ((0, 0), (pad, pad), (0, 0)))
    y = b
    for i in range(k_size):
        y = y + xp[:, i:i + t_len] * w[i]
    return y


def _axial_rope_tables(t_len):
    pos = jnp.arange(t_len)
    row = (pos // GRID_W).astype(F32)
    col = (pos % GRID_W).astype(F32)
    n = HEAD_DIM // 4
    inv = ROPE_BASE ** (-jnp.arange(n, dtype=F32) / n)
    ar = row[:, None] * inv
    ac = col[:, None] * inv
    return (jnp.cos(ar), jnp.sin(ar), jnp.cos(ac), jnp.sin(ac))


def _rotate_half(x, cos, sin):
    x1, x2 = jnp.split(x, 2, axis=-1)
    return jnp.concatenate([x1 * cos - x2 * sin, x2 * cos + x1 * sin], axis=-1)


def _axial_rope(x, cos_r, sin_r, cos_c, sin_c):
    xr, xc = jnp.split(x, 2, axis=-1)
    return jnp.concatenate([_rotate_half(xr, cos_r, sin_r), _rotate_half(xc, cos_c, sin_c)], axis=-1)


def _mlstm_chunkwise(q, k, v, log_i, log_f, state0):
    b_sz, h_sz, t_len, d = q.shape
    L = MLSTM_CHUNK
    nc = t_len // L
    qc = q.reshape(b_sz, h_sz, nc, L, d)
    kc = k.reshape(b_sz, h_sz, nc, L, d)
    vc = v.reshape(b_sz, h_sz, nc, L, d)
    li = log_i.reshape(b_sz, h_sz, nc, L)
    bcum = jnp.cumsum(log_f.reshape(b_sz, h_sz, nc, L), axis=-1)
    b_tot = bcum[..., -1]
    tri = jnp.tril(jnp.ones((L, L), dtype=bool))
    d_log = jnp.where(tri, bcum[..., :, None] - bcum[..., None, :] + li[..., None, :], NEG_INF)
    m_intra = jnp.max(d_log, axis=-1)
    s = jnp.einsum('bhcjd,bhcld->bhcjl', qc, kc).astype(F32) * jnp.exp(d_log - m_intra[..., None])
    num_intra = jnp.einsum('bhcjl,bhcld->bhcjd', s, vc)
    den_intra = jnp.sum(s, axis=-1)
    w_log = b_tot[..., None] - bcum + li
    m_loc = jnp.max(w_log, axis=-1)
    e = jnp.exp(w_log - m_loc[..., None])
    c_loc = jnp.einsum('bhcl,bhcld,bhcle->bhcde', e, kc, vc)
    n_loc = jnp.einsum('bhcl,bhcld->bhcd', e, kc)

    def step(carry, inp):
        c_st, n_st, m_st = carry
        cl, nl, ml, bt = inp
        m_new = jnp.maximum(bt + m_st, ml)
        a = jnp.exp(bt + m_st - m_new)
        g = jnp.exp(ml - m_new)
        new = (a[..., None, None] * c_st + g[..., None, None] * cl, a[..., None] * n_st + g[..., None] * nl, m_new)
        return new, carry

    xs = (jnp.moveaxis(c_loc, 2, 0), jnp.moveaxis(n_loc, 2, 0), jnp.moveaxis(m_loc, 2, 0), jnp.moveaxis(b_tot, 2, 0))
    final, prev = lax.scan(step, state0, xs)
    c_prev = jnp.moveaxis(prev[0], 0, 2)
    n_prev = jnp.moveaxis(prev[1], 0, 2)
    m_prev = jnp.moveaxis(prev[2], 0, 2)
    inter_log = bcum + m_prev[..., None]
    m_q = jnp.maximum(m_intra, inter_log)
    a = jnp.exp(inter_log - m_q)
    g = jnp.exp(m_intra - m_q)
    num = a[..., None] * jnp.einsum('bhcjd,bhcde->bhcje', qc, c_prev) + g[..., None] * num_intra
    den = a * jnp.einsum('bhcjd,bhcd->bhcj', qc, n_prev) + g * den_intra
    h = num / jnp.maximum(jnp.abs(den), jnp.exp(-m_q))[..., None]
    return h.reshape(b_sz, h_sz, t_len, d), final


def _retention_log_decay(first_exp, n_heads):
    e = first_exp + 2.0 * jnp.arange(n_heads, dtype=F32)
    return jnp.log1p(-jnp.exp2(-e))


def _retention_chunkwise(q, k, v, log_gamma, state0):
    b_sz, h_sz, t_len, d = q.shape
    L = RET_CHUNK
    nc = t_len // L
    qc = q.reshape(b_sz, h_sz, nc, L, d)
    kc = k.reshape(b_sz, h_sz, nc, L, d)
    vc = v.reshape(b_sz, h_sz, nc, L, d)
    pos = jnp.arange(L, dtype=F32)
    diff = pos[:, None] - pos[None, :]
    decay = jnp.where(diff >= 0, jnp.exp(log_gamma[:, None, None] * jnp.maximum(diff, 0.0)), 0.0)
    s = jnp.einsum('bhcjd,bhcld->bhcjl', qc, kc).astype(F32) * decay[:, None]
    intra = jnp.einsum('bhcjl,bhcld->bhcjd', s, vc)
    zeta = jnp.exp(log_gamma[:, None] * (L - 1 - pos))
    s_loc = jnp.einsum('hl,bhcld,bhcle->bhcde', zeta, kc, vc)
    g_chunk = jnp.exp(log_gamma * L)[:, None, None]

    def step(r, sl):
        return g_chunk * r + sl, r

    final, r_prev = lax.scan(step, state0, jnp.moveaxis(s_loc, 2, 0))
    r_prev = jnp.moveaxis(r_prev, 0, 2)
    xi = jnp.exp(log_gamma[:, None] * (pos + 1.0))
    inter = jnp.einsum('bhcjd,bhcde->bhcje', qc, r_prev) * xi[:, None, :, None]
    return (intra + inter).reshape(b_sz, h_sz, t_len, d), final


def _na_window(rows):
    wr = min(NA_ROWS, rows)
    r = jnp.arange(rows)
    col = jnp.arange(GRID_W)
    row_idx = jnp.clip(r - wr // 2, 0, rows - wr)[:, None] + jnp.arange(wr)[None, :]
    col_start = jnp.clip(col - NA_COLS // 2, 0, GRID_W - NA_COLS)
    col_in = (col[None, :] >= col_start[:, None]) & (col[None, :] < col_start[:, None] + NA_COLS)
    dr = row_idx - r[:, None] + NA_ROWS - 1
    dc = jnp.clip(col[None, :] - col[:, None] + NA_COLS - 1, 0, 2 * NA_COLS - 2)
    return (row_idx, col_in, dr[:, None, :, None], dc[None, :, None, :])


def _na_latent(q, k, v, k_ctx, v_ctx, row_idx, col_in, bias):
    t_len, h_sz, d = q.shape
    rows, wr = row_idx.shape
    scale = d ** -0.5
    qg = q.reshape(rows, GRID_W, h_sz, d)
    kb = k.reshape(rows, GRID_W, h_sz, d)[row_idx]
    vb = v.reshape(rows, GRID_W, h_sz, d)[row_idx]
    s_loc = jnp.einsum('rqhd,rjwhd->hrqjw', qg, kb).astype(F32) * scale + bias
    s_loc = jnp.where(col_in[:, None, :], s_loc, NEG_INF)
    s_ctx = jnp.einsum('rqhd,chd->hrqc', qg, k_ctx).astype(F32) * scale
    n_loc = wr * GRID_W
    s = jnp.concatenate([s_loc.reshape(h_sz, rows, GRID_W, n_loc), s_ctx], axis=-1)
    p = jax.nn.softmax(s, axis=-1).astype(v.dtype)
    p_loc = p[..., :n_loc].reshape(h_sz, rows, GRID_W, wr, GRID_W)
    o = jnp.einsum('hrqjw,rjwhd->rqhd', p_loc, vb) + jnp.einsum('hrqc,chd->rqhd', p[..., n_loc:], v_ctx)
    return o.reshape(t_len, h_sz * d)


def _ctx_attention(q, k, v):
    s = jnp.einsum('bqhd,bkhd->bhqk', q, k).astype(F32) * HEAD_DIM ** -0.5
    p = jax.nn.softmax(s, axis=-1).astype(v.dtype)
    o = jnp.einsum('bhqk,bkhd->bqhd', p, v)
    return o.reshape(o.shape[0], o.shape[1], -1)


def _hybrid_mixer(xp, cp, rope, na_win, conv_w, conv_b, gate_b, m_norm, q_norm, k_norm, rpb, r_norm, dims):
    h_m, h_na, h_r = dims
    b_sz = xp[0].shape[0]

    def mlstm_prep(p):
        qk = jax.nn.silu(_dwconv_centred(p[0], conv_w, conv_b))
        q, k = jnp.split(qk, 2, axis=-1)
        g = (p[3] + gate_b).astype(F32)
        g = jnp.transpose(g.reshape(b_sz, -1, 4, h_m), (2, 0, 3, 1))
        return (_to_bhtd(_split_heads(q, h_m)), _to_bhtd(_split_heads(k, h_m)) * HEAD_DIM ** -0.5,
                _to_bhtd(_split_heads(p[1], h_m)), g[0], jax.nn.log_sigmoid(g[1]), g[2], jax.nn.log_sigmoid(g[3]))

    qx, kx, vx, ix_f, lfx_f, ix_b, lfx_b = mlstm_prep(xp)
    qc, kc, vc, ic_f, lfc_f, ic_b, lfc_b = mlstm_prep(cp)
    zero_m = (jnp.zeros((b_sz, h_m, HEAD_DIM, HEAD_DIM), F32),
              jnp.zeros((b_sz, h_m, HEAD_DIM), F32), jnp.zeros((b_sz, h_m), F32))
    hc_f, st_f = _mlstm_chunkwise(qc, kc, vc, ic_f, lfc_f, zero_m)
    hc_b, st_b = _mlstm_chunkwise(_flip_t(qc), _flip_t(kc), _flip_t(vc), _flip_t(ic_b), _flip_t(lfc_b), zero_m)
    hx_f, _ = _mlstm_chunkwise(qx, kx, vx, ix_f, lfx_f, st_f)
    hx_b, _ = _mlstm_chunkwise(_flip_t(qx), _flip_t(kx), _flip_t(vx), _flip_t(ix_b), _flip_t(lfx_b), st_b)

    def mlstm_out(h, o):
        return _merge_heads(_rms_norm(h, m_norm)).astype(o.dtype) * jax.nn.sigmoid(o)

    a_x = mlstm_out(hx_f + _flip_t(hx_b), xp[2])

    def na_prep(p):
        return (_rms_norm(_split_heads(p[4], h_na), q_norm), _rms_norm(_split_heads(p[5], h_na), k_norm),
                _split_heads(p[6], h_na))

    nqx, nkx, nvx = na_prep(xp)
    nqc, nkc, nvc = na_prep(cp)
    row_idx, col_in, dr, dc = na_win
    bias = rpb[:, dr, dc].astype(F32)
    b_x = lax.map(lambda a: _na_latent(a[0], a[1], a[2], a[3], a[4], row_idx, col_in, bias),
                  (nqx, nkx, nvx, nkc, nvc))

    lg_f = _retention_log_decay(5.0, h_r)
    lg_b = _retention_log_decay(6.0, h_r)
    rqx = _axial_rope(_to_bhtd(_split_heads(xp[7], h_r)), *rope)
    rkx = _axial_rope(_to_bhtd(_split_heads(xp[8], h_r)), *rope) * HEAD_DIM ** -0.5
    rvx = _to_bhtd(_split_heads(xp[9], h_r))
    rqc = _to_bhtd(_split_heads(cp[7], h_r))
    rkc = _to_bhtd(_split_heads(cp[8], h_r)) * HEAD_DIM ** -0.5
    rvc = _to_bhtd(_split_heads(cp[9], h_r))
    zero_r = jnp.zeros((b_sz, h_r, HEAD_DIM, HEAD_DIM), F32)
    rc_f, rs_f = _retention_chunkwise(rqc, rkc, rvc, lg_f, zero_r)
    rc_b, rs_b = _retention_chunkwise(_flip_t(rqc), _flip_t(rkc), _flip_t(rvc), lg_b, zero_r)
    rx_f, _ = _retention_chunkwise(rqx, rkx, rvx, lg_f, rs_f)
    rx_b, _ = _retention_chunkwise(_flip_t(rqx), _flip_t(rkx), _flip_t(rvx), lg_b, rs_b)

    def ret_out(h, g):
        return _merge_heads(_rms_norm(h, r_norm)).astype(g.dtype) * jax.nn.silu(g)

    c_x = ret_out(rx_f + _flip_t(rx_b), xp[10])
    mix_x = jnp.concatenate([a_x, b_x, c_x], axis=-1)
    a_c = mlstm_out(hc_f + _flip_t(hc_b), cp[2])
    b_c = _ctx_attention(nqc, nkc, nvc)
    c_c = ret_out(rc_f + _flip_t(rc_b), cp[10])
    return mix_x, jnp.concatenate([a_c, b_c, c_c], axis=-1)


def kernel(x, c, ctx, c_ctx, w_mod, b_mod, norm_mix, norm_ffn, w_in, w_out, mlstm_conv_w, mlstm_conv_b,
           mlstm_gate_b, mlstm_norm, na_q_norm, na_k_norm, na_rpb, ret_norm, router_w, router_b,
           expert_w_up, expert_b_up, expert_w_down, expert_b_down):
    b_sz, t_len, d = x.shape
    t_ctx = ctx.shape[1]
    depth = w_in.shape[0]
    n_e = router_w.shape[2]
    d_mix = w_out.shape[1]
    h_m = d_mix // (4 * HEAD_DIM)
    h_na = d_mix // (2 * HEAD_DIM)
    h_r = d_mix // (4 * HEAD_DIM)
    d_m, d_na, d_r = h_m * HEAD_DIM, h_na * HEAD_DIM, h_r * HEAD_DIM
    n_gate = 4 * h_m
    assert b_sz + 1 <= MOD_ROWS and n_e <= LANES and n_gate <= LANES
    s_len = t_ctx + t_len
    assert s_len % ROW_TILE == 0

    s = jnp.concatenate([ctx, x], axis=1)
    cc = jnp.zeros((MOD_ROWS, d), F32).at[:b_sz].set(c).at[b_sz].set(c_ctx)
    mods = _modulation(cc, w_mod, b_mod).reshape(depth, MOD_ROWS, 1, 6 * d)

    g0 = 4 * d_m
    widths = (2 * d_m, d_m, d_m, d_na, d_na, d_na, d_r, d_r, d_r, d_r)
    offs = np.concatenate([[0], np.cumsum(widths)])
    rope = _axial_rope_tables(t_len)
    na_win = _na_window(t_len // GRID_W)
    is_ctx = (jnp.arange(s_len) < t_ctx)[None, :, None]

    for l in range(depth):
        w_main = jnp.concatenate([w_in[l, :, :g0], w_in[l, :, g0 + n_gate:]], axis=1).astype(BF16)
        w_gate = jnp.zeros((d, LANES), BF16).at[:, :n_gate].set(w_in[l, :, g0:g0 + n_gate].astype(BF16))
        p_main, p_gate = _proj_in(s, mods[l], norm_mix[l], w_main, w_gate, t_ctx)

        parts = [p_main[..., offs[k]:offs[k + 1]] for k in range(len(widths))]
        parts.insert(3, p_gate[..., :n_gate])
        xp = [t[:, t_ctx:] for t in parts]
        cp = [t[:, :t_ctx] for t in parts]
        mix_x, mix_c = _hybrid_mixer(xp, cp, rope, na_win, mlstm_conv_w[l], mlstm_conv_b[l], mlstm_gate_b[l],
                                     mlstm_norm[l], na_q_norm[l], na_k_norm[l], na_rpb[l], ret_norm[l],
                                     (h_m, h_na, h_r))
        mix = jnp.concatenate([mix_c, mix_x], axis=1).astype(BF16)

        rw = jnp.zeros((d, LANES), F32).at[:, :n_e].set(router_w[l])
        rb = jnp.zeros((1, LANES), F32).at[0, :n_e].set(router_b[l])
        s, tok, logits = _proj_out(mix, s, mods[l], norm_ffn[l], w_out[l].astype(BF16), rw, rb, t_ctx)

        w_up = expert_w_up[l]
        y = _moe(tok.reshape(b_sz * s_len, d), logits.reshape(b_sz * s_len, LANES)[:, :n_e], n_e,
                 w_up[:, :, 0::2].astype(BF16), w_up[:, :, 1::2].astype(BF16), expert_w_down[l].astype(BF16),
                 expert_b_up[l][:, None, 0::2], expert_b_up[l][:, None, 1::2], expert_b_down[l][:, None, :])
        g2 = jnp.where(is_ctx, mods[l, b_sz, 0, 5 * d:][None, None, :], mods[l, :b_sz, :, 5 * d:])
        s = s + g2 * y.reshape(b_sz, s_len, d)
    return s[:, t_ctx:]
```

```python
import functools

import jax
import jax.numpy as jnp
import numpy as np
from jax import lax
from jax.experimental import pallas as pl
from jax.experimental.pallas import tpu as pltpu

F32 = jnp.float32
BF16 = jnp.bfloat16

GRID_W = 64
HEAD_DIM = 64
MLSTM_CHUNK = 64
RET_CHUNK = 64
NA_ROWS = 8
NA_COLS = 16
ROPE_BASE = 10000.0
TOP_K = 4
SWIGLU_ALPHA = 1.702
SWIGLU_LIMIT = 7.0
NORM_EPS = 1e-6
NEG_INF = -1e30

LANES = 128
VMEM_LIMIT = 48 * 1024 * 1024
MOD_ROWS = 8
ROW_TILE = 768
MOE_TILE = 512


def _cparams(sem):
    return pltpu.CompilerParams(dimension_semantics=sem, vmem_limit_bytes=VMEM_LIMIT)


def _mod_kernel(cc_ref, w_ref, b_ref, o_ref):
    cc = cc_ref[...]
    a = cc * jax.nn.sigmoid(cc)
    o_ref[0] = jnp.dot(a, w_ref[0], precision=lax.Precision.HIGHEST,
                       preferred_element_type=F32) + b_ref[0]


def _modulation(cc, w_mod, b_mod):
    n_l, d, d6 = w_mod.shape
    tn = d6 // 4
    return pl.pallas_call(
        _mod_kernel,
        grid=(n_l, d6 // tn),
        in_specs=[pl.BlockSpec((MOD_ROWS, d), lambda l, j: (0, 0)),
                  pl.BlockSpec((1, d, tn), lambda l, j: (l, 0, j)),
                  pl.BlockSpec((1, 1, tn), lambda l, j: (l, 0, j))],
        out_specs=pl.BlockSpec((1, MOD_ROWS, tn), lambda l, j: (l, 0, j)),
        out_shape=jax.ShapeDtypeStruct((n_l, MOD_ROWS, d6), F32),
        compiler_params=_cparams(("arbitrary", "arbitrary")),
        name="adaln_modulation",
    )(cc, w_mod, b_mod.reshape(n_l, 1, d6))


def _pick_mod(mb_ref, mc_ref, k, d, is_ctx):
    vb = mb_ref[0, :, k * d:(k + 1) * d]
    vc = mc_ref[0, :, k * d:(k + 1) * d]
    return jnp.where(is_ctx, vc, vb)


def _rms(x, g):
    return x * lax.rsqrt(jnp.mean(x * x, axis=-1, keepdims=True) + NORM_EPS) * g


def _proj_in_kernel(s_ref, mb_ref, mc_ref, g_ref, w_ref, wg_ref, pm_ref, pg_ref, xn_ref, *, t_ctx):
    i = pl.program_id(1)
    j = pl.program_id(2)
    tm, d = xn_ref.shape

    @pl.when(j == 0)
    def _():
        row = lax.broadcasted_iota(jnp.int32, (tm, 1), 0) + i * tm
        is_ctx = row < t_ctx
        sh = _pick_mod(mb_ref, mc_ref, 0, d, is_ctx)
        sc = _pick_mod(mb_ref, mc_ref, 1, d, is_ctx)
        h = _rms(s_ref[0], g_ref[...]) * (1.0 + sc) + sh
        xn_ref[...] = h.astype(BF16)
        pg_ref[0] = jnp.dot(xn_ref[...], wg_ref[...], preferred_element_type=F32)

    pm_ref[0] = jnp.dot(xn_ref[...], w_ref[...], preferred_element_type=F32)


def _proj_in(s, mod_l, g, w_main, w_gate, t_ctx):
    b_sz, s_len, d = s.shape
    n_main = w_main.shape[1]
    n_gate = w_gate.shape[1]
    tn = n_main // 4
    tm = ROW_TILE
    return pl.pallas_call(
        functools.partial(_proj_in_kernel, t_ctx=t_ctx),
        grid=(b_sz, s_len // tm, n_main // tn),
        in_specs=[pl.BlockSpec((1, tm, d), lambda b, i, j: (b, i, 0)),
                  pl.BlockSpec((1, 1, 6 * d), lambda b, i, j: (b, 0, 0)),
                  pl.BlockSpec((1, 1, 6 * d), lambda b, i, j: (b_sz, 0, 0)),
                  pl.BlockSpec((1, d), lambda b, i, j: (0, 0)),
                  pl.BlockSpec((d, tn), lambda b, i, j: (0, j)),
                  pl.BlockSpec((d, n_gate), lambda b, i, j: (0, 0))],
        out_specs=[pl.BlockSpec((1, tm, tn), lambda b, i, j: (b, i, j)),
                   pl.BlockSpec((1, tm, n_gate), lambda b, i, j: (b, i, 0))],
        out_shape=[jax.ShapeDtypeStruct((b_sz, s_len, n_main), F32),
                   jax.ShapeDtypeStruct((b_sz, s_len, n_gate), F32)],
        scratch_shapes=[pltpu.VMEM((tm, d), BF16)],
        compiler_params=_cparams(("arbitrary", "arbitrary", "arbitrary")),
        name="proj_in",
    )(s, mod_l, mod_l, g.reshape(1, d), w_main, w_gate)


def _proj_out_kernel(ma_ref, mb2_ref, mc2_ref, s_ref, mb_ref, mc_ref, g_ref, w_ref, rw_ref, rb_ref,
                     so_ref, tok_ref, lg_ref, *, t_ctx):
    i = pl.program_id(1)
    tm, d = s_ref.shape[1], s_ref.shape[2]
    row = lax.broadcasted_iota(jnp.int32, (tm, 1), 0) + i * tm
    is_ctx = row < t_ctx
    g1 = _pick_mod(mb_ref, mc_ref, 2, d, is_ctx)
    ka, kb = ma_ref.shape[2], ma_ref.shape[2] + mb2_ref.shape[2]
    y = (jnp.dot(ma_ref[0], w_ref[0:ka, :], preferred_element_type=F32)
         + jnp.dot(mb2_ref[0], w_ref[ka:kb, :], preferred_element_type=F32)
         + jnp.dot(mc2_ref[0], w_ref[kb:, :], preferred_element_type=F32))
    s_new = s_ref[0] + g1 * y
    so_ref[0] = s_new
    sh = _pick_mod(mb_ref, mc_ref, 3, d, is_ctx)
    sc = _pick_mod(mb_ref, mc_ref, 4, d, is_ctx)
    t = _rms(s_new, g_ref[...]) * (1.0 + sc) + sh
    tok_ref[0] = t.astype(BF16)
    lg_ref[0] = jnp.dot(t, rw_ref[...], precision=lax.Precision.HIGHEST,
                        preferred_element_type=F32) + rb_ref[...]


def _proj_out(mix_parts, s, mod_l, g, w_out, rw, rb, t_ctx):
    b_sz, s_len, d = s.shape
    tm = ROW_TILE
    row_spec = pl.BlockSpec((1, tm, d), lambda b, i: (b, i, 0))
    part_specs = [pl.BlockSpec((1, tm, m.shape[2]), lambda b, i: (b, i, 0)) for m in mix_parts]
    return pl.pallas_call(
        functools.partial(_proj_out_kernel, t_ctx=t_ctx),
        grid=(b_sz, s_len // tm),
        in_specs=part_specs + [row_spec,
                  pl.BlockSpec((1, 1, 6 * d), lambda b, i: (b, 0, 0)),
                  pl.BlockSpec((1, 1, 6 * d), lambda b, i: (b_sz, 0, 0)),
                  pl.BlockSpec((1, d), lambda b, i: (0, 0)),
                  pl.BlockSpec((d, d), lambda b, i: (0, 0)),
                  pl.BlockSpec((d, LANES), lambda b, i: (0, 0)),
                  pl.BlockSpec((1, LANES), lambda b, i: (0, 0))],
        out_specs=[row_spec, row_spec, pl.BlockSpec((1, tm, LANES), lambda b, i: (b, i, 0))],
        out_shape=[jax.ShapeDtypeStruct((b_sz, s_len, d), F32),
                   jax.ShapeDtypeStruct((b_sz, s_len, d), BF16),
                   jax.ShapeDtypeStruct((b_sz, s_len, LANES), F32)],
        compiler_params=_cparams(("arbitrary", "arbitrary")),
        name="proj_out_router",
    )(*mix_parts, s, mod_l, mod_l, g.reshape(1, d), w_out, rw, rb)


def _moe_kernel(be_ref, nu_ref, x_ref, wg_ref, wl_ref, wd_ref, bg_ref, bl_ref, bd_ref, y_ref):
    i = pl.program_id(0)

    @pl.when(i < nu_ref[0])
    def _():
        x = x_ref[...]
        glu = jnp.dot(x, wg_ref[0], preferred_element_type=F32) + bg_ref[0]
        lin = jnp.dot(x, wl_ref[0], preferred_element_type=F32) + bl_ref[0]
        glu = jnp.minimum(glu, SWIGLU_LIMIT)
        lin = jnp.clip(lin, -SWIGLU_LIMIT, SWIGLU_LIMIT)
        act = glu * jax.nn.sigmoid(SWIGLU_ALPHA * glu) * (lin + 1.0)
        y_ref[...] = jnp.dot(act.astype(BF16), wd_ref[0], preferred_element_type=F32) + bd_ref[0]

    @pl.when(i >= nu_ref[0])
    def _():
        y_ref[...] = jnp.zeros_like(y_ref)


def _moe_blocks(blk_e, n_used, x_sorted, w_glu, w_lin, w_down, b_glu, b_lin, b_down):
    n_rows, d = x_sorted.shape
    n_e, _, f = w_glu.shape
    tm = MOE_TILE
    wmap = lambda i, be, nu: (be[i], 0, 0)
    return pl.pallas_call(
        _moe_kernel,
        grid_spec=pltpu.PrefetchScalarGridSpec(
            num_scalar_prefetch=2,
            grid=(n_rows // tm,),
            in_specs=[pl.BlockSpec((tm, d), lambda i, be, nu: (i, 0)),
                      pl.BlockSpec((1, d, f), wmap),
                      pl.BlockSpec((1, d, f), wmap),
                      pl.BlockSpec((1, f, d), wmap),
                      pl.BlockSpec((1, 1, f), wmap),
                      pl.BlockSpec((1, 1, f), wmap),
                      pl.BlockSpec((1, 1, d), wmap)],
            out_specs=pl.BlockSpec((tm, d), lambda i, be, nu: (i, 0))),
        out_shape=jax.ShapeDtypeStruct((n_rows, d), F32),
        compiler_params=_cparams(("arbitrary",)),
        name="moe_expert_blocks",
    )(blk_e, n_used, x_sorted, w_glu, w_lin, w_down, b_glu, b_lin, b_down)


def _moe(tok, logits, n_e, w_glu, w_lin, w_down, b_glu, b_lin, b_down):
    n_tok, d = tok.shape
    tm = MOE_TILE
    top_v, top_e = lax.top_k(logits, TOP_K)
    gates = jax.nn.softmax(top_v, axis=-1)
    n_as = n_tok * TOP_K
    flat_e = top_e.reshape(n_as)
    order = jnp.argsort(flat_e)
    e_sorted = flat_e[order]
    tok_sorted = (order // TOP_K).astype(jnp.int32)
    counts = jnp.bincount(flat_e, length=n_e)
    padded = (counts + tm - 1) // tm * tm
    end_pad = jnp.cumsum(padded)
    start_pad = end_pad - padded
    start = jnp.cumsum(counts) - counts
    dest = (start_pad[e_sorted] + jnp.arange(n_as) - start[e_sorted]).astype(jnp.int32)
    n_blocks = -(-n_as // tm) + n_e
    n_rows = n_blocks * tm
    row_tok = jnp.zeros((n_rows,), jnp.int32).at[dest].set(tok_sorted)
    dest_as = jnp.zeros((n_as,), jnp.int32).at[order].set(dest)
    blk_e = jnp.minimum(jnp.searchsorted(end_pad, jnp.arange(n_blocks) * tm, side='right'),
                        n_e - 1).astype(jnp.int32)
    n_used = (end_pad[-1:] // tm).astype(jnp.int32)
    x_sorted = tok[row_tok]
    y = _moe_blocks(blk_e, n_used, x_sorted, w_glu, w_lin, w_down, b_glu, b_lin, b_down)
    y_as = y[dest_as].reshape(n_tok, TOP_K, d) * gates[..., None]
    return jnp.sum(y_as, axis=1)


NA_STEP = 256
NA_KEYS = NA_ROWS * GRID_W


def _head_rms(x, w, lo):
    xx = x * x
    s0 = jnp.sum(jnp.where(lo, xx, 0.0), axis=-1, keepdims=True)
    s1 = jnp.sum(jnp.where(lo, 0.0, xx), axis=-1, keepdims=True)
    inv = lax.rsqrt(jnp.where(lo, s0, s1) * (1.0 / HEAD_DIM) + NORM_EPS)
    return x * inv * w


def _dot_nt(a, b):
    return lax.dot_general(a, b, (((1,), (1,)), ((), ())), preferred_element_type=F32)


def _na_kernel(q_ref, k_ref, v_ref, qw_ref, kw_ref, bias_ref, o_ref, kn_ref, vb_ref, *, t_ctx, rows):
    rg = pl.program_id(2)
    s_len = k_ref.shape[1]
    lo = lax.broadcasted_iota(jnp.int32, (1, LANES), 1) < HEAD_DIM
    rows_per_step = NA_STEP // GRID_W

    @pl.when(rg == 0)
    def _():
        def prep(c, carry):
            t0 = pl.multiple_of(c * NA_STEP, NA_STEP)
            kn_ref[pl.ds(t0, NA_STEP), :] = _head_rms(k_ref[0, pl.ds(t0, NA_STEP), :], kw_ref[...], lo).astype(BF16)
            vb_ref[pl.ds(t0, NA_STEP), :] = v_ref[0, pl.ds(t0, NA_STEP), :].astype(BF16)
            return carry
        lax.fori_loop(0, s_len // NA_STEP, prep, 0)

    qn = _head_rms(q_ref[0], qw_ref[...], lo) * (HEAD_DIM ** -0.5)
    q0 = jnp.where(lo, qn, 0.0).astype(BF16)
    q1 = jnp.where(lo, 0.0, qn).astype(BF16)
    kc = kn_ref[0:t_ctx, :]
    vc = vb_ref[0:t_ctx, :]

    def finish(g, o2, l):
        o2 = o2 * (1.0 / l)
        o = jnp.where(lo, o2[:GRID_W], o2[GRID_W:])
        o_ref[0, g * GRID_W:(g + 1) * GRID_W, :] = o.astype(o_ref.dtype)

    @pl.when(rg == 0)
    def _():
        for g in range(rows_per_step):
            q2 = jnp.concatenate([q0[g * GRID_W:(g + 1) * GRID_W], q1[g * GRID_W:(g + 1) * GRID_W]], axis=0)
            s_c = _dot_nt(q2, kc)
            m = jnp.max(s_c, axis=-1, keepdims=True)
            p_c = jnp.exp(s_c - m)
            l = jnp.sum(p_c, axis=-1, keepdims=True)
            finish(g, jnp.dot(p_c.astype(BF16), vc, preferred_element_type=F32), l)

    @pl.when(rg > 0)
    def _():
        for g in range(rows_per_step):
            r = (rg - 1) * rows_per_step + g
            row_start = jnp.clip(r - NA_ROWS // 2, 0, rows - NA_ROWS)
            dr0 = row_start - r + NA_ROWS - 1
            t0 = pl.multiple_of(t_ctx + row_start * GRID_W, GRID_W)
            kw = kn_ref[pl.ds(t0, NA_KEYS), :]
            vw = vb_ref[pl.ds(t0, NA_KEYS), :]
            q2 = jnp.concatenate([q0[g * GRID_W:(g + 1) * GRID_W], q1[g * GRID_W:(g + 1) * GRID_W]], axis=0)
            s_l = _dot_nt(q2, kw) + bias_ref[dr0, 0]
            s_c = _dot_nt(q2, kc)
            m = jnp.maximum(jnp.max(s_l, axis=-1, keepdims=True), jnp.max(s_c, axis=-1, keepdims=True))
            p_l = jnp.exp(s_l - m)
            p_c = jnp.exp(s_c - m)
            l = jnp.sum(p_l, axis=-1, keepdims=True) + jnp.sum(p_c, axis=-1, keepdims=True)
            o2 = (jnp.dot(p_l.astype(BF16), vw, preferred_element_type=F32)
                  + jnp.dot(p_c.astype(BF16), vc, preferred_element_type=F32))
            finish(g, o2, l)


def _na_bias_tables(rpb):
    n_l, n_h, n_dr, n_dc = rpb.shape
    col = np.arange(GRID_W)
    dc = np.clip(col[None, :] - col[:, None] + NA_COLS - 1, 0, n_dc - 1)
    onehot = (dc.reshape(1, -1) == np.arange(n_dc)[:, None]).astype(np.float32)
    toe = jnp.dot(rpb.reshape(-1, n_dc), jnp.asarray(onehot), precision=lax.Precision.HIGHEST)
    toe = toe.reshape(n_l, n_h, n_dr, GRID_W, GRID_W)
    col_start = np.clip(col - NA_COLS // 2, 0, GRID_W - NA_COLS)
    col_in = (col[None, :] >= col_start[:, None]) & (col[None, :] < col_start[:, None] + NA_COLS)
    toe = jnp.where(jnp.asarray(col_in), toe, NEG_INF)
    tabs = []
    for dr0 in range(NA_ROWS):
        t = toe[:, :, dr0:dr0 + NA_ROWS].reshape(n_l, n_h // 2, 2, NA_ROWS, GRID_W, GRID_W)
        t = jnp.transpose(t, (0, 1, 2, 4, 3, 5))
        tabs.append(t.reshape(n_l, n_h // 2, 2 * GRID_W, NA_KEYS))
    return jnp.stack(tabs, axis=1)


def _na_attention(p_main, q_w, k_w, bias_tab, t_ctx, col_q, n_heads):
    b_sz, s_len, _ = p_main.shape
    n_hg = n_heads // 2
    rows = (s_len - t_ctx) // GRID_W
    assert t_ctx == NA_STEP and s_len % NA_STEP == 0 and rows >= NA_ROWS
    cq, ck, cv = col_q // LANES, col_q // LANES + n_hg, col_q // LANES + 2 * n_hg
    w2 = lambda w: jnp.concatenate([w, w]).reshape(1, LANES)
    return pl.pallas_call(
        functools.partial(_na_kernel, t_ctx=t_ctx, rows=rows),
        grid=(b_sz, n_hg, s_len // NA_STEP),
        in_specs=[pl.BlockSpec((1, NA_STEP, LANES), lambda b, h, r: (b, r, cq + h)),
                  pl.BlockSpec((1, s_len, LANES), lambda b, h, r: (b, 0, ck + h)),
                  pl.BlockSpec((1, s_len, LANES), lambda b, h, r: (b, 0, cv + h)),
                  pl.BlockSpec((1, LANES), lambda b, h, r: (0, 0)),
                  pl.BlockSpec((1, LANES), lambda b, h, r: (0, 0)),
                  pl.BlockSpec((NA_ROWS, 1, 2 * GRID_W, NA_KEYS), lambda b, h, r: (0, h, 0, 0))],
        out_specs=pl.BlockSpec((1, NA_STEP, LANES), lambda b, h, r: (b, r, h)),
        out_shape=jax.ShapeDtypeStruct((b_sz, s_len, n_heads * HEAD_DIM), BF16),
        scratch_shapes=[pltpu.VMEM((s_len, LANES), BF16), pltpu.VMEM((s_len, LANES), BF16)],
        compiler_params=_cparams(("arbitrary", "arbitrary", "arbitrary")),
        name="na_attention",
    )(p_main, p_main, p_main, w2(q_w), w2(k_w), bias_tab)


MIX_CHUNK = 256


def _dot_tn(a, b):
    return lax.dot_general(a, b, (((0,), (0,)), ((), ())), preferred_element_type=F32)


def _rope_tables(t_ctx, t_len):
    pos = jnp.arange(t_len)
    row = (pos // GRID_W).astype(F32)
    col = (pos % GRID_W).astype(F32)
    n = HEAD_DIM // 4
    inv = ROPE_BASE ** (-jnp.arange(n, dtype=F32) / n)
    ar = row[:, None] * inv
    ac = col[:, None] * inv
    cos = jnp.concatenate([jnp.cos(ar), jnp.cos(ar), jnp.cos(ac), jnp.cos(ac)], axis=-1)
    sin = jnp.concatenate([-jnp.sin(ar), jnp.sin(ar), -jnp.sin(ac), jnp.sin(ac)], axis=-1)
    cos = jnp.concatenate([jnp.ones((t_ctx, HEAD_DIM), F32), cos], axis=0)
    sin = jnp.concatenate([jnp.zeros((t_ctx, HEAD_DIM), F32), sin], axis=0)
    return jnp.tile(cos, (1, 2)), jnp.tile(sin, (1, 2))


def _ret_tables(n_heads):
    L = MIX_CHUNK
    pos = np.arange(L, dtype=np.float32)
    lane_head = np.arange(LANES) // HEAD_DIM
    decay = np.zeros((2, n_heads, L, L), np.float32)
    zeta = np.zeros((2, n_heads // 2, L, LANES), np.float32)
    xi = np.zeros((2, n_heads // 2, L, LANES), np.float32)
    gch = np.zeros((2, n_heads // 2, 1, LANES), np.float32)
    for d, first_exp in enumerate((5.0, 6.0)):
        e = np.float32(first_exp) + np.float32(2.0) * np.arange(n_heads, dtype=np.float32)
        lg = np.log1p(-np.exp2(-e)).astype(np.float32)
        diff = pos[:, None] - pos[None, :]
        if d == 1:
            diff = -diff
        for h in range(n_heads):
            decay[d, h] = np.where(diff >= 0, np.exp(lg[h] * np.maximum(diff, 0.0)), 0.0)
        for hp in range(n_heads // 2):
            lgl = lg[2 * hp + lane_head][None, :]
            to_end = (L - 1 - pos if d == 0 else pos)[:, None]
            zeta[d, hp] = np.exp(lgl * to_end)
            xi[d, hp] = np.exp(lgl * (L - to_end))
            gch[d, hp] = np.exp(lgl * L)
    return tuple(jnp.asarray(a) for a in (decay, zeta, xi, gch))


def _ret_kernel(q_ref, k_ref, v_ref, g_ref, cos_ref, sin_ref, dec_ref, zeta_ref, xi_ref, gch_ref, rn_ref,
                o_ref, qr_ref, kr_ref, vb_ref, acc_ref):
    L = MIX_CHUNK
    s_len = q_ref.shape[1]
    nc = s_len // L
    lane = lax.broadcasted_iota(jnp.int32, (1, LANES), 1)
    lo = lane < HEAD_DIM
    half = (lane & (HEAD_DIM // 4)) == 0
    rid = lax.broadcasted_iota(jnp.int32, (LANES, LANES), 0) < HEAD_DIM
    cid = lax.broadcasted_iota(jnp.int32, (LANES, LANES), 1) < HEAD_DIM
    same_head = rid == cid

    def rope(x, cos, sin):
        up = pltpu.roll(x, LANES - HEAD_DIM // 4, axis=1)
        dn = pltpu.roll(x, HEAD_DIM // 4, axis=1)
        return x * cos + jnp.where(half, up, dn) * sin

    def prep(c, carry):
        sl = pl.ds(pl.multiple_of(c * L, L), L)
        cos, sin = cos_ref[sl, :], sin_ref[sl, :]
        qr_ref[sl, :] = rope(q_ref[0, sl, :], cos, sin).astype(BF16)
        kr_ref[sl, :] = (rope(k_ref[0, sl, :], cos, sin) * HEAD_DIM ** -0.5).astype(BF16)
        vb_ref[sl, :] = v_ref[0, sl, :].astype(BF16)
        return carry
    lax.fori_loop(0, nc, prep, 0)

    def chunk(d, c, state):
        sl = pl.ds(pl.multiple_of(c * L, L), L)
        q, k, v = qr_ref[sl, :], kr_ref[sl, :], vb_ref[sl, :]
        inter = jnp.dot(q, state.astype(BF16), preferred_element_type=F32) * xi_ref[d, 0]
        outs = []
        for h2 in range(2):
            qm = jnp.where(lo if h2 == 0 else jnp.logical_not(lo), q, jnp.zeros_like(q))
            sd = (_dot_nt(qm, k) * dec_ref[d, h2]).astype(BF16)
            outs.append(jnp.dot(sd, v, preferred_element_type=F32))
        y = jnp.where(lo, outs[0], outs[1]) + inter
        kz = (k.astype(F32) * zeta_ref[d, 0]).astype(BF16)
        state = state * gch_ref[d, 0] + jnp.where(same_head, _dot_tn(kz, v), 0.0)
        return sl, y, state

    def fwd(c, state):
        sl, y, state = chunk(0, c, state)
        acc_ref[sl, :] = y
        return state
    lax.fori_loop(0, nc, fwd, jnp.zeros((LANES, LANES), F32))

    def bwd(i, state):
        c = jnp.where(i == 0, 0, nc - i)
        sl, y, state = chunk(1, c, state)
        y = _head_rms(y + acc_ref[sl, :], rn_ref[...], lo)
        g = g_ref[0, sl, :]
        o_ref[0, sl, :] = (y * (g * jax.nn.sigmoid(g))).astype(o_ref.dtype)
        return state
    lax.fori_loop(0, nc, bwd, jnp.zeros((LANES, LANES), F32))


def _seq_spec(s_len, col):
    return pl.BlockSpec((1, s_len, LANES), lambda b, h: (b, 0, col + h), pipeline_mode=pl.Buffered(1))


def _retention(p_main, rope_tabs, ret_tabs, r_w, t_ctx, col_q, n_heads):
    b_sz, s_len, _ = p_main.shape
    n_hp = n_heads // 2
    L = MIX_CHUNK
    assert t_ctx % L == 0 and s_len % L == 0
    c0 = col_q // LANES
    cos, sin = rope_tabs
    decay, zeta, xi, gch = ret_tabs
    const2 = pl.BlockSpec((s_len, LANES), lambda b, h: (0, 0), pipeline_mode=pl.Buffered(1))
    return pl.pallas_call(
        _ret_kernel,
        grid=(b_sz, n_hp),
        in_specs=[_seq_spec(s_len, c0), _seq_spec(s_len, c0 + n_hp), _seq_spec(s_len, c0 + 2 * n_hp),
                  _seq_spec(s_len, c0 + 3 * n_hp), const2, const2,
                  pl.BlockSpec((2, 2, L, L), lambda b, h: (0, h, 0, 0)),
                  pl.BlockSpec((2, 1, L, LANES), lambda b, h: (0, h, 0, 0)),
                  pl.BlockSpec((2, 1, L, LANES), lambda b, h: (0, h, 0, 0)),
                  pl.BlockSpec((2, 1, 1, LANES), lambda b, h: (0, h, 0, 0)),
                  pl.BlockSpec((1, LANES), lambda b, h: (0, 0))],
        out_specs=pl.BlockSpec((1, s_len, LANES), lambda b, h: (b, 0, h)),
        out_shape=jax.ShapeDtypeStruct((b_sz, s_len, n_heads * HEAD_DIM), BF16),
        scratch_shapes=[pltpu.VMEM((s_len, LANES), BF16), pltpu.VMEM((s_len, LANES), BF16),
                        pltpu.VMEM((s_len, LANES), BF16), pltpu.VMEM((s_len, LANES), F32)],
        compiler_params=_cparams(("arbitrary", "arbitrary")),
        name="retention",
    )(p_main, p_main, p_main, p_main, cos, sin, decay, zeta, xi, gch,
      jnp.concatenate([r_w, r_w]).reshape(1, LANES))


N_GATE_TYPES = 4


def _log_sigmoid(x):
    return jnp.minimum(x, 0.0) - jnp.log1p(jnp.exp(-jnp.abs(x)))


def _mlstm_kernel(q_ref, k_ref, v_ref, og_ref, gc_ref, gr_ref, wq_ref, wk_ref, bq_ref, bk_ref, gbc_ref, gbr_ref,
                  mn_ref, o_ref, qc_ref, kc_ref, vb_ref, acc_ref, *, t_ctx):
    L = MIX_CHUNK
    s_len = q_ref.shape[1]
    nc = s_len // L
    lane = lax.broadcasted_iota(jnp.int32, (1, LANES), 1)
    lo = lane < HEAD_DIM
    head_lanes = (lo, jnp.logical_not(lo))
    rid = lax.broadcasted_iota(jnp.int32, (LANES, LANES), 0) < HEAD_DIM
    cid = lax.broadcasted_iota(jnp.int32, (LANES, LANES), 1) < HEAD_DIM
    head_block = (rid & cid, jnp.logical_not(rid | cid))
    row_i = lax.broadcasted_iota(jnp.int32, (L, L), 0)
    col_i = lax.broadcasted_iota(jnp.int32, (L, L), 1)
    causal = (row_i >= col_i, row_i <= col_i)
    tri = tuple(c.astype(F32) for c in causal)
    tri_t = (tri[1], tri[0])
    sub = lax.broadcasted_iota(jnp.int32, (L, 1), 0)

    def conv(x_ref, w_ref, b_ref, t0):
        x = x_ref[0, pl.ds(t0, L), :]
        prev = x_ref[0, pl.ds(jnp.maximum(t0 - 8, 0), 8), :][7:8]
        nxt = x_ref[0, pl.ds(jnp.minimum(t0 + L, s_len - 8), 8), :][0:1]
        prev = jnp.where((t0 != 0) & (t0 != t_ctx), prev, 0.0)
        nxt = jnp.where((t0 + L != t_ctx) & (t0 + L != s_len), nxt, 0.0)
        xm = jnp.where(sub == 0, prev, pltpu.roll(x, 1, axis=0))
        xp = jnp.where(sub == L - 1, nxt, pltpu.roll(x, L - 1, axis=0))
        y = b_ref[...] + xm * w_ref[0:1, :] + x * w_ref[1:2, :] + xp * w_ref[2:3, :]
        return y * jax.nn.sigmoid(y)

    def prep(c, carry):
        t0 = pl.multiple_of(c * L, L)
        sl = pl.ds(t0, L)
        qc_ref[sl, :] = conv(q_ref, wq_ref, bq_ref, t0).astype(BF16)
        kc_ref[sl, :] = (conv(k_ref, wk_ref, bk_ref, t0) * HEAD_DIM ** -0.5).astype(BF16)
        vb_ref[sl, :] = v_ref[0, sl, :].astype(BF16)
        return carry
    lax.fori_loop(0, nc, prep, 0)

    def chunk(d, c, state):
        sl = pl.ds(pl.multiple_of(c * L, L), L)
        q, k, v = qc_ref[sl, :], kc_ref[sl, :], vb_ref[sl, :]
        g_col = gc_ref[0, sl, :] + gbc_ref[0]
        g_row = gr_ref[0, 0, :, sl] + gbr_ref[0]
        cum_col = jnp.dot(tri[d], _log_sigmoid(g_col), precision=lax.Precision.HIGHEST,
                          preferred_element_type=F32)
        cum_row = jnp.dot(_log_sigmoid(g_row), tri_t[d], precision=lax.Precision.HIGHEST,
                          preferred_element_type=F32)
        end = L - 1 if d == 0 else 0
        outs, new_state = [], []
        for h2 in range(2):
            c_st, n_st, m_st = state[h2]
            ci, cf = 2 * (2 * d) + h2, 2 * (2 * d + 1) + h2
            i_col, a_col = g_col[:, ci:ci + 1], cum_col[:, cf:cf + 1]
            i_row, a_row = g_row[ci:ci + 1, :], cum_row[cf:cf + 1, :]
            b_tot = a_row[:, end:end + 1]
            d_log = jnp.where(causal[d], a_col + (i_row - a_row), NEG_INF)
            m_intra = jnp.max(d_log, axis=-1, keepdims=True)
            qm = jnp.where(head_lanes[h2], q, jnp.zeros_like(q))
            s = _dot_nt(qm, k) * jnp.exp(d_log - m_intra)
            num_intra = jnp.dot(s.astype(BF16), v, preferred_element_type=F32)
            den_intra = jnp.sum(s, axis=-1, keepdims=True)
            inter_log = a_col + m_st
            m_q = jnp.maximum(m_intra, inter_log)
            a = jnp.exp(inter_log - m_q)
            g = jnp.exp(m_intra - m_q)
            num = a * jnp.dot(qm, c_st.astype(BF16), preferred_element_type=F32) + g * num_intra
            den = a * jnp.sum(qm.astype(F32) * n_st, axis=-1, keepdims=True) + g * den_intra
            outs.append(num / jnp.maximum(jnp.abs(den), jnp.exp(-m_q)))
            w_log = b_tot - a_col + i_col
            m_loc = jnp.max(w_log, axis=0, keepdims=True)
            ke = jnp.where(head_lanes[h2], k.astype(F32) * jnp.exp(w_log - m_loc), 0.0)
            c_loc = jnp.where(head_block[h2], _dot_tn(ke.astype(BF16), v), 0.0)
            n_loc = jnp.sum(ke, axis=0, keepdims=True)
            m_new = jnp.maximum(b_tot + m_st, m_loc)
            a_s = jnp.exp(b_tot + m_st - m_new)
            g_s = jnp.exp(m_loc - m_new)
            new_state.append((a_s * c_st + g_s * c_loc, a_s * n_st + g_s * n_loc, m_new))
        return sl, jnp.where(lo, outs[0], outs[1]), tuple(new_state)

    zero = tuple((jnp.zeros((LANES, LANES), F32), jnp.zeros((1, LANES), F32), jnp.zeros((1, 1), F32))
                 for _ in range(2))

    def fwd(c, state):
        sl, y, state = chunk(0, c, state)
        acc_ref[sl, :] = y
        return state
    lax.fori_loop(0, nc, fwd, zero)

    def bwd(i, state):
        c = jnp.where(i == 0, t_ctx // L - 1, nc - i + (t_ctx // L - 1))
        sl, y, state = chunk(1, c, state)
        y = _head_rms(y + acc_ref[sl, :], mn_ref[...], lo)
        o_ref[0, sl, :] = (y * jax.nn.sigmoid(og_ref[0, sl, :])).astype(o_ref.dtype)
        return state
    lax.fori_loop(0, nc, bwd, zero)


def _mlstm(p_main, p_gate, conv_w, conv_b, gate_b, m_w, t_ctx, n_heads):
    b_sz, s_len, _ = p_main.shape
    n_hp = n_heads // 2
    L = MIX_CHUNK
    assert t_ctx == L and s_len % L == 0
    n_g = 2 * N_GATE_TYPES
    g_rows = jnp.transpose(p_gate.reshape(b_sz, s_len, n_hp, LANES)[..., :n_g], (0, 2, 3, 1))
    gb = jnp.transpose(gate_b.reshape(N_GATE_TYPES, n_hp, 2), (1, 0, 2)).reshape(n_hp, n_g)
    gb_col = jnp.zeros((n_hp, 1, LANES), F32).at[:, 0, :n_g].set(gb)
    gb_row = gb.reshape(n_hp, n_g, 1)
    vec = lambda col: pl.BlockSpec((1, LANES), lambda b, h: (0, col + h))
    return pl.pallas_call(
        functools.partial(_mlstm_kernel, t_ctx=t_ctx),
        grid=(b_sz, n_hp),
        in_specs=[_seq_spec(s_len, 0), _seq_spec(s_len, n_hp), _seq_spec(s_len, 2 * n_hp),
                  _seq_spec(s_len, 3 * n_hp),
                  pl.BlockSpec((1, s_len, LANES), lambda b, h: (b, 0, h), pipeline_mode=pl.Buffered(1)),
                  pl.BlockSpec((1, 1, n_g, s_len), lambda b, h: (b, h, 0, 0)),
                  pl.BlockSpec((3, LANES), lambda b, h: (0, h)),
                  pl.BlockSpec((3, LANES), lambda b, h: (0, n_hp + h)),
                  vec(0), vec(n_hp),
                  pl.BlockSpec((1, 1, LANES), lambda b, h: (h, 0, 0)),
                  pl.BlockSpec((1, n_g, 1), lambda b, h: (h, 0, 0)),
                  pl.BlockSpec((1, LANES), lambda b, h: (0, 0))],
        out_specs=pl.BlockSpec((1, s_len, LANES), lambda b, h: (b, 0, h)),
        out_shape=jax.ShapeDtypeStruct((b_sz, s_len, n_heads * HEAD_DIM), BF16),
        scratch_shapes=[pltpu.VMEM((s_len, LANES), BF16), pltpu.VMEM((s_len, LANES), BF16),
                        pltpu.VMEM((s_len, LANES), BF16), pltpu.VMEM((s_len, LANES), F32)],
        compiler_params=_cparams(("arbitrary", "arbitrary")),
        name="mlstm",
    )(p_main, p_main, p_main, p_main, p_gate, g_rows, conv_w, conv_w, conv_b.reshape(1, -1),
      conv_b.reshape(1, -1), gb_col, gb_row, jnp.concatenate([m_w, m_w]).reshape(1, LANES))


def _rms_norm(x, g):
    xf = x.astype(F32)
    y = xf * lax.rsqrt(jnp.mean(xf * xf, axis=-1, keepdims=True) + NORM_EPS)
    return (y * g.astype(F32)).astype(x.dtype)


def _split_heads(t, n_heads):
    return t.reshape(t.shape[0], t.shape[1], n_heads, HEAD_DIM)


def _to_bhtd(t):
    return jnp.transpose(t, (0, 2, 1, 3))


def _merge_heads(t):
    b, h, t_len, d = t.shape
    return jnp.transpose(t, (0, 2, 1, 3)).reshape(b, t_len, h * d)


def _flip_t(t):
    return jnp.flip(t, axis=2)


def _dwconv_centred(x, w, b):
    k_size = w.shape[0]
    pad = k_size // 2
    t_len = x.shape[1]
    xp = jnp.pad(x, ((0, 0), (pad, pad), (0, 0)))
    y = b
    for i in range(k_size):
        y = y + xp[:, i:i + t_len] * w[i]
    return y


def _axial_rope_tables(t_len):
    pos = jnp.arange(t_len)
    row = (pos // GRID_W).astype(F32)
    col = (pos % GRID_W).astype(F32)
    n = HEAD_DIM // 4
    inv = ROPE_BASE ** (-jnp.arange(n, dtype=F32) / n)
    ar = row[:, None] * inv
    ac = col[:, None] * inv
    return (jnp.cos(ar), jnp.sin(ar), jnp.cos(ac), jnp.sin(ac))


def _rotate_half(x, cos, sin):
    x1, x2 = jnp.split(x, 2, axis=-1)
    return jnp.concatenate([x1 * cos - x2 * sin, x2 * cos + x1 * sin], axis=-1)


def _axial_rope(x, cos_r, sin_r, cos_c, sin_c):
    xr, xc = jnp.split(x, 2, axis=-1)
    return jnp.concatenate([_rotate_half(xr, cos_r, sin_r), _rotate_half(xc, cos_c, sin_c)], axis=-1)


def _mlstm_chunkwise(q, k, v, log_i, log_f, state0):
    b_sz, h_sz, t_len, d = q.shape
    L = MLSTM_CHUNK
    nc = t_len // L
    qc = q.reshape(b_sz, h_sz, nc, L, d)
    kc = k.reshape(b_sz, h_sz, nc, L, d)
    vc = v.reshape(b_sz, h_sz, nc, L, d)
    li = log_i.reshape(b_sz, h_sz, nc, L)
    bcum = jnp.cumsum(log_f.reshape(b_sz, h_sz, nc, L), axis=-1)
    b_tot = bcum[..., -1]
    tri = jnp.tril(jnp.ones((L, L), dtype=bool))
    d_log = jnp.where(tri, bcum[..., :, None] - bcum[..., None, :] + li[..., None, :], NEG_INF)
    m_intra = jnp.max(d_log, axis=-1)
    s = jnp.einsum('bhcjd,bhcld->bhcjl', qc, kc).astype(F32) * jnp.exp(d_log - m_intra[..., None])
    num_intra = jnp.einsum('bhcjl,bhcld->bhcjd', s, vc)
    den_intra = jnp.sum(s, axis=-1)
    w_log = b_tot[..., None] - bcum + li
    m_loc = jnp.max(w_log, axis=-1)
    e = jnp.exp(w_log - m_loc[..., None])
    c_loc = jnp.einsum('bhcl,bhcld,bhcle->bhcde', e, kc, vc)
    n_loc = jnp.einsum('bhcl,bhcld->bhcd', e, kc)

    def step(carry, inp):
        c_st, n_st, m_st = carry
        cl, nl, ml, bt = inp
        m_new = jnp.maximum(bt + m_st, ml)
        a = jnp.exp(bt + m_st - m_new)
        g = jnp.exp(ml - m_new)
        new = (a[..., None, None] * c_st + g[..., None, None] * cl, a[..., None] * n_st + g[..., None] * nl, m_new)
        return new, carry

    xs = (jnp.moveaxis(c_loc, 2, 0), jnp.moveaxis(n_loc, 2, 0), jnp.moveaxis(m_loc, 2, 0), jnp.moveaxis(b_tot, 2, 0))
    final, prev = lax.scan(step, state0, xs)
    c_prev = jnp.moveaxis(prev[0], 0, 2)
    n_prev = jnp.moveaxis(prev[1], 0, 2)
    m_prev = jnp.moveaxis(prev[2], 0, 2)
    inter_log = bcum + m_prev[..., None]
    m_q = jnp.maximum(m_intra, inter_log)
    a = jnp.exp(inter_log - m_q)
    g = jnp.exp(m_intra - m_q)
    num = a[..., None] * jnp.einsum('bhcjd,bhcde->bhcje', qc, c_prev) + g[..., None] * num_intra
    den = a * jnp.einsum('bhcjd,bhcd->bhcj', qc, n_prev) + g * den_intra
    h = num / jnp.maximum(jnp.abs(den), jnp.exp(-m_q))[..., None]
    return h.reshape(b_sz, h_sz, t_len, d), final


def _retention_log_decay(first_exp, n_heads):
    e = first_exp + 2.0 * jnp.arange(n_heads, dtype=F32)
    return jnp.log1p(-jnp.exp2(-e))


def _retention_chunkwise(q, k, v, log_gamma, state0):
    b_sz, h_sz, t_len, d = q.shape
    L = RET_CHUNK
    nc = t_len // L
    qc = q.reshape(b_sz, h_sz, nc, L, d)
    kc = k.reshape(b_sz, h_sz, nc, L, d)
    vc = v.reshape(b_sz, h_sz, nc, L, d)
    pos = jnp.arange(L, dtype=F32)
    diff = pos[:, None] - pos[None, :]
    decay = jnp.where(diff >= 0, jnp.exp(log_gamma[:, None, None] * jnp.maximum(diff, 0.0)), 0.0)
    s = jnp.einsum('bhcjd,bhcld->bhcjl', qc, kc).astype(F32) * decay[:, None]
    intra = jnp.einsum('bhcjl,bhcld->bhcjd', s, vc)
    zeta = jnp.exp(log_gamma[:, None] * (L - 1 - pos))
    s_loc = jnp.einsum('hl,bhcld,bhcle->bhcde', zeta, kc, vc)
    g_chunk = jnp.exp(log_gamma * L)[:, None, None]

    def step(r, sl):
        return g_chunk * r + sl, r

    final, r_prev = lax.scan(step, state0, jnp.moveaxis(s_loc, 2, 0))
    r_prev = jnp.moveaxis(r_prev, 0, 2)
    xi = jnp.exp(log_gamma[:, None] * (pos + 1.0))
    inter = jnp.einsum('bhcjd,bhcde->bhcje', qc, r_prev) * xi[:, None, :, None]
    return (intra + inter).reshape(b_sz, h_sz, t_len, d), final


def _na_window(rows):
    wr = min(NA_ROWS, rows)
    r = jnp.arange(rows)
    col = jnp.arange(GRID_W)
    row_idx = jnp.clip(r - wr // 2, 0, rows - wr)[:, None] + jnp.arange(wr)[None, :]
    col_start = jnp.clip(col - NA_COLS // 2, 0, GRID_W - NA_COLS)
    col_in = (col[None, :] >= col_start[:, None]) & (col[None, :] < col_start[:, None] + NA_COLS)
    dr = row_idx - r[:, None] + NA_ROWS - 1
    dc = jnp.clip(col[None, :] - col[:, None] + NA_COLS - 1, 0, 2 * NA_COLS - 2)
    return (row_idx, col_in, dr[:, None, :, None], dc[None, :, None, :])


def _na_latent(q, k, v, k_ctx, v_ctx, row_idx, col_in, bias):
    t_len, h_sz, d = q.shape
    rows, wr = row_idx.shape
    scale = d ** -0.5
    qg = q.reshape(rows, GRID_W, h_sz, d)
    kb = k.reshape(rows, GRID_W, h_sz, d)[row_idx]
    vb = v.reshape(rows, GRID_W, h_sz, d)[row_idx]
    s_loc = jnp.einsum('rqhd,rjwhd->hrqjw', qg, kb).astype(F32) * scale + bias
    s_loc = jnp.where(col_in[:, None, :], s_loc, NEG_INF)
    s_ctx = jnp.einsum('rqhd,chd->hrqc', qg, k_ctx).astype(F32) * scale
    n_loc = wr * GRID_W
    s = jnp.concatenate([s_loc.reshape(h_sz, rows, GRID_W, n_loc), s_ctx], axis=-1)
    p = jax.nn.softmax(s, axis=-1).astype(v.dtype)
    p_loc = p[..., :n_loc].reshape(h_sz, rows, GRID_W, wr, GRID_W)
    o = jnp.einsum('hrqjw,rjwhd->rqhd', p_loc, vb) + jnp.einsum('hrqc,chd->rqhd', p[..., n_loc:], v_ctx)
    return o.reshape(t_len, h_sz * d)


def _ctx_attention(q, k, v):
    s = jnp.einsum('bqhd,bkhd->bhqk', q, k).astype(F32) * HEAD_DIM ** -0.5
    p = jax.nn.softmax(s, axis=-1).astype(v.dtype)
    o = jnp.einsum('bhqk,bkhd->bqhd', p, v)
    return o.reshape(o.shape[0], o.shape[1], -1)


def _hybrid_mixer(xp, cp, rope, na_win, conv_w, conv_b, gate_b, m_norm, q_norm, k_norm, rpb, r_norm, dims):
    h_m, h_na, h_r = dims
    b_sz = xp[0].shape[0]

    def mlstm_prep(p):
        qk = jax.nn.silu(_dwconv_centred(p[0], conv_w, conv_b))
        q, k = jnp.split(qk, 2, axis=-1)
        g = (p[3] + gate_b).astype(F32)
        g = jnp.transpose(g.reshape(b_sz, -1, 4, h_m), (2, 0, 3, 1))
        return (_to_bhtd(_split_heads(q, h_m)), _to_bhtd(_split_heads(k, h_m)) * HEAD_DIM ** -0.5,
                _to_bhtd(_split_heads(p[1], h_m)), g[0], jax.nn.log_sigmoid(g[1]), g[2], jax.nn.log_sigmoid(g[3]))

    qx, kx, vx, ix_f, lfx_f, ix_b, lfx_b = mlstm_prep(xp)
    qc, kc, vc, ic_f, lfc_f, ic_b, lfc_b = mlstm_prep(cp)
    zero_m = (jnp.zeros((b_sz, h_m, HEAD_DIM, HEAD_DIM), F32),
              jnp.zeros((b_sz, h_m, HEAD_DIM), F32), jnp.zeros((b_sz, h_m), F32))
    hc_f, st_f = _mlstm_chunkwise(qc, kc, vc, ic_f, lfc_f, zero_m)
    hc_b, st_b = _mlstm_chunkwise(_flip_t(qc), _flip_t(kc), _flip_t(vc), _flip_t(ic_b), _flip_t(lfc_b), zero_m)
    hx_f, _ = _mlstm_chunkwise(qx, kx, vx, ix_f, lfx_f, st_f)
    hx_b, _ = _mlstm_chunkwise(_flip_t(qx), _flip_t(kx), _flip_t(vx), _flip_t(ix_b), _flip_t(lfx_b), st_b)

    def mlstm_out(h, o):
        return _merge_heads(_rms_norm(h, m_norm)).astype(o.dtype) * jax.nn.sigmoid(o)

    a_x = mlstm_out(hx_f + _flip_t(hx_b), xp[2])

    def na_prep(p):
        return (_rms_norm(_split_heads(p[4], h_na), q_norm), _rms_norm(_split_heads(p[5], h_na), k_norm),
                _split_heads(p[6], h_na))

    nqx, nkx, nvx = na_prep(xp)
    nqc, nkc, nvc = na_prep(cp)
    row_idx, col_in, dr, dc = na_win
    bias = rpb[:, dr, dc].astype(F32)
    b_x = lax.map(lambda a: _na_latent(a[0], a[1], a[2], a[3], a[4], row_idx, col_in, bias),
                  (nqx, nkx, nvx, nkc, nvc))

    lg_f = _retention_log_decay(5.0, h_r)
    lg_b = _retention_log_decay(6.0, h_r)
    rqx = _axial_rope(_to_bhtd(_split_heads(xp[7], h_r)), *rope)
    rkx = _axial_rope(_to_bhtd(_split_heads(xp[8], h_r)), *rope) * HEAD_DIM ** -0.5
    rvx = _to_bhtd(_split_heads(xp[9], h_r))
    rqc = _to_bhtd(_split_heads(cp[7], h_r))
    rkc = _to_bhtd(_split_heads(cp[8], h_r)) * HEAD_DIM ** -0.5
    rvc = _to_bhtd(_split_heads(cp[9], h_r))
    zero_r = jnp.zeros((b_sz, h_r, HEAD_DIM, HEAD_DIM), F32)
    rc_f, rs_f = _retention_chunkwise(rqc, rkc, rvc, lg_f, zero_r)
    rc_b, rs_b = _retention_chunkwise(_flip_t(rqc), _flip_t(rkc), _flip_t(rvc), lg_b, zero_r)
    rx_f, _ = _retention_chunkwise(rqx, rkx, rvx, lg_f, rs_f)
    rx_b, _ = _retention_chunkwise(_flip_t(rqx), _flip_t(rkx), _flip_t(rvx), lg_b, rs_b)

    def ret_out(h, g):
        return _merge_heads(_rms_norm(h, r_norm)).astype(g.dtype) * jax.nn.silu(g)

    c_x = ret_out(rx_f + _flip_t(rx_b), xp[10])
    mix_x = jnp.concatenate([a_x, b_x, c_x], axis=-1)
    a_c = mlstm_out(hc_f + _flip_t(hc_b), cp[2])
    b_c = _ctx_attention(nqc, nkc, nvc)
    c_c = ret_out(rc_f + _flip_t(rc_b), cp[10])
    return mix_x, jnp.concatenate([a_c, b_c, c_c], axis=-1)


def kernel(x, c, ctx, c_ctx, w_mod, b_mod, norm_mix, norm_ffn, w_in, w_out, mlstm_conv_w, mlstm_conv_b,
           mlstm_gate_b, mlstm_norm, na_q_norm, na_k_norm, na_rpb, ret_norm, router_w, router_b,
           expert_w_up, expert_b_up, expert_w_down, expert_b_down):
    b_sz, t_len, d = x.shape
    t_ctx = ctx.shape[1]
    depth = w_in.shape[0]
    n_e = router_w.shape[2]
    d_mix = w_out.shape[1]
    h_m = d_mix // (4 * HEAD_DIM)
    h_na = d_mix // (2 * HEAD_DIM)
    h_r = d_mix // (4 * HEAD_DIM)
    d_m, d_na, d_r = h_m * HEAD_DIM, h_na * HEAD_DIM, h_r * HEAD_DIM
    n_gate = 4 * h_m
    assert b_sz + 1 <= MOD_ROWS and n_e <= LANES and n_gate <= LANES
    s_len = t_ctx + t_len
    assert s_len % ROW_TILE == 0

    s = jnp.concatenate([ctx, x], axis=1)
    cc = jnp.zeros((MOD_ROWS, d), F32).at[:b_sz].set(c).at[b_sz].set(c_ctx)
    mods = _modulation(cc, w_mod, b_mod).reshape(depth, MOD_ROWS, 1, 6 * d)

    g0 = 4 * d_m
    col_na = g0
    col_ret = col_na + 3 * d_na
    n_hp = h_m // 2
    gate_src = np.array([[g0 + t * h_m + 2 * hp + h2 for t in range(N_GATE_TYPES) for h2 in range(2)]
                         for hp in range(n_hp)])
    rope_tabs = _rope_tables(t_ctx, t_len)
    ret_tabs = _ret_tables(h_r)
    na_tabs = _na_bias_tables(na_rpb)
    is_ctx = (jnp.arange(s_len) < t_ctx)[None, :, None]

    for l in range(depth):
        w_main = jnp.concatenate([w_in[l, :, :g0], w_in[l, :, g0 + n_gate:]], axis=1).astype(BF16)
        w_gate = jnp.zeros((d, n_hp, LANES), BF16).at[:, :, :gate_src.shape[1]].set(
            w_in[l][:, gate_src].astype(BF16)).reshape(d, n_hp * LANES)
        p_main, p_gate = _proj_in(s, mods[l], norm_mix[l], w_main, w_gate, t_ctx)

        mix = [_mlstm(p_main, p_gate, mlstm_conv_w[l], mlstm_conv_b[l], mlstm_gate_b[l], mlstm_norm[l],
                      t_ctx, h_m),
               _na_attention(p_main, na_q_norm[l], na_k_norm[l], na_tabs[l], t_ctx, col_na, h_na),
               _retention(p_main, rope_tabs, ret_tabs, ret_norm[l], t_ctx, col_ret, h_r)]

        rw = jnp.zeros((d, LANES), F32).at[:, :n_e].set(router_w[l])
        rb = jnp.zeros((1, LANES), F32).at[0, :n_e].set(router_b[l])
        s, tok, logits = _proj_out(mix, s, mods[l], norm_ffn[l], w_out[l].astype(BF16), rw, rb, t_ctx)

        w_up = expert_w_up[l]
        y = _moe(tok.reshape(b_sz * s_len, d), logits.reshape(b_sz * s_len, LANES)[:, :n_e], n_e,
                 w_up[:, :, 0::2].astype(BF16), w_up[:, :, 1::2].astype(BF16), expert_w_down[l].astype(BF16),
                 expert_b_up[l][:, None, 0::2], expert_b_up[l][:, None, 1::2], expert_b_down[l][:, None, :])
        g2 = jnp.where(is_ctx, mods[l, b_sz, 0, 5 * d:][None, None, :], mods[l, :b_sz, :, 5 * d:])
        s = s + g2 * y.reshape(b_sz, s_len, d)
    return s[:, t_ctx:]
```

```python
import functools

import jax
import jax.numpy as jnp
import numpy as np
from jax import lax
from jax.experimental import pallas as pl
from jax.experimental.pallas import tpu as pltpu

F32 = jnp.float32
BF16 = jnp.bfloat16

GRID_W = 64
HEAD_DIM = 64
MLSTM_CHUNK = 64
RET_CHUNK = 64
NA_ROWS = 8
NA_COLS = 16
ROPE_BASE = 10000.0
TOP_K = 4
SWIGLU_ALPHA = 1.702
SWIGLU_LIMIT = 7.0
NORM_EPS = 1e-6
NEG_INF = -1e30

LANES = 128
VMEM_LIMIT = 48 * 1024 * 1024
MOD_ROWS = 8
ROW_TILE = 768
MOE_TILE = 512


def _cparams(sem):
    return pltpu.CompilerParams(dimension_semantics=sem, vmem_limit_bytes=VMEM_LIMIT)


def _mod_kernel(cc_ref, w_ref, b_ref, o_ref):
    cc = cc_ref[...]
    a = cc * jax.nn.sigmoid(cc)
    o_ref[0] = jnp.dot(a, w_ref[0], precision=lax.Precision.HIGHEST,
                       preferred_element_type=F32) + b_ref[0]


def _modulation(cc, w_mod, b_mod):
    n_l, d, d6 = w_mod.shape
    tn = d6 // 4
    return pl.pallas_call(
        _mod_kernel,
        grid=(n_l, d6 // tn),
        in_specs=[pl.BlockSpec((MOD_ROWS, d), lambda l, j: (0, 0)),
                  pl.BlockSpec((1, d, tn), lambda l, j: (l, 0, j)),
                  pl.BlockSpec((1, 1, tn), lambda l, j: (l, 0, j))],
        out_specs=pl.BlockSpec((1, MOD_ROWS, tn), lambda l, j: (l, 0, j)),
        out_shape=jax.ShapeDtypeStruct((n_l, MOD_ROWS, d6), F32),
        compiler_params=_cparams(("arbitrary", "arbitrary")),
        name="adaln_modulation",
    )(cc, w_mod, b_mod.reshape(n_l, 1, d6))


def _pick_mod(mb_ref, mc_ref, k, d, is_ctx):
    vb = mb_ref[0, :, k * d:(k + 1) * d]
    vc = mc_ref[0, :, k * d:(k + 1) * d]
    return jnp.where(is_ctx, vc, vb)


def _rms(x, g):
    return x * lax.rsqrt(jnp.mean(x * x, axis=-1, keepdims=True) + NORM_EPS) * g


def _proj_in_kernel(s_ref, mb_ref, mc_ref, g_ref, w_ref, wg_ref, pm_ref, pg_ref, xn_ref, *, t_ctx):
    i = pl.program_id(1)
    j = pl.program_id(2)
    tm, d = xn_ref.shape

    @pl.when(j == 0)
    def _():
        row = lax.broadcasted_iota(jnp.int32, (tm, 1), 0) + i * tm
        is_ctx = row < t_ctx
        sh = _pick_mod(mb_ref, mc_ref, 0, d, is_ctx)
        sc = _pick_mod(mb_ref, mc_ref, 1, d, is_ctx)
        h = _rms(s_ref[0], g_ref[...]) * (1.0 + sc) + sh
        xn_ref[...] = h.astype(BF16)
        pg_ref[0] = jnp.dot(xn_ref[...], wg_ref[...], preferred_element_type=F32)

    pm_ref[0] = jnp.dot(xn_ref[...], w_ref[...], preferred_element_type=F32)


def _proj_in(s, mod_l, g, w_main, w_gate, t_ctx):
    b_sz, s_len, d = s.shape
    n_main = w_main.shape[1]
    n_gate = w_gate.shape[1]
    tn = n_main // 4
    tm = ROW_TILE
    return pl.pallas_call(
        functools.partial(_proj_in_kernel, t_ctx=t_ctx),
        grid=(b_sz, s_len // tm, n_main // tn),
        in_specs=[pl.BlockSpec((1, tm, d), lambda b, i, j: (b, i, 0)),
                  pl.BlockSpec((1, 1, 6 * d), lambda b, i, j: (b, 0, 0)),
                  pl.BlockSpec((1, 1, 6 * d), lambda b, i, j: (b_sz, 0, 0)),
                  pl.BlockSpec((1, d), lambda b, i, j: (0, 0)),
                  pl.BlockSpec((d, tn), lambda b, i, j: (0, j)),
                  pl.BlockSpec((d, n_gate), lambda b, i, j: (0, 0))],
        out_specs=[pl.BlockSpec((1, tm, tn), lambda b, i, j: (b, i, j)),
                   pl.BlockSpec((1, tm, n_gate), lambda b, i, j: (b, i, 0))],
        out_shape=[jax.ShapeDtypeStruct((b_sz, s_len, n_main), F32),
                   jax.ShapeDtypeStruct((b_sz, s_len, n_gate), F32)],
        scratch_shapes=[pltpu.VMEM((tm, d), BF16)],
        compiler_params=_cparams(("arbitrary", "arbitrary", "arbitrary")),
        name="proj_in",
    )(s, mod_l, mod_l, g.reshape(1, d), w_main, w_gate)


def _proj_out_kernel(ma_ref, mb2_ref, mc2_ref, s_ref, mb_ref, mc_ref, g_ref, w_ref, rw_ref, rb_ref,
                     so_ref, tok_ref, lg_ref, *, t_ctx):
    i = pl.program_id(1)
    tm, d = s_ref.shape[1], s_ref.shape[2]
    row = lax.broadcasted_iota(jnp.int32, (tm, 1), 0) + i * tm
    is_ctx = row < t_ctx
    g1 = _pick_mod(mb_ref, mc_ref, 2, d, is_ctx)
    ka, kb = ma_ref.shape[2], ma_ref.shape[2] + mb2_ref.shape[2]
    y = (jnp.dot(ma_ref[0], w_ref[0:ka, :], preferred_element_type=F32)
         + jnp.dot(mb2_ref[0], w_ref[ka:kb, :], preferred_element_type=F32)
         + jnp.dot(mc2_ref[0], w_ref[kb:, :], preferred_element_type=F32))
    s_new = s_ref[0] + g1 * y
    so_ref[0] = s_new
    sh = _pick_mod(mb_ref, mc_ref, 3, d, is_ctx)
    sc = _pick_mod(mb_ref, mc_ref, 4, d, is_ctx)
    t = _rms(s_new, g_ref[...]) * (1.0 + sc) + sh
    tok_ref[0] = t.astype(BF16)
    lg_ref[0] = jnp.dot(t, rw_ref[...], precision=lax.Precision.HIGHEST,
                        preferred_element_type=F32) + rb_ref[...]


def _proj_out(mix_parts, s, mod_l, g, w_out, rw, rb, t_ctx):
    b_sz, s_len, d = s.shape
    tm = ROW_TILE
    row_spec = pl.BlockSpec((1, tm, d), lambda b, i: (b, i, 0))
    part_specs = [pl.BlockSpec((1, tm, m.shape[2]), lambda b, i: (b, i, 0)) for m in mix_parts]
    return pl.pallas_call(
        functools.partial(_proj_out_kernel, t_ctx=t_ctx),
        grid=(b_sz, s_len // tm),
        in_specs=part_specs + [row_spec,
                  pl.BlockSpec((1, 1, 6 * d), lambda b, i: (b, 0, 0)),
                  pl.BlockSpec((1, 1, 6 * d), lambda b, i: (b_sz, 0, 0)),
                  pl.BlockSpec((1, d), lambda b, i: (0, 0)),
                  pl.BlockSpec((d, d), lambda b, i: (0, 0)),
                  pl.BlockSpec((d, LANES), lambda b, i: (0, 0)),
                  pl.BlockSpec((1, LANES), lambda b, i: (0, 0))],
        out_specs=[row_spec, row_spec, pl.BlockSpec((1, tm, LANES), lambda b, i: (b, i, 0))],
        out_shape=[jax.ShapeDtypeStruct((b_sz, s_len, d), F32),
                   jax.ShapeDtypeStruct((b_sz, s_len, d), BF16),
                   jax.ShapeDtypeStruct((b_sz, s_len, LANES), F32)],
        compiler_params=_cparams(("arbitrary", "arbitrary")),
        name="proj_out_router",
    )(*mix_parts, s, mod_l, mod_l, g.reshape(1, d), w_out, rw, rb)


PAIR = 2 * LANES


def _regroup_kernel(w_ref, p_ref, o_ref):
    for j in range(w_ref.shape[2] // PAIR):
        x = w_ref[0, :, j * PAIR:(j + 1) * PAIR].astype(BF16)
        o_ref[0, :, j * PAIR:(j + 1) * PAIR] = jnp.dot(x, p_ref[...], preferred_element_type=F32).astype(BF16)


def _regroup_up(w_up):
    n_g, d, f2 = w_up.shape
    td = min(512, d)
    assert d % td == 0 and f2 % PAIR == 0
    dst = np.arange(PAIR)
    src = np.where(dst < LANES, 2 * dst, 2 * (dst - LANES) + 1)
    perm = jnp.asarray(np.arange(PAIR)[:, None] == src[None, :], BF16)
    return pl.pallas_call(
        _regroup_kernel,
        grid=(n_g, d // td),
        in_specs=[pl.BlockSpec((1, td, f2), lambda g, i: (g, i, 0)),
                  pl.BlockSpec((PAIR, PAIR), lambda g, i: (0, 0))],
        out_specs=pl.BlockSpec((1, td, f2), lambda g, i: (g, i, 0)),
        out_shape=jax.ShapeDtypeStruct((n_g, d, f2), BF16),
        compiler_params=_cparams(("arbitrary", "arbitrary")),
        name="regroup_expert_up",
    )(w_up, perm)


def _regroup_bias(b_up):
    lead = b_up.shape[:-1]
    b = b_up.reshape(*lead, -1, LANES, 2)
    return jnp.swapaxes(b, -1, -2).reshape(*lead, -1)


def _moe_kernel(be_ref, nu_ref, x_ref, wu_ref, wd_ref, bu_ref, bd_ref, y_ref):
    i = pl.program_id(0)

    @pl.when(i < nu_ref[0])
    def _():
        up = jnp.dot(x_ref[...], wu_ref[0], preferred_element_type=F32) + bu_ref[0]
        acts = []
        for j in range(up.shape[1] // PAIR):
            glu = jnp.minimum(up[:, j * PAIR:j * PAIR + LANES], SWIGLU_LIMIT)
            lin = jnp.clip(up[:, j * PAIR + LANES:(j + 1) * PAIR], -SWIGLU_LIMIT, SWIGLU_LIMIT)
            acts.append((glu * jax.nn.sigmoid(SWIGLU_ALPHA * glu) * (lin + 1.0)).astype(BF16))
        act = jnp.concatenate(acts, axis=1)
        y_ref[...] = jnp.dot(act, wd_ref[0], preferred_element_type=F32) + bd_ref[0]

    @pl.when(i >= nu_ref[0])
    def _():
        y_ref[...] = jnp.zeros_like(y_ref)


def _moe_blocks(blk_e, n_used, x_sorted, w_up, w_down, b_up, b_down):
    n_rows, d = x_sorted.shape
    n_e, _, f2 = w_up.shape
    tm = MOE_TILE
    wmap = lambda i, be, nu: (be[i], 0, 0)
    return pl.pallas_call(
        _moe_kernel,
        grid_spec=pltpu.PrefetchScalarGridSpec(
            num_scalar_prefetch=2,
            grid=(n_rows // tm,),
            in_specs=[pl.BlockSpec((tm, d), lambda i, be, nu: (i, 0)),
                      pl.BlockSpec((1, d, f2), wmap),
                      pl.BlockSpec((1, f2 // 2, d), wmap),
                      pl.BlockSpec((1, 1, f2), wmap),
                      pl.BlockSpec((1, 1, d), wmap)],
            out_specs=pl.BlockSpec((tm, d), lambda i, be, nu: (i, 0))),
        out_shape=jax.ShapeDtypeStruct((n_rows, d), F32),
        compiler_params=_cparams(("arbitrary",)),
        name="moe_expert_blocks",
    )(blk_e, n_used, x_sorted, w_up, w_down, b_up, b_down)


def _combine_kernel(y_ref, gt_ref, s_ref, mb_ref, mc_ref, o_ref, *, t_ctx):
    i = pl.program_id(1)
    tm, d = s_ref.shape[1], s_ref.shape[2]
    row = lax.broadcasted_iota(jnp.int32, (tm, 1), 0) + i * tm
    g2 = _pick_mod(mb_ref, mc_ref, 5, d, row < t_ctx)
    gt = gt_ref[0]
    y = y_ref[0, 0] * gt[:, 0:1]
    for k in range(1, y_ref.shape[0]):
        y = y + y_ref[k, 0] * gt[:, k:k + 1]
    o_ref[0] = s_ref[0] + g2 * y


def _combine(y_as, gates, s, mod_l, t_ctx):
    b_sz, s_len, d = s.shape
    n_k = y_as.shape[0]
    tm = ROW_TILE // 3
    row_spec = pl.BlockSpec((1, tm, d), lambda b, i: (b, i, 0))
    return pl.pallas_call(
        functools.partial(_combine_kernel, t_ctx=t_ctx),
        grid=(b_sz, s_len // tm),
        in_specs=[pl.BlockSpec((n_k, 1, tm, d), lambda b, i: (0, b, i, 0)),
                  pl.BlockSpec((1, tm, n_k), lambda b, i: (b, i, 0)),
                  row_spec,
                  pl.BlockSpec((1, 1, 6 * d), lambda b, i: (b, 0, 0)),
                  pl.BlockSpec((1, 1, 6 * d), lambda b, i: (b_sz, 0, 0))],
        out_specs=row_spec,
        out_shape=jax.ShapeDtypeStruct((b_sz, s_len, d), F32),
        compiler_params=_cparams(("arbitrary", "arbitrary")),
        name="moe_combine",
    )(y_as, gates, s, mod_l, mod_l)


def _moe(tok, logits, s, mod_l, t_ctx, n_e, w_up, w_down, b_up, b_down):
    b_sz, s_len, d = tok.shape
    n_tok = b_sz * s_len
    tm = MOE_TILE
    top_v, top_e = lax.top_k(logits.reshape(n_tok, n_e), TOP_K)
    gates = jax.nn.softmax(top_v, axis=-1)
    n_as = n_tok * TOP_K
    onehot = jnp.sum((top_e[:, :, None] == jnp.arange(n_e)[None, None, :]).astype(jnp.int32), axis=1)
    csum = jnp.cumsum(onehot, axis=0)
    counts = csum[-1]
    rank = jnp.take_along_axis(csum - onehot, top_e, axis=1)
    padded = (counts + tm - 1) // tm * tm
    end_pad = jnp.cumsum(padded)
    start_pad = end_pad - padded
    start = jnp.cumsum(counts) - counts
    dest = (start_pad[top_e] + rank).astype(jnp.int32)
    n_blocks = -(-n_as // tm) + n_e
    blk_e = jnp.minimum(jnp.searchsorted(end_pad, jnp.arange(n_blocks) * tm, side='right'),
                        n_e - 1).astype(jnp.int32)
    n_used = (end_pad[-1:] // tm).astype(jnp.int32)
    tok_sorted = (jnp.argsort(top_e.reshape(n_as)) // TOP_K).astype(jnp.int32)
    row = jnp.arange(n_blocks * tm)
    row_e = jnp.repeat(blk_e, tm)
    j = row - start_pad[row_e]
    src = jnp.clip(start[row_e] + j, 0, n_as - 1)
    row_tok = jnp.where(j < counts[row_e], tok_sorted[src], 0)
    x_sorted = tok.reshape(n_tok, d)[row_tok]
    y = _moe_blocks(blk_e, n_used, x_sorted, w_up, w_down, b_up, b_down)
    y_as = y[dest.T].reshape(TOP_K, b_sz, s_len, d)
    return _combine(y_as, gates.reshape(b_sz, s_len, TOP_K), s, mod_l, t_ctx)


NA_STEP = 256
NA_KEYS = NA_ROWS * GRID_W


def _head_rms(x, w, lo):
    xx = x * x
    s0 = jnp.sum(jnp.where(lo, xx, 0.0), axis=-1, keepdims=True)
    s1 = jnp.sum(jnp.where(lo, 0.0, xx), axis=-1, keepdims=True)
    inv = lax.rsqrt(jnp.where(lo, s0, s1) * (1.0 / HEAD_DIM) + NORM_EPS)
    return x * inv * w


def _dot_nt(a, b):
    return lax.dot_general(a, b, (((1,), (1,)), ((), ())), preferred_element_type=F32)


def _na_kernel(q_ref, k_ref, v_ref, qw_ref, kw_ref, bias_ref, o_ref, kn_ref, vb_ref, *, t_ctx, rows):
    rg = pl.program_id(2)
    s_len = k_ref.shape[1]
    lo = lax.broadcasted_iota(jnp.int32, (1, LANES), 1) < HEAD_DIM
    rows_per_step = NA_STEP // GRID_W

    @pl.when(rg == 0)
    def _():
        def prep(c, carry):
            t0 = pl.multiple_of(c * NA_STEP, NA_STEP)
            kn_ref[pl.ds(t0, NA_STEP), :] = _head_rms(k_ref[0, pl.ds(t0, NA_STEP), :], kw_ref[...], lo).astype(BF16)
            vb_ref[pl.ds(t0, NA_STEP), :] = v_ref[0, pl.ds(t0, NA_STEP), :].astype(BF16)
            return carry
        lax.fori_loop(0, s_len // NA_STEP, prep, 0)

    qn = _head_rms(q_ref[0], qw_ref[...], lo) * (HEAD_DIM ** -0.5)
    q0 = jnp.where(lo, qn, 0.0).astype(BF16)
    q1 = jnp.where(lo, 0.0, qn).astype(BF16)
    kc = kn_ref[0:t_ctx, :]
    vc = vb_ref[0:t_ctx, :]

    def finish(g, o2, l):
        o2 = o2 * (1.0 / l)
        o = jnp.where(lo, o2[:GRID_W], o2[GRID_W:])
        o_ref[0, g * GRID_W:(g + 1) * GRID_W, :] = o.astype(o_ref.dtype)

    @pl.when(rg == 0)
    def _():
        for g in range(rows_per_step):
            q2 = jnp.concatenate([q0[g * GRID_W:(g + 1) * GRID_W], q1[g * GRID_W:(g + 1) * GRID_W]], axis=0)
            s_c = _dot_nt(q2, kc)
            m = jnp.max(s_c, axis=-1, keepdims=True)
            p_c = jnp.exp(s_c - m)
            l = jnp.sum(p_c, axis=-1, keepdims=True)
            finish(g, jnp.dot(p_c.astype(BF16), vc, preferred_element_type=F32), l)

    @pl.when(rg > 0)
    def _():
        for g in range(rows_per_step):
            r = (rg - 1) * rows_per_step + g
            row_start = jnp.clip(r - NA_ROWS // 2, 0, rows - NA_ROWS)
            dr0 = row_start - r + NA_ROWS - 1
            t0 = pl.multiple_of(t_ctx + row_start * GRID_W, GRID_W)
            kw = kn_ref[pl.ds(t0, NA_KEYS), :]
            vw = vb_ref[pl.ds(t0, NA_KEYS), :]
            q2 = jnp.concatenate([q0[g * GRID_W:(g + 1) * GRID_W], q1[g * GRID_W:(g + 1) * GRID_W]], axis=0)
            s_l = _dot_nt(q2, kw) + bias_ref[dr0, 0]
            s_c = _dot_nt(q2, kc)
            m = jnp.maximum(jnp.max(s_l, axis=-1, keepdims=True), jnp.max(s_c, axis=-1, keepdims=True))
            p_l = jnp.exp(s_l - m)
            p_c = jnp.exp(s_c - m)
            l = jnp.sum(p_l, axis=-1, keepdims=True) + jnp.sum(p_c, axis=-1, keepdims=True)
            o2 = (jnp.dot(p_l.astype(BF16), vw, preferred_element_type=F32)
                  + jnp.dot(p_c.astype(BF16), vc, preferred_element_type=F32))
            finish(g, o2, l)


def _na_bias_tables(rpb):
    n_l, n_h, n_dr, n_dc = rpb.shape
    col = np.arange(GRID_W)
    dc = np.clip(col[None, :] - col[:, None] + NA_COLS - 1, 0, n_dc - 1)
    onehot = (dc.reshape(1, -1) == np.arange(n_dc)[:, None]).astype(np.float32)
    toe = jnp.dot(rpb.reshape(-1, n_dc), jnp.asarray(onehot), precision=lax.Precision.HIGHEST)
    toe = toe.reshape(n_l, n_h, n_dr, GRID_W, GRID_W)
    col_start = np.clip(col - NA_COLS // 2, 0, GRID_W - NA_COLS)
    col_in = (col[None, :] >= col_start[:, None]) & (col[None, :] < col_start[:, None] + NA_COLS)
    toe = jnp.where(jnp.asarray(col_in), toe, NEG_INF)
    tabs = []
    for dr0 in range(NA_ROWS):
        t = toe[:, :, dr0:dr0 + NA_ROWS].reshape(n_l, n_h // 2, 2, NA_ROWS, GRID_W, GRID_W)
        t = jnp.transpose(t, (0, 1, 2, 4, 3, 5))
        tabs.append(t.reshape(n_l, n_h // 2, 2 * GRID_W, NA_KEYS))
    return jnp.stack(tabs, axis=1)


def _na_attention(p_main, q_w, k_w, bias_tab, t_ctx, col_q, n_heads):
    b_sz, s_len, _ = p_main.shape
    n_hg = n_heads // 2
    rows = (s_len - t_ctx) // GRID_W
    assert t_ctx == NA_STEP and s_len % NA_STEP == 0 and rows >= NA_ROWS
    cq, ck, cv = col_q // LANES, col_q // LANES + n_hg, col_q // LANES + 2 * n_hg
    w2 = lambda w: jnp.concatenate([w, w]).reshape(1, LANES)
    return pl.pallas_call(
        functools.partial(_na_kernel, t_ctx=t_ctx, rows=rows),
        grid=(b_sz, n_hg, s_len // NA_STEP),
        in_specs=[pl.BlockSpec((1, NA_STEP, LANES), lambda b, h, r: (b, r, cq + h)),
                  pl.BlockSpec((1, s_len, LANES), lambda b, h, r: (b, 0, ck + h)),
                  pl.BlockSpec((1, s_len, LANES), lambda b, h, r: (b, 0, cv + h)),
                  pl.BlockSpec((1, LANES), lambda b, h, r: (0, 0)),
                  pl.BlockSpec((1, LANES), lambda b, h, r: (0, 0)),
                  pl.BlockSpec((NA_ROWS, 1, 2 * GRID_W, NA_KEYS), lambda b, h, r: (0, h, 0, 0))],
        out_specs=pl.BlockSpec((1, NA_STEP, LANES), lambda b, h, r: (b, r, h)),
        out_shape=jax.ShapeDtypeStruct((b_sz, s_len, n_heads * HEAD_DIM), BF16),
        scratch_shapes=[pltpu.VMEM((s_len, LANES), BF16), pltpu.VMEM((s_len, LANES), BF16)],
        compiler_params=_cparams(("arbitrary", "arbitrary", "arbitrary")),
        name="na_attention",
    )(p_main, p_main, p_main, w2(q_w), w2(k_w), bias_tab)


MIX_CHUNK = 256


def _dot_tn(a, b):
    return lax.dot_general(a, b, (((0,), (0,)), ((), ())), preferred_element_type=F32)


def _rope_tables(t_ctx, t_len):
    pos = jnp.arange(t_len)
    row = (pos // GRID_W).astype(F32)
    col = (pos % GRID_W).astype(F32)
    n = HEAD_DIM // 4
    inv = ROPE_BASE ** (-jnp.arange(n, dtype=F32) / n)
    ar = row[:, None] * inv
    ac = col[:, None] * inv
    cos = jnp.concatenate([jnp.cos(ar), jnp.cos(ar), jnp.cos(ac), jnp.cos(ac)], axis=-1)
    sin = jnp.concatenate([-jnp.sin(ar), jnp.sin(ar), -jnp.sin(ac), jnp.sin(ac)], axis=-1)
    cos = jnp.concatenate([jnp.ones((t_ctx, HEAD_DIM), F32), cos], axis=0)
    sin = jnp.concatenate([jnp.zeros((t_ctx, HEAD_DIM), F32), sin], axis=0)
    return jnp.tile(cos, (1, 2)), jnp.tile(sin, (1, 2))


def _ret_tables(n_heads):
    L = MIX_CHUNK
    pos = np.arange(L, dtype=np.float32)
    lane_head = np.arange(LANES) // HEAD_DIM
    decay = np.zeros((2, n_heads, L, L), np.float32)
    zeta = np.zeros((2, n_heads // 2, L, LANES), np.float32)
    xi = np.zeros((2, n_heads // 2, L, LANES), np.float32)
    gch = np.zeros((2, n_heads // 2, 1, LANES), np.float32)
    for d, first_exp in enumerate((5.0, 6.0)):
        e = np.float32(first_exp) + np.float32(2.0) * np.arange(n_heads, dtype=np.float32)
        lg = np.log1p(-np.exp2(-e)).astype(np.float32)
        diff = pos[:, None] - pos[None, :]
        if d == 1:
            diff = -diff
        for h in range(n_heads):
            decay[d, h] = np.where(diff >= 0, np.exp(lg[h] * np.maximum(diff, 0.0)), 0.0)
        for hp in range(n_heads // 2):
            lgl = lg[2 * hp + lane_head][None, :]
            to_end = (L - 1 - pos if d == 0 else pos)[:, None]
            zeta[d, hp] = np.exp(lgl * to_end)
            xi[d, hp] = np.exp(lgl * (L - to_end))
            gch[d, hp] = np.exp(lgl * L)
    return tuple(jnp.asarray(a) for a in (decay, zeta, xi, gch))


def _ret_kernel(q_ref, k_ref, v_ref, g_ref, cos_ref, sin_ref, dec_ref, zeta_ref, xi_ref, gch_ref, rn_ref,
                o_ref, qr_ref, kr_ref, vb_ref, acc_ref):
    L = MIX_CHUNK
    s_len = q_ref.shape[1]
    nc = s_len // L
    lane = lax.broadcasted_iota(jnp.int32, (1, LANES), 1)
    lo = lane < HEAD_DIM
    half = (lane & (HEAD_DIM // 4)) == 0
    rid = lax.broadcasted_iota(jnp.int32, (LANES, LANES), 0) < HEAD_DIM
    cid = lax.broadcasted_iota(jnp.int32, (LANES, LANES), 1) < HEAD_DIM
    same_head = rid == cid

    def rope(x, cos, sin):
        up = pltpu.roll(x, LANES - HEAD_DIM // 4, axis=1)
        dn = pltpu.roll(x, HEAD_DIM // 4, axis=1)
        return x * cos + jnp.where(half, up, dn) * sin

    def prep(c, carry):
        sl = pl.ds(pl.multiple_of(c * L, L), L)
        cos, sin = cos_ref[sl, :], sin_ref[sl, :]
        qr_ref[sl, :] = rope(q_ref[0, sl, :], cos, sin).astype(BF16)
        kr_ref[sl, :] = (rope(k_ref[0, sl, :], cos, sin) * HEAD_DIM ** -0.5).astype(BF16)
        vb_ref[sl, :] = v_ref[0, sl, :].astype(BF16)
        return carry
    lax.fori_loop(0, nc, prep, 0)

    def chunk(d, c, state):
        sl = pl.ds(pl.multiple_of(c * L, L), L)
        q, k, v = qr_ref[sl, :], kr_ref[sl, :], vb_ref[sl, :]
        inter = jnp.dot(q, state.astype(BF16), preferred_element_type=F32) * xi_ref[d, 0]
        outs = []
        for h2 in range(2):
            qm = jnp.where(lo if h2 == 0 else jnp.logical_not(lo), q, jnp.zeros_like(q))
            sd = (_dot_nt(qm, k) * dec_ref[d, h2]).astype(BF16)
            outs.append(jnp.dot(sd, v, preferred_element_type=F32))
        y = jnp.where(lo, outs[0], outs[1]) + inter
        kz = (k.astype(F32) * zeta_ref[d, 0]).astype(BF16)
        state = state * gch_ref[d, 0] + jnp.where(same_head, _dot_tn(kz, v), 0.0)
        return sl, y, state

    def fwd(c, state):
        sl, y, state = chunk(0, c, state)
        acc_ref[sl, :] = y
        return state
    lax.fori_loop(0, nc, fwd, jnp.zeros((LANES, LANES), F32))

    def bwd(i, state):
        c = jnp.where(i == 0, 0, nc - i)
        sl, y, state = chunk(1, c, state)
        y = _head_rms(y + acc_ref[sl, :], rn_ref[...], lo)
        g = g_ref[0, sl, :]
        o_ref[0, sl, :] = (y * (g * jax.nn.sigmoid(g))).astype(o_ref.dtype)
        return state
    lax.fori_loop(0, nc, bwd, jnp.zeros((LANES, LANES), F32))


def _seq_spec(s_len, col):
    return pl.BlockSpec((1, s_len, LANES), lambda b, h: (b, 0, col + h), pipeline_mode=pl.Buffered(1))


def _retention(p_main, rope_tabs, ret_tabs, r_w, t_ctx, col_q, n_heads):
    b_sz, s_len, _ = p_main.shape
    n_hp = n_heads // 2
    L = MIX_CHUNK
    assert t_ctx % L == 0 and s_len % L == 0
    c0 = col_q // LANES
    cos, sin = rope_tabs
    decay, zeta, xi, gch = ret_tabs
    const2 = pl.BlockSpec((s_len, LANES), lambda b, h: (0, 0), pipeline_mode=pl.Buffered(1))
    return pl.pallas_call(
        _ret_kernel,
        grid=(b_sz, n_hp),
        in_specs=[_seq_spec(s_len, c0), _seq_spec(s_len, c0 + n_hp), _seq_spec(s_len, c0 + 2 * n_hp),
                  _seq_spec(s_len, c0 + 3 * n_hp), const2, const2,
                  pl.BlockSpec((2, 2, L, L), lambda b, h: (0, h, 0, 0)),
                  pl.BlockSpec((2, 1, L, LANES), lambda b, h: (0, h, 0, 0)),
                  pl.BlockSpec((2, 1, L, LANES), lambda b, h: (0, h, 0, 0)),
                  pl.BlockSpec((2, 1, 1, LANES), lambda b, h: (0, h, 0, 0)),
                  pl.BlockSpec((1, LANES), lambda b, h: (0, 0))],
        out_specs=pl.BlockSpec((1, s_len, LANES), lambda b, h: (b, 0, h)),
        out_shape=jax.ShapeDtypeStruct((b_sz, s_len, n_heads * HEAD_DIM), BF16),
        scratch_shapes=[pltpu.VMEM((s_len, LANES), BF16), pltpu.VMEM((s_len, LANES), BF16),
                        pltpu.VMEM((s_len, LANES), BF16), pltpu.VMEM((s_len, LANES), F32)],
        compiler_params=_cparams(("arbitrary", "arbitrary")),
        name="retention",
    )(p_main, p_main, p_main, p_main, cos, sin, decay, zeta, xi, gch,
      jnp.concatenate([r_w, r_w]).reshape(1, LANES))


N_GATE_TYPES = 4


def _log_sigmoid(x):
    return jnp.minimum(x, 0.0) - jnp.log1p(jnp.exp(-jnp.abs(x)))


def _mlstm_kernel(q_ref, k_ref, v_ref, og_ref, gc_ref, gr_ref, wq_ref, wk_ref, bq_ref, bk_ref, gbc_ref, gbr_ref,
                  mn_ref, o_ref, qc_ref, kc_ref, vb_ref, acc_ref, *, t_ctx):
    L = MIX_CHUNK
    s_len = q_ref.shape[1]
    nc = s_len // L
    lane = lax.broadcasted_iota(jnp.int32, (1, LANES), 1)
    lo = lane < HEAD_DIM
    head_lanes = (lo, jnp.logical_not(lo))
    rid = lax.broadcasted_iota(jnp.int32, (LANES, LANES), 0) < HEAD_DIM
    cid = lax.broadcasted_iota(jnp.int32, (LANES, LANES), 1) < HEAD_DIM
    head_block = (rid & cid, jnp.logical_not(rid | cid))
    row_i = lax.broadcasted_iota(jnp.int32, (L, L), 0)
    col_i = lax.broadcasted_iota(jnp.int32, (L, L), 1)
    causal = (row_i >= col_i, row_i <= col_i)
    tri = tuple(c.astype(F32) for c in causal)
    tri_t = (tri[1], tri[0])
    sub = lax.broadcasted_iota(jnp.int32, (L, 1), 0)

    def conv(x_ref, w_ref, b_ref, t0):
        x = x_ref[0, pl.ds(t0, L), :]
        prev = x_ref[0, pl.ds(jnp.maximum(t0 - 8, 0), 8), :][7:8]
        nxt = x_ref[0, pl.ds(jnp.minimum(t0 + L, s_len - 8), 8), :][0:1]
        prev = jnp.where((t0 != 0) & (t0 != t_ctx), prev, 0.0)
        nxt = jnp.where((t0 + L != t_ctx) & (t0 + L != s_len), nxt, 0.0)
        xm = jnp.where(sub == 0, prev, pltpu.roll(x, 1, axis=0))
        xp = jnp.where(sub == L - 1, nxt, pltpu.roll(x, L - 1, axis=0))
        y = b_ref[...] + xm * w_ref[0:1, :] + x * w_ref[1:2, :] + xp * w_ref[2:3, :]
        return y * jax.nn.sigmoid(y)

    def prep(c, carry):
        t0 = pl.multiple_of(c * L, L)
        sl = pl.ds(t0, L)
        qc_ref[sl, :] = conv(q_ref, wq_ref, bq_ref, t0).astype(BF16)
        kc_ref[sl, :] = (conv(k_ref, wk_ref, bk_ref, t0) * HEAD_DIM ** -0.5).astype(BF16)
        vb_ref[sl, :] = v_ref[0, sl, :].astype(BF16)
        return carry
    lax.fori_loop(0, nc, prep, 0)

    def chunk(d, c, state):
        sl = pl.ds(pl.multiple_of(c * L, L), L)
        q, k, v = qc_ref[sl, :], kc_ref[sl, :], vb_ref[sl, :]
        g_col = gc_ref[0, sl, :] + gbc_ref[0]
        g_row = gr_ref[0, 0, :, sl] + gbr_ref[0]
        cum_col = jnp.dot(tri[d], _log_sigmoid(g_col), precision=lax.Precision.HIGHEST,
                          preferred_element_type=F32)
        cum_row = jnp.dot(_log_sigmoid(g_row), tri_t[d], precision=lax.Precision.HIGHEST,
                          preferred_element_type=F32)
        end = L - 1 if d == 0 else 0
        outs, new_state = [], []
        for h2 in range(2):
            c_st, n_st, m_st = state[h2]
            ci, cf = 2 * (2 * d) + h2, 2 * (2 * d + 1) + h2
            i_col, a_col = g_col[:, ci:ci + 1], cum_col[:, cf:cf + 1]
            i_row, a_row = g_row[ci:ci + 1, :], cum_row[cf:cf + 1, :]
            b_tot = a_row[:, end:end + 1]
            d_log = jnp.where(causal[d], a_col + (i_row - a_row), NEG_INF)
            m_intra = jnp.max(d_log, axis=-1, keepdims=True)
            qm = jnp.where(head_lanes[h2], q, jnp.zeros_like(q))
            s = _dot_nt(qm, k) * jnp.exp(d_log - m_intra)
            num_intra = jnp.dot(s.astype(BF16), v, preferred_element_type=F32)
            den_intra = jnp.sum(s, axis=-1, keepdims=True)
            inter_log = a_col + m_st
            m_q = jnp.maximum(m_intra, inter_log)
            a = jnp.exp(inter_log - m_q)
            g = jnp.exp(m_intra - m_q)
            num = a * jnp.dot(qm, c_st.astype(BF16), preferred_element_type=F32) + g * num_intra
            den = a * jnp.sum(qm.astype(F32) * n_st, axis=-1, keepdims=True) + g * den_intra
            outs.append(num / jnp.maximum(jnp.abs(den), jnp.exp(-m_q)))
            w_log = b_tot - a_col + i_col
            m_loc = jnp.max(w_log, axis=0, keepdims=True)
            ke = jnp.where(head_lanes[h2], k.astype(F32) * jnp.exp(w_log - m_loc), 0.0)
            c_loc = jnp.where(head_block[h2], _dot_tn(ke.astype(BF16), v), 0.0)
            n_loc = jnp.sum(ke, axis=0, keepdims=True)
            m_new = jnp.maximum(b_tot + m_st, m_loc)
            a_s = jnp.exp(b_tot + m_st - m_new)
            g_s = jnp.exp(m_loc - m_new)
            new_state.append((a_s * c_st + g_s * c_loc, a_s * n_st + g_s * n_loc, m_new))
        return sl, jnp.where(lo, outs[0], outs[1]), tuple(new_state)

    zero = tuple((jnp.zeros((LANES, LANES), F32), jnp.zeros((1, LANES), F32), jnp.zeros((1, 1), F32))
                 for _ in range(2))

    def fwd(c, state):
        sl, y, state = chunk(0, c, state)
        acc_ref[sl, :] = y
        return state
    lax.fori_loop(0, nc, fwd, zero)

    def bwd(i, state):
        c = jnp.where(i == 0, t_ctx // L - 1, nc - i + (t_ctx // L - 1))
        sl, y, state = chunk(1, c, state)
        y = _head_rms(y + acc_ref[sl, :], mn_ref[...], lo)
        o_ref[0, sl, :] = (y * jax.nn.sigmoid(og_ref[0, sl, :])).astype(o_ref.dtype)
        return state
    lax.fori_loop(0, nc, bwd, zero)


def _mlstm(p_main, p_gate, conv_w, conv_b, gate_b, m_w, t_ctx, n_heads):
    b_sz, s_len, _ = p_main.shape
    n_hp = n_heads // 2
    L = MIX_CHUNK
    assert t_ctx == L and s_len % L == 0
    n_g = 2 * N_GATE_TYPES
    g_rows = jnp.transpose(p_gate.reshape(b_sz, s_len, n_hp, LANES)[..., :n_g], (0, 2, 3, 1))
    gb = jnp.transpose(gate_b.reshape(N_GATE_TYPES, n_hp, 2), (1, 0, 2)).reshape(n_hp, n_g)
    gb_col = jnp.zeros((n_hp, 1, LANES), F32).at[:, 0, :n_g].set(gb)
    gb_row = gb.reshape(n_hp, n_g, 1)
    vec = lambda col: pl.BlockSpec((1, LANES), lambda b, h: (0, col + h))
    return pl.pallas_call(
        functools.partial(_mlstm_kernel, t_ctx=t_ctx),
        grid=(b_sz, n_hp),
        in_specs=[_seq_spec(s_len, 0), _seq_spec(s_len, n_hp), _seq_spec(s_len, 2 * n_hp),
                  _seq_spec(s_len, 3 * n_hp),
                  pl.BlockSpec((1, s_len, LANES), lambda b, h: (b, 0, h), pipeline_mode=pl.Buffered(1)),
                  pl.BlockSpec((1, 1, n_g, s_len), lambda b, h: (b, h, 0, 0)),
                  pl.BlockSpec((3, LANES), lambda b, h: (0, h)),
                  pl.BlockSpec((3, LANES), lambda b, h: (0, n_hp + h)),
                  vec(0), vec(n_hp),
                  pl.BlockSpec((1, 1, LANES), lambda b, h: (h, 0, 0)),
                  pl.BlockSpec((1, n_g, 1), lambda b, h: (h, 0, 0)),
                  pl.BlockSpec((1, LANES), lambda b, h: (0, 0))],
        out_specs=pl.BlockSpec((1, s_len, LANES), lambda b, h: (b, 0, h)),
        out_shape=jax.ShapeDtypeStruct((b_sz, s_len, n_heads * HEAD_DIM), BF16),
        scratch_shapes=[pltpu.VMEM((s_len, LANES), BF16), pltpu.VMEM((s_len, LANES), BF16),
                        pltpu.VMEM((s_len, LANES), BF16), pltpu.VMEM((s_len, LANES), F32)],
        compiler_params=_cparams(("arbitrary", "arbitrary")),
        name="mlstm",
    )(p_main, p_main, p_main, p_main, p_gate, g_rows, conv_w, conv_w, conv_b.reshape(1, -1),
      conv_b.reshape(1, -1), gb_col, gb_row, jnp.concatenate([m_w, m_w]).reshape(1, LANES))


def _rms_norm(x, g):
    xf = x.astype(F32)
    y = xf * lax.rsqrt(jnp.mean(xf * xf, axis=-1, keepdims=True) + NORM_EPS)
    return (y * g.astype(F32)).astype(x.dtype)


def _split_heads(t, n_heads):
    return t.reshape(t.shape[0], t.shape[1], n_heads, HEAD_DIM)


def _to_bhtd(t):
    return jnp.transpose(t, (0, 2, 1, 3))


def _merge_heads(t):
    b, h, t_len, d = t.shape
    return jnp.transpose(t, (0, 2, 1, 3)).reshape(b, t_len, h * d)


def _flip_t(t):
    return jnp.flip(t, axis=2)


def _dwconv_centred(x, w, b):
    k_size = w.shape[0]
    pad = k_size // 2
    t_len = x.shape[1]
    xp = jnp.pad(x, ((0, 0), (pad, pad), (0, 0)))
    y = b
    for i in range(k_size):
        y = y + xp[:, i:i + t_len] * w[i]
    return y


def _axial_rope_tables(t_len):
    pos = jnp.arange(t_len)
    row = (pos // GRID_W).astype(F32)
    col = (pos % GRID_W).astype(F32)
    n = HEAD_DIM // 4
    inv = ROPE_BASE ** (-jnp.arange(n, dtype=F32) / n)
    ar = row[:, None] * inv
    ac = col[:, None] * inv
    return (jnp.cos(ar), jnp.sin(ar), jnp.cos(ac), jnp.sin(ac))


def _rotate_half(x, cos, sin):
    x1, x2 = jnp.split(x, 2, axis=-1)
    return jnp.concatenate([x1 * cos - x2 * sin, x2 * cos + x1 * sin], axis=-1)


def _axial_rope(x, cos_r, sin_r, cos_c, sin_c):
    xr, xc = jnp.split(x, 2, axis=-1)
    return jnp.concatenate([_rotate_half(xr, cos_r, sin_r), _rotate_half(xc, cos_c, sin_c)], axis=-1)


def _mlstm_chunkwise(q, k, v, log_i, log_f, state0):
    b_sz, h_sz, t_len, d = q.shape
    L = MLSTM_CHUNK
    nc = t_len // L
    qc = q.reshape(b_sz, h_sz, nc, L, d)
    kc = k.reshape(b_sz, h_sz, nc, L, d)
    vc = v.reshape(b_sz, h_sz, nc, L, d)
    li = log_i.reshape(b_sz, h_sz, nc, L)
    bcum = jnp.cumsum(log_f.reshape(b_sz, h_sz, nc, L), axis=-1)
    b_tot = bcum[..., -1]
    tri = jnp.tril(jnp.ones((L, L), dtype=bool))
    d_log = jnp.where(tri, bcum[..., :, None] - bcum[..., None, :] + li[..., None, :], NEG_INF)
    m_intra = jnp.max(d_log, axis=-1)
    s = jnp.einsum('bhcjd,bhcld->bhcjl', qc, kc).astype(F32) * jnp.exp(d_log - m_intra[..., None])
    num_intra = jnp.einsum('bhcjl,bhcld->bhcjd', s, vc)
    den_intra = jnp.sum(s, axis=-1)
    w_log = b_tot[..., None] - bcum + li
    m_loc = jnp.max(w_log, axis=-1)
    e = jnp.exp(w_log - m_loc[..., None])
    c_loc = jnp.einsum('bhcl,bhcld,bhcle->bhcde', e, kc, vc)
    n_loc = jnp.einsum('bhcl,bhcld->bhcd', e, kc)

    def step(carry, inp):
        c_st, n_st, m_st = carry
        cl, nl, ml, bt = inp
        m_new = jnp.maximum(bt + m_st, ml)
        a = jnp.exp(bt + m_st - m_new)
        g = jnp.exp(ml - m_new)
        new = (a[..., None, None] * c_st + g[..., None, None] * cl, a[..., None] * n_st + g[..., None] * nl, m_new)
        return new, carry

    xs = (jnp.moveaxis(c_loc, 2, 0), jnp.moveaxis(n_loc, 2, 0), jnp.moveaxis(m_loc, 2, 0), jnp.moveaxis(b_tot, 2, 0))
    final, prev = lax.scan(step, state0, xs)
    c_prev = jnp.moveaxis(prev[0], 0, 2)
    n_prev = jnp.moveaxis(prev[1], 0, 2)
    m_prev = jnp.moveaxis(prev[2], 0, 2)
    inter_log = bcum + m_prev[..., None]
    m_q = jnp.maximum(m_intra, inter_log)
    a = jnp.exp(inter_log - m_q)
    g = jnp.exp(m_intra - m_q)
    num = a[..., None] * jnp.einsum('bhcjd,bhcde->bhcje', qc, c_prev) + g[..., None] * num_intra
    den = a * jnp.einsum('bhcjd,bhcd->bhcj', qc, n_prev) + g * den_intra
    h = num / jnp.maximum(jnp.abs(den), jnp.exp(-m_q))[..., None]
    return h.reshape(b_sz, h_sz, t_len, d), final


def _retention_log_decay(first_exp, n_heads):
    e = first_exp + 2.0 * jnp.arange(n_heads, dtype=F32)
    return jnp.log1p(-jnp.exp2(-e))


def _retention_chunkwise(q, k, v, log_gamma, state0):
    b_sz, h_sz, t_len, d = q.shape
    L = RET_CHUNK
    nc = t_len // L
    qc = q.reshape(b_sz, h_sz, nc, L, d)
    kc = k.reshape(b_sz, h_sz, nc, L, d)
    vc = v.reshape(b_sz, h_sz, nc, L, d)
    pos = jnp.arange(L, dtype=F32)
    diff = pos[:, None] - pos[None, :]
    decay = jnp.where(diff >= 0, jnp.exp(log_gamma[:, None, None] * jnp.maximum(diff, 0.0)), 0.0)
    s = jnp.einsum('bhcjd,bhcld->bhcjl', qc, kc).astype(F32) * decay[:, None]
    intra = jnp.einsum('bhcjl,bhcld->bhcjd', s, vc)
    zeta = jnp.exp(log_gamma[:, None] * (L - 1 - pos))
    s_loc = jnp.einsum('hl,bhcld,bhcle->bhcde', zeta, kc, vc)
    g_chunk = jnp.exp(log_gamma * L)[:, None, None]

    def step(r, sl):
        return g_chunk * r + sl, r

    final, r_prev = lax.scan(step, state0, jnp.moveaxis(s_loc, 2, 0))
    r_prev = jnp.moveaxis(r_prev, 0, 2)
    xi = jnp.exp(log_gamma[:, None] * (pos + 1.0))
    inter = jnp.einsum('bhcjd,bhcde->bhcje', qc, r_prev) * xi[:, None, :, None]
    return (intra + inter).reshape(b_sz, h_sz, t_len, d), final


def _na_window(rows):
    wr = min(NA_ROWS, rows)
    r = jnp.arange(rows)
    col = jnp.arange(GRID_W)
    row_idx = jnp.clip(r - wr // 2, 0, rows - wr)[:, None] + jnp.arange(wr)[None, :]
    col_start = jnp.clip(col - NA_COLS // 2, 0, GRID_W - NA_COLS)
    col_in = (col[None, :] >= col_start[:, None]) & (col[None, :] < col_start[:, None] + NA_COLS)
    dr = row_idx - r[:, None] + NA_ROWS - 1
    dc = jnp.clip(col[None, :] - col[:, None] + NA_COLS - 1, 0, 2 * NA_COLS - 2)
    return (row_idx, col_in, dr[:, None, :, None], dc[None, :, None, :])


def _na_latent(q, k, v, k_ctx, v_ctx, row_idx, col_in, bias):
    t_len, h_sz, d = q.shape
    rows, wr = row_idx.shape
    scale = d ** -0.5
    qg = q.reshape(rows, GRID_W, h_sz, d)
    kb = k.reshape(rows, GRID_W, h_sz, d)[row_idx]
    vb = v.reshape(rows, GRID_W, h_sz, d)[row_idx]
    s_loc = jnp.einsum('rqhd,rjwhd->hrqjw', qg, kb).astype(F32) * scale + bias
    s_loc = jnp.where(col_in[:, None, :], s_loc, NEG_INF)
    s_ctx = jnp.einsum('rqhd,chd->hrqc', qg, k_ctx).astype(F32) * scale
    n_loc = wr * GRID_W
    s = jnp.concatenate([s_loc.reshape(h_sz, rows, GRID_W, n_loc), s_ctx], axis=-1)
    p = jax.nn.softmax(s, axis=-1).astype(v.dtype)
    p_loc = p[..., :n_loc].reshape(h_sz, rows, GRID_W, wr, GRID_W)
    o = jnp.einsum('hrqjw,rjwhd->rqhd', p_loc, vb) + jnp.einsum('hrqc,chd->rqhd', p[..., n_loc:], v_ctx)
    return o.reshape(t_len, h_sz * d)


def _ctx_attention(q, k, v):
    s = jnp.einsum('bqhd,bkhd->bhqk', q, k).astype(F32) * HEAD_DIM ** -0.5
    p = jax.nn.softmax(s, axis=-1).astype(v.dtype)
    o = jnp.einsum('bhqk,bkhd->bqhd', p, v)
    return o.reshape(o.shape[0], o.shape[1], -1)


def _hybrid_mixer(xp, cp, rope, na_win, conv_w, conv_b, gate_b, m_norm, q_norm, k_norm, rpb, r_norm, dims):
    h_m, h_na, h_r = dims
    b_sz = xp[0].shape[0]

    def mlstm_prep(p):
        qk = jax.nn.silu(_dwconv_centred(p[0], conv_w, conv_b))
        q, k = jnp.split(qk, 2, axis=-1)
        g = (p[3] + gate_b).astype(F32)
        g = jnp.transpose(g.reshape(b_sz, -1, 4, h_m), (2, 0, 3, 1))
        return (_to_bhtd(_split_heads(q, h_m)), _to_bhtd(_split_heads(k, h_m)) * HEAD_DIM ** -0.5,
                _to_bhtd(_split_heads(p[1], h_m)), g[0], jax.nn.log_sigmoid(g[1]), g[2], jax.nn.log_sigmoid(g[3]))

    qx, kx, vx, ix_f, lfx_f, ix_b, lfx_b = mlstm_prep(xp)
    qc, kc, vc, ic_f, lfc_f, ic_b, lfc_b = mlstm_prep(cp)
    zero_m = (jnp.zeros((b_sz, h_m, HEAD_DIM, HEAD_DIM), F32),
              jnp.zeros((b_sz, h_m, HEAD_DIM), F32), jnp.zeros((b_sz, h_m), F32))
    hc_f, st_f = _mlstm_chunkwise(qc, kc, vc, ic_f, lfc_f, zero_m)
    hc_b, st_b = _mlstm_chunkwise(_flip_t(qc), _flip_t(kc), _flip_t(vc), _flip_t(ic_b), _flip_t(lfc_b), zero_m)
    hx_f, _ = _mlstm_chunkwise(qx, kx, vx, ix_f, lfx_f, st_f)
    hx_b, _ = _mlstm_chunkwise(_flip_t(qx), _flip_t(kx), _flip_t(vx), _flip_t(ix_b), _flip_t(lfx_b), st_b)

    def mlstm_out(h, o):
        return _merge_heads(_rms_norm(h, m_norm)).astype(o.dtype) * jax.nn.sigmoid(o)

    a_x = mlstm_out(hx_f + _flip_t(hx_b), xp[2])

    def na_prep(p):
        return (_rms_norm(_split_heads(p[4], h_na), q_norm), _rms_norm(_split_heads(p[5], h_na), k_norm),
                _split_heads(p[6], h_na))

    nqx, nkx, nvx = na_prep(xp)
    nqc, nkc, nvc = na_prep(cp)
    row_idx, col_in, dr, dc = na_win
    bias = rpb[:, dr, dc].astype(F32)
    b_x = lax.map(lambda a: _na_latent(a[0], a[1], a[2], a[3], a[4], row_idx, col_in, bias),
                  (nqx, nkx, nvx, nkc, nvc))

    lg_f = _retention_log_decay(5.0, h_r)
    lg_b = _retention_log_decay(6.0, h_r)
    rqx = _axial_rope(_to_bhtd(_split_heads(xp[7], h_r)), *rope)
    rkx = _axial_rope(_to_bhtd(_split_heads(xp[8], h_r)), *rope) * HEAD_DIM ** -0.5
    rvx = _to_bhtd(_split_heads(xp[9], h_r))
    rqc = _to_bhtd(_split_heads(cp[7], h_r))
    rkc = _to_bhtd(_split_heads(cp[8], h_r)) * HEAD_DIM ** -0.5
    rvc = _to_bhtd(_split_heads(cp[9], h_r))
    zero_r = jnp.zeros((b_sz, h_r, HEAD_DIM, HEAD_DIM), F32)
    rc_f, rs_f = _retention_chunkwise(rqc, rkc, rvc, lg_f, zero_r)
    rc_b, rs_b = _retention_chunkwise(_flip_t(rqc), _flip_t(rkc), _flip_t(rvc), lg_b, zero_r)
    rx_f, _ = _retention_chunkwise(rqx, rkx, rvx, lg_f, rs_f)
    rx_b, _ = _retention_chunkwise(_flip_t(rqx), _flip_t(rkx), _flip_t(rvx), lg_b, rs_b)

    def ret_out(h, g):
        return _merge_heads(_rms_norm(h, r_norm)).astype(g.dtype) * jax.nn.silu(g)

    c_x = ret_out(rx_f + _flip_t(rx_b), xp[10])
    mix_x = jnp.concatenate([a_x, b_x, c_x], axis=-1)
    a_c = mlstm_out(hc_f + _flip_t(hc_b), cp[2])
    b_c = _ctx_attention(nqc, nkc, nvc)
    c_c = ret_out(rc_f + _flip_t(rc_b), cp[10])
    return mix_x, jnp.concatenate([a_c, b_c, c_c], axis=-1)


def kernel(x, c, ctx, c_ctx, w_mod, b_mod, norm_mix, norm_ffn, w_in, w_out, mlstm_conv_w, mlstm_conv_b,
           mlstm_gate_b, mlstm_norm, na_q_norm, na_k_norm, na_rpb, ret_norm, router_w, router_b,
           expert_w_up, expert_b_up, expert_w_down, expert_b_down):
    b_sz, t_len, d = x.shape
    t_ctx = ctx.shape[1]
    depth = w_in.shape[0]
    n_e = router_w.shape[2]
    d_mix = w_out.shape[1]
    h_m = d_mix // (4 * HEAD_DIM)
    h_na = d_mix // (2 * HEAD_DIM)
    h_r = d_mix // (4 * HEAD_DIM)
    d_m, d_na, d_r = h_m * HEAD_DIM, h_na * HEAD_DIM, h_r * HEAD_DIM
    n_gate = 4 * h_m
    assert b_sz + 1 <= MOD_ROWS and n_e <= LANES and n_gate <= LANES
    s_len = t_ctx + t_len
    assert s_len % ROW_TILE == 0

    s = jnp.concatenate([ctx, x], axis=1)
    cc = jnp.zeros((MOD_ROWS, d), F32).at[:b_sz].set(c).at[b_sz].set(c_ctx)
    mods = _modulation(cc, w_mod, b_mod).reshape(depth, MOD_ROWS, 1, 6 * d)

    g0 = 4 * d_m
    col_na = g0
    col_ret = col_na + 3 * d_na
    n_hp = h_m // 2
    gate_src = np.array([[g0 + t * h_m + 2 * hp + h2 for t in range(N_GATE_TYPES) for h2 in range(2)]
                         for hp in range(n_hp)])
    rope_tabs = _rope_tables(t_ctx, t_len)
    ret_tabs = _ret_tables(h_r)
    na_tabs = _na_bias_tables(na_rpb)
    f2 = expert_w_up.shape[3]
    w_up_all = _regroup_up(expert_w_up.reshape(depth * n_e, d, f2)).reshape(depth, n_e, d, f2)
    b_up_all = _regroup_bias(expert_b_up)
    w_down_all = expert_w_down.astype(BF16)

    for l in range(depth):
        w_main = jnp.concatenate([w_in[l, :, :g0], w_in[l, :, g0 + n_gate:]], axis=1).astype(BF16)
        w_gate = jnp.zeros((d, n_hp, LANES), BF16).at[:, :, :gate_src.shape[1]].set(
            w_in[l][:, gate_src].astype(BF16)).reshape(d, n_hp * LANES)
        p_main, p_gate = _proj_in(s, mods[l], norm_mix[l], w_main, w_gate, t_ctx)

        mix = [_mlstm(p_main, p_gate, mlstm_conv_w[l], mlstm_conv_b[l], mlstm_gate_b[l], mlstm_norm[l],
                      t_ctx, h_m),
               _na_attention(p_main, na_q_norm[l], na_k_norm[l], na_tabs[l], t_ctx, col_na, h_na),
               _retention(p_main, rope_tabs, ret_tabs, ret_norm[l], t_ctx, col_ret, h_r)]

        rw = jnp.zeros((d, LANES), F32).at[:, :n_e].set(router_w[l])
        rb = jnp.zeros((1, LANES), F32).at[0, :n_e].set(router_b[l])
        s, tok, logits = _proj_out(mix, s, mods[l], norm_ffn[l], w_out[l].astype(BF16), rw, rb, t_ctx)

        s = _moe(tok, logits[..., :n_e], s, mods[l], t_ctx, n_e, w_up_all[l], w_down_all[l],
                 b_up_all[l][:, None, :], expert_b_down[l][:, None, :])
    return s[:, t_ctx:]
```

```python
import functools

import jax
import jax.numpy as jnp
import numpy as np
from jax import lax
from jax.experimental import pallas as pl
from jax.experimental.pallas import tpu as pltpu

F32 = jnp.float32
BF16 = jnp.bfloat16

GRID_W = 64
HEAD_DIM = 64
MLSTM_CHUNK = 64
RET_CHUNK = 64
NA_ROWS = 8
NA_COLS = 16
ROPE_BASE = 10000.0
TOP_K = 4
SWIGLU_ALPHA = 1.702
SWIGLU_LIMIT = 7.0
NORM_EPS = 1e-6
NEG_INF = -1e30

LANES = 128
VMEM_LIMIT = 48 * 1024 * 1024
MOD_ROWS = 8
ROW_TILE = 768
MOE_TILE = 512
MOE_VMEM_LIMIT = 56 * 1024 * 1024


def _cparams(sem):
    return pltpu.CompilerParams(dimension_semantics=sem, vmem_limit_bytes=VMEM_LIMIT)


def _mod_kernel(cc_ref, w_ref, b_ref, o_ref):
    cc = cc_ref[...]
    a = cc * jax.nn.sigmoid(cc)
    o_ref[0] = jnp.dot(a, w_ref[0], precision=lax.Precision.HIGHEST,
                       preferred_element_type=F32) + b_ref[0]


def _modulation(cc, w_mod, b_mod):
    n_l, d, d6 = w_mod.shape
    tn = d6 // 4
    return pl.pallas_call(
        _mod_kernel,
        grid=(n_l, d6 // tn),
        in_specs=[pl.BlockSpec((MOD_ROWS, d), lambda l, j: (0, 0)),
                  pl.BlockSpec((1, d, tn), lambda l, j: (l, 0, j)),
                  pl.BlockSpec((1, 1, tn), lambda l, j: (l, 0, j))],
        out_specs=pl.BlockSpec((1, MOD_ROWS, tn), lambda l, j: (l, 0, j)),
        out_shape=jax.ShapeDtypeStruct((n_l, MOD_ROWS, d6), F32),
        compiler_params=_cparams(("arbitrary", "arbitrary")),
        name="adaln_modulation",
    )(cc, w_mod, b_mod.reshape(n_l, 1, d6))


def _pick_mod(mb_ref, mc_ref, k, d, is_ctx):
    vb = mb_ref[0, :, k * d:(k + 1) * d]
    vc = mc_ref[0, :, k * d:(k + 1) * d]
    return jnp.where(is_ctx, vc, vb)


def _rms(x, g):
    return x * lax.rsqrt(jnp.mean(x * x, axis=-1, keepdims=True) + NORM_EPS) * g


def _proj_in_kernel(s_ref, mb_ref, mc_ref, g_ref, w_ref, wg_ref, pm_ref, pg_ref, xn_ref, *, t_ctx):
    i = pl.program_id(1)
    j = pl.program_id(2)
    tm, d = xn_ref.shape

    @pl.when(j == 0)
    def _():
        row = lax.broadcasted_iota(jnp.int32, (tm, 1), 0) + i * tm
        is_ctx = row < t_ctx
        sh = _pick_mod(mb_ref, mc_ref, 0, d, is_ctx)
        sc = _pick_mod(mb_ref, mc_ref, 1, d, is_ctx)
        h = _rms(s_ref[0], g_ref[...]) * (1.0 + sc) + sh
        xn_ref[...] = h.astype(BF16)
        pg_ref[0] = jnp.dot(xn_ref[...], wg_ref[...], preferred_element_type=F32)

    pm_ref[0] = jnp.dot(xn_ref[...], w_ref[...], preferred_element_type=F32)


def _proj_in(s, mod_l, g, w_main, w_gate, t_ctx):
    b_sz, s_len, d = s.shape
    n_main = w_main.shape[1]
    n_gate = w_gate.shape[1]
    tn = n_main // 4
    tm = ROW_TILE
    return pl.pallas_call(
        functools.partial(_proj_in_kernel, t_ctx=t_ctx),
        grid=(b_sz, s_len // tm, n_main // tn),
        in_specs=[pl.BlockSpec((1, tm, d), lambda b, i, j: (b, i, 0)),
                  pl.BlockSpec((1, 1, 6 * d), lambda b, i, j: (b, 0, 0)),
                  pl.BlockSpec((1, 1, 6 * d), lambda b, i, j: (b_sz, 0, 0)),
                  pl.BlockSpec((1, d), lambda b, i, j: (0, 0)),
                  pl.BlockSpec((d, tn), lambda b, i, j: (0, j)),
                  pl.BlockSpec((d, n_gate), lambda b, i, j: (0, 0))],
        out_specs=[pl.BlockSpec((1, tm, tn), lambda b, i, j: (b, i, j)),
                   pl.BlockSpec((1, tm, n_gate), lambda b, i, j: (b, i, 0))],
        out_shape=[jax.ShapeDtypeStruct((b_sz, s_len, n_main), F32),
                   jax.ShapeDtypeStruct((b_sz, s_len, n_gate), F32)],
        scratch_shapes=[pltpu.VMEM((tm, d), BF16)],
        compiler_params=_cparams(("arbitrary", "arbitrary", "arbitrary")),
        name="proj_in",
    )(s, mod_l, mod_l, g.reshape(1, d), w_main, w_gate)


def _proj_out_kernel(ma_ref, mb2_ref, mc2_ref, s_ref, mb_ref, mc_ref, g_ref, w_ref, rw_ref, rb_ref,
                     so_ref, tok_ref, lg_ref, *, t_ctx):
    i = pl.program_id(1)
    tm, d = s_ref.shape[1], s_ref.shape[2]
    row = lax.broadcasted_iota(jnp.int32, (tm, 1), 0) + i * tm
    is_ctx = row < t_ctx
    g1 = _pick_mod(mb_ref, mc_ref, 2, d, is_ctx)
    ka, kb = ma_ref.shape[2], ma_ref.shape[2] + mb2_ref.shape[2]
    y = (jnp.dot(ma_ref[0], w_ref[0:ka, :], preferred_element_type=F32)
         + jnp.dot(mb2_ref[0], w_ref[ka:kb, :], preferred_element_type=F32)
         + jnp.dot(mc2_ref[0], w_ref[kb:, :], preferred_element_type=F32))
    s_new = s_ref[0] + g1 * y
    so_ref[0] = s_new
    sh = _pick_mod(mb_ref, mc_ref, 3, d, is_ctx)
    sc = _pick_mod(mb_ref, mc_ref, 4, d, is_ctx)
    t = _rms(s_new, g_ref[...]) * (1.0 + sc) + sh
    tok_ref[0] = t
    lg_ref[0] = jnp.dot(t, rw_ref[...], precision=lax.Precision.HIGHEST,
                        preferred_element_type=F32) + rb_ref[...]


def _proj_out(mix_parts, s, mod_l, g, w_out, rw, rb, t_ctx):
    b_sz, s_len, d = s.shape
    tm = ROW_TILE
    row_spec = pl.BlockSpec((1, tm, d), lambda b, i: (b, i, 0))
    part_specs = [pl.BlockSpec((1, tm, m.shape[2]), lambda b, i: (b, i, 0)) for m in mix_parts]
    return pl.pallas_call(
        functools.partial(_proj_out_kernel, t_ctx=t_ctx),
        grid=(b_sz, s_len // tm),
        in_specs=part_specs + [row_spec,
                  pl.BlockSpec((1, 1, 6 * d), lambda b, i: (b, 0, 0)),
                  pl.BlockSpec((1, 1, 6 * d), lambda b, i: (b_sz, 0, 0)),
                  pl.BlockSpec((1, d), lambda b, i: (0, 0)),
                  pl.BlockSpec((d, d), lambda b, i: (0, 0)),
                  pl.BlockSpec((d, LANES), lambda b, i: (0, 0)),
                  pl.BlockSpec((1, LANES), lambda b, i: (0, 0))],
        out_specs=[row_spec, row_spec, pl.BlockSpec((1, tm, LANES), lambda b, i: (b, i, 0))],
        out_shape=[jax.ShapeDtypeStruct((b_sz, s_len, d), F32),
                   jax.ShapeDtypeStruct((b_sz, s_len, d), F32),
                   jax.ShapeDtypeStruct((b_sz, s_len, LANES), F32)],
        compiler_params=_cparams(("arbitrary", "arbitrary")),
        name="proj_out_router",
    )(*mix_parts, s, mod_l, mod_l, g.reshape(1, d), w_out, rw, rb)


PAIR = 2 * LANES


def _regroup_perm():
    dst = np.arange(PAIR)
    src = np.where(dst < LANES, 2 * dst, 2 * (dst - LANES) + 1)
    return jnp.asarray(np.arange(PAIR)[:, None] == src[None, :], BF16)


def _regroup_bias(b_up):
    lead = b_up.shape[:-1]
    b = b_up.reshape(*lead, -1, LANES, 2)
    return jnp.swapaxes(b, -1, -2).reshape(*lead, -1)


def _moe_kernel(be_ref, nu_ref, x_ref, wu_ref, wd_ref, bu_ref, bd_ref, p_ref, y_ref, wus_ref, wds_ref):
    i = pl.program_id(0)
    d, f2 = wus_ref.shape
    rows = min(512, d)

    @pl.when(i < nu_ref[0])
    def _():
        @pl.when((i == 0) | (be_ref[i] != be_ref[jnp.maximum(i - 1, 0)]))
        def _():
            for r in range(d // rows):
                for j in range(f2 // PAIR):
                    w = wu_ref[0, r * rows:(r + 1) * rows, j * PAIR:(j + 1) * PAIR].astype(BF16)
                    wus_ref[r * rows:(r + 1) * rows, j * PAIR:(j + 1) * PAIR] = jnp.dot(
                        w, p_ref[...], preferred_element_type=F32).astype(BF16)
            wds_ref[...] = wd_ref[0].astype(BF16)

        up = jnp.dot(x_ref[...].astype(BF16), wus_ref[...], preferred_element_type=F32) + bu_ref[0]
        acts = []
        for j in range(f2 // PAIR):
            glu = jnp.minimum(up[:, j * PAIR:j * PAIR + LANES], SWIGLU_LIMIT)
            lin = jnp.clip(up[:, j * PAIR + LANES:(j + 1) * PAIR], -SWIGLU_LIMIT, SWIGLU_LIMIT)
            acts.append((glu * jax.nn.sigmoid(SWIGLU_ALPHA * glu) * (lin + 1.0)).astype(BF16))
        act = jnp.concatenate(acts, axis=1)
        y_ref[...] = jnp.dot(act, wds_ref[...], preferred_element_type=F32) + bd_ref[0]

    @pl.when(i >= nu_ref[0])
    def _():
        y_ref[...] = jnp.zeros_like(y_ref)


def _moe_blocks(blk_e, n_used, x_sorted, w_up, w_down, b_up, b_down):
    n_rows, d = x_sorted.shape
    _, _, f2 = w_up.shape
    assert f2 % PAIR == 0 and d % min(512, d) == 0
    tm = MOE_TILE
    wmap = lambda i, be, nu: (be[i], 0, 0)
    return pl.pallas_call(
        _moe_kernel,
        grid_spec=pltpu.PrefetchScalarGridSpec(
            num_scalar_prefetch=2,
            grid=(n_rows // tm,),
            in_specs=[pl.BlockSpec((tm, d), lambda i, be, nu: (i, 0)),
                      pl.BlockSpec((1, d, f2), wmap),
                      pl.BlockSpec((1, f2 // 2, d), wmap),
                      pl.BlockSpec((1, 1, f2), wmap),
                      pl.BlockSpec((1, 1, d), wmap),
                      pl.BlockSpec((PAIR, PAIR), lambda i, be, nu: (0, 0))],
            out_specs=pl.BlockSpec((tm, d), lambda i, be, nu: (i, 0)),
            scratch_shapes=[pltpu.VMEM((d, f2), BF16), pltpu.VMEM((f2 // 2, d), BF16)]),
        out_shape=jax.ShapeDtypeStruct((n_rows, d), F32),
        compiler_params=pltpu.CompilerParams(dimension_semantics=("arbitrary",),
                                             vmem_limit_bytes=MOE_VMEM_LIMIT),
        name="moe_expert_blocks",
    )(blk_e, n_used, x_sorted, w_up, w_down, b_up, b_down, _regroup_perm())


def _combine_kernel(y_ref, gt_ref, s_ref, mb_ref, mc_ref, o_ref, *, t_ctx):
    i = pl.program_id(1)
    tm, d = s_ref.shape[1], s_ref.shape[2]
    row = lax.broadcasted_iota(jnp.int32, (tm, 1), 0) + i * tm
    g2 = _pick_mod(mb_ref, mc_ref, 5, d, row < t_ctx)
    gt = gt_ref[0]
    y = y_ref[0, 0] * gt[:, 0:1]
    for k in range(1, y_ref.shape[0]):
        y = y + y_ref[k, 0] * gt[:, k:k + 1]
    o_ref[0] = s_ref[0] + g2 * y


def _combine(y_as, gates, s, mod_l, t_ctx):
    b_sz, s_len, d = s.shape
    n_k = y_as.shape[0]
    tm = ROW_TILE // 3
    row_spec = pl.BlockSpec((1, tm, d), lambda b, i: (b, i, 0))
    return pl.pallas_call(
        functools.partial(_combine_kernel, t_ctx=t_ctx),
        grid=(b_sz, s_len // tm),
        in_specs=[pl.BlockSpec((n_k, 1, tm, d), lambda b, i: (0, b, i, 0)),
                  pl.BlockSpec((1, tm, n_k), lambda b, i: (b, i, 0)),
                  row_spec,
                  pl.BlockSpec((1, 1, 6 * d), lambda b, i: (b, 0, 0)),
                  pl.BlockSpec((1, 1, 6 * d), lambda b, i: (b_sz, 0, 0))],
        out_specs=row_spec,
        out_shape=jax.ShapeDtypeStruct((b_sz, s_len, d), F32),
        compiler_params=_cparams(("arbitrary", "arbitrary")),
        name="moe_combine",
    )(y_as, gates, s, mod_l, mod_l)


def _moe(tok, logits, s, mod_l, t_ctx, n_e, layer, w_up, w_down, b_up, b_down):
    b_sz, s_len, d = tok.shape
    n_tok = b_sz * s_len
    tm = MOE_TILE
    top_v, top_e = lax.top_k(logits.reshape(n_tok, n_e), TOP_K)
    gates = jax.nn.softmax(top_v, axis=-1)
    n_as = n_tok * TOP_K
    onehot = jnp.sum((top_e[:, :, None] == jnp.arange(n_e)[None, None, :]).astype(jnp.int32), axis=1)
    csum = jnp.cumsum(onehot, axis=0)
    counts = csum[-1]
    rank = jnp.take_along_axis(csum - onehot, top_e, axis=1)
    padded = (counts + tm - 1) // tm * tm
    end_pad = jnp.cumsum(padded)
    start_pad = end_pad - padded
    start = jnp.cumsum(counts) - counts
    dest = (start_pad[top_e] + rank).astype(jnp.int32)
    n_blocks = -(-n_as // tm) + n_e
    blk_e = jnp.minimum(jnp.searchsorted(end_pad, jnp.arange(n_blocks) * tm, side='right'),
                        n_e - 1).astype(jnp.int32)
    n_used = (end_pad[-1:] // tm).astype(jnp.int32)
    tok_sorted = (jnp.argsort(top_e.reshape(n_as)) // TOP_K).astype(jnp.int32)
    row = jnp.arange(n_blocks * tm)
    row_e = jnp.repeat(blk_e, tm)
    j = row - start_pad[row_e]
    src = jnp.clip(start[row_e] + j, 0, n_as - 1)
    row_tok = jnp.where(j < counts[row_e], tok_sorted[src], 0)
    x_sorted = tok.reshape(n_tok, d)[row_tok]
    y = _moe_blocks(blk_e + layer * n_e, n_used, x_sorted, w_up, w_down, b_up, b_down)
    y_as = y[dest.T].reshape(TOP_K, b_sz, s_len, d)
    return _combine(y_as, gates.reshape(b_sz, s_len, TOP_K), s, mod_l, t_ctx)


NA_STEP = 256
NA_KEYS = NA_ROWS * GRID_W


def _head_rms(x, w, lo):
    xx = x * x
    s0 = jnp.sum(jnp.where(lo, xx, 0.0), axis=-1, keepdims=True)
    s1 = jnp.sum(jnp.where(lo, 0.0, xx), axis=-1, keepdims=True)
    inv = lax.rsqrt(jnp.where(lo, s0, s1) * (1.0 / HEAD_DIM) + NORM_EPS)
    return x * inv * w


def _dot_nt(a, b):
    return lax.dot_general(a, b, (((1,), (1,)), ((), ())), preferred_element_type=F32)


def _na_kernel(q_ref, k_ref, v_ref, qw_ref, kw_ref, bias_ref, o_ref, kn_ref, vb_ref, *, t_ctx, rows):
    rg = pl.program_id(2)
    s_len = k_ref.shape[1]
    lo = lax.broadcasted_iota(jnp.int32, (1, LANES), 1) < HEAD_DIM
    rows_per_step = NA_STEP // GRID_W

    @pl.when(rg == 0)
    def _():
        def prep(c, carry):
            t0 = pl.multiple_of(c * NA_STEP, NA_STEP)
            kn_ref[pl.ds(t0, NA_STEP), :] = _head_rms(k_ref[0, pl.ds(t0, NA_STEP), :], kw_ref[...], lo).astype(BF16)
            vb_ref[pl.ds(t0, NA_STEP), :] = v_ref[0, pl.ds(t0, NA_STEP), :].astype(BF16)
            return carry
        lax.fori_loop(0, s_len // NA_STEP, prep, 0)

    qn = _head_rms(q_ref[0], qw_ref[...], lo) * (HEAD_DIM ** -0.5)
    q0 = jnp.where(lo, qn, 0.0).astype(BF16)
    q1 = jnp.where(lo, 0.0, qn).astype(BF16)
    kc = kn_ref[0:t_ctx, :]
    vc = vb_ref[0:t_ctx, :]

    def finish(g, o2, l):
        o2 = o2 * (1.0 / l)
        o = jnp.where(lo, o2[:GRID_W], o2[GRID_W:])
        o_ref[0, g * GRID_W:(g + 1) * GRID_W, :] = o.astype(o_ref.dtype)

    @pl.when(rg == 0)
    def _():
        for g in range(rows_per_step):
            q2 = jnp.concatenate([q0[g * GRID_W:(g + 1) * GRID_W], q1[g * GRID_W:(g + 1) * GRID_W]], axis=0)
            s_c = _dot_nt(q2, kc)
            m = jnp.max(s_c, axis=-1, keepdims=True)
            p_c = jnp.exp(s_c - m)
            l = jnp.sum(p_c, axis=-1, keepdims=True)
            finish(g, jnp.dot(p_c.astype(BF16), vc, preferred_element_type=F32), l)

    @pl.when(rg > 0)
    def _():
        for g in range(rows_per_step):
            r = (rg - 1) * rows_per_step + g
            row_start = jnp.clip(r - NA_ROWS // 2, 0, rows - NA_ROWS)
            dr0 = row_start - r + NA_ROWS - 1
            t0 = pl.multiple_of(t_ctx + row_start * GRID_W, GRID_W)
            kw = kn_ref[pl.ds(t0, NA_KEYS), :]
            vw = vb_ref[pl.ds(t0, NA_KEYS), :]
            q2 = jnp.concatenate([q0[g * GRID_W:(g + 1) * GRID_W], q1[g * GRID_W:(g + 1) * GRID_W]], axis=0)
            s_l = _dot_nt(q2, kw) + bias_ref[dr0, 0]
            s_c = _dot_nt(q2, kc)
            m = jnp.maximum(jnp.max(s_l, axis=-1, keepdims=True), jnp.max(s_c, axis=-1, keepdims=True))
            p_l = jnp.exp(s_l - m)
            p_c = jnp.exp(s_c - m)
            l = jnp.sum(p_l, axis=-1, keepdims=True) + jnp.sum(p_c, axis=-1, keepdims=True)
            o2 = (jnp.dot(p_l.astype(BF16), vw, preferred_element_type=F32)
                  + jnp.dot(p_c.astype(BF16), vc, preferred_element_type=F32))
            finish(g, o2, l)


def _na_bias_tables(rpb):
    n_l, n_h, n_dr, n_dc = rpb.shape
    col = np.arange(GRID_W)
    dc = np.clip(col[None, :] - col[:, None] + NA_COLS - 1, 0, n_dc - 1)
    onehot = (dc.reshape(1, -1) == np.arange(n_dc)[:, None]).astype(np.float32)
    toe = jnp.dot(rpb.reshape(-1, n_dc), jnp.asarray(onehot), precision=lax.Precision.HIGHEST)
    toe = toe.reshape(n_l, n_h, n_dr, GRID_W, GRID_W)
    col_start = np.clip(col - NA_COLS // 2, 0, GRID_W - NA_COLS)
    col_in = (col[None, :] >= col_start[:, None]) & (col[None, :] < col_start[:, None] + NA_COLS)
    toe = jnp.where(jnp.asarray(col_in), toe, NEG_INF)
    tabs = []
    for dr0 in range(NA_ROWS):
        t = toe[:, :, dr0:dr0 + NA_ROWS].reshape(n_l, n_h // 2, 2, NA_ROWS, GRID_W, GRID_W)
        t = jnp.transpose(t, (0, 1, 2, 4, 3, 5))
        tabs.append(t.reshape(n_l, n_h // 2, 2 * GRID_W, NA_KEYS))
    return jnp.stack(tabs, axis=1)


def _na_attention(p_main, q_w, k_w, bias_tab, t_ctx, col_q, n_heads):
    b_sz, s_len, _ = p_main.shape
    n_hg = n_heads // 2
    rows = (s_len - t_ctx) // GRID_W
    assert t_ctx == NA_STEP and s_len % NA_STEP == 0 and rows >= NA_ROWS
    cq, ck, cv = col_q // LANES, col_q // LANES + n_hg, col_q // LANES + 2 * n_hg
    w2 = lambda w: jnp.concatenate([w, w]).reshape(1, LANES)
    return pl.pallas_call(
        functools.partial(_na_kernel, t_ctx=t_ctx, rows=rows),
        grid=(b_sz, n_hg, s_len // NA_STEP),
        in_specs=[pl.BlockSpec((1, NA_STEP, LANES), lambda b, h, r: (b, r, cq + h)),
                  pl.BlockSpec((1, s_len, LANES), lambda b, h, r: (b, 0, ck + h)),
                  pl.BlockSpec((1, s_len, LANES), lambda b, h, r: (b, 0, cv + h)),
                  pl.BlockSpec((1, LANES), lambda b, h, r: (0, 0)),
                  pl.BlockSpec((1, LANES), lambda b, h, r: (0, 0)),
                  pl.BlockSpec((NA_ROWS, 1, 2 * GRID_W, NA_KEYS), lambda b, h, r: (0, h, 0, 0))],
        out_specs=pl.BlockSpec((1, NA_STEP, LANES), lambda b, h, r: (b, r, h)),
        out_shape=jax.ShapeDtypeStruct((b_sz, s_len, n_heads * HEAD_DIM), BF16),
        scratch_shapes=[pltpu.VMEM((s_len, LANES), BF16), pltpu.VMEM((s_len, LANES), BF16)],
        compiler_params=_cparams(("arbitrary", "arbitrary", "arbitrary")),
        name="na_attention",
    )(p_main, p_main, p_main, w2(q_w), w2(k_w), bias_tab)


RET_CHUNK_LEN = 256
MLSTM_CHUNK_LEN = 128


def _reverse_chunk(i, nc, n_ctx):
    return jnp.where(i < n_ctx, n_ctx - 1 - i, nc - 1 - i + n_ctx)


def _dot_tn(a, b):
    return lax.dot_general(a, b, (((0,), (0,)), ((), ())), preferred_element_type=F32)


def _rope_tables(t_ctx, t_len):
    pos = jnp.arange(t_len)
    row = (pos // GRID_W).astype(F32)
    col = (pos % GRID_W).astype(F32)
    n = HEAD_DIM // 4
    inv = ROPE_BASE ** (-jnp.arange(n, dtype=F32) / n)
    ar = row[:, None] * inv
    ac = col[:, None] * inv
    cos = jnp.concatenate([jnp.cos(ar), jnp.cos(ar), jnp.cos(ac), jnp.cos(ac)], axis=-1)
    sin = jnp.concatenate([-jnp.sin(ar), jnp.sin(ar), -jnp.sin(ac), jnp.sin(ac)], axis=-1)
    cos = jnp.concatenate([jnp.ones((t_ctx, HEAD_DIM), F32), cos], axis=0)
    sin = jnp.concatenate([jnp.zeros((t_ctx, HEAD_DIM), F32), sin], axis=0)
    return jnp.tile(cos, (1, 2)), jnp.tile(sin, (1, 2))


def _ret_tables(n_heads):
    L = RET_CHUNK_LEN
    pos = np.arange(L, dtype=np.float32)
    lane_head = np.arange(LANES) // HEAD_DIM
    decay = np.zeros((2, n_heads, L, L), np.float32)
    zeta = np.zeros((2, n_heads // 2, L, LANES), np.float32)
    xi = np.zeros((2, n_heads // 2, L, LANES), np.float32)
    gch = np.zeros((2, n_heads // 2, 1, LANES), np.float32)
    for d, first_exp in enumerate((5.0, 6.0)):
        e = np.float32(first_exp) + np.float32(2.0) * np.arange(n_heads, dtype=np.float32)
        lg = np.log1p(-np.exp2(-e)).astype(np.float32)
        diff = pos[:, None] - pos[None, :]
        if d == 1:
            diff = -diff
        for h in range(n_heads):
            decay[d, h] = np.where(diff >= 0, np.exp(lg[h] * np.maximum(diff, 0.0)), 0.0)
        for hp in range(n_heads // 2):
            lgl = lg[2 * hp + lane_head][None, :]
            to_end = (L - 1 - pos if d == 0 else pos)[:, None]
            zeta[d, hp] = np.exp(lgl * to_end)
            xi[d, hp] = np.exp(lgl * (L - to_end))
            gch[d, hp] = np.exp(lgl * L)
    return tuple(jnp.asarray(a) for a in (decay, zeta, xi, gch))


def _ret_kernel(q_ref, k_ref, v_ref, g_ref, cos_ref, sin_ref, dec_ref, zeta_ref, xi_ref, gch_ref, rn_ref,
                o_ref, qr_ref, kr_ref, vb_ref, acc_ref, *, t_ctx):
    L = RET_CHUNK_LEN
    s_len = q_ref.shape[1]
    nc = s_len // L
    lane = lax.broadcasted_iota(jnp.int32, (1, LANES), 1)
    lo = lane < HEAD_DIM
    half = (lane & (HEAD_DIM // 4)) == 0
    rid = lax.broadcasted_iota(jnp.int32, (LANES, LANES), 0) < HEAD_DIM
    cid = lax.broadcasted_iota(jnp.int32, (LANES, LANES), 1) < HEAD_DIM
    same_head = rid == cid

    def rope(x, cos, sin):
        up = pltpu.roll(x, LANES - HEAD_DIM // 4, axis=1)
        dn = pltpu.roll(x, HEAD_DIM // 4, axis=1)
        return x * cos + jnp.where(half, up, dn) * sin

    def prep(c, carry):
        sl = pl.ds(pl.multiple_of(c * L, L), L)
        cos, sin = cos_ref[sl, :], sin_ref[sl, :]
        qr_ref[sl, :] = rope(q_ref[0, sl, :], cos, sin).astype(BF16)
        kr_ref[sl, :] = (rope(k_ref[0, sl, :], cos, sin) * HEAD_DIM ** -0.5).astype(BF16)
        vb_ref[sl, :] = v_ref[0, sl, :].astype(BF16)
        return carry
    lax.fori_loop(0, nc, prep, 0)

    def chunk(d, c, state):
        sl = pl.ds(pl.multiple_of(c * L, L), L)
        q, k, v = qr_ref[sl, :], kr_ref[sl, :], vb_ref[sl, :]
        inter = jnp.dot(q, state.astype(BF16), preferred_element_type=F32) * xi_ref[d, 0]
        outs = []
        for h2 in range(2):
            qm = jnp.where(lo if h2 == 0 else jnp.logical_not(lo), q, jnp.zeros_like(q))
            sd = (_dot_nt(qm, k) * dec_ref[d, h2]).astype(BF16)
            outs.append(jnp.dot(sd, v, preferred_element_type=F32))
        y = jnp.where(lo, outs[0], outs[1]) + inter
        kz = (k.astype(F32) * zeta_ref[d, 0]).astype(BF16)
        state = state * gch_ref[d, 0] + jnp.where(same_head, _dot_tn(kz, v), 0.0)
        return sl, y, state

    def fwd(c, state):
        sl, y, state = chunk(0, c, state)
        acc_ref[sl, :] = y
        return state
    lax.fori_loop(0, nc, fwd, jnp.zeros((LANES, LANES), F32))

    def bwd(i, state):
        c = _reverse_chunk(i, nc, t_ctx // L)
        sl, y, state = chunk(1, c, state)
        y = _head_rms(y + acc_ref[sl, :], rn_ref[...], lo)
        g = g_ref[0, sl, :]
        o_ref[0, sl, :] = (y * (g * jax.nn.sigmoid(g))).astype(o_ref.dtype)
        return state
    lax.fori_loop(0, nc, bwd, jnp.zeros((LANES, LANES), F32))


def _seq_spec(s_len, col):
    return pl.BlockSpec((1, s_len, LANES), lambda b, h: (b, 0, col + h), pipeline_mode=pl.Buffered(1))


def _retention(p_main, rope_tabs, ret_tabs, r_w, t_ctx, col_q, n_heads):
    b_sz, s_len, _ = p_main.shape
    n_hp = n_heads // 2
    L = RET_CHUNK_LEN
    assert t_ctx % L == 0 and s_len % L == 0
    c0 = col_q // LANES
    cos, sin = rope_tabs
    decay, zeta, xi, gch = ret_tabs
    const2 = pl.BlockSpec((s_len, LANES), lambda b, h: (0, 0), pipeline_mode=pl.Buffered(1))
    return pl.pallas_call(
        functools.partial(_ret_kernel, t_ctx=t_ctx),
        grid=(b_sz, n_hp),
        in_specs=[_seq_spec(s_len, c0), _seq_spec(s_len, c0 + n_hp), _seq_spec(s_len, c0 + 2 * n_hp),
                  _seq_spec(s_len, c0 + 3 * n_hp), const2, const2,
                  pl.BlockSpec((2, 2, L, L), lambda b, h: (0, h, 0, 0)),
                  pl.BlockSpec((2, 1, L, LANES), lambda b, h: (0, h, 0, 0)),
                  pl.BlockSpec((2, 1, L, LANES), lambda b, h: (0, h, 0, 0)),
                  pl.BlockSpec((2, 1, 1, LANES), lambda b, h: (0, h, 0, 0)),
                  pl.BlockSpec((1, LANES), lambda b, h: (0, 0))],
        out_specs=pl.BlockSpec((1, s_len, LANES), lambda b, h: (b, 0, h)),
        out_shape=jax.ShapeDtypeStruct((b_sz, s_len, n_heads * HEAD_DIM), BF16),
        scratch_shapes=[pltpu.VMEM((s_len, LANES), BF16), pltpu.VMEM((s_len, LANES), BF16),
                        pltpu.VMEM((s_len, LANES), BF16), pltpu.VMEM((s_len, LANES), F32)],
        compiler_params=_cparams(("arbitrary", "arbitrary")),
        name="retention",
    )(p_main, p_main, p_main, p_main, cos, sin, decay, zeta, xi, gch,
      jnp.concatenate([r_w, r_w]).reshape(1, LANES))


N_GATE_TYPES = 4


def _log_sigmoid(x):
    return jnp.minimum(x, 0.0) - jnp.log1p(jnp.exp(-jnp.abs(x)))


def _mlstm_kernel(q_ref, k_ref, v_ref, og_ref, gc_ref, gr_ref, wq_ref, wk_ref, bq_ref, bk_ref, gbc_ref, gbr_ref,
                  mn_ref, o_ref, qc_ref, kc_ref, vb_ref, acc_ref, *, t_ctx):
    L = MLSTM_CHUNK_LEN
    s_len = q_ref.shape[1]
    nc = s_len // L
    lane = lax.broadcasted_iota(jnp.int32, (1, LANES), 1)
    lo = lane < HEAD_DIM
    head_lanes = (lo, jnp.logical_not(lo))
    rid = lax.broadcasted_iota(jnp.int32, (LANES, LANES), 0) < HEAD_DIM
    cid = lax.broadcasted_iota(jnp.int32, (LANES, LANES), 1) < HEAD_DIM
    head_block = (rid & cid, jnp.logical_not(rid | cid))
    row_i = lax.broadcasted_iota(jnp.int32, (L, L), 0)
    col_i = lax.broadcasted_iota(jnp.int32, (L, L), 1)
    causal = (row_i >= col_i, row_i <= col_i)
    tri = tuple(c.astype(F32) for c in causal)
    tri_t = (tri[1], tri[0])
    sub = lax.broadcasted_iota(jnp.int32, (L, 1), 0)

    def conv(x_ref, w_ref, b_ref, t0):
        x = x_ref[0, pl.ds(t0, L), :]
        prev = x_ref[0, pl.ds(jnp.maximum(t0 - 8, 0), 8), :][7:8]
        nxt = x_ref[0, pl.ds(jnp.minimum(t0 + L, s_len - 8), 8), :][0:1]
        prev = jnp.where((t0 != 0) & (t0 != t_ctx), prev, 0.0)
        nxt = jnp.where((t0 + L != t_ctx) & (t0 + L != s_len), nxt, 0.0)
        xm = jnp.where(sub == 0, prev, pltpu.roll(x, 1, axis=0))
        xp = jnp.where(sub == L - 1, nxt, pltpu.roll(x, L - 1, axis=0))
        y = b_ref[...] + xm * w_ref[0:1, :] + x * w_ref[1:2, :] + xp * w_ref[2:3, :]
        return y * jax.nn.sigmoid(y)

    def prep(c, carry):
        t0 = pl.multiple_of(c * L, L)
        sl = pl.ds(t0, L)
        qc_ref[sl, :] = conv(q_ref, wq_ref, bq_ref, t0).astype(BF16)
        kc_ref[sl, :] = (conv(k_ref, wk_ref, bk_ref, t0) * HEAD_DIM ** -0.5).astype(BF16)
        vb_ref[sl, :] = v_ref[0, sl, :].astype(BF16)
        return carry
    lax.fori_loop(0, nc, prep, 0)

    def chunk(d, c, state):
        sl = pl.ds(pl.multiple_of(c * L, L), L)
        q, k, v = qc_ref[sl, :], kc_ref[sl, :], vb_ref[sl, :]
        g_col = gc_ref[0, sl, :] + gbc_ref[0]
        g_row = gr_ref[0, 0, :, sl] + gbr_ref[0]
        cum_col = jnp.dot(tri[d], _log_sigmoid(g_col), precision=lax.Precision.HIGHEST,
                          preferred_element_type=F32)
        cum_row = jnp.dot(_log_sigmoid(g_row), tri_t[d], precision=lax.Precision.HIGHEST,
                          preferred_element_type=F32)
        end = L - 1 if d == 0 else 0
        outs, new_state = [], []
        for h2 in range(2):
            c_st, n_st, m_st = state[h2]
            ci, cf = 2 * (2 * d) + h2, 2 * (2 * d + 1) + h2
            i_col, a_col = g_col[:, ci:ci + 1], cum_col[:, cf:cf + 1]
            i_row, a_row = g_row[ci:ci + 1, :], cum_row[cf:cf + 1, :]
            b_tot = a_row[:, end:end + 1]
            d_log = jnp.where(causal[d], a_col + (i_row - a_row), NEG_INF)
            m_intra = jnp.max(d_log, axis=-1, keepdims=True)
            qm = jnp.where(head_lanes[h2], q, jnp.zeros_like(q))
            s = _dot_nt(qm, k) * jnp.exp(d_log - m_intra)
            num_intra = jnp.dot(s.astype(BF16), v, preferred_element_type=F32)
            den_intra = jnp.sum(s, axis=-1, keepdims=True)
            inter_log = a_col + m_st
            m_q = jnp.maximum(m_intra, inter_log)
            a = jnp.exp(inter_log - m_q)
            g = jnp.exp(m_intra - m_q)
            num = a * jnp.dot(qm, c_st.astype(BF16), preferred_element_type=F32) + g * num_intra
            den = a * jnp.sum(qm.astype(F32) * n_st, axis=-1, keepdims=True) + g * den_intra
            outs.append(num / jnp.maximum(jnp.abs(den), jnp.exp(-m_q)))
            w_log = b_tot - a_col + i_col
            m_loc = jnp.max(w_log, axis=0, keepdims=True)
            ke = jnp.where(head_lanes[h2], k.astype(F32) * jnp.exp(w_log - m_loc), 0.0)
            c_loc = jnp.where(head_block[h2], _dot_tn(ke.astype(BF16), v), 0.0)
            n_loc = jnp.sum(ke, axis=0, keepdims=True)
            m_new = jnp.maximum(b_tot + m_st, m_loc)
            a_s = jnp.exp(b_tot + m_st - m_new)
            g_s = jnp.exp(m_loc - m_new)
            new_state.append((a_s * c_st + g_s * c_loc, a_s * n_st + g_s * n_loc, m_new))
        return sl, jnp.where(lo, outs[0], outs[1]), tuple(new_state)

    zero = tuple((jnp.zeros((LANES, LANES), F32), jnp.zeros((1, LANES), F32), jnp.zeros((1, 1), F32))
                 for _ in range(2))

    def fwd(c, state):
        sl, y, state = chunk(0, c, state)
        acc_ref[sl, :] = y
        return state
    lax.fori_loop(0, nc, fwd, zero)

    def bwd(i, state):
        c = _reverse_chunk(i, nc, t_ctx // L)
        sl, y, state = chunk(1, c, state)
        y = _head_rms(y + acc_ref[sl, :], mn_ref[...], lo)
        o_ref[0, sl, :] = (y * jax.nn.sigmoid(og_ref[0, sl, :])).astype(o_ref.dtype)
        return state
    lax.fori_loop(0, nc, bwd, zero)


def _mlstm(p_main, p_gate, conv_w, conv_b, gate_b, m_w, t_ctx, n_heads):
    b_sz, s_len, _ = p_main.shape
    n_hp = n_heads // 2
    L = MLSTM_CHUNK_LEN
    assert t_ctx % L == 0 and s_len % L == 0
    n_g = 2 * N_GATE_TYPES
    g_rows = jnp.transpose(p_gate.reshape(b_sz, s_len, n_hp, LANES)[..., :n_g], (0, 2, 3, 1))
    gb = jnp.transpose(gate_b.reshape(N_GATE_TYPES, n_hp, 2), (1, 0, 2)).reshape(n_hp, n_g)
    gb_col = jnp.zeros((n_hp, 1, LANES), F32).at[:, 0, :n_g].set(gb)
    gb_row = gb.reshape(n_hp, n_g, 1)
    vec = lambda col: pl.BlockSpec((1, LANES), lambda b, h: (0, col + h))
    return pl.pallas_call(
        functools.partial(_mlstm_kernel, t_ctx=t_ctx),
        grid=(b_sz, n_hp),
        in_specs=[_seq_spec(s_len, 0), _seq_spec(s_len, n_hp), _seq_spec(s_len, 2 * n_hp),
                  _seq_spec(s_len, 3 * n_hp),
                  pl.BlockSpec((1, s_len, LANES), lambda b, h: (b, 0, h), pipeline_mode=pl.Buffered(1)),
                  pl.BlockSpec((1, 1, n_g, s_len), lambda b, h: (b, h, 0, 0)),
                  pl.BlockSpec((3, LANES), lambda b, h: (0, h)),
                  pl.BlockSpec((3, LANES), lambda b, h: (0, n_hp + h)),
                  vec(0), vec(n_hp),
                  pl.BlockSpec((1, 1, LANES), lambda b, h: (h, 0, 0)),
                  pl.BlockSpec((1, n_g, 1), lambda b, h: (h, 0, 0)),
                  pl.BlockSpec((1, LANES), lambda b, h: (0, 0))],
        out_specs=pl.BlockSpec((1, s_len, LANES), lambda b, h: (b, 0, h)),
        out_shape=jax.ShapeDtypeStruct((b_sz, s_len, n_heads * HEAD_DIM), BF16),
        scratch_shapes=[pltpu.VMEM((s_len, LANES), BF16), pltpu.VMEM((s_len, LANES), BF16),
                        pltpu.VMEM((s_len, LANES), BF16), pltpu.VMEM((s_len, LANES), F32)],
        compiler_params=_cparams(("arbitrary", "arbitrary")),
        name="mlstm",
    )(p_main, p_main, p_main, p_main, p_gate, g_rows, conv_w, conv_w, conv_b.reshape(1, -1),
      conv_b.reshape(1, -1), gb_col, gb_row, jnp.concatenate([m_w, m_w]).reshape(1, LANES))


def _rms_norm(x, g):
    xf = x.astype(F32)
    y = xf * lax.rsqrt(jnp.mean(xf * xf, axis=-1, keepdims=True) + NORM_EPS)
    return (y * g.astype(F32)).astype(x.dtype)


def _split_heads(t, n_heads):
    return t.reshape(t.shape[0], t.shape[1], n_heads, HEAD_DIM)


def _to_bhtd(t):
    return jnp.transpose(t, (0, 2, 1, 3))


def _merge_heads(t):
    b, h, t_len, d = t.shape
    return jnp.transpose(t, (0, 2, 1, 3)).reshape(b, t_len, h * d)


def _flip_t(t):
    return jnp.flip(t, axis=2)


def _dwconv_centred(x, w, b):
    k_size = w.shape[0]
    pad = k_size // 2
    t_len = x.shape[1]
    xp = jnp.pad(x, ((0, 0), (pad, pad), (0, 0)))
    y = b
    for i in range(k_size):
        y = y + xp[:, i:i + t_len] * w[i]
    return y


def _axial_rope_tables(t_len):
    pos = jnp.arange(t_len)
    row = (pos // GRID_W).astype(F32)
    col = (pos % GRID_W).astype(F32)
    n = HEAD_DIM // 4
    inv = ROPE_BASE ** (-jnp.arange(n, dtype=F32) / n)
    ar = row[:, None] * inv
    ac = col[:, None] * inv
    return (jnp.cos(ar), jnp.sin(ar), jnp.cos(ac), jnp.sin(ac))


def _rotate_half(x, cos, sin):
    x1, x2 = jnp.split(x, 2, axis=-1)
    return jnp.concatenate([x1 * cos - x2 * sin, x2 * cos + x1 * sin], axis=-1)


def _axial_rope(x, cos_r, sin_r, cos_c, sin_c):
    xr, xc = jnp.split(x, 2, axis=-1)
    return jnp.concatenate([_rotate_half(xr, cos_r, sin_r), _rotate_half(xc, cos_c, sin_c)], axis=-1)


def _mlstm_chunkwise(q, k, v, log_i, log_f, state0):
    b_sz, h_sz, t_len, d = q.shape
    L = MLSTM_CHUNK
    nc = t_len // L
    qc = q.reshape(b_sz, h_sz, nc, L, d)
    kc = k.reshape(b_sz, h_sz, nc, L, d)
    vc = v.reshape(b_sz, h_sz, nc, L, d)
    li = log_i.reshape(b_sz, h_sz, nc, L)
    bcum = jnp.cumsum(log_f.reshape(b_sz, h_sz, nc, L), axis=-1)
    b_tot = bcum[..., -1]
    tri = jnp.tril(jnp.ones((L, L), dtype=bool))
    d_log = jnp.where(tri, bcum[..., :, None] - bcum[..., None, :] + li[..., None, :], NEG_INF)
    m_intra = jnp.max(d_log, axis=-1)
    s = jnp.einsum('bhcjd,bhcld->bhcjl', qc, kc).astype(F32) * jnp.exp(d_log - m_intra[..., None])
    num_intra = jnp.einsum('bhcjl,bhcld->bhcjd', s, vc)
    den_intra = jnp.sum(s, axis=-1)
    w_log = b_tot[..., None] - bcum + li
    m_loc = jnp.max(w_log, axis=-1)
    e = jnp.exp(w_log - m_loc[..., None])
    c_loc = jnp.einsum('bhcl,bhcld,bhcle->bhcde', e, kc, vc)
    n_loc = jnp.einsum('bhcl,bhcld->bhcd', e, kc)

    def step(carry, inp):
        c_st, n_st, m_st = carry
        cl, nl, ml, bt = inp
        m_new = jnp.maximum(bt + m_st, ml)
        a = jnp.exp(bt + m_st - m_new)
        g = jnp.exp(ml - m_new)
        new = (a[..., None, None] * c_st + g[..., None, None] * cl, a[..., None] * n_st + g[..., None] * nl, m_new)
        return new, carry

    xs = (jnp.moveaxis(c_loc, 2, 0), jnp.moveaxis(n_loc, 2, 0), jnp.moveaxis(m_loc, 2, 0), jnp.moveaxis(b_tot, 2, 0))
    final, prev = lax.scan(step, state0, xs)
    c_prev = jnp.moveaxis(prev[0], 0, 2)
    n_prev = jnp.moveaxis(prev[1], 0, 2)
    m_prev = jnp.moveaxis(prev[2], 0, 2)
    inter_log = bcum + m_prev[..., None]
    m_q = jnp.maximum(m_intra, inter_log)
    a = jnp.exp(inter_log - m_q)
    g = jnp.exp(m_intra - m_q)
    num = a[..., None] * jnp.einsum('bhcjd,bhcde->bhcje', qc, c_prev) + g[..., None] * num_intra
    den = a * jnp.einsum('bhcjd,bhcd->bhcj', qc, n_prev) + g * den_intra
    h = num / jnp.maximum(jnp.abs(den), jnp.exp(-m_q))[..., None]
    return h.reshape(b_sz, h_sz, t_len, d), final


def _retention_log_decay(first_exp, n_heads):
    e = first_exp + 2.0 * jnp.arange(n_heads, dtype=F32)
    return jnp.log1p(-jnp.exp2(-e))


def _retention_chunkwise(q, k, v, log_gamma, state0):
    b_sz, h_sz, t_len, d = q.shape
    L = RET_CHUNK
    nc = t_len // L
    qc = q.reshape(b_sz, h_sz, nc, L, d)
    kc = k.reshape(b_sz, h_sz, nc, L, d)
    vc = v.reshape(b_sz, h_sz, nc, L, d)
    pos = jnp.arange(L, dtype=F32)
    diff = pos[:, None] - pos[None, :]
    decay = jnp.where(diff >= 0, jnp.exp(log_gamma[:, None, None] * jnp.maximum(diff, 0.0)), 0.0)
    s = jnp.einsum('bhcjd,bhcld->bhcjl', qc, kc).astype(F32) * decay[:, None]
    intra = jnp.einsum('bhcjl,bhcld->bhcjd', s, vc)
    zeta = jnp.exp(log_gamma[:, None] * (L - 1 - pos))
    s_loc = jnp.einsum('hl,bhcld,bhcle->bhcde', zeta, kc, vc)
    g_chunk = jnp.exp(log_gamma * L)[:, None, None]

    def step(r, sl):
        return g_chunk * r + sl, r

    final, r_prev = lax.scan(step, state0, jnp.moveaxis(s_loc, 2, 0))
    r_prev = jnp.moveaxis(r_prev, 0, 2)
    xi = jnp.exp(log_gamma[:, None] * (pos + 1.0))
    inter = jnp.einsum('bhcjd,bhcde->bhcje', qc, r_prev) * xi[:, None, :, None]
    return (intra + inter).reshape(b_sz, h_sz, t_len, d), final


def _na_window(rows):
    wr = min(NA_ROWS, rows)
    r = jnp.arange(rows)
    col = jnp.arange(GRID_W)
    row_idx = jnp.clip(r - wr // 2, 0, rows - wr)[:, None] + jnp.arange(wr)[None, :]
    col_start = jnp.clip(col - NA_COLS // 2, 0, GRID_W - NA_COLS)
    col_in = (col[None, :] >= col_start[:, None]) & (col[None, :] < col_start[:, None] + NA_COLS)
    dr = row_idx - r[:, None] + NA_ROWS - 1
    dc = jnp.clip(col[None, :] - col[:, None] + NA_COLS - 1, 0, 2 * NA_COLS - 2)
    return (row_idx, col_in, dr[:, None, :, None], dc[None, :, None, :])


def _na_latent(q, k, v, k_ctx, v_ctx, row_idx, col_in, bias):
    t_len, h_sz, d = q.shape
    rows, wr = row_idx.shape
    scale = d ** -0.5
    qg = q.reshape(rows, GRID_W, h_sz, d)
    kb = k.reshape(rows, GRID_W, h_sz, d)[row_idx]
    vb = v.reshape(rows, GRID_W, h_sz, d)[row_idx]
    s_loc = jnp.einsum('rqhd,rjwhd->hrqjw', qg, kb).astype(F32) * scale + bias
    s_loc = jnp.where(col_in[:, None, :], s_loc, NEG_INF)
    s_ctx = jnp.einsum('rqhd,chd->hrqc', qg, k_ctx).astype(F32) * scale
    n_loc = wr * GRID_W
    s = jnp.concatenate([s_loc.reshape(h_sz, rows, GRID_W, n_loc), s_ctx], axis=-1)
    p = jax.nn.softmax(s, axis=-1).astype(v.dtype)
    p_loc = p[..., :n_loc].reshape(h_sz, rows, GRID_W, wr, GRID_W)
    o = jnp.einsum('hrqjw,rjwhd->rqhd', p_loc, vb) + jnp.einsum('hrqc,chd->rqhd', p[..., n_loc:], v_ctx)
    return o.reshape(t_len, h_sz * d)


def _ctx_attention(q, k, v):
    s = jnp.einsum('bqhd,bkhd->bhqk', q, k).astype(F32) * HEAD_DIM ** -0.5
    p = jax.nn.softmax(s, axis=-1).astype(v.dtype)
    o = jnp.einsum('bhqk,bkhd->bqhd', p, v)
    return o.reshape(o.shape[0], o.shape[1], -1)


def _hybrid_mixer(xp, cp, rope, na_win, conv_w, conv_b, gate_b, m_norm, q_norm, k_norm, rpb, r_norm, dims):
    h_m, h_na, h_r = dims
    b_sz = xp[0].shape[0]

    def mlstm_prep(p):
        qk = jax.nn.silu(_dwconv_centred(p[0], conv_w, conv_b))
        q, k = jnp.split(qk, 2, axis=-1)
        g = (p[3] + gate_b).astype(F32)
        g = jnp.transpose(g.reshape(b_sz, -1, 4, h_m), (2, 0, 3, 1))
        return (_to_bhtd(_split_heads(q, h_m)), _to_bhtd(_split_heads(k, h_m)) * HEAD_DIM ** -0.5,
                _to_bhtd(_split_heads(p[1], h_m)), g[0], jax.nn.log_sigmoid(g[1]), g[2], jax.nn.log_sigmoid(g[3]))

    qx, kx, vx, ix_f, lfx_f, ix_b, lfx_b = mlstm_prep(xp)
    qc, kc, vc, ic_f, lfc_f, ic_b, lfc_b = mlstm_prep(cp)
    zero_m = (jnp.zeros((b_sz, h_m, HEAD_DIM, HEAD_DIM), F32),
              jnp.zeros((b_sz, h_m, HEAD_DIM), F32), jnp.zeros((b_sz, h_m), F32))
    hc_f, st_f = _mlstm_chunkwise(qc, kc, vc, ic_f, lfc_f, zero_m)
    hc_b, st_b = _mlstm_chunkwise(_flip_t(qc), _flip_t(kc), _flip_t(vc), _flip_t(ic_b), _flip_t(lfc_b), zero_m)
    hx_f, _ = _mlstm_chunkwise(qx, kx, vx, ix_f, lfx_f, st_f)
    hx_b, _ = _mlstm_chunkwise(_flip_t(qx), _flip_t(kx), _flip_t(vx), _flip_t(ix_b), _flip_t(lfx_b), st_b)

    def mlstm_out(h, o):
        return _merge_heads(_rms_norm(h, m_norm)).astype(o.dtype) * jax.nn.sigmoid(o)

    a_x = mlstm_out(hx_f + _flip_t(hx_b), xp[2])

    def na_prep(p):
        return (_rms_norm(_split_heads(p[4], h_na), q_norm), _rms_norm(_split_heads(p[5], h_na), k_norm),
                _split_heads(p[6], h_na))

    nqx, nkx, nvx = na_prep(xp)
    nqc, nkc, nvc = na_prep(cp)
    row_idx, col_in, dr, dc = na_win
    bias = rpb[:, dr, dc].astype(F32)
    b_x = lax.map(lambda a: _na_latent(a[0], a[1], a[2], a[3], a[4], row_idx, col_in, bias),
                  (nqx, nkx, nvx, nkc, nvc))

    lg_f = _retention_log_decay(5.0, h_r)
    lg_b = _retention_log_decay(6.0, h_r)
    rqx = _axial_rope(_to_bhtd(_split_heads(xp[7], h_r)), *rope)
    rkx = _axial_rope(_to_bhtd(_split_heads(xp[8], h_r)), *rope) * HEAD_DIM ** -0.5
    rvx = _to_bhtd(_split_heads(xp[9], h_r))
    rqc = _to_bhtd(_split_heads(cp[7], h_r))
    rkc = _to_bhtd(_split_heads(cp[8], h_r)) * HEAD_DIM ** -0.5
    rvc = _to_bhtd(_split_heads(cp[9], h_r))
    zero_r = jnp.zeros((b_sz, h_r, HEAD_DIM, HEAD_DIM), F32)
    rc_f, rs_f = _retention_chunkwise(rqc, rkc, rvc, lg_f, zero_r)
    rc_b, rs_b = _retention_chunkwise(_flip_t(rqc), _flip_t(rkc), _flip_t(rvc), lg_b, zero_r)
    rx_f, _ = _retention_chunkwise(rqx, rkx, rvx, lg_f, rs_f)
    rx_b, _ = _retention_chunkwise(_flip_t(rqx), _flip_t(rkx), _flip_t(rvx), lg_b, rs_b)

    def ret_out(h, g):
        return _merge_heads(_rms_norm(h, r_norm)).astype(g.dtype) * jax.nn.silu(g)

    c_x = ret_out(rx_f + _flip_t(rx_b), xp[10])
    mix_x = jnp.concatenate([a_x, b_x, c_x], axis=-1)
    a_c = mlstm_out(hc_f + _flip_t(hc_b), cp[2])
    b_c = _ctx_attention(nqc, nkc, nvc)
    c_c = ret_out(rc_f + _flip_t(rc_b), cp[10])
    return mix_x, jnp.concatenate([a_c, b_c, c_c], axis=-1)


def kernel(x, c, ctx, c_ctx, w_mod, b_mod, norm_mix, norm_ffn, w_in, w_out, mlstm_conv_w, mlstm_conv_b,
           mlstm_gate_b, mlstm_norm, na_q_norm, na_k_norm, na_rpb, ret_norm, router_w, router_b,
           expert_w_up, expert_b_up, expert_w_down, expert_b_down):
    b_sz, t_len, d = x.shape
    t_ctx = ctx.shape[1]
    depth = w_in.shape[0]
    n_e = router_w.shape[2]
    d_mix = w_out.shape[1]
    h_m = d_mix // (4 * HEAD_DIM)
    h_na = d_mix // (2 * HEAD_DIM)
    h_r = d_mix // (4 * HEAD_DIM)
    d_m, d_na, d_r = h_m * HEAD_DIM, h_na * HEAD_DIM, h_r * HEAD_DIM
    n_gate = 4 * h_m
    assert b_sz + 1 <= MOD_ROWS and n_e <= LANES and n_gate <= LANES
    s_len = t_ctx + t_len
    assert s_len % ROW_TILE == 0

    s = jnp.concatenate([ctx, x], axis=1)
    cc = jnp.zeros((MOD_ROWS, d), F32).at[:b_sz].set(c).at[b_sz].set(c_ctx)
    mods = _modulation(cc, w_mod, b_mod).reshape(depth, MOD_ROWS, 1, 6 * d)

    g0 = 4 * d_m
    col_na = g0
    col_ret = col_na + 3 * d_na
    n_hp = h_m // 2
    gate_src = np.array([[g0 + t * h_m + 2 * hp + h2 for t in range(N_GATE_TYPES) for h2 in range(2)]
                         for hp in range(n_hp)])
    rope_tabs = _rope_tables(t_ctx, t_len)
    ret_tabs = _ret_tables(h_r)
    na_tabs = _na_bias_tables(na_rpb)
    f2 = expert_w_up.shape[3]
    w_up_all = expert_w_up.reshape(depth * n_e, d, f2)
    w_down_all = expert_w_down.reshape(depth * n_e, f2 // 2, d)
    b_up_all = _regroup_bias(expert_b_up).reshape(depth * n_e, 1, f2)
    b_down_all = expert_b_down.reshape(depth * n_e, 1, d)

    for l in range(depth):
        w_main = jnp.concatenate([w_in[l, :, :g0], w_in[l, :, g0 + n_gate:]], axis=1).astype(BF16)
        w_gate = jnp.zeros((d, n_hp, LANES), BF16).at[:, :, :gate_src.shape[1]].set(
            w_in[l][:, gate_src].astype(BF16)).reshape(d, n_hp * LANES)
        p_main, p_gate = _proj_in(s, mods[l], norm_mix[l], w_main, w_gate, t_ctx)

        mix = [_mlstm(p_main, p_gate, mlstm_conv_w[l], mlstm_conv_b[l], mlstm_gate_b[l], mlstm_norm[l],
                      t_ctx, h_m),
               _na_attention(p_main, na_q_norm[l], na_k_norm[l], na_tabs[l], t_ctx, col_na, h_na),
               _retention(p_main, rope_tabs, ret_tabs, ret_norm[l], t_ctx, col_ret, h_r)]

        rw = jnp.zeros((d, LANES), F32).at[:, :n_e].set(router_w[l])
        rb = jnp.zeros((1, LANES), F32).at[0, :n_e].set(router_b[l])
        s, tok, logits = _proj_out(mix, s, mods[l], norm_ffn[l], w_out[l].astype(BF16), rw, rb, t_ctx)

        s = _moe(tok, logits[..., :n_e], s, mods[l], t_ctx, n_e, l, w_up_all, w_down_all, b_up_all, b_down_all)
    return s[:, t_ctx:]
```

```python
import functools

import jax
import jax.numpy as jnp
import numpy as np
from jax import lax
from jax.experimental import pallas as pl
from jax.experimental.pallas import tpu as pltpu

F32 = jnp.float32
BF16 = jnp.bfloat16

GRID_W = 64
HEAD_DIM = 64
MLSTM_CHUNK = 64
RET_CHUNK = 64
NA_ROWS = 8
NA_COLS = 16
ROPE_BASE = 10000.0
TOP_K = 4
SWIGLU_ALPHA = 1.702
SWIGLU_LIMIT = 7.0
NORM_EPS = 1e-6
NEG_INF = -1e30

LANES = 128
VMEM_LIMIT = 48 * 1024 * 1024
MOD_ROWS = 8
ROW_TILE = 768
MOE_TILE = 512
MOE_VMEM_LIMIT = 56 * 1024 * 1024


def _cparams(sem):
    return pltpu.CompilerParams(dimension_semantics=sem, vmem_limit_bytes=VMEM_LIMIT)


def _mod_kernel(cc_ref, w_ref, b_ref, o_ref):
    cc = cc_ref[...]
    a = cc * jax.nn.sigmoid(cc)
    o_ref[0] = jnp.dot(a, w_ref[0], precision=lax.Precision.HIGHEST,
                       preferred_element_type=F32) + b_ref[0]


def _modulation(cc, w_mod, b_mod):
    n_l, d, d6 = w_mod.shape
    tn = d6 // 4
    return pl.pallas_call(
        _mod_kernel,
        grid=(n_l, d6 // tn),
        in_specs=[pl.BlockSpec((MOD_ROWS, d), lambda l, j: (0, 0)),
                  pl.BlockSpec((1, d, tn), lambda l, j: (l, 0, j)),
                  pl.BlockSpec((1, 1, tn), lambda l, j: (l, 0, j))],
        out_specs=pl.BlockSpec((1, MOD_ROWS, tn), lambda l, j: (l, 0, j)),
        out_shape=jax.ShapeDtypeStruct((n_l, MOD_ROWS, d6), F32),
        compiler_params=_cparams(("arbitrary", "arbitrary")),
        name="adaln_modulation",
    )(cc, w_mod, b_mod.reshape(n_l, 1, d6))


def _pick_mod(mb_ref, mc_ref, k, d, is_ctx):
    vb = mb_ref[0, :, k * d:(k + 1) * d]
    vc = mc_ref[0, :, k * d:(k + 1) * d]
    return jnp.where(is_ctx, vc, vb)


def _rms(x, g):
    return x * lax.rsqrt(jnp.mean(x * x, axis=-1, keepdims=True) + NORM_EPS) * g


def _proj_in_kernel(s_ref, mb_ref, mc_ref, g_ref, w_ref, wg_ref, pm_ref, pg_ref, xn_ref, *, t_ctx):
    i = pl.program_id(1)
    j = pl.program_id(2)
    tm, d = xn_ref.shape

    @pl.when(j == 0)
    def _():
        row = lax.broadcasted_iota(jnp.int32, (tm, 1), 0) + i * tm
        is_ctx = row < t_ctx
        sh = _pick_mod(mb_ref, mc_ref, 0, d, is_ctx)
        sc = _pick_mod(mb_ref, mc_ref, 1, d, is_ctx)
        h = _rms(s_ref[0], g_ref[...]) * (1.0 + sc) + sh
        xn_ref[...] = h.astype(BF16)
        pg_ref[0] = jnp.dot(xn_ref[...], wg_ref[...], preferred_element_type=F32)

    pm_ref[0] = jnp.dot(xn_ref[...], w_ref[...], preferred_element_type=F32)


def _proj_in(s, mod_l, g, w_main, w_gate, t_ctx):
    b_sz, s_len, d = s.shape
    n_main = w_main.shape[1]
    n_gate = w_gate.shape[1]
    tn = n_main // 4
    tm = ROW_TILE
    return pl.pallas_call(
        functools.partial(_proj_in_kernel, t_ctx=t_ctx),
        grid=(b_sz, s_len // tm, n_main // tn),
        in_specs=[pl.BlockSpec((1, tm, d), lambda b, i, j: (b, i, 0)),
                  pl.BlockSpec((1, 1, 6 * d), lambda b, i, j: (b, 0, 0)),
                  pl.BlockSpec((1, 1, 6 * d), lambda b, i, j: (b_sz, 0, 0)),
                  pl.BlockSpec((1, d), lambda b, i, j: (0, 0)),
                  pl.BlockSpec((d, tn), lambda b, i, j: (0, j)),
                  pl.BlockSpec((d, n_gate), lambda b, i, j: (0, 0))],
        out_specs=[pl.BlockSpec((1, tm, tn), lambda b, i, j: (b, i, j)),
                   pl.BlockSpec((1, tm, n_gate), lambda b, i, j: (b, i, 0))],
        out_shape=[jax.ShapeDtypeStruct((b_sz, s_len, n_main), F32),
                   jax.ShapeDtypeStruct((b_sz, s_len, n_gate), F32)],
        scratch_shapes=[pltpu.VMEM((tm, d), BF16)],
        compiler_params=_cparams(("arbitrary", "arbitrary", "arbitrary")),
        name="proj_in",
    )(s, mod_l, mod_l, g.reshape(1, d), w_main, w_gate)


def _proj_out_kernel(ma_ref, mb2_ref, mc2_ref, s_ref, mb_ref, mc_ref, g_ref, w_ref, rw_ref, rb_ref,
                     so_ref, tok_ref, lg_ref, *, t_ctx):
    i = pl.program_id(1)
    tm, d = s_ref.shape[1], s_ref.shape[2]
    row = lax.broadcasted_iota(jnp.int32, (tm, 1), 0) + i * tm
    is_ctx = row < t_ctx
    g1 = _pick_mod(mb_ref, mc_ref, 2, d, is_ctx)
    ka, kb = ma_ref.shape[2], ma_ref.shape[2] + mb2_ref.shape[2]
    y = (jnp.dot(ma_ref[0], w_ref[0:ka, :], preferred_element_type=F32)
         + jnp.dot(mb2_ref[0], w_ref[ka:kb, :], preferred_element_type=F32)
         + jnp.dot(mc2_ref[0], w_ref[kb:, :], preferred_element_type=F32))
    s_new = s_ref[0] + g1 * y
    so_ref[0] = s_new
    sh = _pick_mod(mb_ref, mc_ref, 3, d, is_ctx)
    sc = _pick_mod(mb_ref, mc_ref, 4, d, is_ctx)
    t = _rms(s_new, g_ref[...]) * (1.0 + sc) + sh
    tok_ref[0] = t
    lg_ref[0] = jnp.dot(t, rw_ref[...], precision=lax.Precision.HIGHEST,
                        preferred_element_type=F32) + rb_ref[...]


def _proj_out(mix_parts, s, mod_l, g, w_out, rw, rb, t_ctx):
    b_sz, s_len, d = s.shape
    tm = ROW_TILE
    row_spec = pl.BlockSpec((1, tm, d), lambda b, i: (b, i, 0))
    part_specs = [pl.BlockSpec((1, tm, m.shape[2]), lambda b, i: (b, i, 0)) for m in mix_parts]
    return pl.pallas_call(
        functools.partial(_proj_out_kernel, t_ctx=t_ctx),
        grid=(b_sz, s_len // tm),
        in_specs=part_specs + [row_spec,
                  pl.BlockSpec((1, 1, 6 * d), lambda b, i: (b, 0, 0)),
                  pl.BlockSpec((1, 1, 6 * d), lambda b, i: (b_sz, 0, 0)),
                  pl.BlockSpec((1, d), lambda b, i: (0, 0)),
                  pl.BlockSpec((d, d), lambda b, i: (0, 0)),
                  pl.BlockSpec((d, LANES), lambda b, i: (0, 0)),
                  pl.BlockSpec((1, LANES), lambda b, i: (0, 0))],
        out_specs=[row_spec, row_spec, pl.BlockSpec((1, tm, LANES), lambda b, i: (b, i, 0))],
        out_shape=[jax.ShapeDtypeStruct((b_sz, s_len, d), F32),
                   jax.ShapeDtypeStruct((b_sz, s_len, d), F32),
                   jax.ShapeDtypeStruct((b_sz, s_len, LANES), F32)],
        compiler_params=_cparams(("arbitrary", "arbitrary")),
        name="proj_out_router",
    )(*mix_parts, s, mod_l, mod_l, g.reshape(1, d), w_out, rw, rb)


PAIR = 2 * LANES


def _regroup_perm():
    dst = np.arange(PAIR)
    src = np.where(dst < LANES, 2 * dst, 2 * (dst - LANES) + 1)
    return jnp.asarray(np.arange(PAIR)[:, None] == src[None, :], BF16)


def _regroup_bias(b_up):
    lead = b_up.shape[:-1]
    b = b_up.reshape(*lead, -1, LANES, 2)
    return jnp.swapaxes(b, -1, -2).reshape(*lead, -1)


def _moe_kernel(be_ref, nu_ref, x_ref, wu_ref, wd_ref, bu_ref, bd_ref, p_ref, y_ref, wus_ref, wds_ref):
    i = pl.program_id(0)
    d, f2 = wus_ref.shape
    rows = min(512, d)

    @pl.when(i < nu_ref[0])
    def _():
        @pl.when((i == 0) | (be_ref[i] != be_ref[jnp.maximum(i - 1, 0)]))
        def _():
            for r in range(d // rows):
                for j in range(f2 // PAIR):
                    w = wu_ref[0, r * rows:(r + 1) * rows, j * PAIR:(j + 1) * PAIR].astype(BF16)
                    wus_ref[r * rows:(r + 1) * rows, j * PAIR:(j + 1) * PAIR] = jnp.dot(
                        w, p_ref[...], preferred_element_type=F32).astype(BF16)
            wds_ref[...] = wd_ref[0].astype(BF16)

        up = jnp.dot(x_ref[...].astype(BF16), wus_ref[...], preferred_element_type=F32) + bu_ref[0]
        acts = []
        for j in range(f2 // PAIR):
            glu = jnp.minimum(up[:, j * PAIR:j * PAIR + LANES], SWIGLU_LIMIT)
            lin = jnp.clip(up[:, j * PAIR + LANES:(j + 1) * PAIR], -SWIGLU_LIMIT, SWIGLU_LIMIT)
            acts.append((glu * jax.nn.sigmoid(SWIGLU_ALPHA * glu) * (lin + 1.0)).astype(BF16))
        act = jnp.concatenate(acts, axis=1)
        y_ref[...] = jnp.dot(act, wds_ref[...], preferred_element_type=F32) + bd_ref[0]

    @pl.when(i >= nu_ref[0])
    def _():
        y_ref[...] = jnp.zeros_like(y_ref)


def _moe_blocks(blk_e, n_used, x_sorted, w_up, w_down, b_up, b_down):
    n_rows, d = x_sorted.shape
    _, _, f2 = w_up.shape
    assert f2 % PAIR == 0 and d % min(512, d) == 0
    tm = MOE_TILE
    wmap = lambda i, be, nu: (be[i], 0, 0)
    return pl.pallas_call(
        _moe_kernel,
        grid_spec=pltpu.PrefetchScalarGridSpec(
            num_scalar_prefetch=2,
            grid=(n_rows // tm,),
            in_specs=[pl.BlockSpec((tm, d), lambda i, be, nu: (i, 0)),
                      pl.BlockSpec((1, d, f2), wmap),
                      pl.BlockSpec((1, f2 // 2, d), wmap),
                      pl.BlockSpec((1, 1, f2), wmap),
                      pl.BlockSpec((1, 1, d), wmap),
                      pl.BlockSpec((PAIR, PAIR), lambda i, be, nu: (0, 0))],
            out_specs=pl.BlockSpec((tm, d), lambda i, be, nu: (i, 0)),
            scratch_shapes=[pltpu.VMEM((d, f2), BF16), pltpu.VMEM((f2 // 2, d), BF16)]),
        out_shape=jax.ShapeDtypeStruct((n_rows, d), F32),
        compiler_params=pltpu.CompilerParams(dimension_semantics=("arbitrary",),
                                             vmem_limit_bytes=MOE_VMEM_LIMIT),
        name="moe_expert_blocks",
    )(blk_e, n_used, x_sorted, w_up, w_down, b_up, b_down, _regroup_perm())


def _combine_kernel(y_ref, gt_ref, s_ref, mb_ref, mc_ref, o_ref, *, t_ctx):
    i = pl.program_id(1)
    tm, d = s_ref.shape[1], s_ref.shape[2]
    row = lax.broadcasted_iota(jnp.int32, (tm, 1), 0) + i * tm
    g2 = _pick_mod(mb_ref, mc_ref, 5, d, row < t_ctx)
    gt = gt_ref[0]
    y = y_ref[0, 0] * gt[:, 0:1]
    for k in range(1, y_ref.shape[0]):
        y = y + y_ref[k, 0] * gt[:, k:k + 1]
    o_ref[0] = s_ref[0] + g2 * y


def _combine(y_as, gates, s, mod_l, t_ctx):
    b_sz, s_len, d = s.shape
    n_k = y_as.shape[0]
    tm = ROW_TILE // 3
    row_spec = pl.BlockSpec((1, tm, d), lambda b, i: (b, i, 0))
    return pl.pallas_call(
        functools.partial(_combine_kernel, t_ctx=t_ctx),
        grid=(b_sz, s_len // tm),
        in_specs=[pl.BlockSpec((n_k, 1, tm, d), lambda b, i: (0, b, i, 0)),
                  pl.BlockSpec((1, tm, n_k), lambda b, i: (b, i, 0)),
                  row_spec,
                  pl.BlockSpec((1, 1, 6 * d), lambda b, i: (b, 0, 0)),
                  pl.BlockSpec((1, 1, 6 * d), lambda b, i: (b_sz, 0, 0))],
        out_specs=row_spec,
        out_shape=jax.ShapeDtypeStruct((b_sz, s_len, d), F32),
        compiler_params=_cparams(("arbitrary", "arbitrary")),
        name="moe_combine",
    )(y_as, gates, s, mod_l, mod_l)


def _moe(tok, logits, s, mod_l, t_ctx, n_e, layer, w_up, w_down, b_up, b_down):
    b_sz, s_len, d = tok.shape
    n_tok = b_sz * s_len
    tm = MOE_TILE
    top_v, top_e = lax.top_k(logits.reshape(n_tok, n_e), TOP_K)
    gates = jax.nn.softmax(top_v, axis=-1)
    n_as = n_tok * TOP_K
    onehot = jnp.sum((top_e[:, :, None] == jnp.arange(n_e)[None, None, :]).astype(jnp.int32), axis=1)
    csum = jnp.cumsum(onehot, axis=0)
    counts = csum[-1]
    rank = jnp.take_along_axis(csum - onehot, top_e, axis=1)
    padded = (counts + tm - 1) // tm * tm
    end_pad = jnp.cumsum(padded)
    start_pad = end_pad - padded
    start = jnp.cumsum(counts) - counts
    dest = (start_pad[top_e] + rank).astype(jnp.int32)
    n_blocks = -(-n_as // tm) + n_e
    blk_e = jnp.minimum(jnp.searchsorted(end_pad, jnp.arange(n_blocks) * tm, side='right'),
                        n_e - 1).astype(jnp.int32)
    n_used = (end_pad[-1:] // tm).astype(jnp.int32)
    tok_sorted = (jnp.argsort(top_e.reshape(n_as)) // TOP_K).astype(jnp.int32)
    row = jnp.arange(n_blocks * tm)
    row_e = jnp.repeat(blk_e, tm)
    j = row - start_pad[row_e]
    src = jnp.clip(start[row_e] + j, 0, n_as - 1)
    row_tok = jnp.where(j < counts[row_e], tok_sorted[src], 0)
    x_sorted = tok.reshape(n_tok, d)[row_tok]
    y = _moe_blocks(blk_e + layer * n_e, n_used, x_sorted, w_up, w_down, b_up, b_down)
    y_as = y[dest.T].reshape(TOP_K, b_sz, s_len, d)
    return _combine(y_as, gates.reshape(b_sz, s_len, TOP_K), s, mod_l, t_ctx)


NA_STEP = 256
NA_KEYS = NA_ROWS * GRID_W


def _head_rms(x, w, lo):
    xx = x * x
    s0 = jnp.sum(jnp.where(lo, xx, 0.0), axis=-1, keepdims=True)
    s1 = jnp.sum(jnp.where(lo, 0.0, xx), axis=-1, keepdims=True)
    inv = lax.rsqrt(jnp.where(lo, s0, s1) * (1.0 / HEAD_DIM) + NORM_EPS)
    return x * inv * w


def _dot_nt(a, b):
    return lax.dot_general(a, b, (((1,), (1,)), ((), ())), preferred_element_type=F32)


def _na_kernel(q_ref, k_ref, v_ref, qw_ref, kw_ref, bias_ref, o_ref, kt_ref, vb_ref, s_ref, *, t_ctx, rows):
    rg = pl.program_id(2)
    s_len = k_ref.shape[1]
    lo = lax.broadcasted_iota(jnp.int32, (1, LANES), 1) < HEAD_DIM
    rows_per_step = NA_STEP // GRID_W

    @pl.when(rg == 0)
    def _():
        def knorm_t(t0, n):
            return _head_rms(k_ref[0, pl.ds(t0, n), :], kw_ref[...], lo).T.astype(BF16)

        def prep(c, carry):
            t0 = pl.multiple_of(c * NA_STEP, NA_STEP)
            kt_ref[0, :, pl.ds(t0, NA_STEP)] = knorm_t(t0, NA_STEP)
            vb_ref[pl.ds(t0, NA_STEP), :] = jnp.concatenate(
                [v_ref[0, pl.ds(t0, NA_STEP), :].astype(BF16), jnp.ones((NA_STEP, LANES), BF16)], axis=1)
            return carry
        lax.fori_loop(0, s_len // NA_STEP, prep, 0, unroll=3)

        def prep_shifted(c, carry):
            t0 = pl.multiple_of(c * LANES, LANES)
            kt_ref[1, :, pl.ds(t0, LANES)] = knorm_t(pl.multiple_of(t0 + GRID_W, GRID_W), LANES)
            return carry
        lax.fori_loop(0, (s_len - GRID_W) // LANES, prep_shifted, 0, unroll=5)

    qn = _head_rms(q_ref[0], qw_ref[...], lo) * (HEAD_DIM ** -0.5)
    q0 = jnp.where(lo, qn, 0.0).astype(BF16)
    q1 = jnp.where(lo, 0.0, qn).astype(BF16)
    kc_t = kt_ref[0, :, 0:t_ctx]
    vc = vb_ref[0:t_ctx, :]

    def finish(g, o2):
        o2 = o2[:, :LANES] * (1.0 / o2[:, LANES:])
        o = jnp.where(lo, o2[:GRID_W], o2[GRID_W:])
        o_ref[0, g * GRID_W:(g + 1) * GRID_W, :] = o.astype(o_ref.dtype)

    @pl.when(rg == 0)
    def _():
        for g in range(rows_per_step):
            q2 = jnp.concatenate([q0[g * GRID_W:(g + 1) * GRID_W], q1[g * GRID_W:(g + 1) * GRID_W]], axis=0)
            s_c = jnp.dot(q2, kc_t, preferred_element_type=F32)
            p_c = jnp.exp(s_c - jnp.max(s_c, axis=-1, keepdims=True))
            finish(g, jnp.dot(p_c.astype(BF16), vc, preferred_element_type=F32))

    @pl.when(rg > 0)
    def _():
        t0s = []
        for g in range(rows_per_step):
            r = (rg - 1) * rows_per_step + g
            row_start = jnp.clip(r - NA_ROWS // 2, 0, rows - NA_ROWS)
            dr0 = row_start - r + NA_ROWS - 1
            t0 = pl.multiple_of(t_ctx + row_start * GRID_W, GRID_W)
            odd = (t0 // GRID_W) % (LANES // GRID_W)
            kw_t = kt_ref[odd, :, pl.ds(pl.multiple_of(t0 - odd * GRID_W, LANES), NA_KEYS)]
            q2 = jnp.concatenate([q0[g * GRID_W:(g + 1) * GRID_W], q1[g * GRID_W:(g + 1) * GRID_W]], axis=0)
            s_ref[g, :, :NA_KEYS] = jnp.dot(q2, kw_t, preferred_element_type=F32) + bias_ref[dr0, 0]
            s_ref[g, :, NA_KEYS:] = jnp.dot(q2, kc_t, preferred_element_type=F32)
            t0s.append(t0)
        for g in range(rows_per_step):
            s = s_ref[g]
            p = jnp.exp(s - jnp.max(s, axis=-1, keepdims=True)).astype(BF16)
            finish(g, jnp.dot(p[:, :NA_KEYS], vb_ref[pl.ds(t0s[g], NA_KEYS), :], preferred_element_type=F32)
                   + jnp.dot(p[:, NA_KEYS:], vc, preferred_element_type=F32))


def _na_bias_tables(rpb):
    n_l, n_h, n_dr, n_dc = rpb.shape
    col = np.arange(GRID_W)
    dc = np.clip(col[None, :] - col[:, None] + NA_COLS - 1, 0, n_dc - 1)
    onehot = (dc.reshape(1, -1) == np.arange(n_dc)[:, None]).astype(np.float32)
    toe = jnp.dot(rpb.reshape(-1, n_dc), jnp.asarray(onehot), precision=lax.Precision.HIGHEST)
    toe = toe.reshape(n_l, n_h, n_dr, GRID_W, GRID_W)
    col_start = np.clip(col - NA_COLS // 2, 0, GRID_W - NA_COLS)
    col_in = (col[None, :] >= col_start[:, None]) & (col[None, :] < col_start[:, None] + NA_COLS)
    toe = jnp.where(jnp.asarray(col_in), toe, NEG_INF)
    tabs = []
    for dr0 in range(NA_ROWS):
        t = toe[:, :, dr0:dr0 + NA_ROWS].reshape(n_l, n_h // 2, 2, NA_ROWS, GRID_W, GRID_W)
        t = jnp.transpose(t, (0, 1, 2, 4, 3, 5))
        tabs.append(t.reshape(n_l, n_h // 2, 2 * GRID_W, NA_KEYS))
    return jnp.stack(tabs, axis=1)


def _na_attention(p_main, q_w, k_w, bias_tab, t_ctx, col_q, n_heads):
    b_sz, s_len, _ = p_main.shape
    n_hg = n_heads // 2
    rows = (s_len - t_ctx) // GRID_W
    assert t_ctx == NA_STEP and s_len % NA_STEP == 0 and rows >= NA_ROWS and rows % 2 == 0
    cq, ck, cv = col_q // LANES, col_q // LANES + n_hg, col_q // LANES + 2 * n_hg
    w2 = lambda w: jnp.concatenate([w, w]).reshape(1, LANES)
    return pl.pallas_call(
        functools.partial(_na_kernel, t_ctx=t_ctx, rows=rows),
        grid=(b_sz, n_hg, s_len // NA_STEP),
        in_specs=[pl.BlockSpec((1, NA_STEP, LANES), lambda b, h, r: (b, r, cq + h)),
                  pl.BlockSpec((1, s_len, LANES), lambda b, h, r: (b, 0, ck + h)),
                  pl.BlockSpec((1, s_len, LANES), lambda b, h, r: (b, 0, cv + h)),
                  pl.BlockSpec((1, LANES), lambda b, h, r: (0, 0)),
                  pl.BlockSpec((1, LANES), lambda b, h, r: (0, 0)),
                  pl.BlockSpec((NA_ROWS, 1, 2 * GRID_W, NA_KEYS), lambda b, h, r: (0, h, 0, 0))],
        out_specs=pl.BlockSpec((1, NA_STEP, LANES), lambda b, h, r: (b, r, h)),
        out_shape=jax.ShapeDtypeStruct((b_sz, s_len, n_heads * HEAD_DIM), BF16),
        scratch_shapes=[pltpu.VMEM((2, LANES, s_len), BF16), pltpu.VMEM((s_len, 2 * LANES), BF16),
                        pltpu.VMEM((NA_STEP // GRID_W, 2 * GRID_W, NA_KEYS + t_ctx), F32)],
        compiler_params=_cparams(("arbitrary", "arbitrary", "arbitrary")),
        name="na_attention",
    )(p_main, p_main, p_main, w2(q_w), w2(k_w), bias_tab)


RET_CHUNK_LEN = 256
MLSTM_CHUNK_LEN = 128


def _reverse_chunk(i, nc, n_ctx):
    return jnp.where(i < n_ctx, n_ctx - 1 - i, nc - 1 - i + n_ctx)


def _dot_tn(a, b):
    return lax.dot_general(a, b, (((0,), (0,)), ((), ())), preferred_element_type=F32)


def _rope_tables(t_ctx, t_len):
    pos = jnp.arange(t_len)
    row = (pos // GRID_W).astype(F32)
    col = (pos % GRID_W).astype(F32)
    n = HEAD_DIM // 4
    inv = ROPE_BASE ** (-jnp.arange(n, dtype=F32) / n)
    ar = row[:, None] * inv
    ac = col[:, None] * inv
    cos = jnp.concatenate([jnp.cos(ar), jnp.cos(ar), jnp.cos(ac), jnp.cos(ac)], axis=-1)
    sin = jnp.concatenate([-jnp.sin(ar), jnp.sin(ar), -jnp.sin(ac), jnp.sin(ac)], axis=-1)
    cos = jnp.concatenate([jnp.ones((t_ctx, HEAD_DIM), F32), cos], axis=0)
    sin = jnp.concatenate([jnp.zeros((t_ctx, HEAD_DIM), F32), sin], axis=0)
    return jnp.tile(cos, (1, 2)), jnp.tile(sin, (1, 2))


def _ret_tables(n_heads):
    L = RET_CHUNK_LEN
    pos = np.arange(L, dtype=np.float32)
    lane_head = np.arange(LANES) // HEAD_DIM
    decay = np.zeros((2, n_heads, L, L), np.float32)
    zeta = np.zeros((2, n_heads // 2, L, LANES), np.float32)
    xi = np.zeros((2, n_heads // 2, L, LANES), np.float32)
    gch = np.zeros((2, n_heads // 2, 1, LANES), np.float32)
    for d, first_exp in enumerate((5.0, 6.0)):
        e = np.float32(first_exp) + np.float32(2.0) * np.arange(n_heads, dtype=np.float32)
        lg = np.log1p(-np.exp2(-e)).astype(np.float32)
        diff = pos[:, None] - pos[None, :]
        if d == 1:
            diff = -diff
        for h in range(n_heads):
            decay[d, h] = np.where(diff >= 0, np.exp(lg[h] * np.maximum(diff, 0.0)), 0.0)
        for hp in range(n_heads // 2):
            lgl = lg[2 * hp + lane_head][None, :]
            to_end = (L - 1 - pos if d == 0 else pos)[:, None]
            zeta[d, hp] = np.exp(lgl * to_end)
            xi[d, hp] = np.exp(lgl * (L - to_end))
            gch[d, hp] = np.exp(lgl * L)
    return tuple(jnp.asarray(a) for a in (decay, zeta, xi, gch))


def _ret_kernel(q_ref, k_ref, v_ref, g_ref, cos_ref, sin_ref, dec_ref, zeta_ref, xi_ref, gch_ref, rn_ref,
                o_ref, qr_ref, kr_ref, vb_ref, acc_ref, *, t_ctx):
    L = RET_CHUNK_LEN
    s_len = q_ref.shape[1]
    nc = s_len // L
    lane = lax.broadcasted_iota(jnp.int32, (1, LANES), 1)
    lo = lane < HEAD_DIM
    half = (lane & (HEAD_DIM // 4)) == 0
    rid = lax.broadcasted_iota(jnp.int32, (LANES, LANES), 0) < HEAD_DIM
    cid = lax.broadcasted_iota(jnp.int32, (LANES, LANES), 1) < HEAD_DIM
    same_head = rid == cid

    def rope(x, cos, sin):
        up = pltpu.roll(x, LANES - HEAD_DIM // 4, axis=1)
        dn = pltpu.roll(x, HEAD_DIM // 4, axis=1)
        return x * cos + jnp.where(half, up, dn) * sin

    def prep(c, carry):
        sl = pl.ds(pl.multiple_of(c * L, L), L)
        cos, sin = cos_ref[sl, :], sin_ref[sl, :]
        qr_ref[sl, :] = rope(q_ref[0, sl, :], cos, sin).astype(BF16)
        kr_ref[sl, :] = (rope(k_ref[0, sl, :], cos, sin) * HEAD_DIM ** -0.5).astype(BF16)
        vb_ref[sl, :] = v_ref[0, sl, :].astype(BF16)
        return carry
    lax.fori_loop(0, nc, prep, 0)

    def chunk(d, c, state):
        sl = pl.ds(pl.multiple_of(c * L, L), L)
        q, k, v = qr_ref[sl, :], kr_ref[sl, :], vb_ref[sl, :]
        inter = jnp.dot(q, state.astype(BF16), preferred_element_type=F32) * xi_ref[d, 0]
        outs = []
        for h2 in range(2):
            qm = jnp.where(lo if h2 == 0 else jnp.logical_not(lo), q, jnp.zeros_like(q))
            sd = (_dot_nt(qm, k) * dec_ref[d, h2]).astype(BF16)
            outs.append(jnp.dot(sd, v, preferred_element_type=F32))
        y = jnp.where(lo, outs[0], outs[1]) + inter
        kz = (k.astype(F32) * zeta_ref[d, 0]).astype(BF16)
        state = state * gch_ref[d, 0] + jnp.where(same_head, _dot_tn(kz, v), 0.0)
        return sl, y, state

    def fwd(c, state):
        sl, y, state = chunk(0, c, state)
        acc_ref[sl, :] = y
        return state
    lax.fori_loop(0, nc, fwd, jnp.zeros((LANES, LANES), F32))

    def bwd(i, state):
        c = _reverse_chunk(i, nc, t_ctx // L)
        sl, y, state = chunk(1, c, state)
        y = _head_rms(y + acc_ref[sl, :], rn_ref[...], lo)
        g = g_ref[0, sl, :]
        o_ref[0, sl, :] = (y * (g * jax.nn.sigmoid(g))).astype(o_ref.dtype)
        return state
    lax.fori_loop(0, nc, bwd, jnp.zeros((LANES, LANES), F32))


def _seq_spec(s_len, col):
    return pl.BlockSpec((1, s_len, LANES), lambda b, h: (b, 0, col + h), pipeline_mode=pl.Buffered(1))


def _retention(p_main, rope_tabs, ret_tabs, r_w, t_ctx, col_q, n_heads):
    b_sz, s_len, _ = p_main.shape
    n_hp = n_heads // 2
    L = RET_CHUNK_LEN
    assert t_ctx % L == 0 and s_len % L == 0
    c0 = col_q // LANES
    cos, sin = rope_tabs
    decay, zeta, xi, gch = ret_tabs
    const2 = pl.BlockSpec((s_len, LANES), lambda b, h: (0, 0), pipeline_mode=pl.Buffered(1))
    return pl.pallas_call(
        functools.partial(_ret_kernel, t_ctx=t_ctx),
        grid=(b_sz, n_hp),
        in_specs=[_seq_spec(s_len, c0), _seq_spec(s_len, c0 + n_hp), _seq_spec(s_len, c0 + 2 * n_hp),
                  _seq_spec(s_len, c0 + 3 * n_hp), const2, const2,
                  pl.BlockSpec((2, 2, L, L), lambda b, h: (0, h, 0, 0)),
                  pl.BlockSpec((2, 1, L, LANES), lambda b, h: (0, h, 0, 0)),
                  pl.BlockSpec((2, 1, L, LANES), lambda b, h: (0, h, 0, 0)),
                  pl.BlockSpec((2, 1, 1, LANES), lambda b, h: (0, h, 0, 0)),
                  pl.BlockSpec((1, LANES), lambda b, h: (0, 0))],
        out_specs=pl.BlockSpec((1, s_len, LANES), lambda b, h: (b, 0, h)),
        out_shape=jax.ShapeDtypeStruct((b_sz, s_len, n_heads * HEAD_DIM), BF16),
        scratch_shapes=[pltpu.VMEM((s_len, LANES), BF16), pltpu.VMEM((s_len, LANES), BF16),
                        pltpu.VMEM((s_len, LANES), BF16), pltpu.VMEM((s_len, LANES), F32)],
        compiler_params=_cparams(("arbitrary", "arbitrary")),
        name="retention",
    )(p_main, p_main, p_main, p_main, cos, sin, decay, zeta, xi, gch,
      jnp.concatenate([r_w, r_w]).reshape(1, LANES))


N_GATE_TYPES = 4


def _log_sigmoid(x):
    return jnp.minimum(x, 0.0) - jnp.log1p(jnp.exp(-jnp.abs(x)))


def _mlstm_kernel(q_ref, k_ref, v_ref, og_ref, gc_ref, gr_ref, wq_ref, wk_ref, bq_ref, bk_ref, gbc_ref, gbr_ref,
                  mn_ref, o_ref, qc_ref, kc_ref, vb_ref, acc_ref, *, t_ctx):
    L = MLSTM_CHUNK_LEN
    s_len = q_ref.shape[1]
    nc = s_len // L
    lane = lax.broadcasted_iota(jnp.int32, (1, LANES), 1)
    lo = lane < HEAD_DIM
    head_lanes = (lo, jnp.logical_not(lo))
    rid = lax.broadcasted_iota(jnp.int32, (LANES, LANES), 0) < HEAD_DIM
    cid = lax.broadcasted_iota(jnp.int32, (LANES, LANES), 1) < HEAD_DIM
    head_block = (rid & cid, jnp.logical_not(rid | cid))
    row_i = lax.broadcasted_iota(jnp.int32, (L, L), 0)
    col_i = lax.broadcasted_iota(jnp.int32, (L, L), 1)
    causal = (row_i >= col_i, row_i <= col_i)
    tri = tuple(c.astype(F32) for c in causal)
    tri_t = (tri[1], tri[0])
    sub = lax.broadcasted_iota(jnp.int32, (L, 1), 0)

    def conv(x_ref, w_ref, b_ref, t0):
        x = x_ref[0, pl.ds(t0, L), :]
        prev = x_ref[0, pl.ds(jnp.maximum(t0 - 8, 0), 8), :][7:8]
        nxt = x_ref[0, pl.ds(jnp.minimum(t0 + L, s_len - 8), 8), :][0:1]
        prev = jnp.where((t0 != 0) & (t0 != t_ctx), prev, 0.0)
        nxt = jnp.where((t0 + L != t_ctx) & (t0 + L != s_len), nxt, 0.0)
        xm = jnp.where(sub == 0, prev, pltpu.roll(x, 1, axis=0))
        xp = jnp.where(sub == L - 1, nxt, pltpu.roll(x, L - 1, axis=0))
        y = b_ref[...] + xm * w_ref[0:1, :] + x * w_ref[1:2, :] + xp * w_ref[2:3, :]
        return y * jax.nn.sigmoid(y)

    def prep(c, carry):
        t0 = pl.multiple_of(c * L, L)
        sl = pl.ds(t0, L)
        qc_ref[sl, :] = conv(q_ref, wq_ref, bq_ref, t0).astype(BF16)
        kc_ref[sl, :] = (conv(k_ref, wk_ref, bk_ref, t0) * HEAD_DIM ** -0.5).astype(BF16)
        vb_ref[sl, :] = v_ref[0, sl, :].astype(BF16)
        return carry
    lax.fori_loop(0, nc, prep, 0)

    def chunk(d, c, state):
        sl = pl.ds(pl.multiple_of(c * L, L), L)
        q, k, v = qc_ref[sl, :], kc_ref[sl, :], vb_ref[sl, :]
        g_col = gc_ref[0, sl, :] + gbc_ref[0]
        g_row = gr_ref[0, 0, :, sl] + gbr_ref[0]
        cum_col = jnp.dot(tri[d], _log_sigmoid(g_col), precision=lax.Precision.HIGHEST,
                          preferred_element_type=F32)
        cum_row = jnp.dot(_log_sigmoid(g_row), tri_t[d], precision=lax.Precision.HIGHEST,
                          preferred_element_type=F32)
        end = L - 1 if d == 0 else 0
        outs, new_state = [], []
        for h2 in range(2):
            c_st, n_st, m_st = state[h2]
            ci, cf = 2 * (2 * d) + h2, 2 * (2 * d + 1) + h2
            i_col, a_col = g_col[:, ci:ci + 1], cum_col[:, cf:cf + 1]
            i_row, a_row = g_row[ci:ci + 1, :], cum_row[cf:cf + 1, :]
            b_tot = a_row[:, end:end + 1]
            d_log = jnp.where(causal[d], a_col + (i_row - a_row), NEG_INF)
            m_intra = jnp.max(d_log, axis=-1, keepdims=True)
            qm = jnp.where(head_lanes[h2], q, jnp.zeros_like(q))
            s = _dot_nt(qm, k) * jnp.exp(d_log - m_intra)
            num_intra = jnp.dot(s.astype(BF16), v, preferred_element_type=F32)
            den_intra = jnp.sum(s, axis=-1, keepdims=True)
            inter_log = a_col + m_st
            m_q = jnp.maximum(m_intra, inter_log)
            a = jnp.exp(inter_log - m_q)
            g = jnp.exp(m_intra - m_q)
            num = a * jnp.dot(qm, c_st.astype(BF16), preferred_element_type=F32) + g * num_intra
            den = a * jnp.sum(qm.astype(F32) * n_st, axis=-1, keepdims=True) + g * den_intra
            outs.append(num / jnp.maximum(jnp.abs(den), jnp.exp(-m_q)))
            w_log = b_tot - a_col + i_col
            m_loc = jnp.max(w_log, axis=0, keepdims=True)
            ke = jnp.where(head_lanes[h2], k.astype(F32) * jnp.exp(w_log - m_loc), 0.0)
            c_loc = jnp.where(head_block[h2], _dot_tn(ke.astype(BF16), v), 0.0)
            n_loc = jnp.sum(ke, axis=0, keepdims=True)
            m_new = jnp.maximum(b_tot + m_st, m_loc)
            a_s = jnp.exp(b_tot + m_st - m_new)
            g_s = jnp.exp(m_loc - m_new)
            new_state.append((a_s * c_st + g_s * c_loc, a_s * n_st + g_s * n_loc, m_new))
        return sl, jnp.where(lo, outs[0], outs[1]), tuple(new_state)

    zero = tuple((jnp.zeros((LANES, LANES), F32), jnp.zeros((1, LANES), F32), jnp.zeros((1, 1), F32))
                 for _ in range(2))

    def fwd(c, state):
        sl, y, state = chunk(0, c, state)
        acc_ref[sl, :] = y
        return state
    lax.fori_loop(0, nc, fwd, zero)

    def bwd(i, state):
        c = _reverse_chunk(i, nc, t_ctx // L)
        sl, y, state = chunk(1, c, state)
        y = _head_rms(y + acc_ref[sl, :], mn_ref[...], lo)
        o_ref[0, sl, :] = (y * jax.nn.sigmoid(og_ref[0, sl, :])).astype(o_ref.dtype)
        return state
    lax.fori_loop(0, nc, bwd, zero)


def _mlstm_pair_kernel(q_ref, k_ref, v_ref, og_ref, gc_ref, gr_ref, wq_ref, wk_ref, bq_ref, bk_ref, gbc_ref, gbr_ref,
                       mn_ref, o_ref, qc_ref, kt_ref, vb_ref, acc_ref, st_ref, cc_ref, cr_ref, *, t_ctx):
    L = MLSTM_CHUNK_LEN
    s_len = q_ref.shape[1]
    nc = s_len // L
    lane = lax.broadcasted_iota(jnp.int32, (1, LANES), 1)
    lo = lane < HEAD_DIM
    head_lanes = (lo, jnp.logical_not(lo))
    sub_lo = lax.broadcasted_iota(jnp.int32, (LANES, 1), 0) < HEAD_DIM
    head_rows = (sub_lo, jnp.logical_not(sub_lo))
    row_i = lax.broadcasted_iota(jnp.int32, (L, L), 0)
    col_i = lax.broadcasted_iota(jnp.int32, (L, L), 1)
    causal = (row_i >= col_i, row_i <= col_i)
    tri = tuple(c.astype(F32) for c in causal)
    tri_t = (tri[1], tri[0])
    sub = lax.broadcasted_iota(jnp.int32, (L, 1), 0)
    gate_row = lax.broadcasted_iota(jnp.int32, (2 * N_GATE_TYPES, 1), 0)
    ones = jnp.ones((L, LANES), BF16)

    def conv(x_ref, w_ref, b_ref, t0):
        x = x_ref[0, pl.ds(t0, L), :]
        prev = x_ref[0, pl.ds(jnp.maximum(t0 - 8, 0), 8), :][7:8]
        nxt = x_ref[0, pl.ds(jnp.minimum(t0 + L, s_len - 8), 8), :][0:1]
        prev = jnp.where((t0 != 0) & (t0 != t_ctx), prev, 0.0)
        nxt = jnp.where((t0 + L != t_ctx) & (t0 + L != s_len), nxt, 0.0)
        xm = jnp.where(sub == 0, prev, pltpu.roll(x, 1, axis=0))
        xp = jnp.where(sub == L - 1, nxt, pltpu.roll(x, L - 1, axis=0))
        y = b_ref[...] + xm * w_ref[0:1, :] + x * w_ref[1:2, :] + xp * w_ref[2:3, :]
        return y * jax.nn.sigmoid(y)

    def prep(c, carry):
        t0 = pl.multiple_of(c * L, L)
        sl = pl.ds(t0, L)
        qc_ref[sl, :] = conv(q_ref, wq_ref, bq_ref, t0).astype(BF16)
        kt_ref[:, sl] = (conv(k_ref, wk_ref, bk_ref, t0) * HEAD_DIM ** -0.5).T.astype(BF16)
        vb_ref[sl, :] = v_ref[0, sl, :].astype(BF16)
        lf_col = _log_sigmoid(gc_ref[0, sl, :] + gbc_ref[0])
        lf_row = _log_sigmoid(gr_ref[0, 0, :, sl] + gbr_ref[0])
        hdot = functools.partial(jnp.dot, precision=lax.Precision.HIGHEST, preferred_element_type=F32)
        cc_ref[sl, :] = jnp.where(lane < N_GATE_TYPES, hdot(tri[0], lf_col), hdot(tri[1], lf_col))
        cr_ref[:, sl] = jnp.where(gate_row < N_GATE_TYPES, hdot(lf_row, tri_t[0]), hdot(lf_row, tri_t[1]))
        return carry
    lax.fori_loop(0, nc, prep, 0, unroll=2)

    def chunk(d, c, m_state):
        sl = pl.ds(pl.multiple_of(c * L, L), L)
        q, kt, v = qc_ref[sl, :], kt_ref[:, sl], vb_ref[sl, :]
        g_row = gr_ref[0, 0, :, sl] + gbr_ref[0]
        cum_col, cum_row = cc_ref[sl, :], cr_ref[:, sl]
        end = L - 1 if d == 0 else 0
        outs, new_m = [], []
        for h2 in range(2):
            m_st = m_state[h2]
            ci, cf = 2 * (2 * d) + h2, 2 * (2 * d + 1) + h2
            a_rep = jnp.broadcast_to(cum_col[:, cf:cf + 1], (L, LANES))
            i_row, a_row = g_row[ci:ci + 1, :], cum_row[cf:cf + 1, :]
            b_tot = a_row[:, end:end + 1]
            d_log = jnp.where(causal[d], a_rep + (i_row - a_row), NEG_INF)
            m_intra = jnp.broadcast_to(jnp.max(d_log, axis=-1, keepdims=True), (L, LANES))
            qm = jnp.where(head_lanes[h2], q, jnp.zeros_like(q))
            s = jnp.dot(qm, kt, preferred_element_type=F32) * jnp.exp(d_log - m_intra)
            vm = jnp.where(head_lanes[h2], v, jnp.zeros_like(v))
            v1 = jnp.concatenate([vm, ones], axis=1)
            intra = jnp.dot(s.astype(BF16), v1, preferred_element_type=F32)
            inter = jnp.dot(qm, st_ref[h2].astype(BF16), preferred_element_type=F32)
            inter_log = a_rep + m_st
            m_q = jnp.maximum(m_intra, inter_log)
            a = jnp.exp(inter_log - m_q)
            g = jnp.exp(m_intra - m_q)
            num = a * inter[:, :LANES] + g * intra[:, :LANES]
            den = a * inter[:, LANES:] + g * intra[:, LANES:]
            outs.append(num / jnp.maximum(jnp.abs(den), jnp.exp(-m_q)))
            w_row = b_tot - a_row + i_row
            m_loc = jnp.max(w_row, axis=-1, keepdims=True)
            kts = jnp.where(head_rows[h2], kt.astype(F32) * jnp.exp(w_row - m_loc), 0.0).astype(BF16)
            loc = jnp.dot(kts, v1, preferred_element_type=F32)
            m_new = jnp.maximum(b_tot + m_st, m_loc)
            st_ref[h2] = jnp.exp(b_tot + m_st - m_new) * st_ref[h2] + jnp.exp(m_loc - m_new) * loc
            new_m.append(m_new)
        return sl, jnp.where(lo, outs[0], outs[1]), tuple(new_m)

    zero_m = (jnp.zeros((1, 1), F32), jnp.zeros((1, 1), F32))

    st_ref[...] = jnp.zeros_like(st_ref)

    def fwd(c, m_state):
        sl, y, m_state = chunk(0, c, m_state)
        acc_ref[sl, :] = y
        return m_state
    lax.fori_loop(0, nc, fwd, zero_m, unroll=2)

    st_ref[...] = jnp.zeros_like(st_ref)

    def bwd(i, m_state):
        c = _reverse_chunk(i, nc, t_ctx // L)
        sl, y, m_state = chunk(1, c, m_state)
        y = _head_rms(y + acc_ref[sl, :], mn_ref[...], lo)
        o_ref[0, sl, :] = (y * jax.nn.sigmoid(og_ref[0, sl, :])).astype(o_ref.dtype)
        return m_state
    lax.fori_loop(0, nc, bwd, zero_m, unroll=2)


def _mlstm(p_main, p_gate, conv_w, conv_b, gate_b, m_w, t_ctx, n_heads):
    b_sz, s_len, _ = p_main.shape
    n_hp = n_heads // 2
    L = MLSTM_CHUNK_LEN
    assert L == LANES and t_ctx % L == 0 and s_len % L == 0
    n_g = 2 * N_GATE_TYPES
    g_rows = jnp.transpose(p_gate.reshape(b_sz, s_len, n_hp, LANES)[..., :n_g], (0, 2, 3, 1))
    gb = jnp.transpose(gate_b.reshape(N_GATE_TYPES, n_hp, 2), (1, 0, 2)).reshape(n_hp, n_g)
    gb_col = jnp.zeros((n_hp, 1, LANES), F32).at[:, 0, :n_g].set(gb)
    gb_row = gb.reshape(n_hp, n_g, 1)
    vec = lambda col: pl.BlockSpec((1, LANES), lambda b, h: (0, col + h))
    return pl.pallas_call(
        functools.partial(_mlstm_pair_kernel, t_ctx=t_ctx),
        grid=(b_sz, n_hp),
        in_specs=[_seq_spec(s_len, 0), _seq_spec(s_len, n_hp), _seq_spec(s_len, 2 * n_hp),
                  _seq_spec(s_len, 3 * n_hp),
                  pl.BlockSpec((1, s_len, LANES), lambda b, h: (b, 0, h), pipeline_mode=pl.Buffered(1)),
                  pl.BlockSpec((1, 1, n_g, s_len), lambda b, h: (b, h, 0, 0)),
                  pl.BlockSpec((3, LANES), lambda b, h: (0, h)),
                  pl.BlockSpec((3, LANES), lambda b, h: (0, n_hp + h)),
                  vec(0), vec(n_hp),
                  pl.BlockSpec((1, 1, LANES), lambda b, h: (h, 0, 0)),
                  pl.BlockSpec((1, n_g, 1), lambda b, h: (h, 0, 0)),
                  pl.BlockSpec((1, LANES), lambda b, h: (0, 0))],
        out_specs=pl.BlockSpec((1, s_len, LANES), lambda b, h: (b, 0, h)),
        out_shape=jax.ShapeDtypeStruct((b_sz, s_len, n_heads * HEAD_DIM), BF16),
        scratch_shapes=[pltpu.VMEM((s_len, LANES), BF16), pltpu.VMEM((LANES, s_len), BF16),
                        pltpu.VMEM((s_len, LANES), BF16), pltpu.VMEM((s_len, LANES), F32),
                        pltpu.VMEM((2, LANES, 2 * LANES), F32),
                        pltpu.VMEM((s_len, LANES), F32), pltpu.VMEM((n_g, s_len), F32)],
        compiler_params=_cparams(("arbitrary", "arbitrary")),
        name="mlstm",
    )(p_main, p_main, p_main, p_main, p_gate, g_rows, conv_w, conv_w, conv_b.reshape(1, -1),
      conv_b.reshape(1, -1), gb_col, gb_row, jnp.concatenate([m_w, m_w]).reshape(1, LANES))


def _rms_norm(x, g):
    xf = x.astype(F32)
    y = xf * lax.rsqrt(jnp.mean(xf * xf, axis=-1, keepdims=True) + NORM_EPS)
    return (y * g.astype(F32)).astype(x.dtype)


def _split_heads(t, n_heads):
    return t.reshape(t.shape[0], t.shape[1], n_heads, HEAD_DIM)


def _to_bhtd(t):
    return jnp.transpose(t, (0, 2, 1, 3))


def _merge_heads(t):
    b, h, t_len, d = t.shape
    return jnp.transpose(t, (0, 2, 1, 3)).reshape(b, t_len, h * d)


def _flip_t(t):
    return jnp.flip(t, axis=2)


def _dwconv_centred(x, w, b):
    k_size = w.shape[0]
    pad = k_size // 2
    t_len = x.shape[1]
    xp = jnp.pad(x, ((0, 0), (pad, pad), (0, 0)))
    y = b
    for i in range(k_size):
        y = y + xp[:, i:i + t_len] * w[i]
    return y


def _axial_rope_tables(t_len):
    pos = jnp.arange(t_len)
    row = (pos // GRID_W).astype(F32)
    col = (pos % GRID_W).astype(F32)
    n = HEAD_DIM // 4
    inv = ROPE_BASE ** (-jnp.arange(n, dtype=F32) / n)
    ar = row[:, None] * inv
    ac = col[:, None] * inv
    return (jnp.cos(ar), jnp.sin(ar), jnp.cos(ac), jnp.sin(ac))


def _rotate_half(x, cos, sin):
    x1, x2 = jnp.split(x, 2, axis=-1)
    return jnp.concatenate([x1 * cos - x2 * sin, x2 * cos + x1 * sin], axis=-1)


def _axial_rope(x, cos_r, sin_r, cos_c, sin_c):
    xr, xc = jnp.split(x, 2, axis=-1)
    return jnp.concatenate([_rotate_half(xr, cos_r, sin_r), _rotate_half(xc, cos_c, sin_c)], axis=-1)


def _mlstm_chunkwise(q, k, v, log_i, log_f, state0):
    b_sz, h_sz, t_len, d = q.shape
    L = MLSTM_CHUNK
    nc = t_len // L
    qc = q.reshape(b_sz, h_sz, nc, L, d)
    kc = k.reshape(b_sz, h_sz, nc, L, d)
    vc = v.reshape(b_sz, h_sz, nc, L, d)
    li = log_i.reshape(b_sz, h_sz, nc, L)
    bcum = jnp.cumsum(log_f.reshape(b_sz, h_sz, nc, L), axis=-1)
    b_tot = bcum[..., -1]
    tri = jnp.tril(jnp.ones((L, L), dtype=bool))
    d_log = jnp.where(tri, bcum[..., :, None] - bcum[..., None, :] + li[..., None, :], NEG_INF)
    m_intra = jnp.max(d_log, axis=-1)
    s = jnp.einsum('bhcjd,bhcld->bhcjl', qc, kc).astype(F32) * jnp.exp(d_log - m_intra[..., None])
    num_intra = jnp.einsum('bhcjl,bhcld->bhcjd', s, vc)
    den_intra = jnp.sum(s, axis=-1)
    w_log = b_tot[..., None] - bcum + li
    m_loc = jnp.max(w_log, axis=-1)
    e = jnp.exp(w_log - m_loc[..., None])
    c_loc = jnp.einsum('bhcl,bhcld,bhcle->bhcde', e, kc, vc)
    n_loc = jnp.einsum('bhcl,bhcld->bhcd', e, kc)

    def step(carry, inp):
        c_st, n_st, m_st = carry
        cl, nl, ml, bt = inp
        m_new = jnp.maximum(bt + m_st, ml)
        a = jnp.exp(bt + m_st - m_new)
        g = jnp.exp(ml - m_new)
        new = (a[..., None, None] * c_st + g[..., None, None] * cl, a[..., None] * n_st + g[..., None] * nl, m_new)
        return new, carry

    xs = (jnp.moveaxis(c_loc, 2, 0), jnp.moveaxis(n_loc, 2, 0), jnp.moveaxis(m_loc, 2, 0), jnp.moveaxis(b_tot, 2, 0))
    final, prev = lax.scan(step, state0, xs)
    c_prev = jnp.moveaxis(prev[0], 0, 2)
    n_prev = jnp.moveaxis(prev[1], 0, 2)
    m_prev = jnp.moveaxis(prev[2], 0, 2)
    inter_log = bcum + m_prev[..., None]
    m_q = jnp.maximum(m_intra, inter_log)
    a = jnp.exp(inter_log - m_q)
    g = jnp.exp(m_intra - m_q)
    num = a[..., None] * jnp.einsum('bhcjd,bhcde->bhcje', qc, c_prev) + g[..., None] * num_intra
    den = a * jnp.einsum('bhcjd,bhcd->bhcj', qc, n_prev) + g * den_intra
    h = num / jnp.maximum(jnp.abs(den), jnp.exp(-m_q))[..., None]
    return h.reshape(b_sz, h_sz, t_len, d), final


def _retention_log_decay(first_exp, n_heads):
    e = first_exp + 2.0 * jnp.arange(n_heads, dtype=F32)
    return jnp.log1p(-jnp.exp2(-e))


def _retention_chunkwise(q, k, v, log_gamma, state0):
    b_sz, h_sz, t_len, d = q.shape
    L = RET_CHUNK
    nc = t_len // L
    qc = q.reshape(b_sz, h_sz, nc, L, d)
    kc = k.reshape(b_sz, h_sz, nc, L, d)
    vc = v.reshape(b_sz, h_sz, nc, L, d)
    pos = jnp.arange(L, dtype=F32)
    diff = pos[:, None] - pos[None, :]
    decay = jnp.where(diff >= 0, jnp.exp(log_gamma[:, None, None] * jnp.maximum(diff, 0.0)), 0.0)
    s = jnp.einsum('bhcjd,bhcld->bhcjl', qc, kc).astype(F32) * decay[:, None]
    intra = jnp.einsum('bhcjl,bhcld->bhcjd', s, vc)
    zeta = jnp.exp(log_gamma[:, None] * (L - 1 - pos))
    s_loc = jnp.einsum('hl,bhcld,bhcle->bhcde', zeta, kc, vc)
    g_chunk = jnp.exp(log_gamma * L)[:, None, None]

    def step(r, sl):
        return g_chunk * r + sl, r

    final, r_prev = lax.scan(step, state0, jnp.moveaxis(s_loc, 2, 0))
    r_prev = jnp.moveaxis(r_prev, 0, 2)
    xi = jnp.exp(log_gamma[:, None] * (pos + 1.0))
    inter = jnp.einsum('bhcjd,bhcde->bhcje', qc, r_prev) * xi[:, None, :, None]
    return (intra + inter).reshape(b_sz, h_sz, t_len, d), final


def _na_window(rows):
    wr = min(NA_ROWS, rows)
    r = jnp.arange(rows)
    col = jnp.arange(GRID_W)
    row_idx = jnp.clip(r - wr // 2, 0, rows - wr)[:, None] + jnp.arange(wr)[None, :]
    col_start = jnp.clip(col - NA_COLS // 2, 0, GRID_W - NA_COLS)
    col_in = (col[None, :] >= col_start[:, None]) & (col[None, :] < col_start[:, None] + NA_COLS)
    dr = row_idx - r[:, None] + NA_ROWS - 1
    dc = jnp.clip(col[None, :] - col[:, None] + NA_COLS - 1, 0, 2 * NA_COLS - 2)
    return (row_idx, col_in, dr[:, None, :, None], dc[None, :, None, :])


def _na_latent(q, k, v, k_ctx, v_ctx, row_idx, col_in, bias):
    t_len, h_sz, d = q.shape
    rows, wr = row_idx.shape
    scale = d ** -0.5
    qg = q.reshape(rows, GRID_W, h_sz, d)
    kb = k.reshape(rows, GRID_W, h_sz, d)[row_idx]
    vb = v.reshape(rows, GRID_W, h_sz, d)[row_idx]
    s_loc = jnp.einsum('rqhd,rjwhd->hrqjw', qg, kb).astype(F32) * scale + bias
    s_loc = jnp.where(col_in[:, None, :], s_loc, NEG_INF)
    s_ctx = jnp.einsum('rqhd,chd->hrqc', qg, k_ctx).astype(F32) * scale
    n_loc = wr * GRID_W
    s = jnp.concatenate([s_loc.reshape(h_sz, rows, GRID_W, n_loc), s_ctx], axis=-1)
    p = jax.nn.softmax(s, axis=-1).astype(v.dtype)
    p_loc = p[..., :n_loc].reshape(h_sz, rows, GRID_W, wr, GRID_W)
    o = jnp.einsum('hrqjw,rjwhd->rqhd', p_loc, vb) + jnp.einsum('hrqc,chd->rqhd', p[..., n_loc:], v_ctx)
    return o.reshape(t_len, h_sz * d)


def _ctx_attention(q, k, v):
    s = jnp.einsum('bqhd,bkhd->bhqk', q, k).astype(F32) * HEAD_DIM ** -0.5
    p = jax.nn.softmax(s, axis=-1).astype(v.dtype)
    o = jnp.einsum('bhqk,bkhd->bqhd', p, v)
    return o.reshape(o.shape[0], o.shape[1], -1)


def _hybrid_mixer(xp, cp, rope, na_win, conv_w, conv_b, gate_b, m_norm, q_norm, k_norm, rpb, r_norm, dims):
    h_m, h_na, h_r = dims
    b_sz = xp[0].shape[0]

    def mlstm_prep(p):
        qk = jax.nn.silu(_dwconv_centred(p[0], conv_w, conv_b))
        q, k = jnp.split(qk, 2, axis=-1)
        g = (p[3] + gate_b).astype(F32)
        g = jnp.transpose(g.reshape(b_sz, -1, 4, h_m), (2, 0, 3, 1))
        return (_to_bhtd(_split_heads(q, h_m)), _to_bhtd(_split_heads(k, h_m)) * HEAD_DIM ** -0.5,
                _to_bhtd(_split_heads(p[1], h_m)), g[0], jax.nn.log_sigmoid(g[1]), g[2], jax.nn.log_sigmoid(g[3]))

    qx, kx, vx, ix_f, lfx_f, ix_b, lfx_b = mlstm_prep(xp)
    qc, kc, vc, ic_f, lfc_f, ic_b, lfc_b = mlstm_prep(cp)
    zero_m = (jnp.zeros((b_sz, h_m, HEAD_DIM, HEAD_DIM), F32),
              jnp.zeros((b_sz, h_m, HEAD_DIM), F32), jnp.zeros((b_sz, h_m), F32))
    hc_f, st_f = _mlstm_chunkwise(qc, kc, vc, ic_f, lfc_f, zero_m)
    hc_b, st_b = _mlstm_chunkwise(_flip_t(qc), _flip_t(kc), _flip_t(vc), _flip_t(ic_b), _flip_t(lfc_b), zero_m)
    hx_f, _ = _mlstm_chunkwise(qx, kx, vx, ix_f, lfx_f, st_f)
    hx_b, _ = _mlstm_chunkwise(_flip_t(qx), _flip_t(kx), _flip_t(vx), _flip_t(ix_b), _flip_t(lfx_b), st_b)

    def mlstm_out(h, o):
        return _merge_heads(_rms_norm(h, m_norm)).astype(o.dtype) * jax.nn.sigmoid(o)

    a_x = mlstm_out(hx_f + _flip_t(hx_b), xp[2])

    def na_prep(p):
        return (_rms_norm(_split_heads(p[4], h_na), q_norm), _rms_norm(_split_heads(p[5], h_na), k_norm),
                _split_heads(p[6], h_na))

    nqx, nkx, nvx = na_prep(xp)
    nqc, nkc, nvc = na_prep(cp)
    row_idx, col_in, dr, dc = na_win
    bias = rpb[:, dr, dc].astype(F32)
    b_x = lax.map(lambda a: _na_latent(a[0], a[1], a[2], a[3], a[4], row_idx, col_in, bias),
                  (nqx, nkx, nvx, nkc, nvc))

    lg_f = _retention_log_decay(5.0, h_r)
    lg_b = _retention_log_decay(6.0, h_r)
    rqx = _axial_rope(_to_bhtd(_split_heads(xp[7], h_r)), *rope)
    rkx = _axial_rope(_to_bhtd(_split_heads(xp[8], h_r)), *rope) * HEAD_DIM ** -0.5
    rvx = _to_bhtd(_split_heads(xp[9], h_r))
    rqc = _to_bhtd(_split_heads(cp[7], h_r))
    rkc = _to_bhtd(_split_heads(cp[8], h_r)) * HEAD_DIM ** -0.5
    rvc = _to_bhtd(_split_heads(cp[9], h_r))
    zero_r = jnp.zeros((b_sz, h_r, HEAD_DIM, HEAD_DIM), F32)
    rc_f, rs_f = _retention_chunkwise(rqc, rkc, rvc, lg_f, zero_r)
    rc_b, rs_b = _retention_chunkwise(_flip_t(rqc), _flip_t(rkc), _flip_t(rvc), lg_b, zero_r)
    rx_f, _ = _retention_chunkwise(rqx, rkx, rvx, lg_f, rs_f)
    rx_b, _ = _retention_chunkwise(_flip_t(rqx), _flip_t(rkx), _flip_t(rvx), lg_b, rs_b)

    def ret_out(h, g):
        return _merge_heads(_rms_norm(h, r_norm)).astype(g.dtype) * jax.nn.silu(g)

    c_x = ret_out(rx_f + _flip_t(rx_b), xp[10])
    mix_x = jnp.concatenate([a_x, b_x, c_x], axis=-1)
    a_c = mlstm_out(hc_f + _flip_t(hc_b), cp[2])
    b_c = _ctx_attention(nqc, nkc, nvc)
    c_c = ret_out(rc_f + _flip_t(rc_b), cp[10])
    return mix_x, jnp.concatenate([a_c, b_c, c_c], axis=-1)


def kernel(x, c, ctx, c_ctx, w_mod, b_mod, norm_mix, norm_ffn, w_in, w_out, mlstm_conv_w, mlstm_conv_b,
           mlstm_gate_b, mlstm_norm, na_q_norm, na_k_norm, na_rpb, ret_norm, router_w, router_b,
           expert_w_up, expert_b_up, expert_w_down, expert_b_down):
    b_sz, t_len, d = x.shape
    t_ctx = ctx.shape[1]
    depth = w_in.shape[0]
    n_e = router_w.shape[2]
    d_mix = w_out.shape[1]
    h_m = d_mix // (4 * HEAD_DIM)
    h_na = d_mix // (2 * HEAD_DIM)
    h_r = d_mix // (4 * HEAD_DIM)
    d_m, d_na, d_r = h_m * HEAD_DIM, h_na * HEAD_DIM, h_r * HEAD_DIM
    n_gate = 4 * h_m
    assert b_sz + 1 <= MOD_ROWS and n_e <= LANES and n_gate <= LANES
    s_len = t_ctx + t_len
    assert s_len % ROW_TILE == 0

    s = jnp.concatenate([ctx, x], axis=1)
    cc = jnp.zeros((MOD_ROWS, d), F32).at[:b_sz].set(c).at[b_sz].set(c_ctx)
    mods = _modulation(cc, w_mod, b_mod).reshape(depth, MOD_ROWS, 1, 6 * d)

    g0 = 4 * d_m
    col_na = g0
    col_ret = col_na + 3 * d_na
    n_hp = h_m // 2
    gate_src = np.array([[g0 + t * h_m + 2 * hp + h2 for t in range(N_GATE_TYPES) for h2 in range(2)]
                         for hp in range(n_hp)])
    rope_tabs = _rope_tables(t_ctx, t_len)
    ret_tabs = _ret_tables(h_r)
    na_tabs = _na_bias_tables(na_rpb)
    f2 = expert_w_up.shape[3]
    w_up_all = expert_w_up.reshape(depth * n_e, d, f2)
    w_down_all = expert_w_down.reshape(depth * n_e, f2 // 2, d)
    b_up_all = _regroup_bias(expert_b_up).reshape(depth * n_e, 1, f2)
    b_down_all = expert_b_down.reshape(depth * n_e, 1, d)

    for l in range(depth):
        w_main = jnp.concatenate([w_in[l, :, :g0], w_in[l, :, g0 + n_gate:]], axis=1).astype(BF16)
        w_gate = jnp.zeros((d, n_hp, LANES), BF16).at[:, :, :gate_src.shape[1]].set(
            w_in[l][:, gate_src].astype(BF16)).reshape(d, n_hp * LANES)
        p_main, p_gate = _proj_in(s, mods[l], norm_mix[l], w_main, w_gate, t_ctx)

        mix = [_mlstm(p_main, p_gate, mlstm_conv_w[l], mlstm_conv_b[l], mlstm_gate_b[l], mlstm_norm[l],
                      t_ctx, h_m),
               _na_attention(p_main, na_q_norm[l], na_k_norm[l], na_tabs[l], t_ctx, col_na, h_na),
               _retention(p_main, rope_tabs, ret_tabs, ret_norm[l], t_ctx, col_ret, h_r)]

        rw = jnp.zeros((d, LANES), F32).at[:, :n_e].set(router_w[l])
        rb = jnp.zeros((1, LANES), F32).at[0, :n_e].set(router_b[l])
        s, tok, logits = _proj_out(mix, s, mods[l], norm_ffn[l], w_out[l].astype(BF16), rw, rb, t_ctx)

        s = _moe(tok, logits[..., :n_e], s, mods[l], t_ctx, n_e, l, w_up_all, w_down_all, b_up_all, b_down_all)
    return s[:, t_ctx:]
```

```python
import functools

import jax
import jax.numpy as jnp
import numpy as np
from jax import lax
from jax.experimental import pallas as pl
from jax.experimental.pallas import tpu as pltpu

F32 = jnp.float32
BF16 = jnp.bfloat16

GRID_W = 64
HEAD_DIM = 64
MLSTM_CHUNK = 64
RET_CHUNK = 64
NA_ROWS = 8
NA_COLS = 16
ROPE_BASE = 10000.0
TOP_K = 4
SWIGLU_ALPHA = 1.702
SWIGLU_LIMIT = 7.0
NORM_EPS = 1e-6
NEG_INF = -1e30

LANES = 128
VMEM_LIMIT = 48 * 1024 * 1024
MOD_ROWS = 8
ROW_TILE = 768
MOE_TILE = 512
MOE_VMEM_LIMIT = 56 * 1024 * 1024


def _cparams(sem):
    return pltpu.CompilerParams(dimension_semantics=sem, vmem_limit_bytes=VMEM_LIMIT)


def _mod_kernel(cc_ref, w_ref, b_ref, o_ref):
    cc = cc_ref[...]
    a = cc * jax.nn.sigmoid(cc)
    o_ref[0] = jnp.dot(a, w_ref[0], precision=lax.Precision.HIGHEST,
                       preferred_element_type=F32) + b_ref[0]


def _modulation(cc, w_mod, b_mod):
    n_l, d, d6 = w_mod.shape
    tn = d6 // 4
    return pl.pallas_call(
        _mod_kernel,
        grid=(n_l, d6 // tn),
        in_specs=[pl.BlockSpec((MOD_ROWS, d), lambda l, j: (0, 0)),
                  pl.BlockSpec((1, d, tn), lambda l, j: (l, 0, j)),
                  pl.BlockSpec((1, 1, tn), lambda l, j: (l, 0, j))],
        out_specs=pl.BlockSpec((1, MOD_ROWS, tn), lambda l, j: (l, 0, j)),
        out_shape=jax.ShapeDtypeStruct((n_l, MOD_ROWS, d6), F32),
        compiler_params=_cparams(("arbitrary", "arbitrary")),
        name="adaln_modulation",
    )(cc, w_mod, b_mod.reshape(n_l, 1, d6))


def _pick_mod(mb_ref, mc_ref, k, d, is_ctx):
    vb = mb_ref[0, :, k * d:(k + 1) * d]
    vc = mc_ref[0, :, k * d:(k + 1) * d]
    return jnp.where(is_ctx, vc, vb)


def _rms(x, g):
    return x * lax.rsqrt(jnp.mean(x * x, axis=-1, keepdims=True) + NORM_EPS) * g


def _pack_bf16_pairs(x):
    m = x.shape[1] // 2
    hi = lax.bitcast_convert_type(x[:, :m].astype(BF16).astype(F32), jnp.uint32)
    lo = lax.bitcast_convert_type(x[:, m:].astype(BF16).astype(F32), jnp.uint32)
    return hi | (lo >> 16)


def _unpack_bf16_pairs(p):
    hi = lax.bitcast_convert_type(p & jnp.uint32(0xFFFF0000), F32)
    lo = lax.bitcast_convert_type(p << 16, F32)
    return jnp.concatenate([hi, lo], axis=1)


def _proj_in_kernel(s_ref, mb_ref, mc_ref, g_ref, w_ref, wg_ref, pm_ref, pg_ref, xn_ref, *, t_ctx):
    i = pl.program_id(1)
    j = pl.program_id(2)
    tm, d = xn_ref.shape

    @pl.when(j == 0)
    def _():
        row = lax.broadcasted_iota(jnp.int32, (tm, 1), 0) + i * tm
        is_ctx = row < t_ctx
        sh = _pick_mod(mb_ref, mc_ref, 0, d, is_ctx)
        sc = _pick_mod(mb_ref, mc_ref, 1, d, is_ctx)
        h = _rms(s_ref[0], g_ref[...]) * (1.0 + sc) + sh
        xn_ref[...] = h.astype(BF16)
        pg_ref[0] = jnp.dot(xn_ref[...], wg_ref[...], preferred_element_type=F32)

    pm_ref[0] = jnp.dot(xn_ref[...], w_ref[...], preferred_element_type=F32)


def _proj_in(s, mod_l, g, w_main, w_gate, t_ctx):
    b_sz, s_len, d = s.shape
    n_main = w_main.shape[1]
    n_gate = w_gate.shape[1]
    tn = n_main // 4
    tm = ROW_TILE
    return pl.pallas_call(
        functools.partial(_proj_in_kernel, t_ctx=t_ctx),
        grid=(b_sz, s_len // tm, n_main // tn),
        in_specs=[pl.BlockSpec((1, tm, d), lambda b, i, j: (b, i, 0)),
                  pl.BlockSpec((1, 1, 6 * d), lambda b, i, j: (b, 0, 0)),
                  pl.BlockSpec((1, 1, 6 * d), lambda b, i, j: (b_sz, 0, 0)),
                  pl.BlockSpec((1, d), lambda b, i, j: (0, 0)),
                  pl.BlockSpec((d, tn), lambda b, i, j: (0, j)),
                  pl.BlockSpec((d, n_gate), lambda b, i, j: (0, 0))],
        out_specs=[pl.BlockSpec((1, tm, tn), lambda b, i, j: (b, i, j)),
                   pl.BlockSpec((1, tm, n_gate), lambda b, i, j: (b, i, 0))],
        out_shape=[jax.ShapeDtypeStruct((b_sz, s_len, n_main), F32),
                   jax.ShapeDtypeStruct((b_sz, s_len, n_gate), F32)],
        scratch_shapes=[pltpu.VMEM((tm, d), BF16)],
        compiler_params=_cparams(("arbitrary", "arbitrary", "arbitrary")),
        name="proj_in",
    )(s, mod_l, mod_l, g.reshape(1, d), w_main, w_gate)


def _proj_out_kernel(ma_ref, mb2_ref, mc2_ref, s_ref, mb_ref, mc_ref, g_ref, w_ref, rw_ref, rb_ref,
                     so_ref, tok_ref, lg_ref, *, t_ctx):
    i = pl.program_id(1)
    tm, d = s_ref.shape[1], s_ref.shape[2]
    row = lax.broadcasted_iota(jnp.int32, (tm, 1), 0) + i * tm
    is_ctx = row < t_ctx
    g1 = _pick_mod(mb_ref, mc_ref, 2, d, is_ctx)
    ka, kb = ma_ref.shape[2], ma_ref.shape[2] + mb2_ref.shape[2]
    y = (jnp.dot(ma_ref[0], w_ref[0:ka, :], preferred_element_type=F32)
         + jnp.dot(mb2_ref[0], w_ref[ka:kb, :], preferred_element_type=F32)
         + jnp.dot(mc2_ref[0], w_ref[kb:, :], preferred_element_type=F32))
    s_new = s_ref[0] + g1 * y
    so_ref[0] = s_new
    sh = _pick_mod(mb_ref, mc_ref, 3, d, is_ctx)
    sc = _pick_mod(mb_ref, mc_ref, 4, d, is_ctx)
    t = _rms(s_new, g_ref[...]) * (1.0 + sc) + sh
    tok_ref[0] = _pack_bf16_pairs(t)
    lg_ref[0] = jnp.dot(t, rw_ref[...], precision=lax.Precision.HIGHEST,
                        preferred_element_type=F32) + rb_ref[...]


def _proj_out(mix_parts, s, mod_l, g, w_out, rw, rb, t_ctx):
    b_sz, s_len, d = s.shape
    tm = ROW_TILE
    row_spec = pl.BlockSpec((1, tm, d), lambda b, i: (b, i, 0))
    part_specs = [pl.BlockSpec((1, tm, m.shape[2]), lambda b, i: (b, i, 0)) for m in mix_parts]
    return pl.pallas_call(
        functools.partial(_proj_out_kernel, t_ctx=t_ctx),
        grid=(b_sz, s_len // tm),
        in_specs=part_specs + [row_spec,
                  pl.BlockSpec((1, 1, 6 * d), lambda b, i: (b, 0, 0)),
                  pl.BlockSpec((1, 1, 6 * d), lambda b, i: (b_sz, 0, 0)),
                  pl.BlockSpec((1, d), lambda b, i: (0, 0)),
                  pl.BlockSpec((d, d), lambda b, i: (0, 0)),
                  pl.BlockSpec((d, LANES), lambda b, i: (0, 0)),
                  pl.BlockSpec((1, LANES), lambda b, i: (0, 0))],
        out_specs=[row_spec, pl.BlockSpec((1, tm, d // 2), lambda b, i: (b, i, 0)),
                   pl.BlockSpec((1, tm, LANES), lambda b, i: (b, i, 0))],
        out_shape=[jax.ShapeDtypeStruct((b_sz, s_len, d), F32),
                   jax.ShapeDtypeStruct((b_sz, s_len, d // 2), jnp.uint32),
                   jax.ShapeDtypeStruct((b_sz, s_len, LANES), F32)],
        compiler_params=_cparams(("arbitrary", "arbitrary")),
        name="proj_out_router",
    )(*mix_parts, s, mod_l, mod_l, g.reshape(1, d), w_out, rw, rb)


PAIR = 2 * LANES


def _regroup_perm():
    dst = np.arange(PAIR)
    src = np.where(dst < LANES, 2 * dst, 2 * (dst - LANES) + 1)
    return jnp.asarray(np.arange(PAIR)[:, None] == src[None, :], BF16)


def _regroup_bias(b_up):
    lead = b_up.shape[:-1]
    b = b_up.reshape(*lead, -1, LANES, 2)
    return jnp.swapaxes(b, -1, -2).reshape(*lead, -1)


def _moe_kernel(be_ref, nu_ref, x_ref, wu_ref, wd_ref, bu_ref, bd_ref, p_ref, y_ref, wus_ref, wds_ref):
    i = pl.program_id(0)
    d, f2 = wus_ref.shape
    rows = min(512, d)

    @pl.when(i < nu_ref[0])
    def _():
        @pl.when((i == 0) | (be_ref[i] != be_ref[jnp.maximum(i - 1, 0)]))
        def _():
            for r in range(d // rows):
                for j in range(f2 // PAIR):
                    w = wu_ref[0, r * rows:(r + 1) * rows, j * PAIR:(j + 1) * PAIR].astype(BF16)
                    wus_ref[r * rows:(r + 1) * rows, j * PAIR:(j + 1) * PAIR] = jnp.dot(
                        w, p_ref[...], preferred_element_type=F32).astype(BF16)
            wds_ref[...] = wd_ref[0].astype(BF16)

        x = _unpack_bf16_pairs(x_ref[...]).astype(BF16)
        up = jnp.dot(x, wus_ref[...], preferred_element_type=F32) + bu_ref[0]
        acts = []
        for j in range(f2 // PAIR):
            glu = jnp.minimum(up[:, j * PAIR:j * PAIR + LANES], SWIGLU_LIMIT)
            lin = jnp.clip(up[:, j * PAIR + LANES:(j + 1) * PAIR], -SWIGLU_LIMIT, SWIGLU_LIMIT)
            acts.append((glu * jax.nn.sigmoid(SWIGLU_ALPHA * glu) * (lin + 1.0)).astype(BF16))
        act = jnp.concatenate(acts, axis=1)
        y_ref[...] = _pack_bf16_pairs(jnp.dot(act, wds_ref[...], preferred_element_type=F32) + bd_ref[0])

    @pl.when(i >= nu_ref[0])
    def _():
        y_ref[...] = jnp.zeros_like(y_ref)


def _moe_blocks(blk_e, n_used, x_sorted, w_up, w_down, b_up, b_down):
    n_rows = x_sorted.shape[0]
    _, d, f2 = w_up.shape
    assert f2 % PAIR == 0 and d % min(512, d) == 0
    tm = MOE_TILE
    wmap = lambda i, be, nu: (be[i], 0, 0)
    return pl.pallas_call(
        _moe_kernel,
        grid_spec=pltpu.PrefetchScalarGridSpec(
            num_scalar_prefetch=2,
            grid=(n_rows // tm,),
            in_specs=[pl.BlockSpec((tm, d // 2), lambda i, be, nu: (i, 0)),
                      pl.BlockSpec((1, d, f2), wmap),
                      pl.BlockSpec((1, f2 // 2, d), wmap),
                      pl.BlockSpec((1, 1, f2), wmap),
                      pl.BlockSpec((1, 1, d), wmap),
                      pl.BlockSpec((PAIR, PAIR), lambda i, be, nu: (0, 0))],
            out_specs=pl.BlockSpec((tm, d // 2), lambda i, be, nu: (i, 0)),
            scratch_shapes=[pltpu.VMEM((d, f2), BF16), pltpu.VMEM((f2 // 2, d), BF16)]),
        out_shape=jax.ShapeDtypeStruct((n_rows, d // 2), jnp.uint32),
        compiler_params=pltpu.CompilerParams(dimension_semantics=("arbitrary",),
                                             vmem_limit_bytes=MOE_VMEM_LIMIT),
        name="moe_expert_blocks",
    )(blk_e, n_used, x_sorted, w_up, w_down, b_up, b_down, _regroup_perm())


def _combine_kernel(y_ref, gt_ref, s_ref, mb_ref, mc_ref, o_ref, *, t_ctx):
    i = pl.program_id(1)
    tm, d = s_ref.shape[1], s_ref.shape[2]
    row = lax.broadcasted_iota(jnp.int32, (tm, 1), 0) + i * tm
    g2 = _pick_mod(mb_ref, mc_ref, 5, d, row < t_ctx)
    gt = gt_ref[0]
    y = _unpack_bf16_pairs(y_ref[0, 0]) * gt[:, 0:1]
    for k in range(1, y_ref.shape[0]):
        y = y + _unpack_bf16_pairs(y_ref[k, 0]) * gt[:, k:k + 1]
    o_ref[0] = s_ref[0] + g2 * y


def _combine(y_as, gates, s, mod_l, t_ctx):
    b_sz, s_len, d = s.shape
    n_k = y_as.shape[0]
    tm = ROW_TILE // 3
    row_spec = pl.BlockSpec((1, tm, d), lambda b, i: (b, i, 0))
    return pl.pallas_call(
        functools.partial(_combine_kernel, t_ctx=t_ctx),
        grid=(b_sz, s_len // tm),
        in_specs=[pl.BlockSpec((n_k, 1, tm, d // 2), lambda b, i: (0, b, i, 0)),
                  pl.BlockSpec((1, tm, n_k), lambda b, i: (b, i, 0)),
                  row_spec,
                  pl.BlockSpec((1, 1, 6 * d), lambda b, i: (b, 0, 0)),
                  pl.BlockSpec((1, 1, 6 * d), lambda b, i: (b_sz, 0, 0))],
        out_specs=row_spec,
        out_shape=jax.ShapeDtypeStruct((b_sz, s_len, d), F32),
        compiler_params=_cparams(("arbitrary", "arbitrary")),
        name="moe_combine",
    )(y_as, gates, s, mod_l, mod_l)


def _moe(tok, logits, s, mod_l, t_ctx, n_e, layer, w_up, w_down, b_up, b_down):
    b_sz, s_len, d_half = tok.shape
    n_tok = b_sz * s_len
    tm = MOE_TILE
    top_v, top_e = lax.top_k(logits.reshape(n_tok, n_e), TOP_K)
    gates = jax.nn.softmax(top_v, axis=-1)
    n_as = n_tok * TOP_K
    onehot = jnp.sum((top_e[:, :, None] == jnp.arange(n_e)[None, None, :]).astype(jnp.int32), axis=1)
    csum = jnp.cumsum(onehot, axis=0)
    counts = csum[-1]
    rank = jnp.take_along_axis(csum - onehot, top_e, axis=1)
    padded = (counts + tm - 1) // tm * tm
    end_pad = jnp.cumsum(padded)
    start_pad = end_pad - padded
    start = jnp.cumsum(counts) - counts
    dest = (start_pad[top_e] + rank).astype(jnp.int32)
    n_blocks = -(-n_as // tm) + n_e
    blk_e = jnp.minimum(jnp.searchsorted(end_pad, jnp.arange(n_blocks) * tm, side='right'),
                        n_e - 1).astype(jnp.int32)
    n_used = (end_pad[-1:] // tm).astype(jnp.int32)
    tok_sorted = (jnp.argsort(top_e.reshape(n_as)) // TOP_K).astype(jnp.int32)
    row = jnp.arange(n_blocks * tm)
    row_e = jnp.repeat(blk_e, tm)
    j = row - start_pad[row_e]
    src = jnp.clip(start[row_e] + j, 0, n_as - 1)
    row_tok = jnp.where(j < counts[row_e], tok_sorted[src], 0)
    x_sorted = tok.reshape(n_tok, d_half)[row_tok]
    y = _moe_blocks(blk_e + layer * n_e, n_used, x_sorted, w_up, w_down, b_up, b_down)
    y_as = y[dest.T].reshape(TOP_K, b_sz, s_len, d_half)
    return _combine(y_as, gates.reshape(b_sz, s_len, TOP_K), s, mod_l, t_ctx)


NA_STEP = 256
NA_KEYS = NA_ROWS * GRID_W


def _head_rms(x, w, lo):
    xx = x * x
    s0 = jnp.sum(jnp.where(lo, xx, 0.0), axis=-1, keepdims=True)
    s1 = jnp.sum(jnp.where(lo, 0.0, xx), axis=-1, keepdims=True)
    inv = lax.rsqrt(jnp.where(lo, s0, s1) * (1.0 / HEAD_DIM) + NORM_EPS)
    return x * inv * w


def _dot_nt(a, b):
    return lax.dot_general(a, b, (((1,), (1,)), ((), ())), preferred_element_type=F32)


def _na_kernel(q_ref, k_ref, v_ref, qw_ref, kw_ref, bias_ref, o_ref, kt_ref, vb_ref, s_ref, *, t_ctx, rows):
    rg = pl.program_id(2)
    s_len = k_ref.shape[1]
    lo = lax.broadcasted_iota(jnp.int32, (1, LANES), 1) < HEAD_DIM
    rows_per_step = NA_STEP // GRID_W

    @pl.when(rg == 0)
    def _():
        def knorm_t(t0, n):
            return _head_rms(k_ref[0, pl.ds(t0, n), :], kw_ref[...], lo).T.astype(BF16)

        def prep(c, carry):
            t0 = pl.multiple_of(c * NA_STEP, NA_STEP)
            kt_ref[0, :, pl.ds(t0, NA_STEP)] = knorm_t(t0, NA_STEP)
            vb_ref[pl.ds(t0, NA_STEP), :] = jnp.concatenate(
                [v_ref[0, pl.ds(t0, NA_STEP), :].astype(BF16), jnp.ones((NA_STEP, LANES), BF16)], axis=1)
            return carry
        lax.fori_loop(0, s_len // NA_STEP, prep, 0, unroll=3)

        def prep_shifted(c, carry):
            t0 = pl.multiple_of(c * LANES, LANES)
            kt_ref[1, :, pl.ds(t0, LANES)] = knorm_t(pl.multiple_of(t0 + GRID_W, GRID_W), LANES)
            return carry
        lax.fori_loop(0, (s_len - GRID_W) // LANES, prep_shifted, 0, unroll=5)

    qn = _head_rms(q_ref[0], qw_ref[...], lo) * (HEAD_DIM ** -0.5)
    q0 = jnp.where(lo, qn, 0.0).astype(BF16)
    q1 = jnp.where(lo, 0.0, qn).astype(BF16)
    kc_t = kt_ref[0, :, 0:t_ctx]
    vc = vb_ref[0:t_ctx, :]

    def finish(g, o2):
        o2 = o2[:, :LANES] * (1.0 / o2[:, LANES:])
        o = jnp.where(lo, o2[:GRID_W], o2[GRID_W:])
        o_ref[0, g * GRID_W:(g + 1) * GRID_W, :] = o.astype(o_ref.dtype)

    @pl.when(rg == 0)
    def _():
        for g in range(rows_per_step):
            q2 = jnp.concatenate([q0[g * GRID_W:(g + 1) * GRID_W], q1[g * GRID_W:(g + 1) * GRID_W]], axis=0)
            s_c = jnp.dot(q2, kc_t, preferred_element_type=F32)
            p_c = jnp.exp(s_c - jnp.max(s_c, axis=-1, keepdims=True))
            finish(g, jnp.dot(p_c.astype(BF16), vc, preferred_element_type=F32))

    @pl.when(rg > 0)
    def _():
        t0s = []
        for g in range(rows_per_step):
            r = (rg - 1) * rows_per_step + g
            row_start = jnp.clip(r - NA_ROWS // 2, 0, rows - NA_ROWS)
            dr0 = row_start - r + NA_ROWS - 1
            t0 = pl.multiple_of(t_ctx + row_start * GRID_W, GRID_W)
            odd = (t0 // GRID_W) % (LANES // GRID_W)
            kw_t = kt_ref[odd, :, pl.ds(pl.multiple_of(t0 - odd * GRID_W, LANES), NA_KEYS)]
            q2 = jnp.concatenate([q0[g * GRID_W:(g + 1) * GRID_W], q1[g * GRID_W:(g + 1) * GRID_W]], axis=0)
            s_ref[g, :, :NA_KEYS] = jnp.dot(q2, kw_t, preferred_element_type=F32) + bias_ref[dr0, 0]
            s_ref[g, :, NA_KEYS:] = jnp.dot(q2, kc_t, preferred_element_type=F32)
            t0s.append(t0)
        for g in range(rows_per_step):
            s = s_ref[g]
            p = jnp.exp(s - jnp.max(s, axis=-1, keepdims=True)).astype(BF16)
            finish(g, jnp.dot(p[:, :NA_KEYS], vb_ref[pl.ds(t0s[g], NA_KEYS), :], preferred_element_type=F32)
                   + jnp.dot(p[:, NA_KEYS:], vc, preferred_element_type=F32))


def _na_bias_tables(rpb):
    n_l, n_h, n_dr, n_dc = rpb.shape
    col = np.arange(GRID_W)
    dc = np.clip(col[None, :] - col[:, None] + NA_COLS - 1, 0, n_dc - 1)
    onehot = (dc.reshape(1, -1) == np.arange(n_dc)[:, None]).astype(np.float32)
    toe = jnp.dot(rpb.reshape(-1, n_dc), jnp.asarray(onehot), precision=lax.Precision.HIGHEST)
    toe = toe.reshape(n_l, n_h, n_dr, GRID_W, GRID_W)
    col_start = np.clip(col - NA_COLS // 2, 0, GRID_W - NA_COLS)
    col_in = (col[None, :] >= col_start[:, None]) & (col[None, :] < col_start[:, None] + NA_COLS)
    toe = jnp.where(jnp.asarray(col_in), toe, NEG_INF)
    tabs = []
    for dr0 in range(NA_ROWS):
        t = toe[:, :, dr0:dr0 + NA_ROWS].reshape(n_l, n_h // 2, 2, NA_ROWS, GRID_W, GRID_W)
        t = jnp.transpose(t, (0, 1, 2, 4, 3, 5))
        tabs.append(t.reshape(n_l, n_h // 2, 2 * GRID_W, NA_KEYS))
    return jnp.stack(tabs, axis=1)


def _na_seq_kernel(q_ref, k_ref, v_ref, qw_ref, kw_ref, bias_ref, o_ref, kt_ref, vb_ref, s_ref, *, t_ctx, rows):
    s_len = k_ref.shape[1]
    lo = lax.broadcasted_iota(jnp.int32, (1, LANES), 1) < HEAD_DIM
    rows_per_step = NA_STEP // GRID_W

    def knorm_t(t0, n):
        return _head_rms(k_ref[0, pl.ds(t0, n), :], kw_ref[...], lo).T.astype(BF16)

    def prep(c, carry):
        t0 = pl.multiple_of(c * NA_STEP, NA_STEP)
        kt_ref[0, :, pl.ds(t0, NA_STEP)] = knorm_t(t0, NA_STEP)
        vb_ref[pl.ds(t0, NA_STEP), :] = jnp.concatenate(
            [v_ref[0, pl.ds(t0, NA_STEP), :].astype(BF16), jnp.ones((NA_STEP, LANES), BF16)], axis=1)
        return carry
    lax.fori_loop(0, s_len // NA_STEP, prep, 0, unroll=3)

    def prep_shifted(c, carry):
        t0 = pl.multiple_of(c * LANES, LANES)
        kt_ref[1, :, pl.ds(t0, LANES)] = knorm_t(pl.multiple_of(t0 + GRID_W, GRID_W), LANES)
        return carry
    lax.fori_loop(0, (s_len - GRID_W) // LANES, prep_shifted, 0, unroll=5)

    kc_t = kt_ref[0, :, 0:t_ctx]
    vc = vb_ref[0:t_ctx, :]

    def queries(tok0):
        qn = _head_rms(q_ref[0, pl.ds(tok0, NA_STEP), :], qw_ref[...], lo) * (HEAD_DIM ** -0.5)
        q0 = jnp.where(lo, qn, 0.0).astype(BF16)
        q1 = jnp.where(lo, 0.0, qn).astype(BF16)
        return [jnp.concatenate([q0[g * GRID_W:(g + 1) * GRID_W], q1[g * GRID_W:(g + 1) * GRID_W]], axis=0)
                for g in range(rows_per_step)]

    def finish(tok0, g, o2):
        o2 = o2[:, :LANES] * (1.0 / o2[:, LANES:])
        o = jnp.where(lo, o2[:GRID_W], o2[GRID_W:])
        o_ref[0, pl.ds(tok0 + g * GRID_W, GRID_W), :] = o.astype(o_ref.dtype)

    for g, q2 in enumerate(queries(0)):
        s_c = jnp.dot(q2, kc_t, preferred_element_type=F32)
        p_c = jnp.exp(s_c - jnp.max(s_c, axis=-1, keepdims=True))
        finish(0, g, jnp.dot(p_c.astype(BF16), vc, preferred_element_type=F32))

    def row_group(rg, carry):
        tok0 = pl.multiple_of(t_ctx + rg * NA_STEP, NA_STEP)
        t0s = []
        for g, q2 in enumerate(queries(tok0)):
            r = rg * rows_per_step + g
            row_start = jnp.clip(r - NA_ROWS // 2, 0, rows - NA_ROWS)
            dr0 = row_start - r + NA_ROWS - 1
            t0 = pl.multiple_of(t_ctx + row_start * GRID_W, GRID_W)
            odd = (t0 // GRID_W) % (LANES // GRID_W)
            kw_t = kt_ref[odd, :, pl.ds(pl.multiple_of(t0 - odd * GRID_W, LANES), NA_KEYS)]
            s_ref[g, :, :NA_KEYS] = jnp.dot(q2, kw_t, preferred_element_type=F32) + bias_ref[dr0, 0]
            s_ref[g, :, NA_KEYS:] = jnp.dot(q2, kc_t, preferred_element_type=F32)
            t0s.append(t0)
        for g in range(rows_per_step):
            s = s_ref[g]
            p = jnp.exp(s - jnp.max(s, axis=-1, keepdims=True)).astype(BF16)
            finish(tok0, g, jnp.dot(p[:, :NA_KEYS], vb_ref[pl.ds(t0s[g], NA_KEYS), :], preferred_element_type=F32)
                   + jnp.dot(p[:, NA_KEYS:], vc, preferred_element_type=F32))
        return carry
    lax.fori_loop(0, rows // rows_per_step, row_group, 0)


def _na_attention(p_main, q_w, k_w, bias_tab, t_ctx, col_q, n_heads):
    b_sz, s_len, _ = p_main.shape
    n_hg = n_heads // 2
    rows = (s_len - t_ctx) // GRID_W
    rows_per_step = NA_STEP // GRID_W
    assert t_ctx == NA_STEP and s_len % NA_STEP == 0 and rows >= NA_ROWS and rows % rows_per_step == 0
    cq = col_q // LANES
    seq = lambda col: pl.BlockSpec((1, s_len, LANES), lambda b, h: (b, 0, col + h))
    w2 = lambda w: jnp.concatenate([w, w]).reshape(1, LANES)
    return pl.pallas_call(
        functools.partial(_na_seq_kernel, t_ctx=t_ctx, rows=rows),
        grid=(b_sz, n_hg),
        in_specs=[seq(cq), seq(cq + n_hg), seq(cq + 2 * n_hg),
                  pl.BlockSpec((1, LANES), lambda b, h: (0, 0)),
                  pl.BlockSpec((1, LANES), lambda b, h: (0, 0)),
                  pl.BlockSpec((NA_ROWS, 1, 2 * GRID_W, NA_KEYS), lambda b, h: (0, h, 0, 0))],
        out_specs=pl.BlockSpec((1, s_len, LANES), lambda b, h: (b, 0, h)),
        out_shape=jax.ShapeDtypeStruct((b_sz, s_len, n_heads * HEAD_DIM), BF16),
        scratch_shapes=[pltpu.VMEM((2, LANES, s_len), BF16), pltpu.VMEM((s_len, 2 * LANES), BF16),
                        pltpu.VMEM((rows_per_step, 2 * GRID_W, NA_KEYS + t_ctx), F32)],
        compiler_params=_cparams(("arbitrary", "arbitrary")),
        name="na_attention",
    )(p_main, p_main, p_main, w2(q_w), w2(k_w), bias_tab)


RET_CHUNK_LEN = 256
RET_UNROLL = 3
MLSTM_CHUNK_LEN = 128


def _reverse_chunk(i, nc, n_ctx):
    return jnp.where(i < n_ctx, n_ctx - 1 - i, nc - 1 - i + n_ctx)


def _dot_tn(a, b):
    return lax.dot_general(a, b, (((0,), (0,)), ((), ())), preferred_element_type=F32)


def _rope_tables(t_ctx, t_len):
    pos = jnp.arange(t_len)
    row = (pos // GRID_W).astype(F32)
    col = (pos % GRID_W).astype(F32)
    n = HEAD_DIM // 4
    inv = ROPE_BASE ** (-jnp.arange(n, dtype=F32) / n)
    ar = row[:, None] * inv
    ac = col[:, None] * inv
    cos = jnp.concatenate([jnp.cos(ar), jnp.cos(ar), jnp.cos(ac), jnp.cos(ac)], axis=-1)
    sin = jnp.concatenate([-jnp.sin(ar), jnp.sin(ar), -jnp.sin(ac), jnp.sin(ac)], axis=-1)
    cos = jnp.concatenate([jnp.ones((t_ctx, HEAD_DIM), F32), cos], axis=0)
    sin = jnp.concatenate([jnp.zeros((t_ctx, HEAD_DIM), F32), sin], axis=0)
    return jnp.tile(cos, (1, 2)), jnp.tile(sin, (1, 2))


def _ret_tables(n_heads):
    L = RET_CHUNK_LEN
    pos = np.arange(L, dtype=np.float32)
    lane_head = np.arange(LANES) // HEAD_DIM
    decay = np.zeros((2, n_heads, L, L), np.float32)
    zeta = np.zeros((2, n_heads // 2, LANES, L), np.float32)
    xi = np.zeros((2, n_heads // 2, L, LANES), np.float32)
    gch = np.zeros((2, n_heads // 2, 1, LANES), np.float32)
    for d, first_exp in enumerate((5.0, 6.0)):
        e = np.float32(first_exp) + np.float32(2.0) * np.arange(n_heads, dtype=np.float32)
        lg = np.log1p(-np.exp2(-e)).astype(np.float32)
        diff = pos[:, None] - pos[None, :]
        if d == 1:
            diff = -diff
        for h in range(n_heads):
            decay[d, h] = np.where(diff >= 0, np.exp(lg[h] * np.maximum(diff, 0.0)), 0.0)
        for hp in range(n_heads // 2):
            lgl = lg[2 * hp + lane_head][None, :]
            to_end = (L - 1 - pos if d == 0 else pos)[:, None]
            zeta[d, hp] = np.exp(lgl * to_end).T
            xi[d, hp] = np.exp(lgl * (L - to_end))
            gch[d, hp] = np.exp(lgl * L)
    return tuple(jnp.asarray(a) for a in (decay, zeta, xi, gch))


def _ret_kernel(q_ref, k_ref, v_ref, g_ref, cos_ref, sin_ref, dec_ref, zeta_ref, xi_ref, gch_ref, rn_ref,
                o_ref, qr_ref, kt_ref, vb_ref, acc_ref, *, t_ctx):
    L = RET_CHUNK_LEN
    s_len = q_ref.shape[1]
    nc = s_len // L
    lane = lax.broadcasted_iota(jnp.int32, (1, LANES), 1)
    lo = lane < HEAD_DIM
    half = (lane & (HEAD_DIM // 4)) == 0
    rid = lax.broadcasted_iota(jnp.int32, (LANES, LANES), 0) < HEAD_DIM
    cid = lax.broadcasted_iota(jnp.int32, (LANES, LANES), 1) < HEAD_DIM
    same_head = rid == cid

    def rope(x, cos, sin):
        up = pltpu.roll(x, LANES - HEAD_DIM // 4, axis=1)
        dn = pltpu.roll(x, HEAD_DIM // 4, axis=1)
        return x * cos + jnp.where(half, up, dn) * sin

    def prep(c, carry):
        sl = pl.ds(pl.multiple_of(c * L, L), L)
        cos, sin = cos_ref[sl, :], sin_ref[sl, :]
        qr_ref[sl, :] = rope(q_ref[0, sl, :], cos, sin).astype(BF16)
        kt_ref[:, sl] = (rope(k_ref[0, sl, :], cos, sin) * HEAD_DIM ** -0.5).T.astype(BF16)
        vb_ref[sl, :] = v_ref[0, sl, :].astype(BF16)
        return carry
    lax.fori_loop(0, nc, prep, 0, unroll=RET_UNROLL)

    def chunk(d, c, state):
        sl = pl.ds(pl.multiple_of(c * L, L), L)
        q, kt, v = qr_ref[sl, :], kt_ref[:, sl], vb_ref[sl, :]
        inter = jnp.dot(q, state.astype(BF16), preferred_element_type=F32) * xi_ref[d, 0]
        outs = []
        for h2 in range(2):
            qm = jnp.where(lo if h2 == 0 else jnp.logical_not(lo), q, jnp.zeros_like(q))
            sd = (jnp.dot(qm, kt, preferred_element_type=F32) * dec_ref[d, h2]).astype(BF16)
            outs.append(jnp.dot(sd, v, preferred_element_type=F32))
        y = jnp.where(lo, outs[0], outs[1]) + inter
        kz_t = (kt.astype(F32) * zeta_ref[d, 0]).astype(BF16)
        state = state * gch_ref[d, 0] + jnp.where(same_head, jnp.dot(kz_t, v, preferred_element_type=F32), 0.0)
        return sl, y, state

    def fwd(c, state):
        sl, y, state = chunk(0, c, state)
        acc_ref[sl, :] = y
        return state
    lax.fori_loop(0, nc, fwd, jnp.zeros((LANES, LANES), F32), unroll=RET_UNROLL)

    def bwd(i, state):
        c = _reverse_chunk(i, nc, t_ctx // L)
        sl, y, state = chunk(1, c, state)
        y = _head_rms(y + acc_ref[sl, :], rn_ref[...], lo)
        g = g_ref[0, sl, :]
        o_ref[0, sl, :] = (y * (g * jax.nn.sigmoid(g))).astype(o_ref.dtype)
        return state
    lax.fori_loop(0, nc, bwd, jnp.zeros((LANES, LANES), F32), unroll=RET_UNROLL)


def _seq_spec(s_len, col):
    return pl.BlockSpec((1, s_len, LANES), lambda b, h: (b, 0, col + h), pipeline_mode=pl.Buffered(1))


def _retention(p_main, rope_tabs, ret_tabs, r_w, t_ctx, col_q, n_heads):
    b_sz, s_len, _ = p_main.shape
    n_hp = n_heads // 2
    L = RET_CHUNK_LEN
    assert t_ctx % L == 0 and s_len % L == 0
    c0 = col_q // LANES
    cos, sin = rope_tabs
    decay, zeta, xi, gch = ret_tabs
    const2 = pl.BlockSpec((s_len, LANES), lambda b, h: (0, 0), pipeline_mode=pl.Buffered(1))
    return pl.pallas_call(
        functools.partial(_ret_kernel, t_ctx=t_ctx),
        grid=(b_sz, n_hp),
        in_specs=[_seq_spec(s_len, c0), _seq_spec(s_len, c0 + n_hp), _seq_spec(s_len, c0 + 2 * n_hp),
                  _seq_spec(s_len, c0 + 3 * n_hp), const2, const2,
                  pl.BlockSpec((2, 2, L, L), lambda b, h: (0, h, 0, 0)),
                  pl.BlockSpec((2, 1, LANES, L), lambda b, h: (0, h, 0, 0)),
                  pl.BlockSpec((2, 1, L, LANES), lambda b, h: (0, h, 0, 0)),
                  pl.BlockSpec((2, 1, 1, LANES), lambda b, h: (0, h, 0, 0)),
                  pl.BlockSpec((1, LANES), lambda b, h: (0, 0))],
        out_specs=pl.BlockSpec((1, s_len, LANES), lambda b, h: (b, 0, h)),
        out_shape=jax.ShapeDtypeStruct((b_sz, s_len, n_heads * HEAD_DIM), BF16),
        scratch_shapes=[pltpu.VMEM((s_len, LANES), BF16), pltpu.VMEM((LANES, s_len), BF16),
                        pltpu.VMEM((s_len, LANES), BF16), pltpu.VMEM((s_len, LANES), F32)],
        compiler_params=_cparams(("arbitrary", "arbitrary")),
        name="retention",
    )(p_main, p_main, p_main, p_main, cos, sin, decay, zeta, xi, gch,
      jnp.concatenate([r_w, r_w]).reshape(1, LANES))


N_GATE_TYPES = 4


def _log_sigmoid(x):
    return jnp.minimum(x, 0.0) - jnp.log1p(jnp.exp(-jnp.abs(x)))


def _mlstm_kernel(q_ref, k_ref, v_ref, og_ref, gc_ref, gr_ref, wq_ref, wk_ref, bq_ref, bk_ref, gbc_ref, gbr_ref,
                  mn_ref, o_ref, qc_ref, kc_ref, vb_ref, acc_ref, *, t_ctx):
    L = MLSTM_CHUNK_LEN
    s_len = q_ref.shape[1]
    nc = s_len // L
    lane = lax.broadcasted_iota(jnp.int32, (1, LANES), 1)
    lo = lane < HEAD_DIM
    head_lanes = (lo, jnp.logical_not(lo))
    rid = lax.broadcasted_iota(jnp.int32, (LANES, LANES), 0) < HEAD_DIM
    cid = lax.broadcasted_iota(jnp.int32, (LANES, LANES), 1) < HEAD_DIM
    head_block = (rid & cid, jnp.logical_not(rid | cid))
    row_i = lax.broadcasted_iota(jnp.int32, (L, L), 0)
    col_i = lax.broadcasted_iota(jnp.int32, (L, L), 1)
    causal = (row_i >= col_i, row_i <= col_i)
    tri = tuple(c.astype(F32) for c in causal)
    tri_t = (tri[1], tri[0])
    sub = lax.broadcasted_iota(jnp.int32, (L, 1), 0)

    def conv(x_ref, w_ref, b_ref, t0):
        x = x_ref[0, pl.ds(t0, L), :]
        prev = x_ref[0, pl.ds(jnp.maximum(t0 - 8, 0), 8), :][7:8]
        nxt = x_ref[0, pl.ds(jnp.minimum(t0 + L, s_len - 8), 8), :][0:1]
        prev = jnp.where((t0 != 0) & (t0 != t_ctx), prev, 0.0)
        nxt = jnp.where((t0 + L != t_ctx) & (t0 + L != s_len), nxt, 0.0)
        xm = jnp.where(sub == 0, prev, pltpu.roll(x, 1, axis=0))
        xp = jnp.where(sub == L - 1, nxt, pltpu.roll(x, L - 1, axis=0))
        y = b_ref[...] + xm * w_ref[0:1, :] + x * w_ref[1:2, :] + xp * w_ref[2:3, :]
        return y * jax.nn.sigmoid(y)

    def prep(c, carry):
        t0 = pl.multiple_of(c * L, L)
        sl = pl.ds(t0, L)
        qc_ref[sl, :] = conv(q_ref, wq_ref, bq_ref, t0).astype(BF16)
        kc_ref[sl, :] = (conv(k_ref, wk_ref, bk_ref, t0) * HEAD_DIM ** -0.5).astype(BF16)
        vb_ref[sl, :] = v_ref[0, sl, :].astype(BF16)
        return carry
    lax.fori_loop(0, nc, prep, 0)

    def chunk(d, c, state):
        sl = pl.ds(pl.multiple_of(c * L, L), L)
        q, k, v = qc_ref[sl, :], kc_ref[sl, :], vb_ref[sl, :]
        g_col = gc_ref[0, sl, :] + gbc_ref[0]
        g_row = gr_ref[0, 0, :, sl] + gbr_ref[0]
        cum_col = jnp.dot(tri[d], _log_sigmoid(g_col), precision=lax.Precision.HIGHEST,
                          preferred_element_type=F32)
        cum_row = jnp.dot(_log_sigmoid(g_row), tri_t[d], precision=lax.Precision.HIGHEST,
                          preferred_element_type=F32)
        end = L - 1 if d == 0 else 0
        outs, new_state = [], []
        for h2 in range(2):
            c_st, n_st, m_st = state[h2]
            ci, cf = 2 * (2 * d) + h2, 2 * (2 * d + 1) + h2
            i_col, a_col = g_col[:, ci:ci + 1], cum_col[:, cf:cf + 1]
            i_row, a_row = g_row[ci:ci + 1, :], cum_row[cf:cf + 1, :]
            b_tot = a_row[:, end:end + 1]
            d_log = jnp.where(causal[d], a_col + (i_row - a_row), NEG_INF)
            m_intra = jnp.max(d_log, axis=-1, keepdims=True)
            qm = jnp.where(head_lanes[h2], q, jnp.zeros_like(q))
            s = _dot_nt(qm, k) * jnp.exp(d_log - m_intra)
            num_intra = jnp.dot(s.astype(BF16), v, preferred_element_type=F32)
            den_intra = jnp.sum(s, axis=-1, keepdims=True)
            inter_log = a_col + m_st
            m_q = jnp.maximum(m_intra, inter_log)
            a = jnp.exp(inter_log - m_q)
            g = jnp.exp(m_intra - m_q)
            num = a * jnp.dot(qm, c_st.astype(BF16), preferred_element_type=F32) + g * num_intra
            den = a * jnp.sum(qm.astype(F32) * n_st, axis=-1, keepdims=True) + g * den_intra
            outs.append(num / jnp.maximum(jnp.abs(den), jnp.exp(-m_q)))
            w_log = b_tot - a_col + i_col
            m_loc = jnp.max(w_log, axis=0, keepdims=True)
            ke = jnp.where(head_lanes[h2], k.astype(F32) * jnp.exp(w_log - m_loc), 0.0)
            c_loc = jnp.where(head_block[h2], _dot_tn(ke.astype(BF16), v), 0.0)
            n_loc = jnp.sum(ke, axis=0, keepdims=True)
            m_new = jnp.maximum(b_tot + m_st, m_loc)
            a_s = jnp.exp(b_tot + m_st - m_new)
            g_s = jnp.exp(m_loc - m_new)
            new_state.append((a_s * c_st + g_s * c_loc, a_s * n_st + g_s * n_loc, m_new))
        return sl, jnp.where(lo, outs[0], outs[1]), tuple(new_state)

    zero = tuple((jnp.zeros((LANES, LANES), F32), jnp.zeros((1, LANES), F32), jnp.zeros((1, 1), F32))
                 for _ in range(2))

    def fwd(c, state):
        sl, y, state = chunk(0, c, state)
        acc_ref[sl, :] = y
        return state
    lax.fori_loop(0, nc, fwd, zero)

    def bwd(i, state):
        c = _reverse_chunk(i, nc, t_ctx // L)
        sl, y, state = chunk(1, c, state)
        y = _head_rms(y + acc_ref[sl, :], mn_ref[...], lo)
        o_ref[0, sl, :] = (y * jax.nn.sigmoid(og_ref[0, sl, :])).astype(o_ref.dtype)
        return state
    lax.fori_loop(0, nc, bwd, zero)


def _mlstm_pair_kernel(q_ref, k_ref, v_ref, og_ref, gc_ref, gr_ref, wq_ref, wk_ref, bq_ref, bk_ref, gbc_ref, gbr_ref,
                       mn_ref, o_ref, qc_ref, kt_ref, vb_ref, acc_ref, st_ref, cc_ref, cr_ref, *, t_ctx):
    L = MLSTM_CHUNK_LEN
    s_len = q_ref.shape[1]
    nc = s_len // L
    lane = lax.broadcasted_iota(jnp.int32, (1, LANES), 1)
    lo = lane < HEAD_DIM
    head_lanes = (lo, jnp.logical_not(lo))
    sub_lo = lax.broadcasted_iota(jnp.int32, (LANES, 1), 0) < HEAD_DIM
    head_rows = (sub_lo, jnp.logical_not(sub_lo))
    row_i = lax.broadcasted_iota(jnp.int32, (L, L), 0)
    col_i = lax.broadcasted_iota(jnp.int32, (L, L), 1)
    causal = (row_i >= col_i, row_i <= col_i)
    tri = tuple(c.astype(F32) for c in causal)
    tri_t = (tri[1], tri[0])
    sub = lax.broadcasted_iota(jnp.int32, (L, 1), 0)
    gate_row = lax.broadcasted_iota(jnp.int32, (2 * N_GATE_TYPES, 1), 0)
    ones = jnp.ones((L, LANES), BF16)

    def conv(x_ref, w_ref, b_ref, t0):
        x = x_ref[0, pl.ds(t0, L), :]
        prev = x_ref[0, pl.ds(jnp.maximum(t0 - 8, 0), 8), :][7:8]
        nxt = x_ref[0, pl.ds(jnp.minimum(t0 + L, s_len - 8), 8), :][0:1]
        prev = jnp.where((t0 != 0) & (t0 != t_ctx), prev, 0.0)
        nxt = jnp.where((t0 + L != t_ctx) & (t0 + L != s_len), nxt, 0.0)
        xm = jnp.where(sub == 0, prev, pltpu.roll(x, 1, axis=0))
        xp = jnp.where(sub == L - 1, nxt, pltpu.roll(x, L - 1, axis=0))
        y = b_ref[...] + xm * w_ref[0:1, :] + x * w_ref[1:2, :] + xp * w_ref[2:3, :]
        return y * jax.nn.sigmoid(y)

    def prep(c, carry):
        t0 = pl.multiple_of(c * L, L)
        sl = pl.ds(t0, L)
        qc_ref[sl, :] = conv(q_ref, wq_ref, bq_ref, t0).astype(BF16)
        kt_ref[:, sl] = (conv(k_ref, wk_ref, bk_ref, t0) * HEAD_DIM ** -0.5).T.astype(BF16)
        vb_ref[sl, :] = v_ref[0, sl, :].astype(BF16)
        lf_col = _log_sigmoid(gc_ref[0, sl, :] + gbc_ref[0])
        lf_row = _log_sigmoid(gr_ref[0, 0, :, sl] + gbr_ref[0])
        hdot = functools.partial(jnp.dot, precision=lax.Precision.HIGHEST, preferred_element_type=F32)
        cc_ref[sl, :] = jnp.where(lane < N_GATE_TYPES, hdot(tri[0], lf_col), hdot(tri[1], lf_col))
        cr_ref[:, sl] = jnp.where(gate_row < N_GATE_TYPES, hdot(lf_row, tri_t[0]), hdot(lf_row, tri_t[1]))
        return carry
    lax.fori_loop(0, nc, prep, 0, unroll=2)

    def chunk(d, c, m_state):
        sl = pl.ds(pl.multiple_of(c * L, L), L)
        q, kt, v = qc_ref[sl, :], kt_ref[:, sl], vb_ref[sl, :]
        g_row = gr_ref[0, 0, :, sl] + gbr_ref[0]
        cum_col, cum_row = cc_ref[sl, :], cr_ref[:, sl]
        end = L - 1 if d == 0 else 0
        outs, new_m = [], []
        for h2 in range(2):
            m_st = m_state[h2]
            ci, cf = 2 * (2 * d) + h2, 2 * (2 * d + 1) + h2
            a_rep = jnp.broadcast_to(cum_col[:, cf:cf + 1], (L, LANES))
            i_row, a_row = g_row[ci:ci + 1, :], cum_row[cf:cf + 1, :]
            b_tot = a_row[:, end:end + 1]
            d_log = jnp.where(causal[d], a_rep + (i_row - a_row), NEG_INF)
            m_intra = jnp.broadcast_to(jnp.max(d_log, axis=-1, keepdims=True), (L, LANES))
            qm = jnp.where(head_lanes[h2], q, jnp.zeros_like(q))
            s = jnp.dot(qm, kt, preferred_element_type=F32) * jnp.exp(d_log - m_intra)
            vm = jnp.where(head_lanes[h2], v, jnp.zeros_like(v))
            v1 = jnp.concatenate([vm, ones], axis=1)
            intra = jnp.dot(s.astype(BF16), v1, preferred_element_type=F32)
            inter = jnp.dot(qm, st_ref[h2].astype(BF16), preferred_element_type=F32)
            inter_log = a_rep + m_st
            m_q = jnp.maximum(m_intra, inter_log)
            a = jnp.exp(inter_log - m_q)
            g = jnp.exp(m_intra - m_q)
            num = a * inter[:, :LANES] + g * intra[:, :LANES]
            den = a * inter[:, LANES:] + g * intra[:, LANES:]
            outs.append(num / jnp.maximum(jnp.abs(den), jnp.exp(-m_q)))
            w_row = b_tot - a_row + i_row
            m_loc = jnp.max(w_row, axis=-1, keepdims=True)
            kts = jnp.where(head_rows[h2], kt.astype(F32) * jnp.exp(w_row - m_loc), 0.0).astype(BF16)
            loc = jnp.dot(kts, v1, preferred_element_type=F32)
            m_new = jnp.maximum(b_tot + m_st, m_loc)
            st_ref[h2] = jnp.exp(b_tot + m_st - m_new) * st_ref[h2] + jnp.exp(m_loc - m_new) * loc
            new_m.append(m_new)
        return sl, jnp.where(lo, outs[0], outs[1]), tuple(new_m)

    zero_m = (jnp.zeros((1, 1), F32), jnp.zeros((1, 1), F32))

    st_ref[...] = jnp.zeros_like(st_ref)

    def fwd(c, m_state):
        sl, y, m_state = chunk(0, c, m_state)
        acc_ref[sl, :] = y
        return m_state
    lax.fori_loop(0, nc, fwd, zero_m, unroll=2)

    st_ref[...] = jnp.zeros_like(st_ref)

    def bwd(i, m_state):
        c = _reverse_chunk(i, nc, t_ctx // L)
        sl, y, m_state = chunk(1, c, m_state)
        y = _head_rms(y + acc_ref[sl, :], mn_ref[...], lo)
        o_ref[0, sl, :] = (y * jax.nn.sigmoid(og_ref[0, sl, :])).astype(o_ref.dtype)
        return m_state
    lax.fori_loop(0, nc, bwd, zero_m, unroll=2)


def _mlstm(p_main, p_gate, conv_w, conv_b, gate_b, m_w, t_ctx, n_heads):
    b_sz, s_len, _ = p_main.shape
    n_hp = n_heads // 2
    L = MLSTM_CHUNK_LEN
    assert L == LANES and t_ctx % L == 0 and s_len % L == 0
    n_g = 2 * N_GATE_TYPES
    g_rows = jnp.transpose(p_gate.reshape(b_sz, s_len, n_hp, LANES)[..., :n_g], (0, 2, 3, 1))
    gb = jnp.transpose(gate_b.reshape(N_GATE_TYPES, n_hp, 2), (1, 0, 2)).reshape(n_hp, n_g)
    gb_col = jnp.zeros((n_hp, 1, LANES), F32).at[:, 0, :n_g].set(gb)
    gb_row = gb.reshape(n_hp, n_g, 1)
    vec = lambda col: pl.BlockSpec((1, LANES), lambda b, h: (0, col + h))
    return pl.pallas_call(
        functools.partial(_mlstm_pair_kernel, t_ctx=t_ctx),
        grid=(b_sz, n_hp),
        in_specs=[_seq_spec(s_len, 0), _seq_spec(s_len, n_hp), _seq_spec(s_len, 2 * n_hp),
                  _seq_spec(s_len, 3 * n_hp),
                  pl.BlockSpec((1, s_len, LANES), lambda b, h: (b, 0, h), pipeline_mode=pl.Buffered(1)),
                  pl.BlockSpec((1, 1, n_g, s_len), lambda b, h: (b, h, 0, 0)),
                  pl.BlockSpec((3, LANES), lambda b, h: (0, h)),
                  pl.BlockSpec((3, LANES), lambda b, h: (0, n_hp + h)),
                  vec(0), vec(n_hp),
                  pl.BlockSpec((1, 1, LANES), lambda b, h: (h, 0, 0)),
                  pl.BlockSpec((1, n_g, 1), lambda b, h: (h, 0, 0)),
                  pl.BlockSpec((1, LANES), lambda b, h: (0, 0))],
        out_specs=pl.BlockSpec((1, s_len, LANES), lambda b, h: (b, 0, h)),
        out_shape=jax.ShapeDtypeStruct((b_sz, s_len, n_heads * HEAD_DIM), BF16),
        scratch_shapes=[pltpu.VMEM((s_len, LANES), BF16), pltpu.VMEM((LANES, s_len), BF16),
                        pltpu.VMEM((s_len, LANES), BF16), pltpu.VMEM((s_len, LANES), F32),
                        pltpu.VMEM((2, LANES, 2 * LANES), F32),
                        pltpu.VMEM((s_len, LANES), F32), pltpu.VMEM((n_g, s_len), F32)],
        compiler_params=_cparams(("arbitrary", "arbitrary")),
        name="mlstm",
    )(p_main, p_main, p_main, p_main, p_gate, g_rows, conv_w, conv_w, conv_b.reshape(1, -1),
      conv_b.reshape(1, -1), gb_col, gb_row, jnp.concatenate([m_w, m_w]).reshape(1, LANES))


def _rms_norm(x, g):
    xf = x.astype(F32)
    y = xf * lax.rsqrt(jnp.mean(xf * xf, axis=-1, keepdims=True) + NORM_EPS)
    return (y * g.astype(F32)).astype(x.dtype)


def _split_heads(t, n_heads):
    return t.reshape(t.shape[0], t.shape[1], n_heads, HEAD_DIM)


def _to_bhtd(t):
    return jnp.transpose(t, (0, 2, 1, 3))


def _merge_heads(t):
    b, h, t_len, d = t.shape
    return jnp.transpose(t, (0, 2, 1, 3)).reshape(b, t_len, h * d)


def _flip_t(t):
    return jnp.flip(t, axis=2)


def _dwconv_centred(x, w, b):
    k_size = w.shape[0]
    pad = k_size // 2
    t_len = x.shape[1]
    xp = jnp.pad(x, ((0, 0), (pad, pad), (0, 0)))
    y = b
    for i in range(k_size):
        y = y + xp[:, i:i + t_len] * w[i]
    return y


def _axial_rope_tables(t_len):
    pos = jnp.arange(t_len)
    row = (pos // GRID_W).astype(F32)
    col = (pos % GRID_W).astype(F32)
    n = HEAD_DIM // 4
    inv = ROPE_BASE ** (-jnp.arange(n, dtype=F32) / n)
    ar = row[:, None] * inv
    ac = col[:, None] * inv
    return (jnp.cos(ar), jnp.sin(ar), jnp.cos(ac), jnp.sin(ac))


def _rotate_half(x, cos, sin):
    x1, x2 = jnp.split(x, 2, axis=-1)
    return jnp.concatenate([x1 * cos - x2 * sin, x2 * cos + x1 * sin], axis=-1)


def _axial_rope(x, cos_r, sin_r, cos_c, sin_c):
    xr, xc = jnp.split(x, 2, axis=-1)
    return jnp.concatenate([_rotate_half(xr, cos_r, sin_r), _rotate_half(xc, cos_c, sin_c)], axis=-1)


def _mlstm_chunkwise(q, k, v, log_i, log_f, state0):
    b_sz, h_sz, t_len, d = q.shape
    L = MLSTM_CHUNK
    nc = t_len // L
    qc = q.reshape(b_sz, h_sz, nc, L, d)
    kc = k.reshape(b_sz, h_sz, nc, L, d)
    vc = v.reshape(b_sz, h_sz, nc, L, d)
    li = log_i.reshape(b_sz, h_sz, nc, L)
    bcum = jnp.cumsum(log_f.reshape(b_sz, h_sz, nc, L), axis=-1)
    b_tot = bcum[..., -1]
    tri = jnp.tril(jnp.ones((L, L), dtype=bool))
    d_log = jnp.where(tri, bcum[..., :, None] - bcum[..., None, :] + li[..., None, :], NEG_INF)
    m_intra = jnp.max(d_log, axis=-1)
    s = jnp.einsum('bhcjd,bhcld->bhcjl', qc, kc).astype(F32) * jnp.exp(d_log - m_intra[..., None])
    num_intra = jnp.einsum('bhcjl,bhcld->bhcjd', s, vc)
    den_intra = jnp.sum(s, axis=-1)
    w_log = b_tot[..., None] - bcum + li
    m_loc = jnp.max(w_log, axis=-1)
    e = jnp.exp(w_log - m_loc[..., None])
    c_loc = jnp.einsum('bhcl,bhcld,bhcle->bhcde', e, kc, vc)
    n_loc = jnp.einsum('bhcl,bhcld->bhcd', e, kc)

    def step(carry, inp):
        c_st, n_st, m_st = carry
        cl, nl, ml, bt = inp
        m_new = jnp.maximum(bt + m_st, ml)
        a = jnp.exp(bt + m_st - m_new)
        g = jnp.exp(ml - m_new)
        new = (a[..., None, None] * c_st + g[..., None, None] * cl, a[..., None] * n_st + g[..., None] * nl, m_new)
        return new, carry

    xs = (jnp.moveaxis(c_loc, 2, 0), jnp.moveaxis(n_loc, 2, 0), jnp.moveaxis(m_loc, 2, 0), jnp.moveaxis(b_tot, 2, 0))
    final, prev = lax.scan(step, state0, xs)
    c_prev = jnp.moveaxis(prev[0], 0, 2)
    n_prev = jnp.moveaxis(prev[1], 0, 2)
    m_prev = jnp.moveaxis(prev[2], 0, 2)
    inter_log = bcum + m_prev[..., None]
    m_q = jnp.maximum(m_intra, inter_log)
    a = jnp.exp(inter_log - m_q)
    g = jnp.exp(m_intra - m_q)
    num = a[..., None] * jnp.einsum('bhcjd,bhcde->bhcje', qc, c_prev) + g[..., None] * num_intra
    den = a * jnp.einsum('bhcjd,bhcd->bhcj', qc, n_prev) + g * den_intra
    h = num / jnp.maximum(jnp.abs(den), jnp.exp(-m_q))[..., None]
    return h.reshape(b_sz, h_sz, t_len, d), final


def _retention_log_decay(first_exp, n_heads):
    e = first_exp + 2.0 * jnp.arange(n_heads, dtype=F32)
    return jnp.log1p(-jnp.exp2(-e))


def _retention_chunkwise(q, k, v, log_gamma, state0):
    b_sz, h_sz, t_len, d = q.shape
    L = RET_CHUNK
    nc = t_len // L
    qc = q.reshape(b_sz, h_sz, nc, L, d)
    kc = k.reshape(b_sz, h_sz, nc, L, d)
    vc = v.reshape(b_sz, h_sz, nc, L, d)
    pos = jnp.arange(L, dtype=F32)
    diff = pos[:, None] - pos[None, :]
    decay = jnp.where(diff >= 0, jnp.exp(log_gamma[:, None, None] * jnp.maximum(diff, 0.0)), 0.0)
    s = jnp.einsum('bhcjd,bhcld->bhcjl', qc, kc).astype(F32) * decay[:, None]
    intra = jnp.einsum('bhcjl,bhcld->bhcjd', s, vc)
    zeta = jnp.exp(log_gamma[:, None] * (L - 1 - pos))
    s_loc = jnp.einsum('hl,bhcld,bhcle->bhcde', zeta, kc, vc)
    g_chunk = jnp.exp(log_gamma * L)[:, None, None]

    def step(r, sl):
        return g_chunk * r + sl, r

    final, r_prev = lax.scan(step, state0, jnp.moveaxis(s_loc, 2, 0))
    r_prev = jnp.moveaxis(r_prev, 0, 2)
    xi = jnp.exp(log_gamma[:, None] * (pos + 1.0))
    inter = jnp.einsum('bhcjd,bhcde->bhcje', qc, r_prev) * xi[:, None, :, None]
    return (intra + inter).reshape(b_sz, h_sz, t_len, d), final


def _na_window(rows):
    wr = min(NA_ROWS, rows)
    r = jnp.arange(rows)
    col = jnp.arange(GRID_W)
    row_idx = jnp.clip(r - wr // 2, 0, rows - wr)[:, None] + jnp.arange(wr)[None, :]
    col_start = jnp.clip(col - NA_COLS // 2, 0, GRID_W - NA_COLS)
    col_in = (col[None, :] >= col_start[:, None]) & (col[None, :] < col_start[:, None] + NA_COLS)
    dr = row_idx - r[:, None] + NA_ROWS - 1
    dc = jnp.clip(col[None, :] - col[:, None] + NA_COLS - 1, 0, 2 * NA_COLS - 2)
    return (row_idx, col_in, dr[:, None, :, None], dc[None, :, None, :])


def _na_latent(q, k, v, k_ctx, v_ctx, row_idx, col_in, bias):
    t_len, h_sz, d = q.shape
    rows, wr = row_idx.shape
    scale = d ** -0.5
    qg = q.reshape(rows, GRID_W, h_sz, d)
    kb = k.reshape(rows, GRID_W, h_sz, d)[row_idx]
    vb = v.reshape(rows, GRID_W, h_sz, d)[row_idx]
    s_loc = jnp.einsum('rqhd,rjwhd->hrqjw', qg, kb).astype(F32) * scale + bias
    s_loc = jnp.where(col_in[:, None, :], s_loc, NEG_INF)
    s_ctx = jnp.einsum('rqhd,chd->hrqc', qg, k_ctx).astype(F32) * scale
    n_loc = wr * GRID_W
    s = jnp.concatenate([s_loc.reshape(h_sz, rows, GRID_W, n_loc), s_ctx], axis=-1)
    p = jax.nn.softmax(s, axis=-1).astype(v.dtype)
    p_loc = p[..., :n_loc].reshape(h_sz, rows, GRID_W, wr, GRID_W)
    o = jnp.einsum('hrqjw,rjwhd->rqhd', p_loc, vb) + jnp.einsum('hrqc,chd->rqhd', p[..., n_loc:], v_ctx)
    return o.reshape(t_len, h_sz * d)


def _ctx_attention(q, k, v):
    s = jnp.einsum('bqhd,bkhd->bhqk', q, k).astype(F32) * HEAD_DIM ** -0.5
    p = jax.nn.softmax(s, axis=-1).astype(v.dtype)
    o = jnp.einsum('bhqk,bkhd->bqhd', p, v)
    return o.reshape(o.shape[0], o.shape[1], -1)


def _hybrid_mixer(xp, cp, rope, na_win, conv_w, conv_b, gate_b, m_norm, q_norm, k_norm, rpb, r_norm, dims):
    h_m, h_na, h_r = dims
    b_sz = xp[0].shape[0]

    def mlstm_prep(p):
        qk = jax.nn.silu(_dwconv_centred(p[0], conv_w, conv_b))
        q, k = jnp.split(qk, 2, axis=-1)
        g = (p[3] + gate_b).astype(F32)
        g = jnp.transpose(g.reshape(b_sz, -1, 4, h_m), (2, 0, 3, 1))
        return (_to_bhtd(_split_heads(q, h_m)), _to_bhtd(_split_heads(k, h_m)) * HEAD_DIM ** -0.5,
                _to_bhtd(_split_heads(p[1], h_m)), g[0], jax.nn.log_sigmoid(g[1]), g[2], jax.nn.log_sigmoid(g[3]))

    qx, kx, vx, ix_f, lfx_f, ix_b, lfx_b = mlstm_prep(xp)
    qc, kc, vc, ic_f, lfc_f, ic_b, lfc_b = mlstm_prep(cp)
    zero_m = (jnp.zeros((b_sz, h_m, HEAD_DIM, HEAD_DIM), F32),
              jnp.zeros((b_sz, h_m, HEAD_DIM), F32), jnp.zeros((b_sz, h_m), F32))
    hc_f, st_f = _mlstm_chunkwise(qc, kc, vc, ic_f, lfc_f, zero_m)
    hc_b, st_b = _mlstm_chunkwise(_flip_t(qc), _flip_t(kc), _flip_t(vc), _flip_t(ic_b), _flip_t(lfc_b), zero_m)
    hx_f, _ = _mlstm_chunkwise(qx, kx, vx, ix_f, lfx_f, st_f)
    hx_b, _ = _mlstm_chunkwise(_flip_t(qx), _flip_t(kx), _flip_t(vx), _flip_t(ix_b), _flip_t(lfx_b), st_b)

    def mlstm_out(h, o):
        return _merge_heads(_rms_norm(h, m_norm)).astype(o.dtype) * jax.nn.sigmoid(o)

    a_x = mlstm_out(hx_f + _flip_t(hx_b), xp[2])

    def na_prep(p):
        return (_rms_norm(_split_heads(p[4], h_na), q_norm), _rms_norm(_split_heads(p[5], h_na), k_norm),
                _split_heads(p[6], h_na))

    nqx, nkx, nvx = na_prep(xp)
    nqc, nkc, nvc = na_prep(cp)
    row_idx, col_in, dr, dc = na_win
    bias = rpb[:, dr, dc].astype(F32)
    b_x = lax.map(lambda a: _na_latent(a[0], a[1], a[2], a[3], a[4], row_idx, col_in, bias),
                  (nqx, nkx, nvx, nkc, nvc))

    lg_f = _retention_log_decay(5.0, h_r)
    lg_b = _retention_log_decay(6.0, h_r)
    rqx = _axial_rope(_to_bhtd(_split_heads(xp[7], h_r)), *rope)
    rkx = _axial_rope(_to_bhtd(_split_heads(xp[8], h_r)), *rope) * HEAD_DIM ** -0.5
    rvx = _to_bhtd(_split_heads(xp[9], h_r))
    rqc = _to_bhtd(_split_heads(cp[7], h_r))
    rkc = _to_bhtd(_split_heads(cp[8], h_r)) * HEAD_DIM ** -0.5
    rvc = _to_bhtd(_split_heads(cp[9], h_r))
    zero_r = jnp.zeros((b_sz, h_r, HEAD_DIM, HEAD_DIM), F32)
    rc_f, rs_f = _retention_chunkwise(rqc, rkc, rvc, lg_f, zero_r)
    rc_b, rs_b = _retention_chunkwise(_flip_t(rqc), _flip_t(rkc), _flip_t(rvc), lg_b, zero_r)
    rx_f, _ = _retention_chunkwise(rqx, rkx, rvx, lg_f, rs_f)
    rx_b, _ = _retention_chunkwise(_flip_t(rqx), _flip_t(rkx), _flip_t(rvx), lg_b, rs_b)

    def ret_out(h, g):
        return _merge_heads(_rms_norm(h, r_norm)).astype(g.dtype) * jax.nn.silu(g)

    c_x = ret_out(rx_f + _flip_t(rx_b), xp[10])
    mix_x = jnp.concatenate([a_x, b_x, c_x], axis=-1)
    a_c = mlstm_out(hc_f + _flip_t(hc_b), cp[2])
    b_c = _ctx_attention(nqc, nkc, nvc)
    c_c = ret_out(rc_f + _flip_t(rc_b), cp[10])
    return mix_x, jnp.concatenate([a_c, b_c, c_c], axis=-1)


def kernel(x, c, ctx, c_ctx, w_mod, b_mod, norm_mix, norm_ffn, w_in, w_out, mlstm_conv_w, mlstm_conv_b,
           mlstm_gate_b, mlstm_norm, na_q_norm, na_k_norm, na_rpb, ret_norm, router_w, router_b,
           expert_w_up, expert_b_up, expert_w_down, expert_b_down):
    b_sz, t_len, d = x.shape
    t_ctx = ctx.shape[1]
    depth = w_in.shape[0]
    n_e = router_w.shape[2]
    d_mix = w_out.shape[1]
    h_m = d_mix // (4 * HEAD_DIM)
    h_na = d_mix // (2 * HEAD_DIM)
    h_r = d_mix // (4 * HEAD_DIM)
    d_m, d_na, d_r = h_m * HEAD_DIM, h_na * HEAD_DIM, h_r * HEAD_DIM
    n_gate = 4 * h_m
    assert b_sz + 1 <= MOD_ROWS and n_e <= LANES and n_gate <= LANES
    s_len = t_ctx + t_len
    assert s_len % ROW_TILE == 0

    s = jnp.concatenate([ctx, x], axis=1)
    cc = jnp.zeros((MOD_ROWS, d), F32).at[:b_sz].set(c).at[b_sz].set(c_ctx)
    mods = _modulation(cc, w_mod, b_mod).reshape(depth, MOD_ROWS, 1, 6 * d)

    g0 = 4 * d_m
    col_na = g0
    col_ret = col_na + 3 * d_na
    n_hp = h_m // 2
    gate_src = np.array([[g0 + t * h_m + 2 * hp + h2 for t in range(N_GATE_TYPES) for h2 in range(2)]
                         for hp in range(n_hp)])
    rope_tabs = _rope_tables(t_ctx, t_len)
    ret_tabs = _ret_tables(h_r)
    na_tabs = _na_bias_tables(na_rpb)
    f2 = expert_w_up.shape[3]
    w_up_all = expert_w_up.reshape(depth * n_e, d, f2)
    w_down_all = expert_w_down.reshape(depth * n_e, f2 // 2, d)
    b_up_all = _regroup_bias(expert_b_up).reshape(depth * n_e, 1, f2)
    b_down_all = expert_b_down.reshape(depth * n_e, 1, d)

    for l in range(depth):
        w_main = jnp.concatenate([w_in[l, :, :g0], w_in[l, :, g0 + n_gate:]], axis=1).astype(BF16)
        w_gate = jnp.zeros((d, n_hp, LANES), BF16).at[:, :, :gate_src.shape[1]].set(
            w_in[l][:, gate_src].astype(BF16)).reshape(d, n_hp * LANES)
        p_main, p_gate = _proj_in(s, mods[l], norm_mix[l], w_main, w_gate, t_ctx)

        mix = [_mlstm(p_main, p_gate, mlstm_conv_w[l], mlstm_conv_b[l], mlstm_gate_b[l], mlstm_norm[l],
                      t_ctx, h_m),
               _na_attention(p_main, na_q_norm[l], na_k_norm[l], na_tabs[l], t_ctx, col_na, h_na),
               _retention(p_main, rope_tabs, ret_tabs, ret_norm[l], t_ctx, col_ret, h_r)]

        rw = jnp.zeros((d, LANES), F32).at[:, :n_e].set(router_w[l])
        rb = jnp.zeros((1, LANES), F32).at[0, :n_e].set(router_b[l])
        s, tok, logits = _proj_out(mix, s, mods[l], norm_ffn[l], w_out[l].astype(BF16), rw, rb, t_ctx)

        s = _moe(tok, logits[..., :n_e], s, mods[l], t_ctx, n_e, l, w_up_all, w_down_all, b_up_all, b_down_all)
    return s[:, t_ctx:]
```

```python
import functools

import jax
import jax.numpy as jnp
import numpy as np
from jax import lax
from jax.experimental import pallas as pl
from jax.experimental.pallas import tpu as pltpu

F32 = jnp.float32
BF16 = jnp.bfloat16

GRID_W = 64
HEAD_DIM = 64
MLSTM_CHUNK = 64
RET_CHUNK = 64
NA_ROWS = 8
NA_COLS = 16
ROPE_BASE = 10000.0
TOP_K = 4
SWIGLU_ALPHA = 1.702
SWIGLU_LIMIT = 7.0
NORM_EPS = 1e-6
NEG_INF = -1e30

LANES = 128
VMEM_LIMIT = 48 * 1024 * 1024
MOD_ROWS = 8
ROW_TILE = 768
MOE_TILE = 512
MOE_VMEM_LIMIT = 56 * 1024 * 1024


def _cparams(sem):
    return pltpu.CompilerParams(dimension_semantics=sem, vmem_limit_bytes=VMEM_LIMIT)


def _mod_kernel(cc_ref, w_ref, b_ref, o_ref):
    cc = cc_ref[...]
    a = cc * jax.nn.sigmoid(cc)
    o_ref[0] = jnp.dot(a, w_ref[0], precision=lax.Precision.HIGHEST,
                       preferred_element_type=F32) + b_ref[0]


def _modulation(cc, w_mod, b_mod):
    n_l, d, d6 = w_mod.shape
    tn = d6 // 4
    return pl.pallas_call(
        _mod_kernel,
        grid=(n_l, d6 // tn),
        in_specs=[pl.BlockSpec((MOD_ROWS, d), lambda l, j: (0, 0)),
                  pl.BlockSpec((1, d, tn), lambda l, j: (l, 0, j)),
                  pl.BlockSpec((1, 1, tn), lambda l, j: (l, 0, j))],
        out_specs=pl.BlockSpec((1, MOD_ROWS, tn), lambda l, j: (l, 0, j)),
        out_shape=jax.ShapeDtypeStruct((n_l, MOD_ROWS, d6), F32),
        compiler_params=_cparams(("arbitrary", "arbitrary")),
        name="adaln_modulation",
    )(cc, w_mod, b_mod.reshape(n_l, 1, d6))


def _pick_mod(mb_ref, mc_ref, k, d, is_ctx):
    vb = mb_ref[0, :, k * d:(k + 1) * d]
    vc = mc_ref[0, :, k * d:(k + 1) * d]
    return jnp.where(is_ctx, vc, vb)


def _rms(x, g):
    return x * lax.rsqrt(jnp.mean(x * x, axis=-1, keepdims=True) + NORM_EPS) * g


def _pack_bf16_pairs(x):
    m = x.shape[1] // 2
    hi = lax.bitcast_convert_type(x[:, :m].astype(BF16).astype(F32), jnp.uint32)
    lo = lax.bitcast_convert_type(x[:, m:].astype(BF16).astype(F32), jnp.uint32)
    return hi | (lo >> 16)


def _unpack_bf16_pairs(p):
    hi = lax.bitcast_convert_type(p & jnp.uint32(0xFFFF0000), F32)
    lo = lax.bitcast_convert_type(p << 16, F32)
    return jnp.concatenate([hi, lo], axis=1)


def _proj_in_kernel(s_ref, mb_ref, mc_ref, g_ref, w_ref, wg_ref, pm_ref, pg_ref, xn_ref, *, t_ctx):
    i = pl.program_id(1)
    j = pl.program_id(2)
    tm, d = xn_ref.shape

    @pl.when(j == 0)
    def _():
        row = lax.broadcasted_iota(jnp.int32, (tm, 1), 0) + i * tm
        is_ctx = row < t_ctx
        sh = _pick_mod(mb_ref, mc_ref, 0, d, is_ctx)
        sc = _pick_mod(mb_ref, mc_ref, 1, d, is_ctx)
        h = _rms(s_ref[0], g_ref[...]) * (1.0 + sc) + sh
        xn_ref[...] = h.astype(BF16)
        pg_ref[0] = jnp.dot(xn_ref[...], wg_ref[...], preferred_element_type=F32)

    pm_ref[0] = jnp.dot(xn_ref[...], w_ref[...], preferred_element_type=F32)


def _proj_in(s, mod_l, g, w_main, w_gate, t_ctx):
    b_sz, s_len, d = s.shape
    n_main = w_main.shape[1]
    n_gate = w_gate.shape[1]
    tn = n_main // 4
    tm = ROW_TILE
    return pl.pallas_call(
        functools.partial(_proj_in_kernel, t_ctx=t_ctx),
        grid=(b_sz, s_len // tm, n_main // tn),
        in_specs=[pl.BlockSpec((1, tm, d), lambda b, i, j: (b, i, 0)),
                  pl.BlockSpec((1, 1, 6 * d), lambda b, i, j: (b, 0, 0)),
                  pl.BlockSpec((1, 1, 6 * d), lambda b, i, j: (b_sz, 0, 0)),
                  pl.BlockSpec((1, d), lambda b, i, j: (0, 0)),
                  pl.BlockSpec((d, tn), lambda b, i, j: (0, j)),
                  pl.BlockSpec((d, n_gate), lambda b, i, j: (0, 0))],
        out_specs=[pl.BlockSpec((1, tm, tn), lambda b, i, j: (b, i, j)),
                   pl.BlockSpec((1, tm, n_gate), lambda b, i, j: (b, i, 0))],
        out_shape=[jax.ShapeDtypeStruct((b_sz, s_len, n_main), F32),
                   jax.ShapeDtypeStruct((b_sz, s_len, n_gate), F32)],
        scratch_shapes=[pltpu.VMEM((tm, d), BF16)],
        compiler_params=_cparams(("arbitrary", "arbitrary", "arbitrary")),
        name="proj_in",
    )(s, mod_l, mod_l, g.reshape(1, d), w_main, w_gate)


def _proj_out_kernel(ma_ref, mb2_ref, mc2_ref, s_ref, mb_ref, mc_ref, g_ref, w_ref, rw_ref, rb_ref,
                     so_ref, tok_ref, lg_ref, *, t_ctx):
    i = pl.program_id(1)
    tm, d = s_ref.shape[1], s_ref.shape[2]
    row = lax.broadcasted_iota(jnp.int32, (tm, 1), 0) + i * tm
    is_ctx = row < t_ctx
    g1 = _pick_mod(mb_ref, mc_ref, 2, d, is_ctx)
    ka, kb = ma_ref.shape[2], ma_ref.shape[2] + mb2_ref.shape[2]
    y = (jnp.dot(ma_ref[0], w_ref[0:ka, :], preferred_element_type=F32)
         + jnp.dot(mb2_ref[0], w_ref[ka:kb, :], preferred_element_type=F32)
         + jnp.dot(mc2_ref[0], w_ref[kb:, :], preferred_element_type=F32))
    s_new = s_ref[0] + g1 * y
    so_ref[0] = s_new
    sh = _pick_mod(mb_ref, mc_ref, 3, d, is_ctx)
    sc = _pick_mod(mb_ref, mc_ref, 4, d, is_ctx)
    t = _rms(s_new, g_ref[...]) * (1.0 + sc) + sh
    tok_ref[0] = t
    lg_ref[0] = jnp.dot(t, rw_ref[...], precision=lax.Precision.HIGHEST,
                        preferred_element_type=F32) + rb_ref[...]


def _proj_out(mix_parts, s, mod_l, g, w_out, rw, rb, t_ctx):
    b_sz, s_len, d = s.shape
    tm = ROW_TILE
    row_spec = pl.BlockSpec((1, tm, d), lambda b, i: (b, i, 0))
    part_specs = [pl.BlockSpec((1, tm, m.shape[2]), lambda b, i: (b, i, 0)) for m in mix_parts]
    return pl.pallas_call(
        functools.partial(_proj_out_kernel, t_ctx=t_ctx),
        grid=(b_sz, s_len // tm),
        in_specs=part_specs + [row_spec,
                  pl.BlockSpec((1, 1, 6 * d), lambda b, i: (b, 0, 0)),
                  pl.BlockSpec((1, 1, 6 * d), lambda b, i: (b_sz, 0, 0)),
                  pl.BlockSpec((1, d), lambda b, i: (0, 0)),
                  pl.BlockSpec((d, d), lambda b, i: (0, 0)),
                  pl.BlockSpec((d, LANES), lambda b, i: (0, 0)),
                  pl.BlockSpec((1, LANES), lambda b, i: (0, 0))],
        out_specs=[row_spec, row_spec, pl.BlockSpec((1, tm, LANES), lambda b, i: (b, i, 0))],
        out_shape=[jax.ShapeDtypeStruct((b_sz, s_len, d), F32),
                   jax.ShapeDtypeStruct((b_sz, s_len, d), F32),
                   jax.ShapeDtypeStruct((b_sz, s_len, LANES), F32)],
        compiler_params=_cparams(("arbitrary", "arbitrary")),
        name="proj_out_router",
    )(*mix_parts, s, mod_l, mod_l, g.reshape(1, d), w_out, rw, rb)


PAIR = 2 * LANES


def _regroup_perm():
    dst = np.arange(PAIR)
    src = np.where(dst < LANES, 2 * dst, 2 * (dst - LANES) + 1)
    return jnp.asarray(np.arange(PAIR)[:, None] == src[None, :], BF16)


def _regroup_bias(b_up):
    lead = b_up.shape[:-1]
    b = b_up.reshape(*lead, -1, LANES, 2)
    return jnp.swapaxes(b, -1, -2).reshape(*lead, -1)


def _moe_kernel(be_ref, nu_ref, x_ref, wu_ref, wd_ref, bu_ref, bd_ref, p_ref, y_ref, wus_ref, wds_ref):
    i = pl.program_id(0)
    d, f2 = wus_ref.shape
    rows = min(512, d)

    @pl.when(i < nu_ref[0])
    def _():
        @pl.when((i == 0) | (be_ref[i] != be_ref[jnp.maximum(i - 1, 0)]))
        def _():
            for r in range(d // rows):
                for j in range(f2 // PAIR):
                    w = wu_ref[0, r * rows:(r + 1) * rows, j * PAIR:(j + 1) * PAIR].astype(BF16)
                    wus_ref[r * rows:(r + 1) * rows, j * PAIR:(j + 1) * PAIR] = jnp.dot(
                        w, p_ref[...], preferred_element_type=F32).astype(BF16)
            wds_ref[...] = wd_ref[0].astype(BF16)

        up = jnp.dot(x_ref[...].astype(BF16), wus_ref[...], preferred_element_type=F32) + bu_ref[0]
        acts = []
        for j in range(f2 // PAIR):
            glu = jnp.minimum(up[:, j * PAIR:j * PAIR + LANES], SWIGLU_LIMIT)
            lin = jnp.clip(up[:, j * PAIR + LANES:(j + 1) * PAIR], -SWIGLU_LIMIT, SWIGLU_LIMIT)
            acts.append((glu * jax.nn.sigmoid(SWIGLU_ALPHA * glu) * (lin + 1.0)).astype(BF16))
        act = jnp.concatenate(acts, axis=1)
        y_ref[...] = _pack_bf16_pairs(jnp.dot(act, wds_ref[...], preferred_element_type=F32) + bd_ref[0])

    @pl.when(i >= nu_ref[0])
    def _():
        y_ref[...] = jnp.zeros_like(y_ref)


def _moe_blocks(blk_e, n_used, x_sorted, w_up, w_down, b_up, b_down):
    n_rows, d = x_sorted.shape
    _, _, f2 = w_up.shape
    assert f2 % PAIR == 0 and d % min(512, d) == 0
    tm = MOE_TILE
    wmap = lambda i, be, nu: (be[i], 0, 0)
    return pl.pallas_call(
        _moe_kernel,
        grid_spec=pltpu.PrefetchScalarGridSpec(
            num_scalar_prefetch=2,
            grid=(n_rows // tm,),
            in_specs=[pl.BlockSpec((tm, d), lambda i, be, nu: (i, 0)),
                      pl.BlockSpec((1, d, f2), wmap),
                      pl.BlockSpec((1, f2 // 2, d), wmap),
                      pl.BlockSpec((1, 1, f2), wmap),
                      pl.BlockSpec((1, 1, d), wmap),
                      pl.BlockSpec((PAIR, PAIR), lambda i, be, nu: (0, 0))],
            out_specs=pl.BlockSpec((tm, d // 2), lambda i, be, nu: (i, 0)),
            scratch_shapes=[pltpu.VMEM((d, f2), BF16), pltpu.VMEM((f2 // 2, d), BF16)]),
        out_shape=jax.ShapeDtypeStruct((n_rows, d // 2), jnp.uint32),
        compiler_params=pltpu.CompilerParams(dimension_semantics=("arbitrary",),
                                             vmem_limit_bytes=MOE_VMEM_LIMIT),
        name="moe_expert_blocks",
    )(blk_e, n_used, x_sorted, w_up, w_down, b_up, b_down, _regroup_perm())


def _combine_kernel(y_ref, gt_ref, s_ref, mb_ref, mc_ref, o_ref, *, t_ctx):
    i = pl.program_id(1)
    tm, d = s_ref.shape[1], s_ref.shape[2]
    row = lax.broadcasted_iota(jnp.int32, (tm, 1), 0) + i * tm
    g2 = _pick_mod(mb_ref, mc_ref, 5, d, row < t_ctx)
    gt = gt_ref[0]
    y = _unpack_bf16_pairs(y_ref[0, 0]) * gt[:, 0:1]
    for k in range(1, y_ref.shape[0]):
        y = y + _unpack_bf16_pairs(y_ref[k, 0]) * gt[:, k:k + 1]
    o_ref[0] = s_ref[0] + g2 * y


def _combine(y_as, gates, s, mod_l, t_ctx):
    b_sz, s_len, d = s.shape
    n_k = y_as.shape[0]
    tm = ROW_TILE // 3
    row_spec = pl.BlockSpec((1, tm, d), lambda b, i: (b, i, 0))
    return pl.pallas_call(
        functools.partial(_combine_kernel, t_ctx=t_ctx),
        grid=(b_sz, s_len // tm),
        in_specs=[pl.BlockSpec((n_k, 1, tm, d // 2), lambda b, i: (0, b, i, 0)),
                  pl.BlockSpec((1, tm, n_k), lambda b, i: (b, i, 0)),
                  row_spec,
                  pl.BlockSpec((1, 1, 6 * d), lambda b, i: (b, 0, 0)),
                  pl.BlockSpec((1, 1, 6 * d), lambda b, i: (b_sz, 0, 0))],
        out_specs=row_spec,
        out_shape=jax.ShapeDtypeStruct((b_sz, s_len, d), F32),
        compiler_params=_cparams(("arbitrary", "arbitrary")),
        name="moe_combine",
    )(y_as, gates, s, mod_l, mod_l)


def _moe(tok, logits, s, mod_l, t_ctx, n_e, layer, w_up, w_down, b_up, b_down):
    b_sz, s_len, d = tok.shape
    n_tok = b_sz * s_len
    tm = MOE_TILE
    top_v, top_e = lax.top_k(logits.reshape(n_tok, n_e), TOP_K)
    gates = jax.nn.softmax(top_v, axis=-1)
    n_as = n_tok * TOP_K
    onehot = jnp.sum((top_e[:, :, None] == jnp.arange(n_e)[None, None, :]).astype(jnp.int32), axis=1)
    csum = jnp.cumsum(onehot, axis=0)
    counts = csum[-1]
    padded = (counts + tm - 1) // tm * tm
    end_pad = jnp.cumsum(padded)
    start_pad = end_pad - padded
    start = jnp.cumsum(counts) - counts
    dest = jnp.take_along_axis(csum - onehot + start_pad[None, :], top_e, axis=1).astype(jnp.int32)
    n_blocks = -(-n_as // tm) + n_e
    blk_first = jnp.arange(n_blocks) * tm
    blk_e = jnp.minimum(jnp.sum(blk_first[:, None] >= end_pad[None, :], axis=1), n_e - 1).astype(jnp.int32)
    n_used = (end_pad[-1:] // tm).astype(jnp.int32)
    tok_sorted = (jnp.argsort(top_e.reshape(n_as)) // TOP_K).astype(jnp.int32)
    j = (blk_first - start_pad[blk_e])[:, None] + jnp.arange(tm)[None, :]
    src = jnp.clip(start[blk_e][:, None] + j, 0, n_as - 1)
    row_tok = jnp.where(j < counts[blk_e][:, None], tok_sorted[src], 0).reshape(n_blocks * tm)
    x_sorted = tok.reshape(n_tok, d)[row_tok]
    y = _moe_blocks(blk_e + layer * n_e, n_used, x_sorted, w_up, w_down, b_up, b_down)
    y_as = y[dest.T].reshape(TOP_K, b_sz, s_len, d // 2)
    return _combine(y_as, gates.reshape(b_sz, s_len, TOP_K), s, mod_l, t_ctx)


NA_STEP = 256
NA_KEYS = NA_ROWS * GRID_W


def _head_rms(x, w, lo):
    xx = x * x
    s0 = jnp.sum(jnp.where(lo, xx, 0.0), axis=-1, keepdims=True)
    s1 = jnp.sum(jnp.where(lo, 0.0, xx), axis=-1, keepdims=True)
    inv = lax.rsqrt(jnp.where(lo, s0, s1) * (1.0 / HEAD_DIM) + NORM_EPS)
    return x * inv * w


def _dot_nt(a, b):
    return lax.dot_general(a, b, (((1,), (1,)), ((), ())), preferred_element_type=F32)


def _na_kernel(q_ref, k_ref, v_ref, qw_ref, kw_ref, bias_ref, o_ref, kt_ref, vb_ref, s_ref, *, t_ctx, rows):
    rg = pl.program_id(2)
    s_len = k_ref.shape[1]
    lo = lax.broadcasted_iota(jnp.int32, (1, LANES), 1) < HEAD_DIM
    rows_per_step = NA_STEP // GRID_W

    @pl.when(rg == 0)
    def _():
        def knorm_t(t0, n):
            return _head_rms(k_ref[0, pl.ds(t0, n), :], kw_ref[...], lo).T.astype(BF16)

        def prep(c, carry):
            t0 = pl.multiple_of(c * NA_STEP, NA_STEP)
            kt_ref[0, :, pl.ds(t0, NA_STEP)] = knorm_t(t0, NA_STEP)
            vb_ref[pl.ds(t0, NA_STEP), :] = jnp.concatenate(
                [v_ref[0, pl.ds(t0, NA_STEP), :].astype(BF16), jnp.ones((NA_STEP, LANES), BF16)], axis=1)
            return carry
        lax.fori_loop(0, s_len // NA_STEP, prep, 0, unroll=3)

        def prep_shifted(c, carry):
            t0 = pl.multiple_of(c * LANES, LANES)
            kt_ref[1, :, pl.ds(t0, LANES)] = knorm_t(pl.multiple_of(t0 + GRID_W, GRID_W), LANES)
            return carry
        lax.fori_loop(0, (s_len - GRID_W) // LANES, prep_shifted, 0, unroll=5)

    qn = _head_rms(q_ref[0], qw_ref[...], lo) * (HEAD_DIM ** -0.5)
    q0 = jnp.where(lo, qn, 0.0).astype(BF16)
    q1 = jnp.where(lo, 0.0, qn).astype(BF16)
    kc_t = kt_ref[0, :, 0:t_ctx]
    vc = vb_ref[0:t_ctx, :]

    def finish(g, o2):
        o2 = o2[:, :LANES] * (1.0 / o2[:, LANES:])
        o = jnp.where(lo, o2[:GRID_W], o2[GRID_W:])
        o_ref[0, g * GRID_W:(g + 1) * GRID_W, :] = o.astype(o_ref.dtype)

    @pl.when(rg == 0)
    def _():
        for g in range(rows_per_step):
            q2 = jnp.concatenate([q0[g * GRID_W:(g + 1) * GRID_W], q1[g * GRID_W:(g + 1) * GRID_W]], axis=0)
            s_c = jnp.dot(q2, kc_t, preferred_element_type=F32)
            p_c = jnp.exp(s_c - jnp.max(s_c, axis=-1, keepdims=True))
            finish(g, jnp.dot(p_c.astype(BF16), vc, preferred_element_type=F32))

    @pl.when(rg > 0)
    def _():
        t0s = []
        for g in range(rows_per_step):
            r = (rg - 1) * rows_per_step + g
            row_start = jnp.clip(r - NA_ROWS // 2, 0, rows - NA_ROWS)
            dr0 = row_start - r + NA_ROWS - 1
            t0 = pl.multiple_of(t_ctx + row_start * GRID_W, GRID_W)
            odd = (t0 // GRID_W) % (LANES // GRID_W)
            kw_t = kt_ref[odd, :, pl.ds(pl.multiple_of(t0 - odd * GRID_W, LANES), NA_KEYS)]
            q2 = jnp.concatenate([q0[g * GRID_W:(g + 1) * GRID_W], q1[g * GRID_W:(g + 1) * GRID_W]], axis=0)
            s_ref[g, :, :NA_KEYS] = jnp.dot(q2, kw_t, preferred_element_type=F32) + bias_ref[dr0, 0]
            s_ref[g, :, NA_KEYS:] = jnp.dot(q2, kc_t, preferred_element_type=F32)
            t0s.append(t0)
        for g in range(rows_per_step):
            s = s_ref[g]
            p = jnp.exp(s - jnp.max(s, axis=-1, keepdims=True)).astype(BF16)
            finish(g, jnp.dot(p[:, :NA_KEYS], vb_ref[pl.ds(t0s[g], NA_KEYS), :], preferred_element_type=F32)
                   + jnp.dot(p[:, NA_KEYS:], vc, preferred_element_type=F32))


def _na_bias_tables(rpb):
    n_l, n_h, n_dr, n_dc = rpb.shape
    col = np.arange(GRID_W)
    dc = np.clip(col[None, :] - col[:, None] + NA_COLS - 1, 0, n_dc - 1)
    onehot = (dc.reshape(1, -1) == np.arange(n_dc)[:, None]).astype(np.float32)
    toe = jnp.dot(rpb.reshape(-1, n_dc), jnp.asarray(onehot), precision=lax.Precision.HIGHEST)
    toe = toe.reshape(n_l, n_h, n_dr, GRID_W, GRID_W)
    col_start = np.clip(col - NA_COLS // 2, 0, GRID_W - NA_COLS)
    col_in = (col[None, :] >= col_start[:, None]) & (col[None, :] < col_start[:, None] + NA_COLS)
    toe = jnp.where(jnp.asarray(col_in), toe, NEG_INF)
    tabs = []
    for dr0 in range(NA_ROWS):
        t = toe[:, :, dr0:dr0 + NA_ROWS].reshape(n_l, n_h // 2, 2, NA_ROWS, GRID_W, GRID_W)
        t = jnp.transpose(t, (0, 1, 2, 4, 3, 5))
        tabs.append(t.reshape(n_l, n_h // 2, 2 * GRID_W, NA_KEYS))
    return jnp.stack(tabs, axis=1)


def _na_seq_kernel(q_ref, k_ref, v_ref, qw_ref, kw_ref, bias_ref, o_ref, kt_ref, vb_ref, s_ref, *, t_ctx, rows):
    s_len = k_ref.shape[1]
    lo = lax.broadcasted_iota(jnp.int32, (1, LANES), 1) < HEAD_DIM
    rows_per_step = NA_STEP // GRID_W

    def knorm_t(t0, n):
        return _head_rms(k_ref[0, pl.ds(t0, n), :], kw_ref[...], lo).T.astype(BF16)

    def prep(c, carry):
        t0 = pl.multiple_of(c * NA_STEP, NA_STEP)
        kt_ref[0, :, pl.ds(t0, NA_STEP)] = knorm_t(t0, NA_STEP)
        vb_ref[pl.ds(t0, NA_STEP), :] = jnp.concatenate(
            [v_ref[0, pl.ds(t0, NA_STEP), :].astype(BF16), jnp.ones((NA_STEP, LANES), BF16)], axis=1)
        return carry
    lax.fori_loop(0, s_len // NA_STEP, prep, 0, unroll=3)

    def prep_shifted(c, carry):
        t0 = pl.multiple_of(c * LANES, LANES)
        kt_ref[1, :, pl.ds(t0, LANES)] = knorm_t(pl.multiple_of(t0 + GRID_W, GRID_W), LANES)
        return carry
    lax.fori_loop(0, (s_len - GRID_W) // LANES, prep_shifted, 0, unroll=5)

    kc_t = kt_ref[0, :, 0:t_ctx]
    vc = vb_ref[0:t_ctx, :]

    def queries(tok0):
        qn = _head_rms(q_ref[0, pl.ds(tok0, NA_STEP), :], qw_ref[...], lo) * (HEAD_DIM ** -0.5)
        q0 = jnp.where(lo, qn, 0.0).astype(BF16)
        q1 = jnp.where(lo, 0.0, qn).astype(BF16)
        return [jnp.concatenate([q0[g * GRID_W:(g + 1) * GRID_W], q1[g * GRID_W:(g + 1) * GRID_W]], axis=0)
                for g in range(rows_per_step)]

    def finish(tok0, g, o2):
        o2 = o2[:, :LANES] * (1.0 / o2[:, LANES:])
        o = jnp.where(lo, o2[:GRID_W], o2[GRID_W:])
        o_ref[0, pl.ds(tok0 + g * GRID_W, GRID_W), :] = o.astype(o_ref.dtype)

    for g, q2 in enumerate(queries(0)):
        s_c = jnp.dot(q2, kc_t, preferred_element_type=F32)
        p_c = jnp.exp(s_c - jnp.max(s_c, axis=-1, keepdims=True))
        finish(0, g, jnp.dot(p_c.astype(BF16), vc, preferred_element_type=F32))

    def row_group(rg, carry):
        tok0 = pl.multiple_of(t_ctx + rg * NA_STEP, NA_STEP)
        t0s = []
        for g, q2 in enumerate(queries(tok0)):
            r = rg * rows_per_step + g
            row_start = jnp.clip(r - NA_ROWS // 2, 0, rows - NA_ROWS)
            dr0 = row_start - r + NA_ROWS - 1
            t0 = pl.multiple_of(t_ctx + row_start * GRID_W, GRID_W)
            odd = (t0 // GRID_W) % (LANES // GRID_W)
            kw_t = kt_ref[odd, :, pl.ds(pl.multiple_of(t0 - odd * GRID_W, LANES), NA_KEYS)]
            s_ref[g, :, :NA_KEYS] = jnp.dot(q2, kw_t, preferred_element_type=F32) + bias_ref[dr0, 0]
            s_ref[g, :, NA_KEYS:] = jnp.dot(q2, kc_t, preferred_element_type=F32)
            t0s.append(t0)
        for g in range(rows_per_step):
            s = s_ref[g]
            p = jnp.exp(s - jnp.max(s, axis=-1, keepdims=True)).astype(BF16)
            finish(tok0, g, jnp.dot(p[:, :NA_KEYS], vb_ref[pl.ds(t0s[g], NA_KEYS), :], preferred_element_type=F32)
                   + jnp.dot(p[:, NA_KEYS:], vc, preferred_element_type=F32))
        return carry
    lax.fori_loop(0, rows // rows_per_step, row_group, 0)


def _na_attention(p_main, q_w, k_w, bias_tab, t_ctx, col_q, n_heads):
    b_sz, s_len, _ = p_main.shape
    n_hg = n_heads // 2
    rows = (s_len - t_ctx) // GRID_W
    rows_per_step = NA_STEP // GRID_W
    assert t_ctx == NA_STEP and s_len % NA_STEP == 0 and rows >= NA_ROWS and rows % rows_per_step == 0
    cq = col_q // LANES
    seq = lambda col: pl.BlockSpec((1, s_len, LANES), lambda b, h: (b, 0, col + h))
    w2 = lambda w: jnp.concatenate([w, w]).reshape(1, LANES)
    return pl.pallas_call(
        functools.partial(_na_seq_kernel, t_ctx=t_ctx, rows=rows),
        grid=(b_sz, n_hg),
        in_specs=[seq(cq), seq(cq + n_hg), seq(cq + 2 * n_hg),
                  pl.BlockSpec((1, LANES), lambda b, h: (0, 0)),
                  pl.BlockSpec((1, LANES), lambda b, h: (0, 0)),
                  pl.BlockSpec((NA_ROWS, 1, 2 * GRID_W, NA_KEYS), lambda b, h: (0, h, 0, 0))],
        out_specs=pl.BlockSpec((1, s_len, LANES), lambda b, h: (b, 0, h)),
        out_shape=jax.ShapeDtypeStruct((b_sz, s_len, n_heads * HEAD_DIM), BF16),
        scratch_shapes=[pltpu.VMEM((2, LANES, s_len), BF16), pltpu.VMEM((s_len, 2 * LANES), BF16),
                        pltpu.VMEM((rows_per_step, 2 * GRID_W, NA_KEYS + t_ctx), F32)],
        compiler_params=_cparams(("arbitrary", "arbitrary")),
        name="na_attention",
    )(p_main, p_main, p_main, w2(q_w), w2(k_w), bias_tab)


RET_CHUNK_LEN = 256
RET_UNROLL = 3
MLSTM_CHUNK_LEN = 128


def _reverse_chunk(i, nc, n_ctx):
    return jnp.where(i < n_ctx, n_ctx - 1 - i, nc - 1 - i + n_ctx)


def _dot_tn(a, b):
    return lax.dot_general(a, b, (((0,), (0,)), ((), ())), preferred_element_type=F32)


def _rope_tables(t_ctx, t_len):
    pos = jnp.arange(t_len)
    row = (pos // GRID_W).astype(F32)
    col = (pos % GRID_W).astype(F32)
    n = HEAD_DIM // 4
    inv = ROPE_BASE ** (-jnp.arange(n, dtype=F32) / n)
    ar = row[:, None] * inv
    ac = col[:, None] * inv
    cos = jnp.concatenate([jnp.cos(ar), jnp.cos(ar), jnp.cos(ac), jnp.cos(ac)], axis=-1)
    sin = jnp.concatenate([-jnp.sin(ar), jnp.sin(ar), -jnp.sin(ac), jnp.sin(ac)], axis=-1)
    cos = jnp.concatenate([jnp.ones((t_ctx, HEAD_DIM), F32), cos], axis=0)
    sin = jnp.concatenate([jnp.zeros((t_ctx, HEAD_DIM), F32), sin], axis=0)
    return jnp.tile(cos, (1, 2)), jnp.tile(sin, (1, 2))


def _ret_tables(n_heads):
    L = RET_CHUNK_LEN
    pos = np.arange(L, dtype=np.float32)
    lane_head = np.arange(LANES) // HEAD_DIM
    decay = np.zeros((2, n_heads, L, L), np.float32)
    zeta = np.zeros((2, n_heads // 2, LANES, L), np.float32)
    xi = np.zeros((2, n_heads // 2, L, LANES), np.float32)
    gch = np.zeros((2, n_heads // 2, 1, LANES), np.float32)
    for d, first_exp in enumerate((5.0, 6.0)):
        e = np.float32(first_exp) + np.float32(2.0) * np.arange(n_heads, dtype=np.float32)
        lg = np.log1p(-np.exp2(-e)).astype(np.float32)
        diff = pos[:, None] - pos[None, :]
        if d == 1:
            diff = -diff
        for h in range(n_heads):
            decay[d, h] = np.where(diff >= 0, np.exp(lg[h] * np.maximum(diff, 0.0)), 0.0)
        for hp in range(n_heads // 2):
            lgl = lg[2 * hp + lane_head][None, :]
            to_end = (L - 1 - pos if d == 0 else pos)[:, None]
            zeta[d, hp] = np.exp(lgl * to_end).T
            xi[d, hp] = np.exp(lgl * (L - to_end))
            gch[d, hp] = np.exp(lgl * L)
    return tuple(jnp.asarray(a) for a in (decay, zeta, xi, gch))


def _ret_kernel(q_ref, k_ref, v_ref, g_ref, cos_ref, sin_ref, dec_ref, zeta_ref, xi_ref, gch_ref, rn_ref,
                o_ref, qr_ref, kt_ref, vb_ref, acc_ref, *, t_ctx):
    L = RET_CHUNK_LEN
    s_len = q_ref.shape[1]
    nc = s_len // L
    lane = lax.broadcasted_iota(jnp.int32, (1, LANES), 1)
    lo = lane < HEAD_DIM
    half = (lane & (HEAD_DIM // 4)) == 0
    rid = lax.broadcasted_iota(jnp.int32, (LANES, LANES), 0) < HEAD_DIM
    cid = lax.broadcasted_iota(jnp.int32, (LANES, LANES), 1) < HEAD_DIM
    same_head = rid == cid

    def rope(x, cos, sin):
        up = pltpu.roll(x, LANES - HEAD_DIM // 4, axis=1)
        dn = pltpu.roll(x, HEAD_DIM // 4, axis=1)
        return x * cos + jnp.where(half, up, dn) * sin

    def prep(c, carry):
        sl = pl.ds(pl.multiple_of(c * L, L), L)
        cos, sin = cos_ref[sl, :], sin_ref[sl, :]
        qr_ref[sl, :] = rope(q_ref[0, sl, :], cos, sin).astype(BF16)
        kt_ref[:, sl] = (rope(k_ref[0, sl, :], cos, sin) * HEAD_DIM ** -0.5).T.astype(BF16)
        vb_ref[sl, :] = v_ref[0, sl, :].astype(BF16)
        return carry
    lax.fori_loop(0, nc, prep, 0, unroll=RET_UNROLL)

    def chunk(d, c, state):
        sl = pl.ds(pl.multiple_of(c * L, L), L)
        q, kt, v = qr_ref[sl, :], kt_ref[:, sl], vb_ref[sl, :]
        inter = jnp.dot(q, state.astype(BF16), preferred_element_type=F32) * xi_ref[d, 0]
        outs = []
        for h2 in range(2):
            qm = jnp.where(lo if h2 == 0 else jnp.logical_not(lo), q, jnp.zeros_like(q))
            sd = (jnp.dot(qm, kt, preferred_element_type=F32) * dec_ref[d, h2]).astype(BF16)
            outs.append(jnp.dot(sd, v, preferred_element_type=F32))
        y = jnp.where(lo, outs[0], outs[1]) + inter
        kz_t = (kt.astype(F32) * zeta_ref[d, 0]).astype(BF16)
        state = state * gch_ref[d, 0] + jnp.where(same_head, jnp.dot(kz_t, v, preferred_element_type=F32), 0.0)
        return sl, y, state

    def fwd(c, state):
        sl, y, state = chunk(0, c, state)
        acc_ref[sl, :] = y
        return state
    lax.fori_loop(0, nc, fwd, jnp.zeros((LANES, LANES), F32), unroll=RET_UNROLL)

    def bwd(i, state):
        c = _reverse_chunk(i, nc, t_ctx // L)
        sl, y, state = chunk(1, c, state)
        y = _head_rms(y + acc_ref[sl, :], rn_ref[...], lo)
        g = g_ref[0, sl, :]
        o_ref[0, sl, :] = (y * (g * jax.nn.sigmoid(g))).astype(o_ref.dtype)
        return state
    lax.fori_loop(0, nc, bwd, jnp.zeros((LANES, LANES), F32), unroll=RET_UNROLL)


def _seq_spec(s_len, col):
    return pl.BlockSpec((1, s_len, LANES), lambda b, h: (b, 0, col + h), pipeline_mode=pl.Buffered(1))


def _retention(p_main, rope_tabs, ret_tabs, r_w, t_ctx, col_q, n_heads):
    b_sz, s_len, _ = p_main.shape
    n_hp = n_heads // 2
    L = RET_CHUNK_LEN
    assert t_ctx % L == 0 and s_len % L == 0
    c0 = col_q // LANES
    cos, sin = rope_tabs
    decay, zeta, xi, gch = ret_tabs
    const2 = pl.BlockSpec((s_len, LANES), lambda b, h: (0, 0), pipeline_mode=pl.Buffered(1))
    return pl.pallas_call(
        functools.partial(_ret_kernel, t_ctx=t_ctx),
        grid=(b_sz, n_hp),
        in_specs=[_seq_spec(s_len, c0), _seq_spec(s_len, c0 + n_hp), _seq_spec(s_len, c0 + 2 * n_hp),
                  _seq_spec(s_len, c0 + 3 * n_hp), const2, const2,
                  pl.BlockSpec((2, 2, L, L), lambda b, h: (0, h, 0, 0)),
                  pl.BlockSpec((2, 1, LANES, L), lambda b, h: (0, h, 0, 0)),
                  pl.BlockSpec((2, 1, L, LANES), lambda b, h: (0, h, 0, 0)),
                  pl.BlockSpec((2, 1, 1, LANES), lambda b, h: (0, h, 0, 0)),
                  pl.BlockSpec((1, LANES), lambda b, h: (0, 0))],
        out_specs=pl.BlockSpec((1, s_len, LANES), lambda b, h: (b, 0, h)),
        out_shape=jax.ShapeDtypeStruct((b_sz, s_len, n_heads * HEAD_DIM), BF16),
        scratch_shapes=[pltpu.VMEM((s_len, LANES), BF16), pltpu.VMEM((LANES, s_len), BF16),
                        pltpu.VMEM((s_len, LANES), BF16), pltpu.VMEM((s_len, LANES), F32)],
        compiler_params=_cparams(("arbitrary", "arbitrary")),
        name="retention",
    )(p_main, p_main, p_main, p_main, cos, sin, decay, zeta, xi, gch,
      jnp.concatenate([r_w, r_w]).reshape(1, LANES))


N_GATE_TYPES = 4


def _log_sigmoid(x):
    return jnp.minimum(x, 0.0) - jnp.log1p(jnp.exp(-jnp.abs(x)))


def _mlstm_kernel(q_ref, k_ref, v_ref, og_ref, gc_ref, gr_ref, wq_ref, wk_ref, bq_ref, bk_ref, gbc_ref, gbr_ref,
                  mn_ref, o_ref, qc_ref, kc_ref, vb_ref, acc_ref, *, t_ctx):
    L = MLSTM_CHUNK_LEN
    s_len = q_ref.shape[1]
    nc = s_len // L
    lane = lax.broadcasted_iota(jnp.int32, (1, LANES), 1)
    lo = lane < HEAD_DIM
    head_lanes = (lo, jnp.logical_not(lo))
    rid = lax.broadcasted_iota(jnp.int32, (LANES, LANES), 0) < HEAD_DIM
    cid = lax.broadcasted_iota(jnp.int32, (LANES, LANES), 1) < HEAD_DIM
    head_block = (rid & cid, jnp.logical_not(rid | cid))
    row_i = lax.broadcasted_iota(jnp.int32, (L, L), 0)
    col_i = lax.broadcasted_iota(jnp.int32, (L, L), 1)
    causal = (row_i >= col_i, row_i <= col_i)
    tri = tuple(c.astype(F32) for c in causal)
    tri_t = (tri[1], tri[0])
    sub = lax.broadcasted_iota(jnp.int32, (L, 1), 0)

    def conv(x_ref, w_ref, b_ref, t0):
        x = x_ref[0, pl.ds(t0, L), :]
        prev = x_ref[0, pl.ds(jnp.maximum(t0 - 8, 0), 8), :][7:8]
        nxt = x_ref[0, pl.ds(jnp.minimum(t0 + L, s_len - 8), 8), :][0:1]
        prev = jnp.where((t0 != 0) & (t0 != t_ctx), prev, 0.0)
        nxt = jnp.where((t0 + L != t_ctx) & (t0 + L != s_len), nxt, 0.0)
        xm = jnp.where(sub == 0, prev, pltpu.roll(x, 1, axis=0))
        xp = jnp.where(sub == L - 1, nxt, pltpu.roll(x, L - 1, axis=0))
        y = b_ref[...] + xm * w_ref[0:1, :] + x * w_ref[1:2, :] + xp * w_ref[2:3, :]
        return y * jax.nn.sigmoid(y)

    def prep(c, carry):
        t0 = pl.multiple_of(c * L, L)
        sl = pl.ds(t0, L)
        qc_ref[sl, :] = conv(q_ref, wq_ref, bq_ref, t0).astype(BF16)
        kc_ref[sl, :] = (conv(k_ref, wk_ref, bk_ref, t0) * HEAD_DIM ** -0.5).astype(BF16)
        vb_ref[sl, :] = v_ref[0, sl, :].astype(BF16)
        return carry
    lax.fori_loop(0, nc, prep, 0)

    def chunk(d, c, state):
        sl = pl.ds(pl.multiple_of(c * L, L), L)
        q, k, v = qc_ref[sl, :], kc_ref[sl, :], vb_ref[sl, :]
        g_col = gc_ref[0, sl, :] + gbc_ref[0]
        g_row = gr_ref[0, 0, :, sl] + gbr_ref[0]
        cum_col = jnp.dot(tri[d], _log_sigmoid(g_col), precision=lax.Precision.HIGHEST,
                          preferred_element_type=F32)
        cum_row = jnp.dot(_log_sigmoid(g_row), tri_t[d], precision=lax.Precision.HIGHEST,
                          preferred_element_type=F32)
        end = L - 1 if d == 0 else 0
        outs, new_state = [], []
        for h2 in range(2):
            c_st, n_st, m_st = state[h2]
            ci, cf = 2 * (2 * d) + h2, 2 * (2 * d + 1) + h2
            i_col, a_col = g_col[:, ci:ci + 1], cum_col[:, cf:cf + 1]
            i_row, a_row = g_row[ci:ci + 1, :], cum_row[cf:cf + 1, :]
            b_tot = a_row[:, end:end + 1]
            d_log = jnp.where(causal[d], a_col + (i_row - a_row), NEG_INF)
            m_intra = jnp.max(d_log, axis=-1, keepdims=True)
            qm = jnp.where(head_lanes[h2], q, jnp.zeros_like(q))
            s = _dot_nt(qm, k) * jnp.exp(d_log - m_intra)
            num_intra = jnp.dot(s.astype(BF16), v, preferred_element_type=F32)
            den_intra = jnp.sum(s, axis=-1, keepdims=True)
            inter_log = a_col + m_st
            m_q = jnp.maximum(m_intra, inter_log)
            a = jnp.exp(inter_log - m_q)
            g = jnp.exp(m_intra - m_q)
            num = a * jnp.dot(qm, c_st.astype(BF16), preferred_element_type=F32) + g * num_intra
            den = a * jnp.sum(qm.astype(F32) * n_st, axis=-1, keepdims=True) + g * den_intra
            outs.append(num / jnp.maximum(jnp.abs(den), jnp.exp(-m_q)))
            w_log = b_tot - a_col + i_col
            m_loc = jnp.max(w_log, axis=0, keepdims=True)
            ke = jnp.where(head_lanes[h2], k.astype(F32) * jnp.exp(w_log - m_loc), 0.0)
            c_loc = jnp.where(head_block[h2], _dot_tn(ke.astype(BF16), v), 0.0)
            n_loc = jnp.sum(ke, axis=0, keepdims=True)
            m_new = jnp.maximum(b_tot + m_st, m_loc)
            a_s = jnp.exp(b_tot + m_st - m_new)
            g_s = jnp.exp(m_loc - m_new)
            new_state.append((a_s * c_st + g_s * c_loc, a_s * n_st + g_s * n_loc, m_new))
        return sl, jnp.where(lo, outs[0], outs[1]), tuple(new_state)

    zero = tuple((jnp.zeros((LANES, LANES), F32), jnp.zeros((1, LANES), F32), jnp.zeros((1, 1), F32))
                 for _ in range(2))

    def fwd(c, state):
        sl, y, state = chunk(0, c, state)
        acc_ref[sl, :] = y
        return state
    lax.fori_loop(0, nc, fwd, zero)

    def bwd(i, state):
        c = _reverse_chunk(i, nc, t_ctx // L)
        sl, y, state = chunk(1, c, state)
        y = _head_rms(y + acc_ref[sl, :], mn_ref[...], lo)
        o_ref[0, sl, :] = (y * jax.nn.sigmoid(og_ref[0, sl, :])).astype(o_ref.dtype)
        return state
    lax.fori_loop(0, nc, bwd, zero)


def _mlstm_pair_kernel(q_ref, k_ref, v_ref, og_ref, gc_ref, gr_ref, wq_ref, wk_ref, bq_ref, bk_ref, gbc_ref, gbr_ref,
                       mn_ref, o_ref, qc_ref, kt_ref, vb_ref, acc_ref, st_ref, cc_ref, cr_ref, *, t_ctx):
    L = MLSTM_CHUNK_LEN
    s_len = q_ref.shape[1]
    nc = s_len // L
    lane = lax.broadcasted_iota(jnp.int32, (1, LANES), 1)
    lo = lane < HEAD_DIM
    head_lanes = (lo, jnp.logical_not(lo))
    sub_lo = lax.broadcasted_iota(jnp.int32, (LANES, 1), 0) < HEAD_DIM
    head_rows = (sub_lo, jnp.logical_not(sub_lo))
    row_i = lax.broadcasted_iota(jnp.int32, (L, L), 0)
    col_i = lax.broadcasted_iota(jnp.int32, (L, L), 1)
    causal = (row_i >= col_i, row_i <= col_i)
    tri = tuple(c.astype(F32) for c in causal)
    tri_t = (tri[1], tri[0])
    sub = lax.broadcasted_iota(jnp.int32, (L, 1), 0)
    gate_row = lax.broadcasted_iota(jnp.int32, (2 * N_GATE_TYPES, 1), 0)
    ones = jnp.ones((L, LANES), BF16)

    def conv(x_ref, w_ref, b_ref, t0):
        x = x_ref[0, pl.ds(t0, L), :]
        prev = x_ref[0, pl.ds(jnp.maximum(t0 - 8, 0), 8), :][7:8]
        nxt = x_ref[0, pl.ds(jnp.minimum(t0 + L, s_len - 8), 8), :][0:1]
        prev = jnp.where((t0 != 0) & (t0 != t_ctx), prev, 0.0)
        nxt = jnp.where((t0 + L != t_ctx) & (t0 + L != s_len), nxt, 0.0)
        xm = jnp.where(sub == 0, prev, pltpu.roll(x, 1, axis=0))
        xp = jnp.where(sub == L - 1, nxt, pltpu.roll(x, L - 1, axis=0))
        y = b_ref[...] + xm * w_ref[0:1, :] + x * w_ref[1:2, :] + xp * w_ref[2:3, :]
        return y * jax.nn.sigmoid(y)

    def prep(c, carry):
        t0 = pl.multiple_of(c * L, L)
        sl = pl.ds(t0, L)
        qc_ref[sl, :] = conv(q_ref, wq_ref, bq_ref, t0).astype(BF16)
        kt_ref[:, sl] = (conv(k_ref, wk_ref, bk_ref, t0) * HEAD_DIM ** -0.5).T.astype(BF16)
        vb_ref[sl, :] = v_ref[0, sl, :].astype(BF16)
        lf_col = _log_sigmoid(gc_ref[0, sl, :] + gbc_ref[0])
        lf_row = _log_sigmoid(gr_ref[0, 0, :, sl] + gbr_ref[0])
        hdot = functools.partial(jnp.dot, precision=lax.Precision.HIGHEST, preferred_element_type=F32)
        cc_ref[sl, :] = jnp.where(lane < N_GATE_TYPES, hdot(tri[0], lf_col), hdot(tri[1], lf_col))
        cr_ref[:, sl] = jnp.where(gate_row < N_GATE_TYPES, hdot(lf_row, tri_t[0]), hdot(lf_row, tri_t[1]))
        return carry
    lax.fori_loop(0, nc, prep, 0, unroll=2)

    def chunk(d, c, m_state):
        sl = pl.ds(pl.multiple_of(c * L, L), L)
        q, kt, v = qc_ref[sl, :], kt_ref[:, sl], vb_ref[sl, :]
        g_row = gr_ref[0, 0, :, sl] + gbr_ref[0]
        cum_col, cum_row = cc_ref[sl, :], cr_ref[:, sl]
        end = L - 1 if d == 0 else 0
        outs, new_m = [], []
        for h2 in range(2):
            m_st = m_state[h2]
            ci, cf = 2 * (2 * d) + h2, 2 * (2 * d + 1) + h2
            a_rep = jnp.broadcast_to(cum_col[:, cf:cf + 1], (L, LANES))
            i_row, a_row = g_row[ci:ci + 1, :], cum_row[cf:cf + 1, :]
            b_tot = a_row[:, end:end + 1]
            d_log = jnp.where(causal[d], a_rep + (i_row - a_row), NEG_INF)
            m_intra = jnp.broadcast_to(jnp.max(d_log, axis=-1, keepdims=True), (L, LANES))
            qm = jnp.where(head_lanes[h2], q, jnp.zeros_like(q))
            s = jnp.dot(qm, kt, preferred_element_type=F32) * jnp.exp(d_log - m_intra)
            v1 = jnp.where(head_lanes[h2], v, ones)
            intra = jnp.dot(s.astype(BF16), v1, preferred_element_type=F32)
            inter = jnp.dot(qm, st_ref[h2].astype(BF16), preferred_element_type=F32)
            inter_log = a_rep + m_st
            m_q = jnp.maximum(m_intra, inter_log)
            num_den = jnp.exp(inter_log - m_q) * inter + jnp.exp(m_intra - m_q) * intra
            den = pltpu.roll(num_den, HEAD_DIM, axis=1)
            outs.append(num_den / jnp.maximum(jnp.abs(den), jnp.exp(-m_q)))
            w_row = b_tot - a_row + i_row
            m_loc = jnp.max(w_row, axis=-1, keepdims=True)
            kts = jnp.where(head_rows[h2], kt.astype(F32) * jnp.exp(w_row - m_loc), 0.0).astype(BF16)
            loc = jnp.dot(kts, v1, preferred_element_type=F32)
            m_new = jnp.maximum(b_tot + m_st, m_loc)
            st_ref[h2] = jnp.exp(b_tot + m_st - m_new) * st_ref[h2] + jnp.exp(m_loc - m_new) * loc
            new_m.append(m_new)
        return sl, jnp.where(lo, outs[0], outs[1]), tuple(new_m)

    zero_m = (jnp.zeros((1, 1), F32), jnp.zeros((1, 1), F32))

    st_ref[...] = jnp.zeros_like(st_ref)

    def fwd(c, m_state):
        sl, y, m_state = chunk(0, c, m_state)
        acc_ref[sl, :] = y
        return m_state
    lax.fori_loop(0, nc, fwd, zero_m, unroll=2)

    st_ref[...] = jnp.zeros_like(st_ref)

    def bwd(i, m_state):
        c = _reverse_chunk(i, nc, t_ctx // L)
        sl, y, m_state = chunk(1, c, m_state)
        y = _head_rms(y + acc_ref[sl, :], mn_ref[...], lo)
        o_ref[0, sl, :] = (y * jax.nn.sigmoid(og_ref[0, sl, :])).astype(o_ref.dtype)
        return m_state
    lax.fori_loop(0, nc, bwd, zero_m, unroll=2)


def _mlstm(p_main, p_gate, conv_w, conv_b, gate_b, m_w, t_ctx, n_heads):
    b_sz, s_len, _ = p_main.shape
    n_hp = n_heads // 2
    L = MLSTM_CHUNK_LEN
    assert L == LANES and t_ctx % L == 0 and s_len % L == 0
    n_g = 2 * N_GATE_TYPES
    g_rows = jnp.transpose(p_gate.reshape(b_sz, s_len, n_hp, LANES)[..., :n_g], (0, 2, 3, 1))
    gb = jnp.transpose(gate_b.reshape(N_GATE_TYPES, n_hp, 2), (1, 0, 2)).reshape(n_hp, n_g)
    gb_col = jnp.zeros((n_hp, 1, LANES), F32).at[:, 0, :n_g].set(gb)
    gb_row = gb.reshape(n_hp, n_g, 1)
    vec = lambda col: pl.BlockSpec((1, LANES), lambda b, h: (0, col + h))
    return pl.pallas_call(
        functools.partial(_mlstm_pair_kernel, t_ctx=t_ctx),
        grid=(b_sz, n_hp),
        in_specs=[_seq_spec(s_len, 0), _seq_spec(s_len, n_hp), _seq_spec(s_len, 2 * n_hp),
                  _seq_spec(s_len, 3 * n_hp),
                  pl.BlockSpec((1, s_len, LANES), lambda b, h: (b, 0, h), pipeline_mode=pl.Buffered(1)),
                  pl.BlockSpec((1, 1, n_g, s_len), lambda b, h: (b, h, 0, 0)),
                  pl.BlockSpec((3, LANES), lambda b, h: (0, h)),
                  pl.BlockSpec((3, LANES), lambda b, h: (0, n_hp + h)),
                  vec(0), vec(n_hp),
                  pl.BlockSpec((1, 1, LANES), lambda b, h: (h, 0, 0)),
                  pl.BlockSpec((1, n_g, 1), lambda b, h: (h, 0, 0)),
                  pl.BlockSpec((1, LANES), lambda b, h: (0, 0))],
        out_specs=pl.BlockSpec((1, s_len, LANES), lambda b, h: (b, 0, h)),
        out_shape=jax.ShapeDtypeStruct((b_sz, s_len, n_heads * HEAD_DIM), BF16),
        scratch_shapes=[pltpu.VMEM((s_len, LANES), BF16), pltpu.VMEM((LANES, s_len), BF16),
                        pltpu.VMEM((s_len, LANES), BF16), pltpu.VMEM((s_len, LANES), F32),
                        pltpu.VMEM((2, LANES, LANES), F32),
                        pltpu.VMEM((s_len, LANES), F32), pltpu.VMEM((n_g, s_len), F32)],
        compiler_params=_cparams(("arbitrary", "arbitrary")),
        name="mlstm",
    )(p_main, p_main, p_main, p_main, p_gate, g_rows, conv_w, conv_w, conv_b.reshape(1, -1),
      conv_b.reshape(1, -1), gb_col, gb_row, jnp.concatenate([m_w, m_w]).reshape(1, LANES))


def _rms_norm(x, g):
    xf = x.astype(F32)
    y = xf * lax.rsqrt(jnp.mean(xf * xf, axis=-1, keepdims=True) + NORM_EPS)
    return (y * g.astype(F32)).astype(x.dtype)


def _split_heads(t, n_heads):
    return t.reshape(t.shape[0], t.shape[1], n_heads, HEAD_DIM)


def _to_bhtd(t):
    return jnp.transpose(t, (0, 2, 1, 3))


def _merge_heads(t):
    b, h, t_len, d = t.shape
    return jnp.transpose(t, (0, 2, 1, 3)).reshape(b, t_len, h * d)


def _flip_t(t):
    return jnp.flip(t, axis=2)


def _dwconv_centred(x, w, b):
    k_size = w.shape[0]
    pad = k_size // 2
    t_len = x.shape[1]
    xp = jnp.pad(x, ((0, 0), (pad, pad), (0, 0)))
    y = b
    for i in range(k_size):
        y = y + xp[:, i:i + t_len] * w[i]
    return y


def _axial_rope_tables(t_len):
    pos = jnp.arange(t_len)
    row = (pos // GRID_W).astype(F32)
    col = (pos % GRID_W).astype(F32)
    n = HEAD_DIM // 4
    inv = ROPE_BASE ** (-jnp.arange(n, dtype=F32) / n)
    ar = row[:, None] * inv
    ac = col[:, None] * inv
    return (jnp.cos(ar), jnp.sin(ar), jnp.cos(ac), jnp.sin(ac))


def _rotate_half(x, cos, sin):
    x1, x2 = jnp.split(x, 2, axis=-1)
    return jnp.concatenate([x1 * cos - x2 * sin, x2 * cos + x1 * sin], axis=-1)


def _axial_rope(x, cos_r, sin_r, cos_c, sin_c):
    xr, xc = jnp.split(x, 2, axis=-1)
    return jnp.concatenate([_rotate_half(xr, cos_r, sin_r), _rotate_half(xc, cos_c, sin_c)], axis=-1)


def _mlstm_chunkwise(q, k, v, log_i, log_f, state0):
    b_sz, h_sz, t_len, d = q.shape
    L = MLSTM_CHUNK
    nc = t_len // L
    qc = q.reshape(b_sz, h_sz, nc, L, d)
    kc = k.reshape(b_sz, h_sz, nc, L, d)
    vc = v.reshape(b_sz, h_sz, nc, L, d)
    li = log_i.reshape(b_sz, h_sz, nc, L)
    bcum = jnp.cumsum(log_f.reshape(b_sz, h_sz, nc, L), axis=-1)
    b_tot = bcum[..., -1]
    tri = jnp.tril(jnp.ones((L, L), dtype=bool))
    d_log = jnp.where(tri, bcum[..., :, None] - bcum[..., None, :] + li[..., None, :], NEG_INF)
    m_intra = jnp.max(d_log, axis=-1)
    s = jnp.einsum('bhcjd,bhcld->bhcjl', qc, kc).astype(F32) * jnp.exp(d_log - m_intra[..., None])
    num_intra = jnp.einsum('bhcjl,bhcld->bhcjd', s, vc)
    den_intra = jnp.sum(s, axis=-1)
    w_log = b_tot[..., None] - bcum + li
    m_loc = jnp.max(w_log, axis=-1)
    e = jnp.exp(w_log - m_loc[..., None])
    c_loc = jnp.einsum('bhcl,bhcld,bhcle->bhcde', e, kc, vc)
    n_loc = jnp.einsum('bhcl,bhcld->bhcd', e, kc)

    def step(carry, inp):
        c_st, n_st, m_st = carry
        cl, nl, ml, bt = inp
        m_new = jnp.maximum(bt + m_st, ml)
        a = jnp.exp(bt + m_st - m_new)
        g = jnp.exp(ml - m_new)
        new = (a[..., None, None] * c_st + g[..., None, None] * cl, a[..., None] * n_st + g[..., None] * nl, m_new)
        return new, carry

    xs = (jnp.moveaxis(c_loc, 2, 0), jnp.moveaxis(n_loc, 2, 0), jnp.moveaxis(m_loc, 2, 0), jnp.moveaxis(b_tot, 2, 0))
    final, prev = lax.scan(step, state0, xs)
    c_prev = jnp.moveaxis(prev[0], 0, 2)
    n_prev = jnp.moveaxis(prev[1], 0, 2)
    m_prev = jnp.moveaxis(prev[2], 0, 2)
    inter_log = bcum + m_prev[..., None]
    m_q = jnp.maximum(m_intra, inter_log)
    a = jnp.exp(inter_log - m_q)
    g = jnp.exp(m_intra - m_q)
    num = a[..., None] * jnp.einsum('bhcjd,bhcde->bhcje', qc, c_prev) + g[..., None] * num_intra
    den = a * jnp.einsum('bhcjd,bhcd->bhcj', qc, n_prev) + g * den_intra
    h = num / jnp.maximum(jnp.abs(den), jnp.exp(-m_q))[..., None]
    return h.reshape(b_sz, h_sz, t_len, d), final


def _retention_log_decay(first_exp, n_heads):
    e = first_exp + 2.0 * jnp.arange(n_heads, dtype=F32)
    return jnp.log1p(-jnp.exp2(-e))


def _retention_chunkwise(q, k, v, log_gamma, state0):
    b_sz, h_sz, t_len, d = q.shape
    L = RET_CHUNK
    nc = t_len // L
    qc = q.reshape(b_sz, h_sz, nc, L, d)
    kc = k.reshape(b_sz, h_sz, nc, L, d)
    vc = v.reshape(b_sz, h_sz, nc, L, d)
    pos = jnp.arange(L, dtype=F32)
    diff = pos[:, None] - pos[None, :]
    decay = jnp.where(diff >= 0, jnp.exp(log_gamma[:, None, None] * jnp.maximum(diff, 0.0)), 0.0)
    s = jnp.einsum('bhcjd,bhcld->bhcjl', qc, kc).astype(F32) * decay[:, None]
    intra = jnp.einsum('bhcjl,bhcld->bhcjd', s, vc)
    zeta = jnp.exp(log_gamma[:, None] * (L - 1 - pos))
    s_loc = jnp.einsum('hl,bhcld,bhcle->bhcde', zeta, kc, vc)
    g_chunk = jnp.exp(log_gamma * L)[:, None, None]

    def step(r, sl):
        return g_chunk * r + sl, r

    final, r_prev = lax.scan(step, state0, jnp.moveaxis(s_loc, 2, 0))
    r_prev = jnp.moveaxis(r_prev, 0, 2)
    xi = jnp.exp(log_gamma[:, None] * (pos + 1.0))
    inter = jnp.einsum('bhcjd,bhcde->bhcje', qc, r_prev) * xi[:, None, :, None]
    return (intra + inter).reshape(b_sz, h_sz, t_len, d), final


def _na_window(rows):
    wr = min(NA_ROWS, rows)
    r = jnp.arange(rows)
    col = jnp.arange(GRID_W)
    row_idx = jnp.clip(r - wr // 2, 0, rows - wr)[:, None] + jnp.arange(wr)[None, :]
    col_start = jnp.clip(col - NA_COLS // 2, 0, GRID_W - NA_COLS)
    col_in = (col[None, :] >= col_start[:, None]) & (col[None, :] < col_start[:, None] + NA_COLS)
    dr = row_idx - r[:, None] + NA_ROWS - 1
    dc = jnp.clip(col[None, :] - col[:, None] + NA_COLS - 1, 0, 2 * NA_COLS - 2)
    return (row_idx, col_in, dr[:, None, :, None], dc[None, :, None, :])


def _na_latent(q, k, v, k_ctx, v_ctx, row_idx, col_in, bias):
    t_len, h_sz, d = q.shape
    rows, wr = row_idx.shape
    scale = d ** -0.5
    qg = q.reshape(rows, GRID_W, h_sz, d)
    kb = k.reshape(rows, GRID_W, h_sz, d)[row_idx]
    vb = v.reshape(rows, GRID_W, h_sz, d)[row_idx]
    s_loc = jnp.einsum('rqhd,rjwhd->hrqjw', qg, kb).astype(F32) * scale + bias
    s_loc = jnp.where(col_in[:, None, :], s_loc, NEG_INF)
    s_ctx = jnp.einsum('rqhd,chd->hrqc', qg, k_ctx).astype(F32) * scale
    n_loc = wr * GRID_W
    s = jnp.concatenate([s_loc.reshape(h_sz, rows, GRID_W, n_loc), s_ctx], axis=-1)
    p = jax.nn.softmax(s, axis=-1).astype(v.dtype)
    p_loc = p[..., :n_loc].reshape(h_sz, rows, GRID_W, wr, GRID_W)
    o = jnp.einsum('hrqjw,rjwhd->rqhd', p_loc, vb) + jnp.einsum('hrqc,chd->rqhd', p[..., n_loc:], v_ctx)
    return o.reshape(t_len, h_sz * d)


def _ctx_attention(q, k, v):
    s = jnp.einsum('bqhd,bkhd->bhqk', q, k).astype(F32) * HEAD_DIM ** -0.5
    p = jax.nn.softmax(s, axis=-1).astype(v.dtype)
    o = jnp.einsum('bhqk,bkhd->bqhd', p, v)
    return o.reshape(o.shape[0], o.shape[1], -1)


def _hybrid_mixer(xp, cp, rope, na_win, conv_w, conv_b, gate_b, m_norm, q_norm, k_norm, rpb, r_norm, dims):
    h_m, h_na, h_r = dims
    b_sz = xp[0].shape[0]

    def mlstm_prep(p):
        qk = jax.nn.silu(_dwconv_centred(p[0], conv_w, conv_b))
        q, k = jnp.split(qk, 2, axis=-1)
        g = (p[3] + gate_b).astype(F32)
        g = jnp.transpose(g.reshape(b_sz, -1, 4, h_m), (2, 0, 3, 1))
        return (_to_bhtd(_split_heads(q, h_m)), _to_bhtd(_split_heads(k, h_m)) * HEAD_DIM ** -0.5,
                _to_bhtd(_split_heads(p[1], h_m)), g[0], jax.nn.log_sigmoid(g[1]), g[2], jax.nn.log_sigmoid(g[3]))

    qx, kx, vx, ix_f, lfx_f, ix_b, lfx_b = mlstm_prep(xp)
    qc, kc, vc, ic_f, lfc_f, ic_b, lfc_b = mlstm_prep(cp)
    zero_m = (jnp.zeros((b_sz, h_m, HEAD_DIM, HEAD_DIM), F32),
              jnp.zeros((b_sz, h_m, HEAD_DIM), F32), jnp.zeros((b_sz, h_m), F32))
    hc_f, st_f = _mlstm_chunkwise(qc, kc, vc, ic_f, lfc_f, zero_m)
    hc_b, st_b = _mlstm_chunkwise(_flip_t(qc), _flip_t(kc), _flip_t(vc), _flip_t(ic_b), _flip_t(lfc_b), zero_m)
    hx_f, _ = _mlstm_chunkwise(qx, kx, vx, ix_f, lfx_f, st_f)
    hx_b, _ = _mlstm_chunkwise(_flip_t(qx), _flip_t(kx), _flip_t(vx), _flip_t(ix_b), _flip_t(lfx_b), st_b)

    def mlstm_out(h, o):
        return _merge_heads(_rms_norm(h, m_norm)).astype(o.dtype) * jax.nn.sigmoid(o)

    a_x = mlstm_out(hx_f + _flip_t(hx_b), xp[2])

    def na_prep(p):
        return (_rms_norm(_split_heads(p[4], h_na), q_norm), _rms_norm(_split_heads(p[5], h_na), k_norm),
                _split_heads(p[6], h_na))

    nqx, nkx, nvx = na_prep(xp)
    nqc, nkc, nvc = na_prep(cp)
    row_idx, col_in, dr, dc = na_win
    bias = rpb[:, dr, dc].astype(F32)
    b_x = lax.map(lambda a: _na_latent(a[0], a[1], a[2], a[3], a[4], row_idx, col_in, bias),
                  (nqx, nkx, nvx, nkc, nvc))

    lg_f = _retention_log_decay(5.0, h_r)
    lg_b = _retention_log_decay(6.0, h_r)
    rqx = _axial_rope(_to_bhtd(_split_heads(xp[7], h_r)), *rope)
    rkx = _axial_rope(_to_bhtd(_split_heads(xp[8], h_r)), *rope) * HEAD_DIM ** -0.5
    rvx = _to_bhtd(_split_heads(xp[9], h_r))
    rqc = _to_bhtd(_split_heads(cp[7], h_r))
    rkc = _to_bhtd(_split_heads(cp[8], h_r)) * HEAD_DIM ** -0.5
    rvc = _to_bhtd(_split_heads(cp[9], h_r))
    zero_r = jnp.zeros((b_sz, h_r, HEAD_DIM, HEAD_DIM), F32)
    rc_f, rs_f = _retention_chunkwise(rqc, rkc, rvc, lg_f, zero_r)
    rc_b, rs_b = _retention_chunkwise(_flip_t(rqc), _flip_t(rkc), _flip_t(rvc), lg_b, zero_r)
    rx_f, _ = _retention_chunkwise(rqx, rkx, rvx, lg_f, rs_f)
    rx_b, _ = _retention_chunkwise(_flip_t(rqx), _flip_t(rkx), _flip_t(rvx), lg_b, rs_b)

    def ret_out(h, g):
        return _merge_heads(_rms_norm(h, r_norm)).astype(g.dtype) * jax.nn.silu(g)

    c_x = ret_out(rx_f + _flip_t(rx_b), xp[10])
    mix_x = jnp.concatenate([a_x, b_x, c_x], axis=-1)
    a_c = mlstm_out(hc_f + _flip_t(hc_b), cp[2])
    b_c = _ctx_attention(nqc, nkc, nvc)
    c_c = ret_out(rc_f + _flip_t(rc_b), cp[10])
    return mix_x, jnp.concatenate([a_c, b_c, c_c], axis=-1)


def kernel(x, c, ctx, c_ctx, w_mod, b_mod, norm_mix, norm_ffn, w_in, w_out, mlstm_conv_w, mlstm_conv_b,
           mlstm_gate_b, mlstm_norm, na_q_norm, na_k_norm, na_rpb, ret_norm, router_w, router_b,
           expert_w_up, expert_b_up, expert_w_down, expert_b_down):
    b_sz, t_len, d = x.shape
    t_ctx = ctx.shape[1]
    depth = w_in.shape[0]
    n_e = router_w.shape[2]
    d_mix = w_out.shape[1]
    h_m = d_mix // (4 * HEAD_DIM)
    h_na = d_mix // (2 * HEAD_DIM)
    h_r = d_mix // (4 * HEAD_DIM)
    d_m, d_na, d_r = h_m * HEAD_DIM, h_na * HEAD_DIM, h_r * HEAD_DIM
    n_gate = 4 * h_m
    assert b_sz + 1 <= MOD_ROWS and n_e <= LANES and n_gate <= LANES
    s_len = t_ctx + t_len
    assert s_len % ROW_TILE == 0

    s = jnp.concatenate([ctx, x], axis=1)
    cc = jnp.zeros((MOD_ROWS, d), F32).at[:b_sz].set(c).at[b_sz].set(c_ctx)
    mods = _modulation(cc, w_mod, b_mod).reshape(depth, MOD_ROWS, 1, 6 * d)

    g0 = 4 * d_m
    col_na = g0
    col_ret = col_na + 3 * d_na
    n_hp = h_m // 2
    gate_src = np.array([[g0 + t * h_m + 2 * hp + h2 for t in range(N_GATE_TYPES) for h2 in range(2)]
                         for hp in range(n_hp)])
    rope_tabs = _rope_tables(t_ctx, t_len)
    ret_tabs = _ret_tables(h_r)
    na_tabs = _na_bias_tables(na_rpb)
    f2 = expert_w_up.shape[3]
    w_up_all = expert_w_up.reshape(depth * n_e, d, f2)
    w_down_all = expert_w_down.reshape(depth * n_e, f2 // 2, d)
    b_up_all = _regroup_bias(expert_b_up).reshape(depth * n_e, 1, f2)
    b_down_all = expert_b_down.reshape(depth * n_e, 1, d)

    for l in range(depth):
        w_main = jnp.concatenate([w_in[l, :, :g0], w_in[l, :, g0 + n_gate:]], axis=1).astype(BF16)
        w_gate = jnp.zeros((d, n_hp, LANES), BF16).at[:, :, :gate_src.shape[1]].set(
            w_in[l][:, gate_src].astype(BF16)).reshape(d, n_hp * LANES)
        p_main, p_gate = _proj_in(s, mods[l], norm_mix[l], w_main, w_gate, t_ctx)

        mix = [_mlstm(p_main, p_gate, mlstm_conv_w[l], mlstm_conv_b[l], mlstm_gate_b[l], mlstm_norm[l],
                      t_ctx, h_m),
               _na_attention(p_main, na_q_norm[l], na_k_norm[l], na_tabs[l], t_ctx, col_na, h_na),
               _retention(p_main, rope_tabs, ret_tabs, ret_norm[l], t_ctx, col_ret, h_r)]

        rw = jnp.zeros((d, LANES), F32).at[:, :n_e].set(router_w[l])
        rb = jnp.zeros((1, LANES), F32).at[0, :n_e].set(router_b[l])
        s, tok, logits = _proj_out(mix, s, mods[l], norm_ffn[l], w_out[l].astype(BF16), rw, rb, t_ctx)

        s = _moe(tok, logits[..., :n_e], s, mods[l], t_ctx, n_e, l, w_up_all, w_down_all, b_up_all, b_down_all)
    return s[:, t_ctx:]
```

```python
import functools

import jax
import jax.numpy as jnp
import numpy as np
from jax import lax
from jax.experimental import pallas as pl
from jax.experimental.pallas import tpu as pltpu

F32 = jnp.float32
BF16 = jnp.bfloat16

GRID_W = 64
HEAD_DIM = 64
MLSTM_CHUNK = 64
RET_CHUNK = 64
NA_ROWS = 8
NA_COLS = 16
ROPE_BASE = 10000.0
TOP_K = 4
SWIGLU_ALPHA = 1.702
SWIGLU_LIMIT = 7.0
NORM_EPS = 1e-6
NEG_INF = -1e30

LANES = 128
VMEM_LIMIT = 48 * 1024 * 1024
MOD_ROWS = 8
ROW_TILE = 768
MOE_TILE = 512
MOE_VMEM_LIMIT = 56 * 1024 * 1024


def _cparams(sem):
    return pltpu.CompilerParams(dimension_semantics=sem, vmem_limit_bytes=VMEM_LIMIT)


def _mod_kernel(cc_ref, w_ref, b_ref, o_ref):
    cc = cc_ref[...]
    a = cc * jax.nn.sigmoid(cc)
    o_ref[0] = jnp.dot(a, w_ref[0], precision=lax.Precision.HIGHEST,
                       preferred_element_type=F32) + b_ref[0]


def _modulation(cc, w_mod, b_mod):
    n_l, d, d6 = w_mod.shape
    tn = d6 // 4
    return pl.pallas_call(
        _mod_kernel,
        grid=(n_l, d6 // tn),
        in_specs=[pl.BlockSpec((MOD_ROWS, d), lambda l, j: (0, 0)),
                  pl.BlockSpec((1, d, tn), lambda l, j: (l, 0, j)),
                  pl.BlockSpec((1, 1, tn), lambda l, j: (l, 0, j))],
        out_specs=pl.BlockSpec((1, MOD_ROWS, tn), lambda l, j: (l, 0, j)),
        out_shape=jax.ShapeDtypeStruct((n_l, MOD_ROWS, d6), F32),
        compiler_params=_cparams(("arbitrary", "arbitrary")),
        name="adaln_modulation",
    )(cc, w_mod, b_mod.reshape(n_l, 1, d6))


def _pick_mod(mb_ref, mc_ref, k, d, is_ctx):
    vb = mb_ref[0, :, k * d:(k + 1) * d]
    vc = mc_ref[0, :, k * d:(k + 1) * d]
    return jnp.where(is_ctx, vc, vb)


def _rms(x, g):
    return x * lax.rsqrt(jnp.mean(x * x, axis=-1, keepdims=True) + NORM_EPS) * g


def _pack_bf16_pairs(x):
    m = x.shape[1] // 2
    hi = lax.bitcast_convert_type(x[:, :m].astype(BF16).astype(F32), jnp.uint32)
    lo = lax.bitcast_convert_type(x[:, m:].astype(BF16).astype(F32), jnp.uint32)
    return hi | (lo >> 16)


def _unpack_bf16_pairs(p):
    hi = lax.bitcast_convert_type(p & jnp.uint32(0xFFFF0000), F32)
    lo = lax.bitcast_convert_type(p << 16, F32)
    return jnp.concatenate([hi, lo], axis=1)


PROJ_IN_ROWS = 384
PROJ_IN_COLS = 512


def _proj_in_kernel(s_ref, mb_ref, mc_ref, g_ref, w_ref, wg_ref, pm_ref, pg_ref, *, t_ctx):
    i = pl.program_id(1)
    tm, d = s_ref.shape[1], s_ref.shape[2]
    row = lax.broadcasted_iota(jnp.int32, (tm, 1), 0) + i * tm
    is_ctx = row < t_ctx
    sh = _pick_mod(mb_ref, mc_ref, 0, d, is_ctx)
    sc = _pick_mod(mb_ref, mc_ref, 1, d, is_ctx)
    xn = (_rms(s_ref[0], g_ref[...]) * (1.0 + sc) + sh).astype(BF16)
    pg_ref[0] = jnp.dot(xn, wg_ref[...], preferred_element_type=F32)
    for j in range(w_ref.shape[1] // PROJ_IN_COLS):
        cols = slice(j * PROJ_IN_COLS, (j + 1) * PROJ_IN_COLS)
        pm_ref[0, :, cols] = jnp.dot(xn, w_ref[:, cols], preferred_element_type=F32)


def _proj_in(s, mod_l, g, w_main, w_gate, t_ctx):
    b_sz, s_len, d = s.shape
    n_main = w_main.shape[1]
    n_gate = w_gate.shape[1]
    tm = PROJ_IN_ROWS
    assert s_len % tm == 0 and n_main % PROJ_IN_COLS == 0
    return pl.pallas_call(
        functools.partial(_proj_in_kernel, t_ctx=t_ctx),
        grid=(b_sz, s_len // tm),
        in_specs=[pl.BlockSpec((1, tm, d), lambda b, i: (b, i, 0)),
                  pl.BlockSpec((1, 1, 6 * d), lambda b, i: (b, 0, 0)),
                  pl.BlockSpec((1, 1, 6 * d), lambda b, i: (b_sz, 0, 0)),
                  pl.BlockSpec((1, d), lambda b, i: (0, 0)),
                  pl.BlockSpec((d, n_main), lambda b, i: (0, 0)),
                  pl.BlockSpec((d, n_gate), lambda b, i: (0, 0))],
        out_specs=[pl.BlockSpec((1, tm, n_main), lambda b, i: (b, i, 0)),
                   pl.BlockSpec((1, tm, n_gate), lambda b, i: (b, i, 0))],
        out_shape=[jax.ShapeDtypeStruct((b_sz, s_len, n_main), F32),
                   jax.ShapeDtypeStruct((b_sz, s_len, n_gate), F32)],
        compiler_params=_cparams(("arbitrary", "arbitrary")),
        name="proj_in",
    )(s, mod_l, mod_l, g.reshape(1, d), w_main, w_gate)


def _proj_out_kernel(ma_ref, mb2_ref, mc2_ref, s_ref, mb_ref, mc_ref, g_ref, w_ref, rw_ref, rb_ref,
                     so_ref, tok_ref, lg_ref, *, t_ctx):
    i = pl.program_id(1)
    tm, d = s_ref.shape[1], s_ref.shape[2]
    row = lax.broadcasted_iota(jnp.int32, (tm, 1), 0) + i * tm
    is_ctx = row < t_ctx
    g1 = _pick_mod(mb_ref, mc_ref, 2, d, is_ctx)
    ka, kb = ma_ref.shape[2], ma_ref.shape[2] + mb2_ref.shape[2]
    y = (jnp.dot(ma_ref[0], w_ref[0:ka, :], preferred_element_type=F32)
         + jnp.dot(mb2_ref[0], w_ref[ka:kb, :], preferred_element_type=F32)
         + jnp.dot(mc2_ref[0], w_ref[kb:, :], preferred_element_type=F32))
    s_new = s_ref[0] + g1 * y
    so_ref[0] = s_new
    sh = _pick_mod(mb_ref, mc_ref, 3, d, is_ctx)
    sc = _pick_mod(mb_ref, mc_ref, 4, d, is_ctx)
    t = _rms(s_new, g_ref[...]) * (1.0 + sc) + sh
    tok_ref[0] = t
    w = rw_ref[...]
    t_hi, w_hi = t.astype(BF16), w.astype(BF16)
    t_lo = (t - t_hi.astype(F32)).astype(BF16)
    w_lo = (w - w_hi.astype(F32)).astype(BF16)
    lg_ref[0] = (jnp.dot(t_hi, w_hi, preferred_element_type=F32) + jnp.dot(t_hi, w_lo, preferred_element_type=F32)
                 + jnp.dot(t_lo, w_hi, preferred_element_type=F32) + rb_ref[...])


def _proj_out(mix_parts, s, mod_l, g, w_out, rw, rb, t_ctx):
    b_sz, s_len, d = s.shape
    tm = ROW_TILE
    row_spec = pl.BlockSpec((1, tm, d), lambda b, i: (b, i, 0))
    part_specs = [pl.BlockSpec((1, tm, m.shape[2]), lambda b, i: (b, i, 0)) for m in mix_parts]
    return pl.pallas_call(
        functools.partial(_proj_out_kernel, t_ctx=t_ctx),
        grid=(b_sz, s_len // tm),
        in_specs=part_specs + [row_spec,
                  pl.BlockSpec((1, 1, 6 * d), lambda b, i: (b, 0, 0)),
                  pl.BlockSpec((1, 1, 6 * d), lambda b, i: (b_sz, 0, 0)),
                  pl.BlockSpec((1, d), lambda b, i: (0, 0)),
                  pl.BlockSpec((d, d), lambda b, i: (0, 0)),
                  pl.BlockSpec((d, LANES), lambda b, i: (0, 0)),
                  pl.BlockSpec((1, LANES), lambda b, i: (0, 0))],
        out_specs=[row_spec, row_spec, pl.BlockSpec((1, tm, LANES), lambda b, i: (b, i, 0))],
        out_shape=[jax.ShapeDtypeStruct((b_sz, s_len, d), F32),
                   jax.ShapeDtypeStruct((b_sz, s_len, d), F32),
                   jax.ShapeDtypeStruct((b_sz, s_len, LANES), F32)],
        compiler_params=_cparams(("arbitrary", "arbitrary")),
        name="proj_out_router",
    )(*mix_parts, s, mod_l, mod_l, g.reshape(1, d), w_out, rw, rb)


PAIR = 2 * LANES


def _regroup_perm():
    dst = np.arange(PAIR)
    src = np.where(dst < LANES, 2 * dst, 2 * (dst - LANES) + 1)
    return jnp.asarray(np.arange(PAIR)[:, None] == src[None, :], BF16)


def _regroup_bias(b_up):
    lead = b_up.shape[:-1]
    b = b_up.reshape(*lead, -1, LANES, 2)
    return jnp.swapaxes(b, -1, -2).reshape(*lead, -1)


def _moe_kernel(be_ref, nu_ref, x_ref, wu_ref, wd_ref, bu_ref, bd_ref, p_ref, y_ref, wus_ref, wds_ref):
    i = pl.program_id(0)
    d, f2 = wus_ref.shape
    rows = min(512, d)

    @pl.when(i < nu_ref[0])
    def _():
        @pl.when((i == 0) | (be_ref[i] != be_ref[jnp.maximum(i - 1, 0)]))
        def _():
            for r in range(d // rows):
                for j in range(f2 // PAIR):
                    w = wu_ref[0, r * rows:(r + 1) * rows, j * PAIR:(j + 1) * PAIR].astype(BF16)
                    wus_ref[r * rows:(r + 1) * rows, j * PAIR:(j + 1) * PAIR] = jnp.dot(
                        w, p_ref[...], preferred_element_type=F32).astype(BF16)
            wds_ref[...] = wd_ref[0].astype(BF16)

        up = jnp.dot(x_ref[...].astype(BF16), wus_ref[...], preferred_element_type=F32) + bu_ref[0]
        acts = []
        for j in range(f2 // PAIR):
            glu = jnp.minimum(up[:, j * PAIR:j * PAIR + LANES], SWIGLU_LIMIT)
            lin = jnp.clip(up[:, j * PAIR + LANES:(j + 1) * PAIR], -SWIGLU_LIMIT, SWIGLU_LIMIT)
            acts.append((glu * jax.nn.sigmoid(SWIGLU_ALPHA * glu) * (lin + 1.0)).astype(BF16))
        act = jnp.concatenate(acts, axis=1)
        y_ref[...] = _pack_bf16_pairs(jnp.dot(act, wds_ref[...], preferred_element_type=F32) + bd_ref[0])

    @pl.when(i >= nu_ref[0])
    def _():
        y_ref[...] = jnp.zeros_like(y_ref)


def _moe_blocks(blk_e, n_used, x_sorted, w_up, w_down, b_up, b_down):
    n_rows, d = x_sorted.shape
    _, _, f2 = w_up.shape
    assert f2 % PAIR == 0 and d % min(512, d) == 0
    tm = MOE_TILE
    wmap = lambda i, be, nu: (be[i], 0, 0)
    return pl.pallas_call(
        _moe_kernel,
        grid_spec=pltpu.PrefetchScalarGridSpec(
            num_scalar_prefetch=2,
            grid=(n_rows // tm,),
            in_specs=[pl.BlockSpec((tm, d), lambda i, be, nu: (i, 0)),
                      pl.BlockSpec((1, d, f2), wmap),
                      pl.BlockSpec((1, f2 // 2, d), wmap),
                      pl.BlockSpec((1, 1, f2), wmap),
                      pl.BlockSpec((1, 1, d), wmap),
                      pl.BlockSpec((PAIR, PAIR), lambda i, be, nu: (0, 0))],
            out_specs=pl.BlockSpec((tm, d // 2), lambda i, be, nu: (i, 0)),
            scratch_shapes=[pltpu.VMEM((d, f2), BF16), pltpu.VMEM((f2 // 2, d), BF16)]),
        out_shape=jax.ShapeDtypeStruct((n_rows, d // 2), jnp.uint32),
        compiler_params=pltpu.CompilerParams(dimension_semantics=("arbitrary",),
                                             vmem_limit_bytes=MOE_VMEM_LIMIT),
        name="moe_expert_blocks",
    )(blk_e, n_used, x_sorted, w_up, w_down, b_up, b_down, _regroup_perm())


def _combine_kernel(y_ref, gt_ref, s_ref, mb_ref, mc_ref, o_ref, *, t_ctx):
    i = pl.program_id(1)
    tm, d = s_ref.shape[1], s_ref.shape[2]
    row = lax.broadcasted_iota(jnp.int32, (tm, 1), 0) + i * tm
    g2 = _pick_mod(mb_ref, mc_ref, 5, d, row < t_ctx)
    gt = gt_ref[0]
    y = _unpack_bf16_pairs(y_ref[0, 0]) * gt[:, 0:1]
    for k in range(1, y_ref.shape[0]):
        y = y + _unpack_bf16_pairs(y_ref[k, 0]) * gt[:, k:k + 1]
    o_ref[0] = s_ref[0] + g2 * y


def _combine(y_as, gates, s, mod_l, t_ctx):
    b_sz, s_len, d = s.shape
    n_k = y_as.shape[0]
    tm = ROW_TILE // 3
    row_spec = pl.BlockSpec((1, tm, d), lambda b, i: (b, i, 0))
    return pl.pallas_call(
        functools.partial(_combine_kernel, t_ctx=t_ctx),
        grid=(b_sz, s_len // tm),
        in_specs=[pl.BlockSpec((n_k, 1, tm, d // 2), lambda b, i: (0, b, i, 0)),
                  pl.BlockSpec((1, tm, n_k), lambda b, i: (b, i, 0)),
                  row_spec,
                  pl.BlockSpec((1, 1, 6 * d), lambda b, i: (b, 0, 0)),
                  pl.BlockSpec((1, 1, 6 * d), lambda b, i: (b_sz, 0, 0))],
        out_specs=row_spec,
        out_shape=jax.ShapeDtypeStruct((b_sz, s_len, d), F32),
        compiler_params=_cparams(("arbitrary", "arbitrary")),
        name="moe_combine",
    )(y_as, gates, s, mod_l, mod_l)


def _moe(tok, logits, s, mod_l, t_ctx, n_e, layer, w_up, w_down, b_up, b_down):
    b_sz, s_len, d = tok.shape
    n_tok = b_sz * s_len
    tm = MOE_TILE
    top_v, top_e = lax.top_k(logits.reshape(n_tok, n_e), TOP_K)
    gates = jax.nn.softmax(top_v, axis=-1)
    n_as = n_tok * TOP_K
    onehot = jnp.sum((top_e[:, :, None] == jnp.arange(n_e)[None, None, :]).astype(jnp.int32), axis=1)
    csum = jnp.cumsum(onehot, axis=0)
    counts = csum[-1]
    padded = (counts + tm - 1) // tm * tm
    end_pad = jnp.cumsum(padded)
    start_pad = end_pad - padded
    start = jnp.cumsum(counts) - counts
    dest = jnp.take_along_axis(csum - onehot + start_pad[None, :], top_e, axis=1).astype(jnp.int32)
    n_blocks = -(-n_as // tm) + n_e
    blk_first = jnp.arange(n_blocks) * tm
    blk_e = jnp.minimum(jnp.sum(blk_first[:, None] >= end_pad[None, :], axis=1), n_e - 1).astype(jnp.int32)
    n_used = (end_pad[-1:] // tm).astype(jnp.int32)
    tok_sorted = (jnp.argsort(top_e.reshape(n_as)) // TOP_K).astype(jnp.int32)
    j = (blk_first - start_pad[blk_e])[:, None] + jnp.arange(tm)[None, :]
    src = jnp.clip(start[blk_e][:, None] + j, 0, n_as - 1)
    row_tok = jnp.where(j < counts[blk_e][:, None], tok_sorted[src], 0).reshape(n_blocks * tm)
    x_sorted = tok.reshape(n_tok, d)[row_tok]
    y = _moe_blocks(blk_e + layer * n_e, n_used, x_sorted, w_up, w_down, b_up, b_down)
    y_as = y[dest.T].reshape(TOP_K, b_sz, s_len, d // 2)
    return _combine(y_as, gates.reshape(b_sz, s_len, TOP_K), s, mod_l, t_ctx)


NA_STEP = 256
NA_KEYS = NA_ROWS * GRID_W


def _head_rms(x, w, lo):
    xx = x * x
    s0 = jnp.sum(jnp.where(lo, xx, 0.0), axis=-1, keepdims=True)
    s1 = jnp.sum(jnp.where(lo, 0.0, xx), axis=-1, keepdims=True)
    inv = lax.rsqrt(jnp.where(lo, s0, s1) * (1.0 / HEAD_DIM) + NORM_EPS)
    return x * inv * w


def _dot_nt(a, b):
    return lax.dot_general(a, b, (((1,), (1,)), ((), ())), preferred_element_type=F32)


def _na_kernel(q_ref, k_ref, v_ref, qw_ref, kw_ref, bias_ref, o_ref, kt_ref, vb_ref, s_ref, *, t_ctx, rows):
    rg = pl.program_id(2)
    s_len = k_ref.shape[1]
    lo = lax.broadcasted_iota(jnp.int32, (1, LANES), 1) < HEAD_DIM
    rows_per_step = NA_STEP // GRID_W

    @pl.when(rg == 0)
    def _():
        def knorm_t(t0, n):
            return _head_rms(k_ref[0, pl.ds(t0, n), :], kw_ref[...], lo).T.astype(BF16)

        def prep(c, carry):
            t0 = pl.multiple_of(c * NA_STEP, NA_STEP)
            kt_ref[0, :, pl.ds(t0, NA_STEP)] = knorm_t(t0, NA_STEP)
            vb_ref[pl.ds(t0, NA_STEP), :] = jnp.concatenate(
                [v_ref[0, pl.ds(t0, NA_STEP), :].astype(BF16), jnp.ones((NA_STEP, LANES), BF16)], axis=1)
            return carry
        lax.fori_loop(0, s_len // NA_STEP, prep, 0, unroll=3)

        def prep_shifted(c, carry):
            t0 = pl.multiple_of(c * LANES, LANES)
            kt_ref[1, :, pl.ds(t0, LANES)] = knorm_t(pl.multiple_of(t0 + GRID_W, GRID_W), LANES)
            return carry
        lax.fori_loop(0, (s_len - GRID_W) // LANES, prep_shifted, 0, unroll=5)

    qn = _head_rms(q_ref[0], qw_ref[...], lo) * (HEAD_DIM ** -0.5)
    q0 = jnp.where(lo, qn, 0.0).astype(BF16)
    q1 = jnp.where(lo, 0.0, qn).astype(BF16)
    kc_t = kt_ref[0, :, 0:t_ctx]
    vc = vb_ref[0:t_ctx, :]

    def finish(g, o2):
        o2 = o2[:, :LANES] * (1.0 / o2[:, LANES:])
        o = jnp.where(lo, o2[:GRID_W], o2[GRID_W:])
        o_ref[0, g * GRID_W:(g + 1) * GRID_W, :] = o.astype(o_ref.dtype)

    @pl.when(rg == 0)
    def _():
        for g in range(rows_per_step):
            q2 = jnp.concatenate([q0[g * GRID_W:(g + 1) * GRID_W], q1[g * GRID_W:(g + 1) * GRID_W]], axis=0)
            s_c = jnp.dot(q2, kc_t, preferred_element_type=F32)
            p_c = jnp.exp(s_c - jnp.max(s_c, axis=-1, keepdims=True))
            finish(g, jnp.dot(p_c.astype(BF16), vc, preferred_element_type=F32))

    @pl.when(rg > 0)
    def _():
        t0s = []
        for g in range(rows_per_step):
            r = (rg - 1) * rows_per_step + g
            row_start = jnp.clip(r - NA_ROWS // 2, 0, rows - NA_ROWS)
            dr0 = row_start - r + NA_ROWS - 1
            t0 = pl.multiple_of(t_ctx + row_start * GRID_W, GRID_W)
            odd = (t0 // GRID_W) % (LANES // GRID_W)
            kw_t = kt_ref[odd, :, pl.ds(pl.multiple_of(t0 - odd * GRID_W, LANES), NA_KEYS)]
            q2 = jnp.concatenate([q0[g * GRID_W:(g + 1) * GRID_W], q1[g * GRID_W:(g + 1) * GRID_W]], axis=0)
            s_ref[g, :, :NA_KEYS] = jnp.dot(q2, kw_t, preferred_element_type=F32) + bias_ref[dr0, 0]
            s_ref[g, :, NA_KEYS:] = jnp.dot(q2, kc_t, preferred_element_type=F32)
            t0s.append(t0)
        for g in range(rows_per_step):
            s = s_ref[g]
            p = jnp.exp(s - jnp.max(s, axis=-1, keepdims=True)).astype(BF16)
            finish(g, jnp.dot(p[:, :NA_KEYS], vb_ref[pl.ds(t0s[g], NA_KEYS), :], preferred_element_type=F32)
                   + jnp.dot(p[:, NA_KEYS:], vc, preferred_element_type=F32))


def _na_bias_tables(rpb):
    n_l, n_h, n_dr, n_dc = rpb.shape
    col = np.arange(GRID_W)
    dc = np.clip(col[None, :] - col[:, None] + NA_COLS - 1, 0, n_dc - 1)
    onehot = (dc.reshape(1, -1) == np.arange(n_dc)[:, None]).astype(np.float32)
    toe = jnp.dot(rpb.reshape(-1, n_dc), jnp.asarray(onehot), precision=lax.Precision.HIGHEST)
    toe = toe.reshape(n_l, n_h, n_dr, GRID_W, GRID_W)
    col_start = np.clip(col - NA_COLS // 2, 0, GRID_W - NA_COLS)
    col_in = (col[None, :] >= col_start[:, None]) & (col[None, :] < col_start[:, None] + NA_COLS)
    toe = jnp.where(jnp.asarray(col_in), toe, NEG_INF)
    tabs = []
    for dr0 in range(NA_ROWS):
        t = toe[:, :, dr0:dr0 + NA_ROWS].reshape(n_l, n_h // 2, 2, NA_ROWS, GRID_W, GRID_W)
        t = jnp.transpose(t, (0, 1, 2, 4, 3, 5))
        tabs.append(t.reshape(n_l, n_h // 2, 2 * GRID_W, NA_KEYS))
    return jnp.stack(tabs, axis=1)


def _na_seq_kernel(q_ref, k_ref, v_ref, qw_ref, kw_ref, bias_ref, o_ref, kt_ref, vb_ref, s_ref, *, t_ctx, rows):
    s_len = k_ref.shape[1]
    lo = lax.broadcasted_iota(jnp.int32, (1, LANES), 1) < HEAD_DIM
    rows_per_step = NA_STEP // GRID_W

    def knorm_t(t0, n):
        return _head_rms(k_ref[0, pl.ds(t0, n), :], kw_ref[...], lo).T.astype(BF16)

    def prep(c, carry):
        t0 = pl.multiple_of(c * NA_STEP, NA_STEP)
        kt_ref[0, :, pl.ds(t0, NA_STEP)] = knorm_t(t0, NA_STEP)
        vb_ref[pl.ds(t0, NA_STEP), :] = jnp.concatenate(
            [v_ref[0, pl.ds(t0, NA_STEP), :].astype(BF16), jnp.ones((NA_STEP, LANES), BF16)], axis=1)
        return carry
    lax.fori_loop(0, s_len // NA_STEP, prep, 0, unroll=3)

    def prep_shifted(c, carry):
        t0 = pl.multiple_of(c * LANES, LANES)
        kt_ref[1, :, pl.ds(t0, LANES)] = knorm_t(pl.multiple_of(t0 + GRID_W, GRID_W), LANES)
        return carry
    lax.fori_loop(0, (s_len - GRID_W) // LANES, prep_shifted, 0, unroll=5)

    kc_t = kt_ref[0, :, 0:t_ctx]
    vc = vb_ref[0:t_ctx, :]

    def queries(tok0):
        qn = _head_rms(q_ref[0, pl.ds(tok0, NA_STEP), :], qw_ref[...], lo) * (HEAD_DIM ** -0.5)
        q0 = jnp.where(lo, qn, 0.0).astype(BF16)
        q1 = jnp.where(lo, 0.0, qn).astype(BF16)
        return [jnp.concatenate([q0[g * GRID_W:(g + 1) * GRID_W], q1[g * GRID_W:(g + 1) * GRID_W]], axis=0)
                for g in range(rows_per_step)]

    def finish(tok0, g, o2):
        o2 = o2[:, :LANES] * (1.0 / o2[:, LANES:])
        o = jnp.where(lo, o2[:GRID_W], o2[GRID_W:])
        o_ref[0, pl.ds(tok0 + g * GRID_W, GRID_W), :] = o.astype(o_ref.dtype)

    for g, q2 in enumerate(queries(0)):
        s_c = jnp.dot(q2, kc_t, preferred_element_type=F32)
        p_c = jnp.exp(s_c - jnp.max(s_c, axis=-1, keepdims=True))
        finish(0, g, jnp.dot(p_c.astype(BF16), vc, preferred_element_type=F32))

    def row_group(rg, carry):
        tok0 = pl.multiple_of(t_ctx + rg * NA_STEP, NA_STEP)
        t0s = []
        for g, q2 in enumerate(queries(tok0)):
            r = rg * rows_per_step + g
            row_start = jnp.clip(r - NA_ROWS // 2, 0, rows - NA_ROWS)
            dr0 = row_start - r + NA_ROWS - 1
            t0 = pl.multiple_of(t_ctx + row_start * GRID_W, GRID_W)
            odd = (t0 // GRID_W) % (LANES // GRID_W)
            kw_t = kt_ref[odd, :, pl.ds(pl.multiple_of(t0 - odd * GRID_W, LANES), NA_KEYS)]
            s_ref[g, :, :NA_KEYS] = jnp.dot(q2, kw_t, preferred_element_type=F32) + bias_ref[dr0, 0]
            s_ref[g, :, NA_KEYS:] = jnp.dot(q2, kc_t, preferred_element_type=F32)
            t0s.append(t0)
        for g in range(rows_per_step):
            s = s_ref[g]
            p = jnp.exp(s - jnp.max(s, axis=-1, keepdims=True)).astype(BF16)
            finish(tok0, g, jnp.dot(p[:, :NA_KEYS], vb_ref[pl.ds(t0s[g], NA_KEYS), :], preferred_element_type=F32)
                   + jnp.dot(p[:, NA_KEYS:], vc, preferred_element_type=F32))
        return carry
    lax.fori_loop(0, rows // rows_per_step, row_group, 0)


def _na_attention(p_main, q_w, k_w, bias_tab, t_ctx, col_q, n_heads):
    b_sz, s_len, _ = p_main.shape
    n_hg = n_heads // 2
    rows = (s_len - t_ctx) // GRID_W
    rows_per_step = NA_STEP // GRID_W
    assert t_ctx == NA_STEP and s_len % NA_STEP == 0 and rows >= NA_ROWS and rows % rows_per_step == 0
    cq = col_q // LANES
    seq = lambda col: pl.BlockSpec((1, s_len, LANES), lambda b, h: (b, 0, col + h))
    w2 = lambda w: jnp.concatenate([w, w]).reshape(1, LANES)
    return pl.pallas_call(
        functools.partial(_na_seq_kernel, t_ctx=t_ctx, rows=rows),
        grid=(b_sz, n_hg),
        in_specs=[seq(cq), seq(cq + n_hg), seq(cq + 2 * n_hg),
                  pl.BlockSpec((1, LANES), lambda b, h: (0, 0)),
                  pl.BlockSpec((1, LANES), lambda b, h: (0, 0)),
                  pl.BlockSpec((NA_ROWS, 1, 2 * GRID_W, NA_KEYS), lambda b, h: (0, h, 0, 0))],
        out_specs=pl.BlockSpec((1, s_len, LANES), lambda b, h: (b, 0, h)),
        out_shape=jax.ShapeDtypeStruct((b_sz, s_len, n_heads * HEAD_DIM), BF16),
        scratch_shapes=[pltpu.VMEM((2, LANES, s_len), BF16), pltpu.VMEM((s_len, 2 * LANES), BF16),
                        pltpu.VMEM((rows_per_step, 2 * GRID_W, NA_KEYS + t_ctx), F32)],
        compiler_params=_cparams(("arbitrary", "arbitrary")),
        name="na_attention",
    )(p_main, p_main, p_main, w2(q_w), w2(k_w), bias_tab)


RET_CHUNK_LEN = 256
RET_UNROLL = 3
MLSTM_CHUNK_LEN = 128


def _reverse_chunk(i, nc, n_ctx):
    return jnp.where(i < n_ctx, n_ctx - 1 - i, nc - 1 - i + n_ctx)


def _dot_tn(a, b):
    return lax.dot_general(a, b, (((0,), (0,)), ((), ())), preferred_element_type=F32)


def _rope_tables(t_ctx, t_len):
    pos = jnp.arange(t_len)
    row = (pos // GRID_W).astype(F32)
    col = (pos % GRID_W).astype(F32)
    n = HEAD_DIM // 4
    inv = ROPE_BASE ** (-jnp.arange(n, dtype=F32) / n)
    ar = row[:, None] * inv
    ac = col[:, None] * inv
    cos = jnp.concatenate([jnp.cos(ar), jnp.cos(ar), jnp.cos(ac), jnp.cos(ac)], axis=-1)
    sin = jnp.concatenate([-jnp.sin(ar), jnp.sin(ar), -jnp.sin(ac), jnp.sin(ac)], axis=-1)
    cos = jnp.concatenate([jnp.ones((t_ctx, HEAD_DIM), F32), cos], axis=0)
    sin = jnp.concatenate([jnp.zeros((t_ctx, HEAD_DIM), F32), sin], axis=0)
    return jnp.tile(cos, (1, 2)), jnp.tile(sin, (1, 2))


def _ret_tables(n_heads):
    L = RET_CHUNK_LEN
    pos = np.arange(L, dtype=np.float32)
    lane_head = np.arange(LANES) // HEAD_DIM
    decay = np.zeros((2, n_heads, L, L), np.float32)
    zeta = np.zeros((2, n_heads // 2, LANES, L), np.float32)
    xi = np.zeros((2, n_heads // 2, L, LANES), np.float32)
    gch = np.zeros((2, n_heads // 2, 1, LANES), np.float32)
    for d, first_exp in enumerate((5.0, 6.0)):
        e = np.float32(first_exp) + np.float32(2.0) * np.arange(n_heads, dtype=np.float32)
        lg = np.log1p(-np.exp2(-e)).astype(np.float32)
        diff = pos[:, None] - pos[None, :]
        if d == 1:
            diff = -diff
        for h in range(n_heads):
            decay[d, h] = np.where(diff >= 0, np.exp(lg[h] * np.maximum(diff, 0.0)), 0.0)
        for hp in range(n_heads // 2):
            lgl = lg[2 * hp + lane_head][None, :]
            to_end = (L - 1 - pos if d == 0 else pos)[:, None]
            zeta[d, hp] = np.exp(lgl * to_end).T
            xi[d, hp] = np.exp(lgl * (L - to_end))
            gch[d, hp] = np.exp(lgl * L)
    return tuple(jnp.asarray(a) for a in (decay, zeta, xi, gch))


def _ret_kernel(q_ref, k_ref, v_ref, g_ref, cos_ref, sin_ref, dec_ref, zeta_ref, xi_ref, gch_ref, rn_ref,
                o_ref, qr_ref, kt_ref, vb_ref, acc_ref, *, t_ctx):
    L = RET_CHUNK_LEN
    s_len = q_ref.shape[1]
    nc = s_len // L
    lane = lax.broadcasted_iota(jnp.int32, (1, LANES), 1)
    lo = lane < HEAD_DIM
    half = (lane & (HEAD_DIM // 4)) == 0
    rid = lax.broadcasted_iota(jnp.int32, (LANES, LANES), 0) < HEAD_DIM
    cid = lax.broadcasted_iota(jnp.int32, (LANES, LANES), 1) < HEAD_DIM
    same_head = rid == cid

    def rope(x, cos, sin):
        up = pltpu.roll(x, LANES - HEAD_DIM // 4, axis=1)
        dn = pltpu.roll(x, HEAD_DIM // 4, axis=1)
        return x * cos + jnp.where(half, up, dn) * sin

    def prep(c, carry):
        sl = pl.ds(pl.multiple_of(c * L, L), L)
        cos, sin = cos_ref[sl, :], sin_ref[sl, :]
        qr_ref[sl, :] = rope(q_ref[0, sl, :], cos, sin).astype(BF16)
        kt_ref[:, sl] = (rope(k_ref[0, sl, :], cos, sin) * HEAD_DIM ** -0.5).T.astype(BF16)
        vb_ref[sl, :] = v_ref[0, sl, :].astype(BF16)
        return carry
    lax.fori_loop(0, nc, prep, 0, unroll=RET_UNROLL)

    def chunk(d, c, state):
        sl = pl.ds(pl.multiple_of(c * L, L), L)
        q, kt, v = qr_ref[sl, :], kt_ref[:, sl], vb_ref[sl, :]
        inter = jnp.dot(q, state.astype(BF16), preferred_element_type=F32) * xi_ref[d, 0]
        outs = []
        for h2 in range(2):
            qm = jnp.where(lo if h2 == 0 else jnp.logical_not(lo), q, jnp.zeros_like(q))
            sd = (jnp.dot(qm, kt, preferred_element_type=F32) * dec_ref[d, h2]).astype(BF16)
            outs.append(jnp.dot(sd, v, preferred_element_type=F32))
        y = jnp.where(lo, outs[0], outs[1]) + inter
        kz_t = (kt.astype(F32) * zeta_ref[d, 0]).astype(BF16)
        state = state * gch_ref[d, 0] + jnp.where(same_head, jnp.dot(kz_t, v, preferred_element_type=F32), 0.0)
        return sl, y, state

    def fwd(c, state):
        sl, y, state = chunk(0, c, state)
        acc_ref[sl, :] = y
        return state
    lax.fori_loop(0, nc, fwd, jnp.zeros((LANES, LANES), F32), unroll=RET_UNROLL)

    def bwd(i, state):
        c = _reverse_chunk(i, nc, t_ctx // L)
        sl, y, state = chunk(1, c, state)
        y = _head_rms(y + acc_ref[sl, :], rn_ref[...], lo)
        g = g_ref[0, sl, :]
        o_ref[0, sl, :] = (y * (g * jax.nn.sigmoid(g))).astype(o_ref.dtype)
        return state
    lax.fori_loop(0, nc, bwd, jnp.zeros((LANES, LANES), F32), unroll=RET_UNROLL)


def _seq_spec(s_len, col):
    return pl.BlockSpec((1, s_len, LANES), lambda b, h: (b, 0, col + h), pipeline_mode=pl.Buffered(1))


def _retention(p_main, rope_tabs, ret_tabs, r_w, t_ctx, col_q, n_heads):
    b_sz, s_len, _ = p_main.shape
    n_hp = n_heads // 2
    L = RET_CHUNK_LEN
    assert t_ctx % L == 0 and s_len % L == 0
    c0 = col_q // LANES
    cos, sin = rope_tabs
    decay, zeta, xi, gch = ret_tabs
    const2 = pl.BlockSpec((s_len, LANES), lambda b, h: (0, 0), pipeline_mode=pl.Buffered(1))
    return pl.pallas_call(
        functools.partial(_ret_kernel, t_ctx=t_ctx),
        grid=(b_sz, n_hp),
        in_specs=[_seq_spec(s_len, c0), _seq_spec(s_len, c0 + n_hp), _seq_spec(s_len, c0 + 2 * n_hp),
                  _seq_spec(s_len, c0 + 3 * n_hp), const2, const2,
                  pl.BlockSpec((2, 2, L, L), lambda b, h: (0, h, 0, 0)),
                  pl.BlockSpec((2, 1, LANES, L), lambda b, h: (0, h, 0, 0)),
                  pl.BlockSpec((2, 1, L, LANES), lambda b, h: (0, h, 0, 0)),
                  pl.BlockSpec((2, 1, 1, LANES), lambda b, h: (0, h, 0, 0)),
                  pl.BlockSpec((1, LANES), lambda b, h: (0, 0))],
        out_specs=pl.BlockSpec((1, s_len, LANES), lambda b, h: (b, 0, h)),
        out_shape=jax.ShapeDtypeStruct((b_sz, s_len, n_heads * HEAD_DIM), BF16),
        scratch_shapes=[pltpu.VMEM((s_len, LANES), BF16), pltpu.VMEM((LANES, s_len), BF16),
                        pltpu.VMEM((s_len, LANES), BF16), pltpu.VMEM((s_len, LANES), F32)],
        compiler_params=_cparams(("arbitrary", "arbitrary")),
        name="retention",
    )(p_main, p_main, p_main, p_main, cos, sin, decay, zeta, xi, gch,
      jnp.concatenate([r_w, r_w]).reshape(1, LANES))


N_GATE_TYPES = 4


def _log_sigmoid(x):
    return jnp.minimum(x, 0.0) - jnp.log1p(jnp.exp(-jnp.abs(x)))


def _mlstm_kernel(q_ref, k_ref, v_ref, og_ref, gc_ref, gr_ref, wq_ref, wk_ref, bq_ref, bk_ref, gbc_ref, gbr_ref,
                  mn_ref, o_ref, qc_ref, kc_ref, vb_ref, acc_ref, *, t_ctx):
    L = MLSTM_CHUNK_LEN
    s_len = q_ref.shape[1]
    nc = s_len // L
    lane = lax.broadcasted_iota(jnp.int32, (1, LANES), 1)
    lo = lane < HEAD_DIM
    head_lanes = (lo, jnp.logical_not(lo))
    rid = lax.broadcasted_iota(jnp.int32, (LANES, LANES), 0) < HEAD_DIM
    cid = lax.broadcasted_iota(jnp.int32, (LANES, LANES), 1) < HEAD_DIM
    head_block = (rid & cid, jnp.logical_not(rid | cid))
    row_i = lax.broadcasted_iota(jnp.int32, (L, L), 0)
    col_i = lax.broadcasted_iota(jnp.int32, (L, L), 1)
    causal = (row_i >= col_i, row_i <= col_i)
    tri = tuple(c.astype(F32) for c in causal)
    tri_t = (tri[1], tri[0])
    sub = lax.broadcasted_iota(jnp.int32, (L, 1), 0)

    def conv(x_ref, w_ref, b_ref, t0):
        x = x_ref[0, pl.ds(t0, L), :]
        prev = x_ref[0, pl.ds(jnp.maximum(t0 - 8, 0), 8), :][7:8]
        nxt = x_ref[0, pl.ds(jnp.minimum(t0 + L, s_len - 8), 8), :][0:1]
        prev = jnp.where((t0 != 0) & (t0 != t_ctx), prev, 0.0)
        nxt = jnp.where((t0 + L != t_ctx) & (t0 + L != s_len), nxt, 0.0)
        xm = jnp.where(sub == 0, prev, pltpu.roll(x, 1, axis=0))
        xp = jnp.where(sub == L - 1, nxt, pltpu.roll(x, L - 1, axis=0))
        y = b_ref[...] + xm * w_ref[0:1, :] + x * w_ref[1:2, :] + xp * w_ref[2:3, :]
        return y * jax.nn.sigmoid(y)

    def prep(c, carry):
        t0 = pl.multiple_of(c * L, L)
        sl = pl.ds(t0, L)
        qc_ref[sl, :] = conv(q_ref, wq_ref, bq_ref, t0).astype(BF16)
        kc_ref[sl, :] = (conv(k_ref, wk_ref, bk_ref, t0) * HEAD_DIM ** -0.5).astype(BF16)
        vb_ref[sl, :] = v_ref[0, sl, :].astype(BF16)
        return carry
    lax.fori_loop(0, nc, prep, 0)

    def chunk(d, c, state):
        sl = pl.ds(pl.multiple_of(c * L, L), L)
        q, k, v = qc_ref[sl, :], kc_ref[sl, :], vb_ref[sl, :]
        g_col = gc_ref[0, sl, :] + gbc_ref[0]
        g_row = gr_ref[0, 0, :, sl] + gbr_ref[0]
        cum_col = jnp.dot(tri[d], _log_sigmoid(g_col), precision=lax.Precision.HIGHEST,
                          preferred_element_type=F32)
        cum_row = jnp.dot(_log_sigmoid(g_row), tri_t[d], precision=lax.Precision.HIGHEST,
                          preferred_element_type=F32)
        end = L - 1 if d == 0 else 0
        outs, new_state = [], []
        for h2 in range(2):
            c_st, n_st, m_st = state[h2]
            ci, cf = 2 * (2 * d) + h2, 2 * (2 * d + 1) + h2
            i_col, a_col = g_col[:, ci:ci + 1], cum_col[:, cf:cf + 1]
            i_row, a_row = g_row[ci:ci + 1, :], cum_row[cf:cf + 1, :]
            b_tot = a_row[:, end:end + 1]
            d_log = jnp.where(causal[d], a_col + (i_row - a_row), NEG_INF)
            m_intra = jnp.max(d_log, axis=-1, keepdims=True)
            qm = jnp.where(head_lanes[h2], q, jnp.zeros_like(q))
            s = _dot_nt(qm, k) * jnp.exp(d_log - m_intra)
            num_intra = jnp.dot(s.astype(BF16), v, preferred_element_type=F32)
            den_intra = jnp.sum(s, axis=-1, keepdims=True)
            inter_log = a_col + m_st
            m_q = jnp.maximum(m_intra, inter_log)
            a = jnp.exp(inter_log - m_q)
            g = jnp.exp(m_intra - m_q)
            num = a * jnp.dot(qm, c_st.astype(BF16), preferred_element_type=F32) + g * num_intra
            den = a * jnp.sum(qm.astype(F32) * n_st, axis=-1, keepdims=True) + g * den_intra
            outs.append(num / jnp.maximum(jnp.abs(den), jnp.exp(-m_q)))
            w_log = b_tot - a_col + i_col
            m_loc = jnp.max(w_log, axis=0, keepdims=True)
            ke = jnp.where(head_lanes[h2], k.astype(F32) * jnp.exp(w_log - m_loc), 0.0)
            c_loc = jnp.where(head_block[h2], _dot_tn(ke.astype(BF16), v), 0.0)
            n_loc = jnp.sum(ke, axis=0, keepdims=True)
            m_new = jnp.maximum(b_tot + m_st, m_loc)
            a_s = jnp.exp(b_tot + m_st - m_new)
            g_s = jnp.exp(m_loc - m_new)
            new_state.append((a_s * c_st + g_s * c_loc, a_s * n_st + g_s * n_loc, m_new))
        return sl, jnp.where(lo, outs[0], outs[1]), tuple(new_state)

    zero = tuple((jnp.zeros((LANES, LANES), F32), jnp.zeros((1, LANES), F32), jnp.zeros((1, 1), F32))
                 for _ in range(2))

    def fwd(c, state):
        sl, y, state = chunk(0, c, state)
        acc_ref[sl, :] = y
        return state
    lax.fori_loop(0, nc, fwd, zero)

    def bwd(i, state):
        c = _reverse_chunk(i, nc, t_ctx // L)
        sl, y, state = chunk(1, c, state)
        y = _head_rms(y + acc_ref[sl, :], mn_ref[...], lo)
        o_ref[0, sl, :] = (y * jax.nn.sigmoid(og_ref[0, sl, :])).astype(o_ref.dtype)
        return state
    lax.fori_loop(0, nc, bwd, zero)


def _mlstm_pair_kernel(q_ref, k_ref, v_ref, og_ref, gc_ref, gr_ref, wq_ref, wk_ref, bq_ref, bk_ref, gbc_ref, gbr_ref,
                       mn_ref, o_ref, qc_ref, kt_ref, vb_ref, acc_ref, st_ref, cc_ref, cr_ref, *, t_ctx):
    L = MLSTM_CHUNK_LEN
    s_len = q_ref.shape[1]
    nc = s_len // L
    lane = lax.broadcasted_iota(jnp.int32, (1, LANES), 1)
    lo = lane < HEAD_DIM
    head_lanes = (lo, jnp.logical_not(lo))
    sub_lo = lax.broadcasted_iota(jnp.int32, (LANES, 1), 0) < HEAD_DIM
    head_rows = (sub_lo, jnp.logical_not(sub_lo))
    row_i = lax.broadcasted_iota(jnp.int32, (L, L), 0)
    col_i = lax.broadcasted_iota(jnp.int32, (L, L), 1)
    causal = (row_i >= col_i, row_i <= col_i)
    tri = tuple(c.astype(F32) for c in causal)
    tri_t = (tri[1], tri[0])
    sub = lax.broadcasted_iota(jnp.int32, (L, 1), 0)
    gate_row = lax.broadcasted_iota(jnp.int32, (2 * N_GATE_TYPES, 1), 0)
    ones = jnp.ones((L, LANES), BF16)

    def conv(x_ref, w_ref, b_ref, t0):
        x = x_ref[0, pl.ds(t0, L), :]
        prev = x_ref[0, pl.ds(jnp.maximum(t0 - 8, 0), 8), :][7:8]
        nxt = x_ref[0, pl.ds(jnp.minimum(t0 + L, s_len - 8), 8), :][0:1]
        prev = jnp.where((t0 != 0) & (t0 != t_ctx), prev, 0.0)
        nxt = jnp.where((t0 + L != t_ctx) & (t0 + L != s_len), nxt, 0.0)
        xm = jnp.where(sub == 0, prev, pltpu.roll(x, 1, axis=0))
        xp = jnp.where(sub == L - 1, nxt, pltpu.roll(x, L - 1, axis=0))
        y = b_ref[...] + xm * w_ref[0:1, :] + x * w_ref[1:2, :] + xp * w_ref[2:3, :]
        return y * jax.nn.sigmoid(y)

    def prep(c, carry):
        t0 = pl.multiple_of(c * L, L)
        sl = pl.ds(t0, L)
        qc_ref[sl, :] = conv(q_ref, wq_ref, bq_ref, t0).astype(BF16)
        kt_ref[:, sl] = (conv(k_ref, wk_ref, bk_ref, t0) * HEAD_DIM ** -0.5).T.astype(BF16)
        vb_ref[sl, :] = v_ref[0, sl, :].astype(BF16)
        lf_col = _log_sigmoid(gc_ref[0, sl, :] + gbc_ref[0])
        lf_row = _log_sigmoid(gr_ref[0, 0, :, sl] + gbr_ref[0])
        hdot = functools.partial(jnp.dot, precision=lax.Precision.HIGHEST, preferred_element_type=F32)
        cc_ref[sl, :] = jnp.where(lane < N_GATE_TYPES, hdot(tri[0], lf_col), hdot(tri[1], lf_col))
        cr_ref[:, sl] = jnp.where(gate_row < N_GATE_TYPES, hdot(lf_row, tri_t[0]), hdot(lf_row, tri_t[1]))
        return carry
    lax.fori_loop(0, nc, prep, 0, unroll=2)

    def chunk(d, c, m_state):
        sl = pl.ds(pl.multiple_of(c * L, L), L)
        q, kt, v = qc_ref[sl, :], kt_ref[:, sl], vb_ref[sl, :]
        g_row = gr_ref[0, 0, :, sl] + gbr_ref[0]
        cum_col, cum_row = cc_ref[sl, :], cr_ref[:, sl]
        end = L - 1 if d == 0 else 0
        outs, new_m = [], []
        for h2 in range(2):
            m_st = m_state[h2]
            ci, cf = 2 * (2 * d) + h2, 2 * (2 * d + 1) + h2
            a_rep = jnp.broadcast_to(cum_col[:, cf:cf + 1], (L, LANES))
            i_row, a_row = g_row[ci:ci + 1, :], cum_row[cf:cf + 1, :]
            b_tot = a_row[:, end:end + 1]
            d_log = jnp.where(causal[d], a_rep + (i_row - a_row), NEG_INF)
            m_intra = jnp.broadcast_to(jnp.max(d_log, axis=-1, keepdims=True), (L, LANES))
            qm = jnp.where(head_lanes[h2], q, jnp.zeros_like(q))
            s = jnp.dot(qm, kt, preferred_element_type=F32) * jnp.exp(d_log - m_intra)
            v1 = jnp.where(head_lanes[h2], v, ones)
            intra = jnp.dot(s.astype(BF16), v1, preferred_element_type=F32)
            inter = jnp.dot(qm, st_ref[h2].astype(BF16), preferred_element_type=F32)
            inter_log = a_rep + m_st
            m_q = jnp.maximum(m_intra, inter_log)
            num_den = jnp.exp(inter_log - m_q) * inter + jnp.exp(m_intra - m_q) * intra
            den = pltpu.roll(num_den, HEAD_DIM, axis=1)
            outs.append(num_den / jnp.maximum(jnp.abs(den), jnp.exp(-m_q)))
            w_row = b_tot - a_row + i_row
            m_loc = jnp.max(w_row, axis=-1, keepdims=True)
            kts = jnp.where(head_rows[h2], kt.astype(F32) * jnp.exp(w_row - m_loc), 0.0).astype(BF16)
            loc = jnp.dot(kts, v1, preferred_element_type=F32)
            m_new = jnp.maximum(b_tot + m_st, m_loc)
            st_ref[h2] = jnp.exp(b_tot + m_st - m_new) * st_ref[h2] + jnp.exp(m_loc - m_new) * loc
            new_m.append(m_new)
        return sl, jnp.where(lo, outs[0], outs[1]), tuple(new_m)

    zero_m = (jnp.zeros((1, 1), F32), jnp.zeros((1, 1), F32))

    st_ref[...] = jnp.zeros_like(st_ref)

    def fwd(c, m_state):
        sl, y, m_state = chunk(0, c, m_state)
        acc_ref[sl, :] = y
        return m_state
    lax.fori_loop(0, nc, fwd, zero_m, unroll=2)

    st_ref[...] = jnp.zeros_like(st_ref)

    def bwd(i, m_state):
        c = _reverse_chunk(i, nc, t_ctx // L)
        sl, y, m_state = chunk(1, c, m_state)
        y = _head_rms(y + acc_ref[sl, :], mn_ref[...], lo)
        o_ref[0, sl, :] = (y * jax.nn.sigmoid(og_ref[0, sl, :])).astype(o_ref.dtype)
        return m_state
    lax.fori_loop(0, nc, bwd, zero_m, unroll=2)


def _mlstm(p_main, p_gate, conv_w, conv_b, gate_b, m_w, t_ctx, n_heads):
    b_sz, s_len, _ = p_main.shape
    n_hp = n_heads // 2
    L = MLSTM_CHUNK_LEN
    assert L == LANES and t_ctx % L == 0 and s_len % L == 0
    n_g = 2 * N_GATE_TYPES
    g_rows = jnp.transpose(p_gate.reshape(b_sz, s_len, n_hp, LANES)[..., :n_g], (0, 2, 3, 1))
    gb = jnp.transpose(gate_b.reshape(N_GATE_TYPES, n_hp, 2), (1, 0, 2)).reshape(n_hp, n_g)
    gb_col = jnp.zeros((n_hp, 1, LANES), F32).at[:, 0, :n_g].set(gb)
    gb_row = gb.reshape(n_hp, n_g, 1)
    vec = lambda col: pl.BlockSpec((1, LANES), lambda b, h: (0, col + h))
    return pl.pallas_call(
        functools.partial(_mlstm_pair_kernel, t_ctx=t_ctx),
        grid=(b_sz, n_hp),
        in_specs=[_seq_spec(s_len, 0), _seq_spec(s_len, n_hp), _seq_spec(s_len, 2 * n_hp),
                  _seq_spec(s_len, 3 * n_hp),
                  pl.BlockSpec((1, s_len, LANES), lambda b, h: (b, 0, h), pipeline_mode=pl.Buffered(1)),
                  pl.BlockSpec((1, 1, n_g, s_len), lambda b, h: (b, h, 0, 0)),
                  pl.BlockSpec((3, LANES), lambda b, h: (0, h)),
                  pl.BlockSpec((3, LANES), lambda b, h: (0, n_hp + h)),
                  vec(0), vec(n_hp),
                  pl.BlockSpec((1, 1, LANES), lambda b, h: (h, 0, 0)),
                  pl.BlockSpec((1, n_g, 1), lambda b, h: (h, 0, 0)),
                  pl.BlockSpec((1, LANES), lambda b, h: (0, 0))],
        out_specs=pl.BlockSpec((1, s_len, LANES), lambda b, h: (b, 0, h)),
        out_shape=jax.ShapeDtypeStruct((b_sz, s_len, n_heads * HEAD_DIM), BF16),
        scratch_shapes=[pltpu.VMEM((s_len, LANES), BF16), pltpu.VMEM((LANES, s_len), BF16),
                        pltpu.VMEM((s_len, LANES), BF16), pltpu.VMEM((s_len, LANES), F32),
                        pltpu.VMEM((2, LANES, LANES), F32),
                        pltpu.VMEM((s_len, LANES), F32), pltpu.VMEM((n_g, s_len), F32)],
        compiler_params=_cparams(("arbitrary", "arbitrary")),
        name="mlstm",
    )(p_main, p_main, p_main, p_main, p_gate, g_rows, conv_w, conv_w, conv_b.reshape(1, -1),
      conv_b.reshape(1, -1), gb_col, gb_row, jnp.concatenate([m_w, m_w]).reshape(1, LANES))


def _rms_norm(x, g):
    xf = x.astype(F32)
    y = xf * lax.rsqrt(jnp.mean(xf * xf, axis=-1, keepdims=True) + NORM_EPS)
    return (y * g.astype(F32)).astype(x.dtype)


def _split_heads(t, n_heads):
    return t.reshape(t.shape[0], t.shape[1], n_heads, HEAD_DIM)


def _to_bhtd(t):
    return jnp.transpose(t, (0, 2, 1, 3))


def _merge_heads(t):
    b, h, t_len, d = t.shape
    return jnp.transpose(t, (0, 2, 1, 3)).reshape(b, t_len, h * d)


def _flip_t(t):
    return jnp.flip(t, axis=2)


def _dwconv_centred(x, w, b):
    k_size = w.shape[0]
    pad = k_size // 2
    t_len = x.shape[1]
    xp = jnp.pad(x, ((0, 0), (pad, pad), (0, 0)))
    y = b
    for i in range(k_size):
        y = y + xp[:, i:i + t_len] * w[i]
    return y


def _axial_rope_tables(t_len):
    pos = jnp.arange(t_len)
    row = (pos // GRID_W).astype(F32)
    col = (pos % GRID_W).astype(F32)
    n = HEAD_DIM // 4
    inv = ROPE_BASE ** (-jnp.arange(n, dtype=F32) / n)
    ar = row[:, None] * inv
    ac = col[:, None] * inv
    return (jnp.cos(ar), jnp.sin(ar), jnp.cos(ac), jnp.sin(ac))


def _rotate_half(x, cos, sin):
    x1, x2 = jnp.split(x, 2, axis=-1)
    return jnp.concatenate([x1 * cos - x2 * sin, x2 * cos + x1 * sin], axis=-1)


def _axial_rope(x, cos_r, sin_r, cos_c, sin_c):
    xr, xc = jnp.split(x, 2, axis=-1)
    return jnp.concatenate([_rotate_half(xr, cos_r, sin_r), _rotate_half(xc, cos_c, sin_c)], axis=-1)


def _mlstm_chunkwise(q, k, v, log_i, log_f, state0):
    b_sz, h_sz, t_len, d = q.shape
    L = MLSTM_CHUNK
    nc = t_len // L
    qc = q.reshape(b_sz, h_sz, nc, L, d)
    kc = k.reshape(b_sz, h_sz, nc, L, d)
    vc = v.reshape(b_sz, h_sz, nc, L, d)
    li = log_i.reshape(b_sz, h_sz, nc, L)
    bcum = jnp.cumsum(log_f.reshape(b_sz, h_sz, nc, L), axis=-1)
    b_tot = bcum[..., -1]
    tri = jnp.tril(jnp.ones((L, L), dtype=bool))
    d_log = jnp.where(tri, bcum[..., :, None] - bcum[..., None, :] + li[..., None, :], NEG_INF)
    m_intra = jnp.max(d_log, axis=-1)
    s = jnp.einsum('bhcjd,bhcld->bhcjl', qc, kc).astype(F32) * jnp.exp(d_log - m_intra[..., None])
    num_intra = jnp.einsum('bhcjl,bhcld->bhcjd', s, vc)
    den_intra = jnp.sum(s, axis=-1)
    w_log = b_tot[..., None] - bcum + li
    m_loc = jnp.max(w_log, axis=-1)
    e = jnp.exp(w_log - m_loc[..., None])
    c_loc = jnp.einsum('bhcl,bhcld,bhcle->bhcde', e, kc, vc)
    n_loc = jnp.einsum('bhcl,bhcld->bhcd', e, kc)

    def step(carry, inp):
        c_st, n_st, m_st = carry
        cl, nl, ml, bt = inp
        m_new = jnp.maximum(bt + m_st, ml)
        a = jnp.exp(bt + m_st - m_new)
        g = jnp.exp(ml - m_new)
        new = (a[..., None, None] * c_st + g[..., None, None] * cl, a[..., None] * n_st + g[..., None] * nl, m_new)
        return new, carry

    xs = (jnp.moveaxis(c_loc, 2, 0), jnp.moveaxis(n_loc, 2, 0), jnp.moveaxis(m_loc, 2, 0), jnp.moveaxis(b_tot, 2, 0))
    final, prev = lax.scan(step, state0, xs)
    c_prev = jnp.moveaxis(prev[0], 0, 2)
    n_prev = jnp.moveaxis(prev[1], 0, 2)
    m_prev = jnp.moveaxis(prev[2], 0, 2)
    inter_log = bcum + m_prev[..., None]
    m_q = jnp.maximum(m_intra, inter_log)
    a = jnp.exp(inter_log - m_q)
    g = jnp.exp(m_intra - m_q)
    num = a[..., None] * jnp.einsum('bhcjd,bhcde->bhcje', qc, c_prev) + g[..., None] * num_intra
    den = a * jnp.einsum('bhcjd,bhcd->bhcj', qc, n_prev) + g * den_intra
    h = num / jnp.maximum(jnp.abs(den), jnp.exp(-m_q))[..., None]
    return h.reshape(b_sz, h_sz, t_len, d), final


def _retention_log_decay(first_exp, n_heads):
    e = first_exp + 2.0 * jnp.arange(n_heads, dtype=F32)
    return jnp.log1p(-jnp.exp2(-e))


def _retention_chunkwise(q, k, v, log_gamma, state0):
    b_sz, h_sz, t_len, d = q.shape
    L = RET_CHUNK
    nc = t_len // L
    qc = q.reshape(b_sz, h_sz, nc, L, d)
    kc = k.reshape(b_sz, h_sz, nc, L, d)
    vc = v.reshape(b_sz, h_sz, nc, L, d)
    pos = jnp.arange(L, dtype=F32)
    diff = pos[:, None] - pos[None, :]
    decay = jnp.where(diff >= 0, jnp.exp(log_gamma[:, None, None] * jnp.maximum(diff, 0.0)), 0.0)
    s = jnp.einsum('bhcjd,bhcld->bhcjl', qc, kc).astype(F32) * decay[:, None]
    intra = jnp.einsum('bhcjl,bhcld->bhcjd', s, vc)
    zeta = jnp.exp(log_gamma[:, None] * (L - 1 - pos))
    s_loc = jnp.einsum('hl,bhcld,bhcle->bhcde', zeta, kc, vc)
    g_chunk = jnp.exp(log_gamma * L)[:, None, None]

    def step(r, sl):
        return g_chunk * r + sl, r

    final, r_prev = lax.scan(step, state0, jnp.moveaxis(s_loc, 2, 0))
    r_prev = jnp.moveaxis(r_prev, 0, 2)
    xi = jnp.exp(log_gamma[:, None] * (pos + 1.0))
    inter = jnp.einsum('bhcjd,bhcde->bhcje', qc, r_prev) * xi[:, None, :, None]
    return (intra + inter).reshape(b_sz, h_sz, t_len, d), final


def _na_window(rows):
    wr = min(NA_ROWS, rows)
    r = jnp.arange(rows)
    col = jnp.arange(GRID_W)
    row_idx = jnp.clip(r - wr // 2, 0, rows - wr)[:, None] + jnp.arange(wr)[None, :]
    col_start = jnp.clip(col - NA_COLS // 2, 0, GRID_W - NA_COLS)
    col_in = (col[None, :] >= col_start[:, None]) & (col[None, :] < col_start[:, None] + NA_COLS)
    dr = row_idx - r[:, None] + NA_ROWS - 1
    dc = jnp.clip(col[None, :] - col[:, None] + NA_COLS - 1, 0, 2 * NA_COLS - 2)
    return (row_idx, col_in, dr[:, None, :, None], dc[None, :, None, :])


def _na_latent(q, k, v, k_ctx, v_ctx, row_idx, col_in, bias):
    t_len, h_sz, d = q.shape
    rows, wr = row_idx.shape
    scale = d ** -0.5
    qg = q.reshape(rows, GRID_W, h_sz, d)
    kb = k.reshape(rows, GRID_W, h_sz, d)[row_idx]
    vb = v.reshape(rows, GRID_W, h_sz, d)[row_idx]
    s_loc = jnp.einsum('rqhd,rjwhd->hrqjw', qg, kb).astype(F32) * scale + bias
    s_loc = jnp.where(col_in[:, None, :], s_loc, NEG_INF)
    s_ctx = jnp.einsum('rqhd,chd->hrqc', qg, k_ctx).astype(F32) * scale
    n_loc = wr * GRID_W
    s = jnp.concatenate([s_loc.reshape(h_sz, rows, GRID_W, n_loc), s_ctx], axis=-1)
    p = jax.nn.softmax(s, axis=-1).astype(v.dtype)
    p_loc = p[..., :n_loc].reshape(h_sz, rows, GRID_W, wr, GRID_W)
    o = jnp.einsum('hrqjw,rjwhd->rqhd', p_loc, vb) + jnp.einsum('hrqc,chd->rqhd', p[..., n_loc:], v_ctx)
    return o.reshape(t_len, h_sz * d)


def _ctx_attention(q, k, v):
    s = jnp.einsum('bqhd,bkhd->bhqk', q, k).astype(F32) * HEAD_DIM ** -0.5
    p = jax.nn.softmax(s, axis=-1).astype(v.dtype)
    o = jnp.einsum('bhqk,bkhd->bqhd', p, v)
    return o.reshape(o.shape[0], o.shape[1], -1)


def _hybrid_mixer(xp, cp, rope, na_win, conv_w, conv_b, gate_b, m_norm, q_norm, k_norm, rpb, r_norm, dims):
    h_m, h_na, h_r = dims
    b_sz = xp[0].shape[0]

    def mlstm_prep(p):
        qk = jax.nn.silu(_dwconv_centred(p[0], conv_w, conv_b))
        q, k = jnp.split(qk, 2, axis=-1)
        g = (p[3] + gate_b).astype(F32)
        g = jnp.transpose(g.reshape(b_sz, -1, 4, h_m), (2, 0, 3, 1))
        return (_to_bhtd(_split_heads(q, h_m)), _to_bhtd(_split_heads(k, h_m)) * HEAD_DIM ** -0.5,
                _to_bhtd(_split_heads(p[1], h_m)), g[0], jax.nn.log_sigmoid(g[1]), g[2], jax.nn.log_sigmoid(g[3]))

    qx, kx, vx, ix_f, lfx_f, ix_b, lfx_b = mlstm_prep(xp)
    qc, kc, vc, ic_f, lfc_f, ic_b, lfc_b = mlstm_prep(cp)
    zero_m = (jnp.zeros((b_sz, h_m, HEAD_DIM, HEAD_DIM), F32),
              jnp.zeros((b_sz, h_m, HEAD_DIM), F32), jnp.zeros((b_sz, h_m), F32))
    hc_f, st_f = _mlstm_chunkwise(qc, kc, vc, ic_f, lfc_f, zero_m)
    hc_b, st_b = _mlstm_chunkwise(_flip_t(qc), _flip_t(kc), _flip_t(vc), _flip_t(ic_b), _flip_t(lfc_b), zero_m)
    hx_f, _ = _mlstm_chunkwise(qx, kx, vx, ix_f, lfx_f, st_f)
    hx_b, _ = _mlstm_chunkwise(_flip_t(qx), _flip_t(kx), _flip_t(vx), _flip_t(ix_b), _flip_t(lfx_b), st_b)

    def mlstm_out(h, o):
        return _merge_heads(_rms_norm(h, m_norm)).astype(o.dtype) * jax.nn.sigmoid(o)

    a_x = mlstm_out(hx_f + _flip_t(hx_b), xp[2])

    def na_prep(p):
        return (_rms_norm(_split_heads(p[4], h_na), q_norm), _rms_norm(_split_heads(p[5], h_na), k_norm),
                _split_heads(p[6], h_na))

    nqx, nkx, nvx = na_prep(xp)
    nqc, nkc, nvc = na_prep(cp)
    row_idx, col_in, dr, dc = na_win
    bias = rpb[:, dr, dc].astype(F32)
    b_x = lax.map(lambda a: _na_latent(a[0], a[1], a[2], a[3], a[4], row_idx, col_in, bias),
                  (nqx, nkx, nvx, nkc, nvc))

    lg_f = _retention_log_decay(5.0, h_r)
    lg_b = _retention_log_decay(6.0, h_r)
    rqx = _axial_rope(_to_bhtd(_split_heads(xp[7], h_r)), *rope)
    rkx = _axial_rope(_to_bhtd(_split_heads(xp[8], h_r)), *rope) * HEAD_DIM ** -0.5
    rvx = _to_bhtd(_split_heads(xp[9], h_r))
    rqc = _to_bhtd(_split_heads(cp[7], h_r))
    rkc = _to_bhtd(_split_heads(cp[8], h_r)) * HEAD_DIM ** -0.5
    rvc = _to_bhtd(_split_heads(cp[9], h_r))
    zero_r = jnp.zeros((b_sz, h_r, HEAD_DIM, HEAD_DIM), F32)
    rc_f, rs_f = _retention_chunkwise(rqc, rkc, rvc, lg_f, zero_r)
    rc_b, rs_b = _retention_chunkwise(_flip_t(rqc), _flip_t(rkc), _flip_t(rvc), lg_b, zero_r)
    rx_f, _ = _retention_chunkwise(rqx, rkx, rvx, lg_f, rs_f)
    rx_b, _ = _retention_chunkwise(_flip_t(rqx), _flip_t(rkx), _flip_t(rvx), lg_b, rs_b)

    def ret_out(h, g):
        return _merge_heads(_rms_norm(h, r_norm)).astype(g.dtype) * jax.nn.silu(g)

    c_x = ret_out(rx_f + _flip_t(rx_b), xp[10])
    mix_x = jnp.concatenate([a_x, b_x, c_x], axis=-1)
    a_c = mlstm_out(hc_f + _flip_t(hc_b), cp[2])
    b_c = _ctx_attention(nqc, nkc, nvc)
    c_c = ret_out(rc_f + _flip_t(rc_b), cp[10])
    return mix_x, jnp.concatenate([a_c, b_c, c_c], axis=-1)


def kernel(x, c, ctx, c_ctx, w_mod, b_mod, norm_mix, norm_ffn, w_in, w_out, mlstm_conv_w, mlstm_conv_b,
           mlstm_gate_b, mlstm_norm, na_q_norm, na_k_norm, na_rpb, ret_norm, router_w, router_b,
           expert_w_up, expert_b_up, expert_w_down, expert_b_down):
    b_sz, t_len, d = x.shape
    t_ctx = ctx.shape[1]
    depth = w_in.shape[0]
    n_e = router_w.shape[2]
    d_mix = w_out.shape[1]
    h_m = d_mix // (4 * HEAD_DIM)
    h_na = d_mix // (2 * HEAD_DIM)
    h_r = d_mix // (4 * HEAD_DIM)
    d_m, d_na, d_r = h_m * HEAD_DIM, h_na * HEAD_DIM, h_r * HEAD_DIM
    n_gate = 4 * h_m
    assert b_sz + 1 <= MOD_ROWS and n_e <= LANES and n_gate <= LANES
    s_len = t_ctx + t_len
    assert s_len % ROW_TILE == 0

    s = jnp.concatenate([ctx, x], axis=1)
    cc = jnp.zeros((MOD_ROWS, d), F32).at[:b_sz].set(c).at[b_sz].set(c_ctx)
    mods = _modulation(cc, w_mod, b_mod).reshape(depth, MOD_ROWS, 1, 6 * d)

    g0 = 4 * d_m
    col_na = g0
    col_ret = col_na + 3 * d_na
    n_hp = h_m // 2
    gate_src = np.array([[g0 + t * h_m + 2 * hp + h2 for t in range(N_GATE_TYPES) for h2 in range(2)]
                         for hp in range(n_hp)])
    rope_tabs = _rope_tables(t_ctx, t_len)
    ret_tabs = _ret_tables(h_r)
    na_tabs = _na_bias_tables(na_rpb)
    f2 = expert_w_up.shape[3]
    w_up_all = expert_w_up.reshape(depth * n_e, d, f2)
    w_down_all = expert_w_down.reshape(depth * n_e, f2 // 2, d)
    b_up_all = _regroup_bias(expert_b_up).reshape(depth * n_e, 1, f2)
    b_down_all = expert_b_down.reshape(depth * n_e, 1, d)

    for l in range(depth):
        w_main = jnp.concatenate([w_in[l, :, :g0], w_in[l, :, g0 + n_gate:]], axis=1).astype(BF16)
        w_gate = jnp.zeros((d, n_hp, LANES), BF16).at[:, :, :gate_src.shape[1]].set(
            w_in[l][:, gate_src].astype(BF16)).reshape(d, n_hp * LANES)
        p_main, p_gate = _proj_in(s, mods[l], norm_mix[l], w_main, w_gate, t_ctx)

        mix = [_mlstm(p_main, p_gate, mlstm_conv_w[l], mlstm_conv_b[l], mlstm_gate_b[l], mlstm_norm[l],
                      t_ctx, h_m),
               _na_attention(p_main, na_q_norm[l], na_k_norm[l], na_tabs[l], t_ctx, col_na, h_na),
               _retention(p_main, rope_tabs, ret_tabs, ret_norm[l], t_ctx, col_ret, h_r)]

        rw = jnp.zeros((d, LANES), F32).at[:, :n_e].set(router_w[l])
        rb = jnp.zeros((1, LANES), F32).at[0, :n_e].set(router_b[l])
        s, tok, logits = _proj_out(mix, s, mods[l], norm_ffn[l], w_out[l].astype(BF16), rw, rb, t_ctx)

        s = _moe(tok, logits[..., :n_e], s, mods[l], t_ctx, n_e, l, w_up_all, w_down_all, b_up_all, b_down_all)
    return s[:, t_ctx:]
```

```python
import functools

import jax
import jax.numpy as jnp
import numpy as np
from jax import lax
from jax.experimental import pallas as pl
from jax.experimental.pallas import tpu as pltpu

F32 = jnp.float32
BF16 = jnp.bfloat16

GRID_W = 64
HEAD_DIM = 64
MLSTM_CHUNK = 64
RET_CHUNK = 64
NA_ROWS = 8
NA_COLS = 16
ROPE_BASE = 10000.0
TOP_K = 4
SWIGLU_ALPHA = 1.702
SWIGLU_LIMIT = 7.0
NORM_EPS = 1e-6
NEG_INF = -1e30

LANES = 128
VMEM_LIMIT = 48 * 1024 * 1024
MOD_ROWS = 8
ROW_TILE = 768
MOE_TILE = 512
MOE_VMEM_LIMIT = 56 * 1024 * 1024


def _cparams(sem):
    return pltpu.CompilerParams(dimension_semantics=sem, vmem_limit_bytes=VMEM_LIMIT)


def _mod_kernel(cc_ref, w_ref, b_ref, o_ref):
    cc = cc_ref[...]
    a = cc * jax.nn.sigmoid(cc)
    o_ref[0] = jnp.dot(a, w_ref[0], precision=lax.Precision.HIGHEST,
                       preferred_element_type=F32) + b_ref[0]


def _modulation(cc, w_mod, b_mod):
    n_l, d, d6 = w_mod.shape
    tn = d6 // 4
    return pl.pallas_call(
        _mod_kernel,
        grid=(n_l, d6 // tn),
        in_specs=[pl.BlockSpec((MOD_ROWS, d), lambda l, j: (0, 0)),
                  pl.BlockSpec((1, d, tn), lambda l, j: (l, 0, j)),
                  pl.BlockSpec((1, 1, tn), lambda l, j: (l, 0, j))],
        out_specs=pl.BlockSpec((1, MOD_ROWS, tn), lambda l, j: (l, 0, j)),
        out_shape=jax.ShapeDtypeStruct((n_l, MOD_ROWS, d6), F32),
        compiler_params=_cparams(("arbitrary", "arbitrary")),
        name="adaln_modulation",
    )(cc, w_mod, b_mod.reshape(n_l, 1, d6))


def _pick_mod(mb_ref, mc_ref, k, d, is_ctx):
    vb = mb_ref[0, :, k * d:(k + 1) * d]
    vc = mc_ref[0, :, k * d:(k + 1) * d]
    return jnp.where(is_ctx, vc, vb)


def _rms(x, g):
    return x * lax.rsqrt(jnp.mean(x * x, axis=-1, keepdims=True) + NORM_EPS) * g


def _pack_bf16_pairs(x):
    m = x.shape[1] // 2
    hi = lax.bitcast_convert_type(x[:, :m].astype(BF16).astype(F32), jnp.uint32)
    lo = lax.bitcast_convert_type(x[:, m:].astype(BF16).astype(F32), jnp.uint32)
    return hi | (lo >> 16)


def _unpack_bf16_pairs(p):
    hi = lax.bitcast_convert_type(p & jnp.uint32(0xFFFF0000), F32)
    lo = lax.bitcast_convert_type(p << 16, F32)
    return jnp.concatenate([hi, lo], axis=1)


PROJ_IN_ROWS = 384
PROJ_IN_COLS = 512


def _proj_in_kernel(s_ref, mb_ref, mc_ref, g_ref, w_ref, wg_ref, pm_ref, pg_ref, *, t_ctx):
    i = pl.program_id(1)
    tm, d = s_ref.shape[1], s_ref.shape[2]
    row = lax.broadcasted_iota(jnp.int32, (tm, 1), 0) + i * tm
    is_ctx = row < t_ctx
    sh = _pick_mod(mb_ref, mc_ref, 0, d, is_ctx)
    sc = _pick_mod(mb_ref, mc_ref, 1, d, is_ctx)
    xn = (_rms(s_ref[0], g_ref[...]) * (1.0 + sc) + sh).astype(BF16)
    pg_ref[0] = jnp.dot(xn, wg_ref[...], preferred_element_type=F32)
    for j in range(w_ref.shape[1] // PROJ_IN_COLS):
        cols = slice(j * PROJ_IN_COLS, (j + 1) * PROJ_IN_COLS)
        pm_ref[0, :, cols] = jnp.dot(xn, w_ref[:, cols], preferred_element_type=F32)


def _proj_in(s, mod_l, g, w_main, w_gate, t_ctx):
    b_sz, s_len, d = s.shape
    n_main = w_main.shape[1]
    n_gate = w_gate.shape[1]
    tm = PROJ_IN_ROWS
    assert s_len % tm == 0 and n_main % PROJ_IN_COLS == 0
    return pl.pallas_call(
        functools.partial(_proj_in_kernel, t_ctx=t_ctx),
        grid=(b_sz, s_len // tm),
        in_specs=[pl.BlockSpec((1, tm, d), lambda b, i: (b, i, 0)),
                  pl.BlockSpec((1, 1, 6 * d), lambda b, i: (b, 0, 0)),
                  pl.BlockSpec((1, 1, 6 * d), lambda b, i: (b_sz, 0, 0)),
                  pl.BlockSpec((1, d), lambda b, i: (0, 0)),
                  pl.BlockSpec((d, n_main), lambda b, i: (0, 0)),
                  pl.BlockSpec((d, n_gate), lambda b, i: (0, 0))],
        out_specs=[pl.BlockSpec((1, tm, n_main), lambda b, i: (b, i, 0)),
                   pl.BlockSpec((1, tm, n_gate), lambda b, i: (b, i, 0))],
        out_shape=[jax.ShapeDtypeStruct((b_sz, s_len, n_main), F32),
                   jax.ShapeDtypeStruct((b_sz, s_len, n_gate), F32)],
        compiler_params=_cparams(("arbitrary", "arbitrary")),
        name="proj_in",
    )(s, mod_l, mod_l, g.reshape(1, d), w_main, w_gate)


def _proj_out_kernel(ma_ref, mb2_ref, mc2_ref, s_ref, mb_ref, mc_ref, g_ref, w_ref, rw_ref, rb_ref,
                     so_ref, tok_ref, lg_ref, *, t_ctx):
    i = pl.program_id(1)
    tm, d = s_ref.shape[1], s_ref.shape[2]
    row = lax.broadcasted_iota(jnp.int32, (tm, 1), 0) + i * tm
    is_ctx = row < t_ctx
    g1 = _pick_mod(mb_ref, mc_ref, 2, d, is_ctx)
    ka, kb = ma_ref.shape[2], ma_ref.shape[2] + mb2_ref.shape[2]
    y = (jnp.dot(ma_ref[0], w_ref[0:ka, :], preferred_element_type=F32)
         + jnp.dot(mb2_ref[0], w_ref[ka:kb, :], preferred_element_type=F32)
         + jnp.dot(mc2_ref[0], w_ref[kb:, :], preferred_element_type=F32))
    s_new = s_ref[0] + g1 * y
    so_ref[0] = s_new
    sh = _pick_mod(mb_ref, mc_ref, 3, d, is_ctx)
    sc = _pick_mod(mb_ref, mc_ref, 4, d, is_ctx)
    t = _rms(s_new, g_ref[...]) * (1.0 + sc) + sh
    tok_ref[0] = t
    w = rw_ref[...]
    t_hi, w_hi = t.astype(BF16), w.astype(BF16)
    t_lo = (t - t_hi.astype(F32)).astype(BF16)
    w_lo = (w - w_hi.astype(F32)).astype(BF16)
    lg_ref[0] = (jnp.dot(t_hi, w_hi, preferred_element_type=F32) + jnp.dot(t_hi, w_lo, preferred_element_type=F32)
                 + jnp.dot(t_lo, w_hi, preferred_element_type=F32) + rb_ref[...])


def _proj_out(mix_parts, s, mod_l, g, w_out, rw, rb, t_ctx):
    b_sz, s_len, d = s.shape
    tm = ROW_TILE
    row_spec = pl.BlockSpec((1, tm, d), lambda b, i: (b, i, 0))
    part_specs = [pl.BlockSpec((1, tm, m.shape[2]), lambda b, i: (b, i, 0)) for m in mix_parts]
    return pl.pallas_call(
        functools.partial(_proj_out_kernel, t_ctx=t_ctx),
        grid=(b_sz, s_len // tm),
        in_specs=part_specs + [row_spec,
                  pl.BlockSpec((1, 1, 6 * d), lambda b, i: (b, 0, 0)),
                  pl.BlockSpec((1, 1, 6 * d), lambda b, i: (b_sz, 0, 0)),
                  pl.BlockSpec((1, d), lambda b, i: (0, 0)),
                  pl.BlockSpec((d, d), lambda b, i: (0, 0)),
                  pl.BlockSpec((d, LANES), lambda b, i: (0, 0)),
                  pl.BlockSpec((1, LANES), lambda b, i: (0, 0))],
        out_specs=[row_spec, row_spec, pl.BlockSpec((1, tm, LANES), lambda b, i: (b, i, 0))],
        out_shape=[jax.ShapeDtypeStruct((b_sz, s_len, d), F32),
                   jax.ShapeDtypeStruct((b_sz, s_len, d), F32),
                   jax.ShapeDtypeStruct((b_sz, s_len, LANES), F32)],
        compiler_params=_cparams(("arbitrary", "arbitrary")),
        name="proj_out_router",
    )(*mix_parts, s, mod_l, mod_l, g.reshape(1, d), w_out, rw, rb)


PAIR = 2 * LANES


def _regroup_perm():
    dst = np.arange(PAIR)
    src = np.where(dst < LANES, 2 * dst, 2 * (dst - LANES) + 1)
    return jnp.asarray(np.arange(PAIR)[:, None] == src[None, :], BF16)


def _regroup_bias(b_up):
    lead = b_up.shape[:-1]
    b = b_up.reshape(*lead, -1, LANES, 2)
    return jnp.swapaxes(b, -1, -2).reshape(*lead, -1)


def _moe_kernel(be_ref, nu_ref, x_ref, wu_ref, wd_ref, bu_ref, bd_ref, p_ref, y_ref, wus_ref, wds_ref):
    i = pl.program_id(0)
    d, f2 = wus_ref.shape
    rows = min(512, d)

    @pl.when(i < nu_ref[0])
    def _():
        @pl.when((i == 0) | (be_ref[i] != be_ref[jnp.maximum(i - 1, 0)]))
        def _():
            for r in range(d // rows):
                for j in range(f2 // PAIR):
                    w = wu_ref[0, r * rows:(r + 1) * rows, j * PAIR:(j + 1) * PAIR].astype(BF16)
                    wus_ref[r * rows:(r + 1) * rows, j * PAIR:(j + 1) * PAIR] = jnp.dot(
                        w, p_ref[...], preferred_element_type=F32).astype(BF16)
            wds_ref[...] = wd_ref[0].astype(BF16)

        up = jnp.dot(x_ref[...].astype(BF16), wus_ref[...], preferred_element_type=F32) + bu_ref[0]
        acts = []
        for j in range(f2 // PAIR):
            glu = jnp.minimum(up[:, j * PAIR:j * PAIR + LANES], SWIGLU_LIMIT)
            lin = jnp.clip(up[:, j * PAIR + LANES:(j + 1) * PAIR], -SWIGLU_LIMIT, SWIGLU_LIMIT)
            acts.append((glu * jax.nn.sigmoid(SWIGLU_ALPHA * glu) * (lin + 1.0)).astype(BF16))
        act = jnp.concatenate(acts, axis=1)
        y_ref[...] = _pack_bf16_pairs(jnp.dot(act, wds_ref[...], preferred_element_type=F32) + bd_ref[0])

    @pl.when(i >= nu_ref[0])
    def _():
        y_ref[...] = jnp.zeros_like(y_ref)


def _moe_blocks(blk_e, n_used, x_sorted, w_up, w_down, b_up, b_down):
    n_rows, d = x_sorted.shape
    _, _, f2 = w_up.shape
    assert f2 % PAIR == 0 and d % min(512, d) == 0
    tm = MOE_TILE
    wmap = lambda i, be, nu: (be[i], 0, 0)
    return pl.pallas_call(
        _moe_kernel,
        grid_spec=pltpu.PrefetchScalarGridSpec(
            num_scalar_prefetch=2,
            grid=(n_rows // tm,),
            in_specs=[pl.BlockSpec((tm, d), lambda i, be, nu: (i, 0)),
                      pl.BlockSpec((1, d, f2), wmap),
                      pl.BlockSpec((1, f2 // 2, d), wmap),
                      pl.BlockSpec((1, 1, f2), wmap),
                      pl.BlockSpec((1, 1, d), wmap),
                      pl.BlockSpec((PAIR, PAIR), lambda i, be, nu: (0, 0))],
            out_specs=pl.BlockSpec((tm, d // 2), lambda i, be, nu: (i, 0)),
            scratch_shapes=[pltpu.VMEM((d, f2), BF16), pltpu.VMEM((f2 // 2, d), BF16)]),
        out_shape=jax.ShapeDtypeStruct((n_rows, d // 2), jnp.uint32),
        compiler_params=pltpu.CompilerParams(dimension_semantics=("arbitrary",),
                                             vmem_limit_bytes=MOE_VMEM_LIMIT),
        name="moe_expert_blocks",
    )(blk_e, n_used, x_sorted, w_up, w_down, b_up, b_down, _regroup_perm())


def _combine_kernel(y_ref, gt_ref, s_ref, mb_ref, mc_ref, o_ref, *, t_ctx):
    i = pl.program_id(1)
    tm, d = s_ref.shape[1], s_ref.shape[2]
    row = lax.broadcasted_iota(jnp.int32, (tm, 1), 0) + i * tm
    g2 = _pick_mod(mb_ref, mc_ref, 5, d, row < t_ctx)
    gt = gt_ref[0]
    y = _unpack_bf16_pairs(y_ref[0, 0]) * gt[:, 0:1]
    for k in range(1, y_ref.shape[0]):
        y = y + _unpack_bf16_pairs(y_ref[k, 0]) * gt[:, k:k + 1]
    o_ref[0] = s_ref[0] + g2 * y


def _combine(y_as, gates, s, mod_l, t_ctx):
    b_sz, s_len, d = s.shape
    n_k = y_as.shape[0]
    tm = ROW_TILE // 3
    row_spec = pl.BlockSpec((1, tm, d), lambda b, i: (b, i, 0))
    return pl.pallas_call(
        functools.partial(_combine_kernel, t_ctx=t_ctx),
        grid=(b_sz, s_len // tm),
        in_specs=[pl.BlockSpec((n_k, 1, tm, d // 2), lambda b, i: (0, b, i, 0)),
                  pl.BlockSpec((1, tm, n_k), lambda b, i: (b, i, 0)),
                  row_spec,
                  pl.BlockSpec((1, 1, 6 * d), lambda b, i: (b, 0, 0)),
                  pl.BlockSpec((1, 1, 6 * d), lambda b, i: (b_sz, 0, 0))],
        out_specs=row_spec,
        out_shape=jax.ShapeDtypeStruct((b_sz, s_len, d), F32),
        compiler_params=_cparams(("arbitrary", "arbitrary")),
        name="moe_combine",
    )(y_as, gates, s, mod_l, mod_l)


def _moe(tok, logits, s, mod_l, t_ctx, n_e, layer, w_up, w_down, b_up, b_down):
    b_sz, s_len, d = tok.shape
    n_tok = b_sz * s_len
    tm = MOE_TILE
    top_v, top_e = lax.top_k(logits.reshape(n_tok, n_e), TOP_K)
    gates = jax.nn.softmax(top_v, axis=-1)
    n_as = n_tok * TOP_K
    onehot = jnp.sum((top_e[:, :, None] == jnp.arange(n_e)[None, None, :]).astype(jnp.int32), axis=1)
    csum = jnp.cumsum(onehot, axis=0)
    counts = csum[-1]
    padded = (counts + tm - 1) // tm * tm
    end_pad = jnp.cumsum(padded)
    start_pad = end_pad - padded
    start = jnp.cumsum(counts) - counts
    dest = jnp.take_along_axis(csum - onehot + start_pad[None, :], top_e, axis=1).astype(jnp.int32)
    n_blocks = -(-n_as // tm) + n_e
    blk_first = jnp.arange(n_blocks) * tm
    blk_e = jnp.minimum(jnp.sum(blk_first[:, None] >= end_pad[None, :], axis=1), n_e - 1).astype(jnp.int32)
    n_used = (end_pad[-1:] // tm).astype(jnp.int32)
    tok_sorted = (jnp.argsort(top_e.reshape(n_as)) // TOP_K).astype(jnp.int32)
    j = (blk_first - start_pad[blk_e])[:, None] + jnp.arange(tm)[None, :]
    src = jnp.clip(start[blk_e][:, None] + j, 0, n_as - 1)
    row_tok = jnp.where(j < counts[blk_e][:, None], tok_sorted[src], 0).reshape(n_blocks * tm)
    x_sorted = tok.reshape(n_tok, d)[row_tok]
    y = _moe_blocks(blk_e + layer * n_e, n_used, x_sorted, w_up, w_down, b_up, b_down)
    y_as = y[dest.T].reshape(TOP_K, b_sz, s_len, d // 2)
    return _combine(y_as, gates.reshape(b_sz, s_len, TOP_K), s, mod_l, t_ctx)


NA_STEP = 256
NA_KEYS = NA_ROWS * GRID_W


def _head_rms(x, w, lo):
    xx = x * x
    s0 = jnp.sum(jnp.where(lo, xx, 0.0), axis=-1, keepdims=True)
    s1 = jnp.sum(jnp.where(lo, 0.0, xx), axis=-1, keepdims=True)
    inv = lax.rsqrt(jnp.where(lo, s0, s1) * (1.0 / HEAD_DIM) + NORM_EPS)
    return x * inv * w


def _dot_nt(a, b):
    return lax.dot_general(a, b, (((1,), (1,)), ((), ())), preferred_element_type=F32)


def _na_kernel(q_ref, k_ref, v_ref, qw_ref, kw_ref, bias_ref, o_ref, kt_ref, vb_ref, s_ref, *, t_ctx, rows):
    rg = pl.program_id(2)
    s_len = k_ref.shape[1]
    lo = lax.broadcasted_iota(jnp.int32, (1, LANES), 1) < HEAD_DIM
    rows_per_step = NA_STEP // GRID_W

    @pl.when(rg == 0)
    def _():
        def knorm_t(t0, n):
            return _head_rms(k_ref[0, pl.ds(t0, n), :], kw_ref[...], lo).T.astype(BF16)

        def prep(c, carry):
            t0 = pl.multiple_of(c * NA_STEP, NA_STEP)
            kt_ref[0, :, pl.ds(t0, NA_STEP)] = knorm_t(t0, NA_STEP)
            vb_ref[pl.ds(t0, NA_STEP), :] = jnp.concatenate(
                [v_ref[0, pl.ds(t0, NA_STEP), :].astype(BF16), jnp.ones((NA_STEP, LANES), BF16)], axis=1)
            return carry
        lax.fori_loop(0, s_len // NA_STEP, prep, 0, unroll=3)

        def prep_shifted(c, carry):
            t0 = pl.multiple_of(c * LANES, LANES)
            kt_ref[1, :, pl.ds(t0, LANES)] = knorm_t(pl.multiple_of(t0 + GRID_W, GRID_W), LANES)
            return carry
        lax.fori_loop(0, (s_len - GRID_W) // LANES, prep_shifted, 0, unroll=5)

    qn = _head_rms(q_ref[0], qw_ref[...], lo) * (HEAD_DIM ** -0.5)
    q0 = jnp.where(lo, qn, 0.0).astype(BF16)
    q1 = jnp.where(lo, 0.0, qn).astype(BF16)
    kc_t = kt_ref[0, :, 0:t_ctx]
    vc = vb_ref[0:t_ctx, :]

    def finish(g, o2):
        o2 = o2[:, :LANES] * (1.0 / o2[:, LANES:])
        o = jnp.where(lo, o2[:GRID_W], o2[GRID_W:])
        o_ref[0, g * GRID_W:(g + 1) * GRID_W, :] = o.astype(o_ref.dtype)

    @pl.when(rg == 0)
    def _():
        for g in range(rows_per_step):
            q2 = jnp.concatenate([q0[g * GRID_W:(g + 1) * GRID_W], q1[g * GRID_W:(g + 1) * GRID_W]], axis=0)
            s_c = jnp.dot(q2, kc_t, preferred_element_type=F32)
            p_c = jnp.exp(s_c - jnp.max(s_c, axis=-1, keepdims=True))
            finish(g, jnp.dot(p_c.astype(BF16), vc, preferred_element_type=F32))

    @pl.when(rg > 0)
    def _():
        t0s = []
        for g in range(rows_per_step):
            r = (rg - 1) * rows_per_step + g
            row_start = jnp.clip(r - NA_ROWS // 2, 0, rows - NA_ROWS)
            dr0 = row_start - r + NA_ROWS - 1
            t0 = pl.multiple_of(t_ctx + row_start * GRID_W, GRID_W)
            odd = (t0 // GRID_W) % (LANES // GRID_W)
            kw_t = kt_ref[odd, :, pl.ds(pl.multiple_of(t0 - odd * GRID_W, LANES), NA_KEYS)]
            q2 = jnp.concatenate([q0[g * GRID_W:(g + 1) * GRID_W], q1[g * GRID_W:(g + 1) * GRID_W]], axis=0)
            s_ref[g, :, :NA_KEYS] = jnp.dot(q2, kw_t, preferred_element_type=F32) + bias_ref[dr0, 0]
            s_ref[g, :, NA_KEYS:] = jnp.dot(q2, kc_t, preferred_element_type=F32)
            t0s.append(t0)
        for g in range(rows_per_step):
            s = s_ref[g]
            p = jnp.exp(s - jnp.max(s, axis=-1, keepdims=True)).astype(BF16)
            finish(g, jnp.dot(p[:, :NA_KEYS], vb_ref[pl.ds(t0s[g], NA_KEYS), :], preferred_element_type=F32)
                   + jnp.dot(p[:, NA_KEYS:], vc, preferred_element_type=F32))


def _na_bias_tables(rpb):
    n_l, n_h, n_dr, n_dc = rpb.shape
    col = np.arange(GRID_W)
    dc = np.clip(col[None, :] - col[:, None] + NA_COLS - 1, 0, n_dc - 1)
    onehot = (dc.reshape(1, -1) == np.arange(n_dc)[:, None]).astype(np.float32)
    toe = jnp.dot(rpb.reshape(-1, n_dc), jnp.asarray(onehot), precision=lax.Precision.HIGHEST)
    toe = toe.reshape(n_l, n_h, n_dr, GRID_W, GRID_W)
    col_start = np.clip(col - NA_COLS // 2, 0, GRID_W - NA_COLS)
    col_in = (col[None, :] >= col_start[:, None]) & (col[None, :] < col_start[:, None] + NA_COLS)
    toe = jnp.where(jnp.asarray(col_in), toe, NEG_INF)
    tabs = []
    for dr0 in range(NA_ROWS):
        t = toe[:, :, dr0:dr0 + NA_ROWS].reshape(n_l, n_h // 2, 2, NA_ROWS, GRID_W, GRID_W)
        t = jnp.transpose(t, (0, 1, 2, 4, 3, 5))
        tabs.append(t.reshape(n_l, n_h // 2, 2 * GRID_W, NA_KEYS))
    return jnp.stack(tabs, axis=1)


def _na_seq_kernel(q_ref, k_ref, v_ref, qw_ref, kw_ref, bias_ref, o_ref, kt_ref, vb_ref, s_ref, kn_ref, *,
                   t_ctx, rows):
    s_len = k_ref.shape[1]
    lo = lax.broadcasted_iota(jnp.int32, (1, LANES), 1) < HEAD_DIM
    rows_per_step = NA_STEP // GRID_W

    def prep(c, carry):
        t0 = pl.multiple_of(c * NA_STEP, NA_STEP)
        kn = _head_rms(k_ref[0, pl.ds(t0, NA_STEP), :], kw_ref[...], lo).astype(BF16)
        kn_ref[pl.ds(t0, NA_STEP), :] = kn
        kt_ref[0, :, pl.ds(t0, NA_STEP)] = kn.T
        vb_ref[pl.ds(t0, NA_STEP), :] = jnp.concatenate(
            [v_ref[0, pl.ds(t0, NA_STEP), :].astype(BF16), jnp.ones((NA_STEP, LANES), BF16)], axis=1)
        return carry
    lax.fori_loop(0, s_len // NA_STEP, prep, 0, unroll=3)

    def prep_shifted(c, carry):
        t0 = pl.multiple_of(c * LANES, LANES)
        kt_ref[1, :, pl.ds(t0, LANES)] = kn_ref[pl.ds(pl.multiple_of(t0 + GRID_W, GRID_W), LANES), :].T
        return carry
    lax.fori_loop(0, (s_len - GRID_W) // LANES, prep_shifted, 0, unroll=5)

    kc_t = kt_ref[0, :, 0:t_ctx]
    vc = vb_ref[0:t_ctx, :]

    def queries(tok0):
        qn = _head_rms(q_ref[0, pl.ds(tok0, NA_STEP), :], qw_ref[...], lo) * (HEAD_DIM ** -0.5)
        q0 = jnp.where(lo, qn, 0.0).astype(BF16)
        q1 = jnp.where(lo, 0.0, qn).astype(BF16)
        return [jnp.concatenate([q0[g * GRID_W:(g + 1) * GRID_W], q1[g * GRID_W:(g + 1) * GRID_W]], axis=0)
                for g in range(rows_per_step)]

    def finish(tok0, g, o2):
        o2 = o2[:, :LANES] * (1.0 / o2[:, LANES:])
        o = jnp.where(lo, o2[:GRID_W], o2[GRID_W:])
        o_ref[0, pl.ds(tok0 + g * GRID_W, GRID_W), :] = o.astype(o_ref.dtype)

    for g, q2 in enumerate(queries(0)):
        s_c = jnp.dot(q2, kc_t, preferred_element_type=F32)
        p_c = jnp.exp(s_c - jnp.max(s_c, axis=-1, keepdims=True))
        finish(0, g, jnp.dot(p_c.astype(BF16), vc, preferred_element_type=F32))

    def row_group(rg, carry):
        tok0 = pl.multiple_of(t_ctx + rg * NA_STEP, NA_STEP)
        t0s = []
        for g, q2 in enumerate(queries(tok0)):
            r = rg * rows_per_step + g
            row_start = jnp.clip(r - NA_ROWS // 2, 0, rows - NA_ROWS)
            dr0 = row_start - r + NA_ROWS - 1
            t0 = pl.multiple_of(t_ctx + row_start * GRID_W, GRID_W)
            odd = (t0 // GRID_W) % (LANES // GRID_W)
            kw_t = kt_ref[odd, :, pl.ds(pl.multiple_of(t0 - odd * GRID_W, LANES), NA_KEYS)]
            s_ref[g, :, :NA_KEYS] = jnp.dot(q2, kw_t, preferred_element_type=F32) + bias_ref[dr0, 0]
            s_ref[g, :, NA_KEYS:] = jnp.dot(q2, kc_t, preferred_element_type=F32)
            t0s.append(t0)
        for g in range(rows_per_step):
            s = s_ref[g]
            p = jnp.exp(s - jnp.max(s, axis=-1, keepdims=True)).astype(BF16)
            finish(tok0, g, jnp.dot(p[:, :NA_KEYS], vb_ref[pl.ds(t0s[g], NA_KEYS), :], preferred_element_type=F32)
                   + jnp.dot(p[:, NA_KEYS:], vc, preferred_element_type=F32))
        return carry
    lax.fori_loop(0, rows // rows_per_step, row_group, 0)


def _na_attention(p_main, q_w, k_w, bias_tab, t_ctx, col_q, n_heads):
    b_sz, s_len, _ = p_main.shape
    n_hg = n_heads // 2
    rows = (s_len - t_ctx) // GRID_W
    rows_per_step = NA_STEP // GRID_W
    assert t_ctx == NA_STEP and s_len % NA_STEP == 0 and rows >= NA_ROWS and rows % rows_per_step == 0
    cq = col_q // LANES
    seq = lambda col: pl.BlockSpec((1, s_len, LANES), lambda b, h: (b, 0, col + h))
    w2 = lambda w: jnp.concatenate([w, w]).reshape(1, LANES)
    return pl.pallas_call(
        functools.partial(_na_seq_kernel, t_ctx=t_ctx, rows=rows),
        grid=(b_sz, n_hg),
        in_specs=[seq(cq), seq(cq + n_hg), _seq_spec(s_len, cq + 2 * n_hg),
                  pl.BlockSpec((1, LANES), lambda b, h: (0, 0)),
                  pl.BlockSpec((1, LANES), lambda b, h: (0, 0)),
                  pl.BlockSpec((NA_ROWS, 1, 2 * GRID_W, NA_KEYS), lambda b, h: (0, h, 0, 0))],
        out_specs=pl.BlockSpec((1, s_len, LANES), lambda b, h: (b, 0, h)),
        out_shape=jax.ShapeDtypeStruct((b_sz, s_len, n_heads * HEAD_DIM), BF16),
        scratch_shapes=[pltpu.VMEM((2, LANES, s_len), BF16), pltpu.VMEM((s_len, 2 * LANES), BF16),
                        pltpu.VMEM((rows_per_step, 2 * GRID_W, NA_KEYS + t_ctx), F32),
                        pltpu.VMEM((s_len, LANES), BF16)],
        compiler_params=_cparams(("arbitrary", "arbitrary")),
        name="na_attention",
    )(p_main, p_main, p_main, w2(q_w), w2(k_w), bias_tab)


RET_CHUNK_LEN = 256
RET_UNROLL = 3
MLSTM_CHUNK_LEN = 128


def _reverse_chunk(i, nc, n_ctx):
    return jnp.where(i < n_ctx, n_ctx - 1 - i, nc - 1 - i + n_ctx)


def _dot_tn(a, b):
    return lax.dot_general(a, b, (((0,), (0,)), ((), ())), preferred_element_type=F32)


def _rope_tables(t_ctx, t_len):
    pos = jnp.arange(t_len)
    row = (pos // GRID_W).astype(F32)
    col = (pos % GRID_W).astype(F32)
    n = HEAD_DIM // 4
    inv = ROPE_BASE ** (-jnp.arange(n, dtype=F32) / n)
    ar = row[:, None] * inv
    ac = col[:, None] * inv
    cos = jnp.concatenate([jnp.cos(ar), jnp.cos(ar), jnp.cos(ac), jnp.cos(ac)], axis=-1)
    sin = jnp.concatenate([-jnp.sin(ar), jnp.sin(ar), -jnp.sin(ac), jnp.sin(ac)], axis=-1)
    cos = jnp.concatenate([jnp.ones((t_ctx, HEAD_DIM), F32), cos], axis=0)
    sin = jnp.concatenate([jnp.zeros((t_ctx, HEAD_DIM), F32), sin], axis=0)
    return jnp.tile(cos, (1, 2)), jnp.tile(sin, (1, 2))


def _ret_tables(n_heads):
    L = RET_CHUNK_LEN
    pos = np.arange(L, dtype=np.float32)
    lane_head = np.arange(LANES) // HEAD_DIM
    decay = np.zeros((2, n_heads, L, L), np.float32)
    zeta = np.zeros((2, n_heads // 2, LANES, L), np.float32)
    xi = np.zeros((2, n_heads // 2, L, LANES), np.float32)
    gch = np.zeros((2, n_heads // 2, 1, LANES), np.float32)
    for d, first_exp in enumerate((5.0, 6.0)):
        e = np.float32(first_exp) + np.float32(2.0) * np.arange(n_heads, dtype=np.float32)
        lg = np.log1p(-np.exp2(-e)).astype(np.float32)
        diff = pos[:, None] - pos[None, :]
        if d == 1:
            diff = -diff
        for h in range(n_heads):
            decay[d, h] = np.where(diff >= 0, np.exp(lg[h] * np.maximum(diff, 0.0)), 0.0)
        for hp in range(n_heads // 2):
            lgl = lg[2 * hp + lane_head][None, :]
            to_end = (L - 1 - pos if d == 0 else pos)[:, None]
            zeta[d, hp] = np.exp(lgl * to_end).T
            xi[d, hp] = np.exp(lgl * (L - to_end))
            gch[d, hp] = np.exp(lgl * L)
    return tuple(jnp.asarray(a) for a in (decay, zeta, xi, gch))


def _ret_kernel(q_ref, k_ref, v_ref, g_ref, cos_ref, sin_ref, dec_ref, zeta_ref, xi_ref, gch_ref, rn_ref,
                o_ref, qr_ref, kt_ref, vb_ref, acc_ref, *, t_ctx):
    L = RET_CHUNK_LEN
    s_len = q_ref.shape[1]
    nc = s_len // L
    lane = lax.broadcasted_iota(jnp.int32, (1, LANES), 1)
    lo = lane < HEAD_DIM
    half = (lane & (HEAD_DIM // 4)) == 0
    rid = lax.broadcasted_iota(jnp.int32, (LANES, LANES), 0) < HEAD_DIM
    cid = lax.broadcasted_iota(jnp.int32, (LANES, LANES), 1) < HEAD_DIM
    same_head = rid == cid

    def rope(x, cos, sin):
        up = pltpu.roll(x, LANES - HEAD_DIM // 4, axis=1)
        dn = pltpu.roll(x, HEAD_DIM // 4, axis=1)
        return x * cos + jnp.where(half, up, dn) * sin

    def prep(c, carry):
        sl = pl.ds(pl.multiple_of(c * L, L), L)
        cos, sin = cos_ref[sl, :], sin_ref[sl, :]
        qr_ref[sl, :] = rope(q_ref[0, sl, :], cos, sin).astype(BF16)
        kt_ref[:, sl] = (rope(k_ref[0, sl, :], cos, sin) * HEAD_DIM ** -0.5).T.astype(BF16)
        vb_ref[sl, :] = v_ref[0, sl, :].astype(BF16)
        return carry
    lax.fori_loop(0, nc, prep, 0, unroll=RET_UNROLL)

    def chunk(d, c, state):
        sl = pl.ds(pl.multiple_of(c * L, L), L)
        q, kt, v = qr_ref[sl, :], kt_ref[:, sl], vb_ref[sl, :]
        inter = jnp.dot(q, state.astype(BF16), preferred_element_type=F32) * xi_ref[d, 0]
        outs = []
        for h2 in range(2):
            qm = jnp.where(lo if h2 == 0 else jnp.logical_not(lo), q, jnp.zeros_like(q))
            sd = (jnp.dot(qm, kt, preferred_element_type=F32) * dec_ref[d, h2]).astype(BF16)
            outs.append(jnp.dot(sd, v, preferred_element_type=F32))
        y = jnp.where(lo, outs[0], outs[1]) + inter
        kz_t = (kt.astype(F32) * zeta_ref[d, 0]).astype(BF16)
        state = state * gch_ref[d, 0] + jnp.where(same_head, jnp.dot(kz_t, v, preferred_element_type=F32), 0.0)
        return sl, y, state

    def fwd(c, state):
        sl, y, state = chunk(0, c, state)
        acc_ref[sl, :] = y
        return state
    lax.fori_loop(0, nc, fwd, jnp.zeros((LANES, LANES), F32), unroll=RET_UNROLL)

    def bwd(i, state):
        c = _reverse_chunk(i, nc, t_ctx // L)
        sl, y, state = chunk(1, c, state)
        y = _head_rms(y + acc_ref[sl, :], rn_ref[...], lo)
        g = g_ref[0, sl, :]
        o_ref[0, sl, :] = (y * (g * jax.nn.sigmoid(g))).astype(o_ref.dtype)
        return state
    lax.fori_loop(0, nc, bwd, jnp.zeros((LANES, LANES), F32), unroll=RET_UNROLL)


def _seq_spec(s_len, col):
    return pl.BlockSpec((1, s_len, LANES), lambda b, h: (b, 0, col + h), pipeline_mode=pl.Buffered(1))


def _retention(p_main, rope_tabs, ret_tabs, r_w, t_ctx, col_q, n_heads):
    b_sz, s_len, _ = p_main.shape
    n_hp = n_heads // 2
    L = RET_CHUNK_LEN
    assert t_ctx % L == 0 and s_len % L == 0
    c0 = col_q // LANES
    cos, sin = rope_tabs
    decay, zeta, xi, gch = ret_tabs
    const2 = pl.BlockSpec((s_len, LANES), lambda b, h: (0, 0), pipeline_mode=pl.Buffered(1))
    return pl.pallas_call(
        functools.partial(_ret_kernel, t_ctx=t_ctx),
        grid=(b_sz, n_hp),
        in_specs=[_seq_spec(s_len, c0), _seq_spec(s_len, c0 + n_hp), _seq_spec(s_len, c0 + 2 * n_hp),
                  _seq_spec(s_len, c0 + 3 * n_hp), const2, const2,
                  pl.BlockSpec((2, 2, L, L), lambda b, h: (0, h, 0, 0)),
                  pl.BlockSpec((2, 1, LANES, L), lambda b, h: (0, h, 0, 0)),
                  pl.BlockSpec((2, 1, L, LANES), lambda b, h: (0, h, 0, 0)),
                  pl.BlockSpec((2, 1, 1, LANES), lambda b, h: (0, h, 0, 0)),
                  pl.BlockSpec((1, LANES), lambda b, h: (0, 0))],
        out_specs=pl.BlockSpec((1, s_len, LANES), lambda b, h: (b, 0, h)),
        out_shape=jax.ShapeDtypeStruct((b_sz, s_len, n_heads * HEAD_DIM), BF16),
        scratch_shapes=[pltpu.VMEM((s_len, LANES), BF16), pltpu.VMEM((LANES, s_len), BF16),
                        pltpu.VMEM((s_len, LANES), BF16), pltpu.VMEM((s_len, LANES), F32)],
        compiler_params=_cparams(("arbitrary", "arbitrary")),
        name="retention",
    )(p_main, p_main, p_main, p_main, cos, sin, decay, zeta, xi, gch,
      jnp.concatenate([r_w, r_w]).reshape(1, LANES))


N_GATE_TYPES = 4


def _log_sigmoid(x):
    return jnp.minimum(x, 0.0) - jnp.log1p(jnp.exp(-jnp.abs(x)))


def _split3(x):
    hi = x.astype(BF16)
    r = x - hi.astype(F32)
    mid = r.astype(BF16)
    return hi, mid, (r - mid.astype(F32)).astype(BF16)


def _mlstm_kernel(q_ref, k_ref, v_ref, og_ref, gc_ref, gr_ref, wq_ref, wk_ref, bq_ref, bk_ref, gbc_ref, gbr_ref,
                  mn_ref, o_ref, qc_ref, kc_ref, vb_ref, acc_ref, *, t_ctx):
    L = MLSTM_CHUNK_LEN
    s_len = q_ref.shape[1]
    nc = s_len // L
    lane = lax.broadcasted_iota(jnp.int32, (1, LANES), 1)
    lo = lane < HEAD_DIM
    head_lanes = (lo, jnp.logical_not(lo))
    rid = lax.broadcasted_iota(jnp.int32, (LANES, LANES), 0) < HEAD_DIM
    cid = lax.broadcasted_iota(jnp.int32, (LANES, LANES), 1) < HEAD_DIM
    head_block = (rid & cid, jnp.logical_not(rid | cid))
    row_i = lax.broadcasted_iota(jnp.int32, (L, L), 0)
    col_i = lax.broadcasted_iota(jnp.int32, (L, L), 1)
    causal = (row_i >= col_i, row_i <= col_i)
    tri = tuple(c.astype(F32) for c in causal)
    tri_t = (tri[1], tri[0])
    sub = lax.broadcasted_iota(jnp.int32, (L, 1), 0)

    def conv(x_ref, w_ref, b_ref, t0):
        x = x_ref[0, pl.ds(t0, L), :]
        prev = x_ref[0, pl.ds(jnp.maximum(t0 - 8, 0), 8), :][7:8]
        nxt = x_ref[0, pl.ds(jnp.minimum(t0 + L, s_len - 8), 8), :][0:1]
        prev = jnp.where((t0 != 0) & (t0 != t_ctx), prev, 0.0)
        nxt = jnp.where((t0 + L != t_ctx) & (t0 + L != s_len), nxt, 0.0)
        xm = jnp.where(sub == 0, prev, pltpu.roll(x, 1, axis=0))
        xp = jnp.where(sub == L - 1, nxt, pltpu.roll(x, L - 1, axis=0))
        y = b_ref[...] + xm * w_ref[0:1, :] + x * w_ref[1:2, :] + xp * w_ref[2:3, :]
        return y * jax.nn.sigmoid(y)

    def prep(c, carry):
        t0 = pl.multiple_of(c * L, L)
        sl = pl.ds(t0, L)
        qc_ref[sl, :] = conv(q_ref, wq_ref, bq_ref, t0).astype(BF16)
        kc_ref[sl, :] = (conv(k_ref, wk_ref, bk_ref, t0) * HEAD_DIM ** -0.5).astype(BF16)
        vb_ref[sl, :] = v_ref[0, sl, :].astype(BF16)
        return carry
    lax.fori_loop(0, nc, prep, 0)

    def chunk(d, c, state):
        sl = pl.ds(pl.multiple_of(c * L, L), L)
        q, k, v = qc_ref[sl, :], kc_ref[sl, :], vb_ref[sl, :]
        g_col = gc_ref[0, sl, :] + gbc_ref[0]
        g_row = gr_ref[0, 0, :, sl] + gbr_ref[0]
        cum_col = jnp.dot(tri[d], _log_sigmoid(g_col), precision=lax.Precision.HIGHEST,
                          preferred_element_type=F32)
        cum_row = jnp.dot(_log_sigmoid(g_row), tri_t[d], precision=lax.Precision.HIGHEST,
                          preferred_element_type=F32)
        end = L - 1 if d == 0 else 0
        outs, new_state = [], []
        for h2 in range(2):
            c_st, n_st, m_st = state[h2]
            ci, cf = 2 * (2 * d) + h2, 2 * (2 * d + 1) + h2
            i_col, a_col = g_col[:, ci:ci + 1], cum_col[:, cf:cf + 1]
            i_row, a_row = g_row[ci:ci + 1, :], cum_row[cf:cf + 1, :]
            b_tot = a_row[:, end:end + 1]
            d_log = jnp.where(causal[d], a_col + (i_row - a_row), NEG_INF)
            m_intra = jnp.max(d_log, axis=-1, keepdims=True)
            qm = jnp.where(head_lanes[h2], q, jnp.zeros_like(q))
            s = _dot_nt(qm, k) * jnp.exp(d_log - m_intra)
            num_intra = jnp.dot(s.astype(BF16), v, preferred_element_type=F32)
            den_intra = jnp.sum(s, axis=-1, keepdims=True)
            inter_log = a_col + m_st
            m_q = jnp.maximum(m_intra, inter_log)
            a = jnp.exp(inter_log - m_q)
            g = jnp.exp(m_intra - m_q)
            num = a * jnp.dot(qm, c_st.astype(BF16), preferred_element_type=F32) + g * num_intra
            den = a * jnp.sum(qm.astype(F32) * n_st, axis=-1, keepdims=True) + g * den_intra
            outs.append(num / jnp.maximum(jnp.abs(den), jnp.exp(-m_q)))
            w_log = b_tot - a_col + i_col
            m_loc = jnp.max(w_log, axis=0, keepdims=True)
            ke = jnp.where(head_lanes[h2], k.astype(F32) * jnp.exp(w_log - m_loc), 0.0)
            c_loc = jnp.where(head_block[h2], _dot_tn(ke.astype(BF16), v), 0.0)
            n_loc = jnp.sum(ke, axis=0, keepdims=True)
            m_new = jnp.maximum(b_tot + m_st, m_loc)
            a_s = jnp.exp(b_tot + m_st - m_new)
            g_s = jnp.exp(m_loc - m_new)
            new_state.append((a_s * c_st + g_s * c_loc, a_s * n_st + g_s * n_loc, m_new))
        return sl, jnp.where(lo, outs[0], outs[1]), tuple(new_state)

    zero = tuple((jnp.zeros((LANES, LANES), F32), jnp.zeros((1, LANES), F32), jnp.zeros((1, 1), F32))
                 for _ in range(2))

    def fwd(c, state):
        sl, y, state = chunk(0, c, state)
        acc_ref[sl, :] = y
        return state
    lax.fori_loop(0, nc, fwd, zero)

    def bwd(i, state):
        c = _reverse_chunk(i, nc, t_ctx // L)
        sl, y, state = chunk(1, c, state)
        y = _head_rms(y + acc_ref[sl, :], mn_ref[...], lo)
        o_ref[0, sl, :] = (y * jax.nn.sigmoid(og_ref[0, sl, :])).astype(o_ref.dtype)
        return state
    lax.fori_loop(0, nc, bwd, zero)


def _mlstm_pair_kernel(q_ref, k_ref, v_ref, og_ref, gc_ref, gr_ref, wq_ref, wk_ref, bq_ref, bk_ref, gbc_ref, gbr_ref,
                       mn_ref, o_ref, qc_ref, kt_ref, vb_ref, acc_ref, st_ref, cc_ref, cr_ref, *, t_ctx):
    L = MLSTM_CHUNK_LEN
    s_len = q_ref.shape[1]
    nc = s_len // L
    lane = lax.broadcasted_iota(jnp.int32, (1, LANES), 1)
    lo = lane < HEAD_DIM
    head_lanes = (lo, jnp.logical_not(lo))
    sub_lo = lax.broadcasted_iota(jnp.int32, (LANES, 1), 0) < HEAD_DIM
    head_rows = (sub_lo, jnp.logical_not(sub_lo))
    row_i = lax.broadcasted_iota(jnp.int32, (L, L), 0)
    col_i = lax.broadcasted_iota(jnp.int32, (L, L), 1)
    causal = (row_i >= col_i, row_i <= col_i)
    tri_b = causal[0].astype(BF16)
    tri_bt = causal[1].astype(BF16)
    sub = lax.broadcasted_iota(jnp.int32, (L, 1), 0)
    gate_row = lax.broadcasted_iota(jnp.int32, (2 * N_GATE_TYPES, 1), 0)
    ones = jnp.ones((L, LANES), BF16)

    def conv(x_ref, w_ref, b_ref, t0):
        x = x_ref[0, pl.ds(t0, L), :]
        prev = x_ref[0, pl.ds(jnp.maximum(t0 - 8, 0), 8), :][7:8]
        nxt = x_ref[0, pl.ds(jnp.minimum(t0 + L, s_len - 8), 8), :][0:1]
        prev = jnp.where((t0 != 0) & (t0 != t_ctx), prev, 0.0)
        nxt = jnp.where((t0 + L != t_ctx) & (t0 + L != s_len), nxt, 0.0)
        xm = jnp.where(sub == 0, prev, pltpu.roll(x, 1, axis=0))
        xp = jnp.where(sub == L - 1, nxt, pltpu.roll(x, L - 1, axis=0))
        y = b_ref[...] + xm * w_ref[0:1, :] + x * w_ref[1:2, :] + xp * w_ref[2:3, :]
        return y * jax.nn.sigmoid(y)

    def prep(c, carry):
        t0 = pl.multiple_of(c * L, L)
        sl = pl.ds(t0, L)
        qc_ref[sl, :] = conv(q_ref, wq_ref, bq_ref, t0).astype(BF16)
        kt_ref[:, sl] = (conv(k_ref, wk_ref, bk_ref, t0) * HEAD_DIM ** -0.5).T.astype(BF16)
        vb_ref[sl, :] = v_ref[0, sl, :].astype(BF16)
        lf_col = _log_sigmoid(gc_ref[0, sl, :] + gbc_ref[0])
        lf_row = _log_sigmoid(gr_ref[0, 0, :, sl] + gbr_ref[0])
        pre_col = sum(jnp.dot(tri_b, p, preferred_element_type=F32) for p in _split3(lf_col))
        pre_row = sum(jnp.dot(p, tri_bt, preferred_element_type=F32) for p in _split3(lf_row))
        cc_ref[sl, :] = jnp.where(lane < N_GATE_TYPES, pre_col, pre_col[L - 1:L, :] - pre_col + lf_col)
        cr_ref[:, sl] = jnp.where(gate_row < N_GATE_TYPES, pre_row, pre_row[:, L - 1:L] - pre_row + lf_row)
        return carry
    lax.fori_loop(0, nc, prep, 0, unroll=2)

    def chunk(d, c, m_state):
        sl = pl.ds(pl.multiple_of(c * L, L), L)
        q, kt, v = qc_ref[sl, :], kt_ref[:, sl], vb_ref[sl, :]
        g_row = gr_ref[0, 0, :, sl] + gbr_ref[0]
        cum_col, cum_row = cc_ref[sl, :], cr_ref[:, sl]
        end = L - 1 if d == 0 else 0
        outs, new_m = [], []
        for h2 in range(2):
            m_st = m_state[h2]
            ci, cf = 2 * (2 * d) + h2, 2 * (2 * d + 1) + h2
            a_rep = jnp.broadcast_to(cum_col[:, cf:cf + 1], (L, LANES))
            i_row, a_row = g_row[ci:ci + 1, :], cum_row[cf:cf + 1, :]
            b_tot = a_row[:, end:end + 1]
            d_log = jnp.where(causal[d], a_rep + (i_row - a_row), NEG_INF)
            m_intra = jnp.broadcast_to(jnp.max(d_log, axis=-1, keepdims=True), (L, LANES))
            qm = jnp.where(head_lanes[h2], q, jnp.zeros_like(q))
            s = jnp.dot(qm, kt, preferred_element_type=F32) * jnp.exp(d_log - m_intra)
            v1 = jnp.where(head_lanes[h2], v, ones)
            intra = jnp.dot(s.astype(BF16), v1, preferred_element_type=F32)
            inter = jnp.dot(qm, st_ref[h2].astype(BF16), preferred_element_type=F32)
            inter_log = a_rep + m_st
            m_q = jnp.maximum(m_intra, inter_log)
            num_den = jnp.exp(inter_log - m_q) * inter + jnp.exp(m_intra - m_q) * intra
            den = pltpu.roll(num_den, HEAD_DIM, axis=1)
            outs.append(num_den / jnp.maximum(jnp.abs(den), jnp.exp(-m_q)))
            w_row = b_tot - a_row + i_row
            m_loc = jnp.max(w_row, axis=-1, keepdims=True)
            kts = jnp.where(head_rows[h2], kt.astype(F32) * jnp.exp(w_row - m_loc), 0.0).astype(BF16)
            loc = jnp.dot(kts, v1, preferred_element_type=F32)
            m_new = jnp.maximum(b_tot + m_st, m_loc)
            st_ref[h2] = jnp.exp(b_tot + m_st - m_new) * st_ref[h2] + jnp.exp(m_loc - m_new) * loc
            new_m.append(m_new)
        return sl, jnp.where(lo, outs[0], outs[1]), tuple(new_m)

    zero_m = (jnp.zeros((1, 1), F32), jnp.zeros((1, 1), F32))

    st_ref[...] = jnp.zeros_like(st_ref)

    def fwd(c, m_state):
        sl, y, m_state = chunk(0, c, m_state)
        acc_ref[sl, :] = y
        return m_state
    lax.fori_loop(0, nc, fwd, zero_m, unroll=3)

    st_ref[...] = jnp.zeros_like(st_ref)

    def bwd(i, m_state):
        c = _reverse_chunk(i, nc, t_ctx // L)
        sl, y, m_state = chunk(1, c, m_state)
        y = _head_rms(y + acc_ref[sl, :], mn_ref[...], lo)
        o_ref[0, sl, :] = (y * jax.nn.sigmoid(og_ref[0, sl, :])).astype(o_ref.dtype)
        return m_state
    lax.fori_loop(0, nc, bwd, zero_m, unroll=3)


def _mlstm(p_main, p_gate, conv_w, conv_b, gate_b, m_w, t_ctx, n_heads):
    b_sz, s_len, _ = p_main.shape
    n_hp = n_heads // 2
    L = MLSTM_CHUNK_LEN
    assert L == LANES and t_ctx % L == 0 and s_len % L == 0
    n_g = 2 * N_GATE_TYPES
    g_rows = jnp.transpose(p_gate.reshape(b_sz, s_len, n_hp, LANES)[..., :n_g], (0, 2, 3, 1))
    gb = jnp.transpose(gate_b.reshape(N_GATE_TYPES, n_hp, 2), (1, 0, 2)).reshape(n_hp, n_g)
    gb_col = jnp.zeros((n_hp, 1, LANES), F32).at[:, 0, :n_g].set(gb)
    gb_row = gb.reshape(n_hp, n_g, 1)
    vec = lambda col: pl.BlockSpec((1, LANES), lambda b, h: (0, col + h))
    return pl.pallas_call(
        functools.partial(_mlstm_pair_kernel, t_ctx=t_ctx),
        grid=(b_sz, n_hp),
        in_specs=[_seq_spec(s_len, 0), _seq_spec(s_len, n_hp), _seq_spec(s_len, 2 * n_hp),
                  _seq_spec(s_len, 3 * n_hp),
                  pl.BlockSpec((1, s_len, LANES), lambda b, h: (b, 0, h), pipeline_mode=pl.Buffered(1)),
                  pl.BlockSpec((1, 1, n_g, s_len), lambda b, h: (b, h, 0, 0)),
                  pl.BlockSpec((3, LANES), lambda b, h: (0, h)),
                  pl.BlockSpec((3, LANES), lambda b, h: (0, n_hp + h)),
                  vec(0), vec(n_hp),
                  pl.BlockSpec((1, 1, LANES), lambda b, h: (h, 0, 0)),
                  pl.BlockSpec((1, n_g, 1), lambda b, h: (h, 0, 0)),
                  pl.BlockSpec((1, LANES), lambda b, h: (0, 0))],
        out_specs=pl.BlockSpec((1, s_len, LANES), lambda b, h: (b, 0, h)),
        out_shape=jax.ShapeDtypeStruct((b_sz, s_len, n_heads * HEAD_DIM), BF16),
        scratch_shapes=[pltpu.VMEM((s_len, LANES), BF16), pltpu.VMEM((LANES, s_len), BF16),
                        pltpu.VMEM((s_len, LANES), BF16), pltpu.VMEM((s_len, LANES), F32),
                        pltpu.VMEM((2, LANES, LANES), F32),
                        pltpu.VMEM((s_len, LANES), F32), pltpu.VMEM((n_g, s_len), F32)],
        compiler_params=_cparams(("arbitrary", "arbitrary")),
        name="mlstm",
    )(p_main, p_main, p_main, p_main, p_gate, g_rows, conv_w, conv_w, conv_b.reshape(1, -1),
      conv_b.reshape(1, -1), gb_col, gb_row, jnp.concatenate([m_w, m_w]).reshape(1, LANES))


def _rms_norm(x, g):
    xf = x.astype(F32)
    y = xf * lax.rsqrt(jnp.mean(xf * xf, axis=-1, keepdims=True) + NORM_EPS)
    return (y * g.astype(F32)).astype(x.dtype)


def _split_heads(t, n_heads):
    return t.reshape(t.shape[0], t.shape[1], n_heads, HEAD_DIM)


def _to_bhtd(t):
    return jnp.transpose(t, (0, 2, 1, 3))


def _merge_heads(t):
    b, h, t_len, d = t.shape
    return jnp.transpose(t, (0, 2, 1, 3)).reshape(b, t_len, h * d)


def _flip_t(t):
    return jnp.flip(t, axis=2)


def _dwconv_centred(x, w, b):
    k_size = w.shape[0]
    pad = k_size // 2
    t_len = x.shape[1]
    xp = jnp.pad(x, ((0, 0), (pad, pad), (0, 0)))
    y = b
    for i in range(k_size):
        y = y + xp[:, i:i + t_len] * w[i]
    return y


def _axial_rope_tables(t_len):
    pos = jnp.arange(t_len)
    row = (pos // GRID_W).astype(F32)
    col = (pos % GRID_W).astype(F32)
    n = HEAD_DIM // 4
    inv = ROPE_BASE ** (-jnp.arange(n, dtype=F32) / n)
    ar = row[:, None] * inv
    ac = col[:, None] * inv
    return (jnp.cos(ar), jnp.sin(ar), jnp.cos(ac), jnp.sin(ac))


def _rotate_half(x, cos, sin):
    x1, x2 = jnp.split(x, 2, axis=-1)
    return jnp.concatenate([x1 * cos - x2 * sin, x2 * cos + x1 * sin], axis=-1)


def _axial_rope(x, cos_r, sin_r, cos_c, sin_c):
    xr, xc = jnp.split(x, 2, axis=-1)
    return jnp.concatenate([_rotate_half(xr, cos_r, sin_r), _rotate_half(xc, cos_c, sin_c)], axis=-1)


def _mlstm_chunkwise(q, k, v, log_i, log_f, state0):
    b_sz, h_sz, t_len, d = q.shape
    L = MLSTM_CHUNK
    nc = t_len // L
    qc = q.reshape(b_sz, h_sz, nc, L, d)
    kc = k.reshape(b_sz, h_sz, nc, L, d)
    vc = v.reshape(b_sz, h_sz, nc, L, d)
    li = log_i.reshape(b_sz, h_sz, nc, L)
    bcum = jnp.cumsum(log_f.reshape(b_sz, h_sz, nc, L), axis=-1)
    b_tot = bcum[..., -1]
    tri = jnp.tril(jnp.ones((L, L), dtype=bool))
    d_log = jnp.where(tri, bcum[..., :, None] - bcum[..., None, :] + li[..., None, :], NEG_INF)
    m_intra = jnp.max(d_log, axis=-1)
    s = jnp.einsum('bhcjd,bhcld->bhcjl', qc, kc).astype(F32) * jnp.exp(d_log - m_intra[..., None])
    num_intra = jnp.einsum('bhcjl,bhcld->bhcjd', s, vc)
    den_intra = jnp.sum(s, axis=-1)
    w_log = b_tot[..., None] - bcum + li
    m_loc = jnp.max(w_log, axis=-1)
    e = jnp.exp(w_log - m_loc[..., None])
    c_loc = jnp.einsum('bhcl,bhcld,bhcle->bhcde', e, kc, vc)
    n_loc = jnp.einsum('bhcl,bhcld->bhcd', e, kc)

    def step(carry, inp):
        c_st, n_st, m_st = carry
        cl, nl, ml, bt = inp
        m_new = jnp.maximum(bt + m_st, ml)
        a = jnp.exp(bt + m_st - m_new)
        g = jnp.exp(ml - m_new)
        new = (a[..., None, None] * c_st + g[..., None, None] * cl, a[..., None] * n_st + g[..., None] * nl, m_new)
        return new, carry

    xs = (jnp.moveaxis(c_loc, 2, 0), jnp.moveaxis(n_loc, 2, 0), jnp.moveaxis(m_loc, 2, 0), jnp.moveaxis(b_tot, 2, 0))
    final, prev = lax.scan(step, state0, xs)
    c_prev = jnp.moveaxis(prev[0], 0, 2)
    n_prev = jnp.moveaxis(prev[1], 0, 2)
    m_prev = jnp.moveaxis(prev[2], 0, 2)
    inter_log = bcum + m_prev[..., None]
    m_q = jnp.maximum(m_intra, inter_log)
    a = jnp.exp(inter_log - m_q)
    g = jnp.exp(m_intra - m_q)
    num = a[..., None] * jnp.einsum('bhcjd,bhcde->bhcje', qc, c_prev) + g[..., None] * num_intra
    den = a * jnp.einsum('bhcjd,bhcd->bhcj', qc, n_prev) + g * den_intra
    h = num / jnp.maximum(jnp.abs(den), jnp.exp(-m_q))[..., None]
    return h.reshape(b_sz, h_sz, t_len, d), final


def _retention_log_decay(first_exp, n_heads):
    e = first_exp + 2.0 * jnp.arange(n_heads, dtype=F32)
    return jnp.log1p(-jnp.exp2(-e))


def _retention_chunkwise(q, k, v, log_gamma, state0):
    b_sz, h_sz, t_len, d = q.shape
    L = RET_CHUNK
    nc = t_len // L
    qc = q.reshape(b_sz, h_sz, nc, L, d)
    kc = k.reshape(b_sz, h_sz, nc, L, d)
    vc = v.reshape(b_sz, h_sz, nc, L, d)
    pos = jnp.arange(L, dtype=F32)
    diff = pos[:, None] - pos[None, :]
    decay = jnp.where(diff >= 0, jnp.exp(log_gamma[:, None, None] * jnp.maximum(diff, 0.0)), 0.0)
    s = jnp.einsum('bhcjd,bhcld->bhcjl', qc, kc).astype(F32) * decay[:, None]
    intra = jnp.einsum('bhcjl,bhcld->bhcjd', s, vc)
    zeta = jnp.exp(log_gamma[:, None] * (L - 1 - pos))
    s_loc = jnp.einsum('hl,bhcld,bhcle->bhcde', zeta, kc, vc)
    g_chunk = jnp.exp(log_gamma * L)[:, None, None]

    def step(r, sl):
        return g_chunk * r + sl, r

    final, r_prev = lax.scan(step, state0, jnp.moveaxis(s_loc, 2, 0))
    r_prev = jnp.moveaxis(r_prev, 0, 2)
    xi = jnp.exp(log_gamma[:, None] * (pos + 1.0))
    inter = jnp.einsum('bhcjd,bhcde->bhcje', qc, r_prev) * xi[:, None, :, None]
    return (intra + inter).reshape(b_sz, h_sz, t_len, d), final


def _na_window(rows):
    wr = min(NA_ROWS, rows)
    r = jnp.arange(rows)
    col = jnp.arange(GRID_W)
    row_idx = jnp.clip(r - wr // 2, 0, rows - wr)[:, None] + jnp.arange(wr)[None, :]
    col_start = jnp.clip(col - NA_COLS // 2, 0, GRID_W - NA_COLS)
    col_in = (col[None, :] >= col_start[:, None]) & (col[None, :] < col_start[:, None] + NA_COLS)
    dr = row_idx - r[:, None] + NA_ROWS - 1
    dc = jnp.clip(col[None, :] - col[:, None] + NA_COLS - 1, 0, 2 * NA_COLS - 2)
    return (row_idx, col_in, dr[:, None, :, None], dc[None, :, None, :])


def _na_latent(q, k, v, k_ctx, v_ctx, row_idx, col_in, bias):
    t_len, h_sz, d = q.shape
    rows, wr = row_idx.shape
    scale = d ** -0.5
    qg = q.reshape(rows, GRID_W, h_sz, d)
    kb = k.reshape(rows, GRID_W, h_sz, d)[row_idx]
    vb = v.reshape(rows, GRID_W, h_sz, d)[row_idx]
    s_loc = jnp.einsum('rqhd,rjwhd->hrqjw', qg, kb).astype(F32) * scale + bias
    s_loc = jnp.where(col_in[:, None, :], s_loc, NEG_INF)
    s_ctx = jnp.einsum('rqhd,chd->hrqc', qg, k_ctx).astype(F32) * scale
    n_loc = wr * GRID_W
    s = jnp.concatenate([s_loc.reshape(h_sz, rows, GRID_W, n_loc), s_ctx], axis=-1)
    p = jax.nn.softmax(s, axis=-1).astype(v.dtype)
    p_loc = p[..., :n_loc].reshape(h_sz, rows, GRID_W, wr, GRID_W)
    o = jnp.einsum('hrqjw,rjwhd->rqhd', p_loc, vb) + jnp.einsum('hrqc,chd->rqhd', p[..., n_loc:], v_ctx)
    return o.reshape(t_len, h_sz * d)


def _ctx_attention(q, k, v):
    s = jnp.einsum('bqhd,bkhd->bhqk', q, k).astype(F32) * HEAD_DIM ** -0.5
    p = jax.nn.softmax(s, axis=-1).astype(v.dtype)
    o = jnp.einsum('bhqk,bkhd->bqhd', p, v)
    return o.reshape(o.shape[0], o.shape[1], -1)


def _hybrid_mixer(xp, cp, rope, na_win, conv_w, conv_b, gate_b, m_norm, q_norm, k_norm, rpb, r_norm, dims):
    h_m, h_na, h_r = dims
    b_sz = xp[0].shape[0]

    def mlstm_prep(p):
        qk = jax.nn.silu(_dwconv_centred(p[0], conv_w, conv_b))
        q, k = jnp.split(qk, 2, axis=-1)
        g = (p[3] + gate_b).astype(F32)
        g = jnp.transpose(g.reshape(b_sz, -1, 4, h_m), (2, 0, 3, 1))
        return (_to_bhtd(_split_heads(q, h_m)), _to_bhtd(_split_heads(k, h_m)) * HEAD_DIM ** -0.5,
                _to_bhtd(_split_heads(p[1], h_m)), g[0], jax.nn.log_sigmoid(g[1]), g[2], jax.nn.log_sigmoid(g[3]))

    qx, kx, vx, ix_f, lfx_f, ix_b, lfx_b = mlstm_prep(xp)
    qc, kc, vc, ic_f, lfc_f, ic_b, lfc_b = mlstm_prep(cp)
    zero_m = (jnp.zeros((b_sz, h_m, HEAD_DIM, HEAD_DIM), F32),
              jnp.zeros((b_sz, h_m, HEAD_DIM), F32), jnp.zeros((b_sz, h_m), F32))
    hc_f, st_f = _mlstm_chunkwise(qc, kc, vc, ic_f, lfc_f, zero_m)
    hc_b, st_b = _mlstm_chunkwise(_flip_t(qc), _flip_t(kc), _flip_t(vc), _flip_t(ic_b), _flip_t(lfc_b), zero_m)
    hx_f, _ = _mlstm_chunkwise(qx, kx, vx, ix_f, lfx_f, st_f)
    hx_b, _ = _mlstm_chunkwise(_flip_t(qx), _flip_t(kx), _flip_t(vx), _flip_t(ix_b), _flip_t(lfx_b), st_b)

    def mlstm_out(h, o):
        return _merge_heads(_rms_norm(h, m_norm)).astype(o.dtype) * jax.nn.sigmoid(o)

    a_x = mlstm_out(hx_f + _flip_t(hx_b), xp[2])

    def na_prep(p):
        return (_rms_norm(_split_heads(p[4], h_na), q_norm), _rms_norm(_split_heads(p[5], h_na), k_norm),
                _split_heads(p[6], h_na))

    nqx, nkx, nvx = na_prep(xp)
    nqc, nkc, nvc = na_prep(cp)
    row_idx, col_in, dr, dc = na_win
    bias = rpb[:, dr, dc].astype(F32)
    b_x = lax.map(lambda a: _na_latent(a[0], a[1], a[2], a[3], a[4], row_idx, col_in, bias),
                  (nqx, nkx, nvx, nkc, nvc))

    lg_f = _retention_log_decay(5.0, h_r)
    lg_b = _retention_log_decay(6.0, h_r)
    rqx = _axial_rope(_to_bhtd(_split_heads(xp[7], h_r)), *rope)
    rkx = _axial_rope(_to_bhtd(_split_heads(xp[8], h_r)), *rope) * HEAD_DIM ** -0.5
    rvx = _to_bhtd(_split_heads(xp[9], h_r))
    rqc = _to_bhtd(_split_heads(cp[7], h_r))
    rkc = _to_bhtd(_split_heads(cp[8], h_r)) * HEAD_DIM ** -0.5
    rvc = _to_bhtd(_split_heads(cp[9], h_r))
    zero_r = jnp.zeros((b_sz, h_r, HEAD_DIM, HEAD_DIM), F32)
    rc_f, rs_f = _retention_chunkwise(rqc, rkc, rvc, lg_f, zero_r)
    rc_b, rs_b = _retention_chunkwise(_flip_t(rqc), _flip_t(rkc), _flip_t(rvc), lg_b, zero_r)
    rx_f, _ = _retention_chunkwise(rqx, rkx, rvx, lg_f, rs_f)
    rx_b, _ = _retention_chunkwise(_flip_t(rqx), _flip_t(rkx), _flip_t(rvx), lg_b, rs_b)

    def ret_out(h, g):
        return _merge_heads(_rms_norm(h, r_norm)).astype(g.dtype) * jax.nn.silu(g)

    c_x = ret_out(rx_f + _flip_t(rx_b), xp[10])
    mix_x = jnp.concatenate([a_x, b_x, c_x], axis=-1)
    a_c = mlstm_out(hc_f + _flip_t(hc_b), cp[2])
    b_c = _ctx_attention(nqc, nkc, nvc)
    c_c = ret_out(rc_f + _flip_t(rc_b), cp[10])
    return mix_x, jnp.concatenate([a_c, b_c, c_c], axis=-1)


def kernel(x, c, ctx, c_ctx, w_mod, b_mod, norm_mix, norm_ffn, w_in, w_out, mlstm_conv_w, mlstm_conv_b,
           mlstm_gate_b, mlstm_norm, na_q_norm, na_k_norm, na_rpb, ret_norm, router_w, router_b,
           expert_w_up, expert_b_up, expert_w_down, expert_b_down):
    b_sz, t_len, d = x.shape
    t_ctx = ctx.shape[1]
    depth = w_in.shape[0]
    n_e = router_w.shape[2]
    d_mix = w_out.shape[1]
    h_m = d_mix // (4 * HEAD_DIM)
    h_na = d_mix // (2 * HEAD_DIM)
    h_r = d_mix // (4 * HEAD_DIM)
    d_m, d_na, d_r = h_m * HEAD_DIM, h_na * HEAD_DIM, h_r * HEAD_DIM
    n_gate = 4 * h_m
    assert b_sz + 1 <= MOD_ROWS and n_e <= LANES and n_gate <= LANES
    s_len = t_ctx + t_len
    assert s_len % ROW_TILE == 0

    s = jnp.concatenate([ctx, x], axis=1)
    cc = jnp.zeros((MOD_ROWS, d), F32).at[:b_sz].set(c).at[b_sz].set(c_ctx)
    mods = _modulation(cc, w_mod, b_mod).reshape(depth, MOD_ROWS, 1, 6 * d)

    g0 = 4 * d_m
    col_na = g0
    col_ret = col_na + 3 * d_na
    n_hp = h_m // 2
    gate_src = np.array([[g0 + t * h_m + 2 * hp + h2 for t in range(N_GATE_TYPES) for h2 in range(2)]
                         for hp in range(n_hp)])
    rope_tabs = _rope_tables(t_ctx, t_len)
    ret_tabs = _ret_tables(h_r)
    na_tabs = _na_bias_tables(na_rpb)
    f2 = expert_w_up.shape[3]
    w_up_all = expert_w_up.reshape(depth * n_e, d, f2)
    w_down_all = expert_w_down.reshape(depth * n_e, f2 // 2, d)
    b_up_all = _regroup_bias(expert_b_up).reshape(depth * n_e, 1, f2)
    b_down_all = expert_b_down.reshape(depth * n_e, 1, d)

    for l in range(depth):
        w_main = jnp.concatenate([w_in[l, :, :g0], w_in[l, :, g0 + n_gate:]], axis=1).astype(BF16)
        w_gate = jnp.zeros((d, n_hp, LANES), BF16).at[:, :, :gate_src.shape[1]].set(
            w_in[l][:, gate_src].astype(BF16)).reshape(d, n_hp * LANES)
        p_main, p_gate = _proj_in(s, mods[l], norm_mix[l], w_main, w_gate, t_ctx)

        mix = [_mlstm(p_main, p_gate, mlstm_conv_w[l], mlstm_conv_b[l], mlstm_gate_b[l], mlstm_norm[l],
                      t_ctx, h_m),
               _na_attention(p_main, na_q_norm[l], na_k_norm[l], na_tabs[l], t_ctx, col_na, h_na),
               _retention(p_main, rope_tabs, ret_tabs, ret_norm[l], t_ctx, col_ret, h_r)]

        rw = jnp.zeros((d, LANES), F32).at[:, :n_e].set(router_w[l])
        rb = jnp.zeros((1, LANES), F32).at[0, :n_e].set(router_b[l])
        s, tok, logits = _proj_out(mix, s, mods[l], norm_ffn[l], w_out[l].astype(BF16), rw, rb, t_ctx)

        s = _moe(tok, logits[..., :n_e], s, mods[l], t_ctx, n_e, l, w_up_all, w_down_all, b_up_all, b_down_all)
    return s[:, t_ctx:]
```

```python
import functools

import jax
import jax.numpy as jnp
import numpy as np
from jax import lax
from jax.experimental import pallas as pl
from jax.experimental.pallas import tpu as pltpu

F32 = jnp.float32
BF16 = jnp.bfloat16

GRID_W = 64
HEAD_DIM = 64
MLSTM_CHUNK = 64
RET_CHUNK = 64
NA_ROWS = 8
NA_COLS = 16
ROPE_BASE = 10000.0
TOP_K = 4
SWIGLU_ALPHA = 1.702
SWIGLU_LIMIT = 7.0
NORM_EPS = 1e-6
NEG_INF = -1e30

LANES = 128
VMEM_LIMIT = 48 * 1024 * 1024
MOD_ROWS = 8
ROW_TILE = 768
MOE_TILE = 512
MOE_VMEM_LIMIT = 56 * 1024 * 1024


def _cparams(sem):
    return pltpu.CompilerParams(dimension_semantics=sem, vmem_limit_bytes=VMEM_LIMIT)


def _mod_kernel(cc_ref, w_ref, b_ref, o_ref):
    cc = cc_ref[...]
    a = cc * jax.nn.sigmoid(cc)
    o_ref[0] = jnp.dot(a, w_ref[0], precision=lax.Precision.HIGHEST,
                       preferred_element_type=F32) + b_ref[0]


def _modulation(cc, w_mod, b_mod):
    n_l, d, d6 = w_mod.shape
    tn = d6 // 4
    return pl.pallas_call(
        _mod_kernel,
        grid=(n_l, d6 // tn),
        in_specs=[pl.BlockSpec((MOD_ROWS, d), lambda l, j: (0, 0)),
                  pl.BlockSpec((1, d, tn), lambda l, j: (l, 0, j)),
                  pl.BlockSpec((1, 1, tn), lambda l, j: (l, 0, j))],
        out_specs=pl.BlockSpec((1, MOD_ROWS, tn), lambda l, j: (l, 0, j)),
        out_shape=jax.ShapeDtypeStruct((n_l, MOD_ROWS, d6), F32),
        compiler_params=_cparams(("arbitrary", "arbitrary")),
        name="adaln_modulation",
    )(cc, w_mod, b_mod.reshape(n_l, 1, d6))


def _pick_mod(mb_ref, mc_ref, k, d, is_ctx):
    vb = mb_ref[0, :, k * d:(k + 1) * d]
    vc = mc_ref[0, :, k * d:(k + 1) * d]
    return jnp.where(is_ctx, vc, vb)


def _rms(x, g):
    return x * lax.rsqrt(jnp.mean(x * x, axis=-1, keepdims=True) + NORM_EPS) * g


def _pack_bf16_pairs(x):
    m = x.shape[1] // 2
    hi = lax.bitcast_convert_type(x[:, :m].astype(BF16).astype(F32), jnp.uint32)
    lo = lax.bitcast_convert_type(x[:, m:].astype(BF16).astype(F32), jnp.uint32)
    return hi | (lo >> 16)


def _unpack_bf16_pairs(p):
    hi = lax.bitcast_convert_type(p & jnp.uint32(0xFFFF0000), F32)
    lo = lax.bitcast_convert_type(p << 16, F32)
    return jnp.concatenate([hi, lo], axis=1)


PROJ_IN_ROWS = 384
PROJ_IN_COLS = 512


def _ctx_rows(tm, t_ctx):
    return lax.broadcasted_iota(jnp.int32, (tm, 1), 0) + pl.program_id(1) * tm < t_ctx


def _project(s, is_ctx, mb_ref, mc_ref, g_ref, w_ref, wg_ref, pm_ref, pg_ref):
    d = s.shape[1]
    sh = _pick_mod(mb_ref, mc_ref, 0, d, is_ctx)
    sc = _pick_mod(mb_ref, mc_ref, 1, d, is_ctx)
    xn = (_rms(s, g_ref[...]) * (1.0 + sc) + sh).astype(BF16)
    pg_ref[0] = jnp.dot(xn, wg_ref[...], preferred_element_type=F32)
    for j in range(w_ref.shape[1] // PROJ_IN_COLS):
        cols = slice(j * PROJ_IN_COLS, (j + 1) * PROJ_IN_COLS)
        pm_ref[0, :, cols] = jnp.dot(xn, w_ref[:, cols], preferred_element_type=F32)


def _proj_in_kernel(s_ref, mb_ref, mc_ref, g_ref, w_ref, wg_ref, pm_ref, pg_ref, *, t_ctx):
    _project(s_ref[0], _ctx_rows(s_ref.shape[1], t_ctx), mb_ref, mc_ref, g_ref, w_ref, wg_ref, pm_ref, pg_ref)


def _combine_rows(y_ref, gt_ref, s_ref, mb_ref, mc_ref, is_ctx):
    g2 = _pick_mod(mb_ref, mc_ref, 5, s_ref.shape[2], is_ctx)
    gt = gt_ref[0]
    y = _unpack_bf16_pairs(y_ref[0, 0]) * gt[:, 0:1]
    for k in range(1, y_ref.shape[0]):
        y = y + _unpack_bf16_pairs(y_ref[k, 0]) * gt[:, k:k + 1]
    return s_ref[0] + g2 * y


def _combine_proj_in_kernel(y_ref, gt_ref, s_ref, pmb_ref, pmc_ref, mb_ref, mc_ref, g_ref, w_ref, wg_ref,
                            so_ref, pm_ref, pg_ref, *, t_ctx):
    is_ctx = _ctx_rows(s_ref.shape[1], t_ctx)
    s_new = _combine_rows(y_ref, gt_ref, s_ref, pmb_ref, pmc_ref, is_ctx)
    so_ref[0] = s_new
    _project(s_new, is_ctx, mb_ref, mc_ref, g_ref, w_ref, wg_ref, pm_ref, pg_ref)


def _proj_in(s, mod_l, g, w_main, w_gate, t_ctx, combine=None):
    b_sz, s_len, d = s.shape
    n_main = w_main.shape[1]
    n_gate = w_gate.shape[1]
    tm = PROJ_IN_ROWS
    assert s_len % tm == 0 and n_main % PROJ_IN_COLS == 0
    row_spec = pl.BlockSpec((1, tm, d), lambda b, i: (b, i, 0))
    mod_specs = [pl.BlockSpec((1, 1, 6 * d), lambda b, i: (b, 0, 0)),
                 pl.BlockSpec((1, 1, 6 * d), lambda b, i: (b_sz, 0, 0))]
    w_specs = [pl.BlockSpec((1, d), lambda b, i: (0, 0)),
               pl.BlockSpec((d, n_main), lambda b, i: (0, 0)),
               pl.BlockSpec((d, n_gate), lambda b, i: (0, 0))]
    out_specs = [pl.BlockSpec((1, tm, n_main), lambda b, i: (b, i, 0)),
                 pl.BlockSpec((1, tm, n_gate), lambda b, i: (b, i, 0))]
    out_shape = [jax.ShapeDtypeStruct((b_sz, s_len, n_main), F32),
                 jax.ShapeDtypeStruct((b_sz, s_len, n_gate), F32)]
    if combine is None:
        return pl.pallas_call(
            functools.partial(_proj_in_kernel, t_ctx=t_ctx),
            grid=(b_sz, s_len // tm),
            in_specs=[row_spec] + mod_specs + w_specs,
            out_specs=out_specs, out_shape=out_shape,
            compiler_params=_cparams(("arbitrary", "arbitrary")),
            name="proj_in",
        )(s, mod_l, mod_l, g.reshape(1, d), w_main, w_gate)
    y_as, gates, mod_prev = combine
    n_k = y_as.shape[0]
    return pl.pallas_call(
        functools.partial(_combine_proj_in_kernel, t_ctx=t_ctx),
        grid=(b_sz, s_len // tm),
        in_specs=[pl.BlockSpec((n_k, 1, tm, d // 2), lambda b, i: (0, b, i, 0)),
                  pl.BlockSpec((1, tm, n_k), lambda b, i: (b, i, 0)),
                  row_spec] + mod_specs + mod_specs + w_specs,
        out_specs=[row_spec] + out_specs,
        out_shape=[jax.ShapeDtypeStruct((b_sz, s_len, d), F32)] + out_shape,
        compiler_params=_cparams(("arbitrary", "arbitrary")),
        name="combine_proj_in",
    )(y_as, gates, s, mod_prev, mod_prev, mod_l, mod_l, g.reshape(1, d), w_main, w_gate)


def _proj_out_kernel(ma_ref, mb2_ref, mc2_ref, s_ref, mb_ref, mc_ref, g_ref, w_ref, rw_ref, rb_ref,
                     so_ref, tok_ref, lg_ref, *, t_ctx):
    i = pl.program_id(1)
    tm, d = s_ref.shape[1], s_ref.shape[2]
    row = lax.broadcasted_iota(jnp.int32, (tm, 1), 0) + i * tm
    is_ctx = row < t_ctx
    g1 = _pick_mod(mb_ref, mc_ref, 2, d, is_ctx)
    ka, kb = ma_ref.shape[2], ma_ref.shape[2] + mb2_ref.shape[2]
    y = (jnp.dot(ma_ref[0], w_ref[0:ka, :], preferred_element_type=F32)
         + jnp.dot(mb2_ref[0], w_ref[ka:kb, :], preferred_element_type=F32)
         + jnp.dot(mc2_ref[0], w_ref[kb:, :], preferred_element_type=F32))
    s_new = s_ref[0] + g1 * y
    so_ref[0] = s_new
    sh = _pick_mod(mb_ref, mc_ref, 3, d, is_ctx)
    sc = _pick_mod(mb_ref, mc_ref, 4, d, is_ctx)
    t = _rms(s_new, g_ref[...]) * (1.0 + sc) + sh
    tok_ref[0] = t
    w = rw_ref[...]
    t_hi, w_hi = t.astype(BF16), w.astype(BF16)
    t_lo = (t - t_hi.astype(F32)).astype(BF16)
    w_lo = (w - w_hi.astype(F32)).astype(BF16)
    lg_ref[0] = (jnp.dot(t_hi, w_hi, preferred_element_type=F32) + jnp.dot(t_hi, w_lo, preferred_element_type=F32)
                 + jnp.dot(t_lo, w_hi, preferred_element_type=F32) + rb_ref[...])


def _proj_out(mix_parts, s, mod_l, g, w_out, rw, rb, t_ctx):
    b_sz, s_len, d = s.shape
    tm = ROW_TILE
    row_spec = pl.BlockSpec((1, tm, d), lambda b, i: (b, i, 0))
    part_specs = [pl.BlockSpec((1, tm, m.shape[2]), lambda b, i: (b, i, 0)) for m in mix_parts]
    return pl.pallas_call(
        functools.partial(_proj_out_kernel, t_ctx=t_ctx),
        grid=(b_sz, s_len // tm),
        in_specs=part_specs + [row_spec,
                  pl.BlockSpec((1, 1, 6 * d), lambda b, i: (b, 0, 0)),
                  pl.BlockSpec((1, 1, 6 * d), lambda b, i: (b_sz, 0, 0)),
                  pl.BlockSpec((1, d), lambda b, i: (0, 0)),
                  pl.BlockSpec((d, d), lambda b, i: (0, 0)),
                  pl.BlockSpec((d, LANES), lambda b, i: (0, 0)),
                  pl.BlockSpec((1, LANES), lambda b, i: (0, 0))],
        out_specs=[row_spec, row_spec, pl.BlockSpec((1, tm, LANES), lambda b, i: (b, i, 0))],
        out_shape=[jax.ShapeDtypeStruct((b_sz, s_len, d), F32),
                   jax.ShapeDtypeStruct((b_sz, s_len, d), F32),
                   jax.ShapeDtypeStruct((b_sz, s_len, LANES), F32)],
        compiler_params=_cparams(("arbitrary", "arbitrary")),
        name="proj_out_router",
    )(*mix_parts, s, mod_l, mod_l, g.reshape(1, d), w_out, rw, rb)


PAIR = 2 * LANES


def _regroup_perm():
    dst = np.arange(PAIR)
    src = np.where(dst < LANES, 2 * dst, 2 * (dst - LANES) + 1)
    return jnp.asarray(np.arange(PAIR)[:, None] == src[None, :], BF16)


def _regroup_bias(b_up):
    lead = b_up.shape[:-1]
    b = b_up.reshape(*lead, -1, LANES, 2)
    return jnp.swapaxes(b, -1, -2).reshape(*lead, -1)


def _moe_kernel(be_ref, nu_ref, x_ref, wu_ref, wd_ref, bu_ref, bd_ref, p_ref, y_ref, wus_ref, wds_ref):
    i = pl.program_id(0)
    d, f2 = wus_ref.shape
    rows = min(512, d)

    @pl.when(i < nu_ref[0])
    def _():
        @pl.when((i == 0) | (be_ref[i] != be_ref[jnp.maximum(i - 1, 0)]))
        def _():
            for r in range(d // rows):
                for j in range(f2 // PAIR):
                    w = wu_ref[0, r * rows:(r + 1) * rows, j * PAIR:(j + 1) * PAIR].astype(BF16)
                    wus_ref[r * rows:(r + 1) * rows, j * PAIR:(j + 1) * PAIR] = jnp.dot(
                        w, p_ref[...], preferred_element_type=F32).astype(BF16)
            wds_ref[...] = wd_ref[0].astype(BF16)

        up = jnp.dot(x_ref[...].astype(BF16), wus_ref[...], preferred_element_type=F32) + bu_ref[0]
        acts = []
        for j in range(f2 // PAIR):
            glu = jnp.minimum(up[:, j * PAIR:j * PAIR + LANES], SWIGLU_LIMIT)
            lin = jnp.clip(up[:, j * PAIR + LANES:(j + 1) * PAIR], -SWIGLU_LIMIT, SWIGLU_LIMIT)
            acts.append((glu * jax.nn.sigmoid(SWIGLU_ALPHA * glu) * (lin + 1.0)).astype(BF16))
        act = jnp.concatenate(acts, axis=1)
        y_ref[...] = _pack_bf16_pairs(jnp.dot(act, wds_ref[...], preferred_element_type=F32) + bd_ref[0])

    @pl.when(i >= nu_ref[0])
    def _():
        y_ref[...] = jnp.zeros_like(y_ref)


def _moe_blocks(blk_e, n_used, x_sorted, w_up, w_down, b_up, b_down):
    n_rows, d = x_sorted.shape
    _, _, f2 = w_up.shape
    assert f2 % PAIR == 0 and d % min(512, d) == 0
    tm = MOE_TILE
    wmap = lambda i, be, nu: (be[i], 0, 0)
    return pl.pallas_call(
        _moe_kernel,
        grid_spec=pltpu.PrefetchScalarGridSpec(
            num_scalar_prefetch=2,
            grid=(n_rows // tm,),
            in_specs=[pl.BlockSpec((tm, d), lambda i, be, nu: (i, 0)),
                      pl.BlockSpec((1, d, f2), wmap),
                      pl.BlockSpec((1, f2 // 2, d), wmap),
                      pl.BlockSpec((1, 1, f2), wmap),
                      pl.BlockSpec((1, 1, d), wmap),
                      pl.BlockSpec((PAIR, PAIR), lambda i, be, nu: (0, 0))],
            out_specs=pl.BlockSpec((tm, d // 2), lambda i, be, nu: (i, 0)),
            scratch_shapes=[pltpu.VMEM((d, f2), BF16), pltpu.VMEM((f2 // 2, d), BF16)]),
        out_shape=jax.ShapeDtypeStruct((n_rows, d // 2), jnp.uint32),
        compiler_params=pltpu.CompilerParams(dimension_semantics=("arbitrary",),
                                             vmem_limit_bytes=MOE_VMEM_LIMIT),
        name="moe_expert_blocks",
    )(blk_e, n_used, x_sorted, w_up, w_down, b_up, b_down, _regroup_perm())


def _combine_kernel(y_ref, gt_ref, s_ref, mb_ref, mc_ref, o_ref, *, t_ctx):
    o_ref[0] = _combine_rows(y_ref, gt_ref, s_ref, mb_ref, mc_ref, _ctx_rows(s_ref.shape[1], t_ctx))


def _combine(y_as, gates, s, mod_l, t_ctx):
    b_sz, s_len, d = s.shape
    n_k = y_as.shape[0]
    tm = ROW_TILE // 3
    row_spec = pl.BlockSpec((1, tm, d), lambda b, i: (b, i, 0))
    return pl.pallas_call(
        functools.partial(_combine_kernel, t_ctx=t_ctx),
        grid=(b_sz, s_len // tm),
        in_specs=[pl.BlockSpec((n_k, 1, tm, d // 2), lambda b, i: (0, b, i, 0)),
                  pl.BlockSpec((1, tm, n_k), lambda b, i: (b, i, 0)),
                  row_spec,
                  pl.BlockSpec((1, 1, 6 * d), lambda b, i: (b, 0, 0)),
                  pl.BlockSpec((1, 1, 6 * d), lambda b, i: (b_sz, 0, 0))],
        out_specs=row_spec,
        out_shape=jax.ShapeDtypeStruct((b_sz, s_len, d), F32),
        compiler_params=_cparams(("arbitrary", "arbitrary")),
        name="moe_combine",
    )(y_as, gates, s, mod_l, mod_l)


def _moe(tok, logits, n_e, layer, w_up, w_down, b_up, b_down):
    b_sz, s_len, d = tok.shape
    n_tok = b_sz * s_len
    tm = MOE_TILE
    top_v, top_e = lax.top_k(logits.reshape(n_tok, n_e), TOP_K)
    gates = jax.nn.softmax(top_v, axis=-1)
    n_as = n_tok * TOP_K
    onehot = jnp.sum((top_e[:, :, None] == jnp.arange(n_e)[None, None, :]).astype(jnp.int32), axis=1)
    csum = jnp.cumsum(onehot, axis=0)
    counts = csum[-1]
    padded = (counts + tm - 1) // tm * tm
    end_pad = jnp.cumsum(padded)
    start_pad = end_pad - padded
    start = jnp.cumsum(counts) - counts
    dest = jnp.take_along_axis(csum - onehot + start_pad[None, :], top_e, axis=1).astype(jnp.int32)
    n_blocks = -(-n_as // tm) + n_e
    blk_first = jnp.arange(n_blocks) * tm
    blk_e = jnp.minimum(jnp.sum(blk_first[:, None] >= end_pad[None, :], axis=1), n_e - 1).astype(jnp.int32)
    n_used = (end_pad[-1:] // tm).astype(jnp.int32)
    tok_sorted = (jnp.argsort(top_e.reshape(n_as)) // TOP_K).astype(jnp.int32)
    j = (blk_first - start_pad[blk_e])[:, None] + jnp.arange(tm)[None, :]
    src = jnp.clip(start[blk_e][:, None] + j, 0, n_as - 1)
    row_tok = jnp.where(j < counts[blk_e][:, None], tok_sorted[src], 0).reshape(n_blocks * tm)
    x_sorted = tok.reshape(n_tok, d)[row_tok]
    y = _moe_blocks(blk_e + layer * n_e, n_used, x_sorted, w_up, w_down, b_up, b_down)
    y_as = y[dest.T].reshape(TOP_K, b_sz, s_len, d // 2)
    return y_as, gates.reshape(b_sz, s_len, TOP_K)


NA_STEP = 256
NA_KEYS = NA_ROWS * GRID_W
NA_GROUPS_PER_ITER = 2


def _head_rms(x, w, lo):
    xx = x * x
    s0 = jnp.sum(jnp.where(lo, xx, 0.0), axis=-1, keepdims=True)
    s1 = jnp.sum(jnp.where(lo, 0.0, xx), axis=-1, keepdims=True)
    inv = lax.rsqrt(jnp.where(lo, s0, s1) * (1.0 / HEAD_DIM) + NORM_EPS)
    return x * inv * w


def _dot_nt(a, b):
    return lax.dot_general(a, b, (((1,), (1,)), ((), ())), preferred_element_type=F32)


def _na_kernel(q_ref, k_ref, v_ref, qw_ref, kw_ref, bias_ref, o_ref, kt_ref, vb_ref, s_ref, *, t_ctx, rows):
    rg = pl.program_id(2)
    s_len = k_ref.shape[1]
    lo = lax.broadcasted_iota(jnp.int32, (1, LANES), 1) < HEAD_DIM
    rows_per_step = NA_STEP // GRID_W

    @pl.when(rg == 0)
    def _():
        def knorm_t(t0, n):
            return _head_rms(k_ref[0, pl.ds(t0, n), :], kw_ref[...], lo).T.astype(BF16)

        def prep(c, carry):
            t0 = pl.multiple_of(c * NA_STEP, NA_STEP)
            kt_ref[0, :, pl.ds(t0, NA_STEP)] = knorm_t(t0, NA_STEP)
            vb_ref[pl.ds(t0, NA_STEP), :] = jnp.concatenate(
                [v_ref[0, pl.ds(t0, NA_STEP), :].astype(BF16), jnp.ones((NA_STEP, LANES), BF16)], axis=1)
            return carry
        lax.fori_loop(0, s_len // NA_STEP, prep, 0, unroll=3)

        def prep_shifted(c, carry):
            t0 = pl.multiple_of(c * LANES, LANES)
            kt_ref[1, :, pl.ds(t0, LANES)] = knorm_t(pl.multiple_of(t0 + GRID_W, GRID_W), LANES)
            return carry
        lax.fori_loop(0, (s_len - GRID_W) // LANES, prep_shifted, 0, unroll=5)

    qn = _head_rms(q_ref[0], qw_ref[...], lo) * (HEAD_DIM ** -0.5)
    q0 = jnp.where(lo, qn, 0.0).astype(BF16)
    q1 = jnp.where(lo, 0.0, qn).astype(BF16)
    kc_t = kt_ref[0, :, 0:t_ctx]
    vc = vb_ref[0:t_ctx, :]

    def finish(g, o2):
        o2 = o2[:, :LANES] * (1.0 / o2[:, LANES:])
        o = jnp.where(lo, o2[:GRID_W], o2[GRID_W:])
        o_ref[0, g * GRID_W:(g + 1) * GRID_W, :] = o.astype(o_ref.dtype)

    @pl.when(rg == 0)
    def _():
        for g in range(rows_per_step):
            q2 = jnp.concatenate([q0[g * GRID_W:(g + 1) * GRID_W], q1[g * GRID_W:(g + 1) * GRID_W]], axis=0)
            s_c = jnp.dot(q2, kc_t, preferred_element_type=F32)
            p_c = jnp.exp(s_c - jnp.max(s_c, axis=-1, keepdims=True))
            finish(g, jnp.dot(p_c.astype(BF16), vc, preferred_element_type=F32))

    @pl.when(rg > 0)
    def _():
        t0s = []
        for g in range(rows_per_step):
            r = (rg - 1) * rows_per_step + g
            row_start = jnp.clip(r - NA_ROWS // 2, 0, rows - NA_ROWS)
            dr0 = row_start - r + NA_ROWS - 1
            t0 = pl.multiple_of(t_ctx + row_start * GRID_W, GRID_W)
            odd = (t0 // GRID_W) % (LANES // GRID_W)
            kw_t = kt_ref[odd, :, pl.ds(pl.multiple_of(t0 - odd * GRID_W, LANES), NA_KEYS)]
            q2 = jnp.concatenate([q0[g * GRID_W:(g + 1) * GRID_W], q1[g * GRID_W:(g + 1) * GRID_W]], axis=0)
            s_ref[g, :, :NA_KEYS] = jnp.dot(q2, kw_t, preferred_element_type=F32) + bias_ref[dr0, 0]
            s_ref[g, :, NA_KEYS:] = jnp.dot(q2, kc_t, preferred_element_type=F32)
            t0s.append(t0)
        for g in range(rows_per_step):
            s = s_ref[g]
            p = jnp.exp(s - jnp.max(s, axis=-1, keepdims=True)).astype(BF16)
            finish(g, jnp.dot(p[:, :NA_KEYS], vb_ref[pl.ds(t0s[g], NA_KEYS), :], preferred_element_type=F32)
                   + jnp.dot(p[:, NA_KEYS:], vc, preferred_element_type=F32))


def _na_bias_tables(rpb):
    n_l, n_h, n_dr, n_dc = rpb.shape
    col = np.arange(GRID_W)
    dc = np.clip(col[None, :] - col[:, None] + NA_COLS - 1, 0, n_dc - 1)
    onehot = (dc.reshape(1, -1) == np.arange(n_dc)[:, None]).astype(np.float32)
    toe = jnp.dot(rpb.reshape(-1, n_dc), jnp.asarray(onehot), precision=lax.Precision.HIGHEST)
    toe = toe.reshape(n_l, n_h, n_dr, GRID_W, GRID_W)
    col_start = np.clip(col - NA_COLS // 2, 0, GRID_W - NA_COLS)
    col_in = (col[None, :] >= col_start[:, None]) & (col[None, :] < col_start[:, None] + NA_COLS)
    toe = jnp.where(jnp.asarray(col_in), toe, NEG_INF)
    tabs = []
    for dr0 in range(NA_ROWS):
        t = toe[:, :, dr0:dr0 + NA_ROWS].reshape(n_l, n_h // 2, 2, NA_ROWS, GRID_W, GRID_W)
        t = jnp.transpose(t, (0, 1, 2, 4, 3, 5))
        tabs.append(t.reshape(n_l, n_h // 2, 2 * GRID_W, NA_KEYS))
    return jnp.stack(tabs, axis=1)


def _na_seq_kernel(q_ref, k_ref, v_ref, qw_ref, kw_ref, bias_ref, o_ref, kt_ref, vb_ref, s_ref, kn_ref, *,
                   t_ctx, rows):
    s_len = k_ref.shape[1]
    lo = lax.broadcasted_iota(jnp.int32, (1, LANES), 1) < HEAD_DIM
    rows_per_step = NA_STEP // GRID_W

    def prep(c, carry):
        t0 = pl.multiple_of(c * NA_STEP, NA_STEP)
        kn = _head_rms(k_ref[0, pl.ds(t0, NA_STEP), :], kw_ref[...], lo).astype(BF16)
        kn_ref[pl.ds(t0, NA_STEP), :] = kn
        kt_ref[0, :, pl.ds(t0, NA_STEP)] = kn.T
        vb_ref[pl.ds(t0, NA_STEP), :] = jnp.concatenate(
            [v_ref[0, pl.ds(t0, NA_STEP), :].astype(BF16), jnp.ones((NA_STEP, LANES), BF16)], axis=1)
        return carry
    lax.fori_loop(0, s_len // NA_STEP, prep, 0, unroll=3)

    def prep_shifted(c, carry):
        t0 = pl.multiple_of(c * LANES, LANES)
        kt_ref[1, :, pl.ds(t0, LANES)] = kn_ref[pl.ds(pl.multiple_of(t0 + GRID_W, GRID_W), LANES), :].T
        return carry
    lax.fori_loop(0, (s_len - GRID_W) // LANES, prep_shifted, 0, unroll=5)

    kc_t = kt_ref[0, :, 0:t_ctx]
    vc = vb_ref[0:t_ctx, :]

    def queries(tok0):
        qn = _head_rms(q_ref[0, pl.ds(tok0, NA_STEP), :], qw_ref[...], lo) * (HEAD_DIM ** -0.5)
        q0 = jnp.where(lo, qn, 0.0).astype(BF16)
        q1 = jnp.where(lo, 0.0, qn).astype(BF16)
        return [jnp.concatenate([q0[g * GRID_W:(g + 1) * GRID_W], q1[g * GRID_W:(g + 1) * GRID_W]], axis=0)
                for g in range(rows_per_step)]

    def finish(tok0, g, o2):
        o2 = o2[:, :LANES] * (1.0 / o2[:, LANES:])
        o = jnp.where(lo, o2[:GRID_W], o2[GRID_W:])
        o_ref[0, pl.ds(tok0 + g * GRID_W, GRID_W), :] = o.astype(o_ref.dtype)

    for g, q2 in enumerate(queries(0)):
        s_c = jnp.dot(q2, kc_t, preferred_element_type=F32)
        p_c = jnp.exp(s_c - jnp.max(s_c, axis=-1, keepdims=True))
        finish(0, g, jnp.dot(p_c.astype(BF16), vc, preferred_element_type=F32))

    def row_group(rg, slot):
        tok0 = pl.multiple_of(t_ctx + rg * NA_STEP, NA_STEP)
        t0s = []
        s_slot = s_ref.at[slot]
        for g, q2 in enumerate(queries(tok0)):
            r = rg * rows_per_step + g
            row_start = jnp.clip(r - NA_ROWS // 2, 0, rows - NA_ROWS)
            dr0 = row_start - r + NA_ROWS - 1
            t0 = pl.multiple_of(t_ctx + row_start * GRID_W, GRID_W)
            odd = (t0 // GRID_W) % (LANES // GRID_W)
            kw_t = kt_ref[odd, :, pl.ds(pl.multiple_of(t0 - odd * GRID_W, LANES), NA_KEYS)]
            s_slot[g, :, :NA_KEYS] = jnp.dot(q2, kw_t, preferred_element_type=F32) + bias_ref[dr0, 0]
            s_slot[g, :, NA_KEYS:] = jnp.dot(q2, kc_t, preferred_element_type=F32)
            t0s.append(t0)
        for g in range(rows_per_step):
            s = s_slot[g]
            p = jnp.exp(s - jnp.max(s, axis=-1, keepdims=True)).astype(BF16)
            finish(tok0, g, jnp.dot(p[:, :NA_KEYS], vb_ref[pl.ds(t0s[g], NA_KEYS), :], preferred_element_type=F32)
                   + jnp.dot(p[:, NA_KEYS:], vc, preferred_element_type=F32))

    def row_groups(i, carry):
        for slot in range(NA_GROUPS_PER_ITER):
            row_group(i * NA_GROUPS_PER_ITER + slot, slot)
        return carry
    lax.fori_loop(0, rows // rows_per_step // NA_GROUPS_PER_ITER, row_groups, 0)


def _na_attention(p_main, q_w, k_w, bias_tab, t_ctx, col_q, n_heads):
    b_sz, s_len, _ = p_main.shape
    n_hg = n_heads // 2
    rows = (s_len - t_ctx) // GRID_W
    rows_per_step = NA_STEP // GRID_W
    assert t_ctx == NA_STEP and s_len % NA_STEP == 0 and rows >= NA_ROWS
    assert rows % (rows_per_step * NA_GROUPS_PER_ITER) == 0
    cq = col_q // LANES
    seq = lambda col: pl.BlockSpec((1, s_len, LANES), lambda b, h: (b, 0, col + h))
    w2 = lambda w: jnp.concatenate([w, w]).reshape(1, LANES)
    return pl.pallas_call(
        functools.partial(_na_seq_kernel, t_ctx=t_ctx, rows=rows),
        grid=(b_sz, n_hg),
        in_specs=[seq(cq), seq(cq + n_hg), _seq_spec(s_len, cq + 2 * n_hg),
                  pl.BlockSpec((1, LANES), lambda b, h: (0, 0)),
                  pl.BlockSpec((1, LANES), lambda b, h: (0, 0)),
                  pl.BlockSpec((NA_ROWS, 1, 2 * GRID_W, NA_KEYS), lambda b, h: (0, h, 0, 0))],
        out_specs=pl.BlockSpec((1, s_len, LANES), lambda b, h: (b, 0, h)),
        out_shape=jax.ShapeDtypeStruct((b_sz, s_len, n_heads * HEAD_DIM), BF16),
        scratch_shapes=[pltpu.VMEM((2, LANES, s_len), BF16), pltpu.VMEM((s_len, 2 * LANES), BF16),
                        pltpu.VMEM((NA_GROUPS_PER_ITER, rows_per_step, 2 * GRID_W, NA_KEYS + t_ctx), F32),
                        pltpu.VMEM((s_len, LANES), BF16)],
        compiler_params=_cparams(("arbitrary", "arbitrary")),
        name="na_attention",
    )(p_main, p_main, p_main, w2(q_w), w2(k_w), bias_tab)


RET_CHUNK_LEN = 256
RET_UNROLL = 3
MLSTM_CHUNK_LEN = 128


def _reverse_chunk(i, nc, n_ctx):
    return jnp.where(i < n_ctx, n_ctx - 1 - i, nc - 1 - i + n_ctx)


def _dot_tn(a, b):
    return lax.dot_general(a, b, (((0,), (0,)), ((), ())), preferred_element_type=F32)


def _rope_tables(t_ctx, t_len):
    pos = jnp.arange(t_len)
    row = (pos // GRID_W).astype(F32)
    col = (pos % GRID_W).astype(F32)
    n = HEAD_DIM // 4
    inv = ROPE_BASE ** (-jnp.arange(n, dtype=F32) / n)
    ar = row[:, None] * inv
    ac = col[:, None] * inv
    cos = jnp.concatenate([jnp.cos(ar), jnp.cos(ar), jnp.cos(ac), jnp.cos(ac)], axis=-1)
    sin = jnp.concatenate([-jnp.sin(ar), jnp.sin(ar), -jnp.sin(ac), jnp.sin(ac)], axis=-1)
    cos = jnp.concatenate([jnp.ones((t_ctx, HEAD_DIM), F32), cos], axis=0)
    sin = jnp.concatenate([jnp.zeros((t_ctx, HEAD_DIM), F32), sin], axis=0)
    return jnp.tile(cos, (1, 2)), jnp.tile(sin, (1, 2))


def _ret_tables(n_heads):
    L = RET_CHUNK_LEN
    pos = np.arange(L, dtype=np.float32)
    lane_head = np.arange(LANES) // HEAD_DIM
    decay = np.zeros((2, n_heads, L, L), np.float32)
    zeta = np.zeros((2, n_heads // 2, LANES, L), np.float32)
    xi = np.zeros((2, n_heads // 2, L, LANES), np.float32)
    gch = np.zeros((2, n_heads // 2, 1, LANES), np.float32)
    for d, first_exp in enumerate((5.0, 6.0)):
        e = np.float32(first_exp) + np.float32(2.0) * np.arange(n_heads, dtype=np.float32)
        lg = np.log1p(-np.exp2(-e)).astype(np.float32)
        diff = pos[:, None] - pos[None, :]
        if d == 1:
            diff = -diff
        for h in range(n_heads):
            decay[d, h] = np.where(diff >= 0, np.exp(lg[h] * np.maximum(diff, 0.0)), 0.0)
        for hp in range(n_heads // 2):
            lgl = lg[2 * hp + lane_head][None, :]
            to_end = (L - 1 - pos if d == 0 else pos)[:, None]
            zeta[d, hp] = np.exp(lgl * to_end).T
            xi[d, hp] = np.exp(lgl * (L - to_end))
            gch[d, hp] = np.exp(lgl * L)
    return tuple(jnp.asarray(a) for a in (decay, zeta, xi, gch))


def _ret_kernel(q_ref, k_ref, v_ref, g_ref, cos_ref, sin_ref, dec_ref, zeta_ref, xi_ref, gch_ref, rn_ref,
                o_ref, qr_ref, kt_ref, vb_ref, acc_ref, *, t_ctx):
    L = RET_CHUNK_LEN
    s_len = q_ref.shape[1]
    nc = s_len // L
    lane = lax.broadcasted_iota(jnp.int32, (1, LANES), 1)
    lo = lane < HEAD_DIM
    half = (lane & (HEAD_DIM // 4)) == 0
    rid = lax.broadcasted_iota(jnp.int32, (LANES, LANES), 0) < HEAD_DIM
    cid = lax.broadcasted_iota(jnp.int32, (LANES, LANES), 1) < HEAD_DIM
    same_head = rid == cid

    def rope(x, cos, sin):
        up = pltpu.roll(x, LANES - HEAD_DIM // 4, axis=1)
        dn = pltpu.roll(x, HEAD_DIM // 4, axis=1)
        return x * cos + jnp.where(half, up, dn) * sin

    def prep(c, carry):
        sl = pl.ds(pl.multiple_of(c * L, L), L)
        cos, sin = cos_ref[sl, :], sin_ref[sl, :]
        qr_ref[sl, :] = rope(q_ref[0, sl, :], cos, sin).astype(BF16)
        kt_ref[:, sl] = (rope(k_ref[0, sl, :], cos, sin) * HEAD_DIM ** -0.5).T.astype(BF16)
        vb_ref[sl, :] = v_ref[0, sl, :].astype(BF16)
        return carry
    lax.fori_loop(0, nc, prep, 0, unroll=RET_UNROLL)

    def chunk(d, c, state):
        sl = pl.ds(pl.multiple_of(c * L, L), L)
        q, kt, v = qr_ref[sl, :], kt_ref[:, sl], vb_ref[sl, :]
        inter = jnp.dot(q, state.astype(BF16), preferred_element_type=F32) * xi_ref[d, 0]
        outs = []
        for h2 in range(2):
            qm = jnp.where(lo if h2 == 0 else jnp.logical_not(lo), q, jnp.zeros_like(q))
            sd = (jnp.dot(qm, kt, preferred_element_type=F32) * dec_ref[d, h2]).astype(BF16)
            outs.append(jnp.dot(sd, v, preferred_element_type=F32))
        y = jnp.where(lo, outs[0], outs[1]) + inter
        kz_t = (kt.astype(F32) * zeta_ref[d, 0]).astype(BF16)
        state = state * gch_ref[d, 0] + jnp.where(same_head, jnp.dot(kz_t, v, preferred_element_type=F32), 0.0)
        return sl, y, state

    def fwd(c, state):
        sl, y, state = chunk(0, c, state)
        acc_ref[sl, :] = y
        return state
    lax.fori_loop(0, nc, fwd, jnp.zeros((LANES, LANES), F32), unroll=RET_UNROLL)

    def bwd(i, state):
        c = _reverse_chunk(i, nc, t_ctx // L)
        sl, y, state = chunk(1, c, state)
        y = _head_rms(y + acc_ref[sl, :], rn_ref[...], lo)
        g = g_ref[0, sl, :]
        o_ref[0, sl, :] = (y * (g * jax.nn.sigmoid(g))).astype(o_ref.dtype)
        return state
    lax.fori_loop(0, nc, bwd, jnp.zeros((LANES, LANES), F32), unroll=RET_UNROLL)


def _seq_spec(s_len, col):
    return pl.BlockSpec((1, s_len, LANES), lambda b, h: (b, 0, col + h), pipeline_mode=pl.Buffered(1))


def _retention(p_main, rope_tabs, ret_tabs, r_w, t_ctx, col_q, n_heads):
    b_sz, s_len, _ = p_main.shape
    n_hp = n_heads // 2
    L = RET_CHUNK_LEN
    assert t_ctx % L == 0 and s_len % L == 0
    c0 = col_q // LANES
    cos, sin = rope_tabs
    decay, zeta, xi, gch = ret_tabs
    const2 = pl.BlockSpec((s_len, LANES), lambda b, h: (0, 0), pipeline_mode=pl.Buffered(1))
    return pl.pallas_call(
        functools.partial(_ret_kernel, t_ctx=t_ctx),
        grid=(b_sz, n_hp),
        in_specs=[_seq_spec(s_len, c0), _seq_spec(s_len, c0 + n_hp), _seq_spec(s_len, c0 + 2 * n_hp),
                  _seq_spec(s_len, c0 + 3 * n_hp), const2, const2,
                  pl.BlockSpec((2, 2, L, L), lambda b, h: (0, h, 0, 0)),
                  pl.BlockSpec((2, 1, LANES, L), lambda b, h: (0, h, 0, 0)),
                  pl.BlockSpec((2, 1, L, LANES), lambda b, h: (0, h, 0, 0)),
                  pl.BlockSpec((2, 1, 1, LANES), lambda b, h: (0, h, 0, 0)),
                  pl.BlockSpec((1, LANES), lambda b, h: (0, 0))],
        out_specs=pl.BlockSpec((1, s_len, LANES), lambda b, h: (b, 0, h)),
        out_shape=jax.ShapeDtypeStruct((b_sz, s_len, n_heads * HEAD_DIM), BF16),
        scratch_shapes=[pltpu.VMEM((s_len, LANES), BF16), pltpu.VMEM((LANES, s_len), BF16),
                        pltpu.VMEM((s_len, LANES), BF16), pltpu.VMEM((s_len, LANES), F32)],
        compiler_params=_cparams(("arbitrary", "arbitrary")),
        name="retention",
    )(p_main, p_main, p_main, p_main, cos, sin, decay, zeta, xi, gch,
      jnp.concatenate([r_w, r_w]).reshape(1, LANES))


N_GATE_TYPES = 4


def _log_sigmoid(x):
    return jnp.minimum(x, 0.0) - jnp.log1p(jnp.exp(-jnp.abs(x)))


def _split3(x):
    hi = x.astype(BF16)
    r = x - hi.astype(F32)
    mid = r.astype(BF16)
    return hi, mid, (r - mid.astype(F32)).astype(BF16)


def _mlstm_kernel(q_ref, k_ref, v_ref, og_ref, gc_ref, gr_ref, wq_ref, wk_ref, bq_ref, bk_ref, gbc_ref, gbr_ref,
                  mn_ref, o_ref, qc_ref, kc_ref, vb_ref, acc_ref, *, t_ctx):
    L = MLSTM_CHUNK_LEN
    s_len = q_ref.shape[1]
    nc = s_len // L
    lane = lax.broadcasted_iota(jnp.int32, (1, LANES), 1)
    lo = lane < HEAD_DIM
    head_lanes = (lo, jnp.logical_not(lo))
    rid = lax.broadcasted_iota(jnp.int32, (LANES, LANES), 0) < HEAD_DIM
    cid = lax.broadcasted_iota(jnp.int32, (LANES, LANES), 1) < HEAD_DIM
    head_block = (rid & cid, jnp.logical_not(rid | cid))
    row_i = lax.broadcasted_iota(jnp.int32, (L, L), 0)
    col_i = lax.broadcasted_iota(jnp.int32, (L, L), 1)
    causal = (row_i >= col_i, row_i <= col_i)
    tri = tuple(c.astype(F32) for c in causal)
    tri_t = (tri[1], tri[0])
    sub = lax.broadcasted_iota(jnp.int32, (L, 1), 0)

    def conv(x_ref, w_ref, b_ref, t0):
        x = x_ref[0, pl.ds(t0, L), :]
        prev = x_ref[0, pl.ds(jnp.maximum(t0 - 8, 0), 8), :][7:8]
        nxt = x_ref[0, pl.ds(jnp.minimum(t0 + L, s_len - 8), 8), :][0:1]
        prev = jnp.where((t0 != 0) & (t0 != t_ctx), prev, 0.0)
        nxt = jnp.where((t0 + L != t_ctx) & (t0 + L != s_len), nxt, 0.0)
        xm = jnp.where(sub == 0, prev, pltpu.roll(x, 1, axis=0))
        xp = jnp.where(sub == L - 1, nxt, pltpu.roll(x, L - 1, axis=0))
        y = b_ref[...] + xm * w_ref[0:1, :] + x * w_ref[1:2, :] + xp * w_ref[2:3, :]
        return y * jax.nn.sigmoid(y)

    def prep(c, carry):
        t0 = pl.multiple_of(c * L, L)
        sl = pl.ds(t0, L)
        qc_ref[sl, :] = conv(q_ref, wq_ref, bq_ref, t0).astype(BF16)
        kc_ref[sl, :] = (conv(k_ref, wk_ref, bk_ref, t0) * HEAD_DIM ** -0.5).astype(BF16)
        vb_ref[sl, :] = v_ref[0, sl, :].astype(BF16)
        return carry
    lax.fori_loop(0, nc, prep, 0)

    def chunk(d, c, state):
        sl = pl.ds(pl.multiple_of(c * L, L), L)
        q, k, v = qc_ref[sl, :], kc_ref[sl, :], vb_ref[sl, :]
        g_col = gc_ref[0, sl, :] + gbc_ref[0]
        g_row = gr_ref[0, 0, :, sl] + gbr_ref[0]
        cum_col = jnp.dot(tri[d], _log_sigmoid(g_col), precision=lax.Precision.HIGHEST,
                          preferred_element_type=F32)
        cum_row = jnp.dot(_log_sigmoid(g_row), tri_t[d], precision=lax.Precision.HIGHEST,
                          preferred_element_type=F32)
        end = L - 1 if d == 0 else 0
        outs, new_state = [], []
        for h2 in range(2):
            c_st, n_st, m_st = state[h2]
            ci, cf = 2 * (2 * d) + h2, 2 * (2 * d + 1) + h2
            i_col, a_col = g_col[:, ci:ci + 1], cum_col[:, cf:cf + 1]
            i_row, a_row = g_row[ci:ci + 1, :], cum_row[cf:cf + 1, :]
            b_tot = a_row[:, end:end + 1]
            d_log = jnp.where(causal[d], a_col + (i_row - a_row), NEG_INF)
            m_intra = jnp.max(d_log, axis=-1, keepdims=True)
            qm = jnp.where(head_lanes[h2], q, jnp.zeros_like(q))
            s = _dot_nt(qm, k) * jnp.exp(d_log - m_intra)
            num_intra = jnp.dot(s.astype(BF16), v, preferred_element_type=F32)
            den_intra = jnp.sum(s, axis=-1, keepdims=True)
            inter_log = a_col + m_st
            m_q = jnp.maximum(m_intra, inter_log)
            a = jnp.exp(inter_log - m_q)
            g = jnp.exp(m_intra - m_q)
            num = a * jnp.dot(qm, c_st.astype(BF16), preferred_element_type=F32) + g * num_intra
            den = a * jnp.sum(qm.astype(F32) * n_st, axis=-1, keepdims=True) + g * den_intra
            outs.append(num / jnp.maximum(jnp.abs(den), jnp.exp(-m_q)))
            w_log = b_tot - a_col + i_col
            m_loc = jnp.max(w_log, axis=0, keepdims=True)
            ke = jnp.where(head_lanes[h2], k.astype(F32) * jnp.exp(w_log - m_loc), 0.0)
            c_loc = jnp.where(head_block[h2], _dot_tn(ke.astype(BF16), v), 0.0)
            n_loc = jnp.sum(ke, axis=0, keepdims=True)
            m_new = jnp.maximum(b_tot + m_st, m_loc)
            a_s = jnp.exp(b_tot + m_st - m_new)
            g_s = jnp.exp(m_loc - m_new)
            new_state.append((a_s * c_st + g_s * c_loc, a_s * n_st + g_s * n_loc, m_new))
        return sl, jnp.where(lo, outs[0], outs[1]), tuple(new_state)

    zero = tuple((jnp.zeros((LANES, LANES), F32), jnp.zeros((1, LANES), F32), jnp.zeros((1, 1), F32))
                 for _ in range(2))

    def fwd(c, state):
        sl, y, state = chunk(0, c, state)
        acc_ref[sl, :] = y
        return state
    lax.fori_loop(0, nc, fwd, zero)

    def bwd(i, state):
        c = _reverse_chunk(i, nc, t_ctx // L)
        sl, y, state = chunk(1, c, state)
        y = _head_rms(y + acc_ref[sl, :], mn_ref[...], lo)
        o_ref[0, sl, :] = (y * jax.nn.sigmoid(og_ref[0, sl, :])).astype(o_ref.dtype)
        return state
    lax.fori_loop(0, nc, bwd, zero)


def _mlstm_pair_kernel(q_ref, k_ref, v_ref, og_ref, gc_ref, gr_ref, wq_ref, wk_ref, bq_ref, bk_ref, gbc_ref, gbr_ref,
                       mn_ref, o_ref, qc_ref, kt_ref, vb_ref, acc_ref, st_ref, cc_ref, cr_ref, *, t_ctx):
    L = MLSTM_CHUNK_LEN
    s_len = q_ref.shape[1]
    nc = s_len // L
    lane = lax.broadcasted_iota(jnp.int32, (1, LANES), 1)
    lo = lane < HEAD_DIM
    head_lanes = (lo, jnp.logical_not(lo))
    sub_lo = lax.broadcasted_iota(jnp.int32, (LANES, 1), 0) < HEAD_DIM
    head_rows = (sub_lo, jnp.logical_not(sub_lo))
    row_i = lax.broadcasted_iota(jnp.int32, (L, L), 0)
    col_i = lax.broadcasted_iota(jnp.int32, (L, L), 1)
    causal = (row_i >= col_i, row_i <= col_i)
    tri_b = causal[0].astype(BF16)
    tri_bt = causal[1].astype(BF16)
    sub = lax.broadcasted_iota(jnp.int32, (L, 1), 0)
    gate_row = lax.broadcasted_iota(jnp.int32, (2 * N_GATE_TYPES, 1), 0)
    ones = jnp.ones((L, LANES), BF16)

    def conv(x_ref, w_ref, b_ref, t0):
        x = x_ref[0, pl.ds(t0, L), :]
        prev = x_ref[0, pl.ds(jnp.maximum(t0 - 8, 0), 8), :][7:8]
        nxt = x_ref[0, pl.ds(jnp.minimum(t0 + L, s_len - 8), 8), :][0:1]
        prev = jnp.where((t0 != 0) & (t0 != t_ctx), prev, 0.0)
        nxt = jnp.where((t0 + L != t_ctx) & (t0 + L != s_len), nxt, 0.0)
        xm = jnp.where(sub == 0, prev, pltpu.roll(x, 1, axis=0))
        xp = jnp.where(sub == L - 1, nxt, pltpu.roll(x, L - 1, axis=0))
        y = b_ref[...] + xm * w_ref[0:1, :] + x * w_ref[1:2, :] + xp * w_ref[2:3, :]
        return y * jax.nn.sigmoid(y)

    def prep(c, carry):
        t0 = pl.multiple_of(c * L, L)
        sl = pl.ds(t0, L)
        qc_ref[sl, :] = conv(q_ref, wq_ref, bq_ref, t0).astype(BF16)
        kt_ref[:, sl] = (conv(k_ref, wk_ref, bk_ref, t0) * HEAD_DIM ** -0.5).T.astype(BF16)
        vb_ref[sl, :] = v_ref[0, sl, :].astype(BF16)
        lf_col = _log_sigmoid(gc_ref[0, sl, :] + gbc_ref[0])
        lf_row = _log_sigmoid(gr_ref[0, 0, :, sl] + gbr_ref[0])
        pre_col = sum(jnp.dot(tri_b, p, preferred_element_type=F32) for p in _split3(lf_col))
        pre_row = sum(jnp.dot(p, tri_bt, preferred_element_type=F32) for p in _split3(lf_row))
        cc_ref[sl, :] = jnp.where(lane < N_GATE_TYPES, pre_col, pre_col[L - 1:L, :] - pre_col + lf_col)
        cr_ref[:, sl] = jnp.where(gate_row < N_GATE_TYPES, pre_row, pre_row[:, L - 1:L] - pre_row + lf_row)
        return carry
    lax.fori_loop(0, nc, prep, 0, unroll=2)

    def chunk(d, c, m_state):
        sl = pl.ds(pl.multiple_of(c * L, L), L)
        q, kt, v = qc_ref[sl, :], kt_ref[:, sl], vb_ref[sl, :]
        g_row = gr_ref[0, 0, :, sl] + gbr_ref[0]
        cum_col, cum_row = cc_ref[sl, :], cr_ref[:, sl]
        end = L - 1 if d == 0 else 0
        outs, new_m = [], []
        for h2 in range(2):
            m_st = m_state[h2]
            ci, cf = 2 * (2 * d) + h2, 2 * (2 * d + 1) + h2
            a_rep = jnp.broadcast_to(cum_col[:, cf:cf + 1], (L, LANES))
            i_row, a_row = g_row[ci:ci + 1, :], cum_row[cf:cf + 1, :]
            b_tot = a_row[:, end:end + 1]
            d_log = jnp.where(causal[d], a_rep + (i_row - a_row), NEG_INF)
            m_intra = jnp.broadcast_to(jnp.max(d_log, axis=-1, keepdims=True), (L, LANES))
            qm = jnp.where(head_lanes[h2], q, jnp.zeros_like(q))
            s = jnp.dot(qm, kt, preferred_element_type=F32) * jnp.exp(d_log - m_intra)
            v1 = jnp.where(head_lanes[h2], v, ones)
            intra = jnp.dot(s.astype(BF16), v1, preferred_element_type=F32)
            inter = jnp.dot(qm, st_ref[h2].astype(BF16), preferred_element_type=F32)
            inter_log = a_rep + m_st
            m_q = jnp.maximum(m_intra, inter_log)
            num_den = jnp.exp(inter_log - m_q) * inter + jnp.exp(m_intra - m_q) * intra
            den = pltpu.roll(num_den, HEAD_DIM, axis=1)
            outs.append(num_den / jnp.maximum(jnp.abs(den), jnp.exp(-m_q)))
            w_row = b_tot - a_row + i_row
            m_loc = jnp.max(w_row, axis=-1, keepdims=True)
            kts = jnp.where(head_rows[h2], kt.astype(F32) * jnp.exp(w_row - m_loc), 0.0).astype(BF16)
            loc = jnp.dot(kts, v1, preferred_element_type=F32)
            m_new = jnp.maximum(b_tot + m_st, m_loc)
            st_ref[h2] = jnp.exp(b_tot + m_st - m_new) * st_ref[h2] + jnp.exp(m_loc - m_new) * loc
            new_m.append(m_new)
        return sl, jnp.where(lo, outs[0], outs[1]), tuple(new_m)

    zero_m = (jnp.zeros((1, 1), F32), jnp.zeros((1, 1), F32))

    st_ref[...] = jnp.zeros_like(st_ref)

    def fwd(c, m_state):
        sl, y, m_state = chunk(0, c, m_state)
        acc_ref[sl, :] = y
        return m_state
    lax.fori_loop(0, nc, fwd, zero_m, unroll=3)

    st_ref[...] = jnp.zeros_like(st_ref)

    def bwd(i, m_state):
        c = _reverse_chunk(i, nc, t_ctx // L)
        sl, y, m_state = chunk(1, c, m_state)
        y = _head_rms(y + acc_ref[sl, :], mn_ref[...], lo)
        o_ref[0, sl, :] = (y * jax.nn.sigmoid(og_ref[0, sl, :])).astype(o_ref.dtype)
        return m_state
    lax.fori_loop(0, nc, bwd, zero_m, unroll=3)


def _mlstm(p_main, p_gate, conv_w, conv_b, gate_b, m_w, t_ctx, n_heads):
    b_sz, s_len, _ = p_main.shape
    n_hp = n_heads // 2
    L = MLSTM_CHUNK_LEN
    assert L == LANES and t_ctx % L == 0 and s_len % L == 0
    n_g = 2 * N_GATE_TYPES
    g_rows = jnp.transpose(p_gate.reshape(b_sz, s_len, n_hp, LANES)[..., :n_g], (0, 2, 3, 1))
    gb = jnp.transpose(gate_b.reshape(N_GATE_TYPES, n_hp, 2), (1, 0, 2)).reshape(n_hp, n_g)
    gb_col = jnp.zeros((n_hp, 1, LANES), F32).at[:, 0, :n_g].set(gb)
    gb_row = gb.reshape(n_hp, n_g, 1)
    vec = lambda col: pl.BlockSpec((1, LANES), lambda b, h: (0, col + h))
    return pl.pallas_call(
        functools.partial(_mlstm_pair_kernel, t_ctx=t_ctx),
        grid=(b_sz, n_hp),
        in_specs=[_seq_spec(s_len, 0), _seq_spec(s_len, n_hp), _seq_spec(s_len, 2 * n_hp),
                  _seq_spec(s_len, 3 * n_hp),
                  pl.BlockSpec((1, s_len, LANES), lambda b, h: (b, 0, h), pipeline_mode=pl.Buffered(1)),
                  pl.BlockSpec((1, 1, n_g, s_len), lambda b, h: (b, h, 0, 0)),
                  pl.BlockSpec((3, LANES), lambda b, h: (0, h)),
                  pl.BlockSpec((3, LANES), lambda b, h: (0, n_hp + h)),
                  vec(0), vec(n_hp),
                  pl.BlockSpec((1, 1, LANES), lambda b, h: (h, 0, 0)),
                  pl.BlockSpec((1, n_g, 1), lambda b, h: (h, 0, 0)),
                  pl.BlockSpec((1, LANES), lambda b, h: (0, 0))],
        out_specs=pl.BlockSpec((1, s_len, LANES), lambda b, h: (b, 0, h)),
        out_shape=jax.ShapeDtypeStruct((b_sz, s_len, n_heads * HEAD_DIM), BF16),
        scratch_shapes=[pltpu.VMEM((s_len, LANES), BF16), pltpu.VMEM((LANES, s_len), BF16),
                        pltpu.VMEM((s_len, LANES), BF16), pltpu.VMEM((s_len, LANES), F32),
                        pltpu.VMEM((2, LANES, LANES), F32),
                        pltpu.VMEM((s_len, LANES), F32), pltpu.VMEM((n_g, s_len), F32)],
        compiler_params=_cparams(("arbitrary", "arbitrary")),
        name="mlstm",
    )(p_main, p_main, p_main, p_main, p_gate, g_rows, conv_w, conv_w, conv_b.reshape(1, -1),
      conv_b.reshape(1, -1), gb_col, gb_row, jnp.concatenate([m_w, m_w]).reshape(1, LANES))


def _rms_norm(x, g):
    xf = x.astype(F32)
    y = xf * lax.rsqrt(jnp.mean(xf * xf, axis=-1, keepdims=True) + NORM_EPS)
    return (y * g.astype(F32)).astype(x.dtype)


def _split_heads(t, n_heads):
    return t.reshape(t.shape[0], t.shape[1], n_heads, HEAD_DIM)


def _to_bhtd(t):
    return jnp.transpose(t, (0, 2, 1, 3))


def _merge_heads(t):
    b, h, t_len, d = t.shape
    return jnp.transpose(t, (0, 2, 1, 3)).reshape(b, t_len, h * d)


def _flip_t(t):
    return jnp.flip(t, axis=2)


def _dwconv_centred(x, w, b):
    k_size = w.shape[0]
    pad = k_size // 2
    t_len = x.shape[1]
    xp = jnp.pad(x, ((0, 0), (pad, pad), (0, 0)))
    y = b
    for i in range(k_size):
        y = y + xp[:, i:i + t_len] * w[i]
    return y


def _axial_rope_tables(t_len):
    pos = jnp.arange(t_len)
    row = (pos // GRID_W).astype(F32)
    col = (pos % GRID_W).astype(F32)
    n = HEAD_DIM // 4
    inv = ROPE_BASE ** (-jnp.arange(n, dtype=F32) / n)
    ar = row[:, None] * inv
    ac = col[:, None] * inv
    return (jnp.cos(ar), jnp.sin(ar), jnp.cos(ac), jnp.sin(ac))


def _rotate_half(x, cos, sin):
    x1, x2 = jnp.split(x, 2, axis=-1)
    return jnp.concatenate([x1 * cos - x2 * sin, x2 * cos + x1 * sin], axis=-1)


def _axial_rope(x, cos_r, sin_r, cos_c, sin_c):
    xr, xc = jnp.split(x, 2, axis=-1)
    return jnp.concatenate([_rotate_half(xr, cos_r, sin_r), _rotate_half(xc, cos_c, sin_c)], axis=-1)


def _mlstm_chunkwise(q, k, v, log_i, log_f, state0):
    b_sz, h_sz, t_len, d = q.shape
    L = MLSTM_CHUNK
    nc = t_len // L
    qc = q.reshape(b_sz, h_sz, nc, L, d)
    kc = k.reshape(b_sz, h_sz, nc, L, d)
    vc = v.reshape(b_sz, h_sz, nc, L, d)
    li = log_i.reshape(b_sz, h_sz, nc, L)
    bcum = jnp.cumsum(log_f.reshape(b_sz, h_sz, nc, L), axis=-1)
    b_tot = bcum[..., -1]
    tri = jnp.tril(jnp.ones((L, L), dtype=bool))
    d_log = jnp.where(tri, bcum[..., :, None] - bcum[..., None, :] + li[..., None, :], NEG_INF)
    m_intra = jnp.max(d_log, axis=-1)
    s = jnp.einsum('bhcjd,bhcld->bhcjl', qc, kc).astype(F32) * jnp.exp(d_log - m_intra[..., None])
    num_intra = jnp.einsum('bhcjl,bhcld->bhcjd', s, vc)
    den_intra = jnp.sum(s, axis=-1)
    w_log = b_tot[..., None] - bcum + li
    m_loc = jnp.max(w_log, axis=-1)
    e = jnp.exp(w_log - m_loc[..., None])
    c_loc = jnp.einsum('bhcl,bhcld,bhcle->bhcde', e, kc, vc)
    n_loc = jnp.einsum('bhcl,bhcld->bhcd', e, kc)

    def step(carry, inp):
        c_st, n_st, m_st = carry
        cl, nl, ml, bt = inp
        m_new = jnp.maximum(bt + m_st, ml)
        a = jnp.exp(bt + m_st - m_new)
        g = jnp.exp(ml - m_new)
        new = (a[..., None, None] * c_st + g[..., None, None] * cl, a[..., None] * n_st + g[..., None] * nl, m_new)
        return new, carry

    xs = (jnp.moveaxis(c_loc, 2, 0), jnp.moveaxis(n_loc, 2, 0), jnp.moveaxis(m_loc, 2, 0), jnp.moveaxis(b_tot, 2, 0))
    final, prev = lax.scan(step, state0, xs)
    c_prev = jnp.moveaxis(prev[0], 0, 2)
    n_prev = jnp.moveaxis(prev[1], 0, 2)
    m_prev = jnp.moveaxis(prev[2], 0, 2)
    inter_log = bcum + m_prev[..., None]
    m_q = jnp.maximum(m_intra, inter_log)
    a = jnp.exp(inter_log - m_q)
    g = jnp.exp(m_intra - m_q)
    num = a[..., None] * jnp.einsum('bhcjd,bhcde->bhcje', qc, c_prev) + g[..., None] * num_intra
    den = a * jnp.einsum('bhcjd,bhcd->bhcj', qc, n_prev) + g * den_intra
    h = num / jnp.maximum(jnp.abs(den), jnp.exp(-m_q))[..., None]
    return h.reshape(b_sz, h_sz, t_len, d), final


def _retention_log_decay(first_exp, n_heads):
    e = first_exp + 2.0 * jnp.arange(n_heads, dtype=F32)
    return jnp.log1p(-jnp.exp2(-e))


def _retention_chunkwise(q, k, v, log_gamma, state0):
    b_sz, h_sz, t_len, d = q.shape
    L = RET_CHUNK
    nc = t_len // L
    qc = q.reshape(b_sz, h_sz, nc, L, d)
    kc = k.reshape(b_sz, h_sz, nc, L, d)
    vc = v.reshape(b_sz, h_sz, nc, L, d)
    pos = jnp.arange(L, dtype=F32)
    diff = pos[:, None] - pos[None, :]
    decay = jnp.where(diff >= 0, jnp.exp(log_gamma[:, None, None] * jnp.maximum(diff, 0.0)), 0.0)
    s = jnp.einsum('bhcjd,bhcld->bhcjl', qc, kc).astype(F32) * decay[:, None]
    intra = jnp.einsum('bhcjl,bhcld->bhcjd', s, vc)
    zeta = jnp.exp(log_gamma[:, None] * (L - 1 - pos))
    s_loc = jnp.einsum('hl,bhcld,bhcle->bhcde', zeta, kc, vc)
    g_chunk = jnp.exp(log_gamma * L)[:, None, None]

    def step(r, sl):
        return g_chunk * r + sl, r

    final, r_prev = lax.scan(step, state0, jnp.moveaxis(s_loc, 2, 0))
    r_prev = jnp.moveaxis(r_prev, 0, 2)
    xi = jnp.exp(log_gamma[:, None] * (pos + 1.0))
    inter = jnp.einsum('bhcjd,bhcde->bhcje', qc, r_prev) * xi[:, None, :, None]
    return (intra + inter).reshape(b_sz, h_sz, t_len, d), final


def _na_window(rows):
    wr = min(NA_ROWS, rows)
    r = jnp.arange(rows)
    col = jnp.arange(GRID_W)
    row_idx = jnp.clip(r - wr // 2, 0, rows - wr)[:, None] + jnp.arange(wr)[None, :]
    col_start = jnp.clip(col - NA_COLS // 2, 0, GRID_W - NA_COLS)
    col_in = (col[None, :] >= col_start[:, None]) & (col[None, :] < col_start[:, None] + NA_COLS)
    dr = row_idx - r[:, None] + NA_ROWS - 1
    dc = jnp.clip(col[None, :] - col[:, None] + NA_COLS - 1, 0, 2 * NA_COLS - 2)
    return (row_idx, col_in, dr[:, None, :, None], dc[None, :, None, :])


def _na_latent(q, k, v, k_ctx, v_ctx, row_idx, col_in, bias):
    t_len, h_sz, d = q.shape
    rows, wr = row_idx.shape
    scale = d ** -0.5
    qg = q.reshape(rows, GRID_W, h_sz, d)
    kb = k.reshape(rows, GRID_W, h_sz, d)[row_idx]
    vb = v.reshape(rows, GRID_W, h_sz, d)[row_idx]
    s_loc = jnp.einsum('rqhd,rjwhd->hrqjw', qg, kb).astype(F32) * scale + bias
    s_loc = jnp.where(col_in[:, None, :], s_loc, NEG_INF)
    s_ctx = jnp.einsum('rqhd,chd->hrqc', qg, k_ctx).astype(F32) * scale
    n_loc = wr * GRID_W
    s = jnp.concatenate([s_loc.reshape(h_sz, rows, GRID_W, n_loc), s_ctx], axis=-1)
    p = jax.nn.softmax(s, axis=-1).astype(v.dtype)
    p_loc = p[..., :n_loc].reshape(h_sz, rows, GRID_W, wr, GRID_W)
    o = jnp.einsum('hrqjw,rjwhd->rqhd', p_loc, vb) + jnp.einsum('hrqc,chd->rqhd', p[..., n_loc:], v_ctx)
    return o.reshape(t_len, h_sz * d)


def _ctx_attention(q, k, v):
    s = jnp.einsum('bqhd,bkhd->bhqk', q, k).astype(F32) * HEAD_DIM ** -0.5
    p = jax.nn.softmax(s, axis=-1).astype(v.dtype)
    o = jnp.einsum('bhqk,bkhd->bqhd', p, v)
    return o.reshape(o.shape[0], o.shape[1], -1)


def _hybrid_mixer(xp, cp, rope, na_win, conv_w, conv_b, gate_b, m_norm, q_norm, k_norm, rpb, r_norm, dims):
    h_m, h_na, h_r = dims
    b_sz = xp[0].shape[0]

    def mlstm_prep(p):
        qk = jax.nn.silu(_dwconv_centred(p[0], conv_w, conv_b))
        q, k = jnp.split(qk, 2, axis=-1)
        g = (p[3] + gate_b).astype(F32)
        g = jnp.transpose(g.reshape(b_sz, -1, 4, h_m), (2, 0, 3, 1))
        return (_to_bhtd(_split_heads(q, h_m)), _to_bhtd(_split_heads(k, h_m)) * HEAD_DIM ** -0.5,
                _to_bhtd(_split_heads(p[1], h_m)), g[0], jax.nn.log_sigmoid(g[1]), g[2], jax.nn.log_sigmoid(g[3]))

    qx, kx, vx, ix_f, lfx_f, ix_b, lfx_b = mlstm_prep(xp)
    qc, kc, vc, ic_f, lfc_f, ic_b, lfc_b = mlstm_prep(cp)
    zero_m = (jnp.zeros((b_sz, h_m, HEAD_DIM, HEAD_DIM), F32),
              jnp.zeros((b_sz, h_m, HEAD_DIM), F32), jnp.zeros((b_sz, h_m), F32))
    hc_f, st_f = _mlstm_chunkwise(qc, kc, vc, ic_f, lfc_f, zero_m)
    hc_b, st_b = _mlstm_chunkwise(_flip_t(qc), _flip_t(kc), _flip_t(vc), _flip_t(ic_b), _flip_t(lfc_b), zero_m)
    hx_f, _ = _mlstm_chunkwise(qx, kx, vx, ix_f, lfx_f, st_f)
    hx_b, _ = _mlstm_chunkwise(_flip_t(qx), _flip_t(kx), _flip_t(vx), _flip_t(ix_b), _flip_t(lfx_b), st_b)

    def mlstm_out(h, o):
        return _merge_heads(_rms_norm(h, m_norm)).astype(o.dtype) * jax.nn.sigmoid(o)

    a_x = mlstm_out(hx_f + _flip_t(hx_b), xp[2])

    def na_prep(p):
        return (_rms_norm(_split_heads(p[4], h_na), q_norm), _rms_norm(_split_heads(p[5], h_na), k_norm),
                _split_heads(p[6], h_na))

    nqx, nkx, nvx = na_prep(xp)
    nqc, nkc, nvc = na_prep(cp)
    row_idx, col_in, dr, dc = na_win
    bias = rpb[:, dr, dc].astype(F32)
    b_x = lax.map(lambda a: _na_latent(a[0], a[1], a[2], a[3], a[4], row_idx, col_in, bias),
                  (nqx, nkx, nvx, nkc, nvc))

    lg_f = _retention_log_decay(5.0, h_r)
    lg_b = _retention_log_decay(6.0, h_r)
    rqx = _axial_rope(_to_bhtd(_split_heads(xp[7], h_r)), *rope)
    rkx = _axial_rope(_to_bhtd(_split_heads(xp[8], h_r)), *rope) * HEAD_DIM ** -0.5
    rvx = _to_bhtd(_split_heads(xp[9], h_r))
    rqc = _to_bhtd(_split_heads(cp[7], h_r))
    rkc = _to_bhtd(_split_heads(cp[8], h_r)) * HEAD_DIM ** -0.5
    rvc = _to_bhtd(_split_heads(cp[9], h_r))
    zero_r = jnp.zeros((b_sz, h_r, HEAD_DIM, HEAD_DIM), F32)
    rc_f, rs_f = _retention_chunkwise(rqc, rkc, rvc, lg_f, zero_r)
    rc_b, rs_b = _retention_chunkwise(_flip_t(rqc), _flip_t(rkc), _flip_t(rvc), lg_b, zero_r)
    rx_f, _ = _retention_chunkwise(rqx, rkx, rvx, lg_f, rs_f)
    rx_b, _ = _retention_chunkwise(_flip_t(rqx), _flip_t(rkx), _flip_t(rvx), lg_b, rs_b)

    def ret_out(h, g):
        return _merge_heads(_rms_norm(h, r_norm)).astype(g.dtype) * jax.nn.silu(g)

    c_x = ret_out(rx_f + _flip_t(rx_b), xp[10])
    mix_x = jnp.concatenate([a_x, b_x, c_x], axis=-1)
    a_c = mlstm_out(hc_f + _flip_t(hc_b), cp[2])
    b_c = _ctx_attention(nqc, nkc, nvc)
    c_c = ret_out(rc_f + _flip_t(rc_b), cp[10])
    return mix_x, jnp.concatenate([a_c, b_c, c_c], axis=-1)


def kernel(x, c, ctx, c_ctx, w_mod, b_mod, norm_mix, norm_ffn, w_in, w_out, mlstm_conv_w, mlstm_conv_b,
           mlstm_gate_b, mlstm_norm, na_q_norm, na_k_norm, na_rpb, ret_norm, router_w, router_b,
           expert_w_up, expert_b_up, expert_w_down, expert_b_down):
    b_sz, t_len, d = x.shape
    t_ctx = ctx.shape[1]
    depth = w_in.shape[0]
    n_e = router_w.shape[2]
    d_mix = w_out.shape[1]
    h_m = d_mix // (4 * HEAD_DIM)
    h_na = d_mix // (2 * HEAD_DIM)
    h_r = d_mix // (4 * HEAD_DIM)
    d_m, d_na, d_r = h_m * HEAD_DIM, h_na * HEAD_DIM, h_r * HEAD_DIM
    n_gate = 4 * h_m
    assert b_sz + 1 <= MOD_ROWS and n_e <= LANES and n_gate <= LANES
    s_len = t_ctx + t_len
    assert s_len % ROW_TILE == 0

    s = jnp.concatenate([ctx, x], axis=1)
    cc = jnp.zeros((MOD_ROWS, d), F32).at[:b_sz].set(c).at[b_sz].set(c_ctx)
    mods = _modulation(cc, w_mod, b_mod).reshape(depth, MOD_ROWS, 1, 6 * d)

    g0 = 4 * d_m
    col_na = g0
    col_ret = col_na + 3 * d_na
    n_hp = h_m // 2
    gate_src = np.array([[g0 + t * h_m + 2 * hp + h2 for t in range(N_GATE_TYPES) for h2 in range(2)]
                         for hp in range(n_hp)])
    rope_tabs = _rope_tables(t_ctx, t_len)
    ret_tabs = _ret_tables(h_r)
    na_tabs = _na_bias_tables(na_rpb)
    f2 = expert_w_up.shape[3]
    w_up_all = expert_w_up.reshape(depth * n_e, d, f2)
    w_down_all = expert_w_down.reshape(depth * n_e, f2 // 2, d)
    b_up_all = _regroup_bias(expert_b_up).reshape(depth * n_e, 1, f2)
    b_down_all = expert_b_down.reshape(depth * n_e, 1, d)

    experts_out = None
    for l in range(depth):
        w_main = jnp.concatenate([w_in[l, :, :g0], w_in[l, :, g0 + n_gate:]], axis=1).astype(BF16)
        w_gate = jnp.zeros((d, n_hp, LANES), BF16).at[:, :, :gate_src.shape[1]].set(
            w_in[l][:, gate_src].astype(BF16)).reshape(d, n_hp * LANES)
        if experts_out is None:
            p_main, p_gate = _proj_in(s, mods[l], norm_mix[l], w_main, w_gate, t_ctx)
        else:
            s, p_main, p_gate = _proj_in(s, mods[l], norm_mix[l], w_main, w_gate, t_ctx,
                                         combine=(*experts_out, mods[l - 1]))

        mix = [_mlstm(p_main, p_gate, mlstm_conv_w[l], mlstm_conv_b[l], mlstm_gate_b[l], mlstm_norm[l],
                      t_ctx, h_m),
               _na_attention(p_main, na_q_norm[l], na_k_norm[l], na_tabs[l], t_ctx, col_na, h_na),
               _retention(p_main, rope_tabs, ret_tabs, ret_norm[l], t_ctx, col_ret, h_r)]

        rw = jnp.zeros((d, LANES), F32).at[:, :n_e].set(router_w[l])
        rb = jnp.zeros((1, LANES), F32).at[0, :n_e].set(router_b[l])
        s, tok, logits = _proj_out(mix, s, mods[l], norm_ffn[l], w_out[l].astype(BF16), rw, rb, t_ctx)

        experts_out = _moe(tok, logits[..., :n_e], n_e, l, w_up_all, w_down_all, b_up_all, b_down_all)
    s = _combine(*experts_out, s, mods[depth - 1], t_ctx)
    return s[:, t_ctx:]
```

```python
import functools

import jax
import jax.numpy as jnp
import numpy as np
from jax import lax
from jax.experimental import pallas as pl
from jax.experimental.pallas import tpu as pltpu

F32 = jnp.float32
BF16 = jnp.bfloat16

GRID_W = 64
HEAD_DIM = 64
NA_ROWS = 8
NA_COLS = 16
ROPE_BASE = 10000.0
TOP_K = 4
SWIGLU_ALPHA = 1.702
SWIGLU_LIMIT = 7.0
NORM_EPS = 1e-6
NEG_INF = -1e30

LANES = 128
SUBLANES = 8
VMEM_LIMIT = 48 * 1024 * 1024
MOD_ROWS = 8
MOD_COL_TILES = 4
ROW_TILE = 768
COMBINE_ROWS = 256
MOE_TILE = 512
REGROUP_ROWS = 512
MOE_VMEM_LIMIT = 56 * 1024 * 1024


def _cparams(sem):
    return pltpu.CompilerParams(dimension_semantics=sem, vmem_limit_bytes=VMEM_LIMIT)


def _mod_kernel(cc_ref, w_ref, b_ref, o_ref):
    cc = cc_ref[...]
    a = cc * jax.nn.sigmoid(cc)
    o_ref[0] = jnp.dot(a, w_ref[0], precision=lax.Precision.HIGHEST,
                       preferred_element_type=F32) + b_ref[0]


def _modulation(cc, w_mod, b_mod):
    n_l, d, d6 = w_mod.shape
    tn = d6 // MOD_COL_TILES
    return pl.pallas_call(
        _mod_kernel,
        grid=(n_l, MOD_COL_TILES),
        in_specs=[pl.BlockSpec((MOD_ROWS, d), lambda l, j: (0, 0)),
                  pl.BlockSpec((1, d, tn), lambda l, j: (l, 0, j)),
                  pl.BlockSpec((1, 1, tn), lambda l, j: (l, 0, j))],
        out_specs=pl.BlockSpec((1, MOD_ROWS, tn), lambda l, j: (l, 0, j)),
        out_shape=jax.ShapeDtypeStruct((n_l, MOD_ROWS, d6), F32),
        compiler_params=_cparams(("arbitrary", "arbitrary")),
        name="adaln_modulation",
    )(cc, w_mod, b_mod.reshape(n_l, 1, d6))


def _pick_mod(mb_ref, mc_ref, k, d, is_ctx):
    vb = mb_ref[0, :, k * d:(k + 1) * d]
    vc = mc_ref[0, :, k * d:(k + 1) * d]
    return jnp.where(is_ctx, vc, vb)


def _ctx_rows(tm, t_ctx):
    return lax.broadcasted_iota(jnp.int32, (tm, 1), 0) + pl.program_id(1) * tm < t_ctx


def _rms(x, g):
    return x * lax.rsqrt(jnp.mean(x * x, axis=-1, keepdims=True) + NORM_EPS) * g


def _pack_bf16_pairs(x):
    m = x.shape[1] // 2
    hi = lax.bitcast_convert_type(x[:, :m].astype(BF16).astype(F32), jnp.uint32)
    lo = lax.bitcast_convert_type(x[:, m:].astype(BF16).astype(F32), jnp.uint32)
    return hi | (lo >> 16)


def _unpack_bf16_pairs(p):
    hi = lax.bitcast_convert_type(p & jnp.uint32(0xFFFF0000), F32)
    lo = lax.bitcast_convert_type(p << 16, F32)
    return jnp.concatenate([hi, lo], axis=1)


PROJ_IN_ROWS = 384
PROJ_IN_COLS = 512


def _project(s, is_ctx, mb_ref, mc_ref, g_ref, w_ref, wg_ref, pm_ref, pg_ref):
    d = s.shape[1]
    sh = _pick_mod(mb_ref, mc_ref, 0, d, is_ctx)
    sc = _pick_mod(mb_ref, mc_ref, 1, d, is_ctx)
    xn = (_rms(s, g_ref[...]) * (1.0 + sc) + sh).astype(BF16)
    pg_ref[0] = jnp.dot(xn, wg_ref[...], preferred_element_type=F32)
    for j in range(w_ref.shape[1] // PROJ_IN_COLS):
        cols = slice(j * PROJ_IN_COLS, (j + 1) * PROJ_IN_COLS)
        pm_ref[0, :, cols] = jnp.dot(xn, w_ref[:, cols], preferred_element_type=F32)


def _proj_in_kernel(s_ref, mb_ref, mc_ref, g_ref, w_ref, wg_ref, pm_ref, pg_ref, *, t_ctx):
    _project(s_ref[0], _ctx_rows(s_ref.shape[1], t_ctx), mb_ref, mc_ref, g_ref, w_ref, wg_ref, pm_ref, pg_ref)


def _combine_rows(y_ref, gt_ref, s_ref, mb_ref, mc_ref, is_ctx):
    g2 = _pick_mod(mb_ref, mc_ref, 5, s_ref.shape[2], is_ctx)
    gt = gt_ref[0]
    y = _unpack_bf16_pairs(y_ref[0, 0]) * gt[:, 0:1]
    for k in range(1, y_ref.shape[0]):
        y = y + _unpack_bf16_pairs(y_ref[k, 0]) * gt[:, k:k + 1]
    return s_ref[0] + g2 * y


def _combine_proj_in_kernel(y_ref, gt_ref, s_ref, pmb_ref, pmc_ref, mb_ref, mc_ref, g_ref, w_ref, wg_ref,
                            so_ref, pm_ref, pg_ref, *, t_ctx):
    is_ctx = _ctx_rows(s_ref.shape[1], t_ctx)
    s_new = _combine_rows(y_ref, gt_ref, s_ref, pmb_ref, pmc_ref, is_ctx)
    so_ref[0] = s_new
    _project(s_new, is_ctx, mb_ref, mc_ref, g_ref, w_ref, wg_ref, pm_ref, pg_ref)


def _proj_in(s, mod_l, g, w_main, w_gate, t_ctx, combine=None):
    b_sz, s_len, d = s.shape
    n_main = w_main.shape[1]
    n_gate = w_gate.shape[1]
    tm = PROJ_IN_ROWS
    assert s_len % tm == 0 and n_main % PROJ_IN_COLS == 0
    row_spec = pl.BlockSpec((1, tm, d), lambda b, i: (b, i, 0))
    mod_specs = [pl.BlockSpec((1, 1, 6 * d), lambda b, i: (b, 0, 0)),
                 pl.BlockSpec((1, 1, 6 * d), lambda b, i: (b_sz, 0, 0))]
    w_specs = [pl.BlockSpec((1, d), lambda b, i: (0, 0)),
               pl.BlockSpec((d, n_main), lambda b, i: (0, 0)),
               pl.BlockSpec((d, n_gate), lambda b, i: (0, 0))]
    out_specs = [pl.BlockSpec((1, tm, n_main), lambda b, i: (b, i, 0)),
                 pl.BlockSpec((1, tm, n_gate), lambda b, i: (b, i, 0))]
    out_shape = [jax.ShapeDtypeStruct((b_sz, s_len, n_main), F32),
                 jax.ShapeDtypeStruct((b_sz, s_len, n_gate), F32)]
    if combine is None:
        return pl.pallas_call(
            functools.partial(_proj_in_kernel, t_ctx=t_ctx),
            grid=(b_sz, s_len // tm),
            in_specs=[row_spec] + mod_specs + w_specs,
            out_specs=out_specs, out_shape=out_shape,
            compiler_params=_cparams(("arbitrary", "arbitrary")),
            name="proj_in",
        )(s, mod_l, mod_l, g.reshape(1, d), w_main, w_gate)
    y_as, gates, mod_prev = combine
    n_k = y_as.shape[0]
    return pl.pallas_call(
        functools.partial(_combine_proj_in_kernel, t_ctx=t_ctx),
        grid=(b_sz, s_len // tm),
        in_specs=[pl.BlockSpec((n_k, 1, tm, d // 2), lambda b, i: (0, b, i, 0)),
                  pl.BlockSpec((1, tm, n_k), lambda b, i: (b, i, 0)),
                  row_spec] + mod_specs + mod_specs + w_specs,
        out_specs=[row_spec] + out_specs,
        out_shape=[jax.ShapeDtypeStruct((b_sz, s_len, d), F32)] + out_shape,
        compiler_params=_cparams(("arbitrary", "arbitrary")),
        name="combine_proj_in",
    )(y_as, gates, s, mod_prev, mod_prev, mod_l, mod_l, g.reshape(1, d), w_main, w_gate)


def _proj_out_kernel(ma_ref, mb2_ref, mc2_ref, s_ref, mb_ref, mc_ref, g_ref, w_ref, rw_ref, rb_ref,
                     so_ref, tok_ref, lg_ref, *, t_ctx):
    d = s_ref.shape[2]
    is_ctx = _ctx_rows(s_ref.shape[1], t_ctx)
    g1 = _pick_mod(mb_ref, mc_ref, 2, d, is_ctx)
    ka, kb = ma_ref.shape[2], ma_ref.shape[2] + mb2_ref.shape[2]
    y = (jnp.dot(ma_ref[0], w_ref[0:ka, :], preferred_element_type=F32)
         + jnp.dot(mb2_ref[0], w_ref[ka:kb, :], preferred_element_type=F32)
         + jnp.dot(mc2_ref[0], w_ref[kb:, :], preferred_element_type=F32))
    s_new = s_ref[0] + g1 * y
    so_ref[0] = s_new
    sh = _pick_mod(mb_ref, mc_ref, 3, d, is_ctx)
    sc = _pick_mod(mb_ref, mc_ref, 4, d, is_ctx)
    t = _rms(s_new, g_ref[...]) * (1.0 + sc) + sh
    tok_ref[0] = t
    w = rw_ref[...]
    t_hi, w_hi = t.astype(BF16), w.astype(BF16)
    t_lo = (t - t_hi.astype(F32)).astype(BF16)
    w_lo = (w - w_hi.astype(F32)).astype(BF16)
    lg_ref[0] = (jnp.dot(t_hi, w_hi, preferred_element_type=F32) + jnp.dot(t_hi, w_lo, preferred_element_type=F32)
                 + jnp.dot(t_lo, w_hi, preferred_element_type=F32) + rb_ref[...])


def _proj_out(mix_parts, s, mod_l, g, w_out, rw, rb, t_ctx):
    b_sz, s_len, d = s.shape
    tm = ROW_TILE
    row_spec = pl.BlockSpec((1, tm, d), lambda b, i: (b, i, 0))
    part_specs = [pl.BlockSpec((1, tm, m.shape[2]), lambda b, i: (b, i, 0)) for m in mix_parts]
    return pl.pallas_call(
        functools.partial(_proj_out_kernel, t_ctx=t_ctx),
        grid=(b_sz, s_len // tm),
        in_specs=part_specs + [row_spec,
                               pl.BlockSpec((1, 1, 6 * d), lambda b, i: (b, 0, 0)),
                               pl.BlockSpec((1, 1, 6 * d), lambda b, i: (b_sz, 0, 0)),
                               pl.BlockSpec((1, d), lambda b, i: (0, 0)),
                               pl.BlockSpec((d, d), lambda b, i: (0, 0)),
                               pl.BlockSpec((d, LANES), lambda b, i: (0, 0)),
                               pl.BlockSpec((1, LANES), lambda b, i: (0, 0))],
        out_specs=[row_spec, row_spec, pl.BlockSpec((1, tm, LANES), lambda b, i: (b, i, 0))],
        out_shape=[jax.ShapeDtypeStruct((b_sz, s_len, d), F32),
                   jax.ShapeDtypeStruct((b_sz, s_len, d), F32),
                   jax.ShapeDtypeStruct((b_sz, s_len, LANES), F32)],
        compiler_params=_cparams(("arbitrary", "arbitrary")),
        name="proj_out_router",
    )(*mix_parts, s, mod_l, mod_l, g.reshape(1, d), w_out, rw, rb)


PAIR = 2 * LANES


def _regroup_perm():
    dst = np.arange(PAIR)
    src = np.where(dst < LANES, 2 * dst, 2 * (dst - LANES) + 1)
    return jnp.asarray(np.arange(PAIR)[:, None] == src[None, :], BF16)


def _regroup_bias(b_up):
    lead = b_up.shape[:-1]
    b = b_up.reshape(*lead, -1, LANES, 2)
    return jnp.swapaxes(b, -1, -2).reshape(*lead, -1)


def _moe_kernel(be_ref, nu_ref, x_ref, wu_ref, wd_ref, bu_ref, bd_ref, p_ref, y_ref, wus_ref, wds_ref):
    i = pl.program_id(0)
    d, f2 = wus_ref.shape
    rows = min(REGROUP_ROWS, d)

    @pl.when(i < nu_ref[0])
    def _():
        @pl.when((i == 0) | (be_ref[i] != be_ref[jnp.maximum(i - 1, 0)]))
        def _():
            for r in range(d // rows):
                for j in range(f2 // PAIR):
                    w = wu_ref[0, r * rows:(r + 1) * rows, j * PAIR:(j + 1) * PAIR].astype(BF16)
                    wus_ref[r * rows:(r + 1) * rows, j * PAIR:(j + 1) * PAIR] = jnp.dot(
                        w, p_ref[...], preferred_element_type=F32).astype(BF16)
            wds_ref[...] = wd_ref[0].astype(BF16)

        up = jnp.dot(x_ref[...].astype(BF16), wus_ref[...], preferred_element_type=F32) + bu_ref[0]
        acts = []
        for j in range(f2 // PAIR):
            glu = jnp.minimum(up[:, j * PAIR:j * PAIR + LANES], SWIGLU_LIMIT)
            lin = jnp.clip(up[:, j * PAIR + LANES:(j + 1) * PAIR], -SWIGLU_LIMIT, SWIGLU_LIMIT)
            acts.append((glu * jax.nn.sigmoid(SWIGLU_ALPHA * glu) * (lin + 1.0)).astype(BF16))
        act = jnp.concatenate(acts, axis=1)
        y_ref[...] = _pack_bf16_pairs(jnp.dot(act, wds_ref[...], preferred_element_type=F32) + bd_ref[0])

    @pl.when(i >= nu_ref[0])
    def _():
        y_ref[...] = jnp.zeros_like(y_ref)


def _moe_blocks(blk_e, n_used, x_sorted, w_up, w_down, b_up, b_down):
    n_rows, d = x_sorted.shape
    _, _, f2 = w_up.shape
    assert f2 % PAIR == 0 and d % min(REGROUP_ROWS, d) == 0
    tm = MOE_TILE
    wmap = lambda i, be, nu: (be[i], 0, 0)
    return pl.pallas_call(
        _moe_kernel,
        grid_spec=pltpu.PrefetchScalarGridSpec(
            num_scalar_prefetch=2,
            grid=(n_rows // tm,),
            in_specs=[pl.BlockSpec((tm, d), lambda i, be, nu: (i, 0)),
                      pl.BlockSpec((1, d, f2), wmap),
                      pl.BlockSpec((1, f2 // 2, d), wmap),
                      pl.BlockSpec((1, 1, f2), wmap),
                      pl.BlockSpec((1, 1, d), wmap),
                      pl.BlockSpec((PAIR, PAIR), lambda i, be, nu: (0, 0))],
            out_specs=pl.BlockSpec((tm, d // 2), lambda i, be, nu: (i, 0)),
            scratch_shapes=[pltpu.VMEM((d, f2), BF16), pltpu.VMEM((f2 // 2, d), BF16)]),
        out_shape=jax.ShapeDtypeStruct((n_rows, d // 2), jnp.uint32),
        compiler_params=pltpu.CompilerParams(dimension_semantics=("arbitrary",),
                                             vmem_limit_bytes=MOE_VMEM_LIMIT),
        name="moe_expert_blocks",
    )(blk_e, n_used, x_sorted, w_up, w_down, b_up, b_down, _regroup_perm())


def _combine_kernel(y_ref, gt_ref, s_ref, mb_ref, mc_ref, o_ref, *, t_ctx):
    o_ref[0] = _combine_rows(y_ref, gt_ref, s_ref, mb_ref, mc_ref, _ctx_rows(s_ref.shape[1], t_ctx))


def _combine(y_as, gates, s, mod_l, t_ctx):
    b_sz, s_len, d = s.shape
    n_k = y_as.shape[0]
    tm = COMBINE_ROWS
    row_spec = pl.BlockSpec((1, tm, d), lambda b, i: (b, i, 0))
    return pl.pallas_call(
        functools.partial(_combine_kernel, t_ctx=t_ctx),
        grid=(b_sz, s_len // tm),
        in_specs=[pl.BlockSpec((n_k, 1, tm, d // 2), lambda b, i: (0, b, i, 0)),
                  pl.BlockSpec((1, tm, n_k), lambda b, i: (b, i, 0)),
                  row_spec,
                  pl.BlockSpec((1, 1, 6 * d), lambda b, i: (b, 0, 0)),
                  pl.BlockSpec((1, 1, 6 * d), lambda b, i: (b_sz, 0, 0))],
        out_specs=row_spec,
        out_shape=jax.ShapeDtypeStruct((b_sz, s_len, d), F32),
        compiler_params=_cparams(("arbitrary", "arbitrary")),
        name="moe_combine",
    )(y_as, gates, s, mod_l, mod_l)


def _moe(tok, logits, n_e, layer, w_up, w_down, b_up, b_down):
    b_sz, s_len, d = tok.shape
    n_tok = b_sz * s_len
    tm = MOE_TILE
    top_v, top_e = lax.top_k(logits.reshape(n_tok, n_e), TOP_K)
    gates = jax.nn.softmax(top_v, axis=-1)
    n_as = n_tok * TOP_K
    onehot = jnp.sum((top_e[:, :, None] == jnp.arange(n_e)[None, None, :]).astype(jnp.int32), axis=1)
    csum = jnp.cumsum(onehot, axis=0)
    counts = csum[-1]
    padded = (counts + tm - 1) // tm * tm
    end_pad = jnp.cumsum(padded)
    start_pad = end_pad - padded
    start = jnp.cumsum(counts) - counts
    dest = jnp.take_along_axis(csum - onehot + start_pad[None, :], top_e, axis=1).astype(jnp.int32)
    n_blocks = -(-n_as // tm) + n_e
    blk_first = jnp.arange(n_blocks) * tm
    blk_e = jnp.minimum(jnp.sum(blk_first[:, None] >= end_pad[None, :], axis=1), n_e - 1).astype(jnp.int32)
    n_used = (end_pad[-1:] // tm).astype(jnp.int32)
    tok_sorted = (jnp.argsort(top_e.reshape(n_as)) // TOP_K).astype(jnp.int32)
    j = (blk_first - start_pad[blk_e])[:, None] + jnp.arange(tm)[None, :]
    src = jnp.clip(start[blk_e][:, None] + j, 0, n_as - 1)
    row_tok = jnp.where(j < counts[blk_e][:, None], tok_sorted[src], 0).reshape(n_blocks * tm)
    x_sorted = tok.reshape(n_tok, d)[row_tok]
    y = _moe_blocks(blk_e + layer * n_e, n_used, x_sorted, w_up, w_down, b_up, b_down)
    y_as = y[dest.T].reshape(TOP_K, b_sz, s_len, d // 2)
    return y_as, gates.reshape(b_sz, s_len, TOP_K)


MIXER_VMEM_LIMIT = 58 * 1024 * 1024


def _seq_spec(s_len, col, buffers=1):
    return pl.BlockSpec((1, s_len, LANES), lambda b, h: (b, 0, col + h), pipeline_mode=pl.Buffered(buffers))


def _mixer_cparams():
    return pltpu.CompilerParams(dimension_semantics=("arbitrary", "arbitrary"), vmem_limit_bytes=MIXER_VMEM_LIMIT)


def _head_rms(x, w, lo):
    xx = x * x
    s0 = jnp.sum(jnp.where(lo, xx, 0.0), axis=-1, keepdims=True)
    s1 = jnp.sum(jnp.where(lo, 0.0, xx), axis=-1, keepdims=True)
    inv = lax.rsqrt(jnp.where(lo, s0, s1) * (1.0 / HEAD_DIM) + NORM_EPS)
    return x * inv * w


def _reverse_chunk(i, nc, n_ctx):
    return jnp.where(i < n_ctx, n_ctx - 1 - i, nc - 1 - i + n_ctx)


NA_STEP = 256
NA_KEYS = NA_ROWS * GRID_W
NA_GROUPS_PER_ITER = 2
NA_PREP_UNROLL = 3
NA_SHIFT_UNROLL = 5


def _na_bias_tables(rpb):
    n_l, n_h, n_dr, n_dc = rpb.shape
    col = np.arange(GRID_W)
    dc = np.clip(col[None, :] - col[:, None] + NA_COLS - 1, 0, n_dc - 1)
    onehot = (dc.reshape(1, -1) == np.arange(n_dc)[:, None]).astype(np.float32)
    toe = jnp.dot(rpb.reshape(-1, n_dc), jnp.asarray(onehot), precision=lax.Precision.HIGHEST)
    toe = toe.reshape(n_l, n_h, n_dr, GRID_W, GRID_W)
    col_start = np.clip(col - NA_COLS // 2, 0, GRID_W - NA_COLS)
    col_in = (col[None, :] >= col_start[:, None]) & (col[None, :] < col_start[:, None] + NA_COLS)
    toe = jnp.where(jnp.asarray(col_in), toe, NEG_INF)
    tabs = []
    for dr0 in range(NA_ROWS):
        t = toe[:, :, dr0:dr0 + NA_ROWS].reshape(n_l, n_h // 2, 2, NA_ROWS, GRID_W, GRID_W)
        t = jnp.transpose(t, (0, 1, 2, 4, 3, 5))
        tabs.append(t.reshape(n_l, n_h // 2, 2 * GRID_W, NA_KEYS))
    return jnp.stack(tabs, axis=1)


def _na_seq_kernel(q_ref, k_ref, v_ref, qw_ref, kw_ref, bias_ref, o_ref, kt_ref, vb_ref, s_ref, kn_ref, *,
                   t_ctx, rows):
    s_len = k_ref.shape[1]
    lo = lax.broadcasted_iota(jnp.int32, (1, LANES), 1) < HEAD_DIM
    rows_per_step = NA_STEP // GRID_W

    def prep(c, carry):
        t0 = pl.multiple_of(c * NA_STEP, NA_STEP)
        kn = _head_rms(k_ref[0, pl.ds(t0, NA_STEP), :], kw_ref[...], lo).astype(BF16)
        kn_ref[pl.ds(t0, NA_STEP), :] = kn
        kt_ref[0, :, pl.ds(t0, NA_STEP)] = kn.T
        vb_ref[pl.ds(t0, NA_STEP), :] = jnp.concatenate(
            [v_ref[0, pl.ds(t0, NA_STEP), :].astype(BF16), jnp.ones((NA_STEP, LANES), BF16)], axis=1)
        return carry
    lax.fori_loop(0, s_len // NA_STEP, prep, 0, unroll=NA_PREP_UNROLL)

    def prep_shifted(c, carry):
        t0 = pl.multiple_of(c * LANES, LANES)
        kt_ref[1, :, pl.ds(t0, LANES)] = kn_ref[pl.ds(pl.multiple_of(t0 + GRID_W, GRID_W), LANES), :].T
        return carry
    lax.fori_loop(0, (s_len - GRID_W) // LANES, prep_shifted, 0, unroll=NA_SHIFT_UNROLL)

    kc_t = kt_ref[0, :, 0:t_ctx]
    vc = vb_ref[0:t_ctx, :]

    def queries(tok0):
        qn = _head_rms(q_ref[0, pl.ds(tok0, NA_STEP), :], qw_ref[...], lo) * (HEAD_DIM ** -0.5)
        q0 = jnp.where(lo, qn, 0.0).astype(BF16)
        q1 = jnp.where(lo, 0.0, qn).astype(BF16)
        return [jnp.concatenate([q0[g * GRID_W:(g + 1) * GRID_W], q1[g * GRID_W:(g + 1) * GRID_W]], axis=0)
                for g in range(rows_per_step)]

    def finish(tok0, g, o2):
        o2 = o2[:, :LANES] * (1.0 / o2[:, LANES:])
        o = jnp.where(lo, o2[:GRID_W], o2[GRID_W:])
        o_ref[0, pl.ds(tok0 + g * GRID_W, GRID_W), :] = o.astype(o_ref.dtype)

    for g, q2 in enumerate(queries(0)):
        s_c = jnp.dot(q2, kc_t, preferred_element_type=F32)
        p_c = jnp.exp(s_c - jnp.max(s_c, axis=-1, keepdims=True))
        finish(0, g, jnp.dot(p_c.astype(BF16), vc, preferred_element_type=F32))

    def row_group(rg, slot):
        tok0 = pl.multiple_of(t_ctx + rg * NA_STEP, NA_STEP)
        t0s = []
        s_slot = s_ref.at[slot]
        for g, q2 in enumerate(queries(tok0)):
            r = rg * rows_per_step + g
            row_start = jnp.clip(r - NA_ROWS // 2, 0, rows - NA_ROWS)
            dr0 = row_start - r + NA_ROWS - 1
            t0 = pl.multiple_of(t_ctx + row_start * GRID_W, GRID_W)
            odd = (t0 // GRID_W) % (LANES // GRID_W)
            kw_t = kt_ref[odd, :, pl.ds(pl.multiple_of(t0 - odd * GRID_W, LANES), NA_KEYS)]
            s_slot[g, :, :NA_KEYS] = jnp.dot(q2, kw_t, preferred_element_type=F32) + bias_ref[dr0, 0]
            s_slot[g, :, NA_KEYS:] = jnp.dot(q2, kc_t, preferred_element_type=F32)
            t0s.append(t0)
        for g in range(rows_per_step):
            s = s_slot[g]
            p = jnp.exp(s - jnp.max(s, axis=-1, keepdims=True)).astype(BF16)
            finish(tok0, g, jnp.dot(p[:, :NA_KEYS], vb_ref[pl.ds(t0s[g], NA_KEYS), :], preferred_element_type=F32)
                   + jnp.dot(p[:, NA_KEYS:], vc, preferred_element_type=F32))

    def row_groups(i, carry):
        for slot in range(NA_GROUPS_PER_ITER):
            row_group(i * NA_GROUPS_PER_ITER + slot, slot)
        return carry
    lax.fori_loop(0, rows // rows_per_step // NA_GROUPS_PER_ITER, row_groups, 0)


def _na_attention(p_main, q_w, k_w, bias_tab, t_ctx, col_q, n_heads):
    b_sz, s_len, _ = p_main.shape
    n_hg = n_heads // 2
    rows = (s_len - t_ctx) // GRID_W
    rows_per_step = NA_STEP // GRID_W
    assert t_ctx == NA_STEP and s_len % NA_STEP == 0 and rows >= NA_ROWS
    assert rows % (rows_per_step * NA_GROUPS_PER_ITER) == 0
    cq = col_q // LANES
    seq = lambda col: pl.BlockSpec((1, s_len, LANES), lambda b, h: (b, 0, col + h))
    w2 = lambda w: jnp.concatenate([w, w]).reshape(1, LANES)
    return pl.pallas_call(
        functools.partial(_na_seq_kernel, t_ctx=t_ctx, rows=rows),
        grid=(b_sz, n_hg),
        in_specs=[seq(cq), seq(cq + n_hg), _seq_spec(s_len, cq + 2 * n_hg),
                  pl.BlockSpec((1, LANES), lambda b, h: (0, 0)),
                  pl.BlockSpec((1, LANES), lambda b, h: (0, 0)),
                  pl.BlockSpec((NA_ROWS, 1, 2 * GRID_W, NA_KEYS), lambda b, h: (0, h, 0, 0))],
        out_specs=pl.BlockSpec((1, s_len, LANES), lambda b, h: (b, 0, h)),
        out_shape=jax.ShapeDtypeStruct((b_sz, s_len, n_heads * HEAD_DIM), BF16),
        scratch_shapes=[pltpu.VMEM((2, LANES, s_len), BF16), pltpu.VMEM((s_len, 2 * LANES), BF16),
                        pltpu.VMEM((NA_GROUPS_PER_ITER, rows_per_step, 2 * GRID_W, NA_KEYS + t_ctx), F32),
                        pltpu.VMEM((s_len, LANES), BF16)],
        compiler_params=_cparams(("arbitrary", "arbitrary")),
        name="na_attention",
    )(p_main, p_main, p_main, w2(q_w), w2(k_w), bias_tab)


RET_CHUNK_LEN = 256
RET_UNROLL = 3


def _rope_tables(t_ctx, t_len):
    pos = jnp.arange(t_len)
    row = (pos // GRID_W).astype(F32)
    col = (pos % GRID_W).astype(F32)
    n = HEAD_DIM // 4
    inv = ROPE_BASE ** (-jnp.arange(n, dtype=F32) / n)
    ar = row[:, None] * inv
    ac = col[:, None] * inv
    cos = jnp.concatenate([jnp.cos(ar), jnp.cos(ar), jnp.cos(ac), jnp.cos(ac)], axis=-1)
    sin = jnp.concatenate([-jnp.sin(ar), jnp.sin(ar), -jnp.sin(ac), jnp.sin(ac)], axis=-1)
    cos = jnp.concatenate([jnp.ones((t_ctx, HEAD_DIM), F32), cos], axis=0)
    sin = jnp.concatenate([jnp.zeros((t_ctx, HEAD_DIM), F32), sin], axis=0)
    return jnp.tile(cos, (1, 2)), jnp.tile(sin, (1, 2))


def _ret_tables(n_heads):
    L = RET_CHUNK_LEN
    pos = np.arange(L, dtype=np.float32)
    lane_head = np.arange(LANES) // HEAD_DIM
    decay = np.zeros((2, n_heads, L, L), np.float32)
    zeta = np.zeros((2, n_heads // 2, LANES, L), np.float32)
    xi = np.zeros((2, n_heads // 2, L, LANES), np.float32)
    gch = np.zeros((2, n_heads // 2, 1, LANES), np.float32)
    for d, first_exp in enumerate((5.0, 6.0)):
        e = np.float32(first_exp) + np.float32(2.0) * np.arange(n_heads, dtype=np.float32)
        lg = np.log1p(-np.exp2(-e)).astype(np.float32)
        diff = pos[:, None] - pos[None, :]
        if d == 1:
            diff = -diff
        for h in range(n_heads):
            decay[d, h] = np.where(diff >= 0, np.exp(lg[h] * np.maximum(diff, 0.0)), 0.0)
        for hp in range(n_heads // 2):
            lgl = lg[2 * hp + lane_head][None, :]
            to_end = (L - 1 - pos if d == 0 else pos)[:, None]
            zeta[d, hp] = np.exp(lgl * to_end).T
            xi[d, hp] = np.exp(lgl * (L - to_end))
            gch[d, hp] = np.exp(lgl * L)
    return tuple(jnp.asarray(a) for a in (decay, zeta, xi, gch))


def _ret_kernel(q_ref, k_ref, v_ref, g_ref, cos_ref, sin_ref, dec_ref, zeta_ref, xi_ref, gch_ref, rn_ref,
                o_ref, qr_ref, kt_ref, vb_ref, acc_ref, *, t_ctx):
    L = RET_CHUNK_LEN
    s_len = q_ref.shape[1]
    nc = s_len // L
    lane = lax.broadcasted_iota(jnp.int32, (1, LANES), 1)
    lo = lane < HEAD_DIM
    half = (lane & (HEAD_DIM // 4)) == 0
    rid = lax.broadcasted_iota(jnp.int32, (LANES, LANES), 0) < HEAD_DIM
    cid = lax.broadcasted_iota(jnp.int32, (LANES, LANES), 1) < HEAD_DIM
    same_head = rid == cid

    def rope(x, cos, sin):
        up = pltpu.roll(x, LANES - HEAD_DIM // 4, axis=1)
        dn = pltpu.roll(x, HEAD_DIM // 4, axis=1)
        return x * cos + jnp.where(half, up, dn) * sin

    def prep(c, carry):
        sl = pl.ds(pl.multiple_of(c * L, L), L)
        cos, sin = cos_ref[sl, :], sin_ref[sl, :]
        qr_ref[sl, :] = rope(q_ref[0, sl, :], cos, sin).astype(BF16)
        kt_ref[:, sl] = (rope(k_ref[0, sl, :], cos, sin) * HEAD_DIM ** -0.5).T.astype(BF16)
        vb_ref[sl, :] = v_ref[0, sl, :].astype(BF16)
        return carry
    lax.fori_loop(0, nc, prep, 0, unroll=RET_UNROLL)

    def chunk(d, c, state):
        sl = pl.ds(pl.multiple_of(c * L, L), L)
        q, kt, v = qr_ref[sl, :], kt_ref[:, sl], vb_ref[sl, :]
        inter = jnp.dot(q, state.astype(BF16), preferred_element_type=F32) * xi_ref[d, 0]
        outs = []
        for h2 in range(2):
            qm = jnp.where(lo if h2 == 0 else jnp.logical_not(lo), q, jnp.zeros_like(q))
            sd = (jnp.dot(qm, kt, preferred_element_type=F32) * dec_ref[d, h2]).astype(BF16)
            outs.append(jnp.dot(sd, v, preferred_element_type=F32))
        y = jnp.where(lo, outs[0], outs[1]) + inter
        kz_t = (kt.astype(F32) * zeta_ref[d, 0]).astype(BF16)
        state = state * gch_ref[d, 0] + jnp.where(same_head, jnp.dot(kz_t, v, preferred_element_type=F32), 0.0)
        return sl, y, state

    def fwd(c, state):
        sl, y, state = chunk(0, c, state)
        acc_ref[sl, :] = y
        return state
    lax.fori_loop(0, nc, fwd, jnp.zeros((LANES, LANES), F32), unroll=RET_UNROLL)

    def bwd(i, state):
        c = _reverse_chunk(i, nc, t_ctx // L)
        sl, y, state = chunk(1, c, state)
        y = _head_rms(y + acc_ref[sl, :], rn_ref[...], lo)
        g = g_ref[0, sl, :]
        o_ref[0, sl, :] = (y * (g * jax.nn.sigmoid(g))).astype(o_ref.dtype)
        return state
    lax.fori_loop(0, nc, bwd, jnp.zeros((LANES, LANES), F32), unroll=RET_UNROLL)


def _retention(p_main, rope_tabs, ret_tabs, r_w, t_ctx, col_q, n_heads):
    b_sz, s_len, _ = p_main.shape
    n_hp = n_heads // 2
    L = RET_CHUNK_LEN
    assert t_ctx % L == 0 and s_len % L == 0
    c0 = col_q // LANES
    cos, sin = rope_tabs
    decay, zeta, xi, gch = ret_tabs
    const2 = pl.BlockSpec((s_len, LANES), lambda b, h: (0, 0), pipeline_mode=pl.Buffered(1))
    return pl.pallas_call(
        functools.partial(_ret_kernel, t_ctx=t_ctx),
        grid=(b_sz, n_hp),
        in_specs=[_seq_spec(s_len, c0, 2), _seq_spec(s_len, c0 + n_hp, 2), _seq_spec(s_len, c0 + 2 * n_hp, 2),
                  _seq_spec(s_len, c0 + 3 * n_hp), const2, const2,
                  pl.BlockSpec((2, 2, L, L), lambda b, h: (0, h, 0, 0)),
                  pl.BlockSpec((2, 1, LANES, L), lambda b, h: (0, h, 0, 0)),
                  pl.BlockSpec((2, 1, L, LANES), lambda b, h: (0, h, 0, 0)),
                  pl.BlockSpec((2, 1, 1, LANES), lambda b, h: (0, h, 0, 0)),
                  pl.BlockSpec((1, LANES), lambda b, h: (0, 0))],
        out_specs=pl.BlockSpec((1, s_len, LANES), lambda b, h: (b, 0, h)),
        out_shape=jax.ShapeDtypeStruct((b_sz, s_len, n_heads * HEAD_DIM), BF16),
        scratch_shapes=[pltpu.VMEM((s_len, LANES), BF16), pltpu.VMEM((LANES, s_len), BF16),
                        pltpu.VMEM((s_len, LANES), BF16), pltpu.VMEM((s_len, LANES), F32)],
        compiler_params=_mixer_cparams(),
        name="retention",
    )(p_main, p_main, p_main, p_main, cos, sin, decay, zeta, xi, gch,
      jnp.concatenate([r_w, r_w]).reshape(1, LANES))


MLSTM_CHUNK_LEN = LANES
MLSTM_PREP_UNROLL = 2
MLSTM_UNROLL = 3
N_GATE_TYPES = 4
N_FWD_GATES = N_GATE_TYPES


def _log_sigmoid(x):
    return jnp.minimum(x, 0.0) - jnp.log1p(jnp.exp(-jnp.abs(x)))


def _split3(x):
    hi = x.astype(BF16)
    r = x - hi.astype(F32)
    mid = r.astype(BF16)
    return hi, mid, (r - mid.astype(F32)).astype(BF16)


def _mlstm_pair_kernel(q_ref, k_ref, v_ref, og_ref, gc_ref, gr_ref, wq_ref, wk_ref, bq_ref, bk_ref, gbc_ref, gbr_ref,
                       mn_ref, o_ref, qc_ref, kt_ref, vb_ref, acc_ref, st_ref, cc_ref, cr_ref, *, t_ctx):
    L = MLSTM_CHUNK_LEN
    s_len = q_ref.shape[1]
    nc = s_len // L
    lane = lax.broadcasted_iota(jnp.int32, (1, LANES), 1)
    lo = lane < HEAD_DIM
    head_lanes = (lo, jnp.logical_not(lo))
    sub_lo = lax.broadcasted_iota(jnp.int32, (LANES, 1), 0) < HEAD_DIM
    head_rows = (sub_lo, jnp.logical_not(sub_lo))
    row_i = lax.broadcasted_iota(jnp.int32, (L, L), 0)
    col_i = lax.broadcasted_iota(jnp.int32, (L, L), 1)
    causal = (row_i >= col_i, row_i <= col_i)
    tri_b = causal[0].astype(BF16)
    tri_bt = causal[1].astype(BF16)
    sub = lax.broadcasted_iota(jnp.int32, (L, 1), 0)
    gate_row = lax.broadcasted_iota(jnp.int32, (2 * N_GATE_TYPES, 1), 0)
    ones = jnp.ones((L, LANES), BF16)

    def conv(x_ref, w_ref, b_ref, t0):
        x = x_ref[0, pl.ds(t0, L), :]
        prev = x_ref[0, pl.ds(jnp.maximum(t0 - SUBLANES, 0), SUBLANES), :][SUBLANES - 1:SUBLANES]
        nxt = x_ref[0, pl.ds(jnp.minimum(t0 + L, s_len - SUBLANES), SUBLANES), :][0:1]
        prev = jnp.where((t0 != 0) & (t0 != t_ctx), prev, 0.0)
        nxt = jnp.where((t0 + L != t_ctx) & (t0 + L != s_len), nxt, 0.0)
        xm = jnp.where(sub == 0, prev, pltpu.roll(x, 1, axis=0))
        xp = jnp.where(sub == L - 1, nxt, pltpu.roll(x, L - 1, axis=0))
        y = b_ref[...] + xm * w_ref[0:1, :] + x * w_ref[1:2, :] + xp * w_ref[2:3, :]
        return y * jax.nn.sigmoid(y)

    def prep(c, carry):
        t0 = pl.multiple_of(c * L, L)
        sl = pl.ds(t0, L)
        qc_ref[sl, :] = conv(q_ref, wq_ref, bq_ref, t0).astype(BF16)
        kt_ref[:, sl] = (conv(k_ref, wk_ref, bk_ref, t0) * HEAD_DIM ** -0.5).T.astype(BF16)
        vb_ref[sl, :] = v_ref[0, sl, :].astype(BF16)
        lf_col = _log_sigmoid(gc_ref[0, sl, :] + gbc_ref[0])
        lf_row = _log_sigmoid(gr_ref[0, 0, :, sl] + gbr_ref[0])
        pre_col = sum(jnp.dot(tri_b, p, preferred_element_type=F32) for p in _split3(lf_col))
        pre_row = sum(jnp.dot(p, tri_bt, preferred_element_type=F32) for p in _split3(lf_row))
        cc_ref[sl, :] = jnp.where(lane < N_FWD_GATES, pre_col, pre_col[L - 1:L, :] - pre_col + lf_col)
        cr_ref[:, sl] = jnp.where(gate_row < N_FWD_GATES, pre_row, pre_row[:, L - 1:L] - pre_row + lf_row)
        return carry
    lax.fori_loop(0, nc, prep, 0, unroll=MLSTM_PREP_UNROLL)

    def chunk(d, c, m_state):
        sl = pl.ds(pl.multiple_of(c * L, L), L)
        q, kt, v = qc_ref[sl, :], kt_ref[:, sl], vb_ref[sl, :]
        g_row = gr_ref[0, 0, :, sl] + gbr_ref[0]
        cum_col, cum_row = cc_ref[sl, :], cr_ref[:, sl]
        end = L - 1 if d == 0 else 0
        outs, new_m = [], []
        for h2 in range(2):
            m_st = m_state[h2]
            ci, cf = 2 * (2 * d) + h2, 2 * (2 * d + 1) + h2
            a_rep = jnp.broadcast_to(cum_col[:, cf:cf + 1], (L, LANES))
            i_row, a_row = g_row[ci:ci + 1, :], cum_row[cf:cf + 1, :]
            b_tot = a_row[:, end:end + 1]
            d_log = jnp.where(causal[d], a_rep + (i_row - a_row), NEG_INF)
            m_intra = jnp.broadcast_to(jnp.max(d_log, axis=-1, keepdims=True), (L, LANES))
            qm = jnp.where(head_lanes[h2], q, jnp.zeros_like(q))
            s = jnp.dot(qm, kt, preferred_element_type=F32) * jnp.exp(d_log - m_intra)
            v1 = jnp.where(head_lanes[h2], v, ones)
            intra = jnp.dot(s.astype(BF16), v1, preferred_element_type=F32)
            inter = jnp.dot(qm, st_ref[h2].astype(BF16), preferred_element_type=F32)
            inter_log = a_rep + m_st
            m_q = jnp.maximum(m_intra, inter_log)
            num_den = jnp.exp(inter_log - m_q) * inter + jnp.exp(m_intra - m_q) * intra
            den = pltpu.roll(num_den, HEAD_DIM, axis=1)
            outs.append(num_den / jnp.maximum(jnp.abs(den), jnp.exp(-m_q)))
            w_row = b_tot - a_row + i_row
            m_loc = jnp.max(w_row, axis=-1, keepdims=True)
            kts = jnp.where(head_rows[h2], kt.astype(F32) * jnp.exp(w_row - m_loc), 0.0).astype(BF16)
            loc = jnp.dot(kts, v1, preferred_element_type=F32)
            m_new = jnp.maximum(b_tot + m_st, m_loc)
            st_ref[h2] = jnp.exp(b_tot + m_st - m_new) * st_ref[h2] + jnp.exp(m_loc - m_new) * loc
            new_m.append(m_new)
        return sl, jnp.where(lo, outs[0], outs[1]), tuple(new_m)

    zero_m = (jnp.zeros((1, 1), F32), jnp.zeros((1, 1), F32))

    st_ref[...] = jnp.zeros_like(st_ref)

    def fwd(c, m_state):
        sl, y, m_state = chunk(0, c, m_state)
        acc_ref[sl, :] = y
        return m_state
    lax.fori_loop(0, nc, fwd, zero_m, unroll=MLSTM_UNROLL)

    st_ref[...] = jnp.zeros_like(st_ref)

    def bwd(i, m_state):
        c = _reverse_chunk(i, nc, t_ctx // L)
        sl, y, m_state = chunk(1, c, m_state)
        y = _head_rms(y + acc_ref[sl, :], mn_ref[...], lo)
        o_ref[0, sl, :] = (y * jax.nn.sigmoid(og_ref[0, sl, :])).astype(o_ref.dtype)
        return m_state
    lax.fori_loop(0, nc, bwd, zero_m, unroll=MLSTM_UNROLL)


def _mlstm(p_main, p_gate, conv_w, conv_b, gate_b, m_w, t_ctx, n_heads):
    b_sz, s_len, _ = p_main.shape
    n_hp = n_heads // 2
    L = MLSTM_CHUNK_LEN
    assert t_ctx % L == 0 and s_len % L == 0
    n_g = 2 * N_GATE_TYPES
    g_rows = jnp.transpose(p_gate.reshape(b_sz, s_len, n_hp, LANES)[..., :n_g], (0, 2, 3, 1))
    gb = jnp.transpose(gate_b.reshape(N_GATE_TYPES, n_hp, 2), (1, 0, 2)).reshape(n_hp, n_g)
    gb_col = jnp.zeros((n_hp, 1, LANES), F32).at[:, 0, :n_g].set(gb)
    gb_row = gb.reshape(n_hp, n_g, 1)
    vec = lambda col: pl.BlockSpec((1, LANES), lambda b, h: (0, col + h))
    return pl.pallas_call(
        functools.partial(_mlstm_pair_kernel, t_ctx=t_ctx),
        grid=(b_sz, n_hp),
        in_specs=[_seq_spec(s_len, 0, 2), _seq_spec(s_len, n_hp, 2), _seq_spec(s_len, 2 * n_hp, 2),
                  _seq_spec(s_len, 3 * n_hp), _seq_spec(s_len, 0),
                  pl.BlockSpec((1, 1, n_g, s_len), lambda b, h: (b, h, 0, 0)),
                  pl.BlockSpec((3, LANES), lambda b, h: (0, h)),
                  pl.BlockSpec((3, LANES), lambda b, h: (0, n_hp + h)),
                  vec(0), vec(n_hp),
                  pl.BlockSpec((1, 1, LANES), lambda b, h: (h, 0, 0)),
                  pl.BlockSpec((1, n_g, 1), lambda b, h: (h, 0, 0)),
                  pl.BlockSpec((1, LANES), lambda b, h: (0, 0))],
        out_specs=pl.BlockSpec((1, s_len, LANES), lambda b, h: (b, 0, h)),
        out_shape=jax.ShapeDtypeStruct((b_sz, s_len, n_heads * HEAD_DIM), BF16),
        scratch_shapes=[pltpu.VMEM((s_len, LANES), BF16), pltpu.VMEM((LANES, s_len), BF16),
                        pltpu.VMEM((s_len, LANES), BF16), pltpu.VMEM((s_len, LANES), F32),
                        pltpu.VMEM((2, LANES, LANES), F32),
                        pltpu.VMEM((s_len, LANES), F32), pltpu.VMEM((n_g, s_len), F32)],
        compiler_params=_mixer_cparams(),
        name="mlstm",
    )(p_main, p_main, p_main, p_main, p_gate, g_rows, conv_w, conv_w, conv_b.reshape(1, -1),
      conv_b.reshape(1, -1), gb_col, gb_row, jnp.concatenate([m_w, m_w]).reshape(1, LANES))


def kernel(x, c, ctx, c_ctx, w_mod, b_mod, norm_mix, norm_ffn, w_in, w_out, mlstm_conv_w, mlstm_conv_b,
           mlstm_gate_b, mlstm_norm, na_q_norm, na_k_norm, na_rpb, ret_norm, router_w, router_b,
           expert_w_up, expert_b_up, expert_w_down, expert_b_down):
    b_sz, t_len, d = x.shape
    t_ctx = ctx.shape[1]
    depth = w_in.shape[0]
    n_e = router_w.shape[2]
    d_mix = w_out.shape[1]
    h_m = d_mix // (4 * HEAD_DIM)
    h_na = d_mix // (2 * HEAD_DIM)
    h_r = d_mix // (4 * HEAD_DIM)
    d_m, d_na = h_m * HEAD_DIM, h_na * HEAD_DIM
    n_gate = N_GATE_TYPES * h_m
    assert b_sz + 1 <= MOD_ROWS and n_e <= LANES and n_gate <= LANES
    s_len = t_ctx + t_len
    assert s_len % ROW_TILE == 0 and s_len % COMBINE_ROWS == 0

    s = jnp.concatenate([ctx, x], axis=1)
    cc = jnp.zeros((MOD_ROWS, d), F32).at[:b_sz].set(c).at[b_sz].set(c_ctx)
    mods = _modulation(cc, w_mod, b_mod).reshape(depth, MOD_ROWS, 1, 6 * d)

    g0 = 4 * d_m
    col_na = g0
    col_ret = col_na + 3 * d_na
    n_hp = h_m // 2
    gate_src = np.array([[g0 + t * h_m + 2 * hp + h2 for t in range(N_GATE_TYPES) for h2 in range(2)]
                         for hp in range(n_hp)])
    rope_tabs = _rope_tables(t_ctx, t_len)
    ret_tabs = _ret_tables(h_r)
    na_tabs = _na_bias_tables(na_rpb)
    f2 = expert_w_up.shape[3]
    w_up_all = expert_w_up.reshape(depth * n_e, d, f2)
    w_down_all = expert_w_down.reshape(depth * n_e, f2 // 2, d)
    b_up_all = _regroup_bias(expert_b_up).reshape(depth * n_e, 1, f2)
    b_down_all = expert_b_down.reshape(depth * n_e, 1, d)

    experts_out = None
    for l in range(depth):
        w_main = jnp.concatenate([w_in[l, :, :g0], w_in[l, :, g0 + n_gate:]], axis=1).astype(BF16)
        w_gate = jnp.zeros((d, n_hp, LANES), BF16).at[:, :, :gate_src.shape[1]].set(
            w_in[l][:, gate_src].astype(BF16)).reshape(d, n_hp * LANES)
        if experts_out is None:
            p_main, p_gate = _proj_in(s, mods[l], norm_mix[l], w_main, w_gate, t_ctx)
        else:
            s, p_main, p_gate = _proj_in(s, mods[l], norm_mix[l], w_main, w_gate, t_ctx,
                                         combine=(*experts_out, mods[l - 1]))

        mix = [_mlstm(p_main, p_gate, mlstm_conv_w[l], mlstm_conv_b[l], mlstm_gate_b[l], mlstm_norm[l],
                      t_ctx, h_m),
               _na_attention(p_main, na_q_norm[l], na_k_norm[l], na_tabs[l], t_ctx, col_na, h_na),
               _retention(p_main, rope_tabs, ret_tabs, ret_norm[l], t_ctx, col_ret, h_r)]

        rw = jnp.zeros((d, LANES), F32).at[:, :n_e].set(router_w[l])
        rb = jnp.zeros((1, LANES), F32).at[0, :n_e].set(router_b[l])
        s, tok, logits = _proj_out(mix, s, mods[l], norm_ffn[l], w_out[l].astype(BF16), rw, rb, t_ctx)

        experts_out = _moe(tok, logits[..., :n_e], n_e, l, w_up_all, w_down_all, b_up_all, b_down_all)
    s = _combine(*experts_out, s, mods[depth - 1], t_ctx)
    return s[:, t_ctx:]
```

```python
import functools

import jax
import jax.numpy as jnp
import numpy as np
from jax import lax
from jax.experimental import pallas as pl
from jax.experimental.pallas import tpu as pltpu

F32 = jnp.float32
BF16 = jnp.bfloat16

GRID_W = 64
HEAD_DIM = 64
NA_ROWS = 8
NA_COLS = 16
ROPE_BASE = 10000.0
TOP_K = 4
SWIGLU_ALPHA = 1.702
SWIGLU_LIMIT = 7.0
NORM_EPS = 1e-6
NEG_INF = -1e30

LANES = 128
SUBLANES = 8
VMEM_LIMIT = 48 * 1024 * 1024
MOD_ROWS = 8
MOD_COL_TILES = 4
ROW_TILE = 768
COMBINE_ROWS = 256
MOE_TILE = 768
REGROUP_ROWS = 512
MOE_VMEM_LIMIT = 56 * 1024 * 1024


def _cparams(sem):
    return pltpu.CompilerParams(dimension_semantics=sem, vmem_limit_bytes=VMEM_LIMIT)


def _mod_kernel(cc_ref, w_ref, b_ref, o_ref):
    cc = cc_ref[...]
    a = cc * jax.nn.sigmoid(cc)
    o_ref[0] = jnp.dot(a, w_ref[0], precision=lax.Precision.HIGHEST,
                       preferred_element_type=F32) + b_ref[0]


def _modulation(cc, w_mod, b_mod):
    n_l, d, d6 = w_mod.shape
    tn = d6 // MOD_COL_TILES
    return pl.pallas_call(
        _mod_kernel,
        grid=(n_l, MOD_COL_TILES),
        in_specs=[pl.BlockSpec((MOD_ROWS, d), lambda l, j: (0, 0)),
                  pl.BlockSpec((1, d, tn), lambda l, j: (l, 0, j)),
                  pl.BlockSpec((1, 1, tn), lambda l, j: (l, 0, j))],
        out_specs=pl.BlockSpec((1, MOD_ROWS, tn), lambda l, j: (l, 0, j)),
        out_shape=jax.ShapeDtypeStruct((n_l, MOD_ROWS, d6), F32),
        compiler_params=_cparams(("arbitrary", "arbitrary")),
        name="adaln_modulation",
    )(cc, w_mod, b_mod.reshape(n_l, 1, d6))


def _pick_mod(mb_ref, mc_ref, k, d, is_ctx):
    vb = mb_ref[0, :, k * d:(k + 1) * d]
    vc = mc_ref[0, :, k * d:(k + 1) * d]
    return jnp.where(is_ctx, vc, vb)


def _ctx_rows(tm, t_ctx):
    return lax.broadcasted_iota(jnp.int32, (tm, 1), 0) + pl.program_id(1) * tm < t_ctx


def _rms(x, g):
    return x * lax.rsqrt(jnp.mean(x * x, axis=-1, keepdims=True) + NORM_EPS) * g


def _pack_bf16_pairs(x):
    m = x.shape[1] // 2
    hi = lax.bitcast_convert_type(x[:, :m].astype(BF16).astype(F32), jnp.uint32)
    lo = lax.bitcast_convert_type(x[:, m:].astype(BF16).astype(F32), jnp.uint32)
    return hi | (lo >> 16)


def _unpack_bf16_pairs(p):
    hi = lax.bitcast_convert_type(p & jnp.uint32(0xFFFF0000), F32)
    lo = lax.bitcast_convert_type(p << 16, F32)
    return jnp.concatenate([hi, lo], axis=1)


PROJ_IN_ROWS = 384
PROJ_IN_COLS = 512


def _project(s, is_ctx, mb_ref, mc_ref, g_ref, w_ref, wg_ref, pm_ref, pg_ref):
    d = s.shape[1]
    sh = _pick_mod(mb_ref, mc_ref, 0, d, is_ctx)
    sc = _pick_mod(mb_ref, mc_ref, 1, d, is_ctx)
    xn = (_rms(s, g_ref[...]) * (1.0 + sc) + sh).astype(BF16)
    pg_ref[0] = jnp.dot(xn, wg_ref[...], preferred_element_type=F32)
    for j in range(w_ref.shape[1] // PROJ_IN_COLS):
        cols = slice(j * PROJ_IN_COLS, (j + 1) * PROJ_IN_COLS)
        pm_ref[0, :, cols] = jnp.dot(xn, w_ref[:, cols], preferred_element_type=F32)


def _proj_in_kernel(s_ref, mb_ref, mc_ref, g_ref, w_ref, wg_ref, pm_ref, pg_ref, *, t_ctx):
    _project(s_ref[0], _ctx_rows(s_ref.shape[1], t_ctx), mb_ref, mc_ref, g_ref, w_ref, wg_ref, pm_ref, pg_ref)


def _combine_rows(y_ref, gt_ref, s_ref, mb_ref, mc_ref, is_ctx):
    g2 = _pick_mod(mb_ref, mc_ref, 5, s_ref.shape[2], is_ctx)
    gt = gt_ref[0]
    y = _unpack_bf16_pairs(y_ref[0, 0]) * gt[:, 0:1]
    for k in range(1, y_ref.shape[0]):
        y = y + _unpack_bf16_pairs(y_ref[k, 0]) * gt[:, k:k + 1]
    return s_ref[0] + g2 * y


def _combine_proj_in_kernel(y_ref, gt_ref, s_ref, pmb_ref, pmc_ref, mb_ref, mc_ref, g_ref, w_ref, wg_ref,
                            so_ref, pm_ref, pg_ref, *, t_ctx):
    is_ctx = _ctx_rows(s_ref.shape[1], t_ctx)
    s_new = _combine_rows(y_ref, gt_ref, s_ref, pmb_ref, pmc_ref, is_ctx)
    so_ref[0] = s_new
    _project(s_new, is_ctx, mb_ref, mc_ref, g_ref, w_ref, wg_ref, pm_ref, pg_ref)


def _proj_in(s, mod_l, g, w_main, w_gate, t_ctx, combine=None):
    b_sz, s_len, d = s.shape
    n_main = w_main.shape[1]
    n_gate = w_gate.shape[1]
    tm = PROJ_IN_ROWS
    assert s_len % tm == 0 and n_main % PROJ_IN_COLS == 0
    row_spec = pl.BlockSpec((1, tm, d), lambda b, i: (b, i, 0))
    mod_specs = [pl.BlockSpec((1, 1, 6 * d), lambda b, i: (b, 0, 0)),
                 pl.BlockSpec((1, 1, 6 * d), lambda b, i: (b_sz, 0, 0))]
    w_specs = [pl.BlockSpec((1, d), lambda b, i: (0, 0)),
               pl.BlockSpec((d, n_main), lambda b, i: (0, 0)),
               pl.BlockSpec((d, n_gate), lambda b, i: (0, 0))]
    out_specs = [pl.BlockSpec((1, tm, n_main), lambda b, i: (b, i, 0)),
                 pl.BlockSpec((1, tm, n_gate), lambda b, i: (b, i, 0))]
    out_shape = [jax.ShapeDtypeStruct((b_sz, s_len, n_main), F32),
                 jax.ShapeDtypeStruct((b_sz, s_len, n_gate), F32)]
    if combine is None:
        return pl.pallas_call(
            functools.partial(_proj_in_kernel, t_ctx=t_ctx),
            grid=(b_sz, s_len // tm),
            in_specs=[row_spec] + mod_specs + w_specs,
            out_specs=out_specs, out_shape=out_shape,
            compiler_params=_cparams(("arbitrary", "arbitrary")),
            name="proj_in",
        )(s, mod_l, mod_l, g.reshape(1, d), w_main, w_gate)
    y_as, gates, mod_prev = combine
    n_k = y_as.shape[0]
    return pl.pallas_call(
        functools.partial(_combine_proj_in_kernel, t_ctx=t_ctx),
        grid=(b_sz, s_len // tm),
        in_specs=[pl.BlockSpec((n_k, 1, tm, d // 2), lambda b, i: (0, b, i, 0)),
                  pl.BlockSpec((1, tm, n_k), lambda b, i: (b, i, 0)),
                  row_spec] + mod_specs + mod_specs + w_specs,
        out_specs=[row_spec] + out_specs,
        out_shape=[jax.ShapeDtypeStruct((b_sz, s_len, d), F32)] + out_shape,
        compiler_params=_cparams(("arbitrary", "arbitrary")),
        name="combine_proj_in",
    )(y_as, gates, s, mod_prev, mod_prev, mod_l, mod_l, g.reshape(1, d), w_main, w_gate)


def _proj_out_kernel(ma_ref, mb2_ref, mc2_ref, s_ref, mb_ref, mc_ref, g_ref, w_ref, rw_ref, rb_ref,
                     so_ref, tok_ref, lg_ref, *, t_ctx):
    d = s_ref.shape[2]
    is_ctx = _ctx_rows(s_ref.shape[1], t_ctx)
    g1 = _pick_mod(mb_ref, mc_ref, 2, d, is_ctx)
    ka, kb = ma_ref.shape[2], ma_ref.shape[2] + mb2_ref.shape[2]
    y = (jnp.dot(ma_ref[0], w_ref[0:ka, :], preferred_element_type=F32)
         + jnp.dot(mb2_ref[0], w_ref[ka:kb, :], preferred_element_type=F32)
         + jnp.dot(mc2_ref[0], w_ref[kb:, :], preferred_element_type=F32))
    s_new = s_ref[0] + g1 * y
    so_ref[0] = s_new
    sh = _pick_mod(mb_ref, mc_ref, 3, d, is_ctx)
    sc = _pick_mod(mb_ref, mc_ref, 4, d, is_ctx)
    t = _rms(s_new, g_ref[...]) * (1.0 + sc) + sh
    tok_ref[0] = t
    w = rw_ref[...]
    t_hi, w_hi = t.astype(BF16), w.astype(BF16)
    t_lo = (t - t_hi.astype(F32)).astype(BF16)
    w_lo = (w - w_hi.astype(F32)).astype(BF16)
    lg_ref[0] = (jnp.dot(t_hi, w_hi, preferred_element_type=F32) + jnp.dot(t_hi, w_lo, preferred_element_type=F32)
                 + jnp.dot(t_lo, w_hi, preferred_element_type=F32) + rb_ref[...])


def _proj_out(mix_parts, s, mod_l, g, w_out, rw, rb, t_ctx):
    b_sz, s_len, d = s.shape
    tm = ROW_TILE
    row_spec = pl.BlockSpec((1, tm, d), lambda b, i: (b, i, 0))
    part_specs = [pl.BlockSpec((1, tm, m.shape[2]), lambda b, i: (b, i, 0)) for m in mix_parts]
    return pl.pallas_call(
        functools.partial(_proj_out_kernel, t_ctx=t_ctx),
        grid=(b_sz, s_len // tm),
        in_specs=part_specs + [row_spec,
                               pl.BlockSpec((1, 1, 6 * d), lambda b, i: (b, 0, 0)),
                               pl.BlockSpec((1, 1, 6 * d), lambda b, i: (b_sz, 0, 0)),
                               pl.BlockSpec((1, d), lambda b, i: (0, 0)),
                               pl.BlockSpec((d, d), lambda b, i: (0, 0)),
                               pl.BlockSpec((d, LANES), lambda b, i: (0, 0)),
                               pl.BlockSpec((1, LANES), lambda b, i: (0, 0))],
        out_specs=[row_spec, row_spec, pl.BlockSpec((1, tm, LANES), lambda b, i: (b, i, 0))],
        out_shape=[jax.ShapeDtypeStruct((b_sz, s_len, d), F32),
                   jax.ShapeDtypeStruct((b_sz, s_len, d), F32),
                   jax.ShapeDtypeStruct((b_sz, s_len, LANES), F32)],
        compiler_params=_cparams(("arbitrary", "arbitrary")),
        name="proj_out_router",
    )(*mix_parts, s, mod_l, mod_l, g.reshape(1, d), w_out, rw, rb)


PAIR = 2 * LANES


def _regroup_perm():
    dst = np.arange(PAIR)
    src = np.where(dst < LANES, 2 * dst, 2 * (dst - LANES) + 1)
    return jnp.asarray(np.arange(PAIR)[:, None] == src[None, :], BF16)


def _regroup_bias(b_up):
    lead = b_up.shape[:-1]
    b = b_up.reshape(*lead, -1, LANES, 2)
    return jnp.swapaxes(b, -1, -2).reshape(*lead, -1)


def _moe_kernel(be_ref, nu_ref, x_ref, wu_ref, wd_ref, bu_ref, bd_ref, p_ref, y_ref, wus_ref, wds_ref):
    i = pl.program_id(0)
    d, f2 = wus_ref.shape
    rows = min(REGROUP_ROWS, d)

    @pl.when(i < nu_ref[0])
    def _():
        @pl.when((i == 0) | (be_ref[i] != be_ref[jnp.maximum(i - 1, 0)]))
        def _():
            for r in range(d // rows):
                for j in range(f2 // PAIR):
                    w = wu_ref[0, r * rows:(r + 1) * rows, j * PAIR:(j + 1) * PAIR].astype(BF16)
                    wus_ref[r * rows:(r + 1) * rows, j * PAIR:(j + 1) * PAIR] = jnp.dot(
                        w, p_ref[...], preferred_element_type=F32).astype(BF16)
            wds_ref[...] = wd_ref[0].astype(BF16)

        up = jnp.dot(x_ref[...].astype(BF16), wus_ref[...], preferred_element_type=F32) + bu_ref[0]
        acts = []
        for j in range(f2 // PAIR):
            glu = jnp.minimum(up[:, j * PAIR:j * PAIR + LANES], SWIGLU_LIMIT)
            lin = jnp.clip(up[:, j * PAIR + LANES:(j + 1) * PAIR], -SWIGLU_LIMIT, SWIGLU_LIMIT)
            acts.append((glu * jax.nn.sigmoid(SWIGLU_ALPHA * glu) * (lin + 1.0)).astype(BF16))
        act = jnp.concatenate(acts, axis=1)
        y_ref[...] = _pack_bf16_pairs(jnp.dot(act, wds_ref[...], preferred_element_type=F32) + bd_ref[0])

    @pl.when(i >= nu_ref[0])
    def _():
        y_ref[...] = jnp.zeros_like(y_ref)


def _moe_blocks(blk_e, n_used, x_sorted, w_up, w_down, b_up, b_down):
    n_rows, d = x_sorted.shape
    _, _, f2 = w_up.shape
    assert f2 % PAIR == 0 and d % min(REGROUP_ROWS, d) == 0
    tm = MOE_TILE
    wmap = lambda i, be, nu: (be[i], 0, 0)
    return pl.pallas_call(
        _moe_kernel,
        grid_spec=pltpu.PrefetchScalarGridSpec(
            num_scalar_prefetch=2,
            grid=(n_rows // tm,),
            in_specs=[pl.BlockSpec((tm, d), lambda i, be, nu: (i, 0)),
                      pl.BlockSpec((1, d, f2), wmap),
                      pl.BlockSpec((1, f2 // 2, d), wmap),
                      pl.BlockSpec((1, 1, f2), wmap),
                      pl.BlockSpec((1, 1, d), wmap),
                      pl.BlockSpec((PAIR, PAIR), lambda i, be, nu: (0, 0))],
            out_specs=pl.BlockSpec((tm, d // 2), lambda i, be, nu: (i, 0)),
            scratch_shapes=[pltpu.VMEM((d, f2), BF16), pltpu.VMEM((f2 // 2, d), BF16)]),
        out_shape=jax.ShapeDtypeStruct((n_rows, d // 2), jnp.uint32),
        compiler_params=pltpu.CompilerParams(dimension_semantics=("arbitrary",),
                                             vmem_limit_bytes=MOE_VMEM_LIMIT),
        name="moe_expert_blocks",
    )(blk_e, n_used, x_sorted, w_up, w_down, b_up, b_down, _regroup_perm())


def _combine_kernel(y_ref, gt_ref, s_ref, mb_ref, mc_ref, o_ref, *, t_ctx):
    o_ref[0] = _combine_rows(y_ref, gt_ref, s_ref, mb_ref, mc_ref, _ctx_rows(s_ref.shape[1], t_ctx))


def _combine(y_as, gates, s, mod_l, t_ctx):
    b_sz, s_len, d = s.shape
    n_k = y_as.shape[0]
    tm = COMBINE_ROWS
    row_spec = pl.BlockSpec((1, tm, d), lambda b, i: (b, i, 0))
    return pl.pallas_call(
        functools.partial(_combine_kernel, t_ctx=t_ctx),
        grid=(b_sz, s_len // tm),
        in_specs=[pl.BlockSpec((n_k, 1, tm, d // 2), lambda b, i: (0, b, i, 0)),
                  pl.BlockSpec((1, tm, n_k), lambda b, i: (b, i, 0)),
                  row_spec,
                  pl.BlockSpec((1, 1, 6 * d), lambda b, i: (b, 0, 0)),
                  pl.BlockSpec((1, 1, 6 * d), lambda b, i: (b_sz, 0, 0))],
        out_specs=row_spec,
        out_shape=jax.ShapeDtypeStruct((b_sz, s_len, d), F32),
        compiler_params=_cparams(("arbitrary", "arbitrary")),
        name="moe_combine",
    )(y_as, gates, s, mod_l, mod_l)


def _moe(tok, logits, n_e, layer, w_up, w_down, b_up, b_down):
    b_sz, s_len, d = tok.shape
    n_tok = b_sz * s_len
    tm = MOE_TILE
    top_v, top_e = lax.top_k(logits.reshape(n_tok, n_e), TOP_K)
    gates = jax.nn.softmax(top_v, axis=-1)
    n_as = n_tok * TOP_K
    onehot = jnp.sum((top_e[:, :, None] == jnp.arange(n_e)[None, None, :]).astype(jnp.int32), axis=1)
    csum = jnp.cumsum(onehot, axis=0)
    counts = csum[-1]
    padded = (counts + tm - 1) // tm * tm
    end_pad = jnp.cumsum(padded)
    start_pad = end_pad - padded
    start = jnp.cumsum(counts) - counts
    dest = jnp.take_along_axis(csum - onehot + start_pad[None, :], top_e, axis=1).astype(jnp.int32)
    n_blocks = -(-n_as // tm) + n_e
    blk_first = jnp.arange(n_blocks) * tm
    blk_e = jnp.minimum(jnp.sum(blk_first[:, None] >= end_pad[None, :], axis=1), n_e - 1).astype(jnp.int32)
    n_used = (end_pad[-1:] // tm).astype(jnp.int32)
    tok_sorted = (jnp.argsort(top_e.reshape(n_as)) // TOP_K).astype(jnp.int32)
    j = (blk_first - start_pad[blk_e])[:, None] + jnp.arange(tm)[None, :]
    src = jnp.clip(start[blk_e][:, None] + j, 0, n_as - 1)
    row_tok = jnp.where(j < counts[blk_e][:, None], tok_sorted[src], 0).reshape(n_blocks * tm)
    x_sorted = tok.reshape(n_tok, d)[row_tok]
    y = _moe_blocks(blk_e + layer * n_e, n_used, x_sorted, w_up, w_down, b_up, b_down)
    y_as = y[dest.T].reshape(TOP_K, b_sz, s_len, d // 2)
    return y_as, gates.reshape(b_sz, s_len, TOP_K)


MIXER_VMEM_LIMIT = 58 * 1024 * 1024


def _seq_spec(s_len, col, buffers=1):
    return pl.BlockSpec((1, s_len, LANES), lambda b, h: (b, 0, col + h), pipeline_mode=pl.Buffered(buffers))


def _mixer_cparams():
    return pltpu.CompilerParams(dimension_semantics=("arbitrary", "arbitrary"), vmem_limit_bytes=MIXER_VMEM_LIMIT)


def _head_rms(x, w, lo):
    xx = x * x
    s0 = jnp.sum(jnp.where(lo, xx, 0.0), axis=-1, keepdims=True)
    s1 = jnp.sum(jnp.where(lo, 0.0, xx), axis=-1, keepdims=True)
    inv = lax.rsqrt(jnp.where(lo, s0, s1) * (1.0 / HEAD_DIM) + NORM_EPS)
    return x * inv * w


def _reverse_chunk(i, nc, n_ctx):
    return jnp.where(i < n_ctx, n_ctx - 1 - i, nc - 1 - i + n_ctx)


NA_STEP = 256
NA_KEYS = NA_ROWS * GRID_W
NA_GROUPS_PER_ITER = 4
NA_PREP_UNROLL = 3
NA_SHIFT_UNROLL = 5


def _na_bias_tables(rpb):
    n_l, n_h, n_dr, n_dc = rpb.shape
    col = np.arange(GRID_W)
    dc = np.clip(col[None, :] - col[:, None] + NA_COLS - 1, 0, n_dc - 1)
    onehot = (dc.reshape(1, -1) == np.arange(n_dc)[:, None]).astype(np.float32)
    toe = jnp.dot(rpb.reshape(-1, n_dc), jnp.asarray(onehot), precision=lax.Precision.HIGHEST)
    toe = toe.reshape(n_l, n_h, n_dr, GRID_W, GRID_W)
    col_start = np.clip(col - NA_COLS // 2, 0, GRID_W - NA_COLS)
    col_in = (col[None, :] >= col_start[:, None]) & (col[None, :] < col_start[:, None] + NA_COLS)
    toe = jnp.where(jnp.asarray(col_in), toe, NEG_INF)
    tabs = []
    for dr0 in range(NA_ROWS):
        t = toe[:, :, dr0:dr0 + NA_ROWS].reshape(n_l, n_h // 2, 2, NA_ROWS, GRID_W, GRID_W)
        t = jnp.transpose(t, (0, 1, 2, 4, 3, 5))
        tabs.append(t.reshape(n_l, n_h // 2, 2 * GRID_W, NA_KEYS))
    return jnp.stack(tabs, axis=1)


def _na_seq_kernel(q_ref, k_ref, v_ref, qw_ref, kw_ref, bias_ref, o_ref, kt_ref, vb_ref, s_ref, kn_ref, *,
                   t_ctx, rows):
    s_len = k_ref.shape[1]
    lo = lax.broadcasted_iota(jnp.int32, (1, LANES), 1) < HEAD_DIM
    rows_per_step = NA_STEP // GRID_W

    def prep(c, carry):
        t0 = pl.multiple_of(c * NA_STEP, NA_STEP)
        kn = _head_rms(k_ref[0, pl.ds(t0, NA_STEP), :], kw_ref[...], lo).astype(BF16)
        kn_ref[pl.ds(t0, NA_STEP), :] = kn
        kt_ref[0, :, pl.ds(t0, NA_STEP)] = kn.T
        vb_ref[pl.ds(t0, NA_STEP), :] = jnp.concatenate(
            [v_ref[0, pl.ds(t0, NA_STEP), :].astype(BF16), jnp.ones((NA_STEP, LANES), BF16)], axis=1)
        return carry
    lax.fori_loop(0, s_len // NA_STEP, prep, 0, unroll=NA_PREP_UNROLL)

    def prep_shifted(c, carry):
        t0 = pl.multiple_of(c * LANES, LANES)
        kt_ref[1, :, pl.ds(t0, LANES)] = kn_ref[pl.ds(pl.multiple_of(t0 + GRID_W, GRID_W), LANES), :].T
        return carry
    lax.fori_loop(0, (s_len - GRID_W) // LANES, prep_shifted, 0, unroll=NA_SHIFT_UNROLL)

    kc_t = kt_ref[0, :, 0:t_ctx]
    vc = vb_ref[0:t_ctx, :]

    def queries(tok0):
        qn = _head_rms(q_ref[0, pl.ds(tok0, NA_STEP), :], qw_ref[...], lo) * (HEAD_DIM ** -0.5)
        q0 = jnp.where(lo, qn, 0.0).astype(BF16)
        q1 = jnp.where(lo, 0.0, qn).astype(BF16)
        return [jnp.concatenate([q0[g * GRID_W:(g + 1) * GRID_W], q1[g * GRID_W:(g + 1) * GRID_W]], axis=0)
                for g in range(rows_per_step)]

    def finish(tok0, g, o2):
        o2 = o2[:, :LANES] * (1.0 / o2[:, LANES:])
        o = jnp.where(lo, o2[:GRID_W], o2[GRID_W:])
        o_ref[0, pl.ds(tok0 + g * GRID_W, GRID_W), :] = o.astype(o_ref.dtype)

    for g, q2 in enumerate(queries(0)):
        s_c = jnp.dot(q2, kc_t, preferred_element_type=F32)
        p_c = jnp.exp(s_c - jnp.max(s_c, axis=-1, keepdims=True))
        finish(0, g, jnp.dot(p_c.astype(BF16), vc, preferred_element_type=F32))

    def row_group(rg, slot):
        tok0 = pl.multiple_of(t_ctx + rg * NA_STEP, NA_STEP)
        t0s = []
        s_slot = s_ref.at[slot]
        for g, q2 in enumerate(queries(tok0)):
            r = rg * rows_per_step + g
            row_start = jnp.clip(r - NA_ROWS // 2, 0, rows - NA_ROWS)
            dr0 = row_start - r + NA_ROWS - 1
            t0 = pl.multiple_of(t_ctx + row_start * GRID_W, GRID_W)
            odd = (t0 // GRID_W) % (LANES // GRID_W)
            kw_t = kt_ref[odd, :, pl.ds(pl.multiple_of(t0 - odd * GRID_W, LANES), NA_KEYS)]
            s_slot[g, :, :NA_KEYS] = jnp.dot(q2, kw_t, preferred_element_type=F32) + bias_ref[dr0, 0]
            s_slot[g, :, NA_KEYS:] = jnp.dot(q2, kc_t, preferred_element_type=F32)
            t0s.append(t0)
        for g in range(rows_per_step):
            s = s_slot[g]
            p = jnp.exp(s - jnp.max(s, axis=-1, keepdims=True)).astype(BF16)
            finish(tok0, g, jnp.dot(p[:, :NA_KEYS], vb_ref[pl.ds(t0s[g], NA_KEYS), :], preferred_element_type=F32)
                   + jnp.dot(p[:, NA_KEYS:], vc, preferred_element_type=F32))

    def row_groups(i, carry):
        for slot in range(NA_GROUPS_PER_ITER):
            row_group(i * NA_GROUPS_PER_ITER + slot, slot)
        return carry
    lax.fori_loop(0, rows // rows_per_step // NA_GROUPS_PER_ITER, row_groups, 0)


def _na_attention(p_main, q_w, k_w, bias_tab, t_ctx, col_q, n_heads):
    b_sz, s_len, _ = p_main.shape
    n_hg = n_heads // 2
    rows = (s_len - t_ctx) // GRID_W
    rows_per_step = NA_STEP // GRID_W
    assert t_ctx == NA_STEP and s_len % NA_STEP == 0 and rows >= NA_ROWS
    assert rows % (rows_per_step * NA_GROUPS_PER_ITER) == 0
    cq = col_q // LANES
    seq = lambda col: pl.BlockSpec((1, s_len, LANES), lambda b, h: (b, 0, col + h))
    w2 = lambda w: jnp.concatenate([w, w]).reshape(1, LANES)
    return pl.pallas_call(
        functools.partial(_na_seq_kernel, t_ctx=t_ctx, rows=rows),
        grid=(b_sz, n_hg),
        in_specs=[seq(cq), seq(cq + n_hg), _seq_spec(s_len, cq + 2 * n_hg),
                  pl.BlockSpec((1, LANES), lambda b, h: (0, 0)),
                  pl.BlockSpec((1, LANES), lambda b, h: (0, 0)),
                  pl.BlockSpec((NA_ROWS, 1, 2 * GRID_W, NA_KEYS), lambda b, h: (0, h, 0, 0))],
        out_specs=pl.BlockSpec((1, s_len, LANES), lambda b, h: (b, 0, h)),
        out_shape=jax.ShapeDtypeStruct((b_sz, s_len, n_heads * HEAD_DIM), BF16),
        scratch_shapes=[pltpu.VMEM((2, LANES, s_len), BF16), pltpu.VMEM((s_len, 2 * LANES), BF16),
                        pltpu.VMEM((NA_GROUPS_PER_ITER, rows_per_step, 2 * GRID_W, NA_KEYS + t_ctx), F32),
                        pltpu.VMEM((s_len, LANES), BF16)],
        compiler_params=_cparams(("arbitrary", "arbitrary")),
        name="na_attention",
    )(p_main, p_main, p_main, w2(q_w), w2(k_w), bias_tab)


RET_CHUNK_LEN = 256
RET_UNROLL = 3


def _rope_tables(t_ctx, t_len):
    pos = jnp.arange(t_len)
    row = (pos // GRID_W).astype(F32)
    col = (pos % GRID_W).astype(F32)
    n = HEAD_DIM // 4
    inv = ROPE_BASE ** (-jnp.arange(n, dtype=F32) / n)
    ar = row[:, None] * inv
    ac = col[:, None] * inv
    cos = jnp.concatenate([jnp.cos(ar), jnp.cos(ar), jnp.cos(ac), jnp.cos(ac)], axis=-1)
    sin = jnp.concatenate([-jnp.sin(ar), jnp.sin(ar), -jnp.sin(ac), jnp.sin(ac)], axis=-1)
    cos = jnp.concatenate([jnp.ones((t_ctx, HEAD_DIM), F32), cos], axis=0)
    sin = jnp.concatenate([jnp.zeros((t_ctx, HEAD_DIM), F32), sin], axis=0)
    return jnp.tile(cos, (1, 2)), jnp.tile(sin, (1, 2))


def _ret_tables(n_heads):
    L = RET_CHUNK_LEN
    pos = np.arange(L, dtype=np.float32)
    lane_head = np.arange(LANES) // HEAD_DIM
    decay = np.zeros((2, n_heads, L, L), np.float32)
    zeta = np.zeros((2, n_heads // 2, LANES, L), np.float32)
    xi = np.zeros((2, n_heads // 2, L, LANES), np.float32)
    gch = np.zeros((2, n_heads // 2, 1, LANES), np.float32)
    for d, first_exp in enumerate((5.0, 6.0)):
        e = np.float32(first_exp) + np.float32(2.0) * np.arange(n_heads, dtype=np.float32)
        lg = np.log1p(-np.exp2(-e)).astype(np.float32)
        diff = pos[:, None] - pos[None, :]
        if d == 1:
            diff = -diff
        for h in range(n_heads):
            decay[d, h] = np.where(diff >= 0, np.exp(lg[h] * np.maximum(diff, 0.0)), 0.0)
        for hp in range(n_heads // 2):
            lgl = lg[2 * hp + lane_head][None, :]
            to_end = (L - 1 - pos if d == 0 else pos)[:, None]
            zeta[d, hp] = np.exp(lgl * to_end).T
            xi[d, hp] = np.exp(lgl * (L - to_end))
            gch[d, hp] = np.exp(lgl * L)
    return tuple(jnp.asarray(a) for a in (decay, zeta, xi, gch))


def _ret_kernel(q_ref, k_ref, v_ref, g_ref, cos_ref, sin_ref, dec_ref, zeta_ref, xi_ref, gch_ref, rn_ref,
                o_ref, qr_ref, kt_ref, vb_ref, acc_ref, *, t_ctx):
    L = RET_CHUNK_LEN
    s_len = q_ref.shape[1]
    nc = s_len // L
    lane = lax.broadcasted_iota(jnp.int32, (1, LANES), 1)
    lo = lane < HEAD_DIM
    half = (lane & (HEAD_DIM // 4)) == 0
    rid = lax.broadcasted_iota(jnp.int32, (LANES, LANES), 0) < HEAD_DIM
    cid = lax.broadcasted_iota(jnp.int32, (LANES, LANES), 1) < HEAD_DIM
    same_head = rid == cid

    def rope(x, cos, sin):
        up = pltpu.roll(x, LANES - HEAD_DIM // 4, axis=1)
        dn = pltpu.roll(x, HEAD_DIM // 4, axis=1)
        return x * cos + jnp.where(half, up, dn) * sin

    def prep(c, carry):
        sl = pl.ds(pl.multiple_of(c * L, L), L)
        cos, sin = cos_ref[sl, :], sin_ref[sl, :]
        qr_ref[sl, :] = rope(q_ref[0, sl, :], cos, sin).astype(BF16)
        kt_ref[:, sl] = (rope(k_ref[0, sl, :], cos, sin) * HEAD_DIM ** -0.5).T.astype(BF16)
        vb_ref[sl, :] = v_ref[0, sl, :].astype(BF16)
        return carry
    lax.fori_loop(0, nc, prep, 0, unroll=RET_UNROLL)

    def chunk(d, c, state):
        sl = pl.ds(pl.multiple_of(c * L, L), L)
        q, kt, v = qr_ref[sl, :], kt_ref[:, sl], vb_ref[sl, :]
        inter = jnp.dot(q, state.astype(BF16), preferred_element_type=F32) * xi_ref[d, 0]
        outs = []
        for h2 in range(2):
            qm = jnp.where(lo if h2 == 0 else jnp.logical_not(lo), q, jnp.zeros_like(q))
            sd = (jnp.dot(qm, kt, preferred_element_type=F32) * dec_ref[d, h2]).astype(BF16)
            outs.append(jnp.dot(sd, v, preferred_element_type=F32))
        y = jnp.where(lo, outs[0], outs[1]) + inter
        kz_t = (kt.astype(F32) * zeta_ref[d, 0]).astype(BF16)
        state = state * gch_ref[d, 0] + jnp.where(same_head, jnp.dot(kz_t, v, preferred_element_type=F32), 0.0)
        return sl, y, state

    def fwd(c, state):
        sl, y, state = chunk(0, c, state)
        acc_ref[sl, :] = y
        return state
    lax.fori_loop(0, nc, fwd, jnp.zeros((LANES, LANES), F32), unroll=RET_UNROLL)

    def bwd(i, state):
        c = _reverse_chunk(i, nc, t_ctx // L)
        sl, y, state = chunk(1, c, state)
        y = _head_rms(y + acc_ref[sl, :], rn_ref[...], lo)
        g = g_ref[0, sl, :]
        o_ref[0, sl, :] = (y * (g * jax.nn.sigmoid(g))).astype(o_ref.dtype)
        return state
    lax.fori_loop(0, nc, bwd, jnp.zeros((LANES, LANES), F32), unroll=RET_UNROLL)


def _retention(p_main, rope_tabs, ret_tabs, r_w, t_ctx, col_q, n_heads):
    b_sz, s_len, _ = p_main.shape
    n_hp = n_heads // 2
    L = RET_CHUNK_LEN
    assert t_ctx % L == 0 and s_len % L == 0
    c0 = col_q // LANES
    cos, sin = rope_tabs
    decay, zeta, xi, gch = ret_tabs
    const2 = pl.BlockSpec((s_len, LANES), lambda b, h: (0, 0), pipeline_mode=pl.Buffered(1))
    return pl.pallas_call(
        functools.partial(_ret_kernel, t_ctx=t_ctx),
        grid=(b_sz, n_hp),
        in_specs=[_seq_spec(s_len, c0, 2), _seq_spec(s_len, c0 + n_hp, 2), _seq_spec(s_len, c0 + 2 * n_hp, 2),
                  _seq_spec(s_len, c0 + 3 * n_hp), const2, const2,
                  pl.BlockSpec((2, 2, L, L), lambda b, h: (0, h, 0, 0)),
                  pl.BlockSpec((2, 1, LANES, L), lambda b, h: (0, h, 0, 0)),
                  pl.BlockSpec((2, 1, L, LANES), lambda b, h: (0, h, 0, 0)),
                  pl.BlockSpec((2, 1, 1, LANES), lambda b, h: (0, h, 0, 0)),
                  pl.BlockSpec((1, LANES), lambda b, h: (0, 0))],
        out_specs=pl.BlockSpec((1, s_len, LANES), lambda b, h: (b, 0, h)),
        out_shape=jax.ShapeDtypeStruct((b_sz, s_len, n_heads * HEAD_DIM), BF16),
        scratch_shapes=[pltpu.VMEM((s_len, LANES), BF16), pltpu.VMEM((LANES, s_len), BF16),
                        pltpu.VMEM((s_len, LANES), BF16), pltpu.VMEM((s_len, LANES), F32)],
        compiler_params=_mixer_cparams(),
        name="retention",
    )(p_main, p_main, p_main, p_main, cos, sin, decay, zeta, xi, gch,
      jnp.concatenate([r_w, r_w]).reshape(1, LANES))


MLSTM_CHUNK_LEN = LANES
MLSTM_PREP_UNROLL = 2
MLSTM_UNROLL = 3
N_GATE_TYPES = 4
N_FWD_GATES = N_GATE_TYPES


def _log_sigmoid(x):
    return jnp.minimum(x, 0.0) - jnp.log1p(jnp.exp(-jnp.abs(x)))


def _split3(x):
    hi = x.astype(BF16)
    r = x - hi.astype(F32)
    mid = r.astype(BF16)
    return hi, mid, (r - mid.astype(F32)).astype(BF16)


def _mlstm_pair_kernel(q_ref, k_ref, v_ref, og_ref, gc_ref, gr_ref, wq_ref, wk_ref, bq_ref, bk_ref, gbc_ref, gbr_ref,
                       mn_ref, o_ref, qc_ref, kt_ref, vb_ref, acc_ref, st_ref, cc_ref, cr_ref, *, t_ctx):
    L = MLSTM_CHUNK_LEN
    s_len = q_ref.shape[1]
    nc = s_len // L
    lane = lax.broadcasted_iota(jnp.int32, (1, LANES), 1)
    lo = lane < HEAD_DIM
    head_lanes = (lo, jnp.logical_not(lo))
    sub_lo = lax.broadcasted_iota(jnp.int32, (LANES, 1), 0) < HEAD_DIM
    head_rows = (sub_lo, jnp.logical_not(sub_lo))
    row_i = lax.broadcasted_iota(jnp.int32, (L, L), 0)
    col_i = lax.broadcasted_iota(jnp.int32, (L, L), 1)
    causal = (row_i >= col_i, row_i <= col_i)
    tri_b = causal[0].astype(BF16)
    tri_bt = causal[1].astype(BF16)
    sub = lax.broadcasted_iota(jnp.int32, (L, 1), 0)
    gate_row = lax.broadcasted_iota(jnp.int32, (2 * N_GATE_TYPES, 1), 0)
    ones = jnp.ones((L, LANES), BF16)

    def conv(x_ref, w_ref, b_ref, t0):
        x = x_ref[0, pl.ds(t0, L), :]
        prev = x_ref[0, pl.ds(jnp.maximum(t0 - SUBLANES, 0), SUBLANES), :][SUBLANES - 1:SUBLANES]
        nxt = x_ref[0, pl.ds(jnp.minimum(t0 + L, s_len - SUBLANES), SUBLANES), :][0:1]
        prev = jnp.where((t0 != 0) & (t0 != t_ctx), prev, 0.0)
        nxt = jnp.where((t0 + L != t_ctx) & (t0 + L != s_len), nxt, 0.0)
        xm = jnp.where(sub == 0, prev, pltpu.roll(x, 1, axis=0))
        xp = jnp.where(sub == L - 1, nxt, pltpu.roll(x, L - 1, axis=0))
        y = b_ref[...] + xm * w_ref[0:1, :] + x * w_ref[1:2, :] + xp * w_ref[2:3, :]
        return y * jax.nn.sigmoid(y)

    def prep(c, carry):
        t0 = pl.multiple_of(c * L, L)
        sl = pl.ds(t0, L)
        qc_ref[sl, :] = conv(q_ref, wq_ref, bq_ref, t0).astype(BF16)
        kt_ref[:, sl] = (conv(k_ref, wk_ref, bk_ref, t0) * HEAD_DIM ** -0.5).T.astype(BF16)
        vb_ref[sl, :] = v_ref[0, sl, :].astype(BF16)
        lf_col = _log_sigmoid(gc_ref[0, sl, :] + gbc_ref[0])
        lf_row = _log_sigmoid(gr_ref[0, 0, :, sl] + gbr_ref[0])
        pre_col = sum(jnp.dot(tri_b, p, preferred_element_type=F32) for p in _split3(lf_col))
        pre_row = sum(jnp.dot(p, tri_bt, preferred_element_type=F32) for p in _split3(lf_row))
        cc_ref[sl, :] = jnp.where(lane < N_FWD_GATES, pre_col, pre_col[L - 1:L, :] - pre_col + lf_col)
        cr_ref[:, sl] = jnp.where(gate_row < N_FWD_GATES, pre_row, pre_row[:, L - 1:L] - pre_row + lf_row)
        return carry
    lax.fori_loop(0, nc, prep, 0, unroll=MLSTM_PREP_UNROLL)

    def chunk(d, c, m_state):
        sl = pl.ds(pl.multiple_of(c * L, L), L)
        q, kt, v = qc_ref[sl, :], kt_ref[:, sl], vb_ref[sl, :]
        g_row = gr_ref[0, 0, :, sl] + gbr_ref[0]
        cum_col, cum_row = cc_ref[sl, :], cr_ref[:, sl]
        end = L - 1 if d == 0 else 0
        outs, new_m = [], []
        for h2 in range(2):
            m_st = m_state[h2]
            ci, cf = 2 * (2 * d) + h2, 2 * (2 * d + 1) + h2
            a_rep = jnp.broadcast_to(cum_col[:, cf:cf + 1], (L, LANES))
            i_row, a_row = g_row[ci:ci + 1, :], cum_row[cf:cf + 1, :]
            b_tot = a_row[:, end:end + 1]
            d_log = jnp.where(causal[d], a_rep + (i_row - a_row), NEG_INF)
            m_intra = jnp.broadcast_to(jnp.max(d_log, axis=-1, keepdims=True), (L, LANES))
            qm = jnp.where(head_lanes[h2], q, jnp.zeros_like(q))
            s = jnp.dot(qm, kt, preferred_element_type=F32) * jnp.exp(d_log - m_intra)
            v1 = jnp.where(head_lanes[h2], v, ones)
            intra = jnp.dot(s.astype(BF16), v1, preferred_element_type=F32)
            inter = jnp.dot(qm, st_ref[h2].astype(BF16), preferred_element_type=F32)
            inter_log = a_rep + m_st
            m_q = jnp.maximum(m_intra, inter_log)
            num_den = jnp.exp(inter_log - m_q) * inter + jnp.exp(m_intra - m_q) * intra
            den = pltpu.roll(num_den, HEAD_DIM, axis=1)
            outs.append(num_den / jnp.maximum(jnp.abs(den), jnp.exp(-m_q)))
            w_row = b_tot - a_row + i_row
            m_loc = jnp.max(w_row, axis=-1, keepdims=True)
            kts = jnp.where(head_rows[h2], kt.astype(F32) * jnp.exp(w_row - m_loc), 0.0).astype(BF16)
            loc = jnp.dot(kts, v1, preferred_element_type=F32)
            m_new = jnp.maximum(b_tot + m_st, m_loc)
            st_ref[h2] = jnp.exp(b_tot + m_st - m_new) * st_ref[h2] + jnp.exp(m_loc - m_new) * loc
            new_m.append(m_new)
        return sl, jnp.where(lo, outs[0], outs[1]), tuple(new_m)

    zero_m = (jnp.zeros((1, 1), F32), jnp.zeros((1, 1), F32))

    st_ref[...] = jnp.zeros_like(st_ref)

    def fwd(c, m_state):
        sl, y, m_state = chunk(0, c, m_state)
        acc_ref[sl, :] = y
        return m_state
    lax.fori_loop(0, nc, fwd, zero_m, unroll=MLSTM_UNROLL)

    st_ref[...] = jnp.zeros_like(st_ref)

    def bwd(i, m_state):
        c = _reverse_chunk(i, nc, t_ctx // L)
        sl, y, m_state = chunk(1, c, m_state)
        y = _head_rms(y + acc_ref[sl, :], mn_ref[...], lo)
        o_ref[0, sl, :] = (y * jax.nn.sigmoid(og_ref[0, sl, :])).astype(o_ref.dtype)
        return m_state
    lax.fori_loop(0, nc, bwd, zero_m, unroll=MLSTM_UNROLL)


def _mlstm(p_main, p_gate, conv_w, conv_b, gate_b, m_w, t_ctx, n_heads):
    b_sz, s_len, _ = p_main.shape
    n_hp = n_heads // 2
    L = MLSTM_CHUNK_LEN
    assert t_ctx % L == 0 and s_len % L == 0
    n_g = 2 * N_GATE_TYPES
    g_rows = jnp.transpose(p_gate.reshape(b_sz, s_len, n_hp, LANES)[..., :n_g], (0, 2, 3, 1))
    gb = jnp.transpose(gate_b.reshape(N_GATE_TYPES, n_hp, 2), (1, 0, 2)).reshape(n_hp, n_g)
    gb_col = jnp.zeros((n_hp, 1, LANES), F32).at[:, 0, :n_g].set(gb)
    gb_row = gb.reshape(n_hp, n_g, 1)
    vec = lambda col: pl.BlockSpec((1, LANES), lambda b, h: (0, col + h))
    return pl.pallas_call(
        functools.partial(_mlstm_pair_kernel, t_ctx=t_ctx),
        grid=(b_sz, n_hp),
        in_specs=[_seq_spec(s_len, 0, 2), _seq_spec(s_len, n_hp, 2), _seq_spec(s_len, 2 * n_hp, 2),
                  _seq_spec(s_len, 3 * n_hp), _seq_spec(s_len, 0),
                  pl.BlockSpec((1, 1, n_g, s_len), lambda b, h: (b, h, 0, 0)),
                  pl.BlockSpec((3, LANES), lambda b, h: (0, h)),
                  pl.BlockSpec((3, LANES), lambda b, h: (0, n_hp + h)),
                  vec(0), vec(n_hp),
                  pl.BlockSpec((1, 1, LANES), lambda b, h: (h, 0, 0)),
                  pl.BlockSpec((1, n_g, 1), lambda b, h: (h, 0, 0)),
                  pl.BlockSpec((1, LANES), lambda b, h: (0, 0))],
        out_specs=pl.BlockSpec((1, s_len, LANES), lambda b, h: (b, 0, h)),
        out_shape=jax.ShapeDtypeStruct((b_sz, s_len, n_heads * HEAD_DIM), BF16),
        scratch_shapes=[pltpu.VMEM((s_len, LANES), BF16), pltpu.VMEM((LANES, s_len), BF16),
                        pltpu.VMEM((s_len, LANES), BF16), pltpu.VMEM((s_len, LANES), F32),
                        pltpu.VMEM((2, LANES, LANES), F32),
                        pltpu.VMEM((s_len, LANES), F32), pltpu.VMEM((n_g, s_len), F32)],
        compiler_params=_mixer_cparams(),
        name="mlstm",
    )(p_main, p_main, p_main, p_main, p_gate, g_rows, conv_w, conv_w, conv_b.reshape(1, -1),
      conv_b.reshape(1, -1), gb_col, gb_row, jnp.concatenate([m_w, m_w]).reshape(1, LANES))


def kernel(x, c, ctx, c_ctx, w_mod, b_mod, norm_mix, norm_ffn, w_in, w_out, mlstm_conv_w, mlstm_conv_b,
           mlstm_gate_b, mlstm_norm, na_q_norm, na_k_norm, na_rpb, ret_norm, router_w, router_b,
           expert_w_up, expert_b_up, expert_w_down, expert_b_down):
    b_sz, t_len, d = x.shape
    t_ctx = ctx.shape[1]
    depth = w_in.shape[0]
    n_e = router_w.shape[2]
    d_mix = w_out.shape[1]
    h_m = d_mix // (4 * HEAD_DIM)
    h_na = d_mix // (2 * HEAD_DIM)
    h_r = d_mix // (4 * HEAD_DIM)
    d_m, d_na = h_m * HEAD_DIM, h_na * HEAD_DIM
    n_gate = N_GATE_TYPES * h_m
    assert b_sz + 1 <= MOD_ROWS and n_e <= LANES and n_gate <= LANES
    s_len = t_ctx + t_len
    assert s_len % ROW_TILE == 0 and s_len % COMBINE_ROWS == 0

    s = jnp.concatenate([ctx, x], axis=1)
    cc = jnp.zeros((MOD_ROWS, d), F32).at[:b_sz].set(c).at[b_sz].set(c_ctx)
    mods = _modulation(cc, w_mod, b_mod).reshape(depth, MOD_ROWS, 1, 6 * d)

    g0 = 4 * d_m
    col_na = g0
    col_ret = col_na + 3 * d_na
    n_hp = h_m // 2
    gate_src = np.array([[g0 + t * h_m + 2 * hp + h2 for t in range(N_GATE_TYPES) for h2 in range(2)]
                         for hp in range(n_hp)])
    rope_tabs = _rope_tables(t_ctx, t_len)
    ret_tabs = _ret_tables(h_r)
    na_tabs = _na_bias_tables(na_rpb)
    f2 = expert_w_up.shape[3]
    w_up_all = expert_w_up.reshape(depth * n_e, d, f2)
    w_down_all = expert_w_down.reshape(depth * n_e, f2 // 2, d)
    b_up_all = _regroup_bias(expert_b_up).reshape(depth * n_e, 1, f2)
    b_down_all = expert_b_down.reshape(depth * n_e, 1, d)

    experts_out = None
    for l in range(depth):
        w_main = jnp.concatenate([w_in[l, :, :g0], w_in[l, :, g0 + n_gate:]], axis=1).astype(BF16)
        w_gate = jnp.zeros((d, n_hp, LANES), BF16).at[:, :, :gate_src.shape[1]].set(
            w_in[l][:, gate_src].astype(BF16)).reshape(d, n_hp * LANES)
        if experts_out is None:
            p_main, p_gate = _proj_in(s, mods[l], norm_mix[l], w_main, w_gate, t_ctx)
        else:
            s, p_main, p_gate = _proj_in(s, mods[l], norm_mix[l], w_main, w_gate, t_ctx,
                                         combine=(*experts_out, mods[l - 1]))

        mix = [_mlstm(p_main, p_gate, mlstm_conv_w[l], mlstm_conv_b[l], mlstm_gate_b[l], mlstm_norm[l],
                      t_ctx, h_m),
               _na_attention(p_main, na_q_norm[l], na_k_norm[l], na_tabs[l], t_ctx, col_na, h_na),
               _retention(p_main, rope_tabs, ret_tabs, ret_norm[l], t_ctx, col_ret, h_r)]

        rw = jnp.zeros((d, LANES), F32).at[:, :n_e].set(router_w[l])
        rb = jnp.zeros((1, LANES), F32).at[0, :n_e].set(router_b[l])
        s, tok, logits = _proj_out(mix, s, mods[l], norm_ffn[l], w_out[l].astype(BF16), rw, rb, t_ctx)

        experts_out = _moe(tok, logits[..., :n_e], n_e, l, w_up_all, w_down_all, b_up_all, b_down_all)
    s = _combine(*experts_out, s, mods[depth - 1], t_ctx)
    return s[:, t_ctx:]
```

```python
import functools

import jax
import jax.numpy as jnp
import numpy as np
from jax import lax
from jax.experimental import pallas as pl
from jax.experimental.pallas import tpu as pltpu

F32 = jnp.float32
BF16 = jnp.bfloat16

GRID_W = 64
HEAD_DIM = 64
NA_ROWS = 8
NA_COLS = 16
ROPE_BASE = 10000.0
TOP_K = 4
SWIGLU_ALPHA = 1.702
SWIGLU_LIMIT = 7.0
NORM_EPS = 1e-6
NEG_INF = -1e30

LANES = 128
SUBLANES = 8
VMEM_LIMIT = 48 * 1024 * 1024
MOD_ROWS = 8
MOD_COL_TILES = 4
ROW_TILE = 768
COMBINE_ROWS = 256
MOE_TILE = 768
REGROUP_ROWS = 512
MOE_VMEM_LIMIT = 56 * 1024 * 1024


def _cparams(sem):
    return pltpu.CompilerParams(dimension_semantics=sem, vmem_limit_bytes=VMEM_LIMIT)


def _mod_kernel(cc_ref, w_ref, b_ref, o_ref):
    cc = cc_ref[...]
    a = cc * jax.nn.sigmoid(cc)
    o_ref[0] = jnp.dot(a, w_ref[0], precision=lax.Precision.HIGHEST,
                       preferred_element_type=F32) + b_ref[0]


def _modulation(cc, w_mod, b_mod):
    n_l, d, d6 = w_mod.shape
    tn = d6 // MOD_COL_TILES
    return pl.pallas_call(
        _mod_kernel,
        grid=(n_l, MOD_COL_TILES),
        in_specs=[pl.BlockSpec((MOD_ROWS, d), lambda l, j: (0, 0)),
                  pl.BlockSpec((1, d, tn), lambda l, j: (l, 0, j)),
                  pl.BlockSpec((1, 1, tn), lambda l, j: (l, 0, j))],
        out_specs=pl.BlockSpec((1, MOD_ROWS, tn), lambda l, j: (l, 0, j)),
        out_shape=jax.ShapeDtypeStruct((n_l, MOD_ROWS, d6), F32),
        compiler_params=_cparams(("arbitrary", "arbitrary")),
        name="adaln_modulation",
    )(cc, w_mod, b_mod.reshape(n_l, 1, d6))


def _pick_mod(mb_ref, mc_ref, k, d, is_ctx):
    vb = mb_ref[0, :, k * d:(k + 1) * d]
    vc = mc_ref[0, :, k * d:(k + 1) * d]
    return jnp.where(is_ctx, vc, vb)


def _ctx_rows(tm, t_ctx):
    return lax.broadcasted_iota(jnp.int32, (tm, 1), 0) + pl.program_id(1) * tm < t_ctx


def _rms(x, g):
    return x * lax.rsqrt(jnp.mean(x * x, axis=-1, keepdims=True) + NORM_EPS) * g


def _pack_bf16_pairs(x):
    m = x.shape[1] // 2
    hi = lax.bitcast_convert_type(x[:, :m].astype(BF16).astype(F32), jnp.uint32)
    lo = lax.bitcast_convert_type(x[:, m:].astype(BF16).astype(F32), jnp.uint32)
    return hi | (lo >> 16)


def _unpack_bf16_pairs(p):
    hi = lax.bitcast_convert_type(p & jnp.uint32(0xFFFF0000), F32)
    lo = lax.bitcast_convert_type(p << 16, F32)
    return jnp.concatenate([hi, lo], axis=1)


PROJ_IN_ROWS = 384
PROJ_IN_COLS = 512


def _project(s, is_ctx, mb_ref, mc_ref, g_ref, w_ref, wg_ref, pm_ref, pg_ref):
    d = s.shape[1]
    sh = _pick_mod(mb_ref, mc_ref, 0, d, is_ctx)
    sc = _pick_mod(mb_ref, mc_ref, 1, d, is_ctx)
    xn = (_rms(s, g_ref[...]) * (1.0 + sc) + sh).astype(BF16)
    pg_ref[0] = jnp.dot(xn, wg_ref[...], preferred_element_type=F32)
    for j in range(w_ref.shape[1] // PROJ_IN_COLS):
        cols = slice(j * PROJ_IN_COLS, (j + 1) * PROJ_IN_COLS)
        pm_ref[0, :, cols] = jnp.dot(xn, w_ref[:, cols], preferred_element_type=F32)


def _proj_in_kernel(s_ref, mb_ref, mc_ref, g_ref, w_ref, wg_ref, pm_ref, pg_ref, *, t_ctx):
    _project(s_ref[0], _ctx_rows(s_ref.shape[1], t_ctx), mb_ref, mc_ref, g_ref, w_ref, wg_ref, pm_ref, pg_ref)


def _combine_rows(y_ref, gt_ref, s_ref, mb_ref, mc_ref, is_ctx):
    g2 = _pick_mod(mb_ref, mc_ref, 5, s_ref.shape[2], is_ctx)
    gt = gt_ref[0]
    y = _unpack_bf16_pairs(y_ref[0, 0]) * gt[:, 0:1]
    for k in range(1, y_ref.shape[0]):
        y = y + _unpack_bf16_pairs(y_ref[k, 0]) * gt[:, k:k + 1]
    return s_ref[0] + g2 * y


def _combine_proj_in_kernel(y_ref, gt_ref, s_ref, pmb_ref, pmc_ref, mb_ref, mc_ref, g_ref, w_ref, wg_ref,
                            so_ref, pm_ref, pg_ref, *, t_ctx):
    is_ctx = _ctx_rows(s_ref.shape[1], t_ctx)
    s_new = _combine_rows(y_ref, gt_ref, s_ref, pmb_ref, pmc_ref, is_ctx)
    so_ref[0] = s_new
    _project(s_new, is_ctx, mb_ref, mc_ref, g_ref, w_ref, wg_ref, pm_ref, pg_ref)


def _proj_in(s, mod_l, g, w_main, w_gate, t_ctx, combine=None):
    b_sz, s_len, d = s.shape
    n_main = w_main.shape[1]
    n_gate = w_gate.shape[1]
    tm = PROJ_IN_ROWS
    assert s_len % tm == 0 and n_main % PROJ_IN_COLS == 0
    row_spec = pl.BlockSpec((1, tm, d), lambda b, i: (b, i, 0))
    mod_specs = [pl.BlockSpec((1, 1, 6 * d), lambda b, i: (b, 0, 0)),
                 pl.BlockSpec((1, 1, 6 * d), lambda b, i: (b_sz, 0, 0))]
    w_specs = [pl.BlockSpec((1, d), lambda b, i: (0, 0)),
               pl.BlockSpec((d, n_main), lambda b, i: (0, 0)),
               pl.BlockSpec((d, n_gate), lambda b, i: (0, 0))]
    out_specs = [pl.BlockSpec((1, tm, n_main), lambda b, i: (b, i, 0)),
                 pl.BlockSpec((1, tm, n_gate), lambda b, i: (b, i, 0))]
    out_shape = [jax.ShapeDtypeStruct((b_sz, s_len, n_main), F32),
                 jax.ShapeDtypeStruct((b_sz, s_len, n_gate), F32)]
    if combine is None:
        return pl.pallas_call(
            functools.partial(_proj_in_kernel, t_ctx=t_ctx),
            grid=(b_sz, s_len // tm),
            in_specs=[row_spec] + mod_specs + w_specs,
            out_specs=out_specs, out_shape=out_shape,
            compiler_params=_cparams(("arbitrary", "arbitrary")),
            name="proj_in",
        )(s, mod_l, mod_l, g.reshape(1, d), w_main, w_gate)
    y_as, gates, mod_prev = combine
    n_k = y_as.shape[0]
    return pl.pallas_call(
        functools.partial(_combine_proj_in_kernel, t_ctx=t_ctx),
        grid=(b_sz, s_len // tm),
        in_specs=[pl.BlockSpec((n_k, 1, tm, d // 2), lambda b, i: (0, b, i, 0)),
                  pl.BlockSpec((1, tm, n_k), lambda b, i: (b, i, 0)),
                  row_spec] + mod_specs + mod_specs + w_specs,
        out_specs=[row_spec] + out_specs,
        out_shape=[jax.ShapeDtypeStruct((b_sz, s_len, d), F32)] + out_shape,
        compiler_params=_cparams(("arbitrary", "arbitrary")),
        name="combine_proj_in",
    )(y_as, gates, s, mod_prev, mod_prev, mod_l, mod_l, g.reshape(1, d), w_main, w_gate)


def _proj_out_kernel(ma_ref, mb2_ref, mc2_ref, s_ref, mb_ref, mc_ref, g_ref, w_ref, rw_ref, rb_ref,
                     so_ref, tok_ref, lg_ref, *, t_ctx):
    d = s_ref.shape[2]
    is_ctx = _ctx_rows(s_ref.shape[1], t_ctx)
    g1 = _pick_mod(mb_ref, mc_ref, 2, d, is_ctx)
    ka, kb = ma_ref.shape[2], ma_ref.shape[2] + mb2_ref.shape[2]
    y = (jnp.dot(ma_ref[0], w_ref[0:ka, :], preferred_element_type=F32)
         + jnp.dot(mb2_ref[0], w_ref[ka:kb, :], preferred_element_type=F32)
         + jnp.dot(mc2_ref[0], w_ref[kb:, :], preferred_element_type=F32))
    s_new = s_ref[0] + g1 * y
    so_ref[0] = s_new
    sh = _pick_mod(mb_ref, mc_ref, 3, d, is_ctx)
    sc = _pick_mod(mb_ref, mc_ref, 4, d, is_ctx)
    t = _rms(s_new, g_ref[...]) * (1.0 + sc) + sh
    tok_ref[0] = t
    w = rw_ref[...]
    t_hi, w_hi = t.astype(BF16), w.astype(BF16)
    t_lo = (t - t_hi.astype(F32)).astype(BF16)
    w_lo = (w - w_hi.astype(F32)).astype(BF16)
    lg_ref[0] = (jnp.dot(t_hi, w_hi, preferred_element_type=F32) + jnp.dot(t_hi, w_lo, preferred_element_type=F32)
                 + jnp.dot(t_lo, w_hi, preferred_element_type=F32) + rb_ref[...])


def _proj_out(mix_parts, s, mod_l, g, w_out, rw, rb, t_ctx):
    b_sz, s_len, d = s.shape
    tm = ROW_TILE
    row_spec = pl.BlockSpec((1, tm, d), lambda b, i: (b, i, 0))
    part_specs = [pl.BlockSpec((1, tm, m.shape[2]), lambda b, i: (b, i, 0)) for m in mix_parts]
    return pl.pallas_call(
        functools.partial(_proj_out_kernel, t_ctx=t_ctx),
        grid=(b_sz, s_len // tm),
        in_specs=part_specs + [row_spec,
                               pl.BlockSpec((1, 1, 6 * d), lambda b, i: (b, 0, 0)),
                               pl.BlockSpec((1, 1, 6 * d), lambda b, i: (b_sz, 0, 0)),
                               pl.BlockSpec((1, d), lambda b, i: (0, 0)),
                               pl.BlockSpec((d, d), lambda b, i: (0, 0)),
                               pl.BlockSpec((d, LANES), lambda b, i: (0, 0)),
                               pl.BlockSpec((1, LANES), lambda b, i: (0, 0))],
        out_specs=[row_spec, row_spec, pl.BlockSpec((1, tm, LANES), lambda b, i: (b, i, 0))],
        out_shape=[jax.ShapeDtypeStruct((b_sz, s_len, d), F32),
                   jax.ShapeDtypeStruct((b_sz, s_len, d), F32),
                   jax.ShapeDtypeStruct((b_sz, s_len, LANES), F32)],
        compiler_params=_cparams(("arbitrary", "arbitrary")),
        name="proj_out_router",
    )(*mix_parts, s, mod_l, mod_l, g.reshape(1, d), w_out, rw, rb)


PAIR = 2 * LANES


def _regroup_perm():
    dst = np.arange(PAIR)
    src = np.where(dst < LANES, 2 * dst, 2 * (dst - LANES) + 1)
    return jnp.asarray(np.arange(PAIR)[:, None] == src[None, :], BF16)


def _regroup_bias(b_up):
    lead = b_up.shape[:-1]
    b = b_up.reshape(*lead, -1, LANES, 2)
    return jnp.swapaxes(b, -1, -2).reshape(*lead, -1)


def _moe_kernel(be_ref, nu_ref, x_ref, wu_ref, wd_ref, bu_ref, bd_ref, p_ref, y_ref, wus_ref, wds_ref):
    i = pl.program_id(0)
    d, f2 = wus_ref.shape
    rows = min(REGROUP_ROWS, d)

    @pl.when(i < nu_ref[0])
    def _():
        @pl.when((i == 0) | (be_ref[i] != be_ref[jnp.maximum(i - 1, 0)]))
        def _():
            for r in range(d // rows):
                for j in range(f2 // PAIR):
                    w = wu_ref[0, r * rows:(r + 1) * rows, j * PAIR:(j + 1) * PAIR].astype(BF16)
                    wus_ref[r * rows:(r + 1) * rows, j * PAIR:(j + 1) * PAIR] = jnp.dot(
                        w, p_ref[...], preferred_element_type=F32).astype(BF16)
            wds_ref[...] = wd_ref[0].astype(BF16)

        up = jnp.dot(x_ref[...].astype(BF16), wus_ref[...], preferred_element_type=F32) + bu_ref[0]
        acts = []
        for j in range(f2 // PAIR):
            glu = jnp.minimum(up[:, j * PAIR:j * PAIR + LANES], SWIGLU_LIMIT)
            lin = jnp.clip(up[:, j * PAIR + LANES:(j + 1) * PAIR], -SWIGLU_LIMIT, SWIGLU_LIMIT)
            acts.append((glu * jax.nn.sigmoid(SWIGLU_ALPHA * glu) * (lin + 1.0)).astype(BF16))
        act = jnp.concatenate(acts, axis=1)
        y_ref[...] = _pack_bf16_pairs(jnp.dot(act, wds_ref[...], preferred_element_type=F32) + bd_ref[0])

    @pl.when(i >= nu_ref[0])
    def _():
        y_ref[...] = jnp.zeros_like(y_ref)


def _moe_blocks(blk_e, n_used, x_sorted, w_up, w_down, b_up, b_down):
    n_rows, d = x_sorted.shape
    _, _, f2 = w_up.shape
    assert f2 % PAIR == 0 and d % min(REGROUP_ROWS, d) == 0
    tm = MOE_TILE
    wmap = lambda i, be, nu: (be[i], 0, 0)
    return pl.pallas_call(
        _moe_kernel,
        grid_spec=pltpu.PrefetchScalarGridSpec(
            num_scalar_prefetch=2,
            grid=(n_rows // tm,),
            in_specs=[pl.BlockSpec((tm, d), lambda i, be, nu: (i, 0)),
                      pl.BlockSpec((1, d, f2), wmap),
                      pl.BlockSpec((1, f2 // 2, d), wmap),
                      pl.BlockSpec((1, 1, f2), wmap),
                      pl.BlockSpec((1, 1, d), wmap),
                      pl.BlockSpec((PAIR, PAIR), lambda i, be, nu: (0, 0))],
            out_specs=pl.BlockSpec((tm, d // 2), lambda i, be, nu: (i, 0)),
            scratch_shapes=[pltpu.VMEM((d, f2), BF16), pltpu.VMEM((f2 // 2, d), BF16)]),
        out_shape=jax.ShapeDtypeStruct((n_rows, d // 2), jnp.uint32),
        compiler_params=pltpu.CompilerParams(dimension_semantics=("arbitrary",),
                                             vmem_limit_bytes=MOE_VMEM_LIMIT),
        name="moe_expert_blocks",
    )(blk_e, n_used, x_sorted, w_up, w_down, b_up, b_down, _regroup_perm())


def _combine_kernel(y_ref, gt_ref, s_ref, mb_ref, mc_ref, o_ref, *, t_ctx):
    o_ref[0] = _combine_rows(y_ref, gt_ref, s_ref, mb_ref, mc_ref, _ctx_rows(s_ref.shape[1], t_ctx))


def _combine(y_as, gates, s, mod_l, t_ctx):
    b_sz, s_len, d = s.shape
    n_k = y_as.shape[0]
    tm = COMBINE_ROWS
    row_spec = pl.BlockSpec((1, tm, d), lambda b, i: (b, i, 0))
    return pl.pallas_call(
        functools.partial(_combine_kernel, t_ctx=t_ctx),
        grid=(b_sz, s_len // tm),
        in_specs=[pl.BlockSpec((n_k, 1, tm, d // 2), lambda b, i: (0, b, i, 0)),
                  pl.BlockSpec((1, tm, n_k), lambda b, i: (b, i, 0)),
                  row_spec,
                  pl.BlockSpec((1, 1, 6 * d), lambda b, i: (b, 0, 0)),
                  pl.BlockSpec((1, 1, 6 * d), lambda b, i: (b_sz, 0, 0))],
        out_specs=row_spec,
        out_shape=jax.ShapeDtypeStruct((b_sz, s_len, d), F32),
        compiler_params=_cparams(("arbitrary", "arbitrary")),
        name="moe_combine",
    )(y_as, gates, s, mod_l, mod_l)


def _moe(tok, logits, n_e, layer, w_up, w_down, b_up, b_down):
    b_sz, s_len, d = tok.shape
    n_tok = b_sz * s_len
    tm = MOE_TILE
    top_v, top_e = lax.top_k(logits.reshape(n_tok, n_e), TOP_K)
    gates = jax.nn.softmax(top_v, axis=-1)
    n_as = n_tok * TOP_K
    onehot = jnp.sum((top_e[:, :, None] == jnp.arange(n_e)[None, None, :]).astype(jnp.int32), axis=1)
    csum = jnp.cumsum(onehot, axis=0)
    counts = csum[-1]
    padded = (counts + tm - 1) // tm * tm
    end_pad = jnp.cumsum(padded)
    start_pad = end_pad - padded
    start = jnp.cumsum(counts) - counts
    dest = jnp.take_along_axis(csum - onehot + start_pad[None, :], top_e, axis=1).astype(jnp.int32)
    n_blocks = -(-n_as // tm) + n_e
    blk_first = jnp.arange(n_blocks) * tm
    blk_e = jnp.minimum(jnp.sum(blk_first[:, None] >= end_pad[None, :], axis=1), n_e - 1).astype(jnp.int32)
    n_used = (end_pad[-1:] // tm).astype(jnp.int32)
    tok_sorted = (jnp.argsort(top_e.reshape(n_as)) // TOP_K).astype(jnp.int32)
    j = (blk_first - start_pad[blk_e])[:, None] + jnp.arange(tm)[None, :]
    src = jnp.clip(start[blk_e][:, None] + j, 0, n_as - 1)
    row_tok = jnp.where(j < counts[blk_e][:, None], tok_sorted[src], 0).reshape(n_blocks * tm)
    x_sorted = tok.reshape(n_tok, d)[row_tok]
    y = _moe_blocks(blk_e + layer * n_e, n_used, x_sorted, w_up, w_down, b_up, b_down)
    y_as = y[dest.T].reshape(TOP_K, b_sz, s_len, d // 2)
    return y_as, gates.reshape(b_sz, s_len, TOP_K)


MIXER_VMEM_LIMIT = 58 * 1024 * 1024


def _seq_spec(s_len, col, buffers=1):
    return pl.BlockSpec((1, s_len, LANES), lambda b, h: (b, 0, col + h), pipeline_mode=pl.Buffered(buffers))


def _mixer_cparams():
    return pltpu.CompilerParams(dimension_semantics=("arbitrary", "arbitrary"), vmem_limit_bytes=MIXER_VMEM_LIMIT)


def _head_rms(x, w, lo):
    xx = x * x
    s0 = jnp.sum(jnp.where(lo, xx, 0.0), axis=-1, keepdims=True)
    s1 = jnp.sum(jnp.where(lo, 0.0, xx), axis=-1, keepdims=True)
    inv = lax.rsqrt(jnp.where(lo, s0, s1) * (1.0 / HEAD_DIM) + NORM_EPS)
    return x * inv * w


def _reverse_chunk(i, nc, n_ctx):
    return jnp.where(i < n_ctx, n_ctx - 1 - i, nc - 1 - i + n_ctx)


NA_STEP = 256
NA_KEYS = NA_ROWS * GRID_W
NA_GROUPS_PER_ITER = 4
NA_PREP_UNROLL = 3
NA_SHIFT_UNROLL = 5


def _na_bias_tables(rpb):
    n_l, n_h, n_dr, n_dc = rpb.shape
    col = np.arange(GRID_W)
    dc = np.clip(col[None, :] - col[:, None] + NA_COLS - 1, 0, n_dc - 1)
    onehot = (dc.reshape(1, -1) == np.arange(n_dc)[:, None]).astype(np.float32)
    toe = jnp.dot(rpb.reshape(-1, n_dc), jnp.asarray(onehot), precision=lax.Precision.HIGHEST)
    toe = toe.reshape(n_l, n_h, n_dr, GRID_W, GRID_W)
    col_start = np.clip(col - NA_COLS // 2, 0, GRID_W - NA_COLS)
    col_in = (col[None, :] >= col_start[:, None]) & (col[None, :] < col_start[:, None] + NA_COLS)
    toe = jnp.where(jnp.asarray(col_in), toe, NEG_INF)
    tabs = []
    for dr0 in range(NA_ROWS):
        t = toe[:, :, dr0:dr0 + NA_ROWS].reshape(n_l, n_h // 2, 2, NA_ROWS, GRID_W, GRID_W)
        t = jnp.transpose(t, (0, 1, 2, 4, 3, 5))
        tabs.append(t.reshape(n_l, n_h // 2, 2 * GRID_W, NA_KEYS))
    return jnp.stack(tabs, axis=1)


def _na_seq_kernel(q_ref, k_ref, v_ref, qw_ref, kw_ref, bias_ref, o_ref, kt_ref, vb_ref, s_ref, kn_ref, *,
                   t_ctx, rows):
    s_len = k_ref.shape[1]
    lo = lax.broadcasted_iota(jnp.int32, (1, LANES), 1) < HEAD_DIM
    rows_per_step = NA_STEP // GRID_W

    def prep(c, carry):
        t0 = pl.multiple_of(c * NA_STEP, NA_STEP)
        kn = _head_rms(k_ref[0, pl.ds(t0, NA_STEP), :], kw_ref[...], lo).astype(BF16)
        kn_ref[pl.ds(t0, NA_STEP), :] = kn
        kt_ref[0, :, pl.ds(t0, NA_STEP)] = kn.T
        vb_ref[pl.ds(t0, NA_STEP), :] = jnp.concatenate(
            [v_ref[0, pl.ds(t0, NA_STEP), :].astype(BF16), jnp.ones((NA_STEP, LANES), BF16)], axis=1)
        return carry
    lax.fori_loop(0, s_len // NA_STEP, prep, 0, unroll=NA_PREP_UNROLL)

    def prep_shifted(c, carry):
        t0 = pl.multiple_of(c * LANES, LANES)
        kt_ref[1, :, pl.ds(t0, LANES)] = kn_ref[pl.ds(pl.multiple_of(t0 + GRID_W, GRID_W), LANES), :].T
        return carry
    lax.fori_loop(0, (s_len - GRID_W) // LANES, prep_shifted, 0, unroll=NA_SHIFT_UNROLL)

    kc_t = kt_ref[0, :, 0:t_ctx]
    vc = vb_ref[0:t_ctx, :]

    def queries(tok0):
        qn = _head_rms(q_ref[0, pl.ds(tok0, NA_STEP), :], qw_ref[...], lo) * (HEAD_DIM ** -0.5)
        q0 = jnp.where(lo, qn, 0.0).astype(BF16)
        q1 = jnp.where(lo, 0.0, qn).astype(BF16)
        return [jnp.concatenate([q0[g * GRID_W:(g + 1) * GRID_W], q1[g * GRID_W:(g + 1) * GRID_W]], axis=0)
                for g in range(rows_per_step)]

    def finish(tok0, g, o2):
        o2 = o2[:, :LANES] * (1.0 / o2[:, LANES:])
        o = jnp.where(lo, o2[:GRID_W], o2[GRID_W:])
        o_ref[0, pl.ds(tok0 + g * GRID_W, GRID_W), :] = o.astype(o_ref.dtype)

    for g, q2 in enumerate(queries(0)):
        s_c = jnp.dot(q2, kc_t, preferred_element_type=F32)
        p_c = jnp.exp(s_c - jnp.max(s_c, axis=-1, keepdims=True))
        finish(0, g, jnp.dot(p_c.astype(BF16), vc, preferred_element_type=F32))

    def row_group(rg, slot):
        tok0 = pl.multiple_of(t_ctx + rg * NA_STEP, NA_STEP)
        t0s = []
        s_slot = s_ref.at[slot]
        for g, q2 in enumerate(queries(tok0)):
            r = rg * rows_per_step + g
            row_start = jnp.clip(r - NA_ROWS // 2, 0, rows - NA_ROWS)
            dr0 = row_start - r + NA_ROWS - 1
            t0 = pl.multiple_of(t_ctx + row_start * GRID_W, GRID_W)
            odd = (t0 // GRID_W) % (LANES // GRID_W)
            kw_t = kt_ref[odd, :, pl.ds(pl.multiple_of(t0 - odd * GRID_W, LANES), NA_KEYS)]
            s_slot[g, :, :NA_KEYS] = jnp.dot(q2, kw_t, preferred_element_type=F32) + bias_ref[dr0, 0]
            s_slot[g, :, NA_KEYS:] = jnp.dot(q2, kc_t, preferred_element_type=F32)
            t0s.append(t0)
        for g in range(rows_per_step):
            s = s_slot[g]
            p = jnp.exp(s - jnp.max(s, axis=-1, keepdims=True)).astype(BF16)
            finish(tok0, g, jnp.dot(p[:, :NA_KEYS], vb_ref[pl.ds(t0s[g], NA_KEYS), :], preferred_element_type=F32)
                   + jnp.dot(p[:, NA_KEYS:], vc, preferred_element_type=F32))

    def row_groups(i, carry):
        for slot in range(NA_GROUPS_PER_ITER):
            row_group(i * NA_GROUPS_PER_ITER + slot, slot)
        return carry
    lax.fori_loop(0, rows // rows_per_step // NA_GROUPS_PER_ITER, row_groups, 0)


def _na_attention(p_main, q_w, k_w, bias_tab, t_ctx, col_q, n_heads):
    b_sz, s_len, _ = p_main.shape
    n_hg = n_heads // 2
    rows = (s_len - t_ctx) // GRID_W
    rows_per_step = NA_STEP // GRID_W
    assert t_ctx == NA_STEP and s_len % NA_STEP == 0 and rows >= NA_ROWS
    assert rows % (rows_per_step * NA_GROUPS_PER_ITER) == 0
    cq = col_q // LANES
    seq = lambda col: pl.BlockSpec((1, s_len, LANES), lambda b, h: (b, 0, col + h))
    w2 = lambda w: jnp.concatenate([w, w]).reshape(1, LANES)
    return pl.pallas_call(
        functools.partial(_na_seq_kernel, t_ctx=t_ctx, rows=rows),
        grid=(b_sz, n_hg),
        in_specs=[seq(cq), seq(cq + n_hg), _seq_spec(s_len, cq + 2 * n_hg),
                  pl.BlockSpec((1, LANES), lambda b, h: (0, 0)),
                  pl.BlockSpec((1, LANES), lambda b, h: (0, 0)),
                  pl.BlockSpec((NA_ROWS, 1, 2 * GRID_W, NA_KEYS), lambda b, h: (0, h, 0, 0))],
        out_specs=pl.BlockSpec((1, s_len, LANES), lambda b, h: (b, 0, h)),
        out_shape=jax.ShapeDtypeStruct((b_sz, s_len, n_heads * HEAD_DIM), BF16),
        scratch_shapes=[pltpu.VMEM((2, LANES, s_len), BF16), pltpu.VMEM((s_len, 2 * LANES), BF16),
                        pltpu.VMEM((NA_GROUPS_PER_ITER, rows_per_step, 2 * GRID_W, NA_KEYS + t_ctx), F32),
                        pltpu.VMEM((s_len, LANES), BF16)],
        compiler_params=_cparams(("arbitrary", "arbitrary")),
        name="na_attention",
    )(p_main, p_main, p_main, w2(q_w), w2(k_w), bias_tab)


RET_CHUNK_LEN = 128
RET_UNROLL = 6


def _rope_tables(t_ctx, t_len):
    pos = jnp.arange(t_len)
    row = (pos // GRID_W).astype(F32)
    col = (pos % GRID_W).astype(F32)
    n = HEAD_DIM // 4
    inv = ROPE_BASE ** (-jnp.arange(n, dtype=F32) / n)
    ar = row[:, None] * inv
    ac = col[:, None] * inv
    cos = jnp.concatenate([jnp.cos(ar), jnp.cos(ar), jnp.cos(ac), jnp.cos(ac)], axis=-1)
    sin = jnp.concatenate([-jnp.sin(ar), jnp.sin(ar), -jnp.sin(ac), jnp.sin(ac)], axis=-1)
    cos = jnp.concatenate([jnp.ones((t_ctx, HEAD_DIM), F32), cos], axis=0)
    sin = jnp.concatenate([jnp.zeros((t_ctx, HEAD_DIM), F32), sin], axis=0)
    return jnp.tile(cos, (1, 2)), jnp.tile(sin, (1, 2))


def _ret_tables(n_heads):
    L = RET_CHUNK_LEN
    pos = np.arange(L, dtype=np.float32)
    lane_head = np.arange(LANES) // HEAD_DIM
    decay = np.zeros((2, n_heads, L, L), np.float32)
    zeta = np.zeros((2, n_heads // 2, LANES, L), np.float32)
    xi = np.zeros((2, n_heads // 2, L, LANES), np.float32)
    gch = np.zeros((2, n_heads // 2, 1, LANES), np.float32)
    for d, first_exp in enumerate((5.0, 6.0)):
        e = np.float32(first_exp) + np.float32(2.0) * np.arange(n_heads, dtype=np.float32)
        lg = np.log1p(-np.exp2(-e)).astype(np.float32)
        diff = pos[:, None] - pos[None, :]
        if d == 1:
            diff = -diff
        for h in range(n_heads):
            decay[d, h] = np.where(diff >= 0, np.exp(lg[h] * np.maximum(diff, 0.0)), 0.0)
        for hp in range(n_heads // 2):
            lgl = lg[2 * hp + lane_head][None, :]
            to_end = (L - 1 - pos if d == 0 else pos)[:, None]
            zeta[d, hp] = np.exp(lgl * to_end).T
            xi[d, hp] = np.exp(lgl * (L - to_end))
            gch[d, hp] = np.exp(lgl * L)
    return tuple(jnp.asarray(a) for a in (decay, zeta, xi, gch))


def _ret_kernel(q_ref, k_ref, v_ref, g_ref, cos_ref, sin_ref, dec_ref, zeta_ref, xi_ref, gch_ref, rn_ref,
                o_ref, qr_ref, kt_ref, vb_ref, acc_ref, *, t_ctx):
    L = RET_CHUNK_LEN
    s_len = q_ref.shape[1]
    nc = s_len // L
    lane = lax.broadcasted_iota(jnp.int32, (1, LANES), 1)
    lo = lane < HEAD_DIM
    half = (lane & (HEAD_DIM // 4)) == 0
    rid = lax.broadcasted_iota(jnp.int32, (LANES, LANES), 0) < HEAD_DIM
    cid = lax.broadcasted_iota(jnp.int32, (LANES, LANES), 1) < HEAD_DIM
    same_head = rid == cid

    def rope(x, cos, sin):
        up = pltpu.roll(x, LANES - HEAD_DIM // 4, axis=1)
        dn = pltpu.roll(x, HEAD_DIM // 4, axis=1)
        return x * cos + jnp.where(half, up, dn) * sin

    def prep(c, carry):
        sl = pl.ds(pl.multiple_of(c * L, L), L)
        cos, sin = cos_ref[sl, :], sin_ref[sl, :]
        qr_ref[sl, :] = rope(q_ref[0, sl, :], cos, sin).astype(BF16)
        kt_ref[:, sl] = (rope(k_ref[0, sl, :], cos, sin) * HEAD_DIM ** -0.5).T.astype(BF16)
        vb_ref[sl, :] = v_ref[0, sl, :].astype(BF16)
        return carry
    lax.fori_loop(0, nc, prep, 0, unroll=RET_UNROLL)

    def chunk(d, c, state):
        sl = pl.ds(pl.multiple_of(c * L, L), L)
        q, kt, v = qr_ref[sl, :], kt_ref[:, sl], vb_ref[sl, :]
        inter = jnp.dot(q, state.astype(BF16), preferred_element_type=F32) * xi_ref[d, 0]
        outs = []
        for h2 in range(2):
            qm = jnp.where(lo if h2 == 0 else jnp.logical_not(lo), q, jnp.zeros_like(q))
            sd = (jnp.dot(qm, kt, preferred_element_type=F32) * dec_ref[d, h2]).astype(BF16)
            outs.append(jnp.dot(sd, v, preferred_element_type=F32))
        y = jnp.where(lo, outs[0], outs[1]) + inter
        kz_t = (kt.astype(F32) * zeta_ref[d, 0]).astype(BF16)
        state = state * gch_ref[d, 0] + jnp.where(same_head, jnp.dot(kz_t, v, preferred_element_type=F32), 0.0)
        return sl, y, state

    def fwd(c, state):
        sl, y, state = chunk(0, c, state)
        acc_ref[sl, :] = y
        return state
    lax.fori_loop(0, nc, fwd, jnp.zeros((LANES, LANES), F32), unroll=RET_UNROLL)

    def bwd(i, state):
        c = _reverse_chunk(i, nc, t_ctx // L)
        sl, y, state = chunk(1, c, state)
        y = _head_rms(y + acc_ref[sl, :], rn_ref[...], lo)
        g = g_ref[0, sl, :]
        o_ref[0, sl, :] = (y * (g * jax.nn.sigmoid(g))).astype(o_ref.dtype)
        return state
    lax.fori_loop(0, nc, bwd, jnp.zeros((LANES, LANES), F32), unroll=RET_UNROLL)


def _retention(p_main, rope_tabs, ret_tabs, r_w, t_ctx, col_q, n_heads):
    b_sz, s_len, _ = p_main.shape
    n_hp = n_heads // 2
    L = RET_CHUNK_LEN
    assert t_ctx % L == 0 and s_len % L == 0
    c0 = col_q // LANES
    cos, sin = rope_tabs
    decay, zeta, xi, gch = ret_tabs
    const2 = pl.BlockSpec((s_len, LANES), lambda b, h: (0, 0), pipeline_mode=pl.Buffered(1))
    return pl.pallas_call(
        functools.partial(_ret_kernel, t_ctx=t_ctx),
        grid=(b_sz, n_hp),
        in_specs=[_seq_spec(s_len, c0, 2), _seq_spec(s_len, c0 + n_hp, 2), _seq_spec(s_len, c0 + 2 * n_hp, 2),
                  _seq_spec(s_len, c0 + 3 * n_hp), const2, const2,
                  pl.BlockSpec((2, 2, L, L), lambda b, h: (0, h, 0, 0)),
                  pl.BlockSpec((2, 1, LANES, L), lambda b, h: (0, h, 0, 0)),
                  pl.BlockSpec((2, 1, L, LANES), lambda b, h: (0, h, 0, 0)),
                  pl.BlockSpec((2, 1, 1, LANES), lambda b, h: (0, h, 0, 0)),
                  pl.BlockSpec((1, LANES), lambda b, h: (0, 0))],
        out_specs=pl.BlockSpec((1, s_len, LANES), lambda b, h: (b, 0, h)),
        out_shape=jax.ShapeDtypeStruct((b_sz, s_len, n_heads * HEAD_DIM), BF16),
        scratch_shapes=[pltpu.VMEM((s_len, LANES), BF16), pltpu.VMEM((LANES, s_len), BF16),
                        pltpu.VMEM((s_len, LANES), BF16), pltpu.VMEM((s_len, LANES), F32)],
        compiler_params=_mixer_cparams(),
        name="retention",
    )(p_main, p_main, p_main, p_main, cos, sin, decay, zeta, xi, gch,
      jnp.concatenate([r_w, r_w]).reshape(1, LANES))


MLSTM_CHUNK_LEN = LANES
MLSTM_PREP_UNROLL = 3
MLSTM_UNROLL = 6
N_GATE_TYPES = 4
N_FWD_GATES = N_GATE_TYPES


def _log_sigmoid(x):
    return jnp.minimum(x, 0.0) - jnp.log1p(jnp.exp(-jnp.abs(x)))


def _split3(x):
    hi = x.astype(BF16)
    r = x - hi.astype(F32)
    mid = r.astype(BF16)
    return hi, mid, (r - mid.astype(F32)).astype(BF16)


def _mlstm_pair_kernel(q_ref, k_ref, v_ref, og_ref, gc_ref, gr_ref, wq_ref, wk_ref, bq_ref, bk_ref, gbc_ref, gbr_ref,
                       mn_ref, o_ref, qc_ref, kt_ref, vb_ref, acc_ref, st_ref, cc_ref, cr_ref, *, t_ctx):
    L = MLSTM_CHUNK_LEN
    s_len = q_ref.shape[1]
    nc = s_len // L
    lane = lax.broadcasted_iota(jnp.int32, (1, LANES), 1)
    lo = lane < HEAD_DIM
    head_lanes = (lo, jnp.logical_not(lo))
    sub_lo = lax.broadcasted_iota(jnp.int32, (LANES, 1), 0) < HEAD_DIM
    head_rows = (sub_lo, jnp.logical_not(sub_lo))
    row_i = lax.broadcasted_iota(jnp.int32, (L, L), 0)
    col_i = lax.broadcasted_iota(jnp.int32, (L, L), 1)
    causal = (row_i >= col_i, row_i <= col_i)
    tri_b = causal[0].astype(BF16)
    tri_bt = causal[1].astype(BF16)
    sub = lax.broadcasted_iota(jnp.int32, (L, 1), 0)
    gate_row = lax.broadcasted_iota(jnp.int32, (2 * N_GATE_TYPES, 1), 0)
    ones = jnp.ones((L, LANES), BF16)

    def conv(x_ref, w_ref, b_ref, t0):
        x = x_ref[0, pl.ds(t0, L), :]
        prev = x_ref[0, pl.ds(jnp.maximum(t0 - SUBLANES, 0), SUBLANES), :][SUBLANES - 1:SUBLANES]
        nxt = x_ref[0, pl.ds(jnp.minimum(t0 + L, s_len - SUBLANES), SUBLANES), :][0:1]
        prev = jnp.where((t0 != 0) & (t0 != t_ctx), prev, 0.0)
        nxt = jnp.where((t0 + L != t_ctx) & (t0 + L != s_len), nxt, 0.0)
        xm = jnp.where(sub == 0, prev, pltpu.roll(x, 1, axis=0))
        xp = jnp.where(sub == L - 1, nxt, pltpu.roll(x, L - 1, axis=0))
        y = b_ref[...] + xm * w_ref[0:1, :] + x * w_ref[1:2, :] + xp * w_ref[2:3, :]
        return y * jax.nn.sigmoid(y)

    def prep(c, carry):
        t0 = pl.multiple_of(c * L, L)
        sl = pl.ds(t0, L)
        qc_ref[sl, :] = conv(q_ref, wq_ref, bq_ref, t0).astype(BF16)
        kt_ref[:, sl] = (conv(k_ref, wk_ref, bk_ref, t0) * HEAD_DIM ** -0.5).T.astype(BF16)
        vb_ref[sl, :] = v_ref[0, sl, :].astype(BF16)
        lf_col = _log_sigmoid(gc_ref[0, sl, :] + gbc_ref[0])
        lf_row = _log_sigmoid(gr_ref[0, 0, :, sl] + gbr_ref[0])
        pre_col = sum(jnp.dot(tri_b, p, preferred_element_type=F32) for p in _split3(lf_col))
        pre_row = sum(jnp.dot(p, tri_bt, preferred_element_type=F32) for p in _split3(lf_row))
        cc_ref[sl, :] = jnp.where(lane < N_FWD_GATES, pre_col, pre_col[L - 1:L, :] - pre_col + lf_col)
        cr_ref[:, sl] = jnp.where(gate_row < N_FWD_GATES, pre_row, pre_row[:, L - 1:L] - pre_row + lf_row)
        return carry
    lax.fori_loop(0, nc, prep, 0, unroll=MLSTM_PREP_UNROLL)

    def chunk(d, c, m_state):
        sl = pl.ds(pl.multiple_of(c * L, L), L)
        q, kt, v = qc_ref[sl, :], kt_ref[:, sl], vb_ref[sl, :]
        g_row = gr_ref[0, 0, :, sl] + gbr_ref[0]
        cum_col, cum_row = cc_ref[sl, :], cr_ref[:, sl]
        end = L - 1 if d == 0 else 0
        outs, new_m = [], []
        for h2 in range(2):
            m_st = m_state[h2]
            ci, cf = 2 * (2 * d) + h2, 2 * (2 * d + 1) + h2
            a_rep = jnp.broadcast_to(cum_col[:, cf:cf + 1], (L, LANES))
            i_row, a_row = g_row[ci:ci + 1, :], cum_row[cf:cf + 1, :]
            b_tot = a_row[:, end:end + 1]
            d_log = jnp.where(causal[d], a_rep + (i_row - a_row), NEG_INF)
            m_intra = jnp.broadcast_to(jnp.max(d_log, axis=-1, keepdims=True), (L, LANES))
            qm = jnp.where(head_lanes[h2], q, jnp.zeros_like(q))
            s = jnp.dot(qm, kt, preferred_element_type=F32) * jnp.exp(d_log - m_intra)
            v1 = jnp.where(head_lanes[h2], v, ones)
            intra = jnp.dot(s.astype(BF16), v1, preferred_element_type=F32)
            inter = jnp.dot(qm, st_ref[h2].astype(BF16), preferred_element_type=F32)
            inter_log = a_rep + m_st
            m_q = jnp.maximum(m_intra, inter_log)
            num_den = jnp.exp(inter_log - m_q) * inter + jnp.exp(m_intra - m_q) * intra
            den = pltpu.roll(num_den, HEAD_DIM, axis=1)
            outs.append(num_den / jnp.maximum(jnp.abs(den), jnp.exp(-m_q)))
            w_row = b_tot - a_row + i_row
            m_loc = jnp.max(w_row, axis=-1, keepdims=True)
            kts = jnp.where(head_rows[h2], kt.astype(F32) * jnp.exp(w_row - m_loc), 0.0).astype(BF16)
            loc = jnp.dot(kts, v1, preferred_element_type=F32)
            m_new = jnp.maximum(b_tot + m_st, m_loc)
            st_ref[h2] = jnp.exp(b_tot + m_st - m_new) * st_ref[h2] + jnp.exp(m_loc - m_new) * loc
            new_m.append(m_new)
        return sl, jnp.where(lo, outs[0], outs[1]), tuple(new_m)

    zero_m = (jnp.zeros((1, 1), F32), jnp.zeros((1, 1), F32))

    st_ref[...] = jnp.zeros_like(st_ref)

    def fwd(c, m_state):
        sl, y, m_state = chunk(0, c, m_state)
        acc_ref[sl, :] = y
        return m_state
    lax.fori_loop(0, nc, fwd, zero_m, unroll=MLSTM_UNROLL)

    st_ref[...] = jnp.zeros_like(st_ref)

    def bwd(i, m_state):
        c = _reverse_chunk(i, nc, t_ctx // L)
        sl, y, m_state = chunk(1, c, m_state)
        y = _head_rms(y + acc_ref[sl, :], mn_ref[...], lo)
        o_ref[0, sl, :] = (y * jax.nn.sigmoid(og_ref[0, sl, :])).astype(o_ref.dtype)
        return m_state
    lax.fori_loop(0, nc, bwd, zero_m, unroll=MLSTM_UNROLL)


def _mlstm(p_main, p_gate, conv_w, conv_b, gate_b, m_w, t_ctx, n_heads):
    b_sz, s_len, _ = p_main.shape
    n_hp = n_heads // 2
    L = MLSTM_CHUNK_LEN
    assert t_ctx % L == 0 and s_len % L == 0
    n_g = 2 * N_GATE_TYPES
    g_rows = jnp.transpose(p_gate.reshape(b_sz, s_len, n_hp, LANES)[..., :n_g], (0, 2, 3, 1))
    gb = jnp.transpose(gate_b.reshape(N_GATE_TYPES, n_hp, 2), (1, 0, 2)).reshape(n_hp, n_g)
    gb_col = jnp.zeros((n_hp, 1, LANES), F32).at[:, 0, :n_g].set(gb)
    gb_row = gb.reshape(n_hp, n_g, 1)
    vec = lambda col: pl.BlockSpec((1, LANES), lambda b, h: (0, col + h))
    return pl.pallas_call(
        functools.partial(_mlstm_pair_kernel, t_ctx=t_ctx),
        grid=(b_sz, n_hp),
        in_specs=[_seq_spec(s_len, 0, 2), _seq_spec(s_len, n_hp, 2), _seq_spec(s_len, 2 * n_hp, 2),
                  _seq_spec(s_len, 3 * n_hp), _seq_spec(s_len, 0),
                  pl.BlockSpec((1, 1, n_g, s_len), lambda b, h: (b, h, 0, 0)),
                  pl.BlockSpec((3, LANES), lambda b, h: (0, h)),
                  pl.BlockSpec((3, LANES), lambda b, h: (0, n_hp + h)),
                  vec(0), vec(n_hp),
                  pl.BlockSpec((1, 1, LANES), lambda b, h: (h, 0, 0)),
                  pl.BlockSpec((1, n_g, 1), lambda b, h: (h, 0, 0)),
                  pl.BlockSpec((1, LANES), lambda b, h: (0, 0))],
        out_specs=pl.BlockSpec((1, s_len, LANES), lambda b, h: (b, 0, h)),
        out_shape=jax.ShapeDtypeStruct((b_sz, s_len, n_heads * HEAD_DIM), BF16),
        scratch_shapes=[pltpu.VMEM((s_len, LANES), BF16), pltpu.VMEM((LANES, s_len), BF16),
                        pltpu.VMEM((s_len, LANES), BF16), pltpu.VMEM((s_len, LANES), F32),
                        pltpu.VMEM((2, LANES, LANES), F32),
                        pltpu.VMEM((s_len, LANES), F32), pltpu.VMEM((n_g, s_len), F32)],
        compiler_params=_mixer_cparams(),
        name="mlstm",
    )(p_main, p_main, p_main, p_main, p_gate, g_rows, conv_w, conv_w, conv_b.reshape(1, -1),
      conv_b.reshape(1, -1), gb_col, gb_row, jnp.concatenate([m_w, m_w]).reshape(1, LANES))


def kernel(x, c, ctx, c_ctx, w_mod, b_mod, norm_mix, norm_ffn, w_in, w_out, mlstm_conv_w, mlstm_conv_b,
           mlstm_gate_b, mlstm_norm, na_q_norm, na_k_norm, na_rpb, ret_norm, router_w, router_b,
           expert_w_up, expert_b_up, expert_w_down, expert_b_down):
    b_sz, t_len, d = x.shape
    t_ctx = ctx.shape[1]
    depth = w_in.shape[0]
    n_e = router_w.shape[2]
    d_mix = w_out.shape[1]
    h_m = d_mix // (4 * HEAD_DIM)
    h_na = d_mix // (2 * HEAD_DIM)
    h_r = d_mix // (4 * HEAD_DIM)
    d_m, d_na = h_m * HEAD_DIM, h_na * HEAD_DIM
    n_gate = N_GATE_TYPES * h_m
    assert b_sz + 1 <= MOD_ROWS and n_e <= LANES and n_gate <= LANES
    s_len = t_ctx + t_len
    assert s_len % ROW_TILE == 0 and s_len % COMBINE_ROWS == 0

    s = jnp.concatenate([ctx, x], axis=1)
    cc = jnp.zeros((MOD_ROWS, d), F32).at[:b_sz].set(c).at[b_sz].set(c_ctx)
    mods = _modulation(cc, w_mod, b_mod).reshape(depth, MOD_ROWS, 1, 6 * d)

    g0 = 4 * d_m
    col_na = g0
    col_ret = col_na + 3 * d_na
    n_hp = h_m // 2
    gate_src = np.array([[g0 + t * h_m + 2 * hp + h2 for t in range(N_GATE_TYPES) for h2 in range(2)]
                         for hp in range(n_hp)])
    rope_tabs = _rope_tables(t_ctx, t_len)
    ret_tabs = _ret_tables(h_r)
    na_tabs = _na_bias_tables(na_rpb)
    f2 = expert_w_up.shape[3]
    w_up_all = expert_w_up.reshape(depth * n_e, d, f2)
    w_down_all = expert_w_down.reshape(depth * n_e, f2 // 2, d)
    b_up_all = _regroup_bias(expert_b_up).reshape(depth * n_e, 1, f2)
    b_down_all = expert_b_down.reshape(depth * n_e, 1, d)

    experts_out = None
    for l in range(depth):
        w_main = jnp.concatenate([w_in[l, :, :g0], w_in[l, :, g0 + n_gate:]], axis=1).astype(BF16)
        w_gate = jnp.zeros((d, n_hp, LANES), BF16).at[:, :, :gate_src.shape[1]].set(
            w_in[l][:, gate_src].astype(BF16)).reshape(d, n_hp * LANES)
        if experts_out is None:
            p_main, p_gate = _proj_in(s, mods[l], norm_mix[l], w_main, w_gate, t_ctx)
        else:
            s, p_main, p_gate = _proj_in(s, mods[l], norm_mix[l], w_main, w_gate, t_ctx,
                                         combine=(*experts_out, mods[l - 1]))

        mix = [_mlstm(p_main, p_gate, mlstm_conv_w[l], mlstm_conv_b[l], mlstm_gate_b[l], mlstm_norm[l],
                      t_ctx, h_m),
               _na_attention(p_main, na_q_norm[l], na_k_norm[l], na_tabs[l], t_ctx, col_na, h_na),
               _retention(p_main, rope_tabs, ret_tabs, ret_norm[l], t_ctx, col_ret, h_r)]

        rw = jnp.zeros((d, LANES), F32).at[:, :n_e].set(router_w[l])
        rb = jnp.zeros((1, LANES), F32).at[0, :n_e].set(router_b[l])
        s, tok, logits = _proj_out(mix, s, mods[l], norm_ffn[l], w_out[l].astype(BF16), rw, rb, t_ctx)

        experts_out = _moe(tok, logits[..., :n_e], n_e, l, w_up_all, w_down_all, b_up_all, b_down_all)
    s = _combine(*experts_out, s, mods[depth - 1], t_ctx)
    return s[:, t_ctx:]
```

```python
import functools

import jax
import jax.numpy as jnp
import numpy as np
from jax import lax
from jax.experimental import pallas as pl
from jax.experimental.pallas import tpu as pltpu

F32 = jnp.float32
BF16 = jnp.bfloat16

GRID_W = 64
HEAD_DIM = 64
NA_ROWS = 8
NA_COLS = 16
ROPE_BASE = 10000.0
TOP_K = 4
SWIGLU_ALPHA = 1.702
SWIGLU_LIMIT = 7.0
NORM_EPS = 1e-6
NEG_INF = -1e30

LANES = 128
SUBLANES = 8
VMEM_LIMIT = 48 * 1024 * 1024
MOD_ROWS = 8
MOD_COL_TILES = 4
ROW_TILE = 768
COMBINE_ROWS = 256
MOE_TILE = 768
REGROUP_ROWS = 512
MOE_VMEM_LIMIT = 56 * 1024 * 1024


def _cparams(sem):
    return pltpu.CompilerParams(dimension_semantics=sem, vmem_limit_bytes=VMEM_LIMIT)


def _mod_kernel(cc_ref, w_ref, b_ref, o_ref):
    cc = cc_ref[...]
    a = cc * jax.nn.sigmoid(cc)
    o_ref[0] = jnp.dot(a, w_ref[0], precision=lax.Precision.HIGHEST,
                       preferred_element_type=F32) + b_ref[0]


def _modulation(cc, w_mod, b_mod):
    n_l, d, d6 = w_mod.shape
    tn = d6 // MOD_COL_TILES
    return pl.pallas_call(
        _mod_kernel,
        grid=(n_l, MOD_COL_TILES),
        in_specs=[pl.BlockSpec((MOD_ROWS, d), lambda l, j: (0, 0)),
                  pl.BlockSpec((1, d, tn), lambda l, j: (l, 0, j)),
                  pl.BlockSpec((1, 1, tn), lambda l, j: (l, 0, j))],
        out_specs=pl.BlockSpec((1, MOD_ROWS, tn), lambda l, j: (l, 0, j)),
        out_shape=jax.ShapeDtypeStruct((n_l, MOD_ROWS, d6), F32),
        compiler_params=_cparams(("arbitrary", "arbitrary")),
        name="adaln_modulation",
    )(cc, w_mod, b_mod.reshape(n_l, 1, d6))


def _pick_mod(mb_ref, mc_ref, k, d, is_ctx):
    vb = mb_ref[0, :, k * d:(k + 1) * d]
    vc = mc_ref[0, :, k * d:(k + 1) * d]
    return jnp.where(is_ctx, vc, vb)


def _ctx_rows(tm, t_ctx):
    return lax.broadcasted_iota(jnp.int32, (tm, 1), 0) + pl.program_id(1) * tm < t_ctx


def _rms(x, g):
    return x * lax.rsqrt(jnp.mean(x * x, axis=-1, keepdims=True) + NORM_EPS) * g


def _pack_bf16_pairs(x):
    m = x.shape[1] // 2
    hi = lax.bitcast_convert_type(x[:, :m].astype(BF16).astype(F32), jnp.uint32)
    lo = lax.bitcast_convert_type(x[:, m:].astype(BF16).astype(F32), jnp.uint32)
    return hi | (lo >> 16)


def _unpack_bf16_pairs(p):
    hi = lax.bitcast_convert_type(p & jnp.uint32(0xFFFF0000), F32)
    lo = lax.bitcast_convert_type(p << 16, F32)
    return jnp.concatenate([hi, lo], axis=1)


PROJ_IN_ROWS = 384
PROJ_IN_COLS = 512


def _project(s, is_ctx, mb_ref, mc_ref, g_ref, w_ref, wg_ref, pm_ref, pg_ref):
    d = s.shape[1]
    sh = _pick_mod(mb_ref, mc_ref, 0, d, is_ctx)
    sc = _pick_mod(mb_ref, mc_ref, 1, d, is_ctx)
    xn = (_rms(s, g_ref[...]) * (1.0 + sc) + sh).astype(BF16)
    pg_ref[0] = jnp.dot(xn, wg_ref[...], preferred_element_type=F32)
    for j in range(w_ref.shape[1] // PROJ_IN_COLS):
        cols = slice(j * PROJ_IN_COLS, (j + 1) * PROJ_IN_COLS)
        pm_ref[0, :, cols] = jnp.dot(xn, w_ref[:, cols], preferred_element_type=F32)


def _proj_in_kernel(s_ref, mb_ref, mc_ref, g_ref, w_ref, wg_ref, pm_ref, pg_ref, *, t_ctx):
    _project(s_ref[0], _ctx_rows(s_ref.shape[1], t_ctx), mb_ref, mc_ref, g_ref, w_ref, wg_ref, pm_ref, pg_ref)


def _combine_rows(y_ref, gt_ref, s_ref, mb_ref, mc_ref, is_ctx):
    g2 = _pick_mod(mb_ref, mc_ref, 5, s_ref.shape[2], is_ctx)
    gt = gt_ref[0]
    y = _unpack_bf16_pairs(y_ref[0, 0]) * gt[:, 0:1]
    for k in range(1, y_ref.shape[0]):
        y = y + _unpack_bf16_pairs(y_ref[k, 0]) * gt[:, k:k + 1]
    return s_ref[0] + g2 * y


def _combine_proj_in_kernel(y_ref, gt_ref, s_ref, pmb_ref, pmc_ref, mb_ref, mc_ref, g_ref, w_ref, wg_ref,
                            so_ref, pm_ref, pg_ref, *, t_ctx):
    is_ctx = _ctx_rows(s_ref.shape[1], t_ctx)
    s_new = _combine_rows(y_ref, gt_ref, s_ref, pmb_ref, pmc_ref, is_ctx)
    so_ref[0] = s_new
    _project(s_new, is_ctx, mb_ref, mc_ref, g_ref, w_ref, wg_ref, pm_ref, pg_ref)


def _proj_in(s, mod_l, g, w_main, w_gate, t_ctx, combine=None):
    b_sz, s_len, d = s.shape
    n_main = w_main.shape[1]
    n_gate = w_gate.shape[1]
    tm = PROJ_IN_ROWS
    assert s_len % tm == 0 and n_main % PROJ_IN_COLS == 0
    row_spec = pl.BlockSpec((1, tm, d), lambda b, i: (b, i, 0))
    mod_specs = [pl.BlockSpec((1, 1, 6 * d), lambda b, i: (b, 0, 0)),
                 pl.BlockSpec((1, 1, 6 * d), lambda b, i: (b_sz, 0, 0))]
    w_specs = [pl.BlockSpec((1, d), lambda b, i: (0, 0)),
               pl.BlockSpec((d, n_main), lambda b, i: (0, 0)),
               pl.BlockSpec((d, n_gate), lambda b, i: (0, 0))]
    out_specs = [pl.BlockSpec((1, tm, n_main), lambda b, i: (b, i, 0)),
                 pl.BlockSpec((1, tm, n_gate), lambda b, i: (b, i, 0))]
    out_shape = [jax.ShapeDtypeStruct((b_sz, s_len, n_main), F32),
                 jax.ShapeDtypeStruct((b_sz, s_len, n_gate), F32)]
    if combine is None:
        return pl.pallas_call(
            functools.partial(_proj_in_kernel, t_ctx=t_ctx),
            grid=(b_sz, s_len // tm),
            in_specs=[row_spec] + mod_specs + w_specs,
            out_specs=out_specs, out_shape=out_shape,
            compiler_params=_cparams(("arbitrary", "arbitrary")),
            name="proj_in",
        )(s, mod_l, mod_l, g.reshape(1, d), w_main, w_gate)
    y_as, gates, mod_prev = combine
    n_k = y_as.shape[0]
    return pl.pallas_call(
        functools.partial(_combine_proj_in_kernel, t_ctx=t_ctx),
        grid=(b_sz, s_len // tm),
        in_specs=[pl.BlockSpec((n_k, 1, tm, d // 2), lambda b, i: (0, b, i, 0)),
                  pl.BlockSpec((1, tm, n_k), lambda b, i: (b, i, 0)),
                  row_spec] + mod_specs + mod_specs + w_specs,
        out_specs=[row_spec] + out_specs,
        out_shape=[jax.ShapeDtypeStruct((b_sz, s_len, d), F32)] + out_shape,
        compiler_params=_cparams(("arbitrary", "arbitrary")),
        name="combine_proj_in",
    )(y_as, gates, s, mod_prev, mod_prev, mod_l, mod_l, g.reshape(1, d), w_main, w_gate)


def _proj_out_kernel(ma_ref, mb2_ref, mc2_ref, s_ref, mb_ref, mc_ref, g_ref, w_ref, rw_ref, rb_ref,
                     so_ref, tok_ref, lg_ref, *, t_ctx):
    d = s_ref.shape[2]
    is_ctx = _ctx_rows(s_ref.shape[1], t_ctx)
    g1 = _pick_mod(mb_ref, mc_ref, 2, d, is_ctx)
    ka, kb = ma_ref.shape[2], ma_ref.shape[2] + mb2_ref.shape[2]
    y = (jnp.dot(ma_ref[0], w_ref[0:ka, :], preferred_element_type=F32)
         + jnp.dot(mb2_ref[0], w_ref[ka:kb, :], preferred_element_type=F32)
         + jnp.dot(mc2_ref[0], w_ref[kb:, :], preferred_element_type=F32))
    s_new = s_ref[0] + g1 * y
    so_ref[0] = s_new
    sh = _pick_mod(mb_ref, mc_ref, 3, d, is_ctx)
    sc = _pick_mod(mb_ref, mc_ref, 4, d, is_ctx)
    t = _rms(s_new, g_ref[...]) * (1.0 + sc) + sh
    tok_ref[0] = t
    w = rw_ref[...]
    t_hi, w_hi = t.astype(BF16), w.astype(BF16)
    t_lo = (t - t_hi.astype(F32)).astype(BF16)
    w_lo = (w - w_hi.astype(F32)).astype(BF16)
    lg_ref[0] = (jnp.dot(t_hi, w_hi, preferred_element_type=F32) + jnp.dot(t_hi, w_lo, preferred_element_type=F32)
                 + jnp.dot(t_lo, w_hi, preferred_element_type=F32) + rb_ref[...])


def _proj_out(mix_parts, s, mod_l, g, w_out, rw, rb, t_ctx):
    b_sz, s_len, d = s.shape
    tm = ROW_TILE
    row_spec = pl.BlockSpec((1, tm, d), lambda b, i: (b, i, 0))
    part_specs = [pl.BlockSpec((1, tm, m.shape[2]), lambda b, i: (b, i, 0)) for m in mix_parts]
    return pl.pallas_call(
        functools.partial(_proj_out_kernel, t_ctx=t_ctx),
        grid=(b_sz, s_len // tm),
        in_specs=part_specs + [row_spec,
                               pl.BlockSpec((1, 1, 6 * d), lambda b, i: (b, 0, 0)),
                               pl.BlockSpec((1, 1, 6 * d), lambda b, i: (b_sz, 0, 0)),
                               pl.BlockSpec((1, d), lambda b, i: (0, 0)),
                               pl.BlockSpec((d, d), lambda b, i: (0, 0)),
                               pl.BlockSpec((d, LANES), lambda b, i: (0, 0)),
                               pl.BlockSpec((1, LANES), lambda b, i: (0, 0))],
        out_specs=[row_spec, row_spec, pl.BlockSpec((1, tm, LANES), lambda b, i: (b, i, 0))],
        out_shape=[jax.ShapeDtypeStruct((b_sz, s_len, d), F32),
                   jax.ShapeDtypeStruct((b_sz, s_len, d), F32),
                   jax.ShapeDtypeStruct((b_sz, s_len, LANES), F32)],
        compiler_params=_cparams(("arbitrary", "arbitrary")),
        name="proj_out_router",
    )(*mix_parts, s, mod_l, mod_l, g.reshape(1, d), w_out, rw, rb)


PAIR = 2 * LANES


def _regroup_perm():
    dst = np.arange(PAIR)
    src = np.where(dst < LANES, 2 * dst, 2 * (dst - LANES) + 1)
    return jnp.asarray(np.arange(PAIR)[:, None] == src[None, :], BF16)


def _regroup_bias(b_up):
    lead = b_up.shape[:-1]
    b = b_up.reshape(*lead, -1, LANES, 2)
    return jnp.swapaxes(b, -1, -2).reshape(*lead, -1)


def _moe_kernel(be_ref, nu_ref, x_ref, wu_ref, wd_ref, bu_ref, bd_ref, p_ref, y_ref, wus_ref, wds_ref):
    i = pl.program_id(0)
    d, f2 = wus_ref.shape
    rows = min(REGROUP_ROWS, d)

    @pl.when(i < nu_ref[0])
    def _():
        @pl.when((i == 0) | (be_ref[i] != be_ref[jnp.maximum(i - 1, 0)]))
        def _():
            for r in range(d // rows):
                for j in range(f2 // PAIR):
                    w = wu_ref[0, r * rows:(r + 1) * rows, j * PAIR:(j + 1) * PAIR].astype(BF16)
                    wus_ref[r * rows:(r + 1) * rows, j * PAIR:(j + 1) * PAIR] = jnp.dot(
                        w, p_ref[...], preferred_element_type=F32).astype(BF16)
            wds_ref[...] = wd_ref[0].astype(BF16)

        up = jnp.dot(x_ref[...].astype(BF16), wus_ref[...], preferred_element_type=F32) + bu_ref[0]
        acts = []
        for j in range(f2 // PAIR):
            glu = jnp.minimum(up[:, j * PAIR:j * PAIR + LANES], SWIGLU_LIMIT)
            lin = jnp.clip(up[:, j * PAIR + LANES:(j + 1) * PAIR], -SWIGLU_LIMIT, SWIGLU_LIMIT)
            acts.append((glu * jax.nn.sigmoid(SWIGLU_ALPHA * glu) * (lin + 1.0)).astype(BF16))
        act = jnp.concatenate(acts, axis=1)
        y_ref[...] = _pack_bf16_pairs(jnp.dot(act, wds_ref[...], preferred_element_type=F32) + bd_ref[0])

    @pl.when(i >= nu_ref[0])
    def _():
        y_ref[...] = jnp.zeros_like(y_ref)


def _moe_blocks(blk_e, n_used, x_sorted, w_up, w_down, b_up, b_down):
    n_rows, d = x_sorted.shape
    _, _, f2 = w_up.shape
    assert f2 % PAIR == 0 and d % min(REGROUP_ROWS, d) == 0
    tm = MOE_TILE
    wmap = lambda i, be, nu: (be[i], 0, 0)
    return pl.pallas_call(
        _moe_kernel,
        grid_spec=pltpu.PrefetchScalarGridSpec(
            num_scalar_prefetch=2,
            grid=(n_rows // tm,),
            in_specs=[pl.BlockSpec((tm, d), lambda i, be, nu: (i, 0)),
                      pl.BlockSpec((1, d, f2), wmap),
                      pl.BlockSpec((1, f2 // 2, d), wmap),
                      pl.BlockSpec((1, 1, f2), wmap),
                      pl.BlockSpec((1, 1, d), wmap),
                      pl.BlockSpec((PAIR, PAIR), lambda i, be, nu: (0, 0))],
            out_specs=pl.BlockSpec((tm, d // 2), lambda i, be, nu: (i, 0)),
            scratch_shapes=[pltpu.VMEM((d, f2), BF16), pltpu.VMEM((f2 // 2, d), BF16)]),
        out_shape=jax.ShapeDtypeStruct((n_rows, d // 2), jnp.uint32),
        compiler_params=pltpu.CompilerParams(dimension_semantics=("arbitrary",),
                                             vmem_limit_bytes=MOE_VMEM_LIMIT),
        name="moe_expert_blocks",
    )(blk_e, n_used, x_sorted, w_up, w_down, b_up, b_down, _regroup_perm())


def _latent_combine_kernel(y_ref, gt_ref, s_ref, mb_ref, mc_ref, o_ref):
    no_ctx = lax.broadcasted_iota(jnp.int32, (s_ref.shape[1], 1), 0) < 0
    o_ref[0] = _combine_rows(y_ref, gt_ref, s_ref, mb_ref, mc_ref, no_ctx)


def _latent_combine(y_as, gates, s, mod_l, t_ctx):
    b_sz, s_len, d = s.shape
    n_k = y_as.shape[0]
    tm = COMBINE_ROWS
    assert t_ctx % tm == 0 and s_len % tm == 0
    skip = t_ctx // tm
    return pl.pallas_call(
        _latent_combine_kernel,
        grid=(b_sz, s_len // tm - skip),
        in_specs=[pl.BlockSpec((n_k, 1, tm, d // 2), lambda b, i: (0, b, i + skip, 0)),
                  pl.BlockSpec((1, tm, n_k), lambda b, i: (b, i + skip, 0)),
                  pl.BlockSpec((1, tm, d), lambda b, i: (b, i + skip, 0)),
                  pl.BlockSpec((1, 1, 6 * d), lambda b, i: (b, 0, 0)),
                  pl.BlockSpec((1, 1, 6 * d), lambda b, i: (b_sz, 0, 0))],
        out_specs=pl.BlockSpec((1, tm, d), lambda b, i: (b, i, 0)),
        out_shape=jax.ShapeDtypeStruct((b_sz, s_len - t_ctx, d), F32),
        compiler_params=_cparams(("arbitrary", "arbitrary")),
        name="moe_combine",
    )(y_as, gates, s, mod_l, mod_l)


def _moe(tok, logits, n_e, layer, w_up, w_down, b_up, b_down):
    b_sz, s_len, d = tok.shape
    n_tok = b_sz * s_len
    tm = MOE_TILE
    top_v, top_e = lax.top_k(logits.reshape(n_tok, n_e), TOP_K)
    gates = jax.nn.softmax(top_v, axis=-1)
    n_as = n_tok * TOP_K
    onehot = jnp.sum((top_e[:, :, None] == jnp.arange(n_e)[None, None, :]).astype(jnp.int32), axis=1)
    csum = jnp.cumsum(onehot, axis=0)
    counts = csum[-1]
    padded = (counts + tm - 1) // tm * tm
    end_pad = jnp.cumsum(padded)
    start_pad = end_pad - padded
    start = jnp.cumsum(counts) - counts
    dest = jnp.take_along_axis(csum - onehot + start_pad[None, :], top_e, axis=1).astype(jnp.int32)
    n_blocks = -(-n_as // tm) + n_e
    blk_first = jnp.arange(n_blocks) * tm
    blk_e = jnp.minimum(jnp.sum(blk_first[:, None] >= end_pad[None, :], axis=1), n_e - 1).astype(jnp.int32)
    n_used = (end_pad[-1:] // tm).astype(jnp.int32)
    tok_sorted = (jnp.argsort(top_e.reshape(n_as)) // TOP_K).astype(jnp.int32)
    j = (blk_first - start_pad[blk_e])[:, None] + jnp.arange(tm)[None, :]
    src = jnp.clip(start[blk_e][:, None] + j, 0, n_as - 1)
    row_tok = jnp.where(j < counts[blk_e][:, None], tok_sorted[src], 0).reshape(n_blocks * tm)
    x_sorted = tok.reshape(n_tok, d)[row_tok]
    y = _moe_blocks(blk_e + layer * n_e, n_used, x_sorted, w_up, w_down, b_up, b_down)
    y_as = y[dest.T].reshape(TOP_K, b_sz, s_len, d // 2)
    return y_as, gates.reshape(b_sz, s_len, TOP_K)


MIXER_VMEM_LIMIT = 58 * 1024 * 1024


def _seq_spec(s_len, col, buffers=1):
    return pl.BlockSpec((1, s_len, LANES), lambda b, h: (b, 0, col + h), pipeline_mode=pl.Buffered(buffers))


def _mixer_cparams():
    return pltpu.CompilerParams(dimension_semantics=("arbitrary", "arbitrary"), vmem_limit_bytes=MIXER_VMEM_LIMIT)


def _head_rms(x, w, lo):
    xx = x * x
    s0 = jnp.sum(jnp.where(lo, xx, 0.0), axis=-1, keepdims=True)
    s1 = jnp.sum(jnp.where(lo, 0.0, xx), axis=-1, keepdims=True)
    inv = lax.rsqrt(jnp.where(lo, s0, s1) * (1.0 / HEAD_DIM) + NORM_EPS)
    return x * inv * w


def _reverse_chunk(i, nc, n_ctx):
    return jnp.where(i < n_ctx, n_ctx - 1 - i, nc - 1 - i + n_ctx)


NA_STEP = 256
NA_KEYS = NA_ROWS * GRID_W
NA_GROUPS_PER_ITER = 4
NA_PREP_UNROLL = 3
NA_SHIFT_UNROLL = 5


def _na_bias_tables(rpb):
    n_l, n_h, n_dr, n_dc = rpb.shape
    col = np.arange(GRID_W)
    dc = np.clip(col[None, :] - col[:, None] + NA_COLS - 1, 0, n_dc - 1)
    onehot = (dc.reshape(1, -1) == np.arange(n_dc)[:, None]).astype(np.float32)
    toe = jnp.dot(rpb.reshape(-1, n_dc), jnp.asarray(onehot), precision=lax.Precision.HIGHEST)
    toe = toe.reshape(n_l, n_h, n_dr, GRID_W, GRID_W)
    col_start = np.clip(col - NA_COLS // 2, 0, GRID_W - NA_COLS)
    col_in = (col[None, :] >= col_start[:, None]) & (col[None, :] < col_start[:, None] + NA_COLS)
    toe = jnp.where(jnp.asarray(col_in), toe, NEG_INF)
    tabs = []
    for dr0 in range(NA_ROWS):
        t = toe[:, :, dr0:dr0 + NA_ROWS].reshape(n_l, n_h // 2, 2, NA_ROWS, GRID_W, GRID_W)
        t = jnp.transpose(t, (0, 1, 2, 4, 3, 5))
        tabs.append(t.reshape(n_l, n_h // 2, 2 * GRID_W, NA_KEYS))
    return jnp.stack(tabs, axis=1)


def _na_seq_kernel(q_ref, k_ref, v_ref, qw_ref, kw_ref, bias_ref, o_ref, kt_ref, vb_ref, s_ref, kn_ref, *,
                   t_ctx, rows):
    s_len = k_ref.shape[1]
    lo = lax.broadcasted_iota(jnp.int32, (1, LANES), 1) < HEAD_DIM
    rows_per_step = NA_STEP // GRID_W

    def prep(c, carry):
        t0 = pl.multiple_of(c * NA_STEP, NA_STEP)
        kn = _head_rms(k_ref[0, pl.ds(t0, NA_STEP), :], kw_ref[...], lo).astype(BF16)
        kn_ref[pl.ds(t0, NA_STEP), :] = kn
        kt_ref[0, :, pl.ds(t0, NA_STEP)] = kn.T
        vb_ref[pl.ds(t0, NA_STEP), :] = jnp.concatenate(
            [v_ref[0, pl.ds(t0, NA_STEP), :].astype(BF16), jnp.ones((NA_STEP, LANES), BF16)], axis=1)
        return carry
    lax.fori_loop(0, s_len // NA_STEP, prep, 0, unroll=NA_PREP_UNROLL)

    def prep_shifted(c, carry):
        t0 = pl.multiple_of(c * LANES, LANES)
        kt_ref[1, :, pl.ds(t0, LANES)] = kn_ref[pl.ds(pl.multiple_of(t0 + GRID_W, GRID_W), LANES), :].T
        return carry
    lax.fori_loop(0, (s_len - GRID_W) // LANES, prep_shifted, 0, unroll=NA_SHIFT_UNROLL)

    kc_t = kt_ref[0, :, 0:t_ctx]
    vc = vb_ref[0:t_ctx, :]

    def queries(tok0):
        qn = _head_rms(q_ref[0, pl.ds(tok0, NA_STEP), :], qw_ref[...], lo) * (HEAD_DIM ** -0.5)
        q0 = jnp.where(lo, qn, 0.0).astype(BF16)
        q1 = jnp.where(lo, 0.0, qn).astype(BF16)
        return [jnp.concatenate([q0[g * GRID_W:(g + 1) * GRID_W], q1[g * GRID_W:(g + 1) * GRID_W]], axis=0)
                for g in range(rows_per_step)]

    def finish(tok0, g, o2):
        o2 = o2[:, :LANES] * (1.0 / o2[:, LANES:])
        o = jnp.where(lo, o2[:GRID_W], o2[GRID_W:])
        o_ref[0, pl.ds(tok0 + g * GRID_W, GRID_W), :] = o.astype(o_ref.dtype)

    for g, q2 in enumerate(queries(0)):
        s_c = jnp.dot(q2, kc_t, preferred_element_type=F32)
        p_c = jnp.exp(s_c - jnp.max(s_c, axis=-1, keepdims=True))
        finish(0, g, jnp.dot(p_c.astype(BF16), vc, preferred_element_type=F32))

    def row_group(rg, slot):
        tok0 = pl.multiple_of(t_ctx + rg * NA_STEP, NA_STEP)
        t0s = []
        s_slot = s_ref.at[slot]
        for g, q2 in enumerate(queries(tok0)):
            r = rg * rows_per_step + g
            row_start = jnp.clip(r - NA_ROWS // 2, 0, rows - NA_ROWS)
            dr0 = row_start - r + NA_ROWS - 1
            t0 = pl.multiple_of(t_ctx + row_start * GRID_W, GRID_W)
            odd = (t0 // GRID_W) % (LANES // GRID_W)
            kw_t = kt_ref[odd, :, pl.ds(pl.multiple_of(t0 - odd * GRID_W, LANES), NA_KEYS)]
            s_slot[g, :, :NA_KEYS] = jnp.dot(q2, kw_t, preferred_element_type=F32) + bias_ref[dr0, 0]
            s_slot[g, :, NA_KEYS:] = jnp.dot(q2, kc_t, preferred_element_type=F32)
            t0s.append(t0)
        for g in range(rows_per_step):
            s = s_slot[g]
            p = jnp.exp(s - jnp.max(s, axis=-1, keepdims=True)).astype(BF16)
            finish(tok0, g, jnp.dot(p[:, :NA_KEYS], vb_ref[pl.ds(t0s[g], NA_KEYS), :], preferred_element_type=F32)
                   + jnp.dot(p[:, NA_KEYS:], vc, preferred_element_type=F32))

    def row_groups(i, carry):
        for slot in range(NA_GROUPS_PER_ITER):
            row_group(i * NA_GROUPS_PER_ITER + slot, slot)
        return carry
    lax.fori_loop(0, rows // rows_per_step // NA_GROUPS_PER_ITER, row_groups, 0)


def _na_attention(p_main, q_w, k_w, bias_tab, t_ctx, col_q, n_heads):
    b_sz, s_len, _ = p_main.shape
    n_hg = n_heads // 2
    rows = (s_len - t_ctx) // GRID_W
    rows_per_step = NA_STEP // GRID_W
    assert t_ctx == NA_STEP and s_len % NA_STEP == 0 and rows >= NA_ROWS
    assert rows % (rows_per_step * NA_GROUPS_PER_ITER) == 0
    cq = col_q // LANES
    seq = lambda col: pl.BlockSpec((1, s_len, LANES), lambda b, h: (b, 0, col + h))
    w2 = lambda w: jnp.concatenate([w, w]).reshape(1, LANES)
    return pl.pallas_call(
        functools.partial(_na_seq_kernel, t_ctx=t_ctx, rows=rows),
        grid=(b_sz, n_hg),
        in_specs=[seq(cq), seq(cq + n_hg), _seq_spec(s_len, cq + 2 * n_hg),
                  pl.BlockSpec((1, LANES), lambda b, h: (0, 0)),
                  pl.BlockSpec((1, LANES), lambda b, h: (0, 0)),
                  pl.BlockSpec((NA_ROWS, 1, 2 * GRID_W, NA_KEYS), lambda b, h: (0, h, 0, 0))],
        out_specs=pl.BlockSpec((1, s_len, LANES), lambda b, h: (b, 0, h)),
        out_shape=jax.ShapeDtypeStruct((b_sz, s_len, n_heads * HEAD_DIM), BF16),
        scratch_shapes=[pltpu.VMEM((2, LANES, s_len), BF16), pltpu.VMEM((s_len, 2 * LANES), BF16),
                        pltpu.VMEM((NA_GROUPS_PER_ITER, rows_per_step, 2 * GRID_W, NA_KEYS + t_ctx), F32),
                        pltpu.VMEM((s_len, LANES), BF16)],
        compiler_params=_cparams(("arbitrary", "arbitrary")),
        name="na_attention",
    )(p_main, p_main, p_main, w2(q_w), w2(k_w), bias_tab)


RET_CHUNK_LEN = 128
RET_UNROLL = 6


def _rope_tables(t_ctx, t_len):
    pos = jnp.arange(t_len)
    row = (pos // GRID_W).astype(F32)
    col = (pos % GRID_W).astype(F32)
    n = HEAD_DIM // 4
    inv = ROPE_BASE ** (-jnp.arange(n, dtype=F32) / n)
    ar = row[:, None] * inv
    ac = col[:, None] * inv
    cos = jnp.concatenate([jnp.cos(ar), jnp.cos(ar), jnp.cos(ac), jnp.cos(ac)], axis=-1)
    sin = jnp.concatenate([-jnp.sin(ar), jnp.sin(ar), -jnp.sin(ac), jnp.sin(ac)], axis=-1)
    cos = jnp.concatenate([jnp.ones((t_ctx, HEAD_DIM), F32), cos], axis=0)
    sin = jnp.concatenate([jnp.zeros((t_ctx, HEAD_DIM), F32), sin], axis=0)
    return jnp.tile(cos, (1, 2)), jnp.tile(sin, (1, 2))


def _ret_tables(n_heads):
    L = RET_CHUNK_LEN
    pos = np.arange(L, dtype=np.float32)
    lane_head = np.arange(LANES) // HEAD_DIM
    decay = np.zeros((2, n_heads, L, L), np.float32)
    zeta = np.zeros((2, n_heads // 2, LANES, L), np.float32)
    xi = np.zeros((2, n_heads // 2, L, LANES), np.float32)
    gch = np.zeros((2, n_heads // 2, 1, LANES), np.float32)
    for d, first_exp in enumerate((5.0, 6.0)):
        e = np.float32(first_exp) + np.float32(2.0) * np.arange(n_heads, dtype=np.float32)
        lg = np.log1p(-np.exp2(-e)).astype(np.float32)
        diff = pos[:, None] - pos[None, :]
        if d == 1:
            diff = -diff
        for h in range(n_heads):
            decay[d, h] = np.where(diff >= 0, np.exp(lg[h] * np.maximum(diff, 0.0)), 0.0)
        for hp in range(n_heads // 2):
            lgl = lg[2 * hp + lane_head][None, :]
            to_end = (L - 1 - pos if d == 0 else pos)[:, None]
            zeta[d, hp] = np.exp(lgl * to_end).T
            xi[d, hp] = np.exp(lgl * (L - to_end))
            gch[d, hp] = np.exp(lgl * L)
    return tuple(jnp.asarray(a) for a in (decay, zeta, xi, gch))


def _ret_kernel(q_ref, k_ref, v_ref, g_ref, cos_ref, sin_ref, dec_ref, zeta_ref, xi_ref, gch_ref, rn_ref,
                o_ref, qr_ref, kt_ref, vb_ref, acc_ref, *, t_ctx):
    L = RET_CHUNK_LEN
    s_len = q_ref.shape[1]
    nc = s_len // L
    lane = lax.broadcasted_iota(jnp.int32, (1, LANES), 1)
    lo = lane < HEAD_DIM
    half = (lane & (HEAD_DIM // 4)) == 0
    rid = lax.broadcasted_iota(jnp.int32, (LANES, LANES), 0) < HEAD_DIM
    cid = lax.broadcasted_iota(jnp.int32, (LANES, LANES), 1) < HEAD_DIM
    same_head = rid == cid

    def rope(x, cos, sin):
        up = pltpu.roll(x, LANES - HEAD_DIM // 4, axis=1)
        dn = pltpu.roll(x, HEAD_DIM // 4, axis=1)
        return x * cos + jnp.where(half, up, dn) * sin

    def prep(c, carry):
        sl = pl.ds(pl.multiple_of(c * L, L), L)
        cos, sin = cos_ref[sl, :], sin_ref[sl, :]
        qr_ref[sl, :] = rope(q_ref[0, sl, :], cos, sin).astype(BF16)
        kt_ref[:, sl] = (rope(k_ref[0, sl, :], cos, sin) * HEAD_DIM ** -0.5).T.astype(BF16)
        vb_ref[sl, :] = v_ref[0, sl, :].astype(BF16)
        return carry
    lax.fori_loop(0, nc, prep, 0, unroll=RET_UNROLL)

    def chunk(d, c, state):
        sl = pl.ds(pl.multiple_of(c * L, L), L)
        q, kt, v = qr_ref[sl, :], kt_ref[:, sl], vb_ref[sl, :]
        inter = jnp.dot(q, state.astype(BF16), preferred_element_type=F32) * xi_ref[d, 0]
        outs = []
        for h2 in range(2):
            qm = jnp.where(lo if h2 == 0 else jnp.logical_not(lo), q, jnp.zeros_like(q))
            sd = (jnp.dot(qm, kt, preferred_element_type=F32) * dec_ref[d, h2]).astype(BF16)
            outs.append(jnp.dot(sd, v, preferred_element_type=F32))
        y = jnp.where(lo, outs[0], outs[1]) + inter
        kz_t = (kt.astype(F32) * zeta_ref[d, 0]).astype(BF16)
        state = state * gch_ref[d, 0] + jnp.where(same_head, jnp.dot(kz_t, v, preferred_element_type=F32), 0.0)
        return sl, y, state

    def fwd(c, state):
        sl, y, state = chunk(0, c, state)
        acc_ref[sl, :] = y
        return state
    lax.fori_loop(0, nc, fwd, jnp.zeros((LANES, LANES), F32), unroll=RET_UNROLL)

    def bwd(i, state):
        c = _reverse_chunk(i, nc, t_ctx // L)
        sl, y, state = chunk(1, c, state)
        y = _head_rms(y + acc_ref[sl, :], rn_ref[...], lo)
        g = g_ref[0, sl, :]
        o_ref[0, sl, :] = (y * (g * jax.nn.sigmoid(g))).astype(o_ref.dtype)
        return state
    lax.fori_loop(0, nc, bwd, jnp.zeros((LANES, LANES), F32), unroll=RET_UNROLL)


def _retention(p_main, rope_tabs, ret_tabs, r_w, t_ctx, col_q, n_heads):
    b_sz, s_len, _ = p_main.shape
    n_hp = n_heads // 2
    L = RET_CHUNK_LEN
    assert t_ctx % L == 0 and s_len % L == 0
    c0 = col_q // LANES
    cos, sin = rope_tabs
    decay, zeta, xi, gch = ret_tabs
    const2 = pl.BlockSpec((s_len, LANES), lambda b, h: (0, 0), pipeline_mode=pl.Buffered(1))
    return pl.pallas_call(
        functools.partial(_ret_kernel, t_ctx=t_ctx),
        grid=(b_sz, n_hp),
        in_specs=[_seq_spec(s_len, c0, 2), _seq_spec(s_len, c0 + n_hp, 2), _seq_spec(s_len, c0 + 2 * n_hp, 2),
                  _seq_spec(s_len, c0 + 3 * n_hp), const2, const2,
                  pl.BlockSpec((2, 2, L, L), lambda b, h: (0, h, 0, 0)),
                  pl.BlockSpec((2, 1, LANES, L), lambda b, h: (0, h, 0, 0)),
                  pl.BlockSpec((2, 1, L, LANES), lambda b, h: (0, h, 0, 0)),
                  pl.BlockSpec((2, 1, 1, LANES), lambda b, h: (0, h, 0, 0)),
                  pl.BlockSpec((1, LANES), lambda b, h: (0, 0))],
        out_specs=pl.BlockSpec((1, s_len, LANES), lambda b, h: (b, 0, h)),
        out_shape=jax.ShapeDtypeStruct((b_sz, s_len, n_heads * HEAD_DIM), BF16),
        scratch_shapes=[pltpu.VMEM((s_len, LANES), BF16), pltpu.VMEM((LANES, s_len), BF16),
                        pltpu.VMEM((s_len, LANES), BF16), pltpu.VMEM((s_len, LANES), F32)],
        compiler_params=_mixer_cparams(),
        name="retention",
    )(p_main, p_main, p_main, p_main, cos, sin, decay, zeta, xi, gch,
      jnp.concatenate([r_w, r_w]).reshape(1, LANES))


MLSTM_CHUNK_LEN = LANES
MLSTM_PREP_UNROLL = 3
MLSTM_UNROLL = 6
N_GATE_TYPES = 4
N_FWD_GATES = N_GATE_TYPES


def _log_sigmoid(x):
    return jnp.minimum(x, 0.0) - jnp.log1p(jnp.exp(-jnp.abs(x)))


def _split3(x):
    hi = x.astype(BF16)
    r = x - hi.astype(F32)
    mid = r.astype(BF16)
    return hi, mid, (r - mid.astype(F32)).astype(BF16)


def _mlstm_pair_kernel(q_ref, k_ref, v_ref, og_ref, gc_ref, gr_ref, wq_ref, wk_ref, bq_ref, bk_ref, gbc_ref, gbr_ref,
                       mn_ref, o_ref, qc_ref, kt_ref, vb_ref, acc_ref, st_ref, cc_ref, cr_ref, *, t_ctx):
    L = MLSTM_CHUNK_LEN
    s_len = q_ref.shape[1]
    nc = s_len // L
    lane = lax.broadcasted_iota(jnp.int32, (1, LANES), 1)
    lo = lane < HEAD_DIM
    head_lanes = (lo, jnp.logical_not(lo))
    sub_lo = lax.broadcasted_iota(jnp.int32, (LANES, 1), 0) < HEAD_DIM
    head_rows = (sub_lo, jnp.logical_not(sub_lo))
    row_i = lax.broadcasted_iota(jnp.int32, (L, L), 0)
    col_i = lax.broadcasted_iota(jnp.int32, (L, L), 1)
    causal = (row_i >= col_i, row_i <= col_i)
    tri_b = causal[0].astype(BF16)
    tri_bt = causal[1].astype(BF16)
    sub = lax.broadcasted_iota(jnp.int32, (L, 1), 0)
    gate_row = lax.broadcasted_iota(jnp.int32, (2 * N_GATE_TYPES, 1), 0)
    ones = jnp.ones((L, LANES), BF16)

    def conv(x_ref, w_ref, b_ref, t0):
        x = x_ref[0, pl.ds(t0, L), :]
        prev = x_ref[0, pl.ds(jnp.maximum(t0 - SUBLANES, 0), SUBLANES), :][SUBLANES - 1:SUBLANES]
        nxt = x_ref[0, pl.ds(jnp.minimum(t0 + L, s_len - SUBLANES), SUBLANES), :][0:1]
        prev = jnp.where((t0 != 0) & (t0 != t_ctx), prev, 0.0)
        nxt = jnp.where((t0 + L != t_ctx) & (t0 + L != s_len), nxt, 0.0)
        xm = jnp.where(sub == 0, prev, pltpu.roll(x, 1, axis=0))
        xp = jnp.where(sub == L - 1, nxt, pltpu.roll(x, L - 1, axis=0))
        y = b_ref[...] + xm * w_ref[0:1, :] + x * w_ref[1:2, :] + xp * w_ref[2:3, :]
        return y * jax.nn.sigmoid(y)

    def prep(c, carry):
        t0 = pl.multiple_of(c * L, L)
        sl = pl.ds(t0, L)
        qc_ref[sl, :] = conv(q_ref, wq_ref, bq_ref, t0).astype(BF16)
        kt_ref[:, sl] = (conv(k_ref, wk_ref, bk_ref, t0) * HEAD_DIM ** -0.5).T.astype(BF16)
        vb_ref[sl, :] = v_ref[0, sl, :].astype(BF16)
        lf_col = _log_sigmoid(gc_ref[0, sl, :] + gbc_ref[0])
        lf_row = _log_sigmoid(gr_ref[0, 0, :, sl] + gbr_ref[0])
        pre_col = sum(jnp.dot(tri_b, p, preferred_element_type=F32) for p in _split3(lf_col))
        pre_row = sum(jnp.dot(p, tri_bt, preferred_element_type=F32) for p in _split3(lf_row))
        cc_ref[sl, :] = jnp.where(lane < N_FWD_GATES, pre_col, pre_col[L - 1:L, :] - pre_col + lf_col)
        cr_ref[:, sl] = jnp.where(gate_row < N_FWD_GATES, pre_row, pre_row[:, L - 1:L] - pre_row + lf_row)
        return carry
    lax.fori_loop(0, nc, prep, 0, unroll=MLSTM_PREP_UNROLL)

    def chunk(d, c, m_state):
        sl = pl.ds(pl.multiple_of(c * L, L), L)
        q, kt, v = qc_ref[sl, :], kt_ref[:, sl], vb_ref[sl, :]
        g_row = gr_ref[0, 0, :, sl] + gbr_ref[0]
        cum_col, cum_row = cc_ref[sl, :], cr_ref[:, sl]
        end = L - 1 if d == 0 else 0
        outs, new_m = [], []
        for h2 in range(2):
            m_st = m_state[h2]
            ci, cf = 2 * (2 * d) + h2, 2 * (2 * d + 1) + h2
            a_rep = jnp.broadcast_to(cum_col[:, cf:cf + 1], (L, LANES))
            i_row, a_row = g_row[ci:ci + 1, :], cum_row[cf:cf + 1, :]
            b_tot = a_row[:, end:end + 1]
            d_log = jnp.where(causal[d], a_rep + (i_row - a_row), NEG_INF)
            m_intra = jnp.broadcast_to(jnp.max(d_log, axis=-1, keepdims=True), (L, LANES))
            qm = jnp.where(head_lanes[h2], q, jnp.zeros_like(q))
            s = jnp.dot(qm, kt, preferred_element_type=F32) * jnp.exp(d_log - m_intra)
            v1 = jnp.where(head_lanes[h2], v, ones)
            intra = jnp.dot(s.astype(BF16), v1, preferred_element_type=F32)
            inter = jnp.dot(qm, st_ref[h2].astype(BF16), preferred_element_type=F32)
            inter_log = a_rep + m_st
            m_q = jnp.maximum(m_intra, inter_log)
            num_den = jnp.exp(inter_log - m_q) * inter + jnp.exp(m_intra - m_q) * intra
            den = pltpu.roll(num_den, HEAD_DIM, axis=1)
            outs.append(num_den / jnp.maximum(jnp.abs(den), jnp.exp(-m_q)))
            w_row = b_tot - a_row + i_row
            m_loc = jnp.max(w_row, axis=-1, keepdims=True)
            kts = jnp.where(head_rows[h2], kt.astype(F32) * jnp.exp(w_row - m_loc), 0.0).astype(BF16)
            loc = jnp.dot(kts, v1, preferred_element_type=F32)
            m_new = jnp.maximum(b_tot + m_st, m_loc)
            st_ref[h2] = jnp.exp(b_tot + m_st - m_new) * st_ref[h2] + jnp.exp(m_loc - m_new) * loc
            new_m.append(m_new)
        return sl, jnp.where(lo, outs[0], outs[1]), tuple(new_m)

    zero_m = (jnp.zeros((1, 1), F32), jnp.zeros((1, 1), F32))

    st_ref[...] = jnp.zeros_like(st_ref)

    def fwd(c, m_state):
        sl, y, m_state = chunk(0, c, m_state)
        acc_ref[sl, :] = y
        return m_state
    lax.fori_loop(0, nc, fwd, zero_m, unroll=MLSTM_UNROLL)

    st_ref[...] = jnp.zeros_like(st_ref)

    def bwd(i, m_state):
        c = _reverse_chunk(i, nc, t_ctx // L)
        sl, y, m_state = chunk(1, c, m_state)
        y = _head_rms(y + acc_ref[sl, :], mn_ref[...], lo)
        o_ref[0, sl, :] = (y * jax.nn.sigmoid(og_ref[0, sl, :])).astype(o_ref.dtype)
        return m_state
    lax.fori_loop(0, nc, bwd, zero_m, unroll=MLSTM_UNROLL)


def _mlstm(p_main, p_gate, conv_w, conv_b, gate_b, m_w, t_ctx, n_heads):
    b_sz, s_len, _ = p_main.shape
    n_hp = n_heads // 2
    L = MLSTM_CHUNK_LEN
    assert t_ctx % L == 0 and s_len % L == 0
    n_g = 2 * N_GATE_TYPES
    g_rows = jnp.transpose(p_gate.reshape(b_sz, s_len, n_hp, LANES)[..., :n_g], (0, 2, 3, 1))
    gb = jnp.transpose(gate_b.reshape(N_GATE_TYPES, n_hp, 2), (1, 0, 2)).reshape(n_hp, n_g)
    gb_col = jnp.zeros((n_hp, 1, LANES), F32).at[:, 0, :n_g].set(gb)
    gb_row = gb.reshape(n_hp, n_g, 1)
    vec = lambda col: pl.BlockSpec((1, LANES), lambda b, h: (0, col + h))
    return pl.pallas_call(
        functools.partial(_mlstm_pair_kernel, t_ctx=t_ctx),
        grid=(b_sz, n_hp),
        in_specs=[_seq_spec(s_len, 0, 2), _seq_spec(s_len, n_hp, 2), _seq_spec(s_len, 2 * n_hp, 2),
                  _seq_spec(s_len, 3 * n_hp), _seq_spec(s_len, 0),
                  pl.BlockSpec((1, 1, n_g, s_len), lambda b, h: (b, h, 0, 0)),
                  pl.BlockSpec((3, LANES), lambda b, h: (0, h)),
                  pl.BlockSpec((3, LANES), lambda b, h: (0, n_hp + h)),
                  vec(0), vec(n_hp),
                  pl.BlockSpec((1, 1, LANES), lambda b, h: (h, 0, 0)),
                  pl.BlockSpec((1, n_g, 1), lambda b, h: (h, 0, 0)),
                  pl.BlockSpec((1, LANES), lambda b, h: (0, 0))],
        out_specs=pl.BlockSpec((1, s_len, LANES), lambda b, h: (b, 0, h)),
        out_shape=jax.ShapeDtypeStruct((b_sz, s_len, n_heads * HEAD_DIM), BF16),
        scratch_shapes=[pltpu.VMEM((s_len, LANES), BF16), pltpu.VMEM((LANES, s_len), BF16),
                        pltpu.VMEM((s_len, LANES), BF16), pltpu.VMEM((s_len, LANES), F32),
                        pltpu.VMEM((2, LANES, LANES), F32),
                        pltpu.VMEM((s_len, LANES), F32), pltpu.VMEM((n_g, s_len), F32)],
        compiler_params=_mixer_cparams(),
        name="mlstm",
    )(p_main, p_main, p_main, p_main, p_gate, g_rows, conv_w, conv_w, conv_b.reshape(1, -1),
      conv_b.reshape(1, -1), gb_col, gb_row, jnp.concatenate([m_w, m_w]).reshape(1, LANES))


def kernel(x, c, ctx, c_ctx, w_mod, b_mod, norm_mix, norm_ffn, w_in, w_out, mlstm_conv_w, mlstm_conv_b,
           mlstm_gate_b, mlstm_norm, na_q_norm, na_k_norm, na_rpb, ret_norm, router_w, router_b,
           expert_w_up, expert_b_up, expert_w_down, expert_b_down):
    b_sz, t_len, d = x.shape
    t_ctx = ctx.shape[1]
    depth = w_in.shape[0]
    n_e = router_w.shape[2]
    d_mix = w_out.shape[1]
    h_m = d_mix // (4 * HEAD_DIM)
    h_na = d_mix // (2 * HEAD_DIM)
    h_r = d_mix // (4 * HEAD_DIM)
    d_m, d_na = h_m * HEAD_DIM, h_na * HEAD_DIM
    n_gate = N_GATE_TYPES * h_m
    assert b_sz + 1 <= MOD_ROWS and n_e <= LANES and n_gate <= LANES
    s_len = t_ctx + t_len
    assert s_len % ROW_TILE == 0 and s_len % COMBINE_ROWS == 0

    s = jnp.concatenate([ctx, x], axis=1)
    cc = jnp.zeros((MOD_ROWS, d), F32).at[:b_sz].set(c).at[b_sz].set(c_ctx)
    mods = _modulation(cc, w_mod, b_mod).reshape(depth, MOD_ROWS, 1, 6 * d)

    g0 = 4 * d_m
    col_na = g0
    col_ret = col_na + 3 * d_na
    n_hp = h_m // 2
    gate_src = np.array([[g0 + t * h_m + 2 * hp + h2 for t in range(N_GATE_TYPES) for h2 in range(2)]
                         for hp in range(n_hp)])
    rope_tabs = _rope_tables(t_ctx, t_len)
    ret_tabs = _ret_tables(h_r)
    na_tabs = _na_bias_tables(na_rpb)
    f2 = expert_w_up.shape[3]
    w_up_all = expert_w_up.reshape(depth * n_e, d, f2)
    w_down_all = expert_w_down.reshape(depth * n_e, f2 // 2, d)
    b_up_all = _regroup_bias(expert_b_up).reshape(depth * n_e, 1, f2)
    b_down_all = expert_b_down.reshape(depth * n_e, 1, d)

    experts_out = None
    for l in range(depth):
        w_main = jnp.concatenate([w_in[l, :, :g0], w_in[l, :, g0 + n_gate:]], axis=1).astype(BF16)
        w_gate = jnp.zeros((d, n_hp, LANES), BF16).at[:, :, :gate_src.shape[1]].set(
            w_in[l][:, gate_src].astype(BF16)).reshape(d, n_hp * LANES)
        if experts_out is None:
            p_main, p_gate = _proj_in(s, mods[l], norm_mix[l], w_main, w_gate, t_ctx)
        else:
            s, p_main, p_gate = _proj_in(s, mods[l], norm_mix[l], w_main, w_gate, t_ctx,
                                         combine=(*experts_out, mods[l - 1]))

        mix = [_mlstm(p_main, p_gate, mlstm_conv_w[l], mlstm_conv_b[l], mlstm_gate_b[l], mlstm_norm[l],
                      t_ctx, h_m),
               _na_attention(p_main, na_q_norm[l], na_k_norm[l], na_tabs[l], t_ctx, col_na, h_na),
               _retention(p_main, rope_tabs, ret_tabs, ret_norm[l], t_ctx, col_ret, h_r)]

        rw = jnp.zeros((d, LANES), F32).at[:, :n_e].set(router_w[l])
        rb = jnp.zeros((1, LANES), F32).at[0, :n_e].set(router_b[l])
        s, tok, logits = _proj_out(mix, s, mods[l], norm_ffn[l], w_out[l].astype(BF16), rw, rb, t_ctx)

        experts_out = _moe(tok, logits[..., :n_e], n_e, l, w_up_all, w_down_all, b_up_all, b_down_all)
    return _latent_combine(*experts_out, s, mods[depth - 1], t_ctx)
```

```python
import functools

import jax
import jax.numpy as jnp
import numpy as np
from jax import lax
from jax.experimental import pallas as pl
from jax.experimental.pallas import tpu as pltpu

F32 = jnp.float32
BF16 = jnp.bfloat16

GRID_W = 64
HEAD_DIM = 64
NA_ROWS = 8
NA_COLS = 16
ROPE_BASE = 10000.0
TOP_K = 4
SWIGLU_ALPHA = 1.702
SWIGLU_LIMIT = 7.0
NORM_EPS = 1e-6
NEG_INF = -1e30

LANES = 128
SUBLANES = 8
VMEM_LIMIT = 48 * 1024 * 1024
MOD_ROWS = 8
MOD_COL_TILES = 4
ROW_TILE = 768
COMBINE_ROWS = 256
MOE_TILE = 768
REGROUP_ROWS = 512
MOE_VMEM_LIMIT = 56 * 1024 * 1024


def _cparams(sem):
    return pltpu.CompilerParams(dimension_semantics=sem, vmem_limit_bytes=VMEM_LIMIT)


def _mod_kernel(cc_ref, w_ref, b_ref, o_ref):
    cc = cc_ref[...]
    a = cc * jax.nn.sigmoid(cc)
    o_ref[0] = jnp.dot(a, w_ref[0], precision=lax.Precision.HIGHEST,
                       preferred_element_type=F32) + b_ref[0]


def _modulation(cc, w_mod, b_mod):
    n_l, d, d6 = w_mod.shape
    tn = d6 // MOD_COL_TILES
    return pl.pallas_call(
        _mod_kernel,
        grid=(n_l, MOD_COL_TILES),
        in_specs=[pl.BlockSpec((MOD_ROWS, d), lambda l, j: (0, 0)),
                  pl.BlockSpec((1, d, tn), lambda l, j: (l, 0, j)),
                  pl.BlockSpec((1, 1, tn), lambda l, j: (l, 0, j))],
        out_specs=pl.BlockSpec((1, MOD_ROWS, tn), lambda l, j: (l, 0, j)),
        out_shape=jax.ShapeDtypeStruct((n_l, MOD_ROWS, d6), F32),
        compiler_params=_cparams(("arbitrary", "arbitrary")),
        name="adaln_modulation",
    )(cc, w_mod, b_mod.reshape(n_l, 1, d6))


def _pick_mod(mb_ref, mc_ref, k, d, is_ctx):
    vb = mb_ref[0, :, k * d:(k + 1) * d]
    vc = mc_ref[0, :, k * d:(k + 1) * d]
    return jnp.where(is_ctx, vc, vb)


def _ctx_rows(tm, t_ctx):
    return lax.broadcasted_iota(jnp.int32, (tm, 1), 0) + pl.program_id(1) * tm < t_ctx


def _rms(x, g):
    return x * lax.rsqrt(jnp.mean(x * x, axis=-1, keepdims=True) + NORM_EPS) * g


def _pack_bf16_pairs(x):
    m = x.shape[1] // 2
    hi = lax.bitcast_convert_type(x[:, :m].astype(BF16).astype(F32), jnp.uint32)
    lo = lax.bitcast_convert_type(x[:, m:].astype(BF16).astype(F32), jnp.uint32)
    return hi | (lo >> 16)


def _unpack_bf16_pairs(p):
    hi = lax.bitcast_convert_type(p & jnp.uint32(0xFFFF0000), F32)
    lo = lax.bitcast_convert_type(p << 16, F32)
    return jnp.concatenate([hi, lo], axis=1)


PROJ_IN_ROWS = 384
PROJ_IN_COLS = 512


def _project(s, is_ctx, mb_ref, mc_ref, g_ref, w_ref, wg_ref, pm_ref, pg_ref):
    d = s.shape[1]
    sh = _pick_mod(mb_ref, mc_ref, 0, d, is_ctx)
    sc = _pick_mod(mb_ref, mc_ref, 1, d, is_ctx)
    xn = (_rms(s, g_ref[...]) * (1.0 + sc) + sh).astype(BF16)
    pg_ref[0] = jnp.dot(xn, wg_ref[...], preferred_element_type=F32)
    for j in range(w_ref.shape[1] // PROJ_IN_COLS):
        cols = slice(j * PROJ_IN_COLS, (j + 1) * PROJ_IN_COLS)
        pm_ref[0, :, cols] = jnp.dot(xn, w_ref[:, cols], preferred_element_type=F32)


def _proj_in_kernel(s_ref, mb_ref, mc_ref, g_ref, w_ref, wg_ref, pm_ref, pg_ref, *, t_ctx):
    _project(s_ref[0], _ctx_rows(s_ref.shape[1], t_ctx), mb_ref, mc_ref, g_ref, w_ref, wg_ref, pm_ref, pg_ref)


def _combine_rows(y_ref, gt_ref, s_ref, mb_ref, mc_ref, is_ctx):
    g2 = _pick_mod(mb_ref, mc_ref, 5, s_ref.shape[2], is_ctx)
    gt = gt_ref[0]
    y = _unpack_bf16_pairs(y_ref[0, 0]) * gt[:, 0:1]
    for k in range(1, y_ref.shape[0]):
        y = y + _unpack_bf16_pairs(y_ref[k, 0]) * gt[:, k:k + 1]
    return s_ref[0] + g2 * y


def _combine_proj_in_kernel(y_ref, gt_ref, s_ref, pmb_ref, pmc_ref, mb_ref, mc_ref, g_ref, w_ref, wg_ref,
                            so_ref, pm_ref, pg_ref, *, t_ctx):
    is_ctx = _ctx_rows(s_ref.shape[1], t_ctx)
    s_new = _combine_rows(y_ref, gt_ref, s_ref, pmb_ref, pmc_ref, is_ctx)
    so_ref[0] = s_new
    _project(s_new, is_ctx, mb_ref, mc_ref, g_ref, w_ref, wg_ref, pm_ref, pg_ref)


def _proj_in(s, mod_l, g, w_main, w_gate, t_ctx, combine=None):
    b_sz, s_len, d = s.shape
    n_main = w_main.shape[1]
    n_gate = w_gate.shape[1]
    tm = PROJ_IN_ROWS
    assert s_len % tm == 0 and n_main % PROJ_IN_COLS == 0
    row_spec = pl.BlockSpec((1, tm, d), lambda b, i: (b, i, 0))
    mod_specs = [pl.BlockSpec((1, 1, 6 * d), lambda b, i: (b, 0, 0)),
                 pl.BlockSpec((1, 1, 6 * d), lambda b, i: (b_sz, 0, 0))]
    w_specs = [pl.BlockSpec((1, d), lambda b, i: (0, 0)),
               pl.BlockSpec((d, n_main), lambda b, i: (0, 0)),
               pl.BlockSpec((d, n_gate), lambda b, i: (0, 0))]
    out_specs = [pl.BlockSpec((1, tm, n_main), lambda b, i: (b, i, 0)),
                 pl.BlockSpec((1, tm, n_gate), lambda b, i: (b, i, 0))]
    out_shape = [jax.ShapeDtypeStruct((b_sz, s_len, n_main), F32),
                 jax.ShapeDtypeStruct((b_sz, s_len, n_gate), F32)]
    if combine is None:
        return pl.pallas_call(
            functools.partial(_proj_in_kernel, t_ctx=t_ctx),
            grid=(b_sz, s_len // tm),
            in_specs=[row_spec] + mod_specs + w_specs,
            out_specs=out_specs, out_shape=out_shape,
            compiler_params=_cparams(("arbitrary", "arbitrary")),
            name="proj_in",
        )(s, mod_l, mod_l, g.reshape(1, d), w_main, w_gate)
    y_as, gates, mod_prev = combine
    n_k = y_as.shape[0]
    return pl.pallas_call(
        functools.partial(_combine_proj_in_kernel, t_ctx=t_ctx),
        grid=(b_sz, s_len // tm),
        in_specs=[pl.BlockSpec((n_k, 1, tm, d // 2), lambda b, i: (0, b, i, 0)),
                  pl.BlockSpec((1, tm, n_k), lambda b, i: (b, i, 0)),
                  row_spec] + mod_specs + mod_specs + w_specs,
        out_specs=[row_spec] + out_specs,
        out_shape=[jax.ShapeDtypeStruct((b_sz, s_len, d), F32)] + out_shape,
        compiler_params=_cparams(("arbitrary", "arbitrary")),
        name="combine_proj_in",
    )(y_as, gates, s, mod_prev, mod_prev, mod_l, mod_l, g.reshape(1, d), w_main, w_gate)


def _proj_out_kernel(ma_ref, mb2_ref, mc2_ref, s_ref, mb_ref, mc_ref, g_ref, w_ref, rw_ref, rb_ref,
                     so_ref, tok_ref, lg_ref, *, t_ctx):
    d = s_ref.shape[2]
    is_ctx = _ctx_rows(s_ref.shape[1], t_ctx)
    g1 = _pick_mod(mb_ref, mc_ref, 2, d, is_ctx)
    ka, kb = ma_ref.shape[2], ma_ref.shape[2] + mb2_ref.shape[2]
    y = (jnp.dot(ma_ref[0], w_ref[0:ka, :], preferred_element_type=F32)
         + jnp.dot(mb2_ref[0], w_ref[ka:kb, :], preferred_element_type=F32)
         + jnp.dot(mc2_ref[0], w_ref[kb:, :], preferred_element_type=F32))
    s_new = s_ref[0] + g1 * y
    so_ref[0] = s_new
    sh = _pick_mod(mb_ref, mc_ref, 3, d, is_ctx)
    sc = _pick_mod(mb_ref, mc_ref, 4, d, is_ctx)
    t = _rms(s_new, g_ref[...]) * (1.0 + sc) + sh
    tok_ref[0] = t
    w = rw_ref[...]
    t_hi, w_hi = t.astype(BF16), w.astype(BF16)
    t_lo = (t - t_hi.astype(F32)).astype(BF16)
    w_lo = (w - w_hi.astype(F32)).astype(BF16)
    lg_ref[0] = (jnp.dot(t_hi, w_hi, preferred_element_type=F32) + jnp.dot(t_hi, w_lo, preferred_element_type=F32)
                 + jnp.dot(t_lo, w_hi, preferred_element_type=F32) + rb_ref[...])


def _proj_out(mix_parts, s, mod_l, g, w_out, rw, rb, t_ctx):
    b_sz, s_len, d = s.shape
    tm = ROW_TILE
    row_spec = pl.BlockSpec((1, tm, d), lambda b, i: (b, i, 0))
    part_specs = [pl.BlockSpec((1, tm, m.shape[2]), lambda b, i: (b, i, 0)) for m in mix_parts]
    return pl.pallas_call(
        functools.partial(_proj_out_kernel, t_ctx=t_ctx),
        grid=(b_sz, s_len // tm),
        in_specs=part_specs + [row_spec,
                               pl.BlockSpec((1, 1, 6 * d), lambda b, i: (b, 0, 0)),
                               pl.BlockSpec((1, 1, 6 * d), lambda b, i: (b_sz, 0, 0)),
                               pl.BlockSpec((1, d), lambda b, i: (0, 0)),
                               pl.BlockSpec((d, d), lambda b, i: (0, 0)),
                               pl.BlockSpec((d, LANES), lambda b, i: (0, 0)),
                               pl.BlockSpec((1, LANES), lambda b, i: (0, 0))],
        out_specs=[row_spec, row_spec, pl.BlockSpec((1, tm, LANES), lambda b, i: (b, i, 0))],
        out_shape=[jax.ShapeDtypeStruct((b_sz, s_len, d), F32),
                   jax.ShapeDtypeStruct((b_sz, s_len, d), F32),
                   jax.ShapeDtypeStruct((b_sz, s_len, LANES), F32)],
        compiler_params=_cparams(("arbitrary", "arbitrary")),
        name="proj_out_router",
    )(*mix_parts, s, mod_l, mod_l, g.reshape(1, d), w_out, rw, rb)


PAIR = 2 * LANES


def _regroup_perm():
    dst = np.arange(PAIR)
    src = np.where(dst < LANES, 2 * dst, 2 * (dst - LANES) + 1)
    return jnp.asarray(np.arange(PAIR)[:, None] == src[None, :], BF16)


def _regroup_bias(b_up):
    lead = b_up.shape[:-1]
    b = b_up.reshape(*lead, -1, LANES, 2)
    return jnp.swapaxes(b, -1, -2).reshape(*lead, -1)


def _moe_kernel(be_ref, nu_ref, x_ref, wu_ref, wd_ref, bu_ref, bd_ref, p_ref, y_ref, wus_ref, wds_ref):
    i = pl.program_id(0)
    d, f2 = wus_ref.shape
    rows = min(REGROUP_ROWS, d)

    @pl.when(i < nu_ref[0])
    def _():
        @pl.when((i == 0) | (be_ref[i] != be_ref[jnp.maximum(i - 1, 0)]))
        def _():
            for r in range(d // rows):
                for j in range(f2 // PAIR):
                    w = wu_ref[0, r * rows:(r + 1) * rows, j * PAIR:(j + 1) * PAIR].astype(BF16)
                    wus_ref[r * rows:(r + 1) * rows, j * PAIR:(j + 1) * PAIR] = jnp.dot(
                        w, p_ref[...], preferred_element_type=F32).astype(BF16)
            wds_ref[...] = wd_ref[0].astype(BF16)

        up = jnp.dot(x_ref[...].astype(BF16), wus_ref[...], preferred_element_type=F32) + bu_ref[0]
        acts = []
        for j in range(f2 // PAIR):
            glu = jnp.minimum(up[:, j * PAIR:j * PAIR + LANES], SWIGLU_LIMIT)
            lin = jnp.clip(up[:, j * PAIR + LANES:(j + 1) * PAIR], -SWIGLU_LIMIT, SWIGLU_LIMIT)
            acts.append((glu * jax.nn.sigmoid(SWIGLU_ALPHA * glu) * (lin + 1.0)).astype(BF16))
        act = jnp.concatenate(acts, axis=1)
        y_ref[...] = _pack_bf16_pairs(jnp.dot(act, wds_ref[...], preferred_element_type=F32) + bd_ref[0])

    @pl.when(i >= nu_ref[0])
    def _():
        y_ref[...] = jnp.zeros_like(y_ref)


def _moe_blocks(blk_e, n_used, x_sorted, w_up, w_down, b_up, b_down):
    n_rows, d = x_sorted.shape
    _, _, f2 = w_up.shape
    assert f2 % PAIR == 0 and d % min(REGROUP_ROWS, d) == 0
    tm = MOE_TILE
    wmap = lambda i, be, nu: (be[i], 0, 0)
    return pl.pallas_call(
        _moe_kernel,
        grid_spec=pltpu.PrefetchScalarGridSpec(
            num_scalar_prefetch=2,
            grid=(n_rows // tm,),
            in_specs=[pl.BlockSpec((tm, d), lambda i, be, nu: (i, 0)),
                      pl.BlockSpec((1, d, f2), wmap),
                      pl.BlockSpec((1, f2 // 2, d), wmap),
                      pl.BlockSpec((1, 1, f2), wmap),
                      pl.BlockSpec((1, 1, d), wmap),
                      pl.BlockSpec((PAIR, PAIR), lambda i, be, nu: (0, 0))],
            out_specs=pl.BlockSpec((tm, d // 2), lambda i, be, nu: (i, 0)),
            scratch_shapes=[pltpu.VMEM((d, f2), BF16), pltpu.VMEM((f2 // 2, d), BF16)]),
        out_shape=jax.ShapeDtypeStruct((n_rows, d // 2), jnp.uint32),
        compiler_params=pltpu.CompilerParams(dimension_semantics=("arbitrary",),
                                             vmem_limit_bytes=MOE_VMEM_LIMIT),
        name="moe_expert_blocks",
    )(blk_e, n_used, x_sorted, w_up, w_down, b_up, b_down, _regroup_perm())


def _latent_combine_kernel(y_ref, gt_ref, s_ref, mb_ref, mc_ref, o_ref):
    no_ctx = lax.broadcasted_iota(jnp.int32, (s_ref.shape[1], 1), 0) < 0
    o_ref[0] = _combine_rows(y_ref, gt_ref, s_ref, mb_ref, mc_ref, no_ctx)


def _latent_combine(y_as, gates, s, mod_l, t_ctx):
    b_sz, s_len, d = s.shape
    n_k = y_as.shape[0]
    tm = COMBINE_ROWS
    assert t_ctx % tm == 0 and s_len % tm == 0
    skip = t_ctx // tm
    return pl.pallas_call(
        _latent_combine_kernel,
        grid=(b_sz, s_len // tm - skip),
        in_specs=[pl.BlockSpec((n_k, 1, tm, d // 2), lambda b, i: (0, b, i + skip, 0)),
                  pl.BlockSpec((1, tm, n_k), lambda b, i: (b, i + skip, 0)),
                  pl.BlockSpec((1, tm, d), lambda b, i: (b, i + skip, 0)),
                  pl.BlockSpec((1, 1, 6 * d), lambda b, i: (b, 0, 0)),
                  pl.BlockSpec((1, 1, 6 * d), lambda b, i: (b_sz, 0, 0))],
        out_specs=pl.BlockSpec((1, tm, d), lambda b, i: (b, i, 0)),
        out_shape=jax.ShapeDtypeStruct((b_sz, s_len - t_ctx, d), F32),
        compiler_params=_cparams(("arbitrary", "arbitrary")),
        name="moe_combine",
    )(y_as, gates, s, mod_l, mod_l)


def _moe(tok, logits, n_e, layer, w_up, w_down, b_up, b_down):
    b_sz, s_len, d = tok.shape
    n_tok = b_sz * s_len
    tm = MOE_TILE
    top_v, top_e = lax.top_k(logits.reshape(n_tok, n_e), TOP_K)
    gates = jax.nn.softmax(top_v, axis=-1)
    n_as = n_tok * TOP_K
    onehot = jnp.sum((top_e[:, :, None] == jnp.arange(n_e)[None, None, :]).astype(jnp.int32), axis=1)
    csum = jnp.cumsum(onehot, axis=0)
    counts = csum[-1]
    padded = (counts + tm - 1) // tm * tm
    end_pad = jnp.cumsum(padded)
    start_pad = end_pad - padded
    start = jnp.cumsum(counts) - counts
    dest = jnp.take_along_axis(csum - onehot + start_pad[None, :], top_e, axis=1).astype(jnp.int32)
    n_blocks = -(-n_as // tm) + n_e
    blk_first = jnp.arange(n_blocks) * tm
    blk_e = jnp.minimum(jnp.sum(blk_first[:, None] >= end_pad[None, :], axis=1), n_e - 1).astype(jnp.int32)
    n_used = (end_pad[-1:] // tm).astype(jnp.int32)
    tok_sorted = (jnp.argsort(top_e.reshape(n_as)) // TOP_K).astype(jnp.int32)
    j = (blk_first - start_pad[blk_e])[:, None] + jnp.arange(tm)[None, :]
    src = jnp.clip(start[blk_e][:, None] + j, 0, n_as - 1)
    row_tok = jnp.where(j < counts[blk_e][:, None], tok_sorted[src], 0).reshape(n_blocks * tm)
    x_sorted = tok.reshape(n_tok, d)[row_tok]
    y = _moe_blocks(blk_e + layer * n_e, n_used, x_sorted, w_up, w_down, b_up, b_down)
    y_as = y[dest.T].reshape(TOP_K, b_sz, s_len, d // 2)
    return y_as, gates.reshape(b_sz, s_len, TOP_K)


MIXER_VMEM_LIMIT = 58 * 1024 * 1024


def _seq_spec(s_len, col, buffers=1):
    return pl.BlockSpec((1, s_len, LANES), lambda b, h: (b, 0, col + h), pipeline_mode=pl.Buffered(buffers))


def _mixer_cparams():
    return pltpu.CompilerParams(dimension_semantics=("arbitrary", "arbitrary"), vmem_limit_bytes=MIXER_VMEM_LIMIT)


def _head_rms(x, w, lo):
    xx = x * x
    s0 = jnp.sum(jnp.where(lo, xx, 0.0), axis=-1, keepdims=True)
    s1 = jnp.sum(jnp.where(lo, 0.0, xx), axis=-1, keepdims=True)
    inv = lax.rsqrt(jnp.where(lo, s0, s1) * (1.0 / HEAD_DIM) + NORM_EPS)
    return x * inv * w


def _reverse_chunk(i, nc, n_ctx):
    return jnp.where(i < n_ctx, n_ctx - 1 - i, nc - 1 - i + n_ctx)


NA_STEP = 256
NA_KEYS = NA_ROWS * GRID_W
NA_GROUPS_PER_ITER = 4
NA_PREP_UNROLL = 11
NA_SHIFT_UNROLL = 13


def _na_bias_tables(rpb):
    n_l, n_h, n_dr, n_dc = rpb.shape
    col = np.arange(GRID_W)
    dc = np.clip(col[None, :] - col[:, None] + NA_COLS - 1, 0, n_dc - 1)
    onehot = (dc.reshape(1, -1) == np.arange(n_dc)[:, None]).astype(np.float32)
    toe = jnp.dot(rpb.reshape(-1, n_dc), jnp.asarray(onehot), precision=lax.Precision.HIGHEST)
    toe = toe.reshape(n_l, n_h, n_dr, GRID_W, GRID_W)
    col_start = np.clip(col - NA_COLS // 2, 0, GRID_W - NA_COLS)
    col_in = (col[None, :] >= col_start[:, None]) & (col[None, :] < col_start[:, None] + NA_COLS)
    toe = jnp.where(jnp.asarray(col_in), toe, NEG_INF)
    tabs = []
    for dr0 in range(NA_ROWS):
        t = toe[:, :, dr0:dr0 + NA_ROWS].reshape(n_l, n_h // 2, 2, NA_ROWS, GRID_W, GRID_W)
        t = jnp.transpose(t, (0, 1, 2, 4, 3, 5))
        tabs.append(t.reshape(n_l, n_h // 2, 2 * GRID_W, NA_KEYS))
    return jnp.stack(tabs, axis=1)


def _na_seq_kernel(q_ref, k_ref, v_ref, qw_ref, kw_ref, bias_ref, o_ref, kt_ref, vb_ref, s_ref, kn_ref, *,
                   t_ctx, rows):
    s_len = k_ref.shape[1]
    lo = lax.broadcasted_iota(jnp.int32, (1, LANES), 1) < HEAD_DIM
    rows_per_step = NA_STEP // GRID_W

    def prep(c, carry):
        t0 = pl.multiple_of(c * NA_STEP, NA_STEP)
        kn = _head_rms(k_ref[0, pl.ds(t0, NA_STEP), :], kw_ref[...], lo).astype(BF16)
        kn_ref[pl.ds(t0, NA_STEP), :] = kn
        kt_ref[0, :, pl.ds(t0, NA_STEP)] = kn.T
        vb_ref[pl.ds(t0, NA_STEP), :] = jnp.concatenate(
            [v_ref[0, pl.ds(t0, NA_STEP), :].astype(BF16), jnp.ones((NA_STEP, LANES), BF16)], axis=1)
        return carry
    lax.fori_loop(0, s_len // NA_STEP, prep, 0, unroll=NA_PREP_UNROLL)

    def prep_shifted(c, carry):
        t0 = pl.multiple_of(c * LANES, LANES)
        kt_ref[1, :, pl.ds(t0, LANES)] = kn_ref[pl.ds(pl.multiple_of(t0 + GRID_W, GRID_W), LANES), :].T
        return carry
    lax.fori_loop(0, (s_len - GRID_W) // LANES, prep_shifted, 0, unroll=NA_SHIFT_UNROLL)

    kc_t = kt_ref[0, :, 0:t_ctx]
    vc = vb_ref[0:t_ctx, :]

    def queries(tok0):
        qn = _head_rms(q_ref[0, pl.ds(tok0, NA_STEP), :], qw_ref[...], lo) * (HEAD_DIM ** -0.5)
        q0 = jnp.where(lo, qn, 0.0).astype(BF16)
        q1 = jnp.where(lo, 0.0, qn).astype(BF16)
        return [jnp.concatenate([q0[g * GRID_W:(g + 1) * GRID_W], q1[g * GRID_W:(g + 1) * GRID_W]], axis=0)
                for g in range(rows_per_step)]

    def finish(tok0, g, o2):
        o2 = o2[:, :LANES] * (1.0 / o2[:, LANES:])
        o = jnp.where(lo, o2[:GRID_W], o2[GRID_W:])
        o_ref[0, pl.ds(tok0 + g * GRID_W, GRID_W), :] = o.astype(o_ref.dtype)

    for g, q2 in enumerate(queries(0)):
        s_c = jnp.dot(q2, kc_t, preferred_element_type=F32)
        p_c = jnp.exp(s_c - jnp.max(s_c, axis=-1, keepdims=True))
        finish(0, g, jnp.dot(p_c.astype(BF16), vc, preferred_element_type=F32))

    def row_group(rg, slot):
        tok0 = pl.multiple_of(t_ctx + rg * NA_STEP, NA_STEP)
        t0s = []
        s_slot = s_ref.at[slot]
        for g, q2 in enumerate(queries(tok0)):
            r = rg * rows_per_step + g
            row_start = jnp.clip(r - NA_ROWS // 2, 0, rows - NA_ROWS)
            dr0 = row_start - r + NA_ROWS - 1
            t0 = pl.multiple_of(t_ctx + row_start * GRID_W, GRID_W)
            odd = (t0 // GRID_W) % (LANES // GRID_W)
            kw_t = kt_ref[odd, :, pl.ds(pl.multiple_of(t0 - odd * GRID_W, LANES), NA_KEYS)]
            s_slot[g, :, :NA_KEYS] = jnp.dot(q2, kw_t, preferred_element_type=F32) + bias_ref[dr0, 0]
            s_slot[g, :, NA_KEYS:] = jnp.dot(q2, kc_t, preferred_element_type=F32)
            t0s.append(t0)
        for g in range(rows_per_step):
            s = s_slot[g]
            p = jnp.exp(s - jnp.max(s, axis=-1, keepdims=True)).astype(BF16)
            finish(tok0, g, jnp.dot(p[:, :NA_KEYS], vb_ref[pl.ds(t0s[g], NA_KEYS), :], preferred_element_type=F32)
                   + jnp.dot(p[:, NA_KEYS:], vc, preferred_element_type=F32))

    def row_groups(i, carry):
        for slot in range(NA_GROUPS_PER_ITER):
            row_group(i * NA_GROUPS_PER_ITER + slot, slot)
        return carry
    lax.fori_loop(0, rows // rows_per_step // NA_GROUPS_PER_ITER, row_groups, 0)


def _na_attention(p_main, q_w, k_w, bias_tab, t_ctx, col_q, n_heads):
    b_sz, s_len, _ = p_main.shape
    n_hg = n_heads // 2
    rows = (s_len - t_ctx) // GRID_W
    rows_per_step = NA_STEP // GRID_W
    assert t_ctx == NA_STEP and s_len % NA_STEP == 0 and rows >= NA_ROWS
    assert rows % (rows_per_step * NA_GROUPS_PER_ITER) == 0
    cq = col_q // LANES
    seq = lambda col: pl.BlockSpec((1, s_len, LANES), lambda b, h: (b, 0, col + h))
    w2 = lambda w: jnp.concatenate([w, w]).reshape(1, LANES)
    return pl.pallas_call(
        functools.partial(_na_seq_kernel, t_ctx=t_ctx, rows=rows),
        grid=(b_sz, n_hg),
        in_specs=[seq(cq), seq(cq + n_hg), _seq_spec(s_len, cq + 2 * n_hg),
                  pl.BlockSpec((1, LANES), lambda b, h: (0, 0)),
                  pl.BlockSpec((1, LANES), lambda b, h: (0, 0)),
                  pl.BlockSpec((NA_ROWS, 1, 2 * GRID_W, NA_KEYS), lambda b, h: (0, h, 0, 0))],
        out_specs=pl.BlockSpec((1, s_len, LANES), lambda b, h: (b, 0, h)),
        out_shape=jax.ShapeDtypeStruct((b_sz, s_len, n_heads * HEAD_DIM), BF16),
        scratch_shapes=[pltpu.VMEM((2, LANES, s_len), BF16), pltpu.VMEM((s_len, 2 * LANES), BF16),
                        pltpu.VMEM((NA_GROUPS_PER_ITER, rows_per_step, 2 * GRID_W, NA_KEYS + t_ctx), F32),
                        pltpu.VMEM((s_len, LANES), BF16)],
        compiler_params=_cparams(("arbitrary", "arbitrary")),
        name="na_attention",
    )(p_main, p_main, p_main, w2(q_w), w2(k_w), bias_tab)


RET_CHUNK_LEN = 128
RET_UNROLL = 6


def _rope_tables(t_ctx, t_len):
    pos = jnp.arange(t_len)
    row = (pos // GRID_W).astype(F32)
    col = (pos % GRID_W).astype(F32)
    n = HEAD_DIM // 4
    inv = ROPE_BASE ** (-jnp.arange(n, dtype=F32) / n)
    ar = row[:, None] * inv
    ac = col[:, None] * inv
    cos = jnp.concatenate([jnp.cos(ar), jnp.cos(ar), jnp.cos(ac), jnp.cos(ac)], axis=-1)
    sin = jnp.concatenate([-jnp.sin(ar), jnp.sin(ar), -jnp.sin(ac), jnp.sin(ac)], axis=-1)
    cos = jnp.concatenate([jnp.ones((t_ctx, HEAD_DIM), F32), cos], axis=0)
    sin = jnp.concatenate([jnp.zeros((t_ctx, HEAD_DIM), F32), sin], axis=0)
    return jnp.tile(cos, (1, 2)), jnp.tile(sin, (1, 2))


def _ret_tables(n_heads):
    L = RET_CHUNK_LEN
    pos = np.arange(L, dtype=np.float32)
    lane_head = np.arange(LANES) // HEAD_DIM
    decay = np.zeros((2, n_heads, L, L), np.float32)
    zeta = np.zeros((2, n_heads // 2, LANES, L), np.float32)
    xi = np.zeros((2, n_heads // 2, L, LANES), np.float32)
    gch = np.zeros((2, n_heads // 2, 1, LANES), np.float32)
    for d, first_exp in enumerate((5.0, 6.0)):
        e = np.float32(first_exp) + np.float32(2.0) * np.arange(n_heads, dtype=np.float32)
        lg = np.log1p(-np.exp2(-e)).astype(np.float32)
        diff = pos[:, None] - pos[None, :]
        if d == 1:
            diff = -diff
        for h in range(n_heads):
            decay[d, h] = np.where(diff >= 0, np.exp(lg[h] * np.maximum(diff, 0.0)), 0.0)
        for hp in range(n_heads // 2):
            lgl = lg[2 * hp + lane_head][None, :]
            to_end = (L - 1 - pos if d == 0 else pos)[:, None]
            zeta[d, hp] = np.exp(lgl * to_end).T
            xi[d, hp] = np.exp(lgl * (L - to_end))
            gch[d, hp] = np.exp(lgl * L)
    return tuple(jnp.asarray(a) for a in (decay, zeta, xi, gch))


def _ret_kernel(q_ref, k_ref, v_ref, g_ref, cos_ref, sin_ref, dec_ref, zeta_ref, xi_ref, gch_ref, rn_ref,
                o_ref, qr_ref, kt_ref, vb_ref, acc_ref, *, t_ctx):
    L = RET_CHUNK_LEN
    s_len = q_ref.shape[1]
    nc = s_len // L
    lane = lax.broadcasted_iota(jnp.int32, (1, LANES), 1)
    lo = lane < HEAD_DIM
    half = (lane & (HEAD_DIM // 4)) == 0
    rid = lax.broadcasted_iota(jnp.int32, (LANES, LANES), 0) < HEAD_DIM
    cid = lax.broadcasted_iota(jnp.int32, (LANES, LANES), 1) < HEAD_DIM
    same_head = rid == cid

    def rope(x, cos, sin):
        up = pltpu.roll(x, LANES - HEAD_DIM // 4, axis=1)
        dn = pltpu.roll(x, HEAD_DIM // 4, axis=1)
        return x * cos + jnp.where(half, up, dn) * sin

    def prep(c, carry):
        sl = pl.ds(pl.multiple_of(c * L, L), L)
        cos, sin = cos_ref[sl, :], sin_ref[sl, :]
        qr_ref[sl, :] = rope(q_ref[0, sl, :], cos, sin).astype(BF16)
        kt_ref[:, sl] = (rope(k_ref[0, sl, :], cos, sin) * HEAD_DIM ** -0.5).T.astype(BF16)
        vb_ref[sl, :] = v_ref[0, sl, :].astype(BF16)
        return carry
    lax.fori_loop(0, nc, prep, 0, unroll=RET_UNROLL)

    def chunk(d, c, state):
        sl = pl.ds(pl.multiple_of(c * L, L), L)
        q, kt, v = qr_ref[sl, :], kt_ref[:, sl], vb_ref[sl, :]
        inter = jnp.dot(q, state.astype(BF16), preferred_element_type=F32) * xi_ref[d, 0]
        outs = []
        for h2 in range(2):
            qm = jnp.where(lo if h2 == 0 else jnp.logical_not(lo), q, jnp.zeros_like(q))
            sd = (jnp.dot(qm, kt, preferred_element_type=F32) * dec_ref[d, h2]).astype(BF16)
            outs.append(jnp.dot(sd, v, preferred_element_type=F32))
        y = jnp.where(lo, outs[0], outs[1]) + inter
        kz_t = (kt.astype(F32) * zeta_ref[d, 0]).astype(BF16)
        state = state * gch_ref[d, 0] + jnp.where(same_head, jnp.dot(kz_t, v, preferred_element_type=F32), 0.0)
        return sl, y, state

    def fwd(c, state):
        sl, y, state = chunk(0, c, state)
        acc_ref[sl, :] = y
        return state
    lax.fori_loop(0, nc, fwd, jnp.zeros((LANES, LANES), F32), unroll=RET_UNROLL)

    def bwd(i, state):
        c = _reverse_chunk(i, nc, t_ctx // L)
        sl, y, state = chunk(1, c, state)
        y = _head_rms(y + acc_ref[sl, :], rn_ref[...], lo)
        g = g_ref[0, sl, :]
        o_ref[0, sl, :] = (y * (g * jax.nn.sigmoid(g))).astype(o_ref.dtype)
        return state
    lax.fori_loop(0, nc, bwd, jnp.zeros((LANES, LANES), F32), unroll=RET_UNROLL)


def _retention(p_main, rope_tabs, ret_tabs, r_w, t_ctx, col_q, n_heads):
    b_sz, s_len, _ = p_main.shape
    n_hp = n_heads // 2
    L = RET_CHUNK_LEN
    assert t_ctx % L == 0 and s_len % L == 0
    c0 = col_q // LANES
    cos, sin = rope_tabs
    decay, zeta, xi, gch = ret_tabs
    const2 = pl.BlockSpec((s_len, LANES), lambda b, h: (0, 0), pipeline_mode=pl.Buffered(1))
    return pl.pallas_call(
        functools.partial(_ret_kernel, t_ctx=t_ctx),
        grid=(b_sz, n_hp),
        in_specs=[_seq_spec(s_len, c0, 2), _seq_spec(s_len, c0 + n_hp, 2), _seq_spec(s_len, c0 + 2 * n_hp, 2),
                  _seq_spec(s_len, c0 + 3 * n_hp), const2, const2,
                  pl.BlockSpec((2, 2, L, L), lambda b, h: (0, h, 0, 0)),
                  pl.BlockSpec((2, 1, LANES, L), lambda b, h: (0, h, 0, 0)),
                  pl.BlockSpec((2, 1, L, LANES), lambda b, h: (0, h, 0, 0)),
                  pl.BlockSpec((2, 1, 1, LANES), lambda b, h: (0, h, 0, 0)),
                  pl.BlockSpec((1, LANES), lambda b, h: (0, 0))],
        out_specs=pl.BlockSpec((1, s_len, LANES), lambda b, h: (b, 0, h)),
        out_shape=jax.ShapeDtypeStruct((b_sz, s_len, n_heads * HEAD_DIM), BF16),
        scratch_shapes=[pltpu.VMEM((s_len, LANES), BF16), pltpu.VMEM((LANES, s_len), BF16),
                        pltpu.VMEM((s_len, LANES), BF16), pltpu.VMEM((s_len, LANES), F32)],
        compiler_params=_mixer_cparams(),
        name="retention",
    )(p_main, p_main, p_main, p_main, cos, sin, decay, zeta, xi, gch,
      jnp.concatenate([r_w, r_w]).reshape(1, LANES))


MLSTM_CHUNK_LEN = LANES
MLSTM_PREP_UNROLL = 6
MLSTM_UNROLL = 6
N_GATE_TYPES = 4
N_FWD_GATES = N_GATE_TYPES


def _log_sigmoid(x):
    return jnp.minimum(x, 0.0) - jnp.log1p(jnp.exp(-jnp.abs(x)))


def _split3(x):
    hi = x.astype(BF16)
    r = x - hi.astype(F32)
    mid = r.astype(BF16)
    return hi, mid, (r - mid.astype(F32)).astype(BF16)


def _mlstm_pair_kernel(q_ref, k_ref, v_ref, og_ref, gc_ref, gr_ref, wq_ref, wk_ref, bq_ref, bk_ref, gbc_ref, gbr_ref,
                       mn_ref, o_ref, qc_ref, kt_ref, vb_ref, acc_ref, st_ref, cc_ref, cr_ref, *, t_ctx):
    L = MLSTM_CHUNK_LEN
    s_len = q_ref.shape[1]
    nc = s_len // L
    lane = lax.broadcasted_iota(jnp.int32, (1, LANES), 1)
    lo = lane < HEAD_DIM
    head_lanes = (lo, jnp.logical_not(lo))
    sub_lo = lax.broadcasted_iota(jnp.int32, (LANES, 1), 0) < HEAD_DIM
    head_rows = (sub_lo, jnp.logical_not(sub_lo))
    row_i = lax.broadcasted_iota(jnp.int32, (L, L), 0)
    col_i = lax.broadcasted_iota(jnp.int32, (L, L), 1)
    causal = (row_i >= col_i, row_i <= col_i)
    tri_b = causal[0].astype(BF16)
    tri_bt = causal[1].astype(BF16)
    sub = lax.broadcasted_iota(jnp.int32, (L, 1), 0)
    gate_row = lax.broadcasted_iota(jnp.int32, (2 * N_GATE_TYPES, 1), 0)
    ones = jnp.ones((L, LANES), BF16)

    def conv(x_ref, w_ref, b_ref, t0):
        x = x_ref[0, pl.ds(t0, L), :]
        prev = x_ref[0, pl.ds(jnp.maximum(t0 - SUBLANES, 0), SUBLANES), :][SUBLANES - 1:SUBLANES]
        nxt = x_ref[0, pl.ds(jnp.minimum(t0 + L, s_len - SUBLANES), SUBLANES), :][0:1]
        prev = jnp.where((t0 != 0) & (t0 != t_ctx), prev, 0.0)
        nxt = jnp.where((t0 + L != t_ctx) & (t0 + L != s_len), nxt, 0.0)
        xm = jnp.where(sub == 0, prev, pltpu.roll(x, 1, axis=0))
        xp = jnp.where(sub == L - 1, nxt, pltpu.roll(x, L - 1, axis=0))
        y = b_ref[...] + xm * w_ref[0:1, :] + x * w_ref[1:2, :] + xp * w_ref[2:3, :]
        return y * jax.nn.sigmoid(y)

    def prep(c, carry):
        t0 = pl.multiple_of(c * L, L)
        sl = pl.ds(t0, L)
        qc_ref[sl, :] = conv(q_ref, wq_ref, bq_ref, t0).astype(BF16)
        kt_ref[:, sl] = (conv(k_ref, wk_ref, bk_ref, t0) * HEAD_DIM ** -0.5).T.astype(BF16)
        vb_ref[sl, :] = v_ref[0, sl, :].astype(BF16)
        lf_col = _log_sigmoid(gc_ref[0, sl, :] + gbc_ref[0])
        lf_row = _log_sigmoid(gr_ref[0, 0, :, sl] + gbr_ref[0])
        pre_col = sum(jnp.dot(tri_b, p, preferred_element_type=F32) for p in _split3(lf_col))
        pre_row = sum(jnp.dot(p, tri_bt, preferred_element_type=F32) for p in _split3(lf_row))
        cc_ref[sl, :] = jnp.where(lane < N_FWD_GATES, pre_col, pre_col[L - 1:L, :] - pre_col + lf_col)
        cr_ref[:, sl] = jnp.where(gate_row < N_FWD_GATES, pre_row, pre_row[:, L - 1:L] - pre_row + lf_row)
        return carry
    lax.fori_loop(0, nc, prep, 0, unroll=MLSTM_PREP_UNROLL)

    def chunk(d, c, m_state):
        sl = pl.ds(pl.multiple_of(c * L, L), L)
        q, kt, v = qc_ref[sl, :], kt_ref[:, sl], vb_ref[sl, :]
        g_row = gr_ref[0, 0, :, sl] + gbr_ref[0]
        cum_col, cum_row = cc_ref[sl, :], cr_ref[:, sl]
        end = L - 1 if d == 0 else 0
        outs, new_m = [], []
        for h2 in range(2):
            m_st = m_state[h2]
            ci, cf = 2 * (2 * d) + h2, 2 * (2 * d + 1) + h2
            a_rep = jnp.broadcast_to(cum_col[:, cf:cf + 1], (L, LANES))
            i_row, a_row = g_row[ci:ci + 1, :], cum_row[cf:cf + 1, :]
            b_tot = a_row[:, end:end + 1]
            d_log = jnp.where(causal[d], a_rep + (i_row - a_row), NEG_INF)
            m_intra = jnp.broadcast_to(jnp.max(d_log, axis=-1, keepdims=True), (L, LANES))
            qm = jnp.where(head_lanes[h2], q, jnp.zeros_like(q))
            s = jnp.dot(qm, kt, preferred_element_type=F32) * jnp.exp(d_log - m_intra)
            v1 = jnp.where(head_lanes[h2], v, ones)
            intra = jnp.dot(s.astype(BF16), v1, preferred_element_type=F32)
            inter = jnp.dot(qm, st_ref[h2].astype(BF16), preferred_element_type=F32)
            inter_log = a_rep + m_st
            m_q = jnp.maximum(m_intra, inter_log)
            num_den = jnp.exp(inter_log - m_q) * inter + jnp.exp(m_intra - m_q) * intra
            den = pltpu.roll(num_den, HEAD_DIM, axis=1)
            outs.append(num_den / jnp.maximum(jnp.abs(den), jnp.exp(-m_q)))
            w_row = b_tot - a_row + i_row
            m_loc = jnp.max(w_row, axis=-1, keepdims=True)
            kts = jnp.where(head_rows[h2], kt.astype(F32) * jnp.exp(w_row - m_loc), 0.0).astype(BF16)
            loc = jnp.dot(kts, v1, preferred_element_type=F32)
            m_new = jnp.maximum(b_tot + m_st, m_loc)
            st_ref[h2] = jnp.exp(b_tot + m_st - m_new) * st_ref[h2] + jnp.exp(m_loc - m_new) * loc
            new_m.append(m_new)
        return sl, jnp.where(lo, outs[0], outs[1]), tuple(new_m)

    zero_m = (jnp.zeros((1, 1), F32), jnp.zeros((1, 1), F32))

    st_ref[...] = jnp.zeros_like(st_ref)

    def fwd(c, m_state):
        sl, y, m_state = chunk(0, c, m_state)
        acc_ref[sl, :] = y
        return m_state
    lax.fori_loop(0, nc, fwd, zero_m, unroll=MLSTM_UNROLL)

    st_ref[...] = jnp.zeros_like(st_ref)

    def bwd(i, m_state):
        c = _reverse_chunk(i, nc, t_ctx // L)
        sl, y, m_state = chunk(1, c, m_state)
        y = _head_rms(y + acc_ref[sl, :], mn_ref[...], lo)
        o_ref[0, sl, :] = (y * jax.nn.sigmoid(og_ref[0, sl, :])).astype(o_ref.dtype)
        return m_state
    lax.fori_loop(0, nc, bwd, zero_m, unroll=MLSTM_UNROLL)


def _mlstm(p_main, p_gate, conv_w, conv_b, gate_b, m_w, t_ctx, n_heads):
    b_sz, s_len, _ = p_main.shape
    n_hp = n_heads // 2
    L = MLSTM_CHUNK_LEN
    assert t_ctx % L == 0 and s_len % L == 0
    n_g = 2 * N_GATE_TYPES
    g_rows = jnp.transpose(p_gate.reshape(b_sz, s_len, n_hp, LANES)[..., :n_g], (0, 2, 3, 1))
    gb = jnp.transpose(gate_b.reshape(N_GATE_TYPES, n_hp, 2), (1, 0, 2)).reshape(n_hp, n_g)
    gb_col = jnp.zeros((n_hp, 1, LANES), F32).at[:, 0, :n_g].set(gb)
    gb_row = gb.reshape(n_hp, n_g, 1)
    vec = lambda col: pl.BlockSpec((1, LANES), lambda b, h: (0, col + h))
    return pl.pallas_call(
        functools.partial(_mlstm_pair_kernel, t_ctx=t_ctx),
        grid=(b_sz, n_hp),
        in_specs=[_seq_spec(s_len, 0, 2), _seq_spec(s_len, n_hp, 2), _seq_spec(s_len, 2 * n_hp, 2),
                  _seq_spec(s_len, 3 * n_hp), _seq_spec(s_len, 0),
                  pl.BlockSpec((1, 1, n_g, s_len), lambda b, h: (b, h, 0, 0)),
                  pl.BlockSpec((3, LANES), lambda b, h: (0, h)),
                  pl.BlockSpec((3, LANES), lambda b, h: (0, n_hp + h)),
                  vec(0), vec(n_hp),
                  pl.BlockSpec((1, 1, LANES), lambda b, h: (h, 0, 0)),
                  pl.BlockSpec((1, n_g, 1), lambda b, h: (h, 0, 0)),
                  pl.BlockSpec((1, LANES), lambda b, h: (0, 0))],
        out_specs=pl.BlockSpec((1, s_len, LANES), lambda b, h: (b, 0, h)),
        out_shape=jax.ShapeDtypeStruct((b_sz, s_len, n_heads * HEAD_DIM), BF16),
        scratch_shapes=[pltpu.VMEM((s_len, LANES), BF16), pltpu.VMEM((LANES, s_len), BF16),
                        pltpu.VMEM((s_len, LANES), BF16), pltpu.VMEM((s_len, LANES), F32),
                        pltpu.VMEM((2, LANES, LANES), F32),
                        pltpu.VMEM((s_len, LANES), F32), pltpu.VMEM((n_g, s_len), F32)],
        compiler_params=_mixer_cparams(),
        name="mlstm",
    )(p_main, p_main, p_main, p_main, p_gate, g_rows, conv_w, conv_w, conv_b.reshape(1, -1),
      conv_b.reshape(1, -1), gb_col, gb_row, jnp.concatenate([m_w, m_w]).reshape(1, LANES))


def kernel(x, c, ctx, c_ctx, w_mod, b_mod, norm_mix, norm_ffn, w_in, w_out, mlstm_conv_w, mlstm_conv_b,
           mlstm_gate_b, mlstm_norm, na_q_norm, na_k_norm, na_rpb, ret_norm, router_w, router_b,
           expert_w_up, expert_b_up, expert_w_down, expert_b_down):
    b_sz, t_len, d = x.shape
    t_ctx = ctx.shape[1]
    depth = w_in.shape[0]
    n_e = router_w.shape[2]
    d_mix = w_out.shape[1]
    h_m = d_mix // (4 * HEAD_DIM)
    h_na = d_mix // (2 * HEAD_DIM)
    h_r = d_mix // (4 * HEAD_DIM)
    d_m, d_na = h_m * HEAD_DIM, h_na * HEAD_DIM
    n_gate = N_GATE_TYPES * h_m
    assert b_sz + 1 <= MOD_ROWS and n_e <= LANES and n_gate <= LANES
    s_len = t_ctx + t_len
    assert s_len % ROW_TILE == 0 and s_len % COMBINE_ROWS == 0

    s = jnp.concatenate([ctx, x], axis=1)
    cc = jnp.zeros((MOD_ROWS, d), F32).at[:b_sz].set(c).at[b_sz].set(c_ctx)
    mods = _modulation(cc, w_mod, b_mod).reshape(depth, MOD_ROWS, 1, 6 * d)

    g0 = 4 * d_m
    col_na = g0
    col_ret = col_na + 3 * d_na
    n_hp = h_m // 2
    gate_src = np.array([[g0 + t * h_m + 2 * hp + h2 for t in range(N_GATE_TYPES) for h2 in range(2)]
                         for hp in range(n_hp)])
    rope_tabs = _rope_tables(t_ctx, t_len)
    ret_tabs = _ret_tables(h_r)
    na_tabs = _na_bias_tables(na_rpb)
    f2 = expert_w_up.shape[3]
    w_up_all = expert_w_up.reshape(depth * n_e, d, f2)
    w_down_all = expert_w_down.reshape(depth * n_e, f2 // 2, d)
    b_up_all = _regroup_bias(expert_b_up).reshape(depth * n_e, 1, f2)
    b_down_all = expert_b_down.reshape(depth * n_e, 1, d)

    experts_out = None
    for l in range(depth):
        w_main = jnp.concatenate([w_in[l, :, :g0], w_in[l, :, g0 + n_gate:]], axis=1).astype(BF16)
        w_gate = jnp.zeros((d, n_hp, LANES), BF16).at[:, :, :gate_src.shape[1]].set(
            w_in[l][:, gate_src].astype(BF16)).reshape(d, n_hp * LANES)
        if experts_out is None:
            p_main, p_gate = _proj_in(s, mods[l], norm_mix[l], w_main, w_gate, t_ctx)
        else:
            s, p_main, p_gate = _proj_in(s, mods[l], norm_mix[l], w_main, w_gate, t_ctx,
                                         combine=(*experts_out, mods[l - 1]))

        mix = [_mlstm(p_main, p_gate, mlstm_conv_w[l], mlstm_conv_b[l], mlstm_gate_b[l], mlstm_norm[l],
                      t_ctx, h_m),
               _na_attention(p_main, na_q_norm[l], na_k_norm[l], na_tabs[l], t_ctx, col_na, h_na),
               _retention(p_main, rope_tabs, ret_tabs, ret_norm[l], t_ctx, col_ret, h_r)]

        rw = jnp.zeros((d, LANES), F32).at[:, :n_e].set(router_w[l])
        rb = jnp.zeros((1, LANES), F32).at[0, :n_e].set(router_b[l])
        s, tok, logits = _proj_out(mix, s, mods[l], norm_ffn[l], w_out[l].astype(BF16), rw, rb, t_ctx)

        experts_out = _moe(tok, logits[..., :n_e], n_e, l, w_up_all, w_down_all, b_up_all, b_down_all)
    return _latent_combine(*experts_out, s, mods[depth - 1], t_ctx)
```

```python
import functools

import jax
import jax.numpy as jnp
import numpy as np
from jax import lax
from jax.experimental import pallas as pl
from jax.experimental.pallas import tpu as pltpu

F32 = jnp.float32
BF16 = jnp.bfloat16

GRID_W = 64
HEAD_DIM = 64
NA_ROWS = 8
NA_COLS = 16
ROPE_BASE = 10000.0
TOP_K = 4
SWIGLU_ALPHA = 1.702
SWIGLU_LIMIT = 7.0
NORM_EPS = 1e-6
NEG_INF = -1e30

LANES = 128
SUBLANES = 8
VMEM_LIMIT = 48 * 1024 * 1024
MOD_ROWS = 8
MOD_COL_TILES = 4
ROW_TILE = 768
COMBINE_ROWS = 256
MOE_TILE = 768
REGROUP_ROWS = 512
MOE_VMEM_LIMIT = 56 * 1024 * 1024


def _cparams(sem):
    return pltpu.CompilerParams(dimension_semantics=sem, vmem_limit_bytes=VMEM_LIMIT)


def _mod_kernel(cc_ref, w_ref, b_ref, o_ref):
    cc = cc_ref[...]
    a = cc * jax.nn.sigmoid(cc)
    o_ref[0] = jnp.dot(a, w_ref[0], precision=lax.Precision.HIGHEST,
                       preferred_element_type=F32) + b_ref[0]


def _modulation(cc, w_mod, b_mod):
    n_l, d, d6 = w_mod.shape
    tn = d6 // MOD_COL_TILES
    return pl.pallas_call(
        _mod_kernel,
        grid=(n_l, MOD_COL_TILES),
        in_specs=[pl.BlockSpec((MOD_ROWS, d), lambda l, j: (0, 0)),
                  pl.BlockSpec((1, d, tn), lambda l, j: (l, 0, j)),
                  pl.BlockSpec((1, 1, tn), lambda l, j: (l, 0, j))],
        out_specs=pl.BlockSpec((1, MOD_ROWS, tn), lambda l, j: (l, 0, j)),
        out_shape=jax.ShapeDtypeStruct((n_l, MOD_ROWS, d6), F32),
        compiler_params=_cparams(("arbitrary", "arbitrary")),
        name="adaln_modulation",
    )(cc, w_mod, b_mod.reshape(n_l, 1, d6))


def _pick_mod(mb_ref, mc_ref, k, d, is_ctx):
    vb = mb_ref[0, :, k * d:(k + 1) * d]
    vc = mc_ref[0, :, k * d:(k + 1) * d]
    return jnp.where(is_ctx, vc, vb)


def _ctx_rows(tm, t_ctx):
    return lax.broadcasted_iota(jnp.int32, (tm, 1), 0) + pl.program_id(1) * tm < t_ctx


def _rms(x, g):
    return x * lax.rsqrt(jnp.mean(x * x, axis=-1, keepdims=True) + NORM_EPS) * g


def _pack_bf16_pairs(x):
    m = x.shape[1] // 2
    hi = lax.bitcast_convert_type(x[:, :m].astype(BF16).astype(F32), jnp.uint32)
    lo = lax.bitcast_convert_type(x[:, m:].astype(BF16).astype(F32), jnp.uint32)
    return hi | (lo >> 16)


def _unpack_bf16_pairs(p):
    hi = lax.bitcast_convert_type(p & jnp.uint32(0xFFFF0000), F32)
    lo = lax.bitcast_convert_type(p << 16, F32)
    return jnp.concatenate([hi, lo], axis=1)


PROJ_IN_ROWS = 384
PROJ_IN_COLS = 512


def _project(s, is_ctx, mb_ref, mc_ref, g_ref, w_ref, wg_ref, pm_ref, pg_ref):
    d = s.shape[1]
    sh = _pick_mod(mb_ref, mc_ref, 0, d, is_ctx)
    sc = _pick_mod(mb_ref, mc_ref, 1, d, is_ctx)
    xn = (_rms(s, g_ref[...]) * (1.0 + sc) + sh).astype(BF16)
    pg_ref[0] = jnp.dot(xn, wg_ref[...], preferred_element_type=F32)
    for j in range(w_ref.shape[1] // PROJ_IN_COLS):
        cols = slice(j * PROJ_IN_COLS, (j + 1) * PROJ_IN_COLS)
        pm_ref[0, :, cols] = jnp.dot(xn, w_ref[:, cols], preferred_element_type=F32)


def _proj_in_kernel(s_ref, mb_ref, mc_ref, g_ref, w_ref, wg_ref, pm_ref, pg_ref, *, t_ctx):
    _project(s_ref[0], _ctx_rows(s_ref.shape[1], t_ctx), mb_ref, mc_ref, g_ref, w_ref, wg_ref, pm_ref, pg_ref)


def _combine_rows(y_ref, gt_ref, s_ref, mb_ref, mc_ref, is_ctx):
    g2 = _pick_mod(mb_ref, mc_ref, 5, s_ref.shape[2], is_ctx)
    gt = gt_ref[0]
    y = _unpack_bf16_pairs(y_ref[0, 0]) * gt[:, 0:1]
    for k in range(1, y_ref.shape[0]):
        y = y + _unpack_bf16_pairs(y_ref[k, 0]) * gt[:, k:k + 1]
    return s_ref[0] + g2 * y


def _combine_proj_in_kernel(y_ref, gt_ref, s_ref, pmb_ref, pmc_ref, mb_ref, mc_ref, g_ref, w_ref, wg_ref,
                            so_ref, pm_ref, pg_ref, *, t_ctx):
    is_ctx = _ctx_rows(s_ref.shape[1], t_ctx)
    s_new = _combine_rows(y_ref, gt_ref, s_ref, pmb_ref, pmc_ref, is_ctx)
    so_ref[0] = s_new
    _project(s_new, is_ctx, mb_ref, mc_ref, g_ref, w_ref, wg_ref, pm_ref, pg_ref)


def _proj_in(s, mod_l, g, w_main, w_gate, t_ctx, combine=None):
    b_sz, s_len, d = s.shape
    n_main = w_main.shape[1]
    n_gate = w_gate.shape[1]
    tm = PROJ_IN_ROWS
    assert s_len % tm == 0 and n_main % PROJ_IN_COLS == 0
    row_spec = pl.BlockSpec((1, tm, d), lambda b, i: (b, i, 0))
    mod_specs = [pl.BlockSpec((1, 1, 6 * d), lambda b, i: (b, 0, 0)),
                 pl.BlockSpec((1, 1, 6 * d), lambda b, i: (b_sz, 0, 0))]
    w_specs = [pl.BlockSpec((1, d), lambda b, i: (0, 0)),
               pl.BlockSpec((d, n_main), lambda b, i: (0, 0)),
               pl.BlockSpec((d, n_gate), lambda b, i: (0, 0))]
    out_specs = [pl.BlockSpec((1, tm, n_main), lambda b, i: (b, i, 0)),
                 pl.BlockSpec((1, tm, n_gate), lambda b, i: (b, i, 0))]
    out_shape = [jax.ShapeDtypeStruct((b_sz, s_len, n_main), F32),
                 jax.ShapeDtypeStruct((b_sz, s_len, n_gate), F32)]
    if combine is None:
        return pl.pallas_call(
            functools.partial(_proj_in_kernel, t_ctx=t_ctx),
            grid=(b_sz, s_len // tm),
            in_specs=[row_spec] + mod_specs + w_specs,
            out_specs=out_specs, out_shape=out_shape,
            compiler_params=_cparams(("arbitrary", "arbitrary")),
            name="proj_in",
        )(s, mod_l, mod_l, g.reshape(1, d), w_main, w_gate)
    y_as, gates, mod_prev = combine
    n_k = y_as.shape[0]
    return pl.pallas_call(
        functools.partial(_combine_proj_in_kernel, t_ctx=t_ctx),
        grid=(b_sz, s_len // tm),
        in_specs=[pl.BlockSpec((n_k, 1, tm, d // 2), lambda b, i: (0, b, i, 0)),
                  pl.BlockSpec((1, tm, n_k), lambda b, i: (b, i, 0)),
                  row_spec] + mod_specs + mod_specs + w_specs,
        out_specs=[row_spec] + out_specs,
        out_shape=[jax.ShapeDtypeStruct((b_sz, s_len, d), F32)] + out_shape,
        compiler_params=_cparams(("arbitrary", "arbitrary")),
        name="combine_proj_in",
    )(y_as, gates, s, mod_prev, mod_prev, mod_l, mod_l, g.reshape(1, d), w_main, w_gate)


def _proj_out_kernel(ma_ref, mb2_ref, mc2_ref, s_ref, mb_ref, mc_ref, g_ref, w_ref, rw_ref, rb_ref,
                     so_ref, tok_ref, lg_ref, *, t_ctx):
    d = s_ref.shape[2]
    is_ctx = _ctx_rows(s_ref.shape[1], t_ctx)
    g1 = _pick_mod(mb_ref, mc_ref, 2, d, is_ctx)
    ka, kb = ma_ref.shape[2], ma_ref.shape[2] + mb2_ref.shape[2]
    y = (jnp.dot(ma_ref[0], w_ref[0:ka, :], preferred_element_type=F32)
         + jnp.dot(mb2_ref[0], w_ref[ka:kb, :], preferred_element_type=F32)
         + jnp.dot(mc2_ref[0], w_ref[kb:, :], preferred_element_type=F32))
    s_new = s_ref[0] + g1 * y
    so_ref[0] = s_new
    sh = _pick_mod(mb_ref, mc_ref, 3, d, is_ctx)
    sc = _pick_mod(mb_ref, mc_ref, 4, d, is_ctx)
    t = _rms(s_new, g_ref[...]) * (1.0 + sc) + sh
    tok_ref[0] = t
    w = rw_ref[...]
    t_hi, w_hi = t.astype(BF16), w.astype(BF16)
    t_lo = (t - t_hi.astype(F32)).astype(BF16)
    w_lo = (w - w_hi.astype(F32)).astype(BF16)
    lg_ref[0] = (jnp.dot(t_hi, w_hi, preferred_element_type=F32) + jnp.dot(t_hi, w_lo, preferred_element_type=F32)
                 + jnp.dot(t_lo, w_hi, preferred_element_type=F32) + rb_ref[...])


def _proj_out(mix_parts, s, mod_l, g, w_out, rw, rb, t_ctx):
    b_sz, s_len, d = s.shape
    tm = ROW_TILE
    row_spec = pl.BlockSpec((1, tm, d), lambda b, i: (b, i, 0))
    part_specs = [pl.BlockSpec((1, tm, m.shape[2]), lambda b, i: (b, i, 0)) for m in mix_parts]
    return pl.pallas_call(
        functools.partial(_proj_out_kernel, t_ctx=t_ctx),
        grid=(b_sz, s_len // tm),
        in_specs=part_specs + [row_spec,
                               pl.BlockSpec((1, 1, 6 * d), lambda b, i: (b, 0, 0)),
                               pl.BlockSpec((1, 1, 6 * d), lambda b, i: (b_sz, 0, 0)),
                               pl.BlockSpec((1, d), lambda b, i: (0, 0)),
                               pl.BlockSpec((d, d), lambda b, i: (0, 0)),
                               pl.BlockSpec((d, LANES), lambda b, i: (0, 0)),
                               pl.BlockSpec((1, LANES), lambda b, i: (0, 0))],
        out_specs=[row_spec, row_spec, pl.BlockSpec((1, tm, LANES), lambda b, i: (b, i, 0))],
        out_shape=[jax.ShapeDtypeStruct((b_sz, s_len, d), F32),
                   jax.ShapeDtypeStruct((b_sz, s_len, d), F32),
                   jax.ShapeDtypeStruct((b_sz, s_len, LANES), F32)],
        compiler_params=_cparams(("arbitrary", "arbitrary")),
        name="proj_out_router",
    )(*mix_parts, s, mod_l, mod_l, g.reshape(1, d), w_out, rw, rb)


PAIR = 2 * LANES


def _regroup_perm():
    dst = np.arange(PAIR)
    src = np.where(dst < LANES, 2 * dst, 2 * (dst - LANES) + 1)
    return jnp.asarray(np.arange(PAIR)[:, None] == src[None, :], BF16)


def _regroup_bias(b_up):
    lead = b_up.shape[:-1]
    b = b_up.reshape(*lead, -1, LANES, 2)
    return jnp.swapaxes(b, -1, -2).reshape(*lead, -1)


def _moe_kernel(be_ref, nu_ref, x_ref, wu_ref, wd_ref, bu_ref, bd_ref, p_ref, y_ref, wus_ref, wds_ref):
    i = pl.program_id(0)
    d, f2 = wus_ref.shape
    rows = min(REGROUP_ROWS, d)

    @pl.when(i < nu_ref[0])
    def _():
        @pl.when((i == 0) | (be_ref[i] != be_ref[jnp.maximum(i - 1, 0)]))
        def _():
            for r in range(d // rows):
                for j in range(f2 // PAIR):
                    w = wu_ref[0, r * rows:(r + 1) * rows, j * PAIR:(j + 1) * PAIR].astype(BF16)
                    wus_ref[r * rows:(r + 1) * rows, j * PAIR:(j + 1) * PAIR] = jnp.dot(
                        w, p_ref[...], preferred_element_type=F32).astype(BF16)
            wds_ref[...] = wd_ref[0].astype(BF16)

        up = jnp.dot(x_ref[...].astype(BF16), wus_ref[...], preferred_element_type=F32) + bu_ref[0]
        acts = []
        for j in range(f2 // PAIR):
            glu = jnp.minimum(up[:, j * PAIR:j * PAIR + LANES], SWIGLU_LIMIT)
            lin = jnp.clip(up[:, j * PAIR + LANES:(j + 1) * PAIR], -SWIGLU_LIMIT, SWIGLU_LIMIT)
            acts.append((glu * jax.nn.sigmoid(SWIGLU_ALPHA * glu) * (lin + 1.0)).astype(BF16))
        act = jnp.concatenate(acts, axis=1)
        y_ref[...] = _pack_bf16_pairs(jnp.dot(act, wds_ref[...], preferred_element_type=F32) + bd_ref[0])

    @pl.when(i >= nu_ref[0])
    def _():
        y_ref[...] = jnp.zeros_like(y_ref)


def _moe_blocks(blk_e, n_used, x_sorted, w_up, w_down, b_up, b_down):
    n_rows, d = x_sorted.shape
    _, _, f2 = w_up.shape
    assert f2 % PAIR == 0 and d % min(REGROUP_ROWS, d) == 0
    tm = MOE_TILE
    wmap = lambda i, be, nu: (be[i], 0, 0)
    return pl.pallas_call(
        _moe_kernel,
        grid_spec=pltpu.PrefetchScalarGridSpec(
            num_scalar_prefetch=2,
            grid=(n_rows // tm,),
            in_specs=[pl.BlockSpec((tm, d), lambda i, be, nu: (i, 0)),
                      pl.BlockSpec((1, d, f2), wmap),
                      pl.BlockSpec((1, f2 // 2, d), wmap),
                      pl.BlockSpec((1, 1, f2), wmap),
                      pl.BlockSpec((1, 1, d), wmap),
                      pl.BlockSpec((PAIR, PAIR), lambda i, be, nu: (0, 0))],
            out_specs=pl.BlockSpec((tm, d // 2), lambda i, be, nu: (i, 0)),
            scratch_shapes=[pltpu.VMEM((d, f2), BF16), pltpu.VMEM((f2 // 2, d), BF16)]),
        out_shape=jax.ShapeDtypeStruct((n_rows, d // 2), jnp.uint32),
        compiler_params=pltpu.CompilerParams(dimension_semantics=("arbitrary",),
                                             vmem_limit_bytes=MOE_VMEM_LIMIT),
        name="moe_expert_blocks",
    )(blk_e, n_used, x_sorted, w_up, w_down, b_up, b_down, _regroup_perm())


def _latent_combine_kernel(y_ref, gt_ref, s_ref, mb_ref, mc_ref, o_ref):
    no_ctx = lax.broadcasted_iota(jnp.int32, (s_ref.shape[1], 1), 0) < 0
    o_ref[0] = _combine_rows(y_ref, gt_ref, s_ref, mb_ref, mc_ref, no_ctx)


def _latent_combine(y_as, gates, s, mod_l, t_ctx):
    b_sz, s_len, d = s.shape
    n_k = y_as.shape[0]
    tm = COMBINE_ROWS
    assert t_ctx % tm == 0 and s_len % tm == 0
    skip = t_ctx // tm
    return pl.pallas_call(
        _latent_combine_kernel,
        grid=(b_sz, s_len // tm - skip),
        in_specs=[pl.BlockSpec((n_k, 1, tm, d // 2), lambda b, i: (0, b, i + skip, 0)),
                  pl.BlockSpec((1, tm, n_k), lambda b, i: (b, i + skip, 0)),
                  pl.BlockSpec((1, tm, d), lambda b, i: (b, i + skip, 0)),
                  pl.BlockSpec((1, 1, 6 * d), lambda b, i: (b, 0, 0)),
                  pl.BlockSpec((1, 1, 6 * d), lambda b, i: (b_sz, 0, 0))],
        out_specs=pl.BlockSpec((1, tm, d), lambda b, i: (b, i, 0)),
        out_shape=jax.ShapeDtypeStruct((b_sz, s_len - t_ctx, d), F32),
        compiler_params=_cparams(("arbitrary", "arbitrary")),
        name="moe_combine",
    )(y_as, gates, s, mod_l, mod_l)


def _moe(tok, logits, n_e, layer, w_up, w_down, b_up, b_down):
    b_sz, s_len, d = tok.shape
    n_tok = b_sz * s_len
    tm = MOE_TILE
    top_v, top_e = lax.top_k(logits.reshape(n_tok, n_e), TOP_K)
    gates = jax.nn.softmax(top_v, axis=-1)
    n_as = n_tok * TOP_K
    onehot = jnp.sum((top_e[:, :, None] == jnp.arange(n_e)[None, None, :]).astype(jnp.int32), axis=1)
    csum = jnp.cumsum(onehot, axis=0)
    counts = csum[-1]
    padded = (counts + tm - 1) // tm * tm
    end_pad = jnp.cumsum(padded)
    start_pad = end_pad - padded
    start = jnp.cumsum(counts) - counts
    dest = jnp.take_along_axis(csum - onehot + start_pad[None, :], top_e, axis=1).astype(jnp.int32)
    n_blocks = -(-n_as // tm) + n_e
    blk_first = jnp.arange(n_blocks) * tm
    blk_e = jnp.minimum(jnp.sum(blk_first[:, None] >= end_pad[None, :], axis=1), n_e - 1).astype(jnp.int32)
    n_used = (end_pad[-1:] // tm).astype(jnp.int32)
    tok_sorted = (jnp.argsort(top_e.reshape(n_as)) // TOP_K).astype(jnp.int32)
    j = (blk_first - start_pad[blk_e])[:, None] + jnp.arange(tm)[None, :]
    src = jnp.clip(start[blk_e][:, None] + j, 0, n_as - 1)
    spread = (blk_first[:, None] + jnp.arange(tm)[None, :]) % n_tok
    row_tok = jnp.where(j < counts[blk_e][:, None], tok_sorted[src], spread).reshape(n_blocks * tm)
    x_sorted = tok.reshape(n_tok, d)[row_tok]
    y = _moe_blocks(blk_e + layer * n_e, n_used, x_sorted, w_up, w_down, b_up, b_down)
    y_as = y[dest.T].reshape(TOP_K, b_sz, s_len, d // 2)
    return y_as, gates.reshape(b_sz, s_len, TOP_K)


MIXER_VMEM_LIMIT = 58 * 1024 * 1024


def _seq_spec(s_len, col, buffers=1):
    return pl.BlockSpec((1, s_len, LANES), lambda b, h: (b, 0, col + h), pipeline_mode=pl.Buffered(buffers))


def _mixer_cparams():
    return pltpu.CompilerParams(dimension_semantics=("arbitrary", "arbitrary"), vmem_limit_bytes=MIXER_VMEM_LIMIT)


def _head_rms(x, w, lo):
    xx = x * x
    s0 = jnp.sum(jnp.where(lo, xx, 0.0), axis=-1, keepdims=True)
    s1 = jnp.sum(jnp.where(lo, 0.0, xx), axis=-1, keepdims=True)
    inv = lax.rsqrt(jnp.where(lo, s0, s1) * (1.0 / HEAD_DIM) + NORM_EPS)
    return x * inv * w


def _reverse_chunk(i, nc, n_ctx):
    return jnp.where(i < n_ctx, n_ctx - 1 - i, nc - 1 - i + n_ctx)


NA_STEP = 256
NA_KEYS = NA_ROWS * GRID_W
NA_GROUPS_PER_ITER = 4
NA_PREP_UNROLL = 11
NA_SHIFT_UNROLL = 13


def _na_bias_tables(rpb):
    n_l, n_h, n_dr, n_dc = rpb.shape
    col = np.arange(GRID_W)
    dc = np.clip(col[None, :] - col[:, None] + NA_COLS - 1, 0, n_dc - 1)
    onehot = (dc.reshape(1, -1) == np.arange(n_dc)[:, None]).astype(np.float32)
    toe = jnp.dot(rpb.reshape(-1, n_dc), jnp.asarray(onehot), precision=lax.Precision.HIGHEST)
    toe = toe.reshape(n_l, n_h, n_dr, GRID_W, GRID_W)
    col_start = np.clip(col - NA_COLS // 2, 0, GRID_W - NA_COLS)
    col_in = (col[None, :] >= col_start[:, None]) & (col[None, :] < col_start[:, None] + NA_COLS)
    toe = jnp.where(jnp.asarray(col_in), toe, NEG_INF)
    tabs = []
    for dr0 in range(NA_ROWS):
        t = toe[:, :, dr0:dr0 + NA_ROWS].reshape(n_l, n_h // 2, 2, NA_ROWS, GRID_W, GRID_W)
        t = jnp.transpose(t, (0, 1, 2, 4, 3, 5))
        tabs.append(t.reshape(n_l, n_h // 2, 2 * GRID_W, NA_KEYS))
    return jnp.stack(tabs, axis=1)


def _na_seq_kernel(q_ref, k_ref, v_ref, qw_ref, kw_ref, bias_ref, o_ref, kt_ref, vb_ref, s_ref, kn_ref, *,
                   t_ctx, rows):
    s_len = k_ref.shape[1]
    lo = lax.broadcasted_iota(jnp.int32, (1, LANES), 1) < HEAD_DIM
    rows_per_step = NA_STEP // GRID_W

    def prep(c, carry):
        t0 = pl.multiple_of(c * NA_STEP, NA_STEP)
        kn = _head_rms(k_ref[0, pl.ds(t0, NA_STEP), :], kw_ref[...], lo).astype(BF16)
        kn_ref[pl.ds(t0, NA_STEP), :] = kn
        kt_ref[0, :, pl.ds(t0, NA_STEP)] = kn.T
        vb_ref[pl.ds(t0, NA_STEP), :] = jnp.concatenate(
            [v_ref[0, pl.ds(t0, NA_STEP), :].astype(BF16), jnp.ones((NA_STEP, LANES), BF16)], axis=1)
        return carry
    lax.fori_loop(0, s_len // NA_STEP, prep, 0, unroll=NA_PREP_UNROLL)

    def prep_shifted(c, carry):
        t0 = pl.multiple_of(c * LANES, LANES)
        kt_ref[1, :, pl.ds(t0, LANES)] = kn_ref[pl.ds(pl.multiple_of(t0 + GRID_W, GRID_W), LANES), :].T
        return carry
    lax.fori_loop(0, (s_len - GRID_W) // LANES, prep_shifted, 0, unroll=NA_SHIFT_UNROLL)

    kc_t = kt_ref[0, :, 0:t_ctx]
    vc = vb_ref[0:t_ctx, :]

    def queries(tok0):
        qn = _head_rms(q_ref[0, pl.ds(tok0, NA_STEP), :], qw_ref[...], lo) * (HEAD_DIM ** -0.5)
        q0 = jnp.where(lo, qn, 0.0).astype(BF16)
        q1 = jnp.where(lo, 0.0, qn).astype(BF16)
        return [jnp.concatenate([q0[g * GRID_W:(g + 1) * GRID_W], q1[g * GRID_W:(g + 1) * GRID_W]], axis=0)
                for g in range(rows_per_step)]

    def finish(tok0, g, o2):
        o2 = o2[:, :LANES] * (1.0 / o2[:, LANES:])
        o = jnp.where(lo, o2[:GRID_W], o2[GRID_W:])
        o_ref[0, pl.ds(tok0 + g * GRID_W, GRID_W), :] = o.astype(o_ref.dtype)

    for g, q2 in enumerate(queries(0)):
        s_c = jnp.dot(q2, kc_t, preferred_element_type=F32)
        p_c = jnp.exp(s_c - jnp.max(s_c, axis=-1, keepdims=True))
        finish(0, g, jnp.dot(p_c.astype(BF16), vc, preferred_element_type=F32))

    def row_group(rg, slot):
        tok0 = pl.multiple_of(t_ctx + rg * NA_STEP, NA_STEP)
        t0s = []
        s_slot = s_ref.at[slot]
        for g, q2 in enumerate(queries(tok0)):
            r = rg * rows_per_step + g
            row_start = jnp.clip(r - NA_ROWS // 2, 0, rows - NA_ROWS)
            dr0 = row_start - r + NA_ROWS - 1
            t0 = pl.multiple_of(t_ctx + row_start * GRID_W, GRID_W)
            odd = (t0 // GRID_W) % (LANES // GRID_W)
            kw_t = kt_ref[odd, :, pl.ds(pl.multiple_of(t0 - odd * GRID_W, LANES), NA_KEYS)]
            s_slot[g, :, :NA_KEYS] = jnp.dot(q2, kw_t, preferred_element_type=F32) + bias_ref[dr0, 0]
            s_slot[g, :, NA_KEYS:] = jnp.dot(q2, kc_t, preferred_element_type=F32)
            t0s.append(t0)
        for g in range(rows_per_step):
            s = s_slot[g]
            p = jnp.exp(s - jnp.max(s, axis=-1, keepdims=True)).astype(BF16)
            finish(tok0, g, jnp.dot(p[:, :NA_KEYS], vb_ref[pl.ds(t0s[g], NA_KEYS), :], preferred_element_type=F32)
                   + jnp.dot(p[:, NA_KEYS:], vc, preferred_element_type=F32))

    def row_groups(i, carry):
        for slot in range(NA_GROUPS_PER_ITER):
            row_group(i * NA_GROUPS_PER_ITER + slot, slot)
        return carry
    lax.fori_loop(0, rows // rows_per_step // NA_GROUPS_PER_ITER, row_groups, 0)


def _na_attention(p_main, q_w, k_w, bias_tab, t_ctx, col_q, n_heads):
    b_sz, s_len, _ = p_main.shape
    n_hg = n_heads // 2
    rows = (s_len - t_ctx) // GRID_W
    rows_per_step = NA_STEP // GRID_W
    assert t_ctx == NA_STEP and s_len % NA_STEP == 0 and rows >= NA_ROWS
    assert rows % (rows_per_step * NA_GROUPS_PER_ITER) == 0
    cq = col_q // LANES
    seq = lambda col: pl.BlockSpec((1, s_len, LANES), lambda b, h: (b, 0, col + h))
    w2 = lambda w: jnp.concatenate([w, w]).reshape(1, LANES)
    return pl.pallas_call(
        functools.partial(_na_seq_kernel, t_ctx=t_ctx, rows=rows),
        grid=(b_sz, n_hg),
        in_specs=[seq(cq), seq(cq + n_hg), _seq_spec(s_len, cq + 2 * n_hg),
                  pl.BlockSpec((1, LANES), lambda b, h: (0, 0)),
                  pl.BlockSpec((1, LANES), lambda b, h: (0, 0)),
                  pl.BlockSpec((NA_ROWS, 1, 2 * GRID_W, NA_KEYS), lambda b, h: (0, h, 0, 0))],
        out_specs=pl.BlockSpec((1, s_len, LANES), lambda b, h: (b, 0, h)),
        out_shape=jax.ShapeDtypeStruct((b_sz, s_len, n_heads * HEAD_DIM), BF16),
        scratch_shapes=[pltpu.VMEM((2, LANES, s_len), BF16), pltpu.VMEM((s_len, 2 * LANES), BF16),
                        pltpu.VMEM((NA_GROUPS_PER_ITER, rows_per_step, 2 * GRID_W, NA_KEYS + t_ctx), F32),
                        pltpu.VMEM((s_len, LANES), BF16)],
        compiler_params=_cparams(("arbitrary", "arbitrary")),
        name="na_attention",
    )(p_main, p_main, p_main, w2(q_w), w2(k_w), bias_tab)


RET_CHUNK_LEN = 128
RET_UNROLL = 6


def _rope_tables(t_ctx, t_len):
    pos = jnp.arange(t_len)
    row = (pos // GRID_W).astype(F32)
    col = (pos % GRID_W).astype(F32)
    n = HEAD_DIM // 4
    inv = ROPE_BASE ** (-jnp.arange(n, dtype=F32) / n)
    ar = row[:, None] * inv
    ac = col[:, None] * inv
    cos = jnp.concatenate([jnp.cos(ar), jnp.cos(ar), jnp.cos(ac), jnp.cos(ac)], axis=-1)
    sin = jnp.concatenate([-jnp.sin(ar), jnp.sin(ar), -jnp.sin(ac), jnp.sin(ac)], axis=-1)
    cos = jnp.concatenate([jnp.ones((t_ctx, HEAD_DIM), F32), cos], axis=0)
    sin = jnp.concatenate([jnp.zeros((t_ctx, HEAD_DIM), F32), sin], axis=0)
    return jnp.tile(cos, (1, 2)), jnp.tile(sin, (1, 2))


def _ret_tables(n_heads):
    L = RET_CHUNK_LEN
    pos = np.arange(L, dtype=np.float32)
    lane_head = np.arange(LANES) // HEAD_DIM
    decay = np.zeros((2, n_heads, L, L), np.float32)
    zeta = np.zeros((2, n_heads // 2, LANES, L), np.float32)
    xi = np.zeros((2, n_heads // 2, L, LANES), np.float32)
    gch = np.zeros((2, n_heads // 2, 1, LANES), np.float32)
    for d, first_exp in enumerate((5.0, 6.0)):
        e = np.float32(first_exp) + np.float32(2.0) * np.arange(n_heads, dtype=np.float32)
        lg = np.log1p(-np.exp2(-e)).astype(np.float32)
        diff = pos[:, None] - pos[None, :]
        if d == 1:
            diff = -diff
        for h in range(n_heads):
            decay[d, h] = np.where(diff >= 0, np.exp(lg[h] * np.maximum(diff, 0.0)), 0.0)
        for hp in range(n_heads // 2):
            lgl = lg[2 * hp + lane_head][None, :]
            to_end = (L - 1 - pos if d == 0 else pos)[:, None]
            zeta[d, hp] = np.exp(lgl * to_end).T
            xi[d, hp] = np.exp(lgl * (L - to_end))
            gch[d, hp] = np.exp(lgl * L)
    return tuple(jnp.asarray(a) for a in (decay, zeta, xi, gch))


def _ret_kernel(q_ref, k_ref, v_ref, g_ref, cos_ref, sin_ref, dec_ref, zeta_ref, xi_ref, gch_ref, rn_ref,
                o_ref, qr_ref, kt_ref, vb_ref, acc_ref, *, t_ctx):
    L = RET_CHUNK_LEN
    s_len = q_ref.shape[1]
    nc = s_len // L
    lane = lax.broadcasted_iota(jnp.int32, (1, LANES), 1)
    lo = lane < HEAD_DIM
    half = (lane & (HEAD_DIM // 4)) == 0
    rid = lax.broadcasted_iota(jnp.int32, (LANES, LANES), 0) < HEAD_DIM
    cid = lax.broadcasted_iota(jnp.int32, (LANES, LANES), 1) < HEAD_DIM
    same_head = rid == cid

    def rope(x, cos, sin):
        up = pltpu.roll(x, LANES - HEAD_DIM // 4, axis=1)
        dn = pltpu.roll(x, HEAD_DIM // 4, axis=1)
        return x * cos + jnp.where(half, up, dn) * sin

    def prep(c, carry):
        sl = pl.ds(pl.multiple_of(c * L, L), L)
        cos, sin = cos_ref[sl, :], sin_ref[sl, :]
        qr_ref[sl, :] = rope(q_ref[0, sl, :], cos, sin).astype(BF16)
        kt_ref[:, sl] = (rope(k_ref[0, sl, :], cos, sin) * HEAD_DIM ** -0.5).T.astype(BF16)
        vb_ref[sl, :] = v_ref[0, sl, :].astype(BF16)
        return carry
    lax.fori_loop(0, nc, prep, 0, unroll=RET_UNROLL)

    def chunk(d, c, state):
        sl = pl.ds(pl.multiple_of(c * L, L), L)
        q, kt, v = qr_ref[sl, :], kt_ref[:, sl], vb_ref[sl, :]
        inter = jnp.dot(q, state.astype(BF16), preferred_element_type=F32) * xi_ref[d, 0]
        outs = []
        for h2 in range(2):
            qm = jnp.where(lo if h2 == 0 else jnp.logical_not(lo), q, jnp.zeros_like(q))
            sd = (jnp.dot(qm, kt, preferred_element_type=F32) * dec_ref[d, h2]).astype(BF16)
            outs.append(jnp.dot(sd, v, preferred_element_type=F32))
        y = jnp.where(lo, outs[0], outs[1]) + inter
        kz_t = (kt.astype(F32) * zeta_ref[d, 0]).astype(BF16)
        state = state * gch_ref[d, 0] + jnp.where(same_head, jnp.dot(kz_t, v, preferred_element_type=F32), 0.0)
        return sl, y, state

    def fwd(c, state):
        sl, y, state = chunk(0, c, state)
        acc_ref[sl, :] = y
        return state
    lax.fori_loop(0, nc, fwd, jnp.zeros((LANES, LANES), F32), unroll=RET_UNROLL)

    def bwd(i, state):
        c = _reverse_chunk(i, nc, t_ctx // L)
        sl, y, state = chunk(1, c, state)
        y = _head_rms(y + acc_ref[sl, :], rn_ref[...], lo)
        g = g_ref[0, sl, :]
        o_ref[0, sl, :] = (y * (g * jax.nn.sigmoid(g))).astype(o_ref.dtype)
        return state
    lax.fori_loop(0, nc, bwd, jnp.zeros((LANES, LANES), F32), unroll=RET_UNROLL)


def _retention(p_main, rope_tabs, ret_tabs, r_w, t_ctx, col_q, n_heads):
    b_sz, s_len, _ = p_main.shape
    n_hp = n_heads // 2
    L = RET_CHUNK_LEN
    assert t_ctx % L == 0 and s_len % L == 0
    c0 = col_q // LANES
    cos, sin = rope_tabs
    decay, zeta, xi, gch = ret_tabs
    const2 = pl.BlockSpec((s_len, LANES), lambda b, h: (0, 0), pipeline_mode=pl.Buffered(1))
    return pl.pallas_call(
        functools.partial(_ret_kernel, t_ctx=t_ctx),
        grid=(b_sz, n_hp),
        in_specs=[_seq_spec(s_len, c0, 2), _seq_spec(s_len, c0 + n_hp, 2), _seq_spec(s_len, c0 + 2 * n_hp, 2),
                  _seq_spec(s_len, c0 + 3 * n_hp), const2, const2,
                  pl.BlockSpec((2, 2, L, L), lambda b, h: (0, h, 0, 0)),
                  pl.BlockSpec((2, 1, LANES, L), lambda b, h: (0, h, 0, 0)),
                  pl.BlockSpec((2, 1, L, LANES), lambda b, h: (0, h, 0, 0)),
                  pl.BlockSpec((2, 1, 1, LANES), lambda b, h: (0, h, 0, 0)),
                  pl.BlockSpec((1, LANES), lambda b, h: (0, 0))],
        out_specs=pl.BlockSpec((1, s_len, LANES), lambda b, h: (b, 0, h)),
        out_shape=jax.ShapeDtypeStruct((b_sz, s_len, n_heads * HEAD_DIM), BF16),
        scratch_shapes=[pltpu.VMEM((s_len, LANES), BF16), pltpu.VMEM((LANES, s_len), BF16),
                        pltpu.VMEM((s_len, LANES), BF16), pltpu.VMEM((s_len, LANES), F32)],
        compiler_params=_mixer_cparams(),
        name="retention",
    )(p_main, p_main, p_main, p_main, cos, sin, decay, zeta, xi, gch,
      jnp.concatenate([r_w, r_w]).reshape(1, LANES))


MLSTM_CHUNK_LEN = LANES
MLSTM_PREP_UNROLL = 6
MLSTM_UNROLL = 6
N_GATE_TYPES = 4
N_FWD_GATES = N_GATE_TYPES


def _log_sigmoid(x):
    return jnp.minimum(x, 0.0) - jnp.log1p(jnp.exp(-jnp.abs(x)))


def _split3(x):
    hi = x.astype(BF16)
    r = x - hi.astype(F32)
    mid = r.astype(BF16)
    return hi, mid, (r - mid.astype(F32)).astype(BF16)


def _mlstm_pair_kernel(q_ref, k_ref, v_ref, og_ref, gc_ref, gr_ref, wq_ref, wk_ref, bq_ref, bk_ref, gbc_ref, gbr_ref,
                       mn_ref, o_ref, qc_ref, kt_ref, vb_ref, acc_ref, st_ref, cc_ref, cr_ref, *, t_ctx):
    L = MLSTM_CHUNK_LEN
    s_len = q_ref.shape[1]
    nc = s_len // L
    lane = lax.broadcasted_iota(jnp.int32, (1, LANES), 1)
    lo = lane < HEAD_DIM
    head_lanes = (lo, jnp.logical_not(lo))
    sub_lo = lax.broadcasted_iota(jnp.int32, (LANES, 1), 0) < HEAD_DIM
    head_rows = (sub_lo, jnp.logical_not(sub_lo))
    row_i = lax.broadcasted_iota(jnp.int32, (L, L), 0)
    col_i = lax.broadcasted_iota(jnp.int32, (L, L), 1)
    causal = (row_i >= col_i, row_i <= col_i)
    tri_b = causal[0].astype(BF16)
    tri_bt = causal[1].astype(BF16)
    sub = lax.broadcasted_iota(jnp.int32, (L, 1), 0)
    gate_row = lax.broadcasted_iota(jnp.int32, (2 * N_GATE_TYPES, 1), 0)
    ones = jnp.ones((L, LANES), BF16)

    def conv(x_ref, w_ref, b_ref, t0):
        x = x_ref[0, pl.ds(t0, L), :]
        prev = x_ref[0, pl.ds(jnp.maximum(t0 - SUBLANES, 0), SUBLANES), :][SUBLANES - 1:SUBLANES]
        nxt = x_ref[0, pl.ds(jnp.minimum(t0 + L, s_len - SUBLANES), SUBLANES), :][0:1]
        prev = jnp.where((t0 != 0) & (t0 != t_ctx), prev, 0.0)
        nxt = jnp.where((t0 + L != t_ctx) & (t0 + L != s_len), nxt, 0.0)
        xm = jnp.where(sub == 0, prev, pltpu.roll(x, 1, axis=0))
        xp = jnp.where(sub == L - 1, nxt, pltpu.roll(x, L - 1, axis=0))
        y = b_ref[...] + xm * w_ref[0:1, :] + x * w_ref[1:2, :] + xp * w_ref[2:3, :]
        return y * jax.nn.sigmoid(y)

    def prep(c, carry):
        t0 = pl.multiple_of(c * L, L)
        sl = pl.ds(t0, L)
        qc_ref[sl, :] = conv(q_ref, wq_ref, bq_ref, t0).astype(BF16)
        kt_ref[:, sl] = (conv(k_ref, wk_ref, bk_ref, t0) * HEAD_DIM ** -0.5).T.astype(BF16)
        vb_ref[sl, :] = v_ref[0, sl, :].astype(BF16)
        lf_col = _log_sigmoid(gc_ref[0, sl, :] + gbc_ref[0])
        lf_row = _log_sigmoid(gr_ref[0, 0, :, sl] + gbr_ref[0])
        pre_col = sum(jnp.dot(tri_b, p, preferred_element_type=F32) for p in _split3(lf_col))
        pre_row = sum(jnp.dot(p, tri_bt, preferred_element_type=F32) for p in _split3(lf_row))
        cc_ref[sl, :] = jnp.where(lane < N_FWD_GATES, pre_col, pre_col[L - 1:L, :] - pre_col + lf_col)
        cr_ref[:, sl] = jnp.where(gate_row < N_FWD_GATES, pre_row, pre_row[:, L - 1:L] - pre_row + lf_row)
        return carry
    lax.fori_loop(0, nc, prep, 0, unroll=MLSTM_PREP_UNROLL)

    def chunk(d, c, m_state):
        sl = pl.ds(pl.multiple_of(c * L, L), L)
        q, kt, v = qc_ref[sl, :], kt_ref[:, sl], vb_ref[sl, :]
        g_row = gr_ref[0, 0, :, sl] + gbr_ref[0]
        cum_col, cum_row = cc_ref[sl, :], cr_ref[:, sl]
        end = L - 1 if d == 0 else 0
        outs, new_m = [], []
        for h2 in range(2):
            m_st = m_state[h2]
            ci, cf = 2 * (2 * d) + h2, 2 * (2 * d + 1) + h2
            a_rep = jnp.broadcast_to(cum_col[:, cf:cf + 1], (L, LANES))
            i_row, a_row = g_row[ci:ci + 1, :], cum_row[cf:cf + 1, :]
            b_tot = a_row[:, end:end + 1]
            d_log = jnp.where(causal[d], a_rep + (i_row - a_row), NEG_INF)
            m_intra = jnp.broadcast_to(jnp.max(d_log, axis=-1, keepdims=True), (L, LANES))
            qm = jnp.where(head_lanes[h2], q, jnp.zeros_like(q))
            s = jnp.dot(qm, kt, preferred_element_type=F32) * jnp.exp(d_log - m_intra)
            v1 = jnp.where(head_lanes[h2], v, ones)
            intra = jnp.dot(s.astype(BF16), v1, preferred_element_type=F32)
            inter = jnp.dot(qm, st_ref[h2].astype(BF16), preferred_element_type=F32)
            inter_log = a_rep + m_st
            m_q = jnp.maximum(m_intra, inter_log)
            num_den = jnp.exp(inter_log - m_q) * inter + jnp.exp(m_intra - m_q) * intra
            den = pltpu.roll(num_den, HEAD_DIM, axis=1)
            outs.append(num_den / jnp.maximum(jnp.abs(den), jnp.exp(-m_q)))
            w_row = b_tot - a_row + i_row
            m_loc = jnp.max(w_row, axis=-1, keepdims=True)
            kts = jnp.where(head_rows[h2], kt.astype(F32) * jnp.exp(w_row - m_loc), 0.0).astype(BF16)
            loc = jnp.dot(kts, v1, preferred_element_type=F32)
            m_new = jnp.maximum(b_tot + m_st, m_loc)
            st_ref[h2] = jnp.exp(b_tot + m_st - m_new) * st_ref[h2] + jnp.exp(m_loc - m_new) * loc
            new_m.append(m_new)
        return sl, jnp.where(lo, outs[0], outs[1]), tuple(new_m)

    zero_m = (jnp.zeros((1, 1), F32), jnp.zeros((1, 1), F32))

    st_ref[...] = jnp.zeros_like(st_ref)

    def fwd(c, m_state):
        sl, y, m_state = chunk(0, c, m_state)
        acc_ref[sl, :] = y
        return m_state
    lax.fori_loop(0, nc, fwd, zero_m, unroll=MLSTM_UNROLL)

    st_ref[...] = jnp.zeros_like(st_ref)

    def bwd(i, m_state):
        c = _reverse_chunk(i, nc, t_ctx // L)
        sl, y, m_state = chunk(1, c, m_state)
        y = _head_rms(y + acc_ref[sl, :], mn_ref[...], lo)
        o_ref[0, sl, :] = (y * jax.nn.sigmoid(og_ref[0, sl, :])).astype(o_ref.dtype)
        return m_state
    lax.fori_loop(0, nc, bwd, zero_m, unroll=MLSTM_UNROLL)


def _mlstm(p_main, p_gate, conv_w, conv_b, gate_b, m_w, t_ctx, n_heads):
    b_sz, s_len, _ = p_main.shape
    n_hp = n_heads // 2
    L = MLSTM_CHUNK_LEN
    assert t_ctx % L == 0 and s_len % L == 0
    n_g = 2 * N_GATE_TYPES
    g_rows = jnp.transpose(p_gate.reshape(b_sz, s_len, n_hp, LANES)[..., :n_g], (0, 2, 3, 1))
    gb = jnp.transpose(gate_b.reshape(N_GATE_TYPES, n_hp, 2), (1, 0, 2)).reshape(n_hp, n_g)
    gb_col = jnp.zeros((n_hp, 1, LANES), F32).at[:, 0, :n_g].set(gb)
    gb_row = gb.reshape(n_hp, n_g, 1)
    vec = lambda col: pl.BlockSpec((1, LANES), lambda b, h: (0, col + h))
    return pl.pallas_call(
        functools.partial(_mlstm_pair_kernel, t_ctx=t_ctx),
        grid=(b_sz, n_hp),
        in_specs=[_seq_spec(s_len, 0, 2), _seq_spec(s_len, n_hp, 2), _seq_spec(s_len, 2 * n_hp, 2),
                  _seq_spec(s_len, 3 * n_hp), _seq_spec(s_len, 0),
                  pl.BlockSpec((1, 1, n_g, s_len), lambda b, h: (b, h, 0, 0)),
                  pl.BlockSpec((3, LANES), lambda b, h: (0, h)),
                  pl.BlockSpec((3, LANES), lambda b, h: (0, n_hp + h)),
                  vec(0), vec(n_hp),
                  pl.BlockSpec((1, 1, LANES), lambda b, h: (h, 0, 0)),
                  pl.BlockSpec((1, n_g, 1), lambda b, h: (h, 0, 0)),
                  pl.BlockSpec((1, LANES), lambda b, h: (0, 0))],
        out_specs=pl.BlockSpec((1, s_len, LANES), lambda b, h: (b, 0, h)),
        out_shape=jax.ShapeDtypeStruct((b_sz, s_len, n_heads * HEAD_DIM), BF16),
        scratch_shapes=[pltpu.VMEM((s_len, LANES), BF16), pltpu.VMEM((LANES, s_len), BF16),
                        pltpu.VMEM((s_len, LANES), BF16), pltpu.VMEM((s_len, LANES), F32),
                        pltpu.VMEM((2, LANES, LANES), F32),
                        pltpu.VMEM((s_len, LANES), F32), pltpu.VMEM((n_g, s_len), F32)],
        compiler_params=_mixer_cparams(),
        name="mlstm",
    )(p_main, p_main, p_main, p_main, p_gate, g_rows, conv_w, conv_w, conv_b.reshape(1, -1),
      conv_b.reshape(1, -1), gb_col, gb_row, jnp.concatenate([m_w, m_w]).reshape(1, LANES))


def kernel(x, c, ctx, c_ctx, w_mod, b_mod, norm_mix, norm_ffn, w_in, w_out, mlstm_conv_w, mlstm_conv_b,
           mlstm_gate_b, mlstm_norm, na_q_norm, na_k_norm, na_rpb, ret_norm, router_w, router_b,
           expert_w_up, expert_b_up, expert_w_down, expert_b_down):
    b_sz, t_len, d = x.shape
    t_ctx = ctx.shape[1]
    depth = w_in.shape[0]
    n_e = router_w.shape[2]
    d_mix = w_out.shape[1]
    h_m = d_mix // (4 * HEAD_DIM)
    h_na = d_mix // (2 * HEAD_DIM)
    h_r = d_mix // (4 * HEAD_DIM)
    d_m, d_na = h_m * HEAD_DIM, h_na * HEAD_DIM
    n_gate = N_GATE_TYPES * h_m
    assert b_sz + 1 <= MOD_ROWS and n_e <= LANES and n_gate <= LANES
    s_len = t_ctx + t_len
    assert s_len % ROW_TILE == 0 and s_len % COMBINE_ROWS == 0

    s = jnp.concatenate([ctx, x], axis=1)
    cc = jnp.zeros((MOD_ROWS, d), F32).at[:b_sz].set(c).at[b_sz].set(c_ctx)
    mods = _modulation(cc, w_mod, b_mod).reshape(depth, MOD_ROWS, 1, 6 * d)

    g0 = 4 * d_m
    col_na = g0
    col_ret = col_na + 3 * d_na
    n_hp = h_m // 2
    gate_src = np.array([[g0 + t * h_m + 2 * hp + h2 for t in range(N_GATE_TYPES) for h2 in range(2)]
                         for hp in range(n_hp)])
    rope_tabs = _rope_tables(t_ctx, t_len)
    ret_tabs = _ret_tables(h_r)
    na_tabs = _na_bias_tables(na_rpb)
    f2 = expert_w_up.shape[3]
    w_up_all = expert_w_up.reshape(depth * n_e, d, f2)
    w_down_all = expert_w_down.reshape(depth * n_e, f2 // 2, d)
    b_up_all = _regroup_bias(expert_b_up).reshape(depth * n_e, 1, f2)
    b_down_all = expert_b_down.reshape(depth * n_e, 1, d)

    experts_out = None
    for l in range(depth):
        w_main = jnp.concatenate([w_in[l, :, :g0], w_in[l, :, g0 + n_gate:]], axis=1).astype(BF16)
        w_gate = jnp.zeros((d, n_hp, LANES), BF16).at[:, :, :gate_src.shape[1]].set(
            w_in[l][:, gate_src].astype(BF16)).reshape(d, n_hp * LANES)
        if experts_out is None:
            p_main, p_gate = _proj_in(s, mods[l], norm_mix[l], w_main, w_gate, t_ctx)
        else:
            s, p_main, p_gate = _proj_in(s, mods[l], norm_mix[l], w_main, w_gate, t_ctx,
                                         combine=(*experts_out, mods[l - 1]))

        mix = [_mlstm(p_main, p_gate, mlstm_conv_w[l], mlstm_conv_b[l], mlstm_gate_b[l], mlstm_norm[l],
                      t_ctx, h_m),
               _na_attention(p_main, na_q_norm[l], na_k_norm[l], na_tabs[l], t_ctx, col_na, h_na),
               _retention(p_main, rope_tabs, ret_tabs, ret_norm[l], t_ctx, col_ret, h_r)]

        rw = jnp.zeros((d, LANES), F32).at[:, :n_e].set(router_w[l])
        rb = jnp.zeros((1, LANES), F32).at[0, :n_e].set(router_b[l])
        s, tok, logits = _proj_out(mix, s, mods[l], norm_ffn[l], w_out[l].astype(BF16), rw, rb, t_ctx)

        experts_out = _moe(tok, logits[..., :n_e], n_e, l, w_up_all, w_down_all, b_up_all, b_down_all)
    return _latent_combine(*experts_out, s, mods[depth - 1], t_ctx)
```
